```python
import jax, jax.numpy as jnp
from jax import lax
import numpy as np

D_MODEL = 1024
BATCH = 8
SEQ = 2048
DEPTH = 4

HEAD_DIM = 64
N_HEADS = D_MODEL // HEAD_DIM
N_SB_HEADS = N_HEADS // 2
N_DIL_HEADS = N_HEADS - N_SB_HEADS
N_FOX_HEADS = N_HEADS
D_ATTN = N_HEADS * HEAD_DIM
D_FF = -(-8 * D_MODEL // (3 * 256)) * 256
ROPE_THETA = 500000.0
ROT_DIM = HEAD_DIM // 4
Q_BLOCK = 128
DIL_PATTERNS = ((128, 1), (512, 4), (2048, 16))
RMS_EPS = 1e-5
N_EVEN = (DEPTH + 1) // 2
N_ODD = DEPTH // 2

kernel_name = "hybrid_stickbreak_dilated_fox_trunk"


def rms_norm(x, g):
    xf = x.astype(jnp.float32)
    y = xf * lax.rsqrt(jnp.mean(xf * xf, axis=-1, keepdims=True) + RMS_EPS)
    return (y * g.astype(jnp.float32)).astype(x.dtype)


def partial_rotary(x, pos):
    half = ROT_DIM // 2
    inv_freq = ROPE_THETA ** (-jnp.arange(half, dtype=jnp.float32) * 2.0 / ROT_DIM)
    ang = pos[:, None] * inv_freq[None, :]
    cos = jnp.cos(ang)[:, None, :].astype(x.dtype)
    sin = jnp.sin(ang)[:, None, :].astype(x.dtype)
    x1, x2, x_pass = x[..., :half], x[..., half:ROT_DIM], x[..., ROT_DIM:]
    return jnp.concatenate([x1 * cos - x2 * sin, x2 * cos + x1 * sin, x_pass], axis=-1)


def stick_breaking_attention(q, k, v):
    B, H, S, Dh = q.shape
    nb = S // Q_BLOCK
    qb = q.reshape(B, H, nb, Q_BLOCK, Dh).transpose(2, 0, 1, 3, 4)
    kpos = jnp.arange(S)

    def block(args):
        qblk, n = args
        z = jnp.einsum('bhqd,bhkd->bhqk', qblk, k).astype(jnp.float32)
        qpos = n * Q_BLOCK + jnp.arange(Q_BLOCK)
        strict = kpos[None, :] < qpos[:, None]
        log_1m_beta = jnp.where(strict, jax.nn.log_sigmoid(-z), 0.0)
        between = lax.cumsum(log_1m_beta, axis=3, reverse=True) - log_1m_beta
        a = jnp.where(strict, jnp.exp(jax.nn.log_sigmoid(z) + between), 0.0)
        return jnp.einsum('bhqk,bhkd->bhqd', a.astype(v.dtype), v)

    out = lax.map(block, (qb, jnp.arange(nb)))
    return out.transpose(1, 2, 0, 3, 4).reshape(B, H, S, Dh)


def dilated_window_attention(q, k, v, window, dilation):
    B, S, H, Dh = q.shape
    span = window // dilation
    L = S // dilation
    nb = -(-L // span)
    Lp = nb * span

    def to_blocks(t):
        t = t.reshape(B, L, dilation, H, Dh)
        t = jnp.pad(t, ((0, 0), (0, Lp - L), (0, 0), (0, 0), (0, 0)))
        return t.reshape(B, nb, span, dilation, H, Dh)

    def with_prev(t):
        prev = jnp.pad(t[:, :-1], ((0, 0), (1, 0), (0, 0), (0, 0), (0, 0), (0, 0)))
        return jnp.concatenate([prev, t], axis=2)

    qb = to_blocks(q)
    kw = with_prev(to_blocks(k))
    vw = with_prev(to_blocks(v))
    s = jnp.einsum('bnqrhd,bnkrhd->bnrhqk', qb, kw).astype(jnp.float32)
    a = jnp.arange(span)[None, :, None]
    kk = jnp.arange(2 * span)[None, None, :]
    blk = jnp.arange(nb)[:, None, None]
    valid = (kk >= a) & (kk <= a + span) & (blk * span - span + kk >= 0)
    s = jnp.where(valid[None, :, None, None], s, -jnp.inf)
    m = jnp.max(s, axis=-1, keepdims=True)
    p = jnp.exp(s - m)
    l = jnp.sum(p, axis=-1, keepdims=True)
    o = jnp.einsum('bnrhqk,bnkrhd->bnqrhd', (p / l).astype(v.dtype), vw)
    lse = (m + jnp.log(l))[..., 0]
    o = o.reshape(B, Lp, dilation, H, Dh)[:, :L].reshape(B, S, H, Dh)
    lse = lse.transpose(0, 1, 4, 2, 3).reshape(B, Lp, dilation, H)[:, :L].reshape(B, S, H)
    return o, lse


def forgetting_attention(q, k, v, log_f):
    B, H, S, Dh = q.shape
    F = lax.cumsum(log_f, axis=2)
    nb = S // Q_BLOCK
    qb = q.reshape(B, H, nb, Q_BLOCK, Dh).transpose(2, 0, 1, 3, 4)
    Fb = F.reshape(B, H, nb, Q_BLOCK).transpose(2, 0, 1, 3)
    kpos = jnp.arange(S)

    def block(args):
        qblk, Fq, n = args
        z = jnp.einsum('bhqd,bhkd->bhqk', qblk, k).astype(jnp.float32)
        z = z + Fq[..., None] - F[:, :, None, :]
        qpos = n * Q_BLOCK + jnp.arange(Q_BLOCK)
        z = jnp.where(kpos[None, :] <= qpos[:, None], z, -jnp.inf)
        p = jax.nn.softmax(z, axis=-1)
        return jnp.einsum('bhqk,bhkd->bhqd', p.astype(v.dtype), v)

    out = lax.map(block, (qb, Fb, jnp.arange(nb)))
    return out.transpose(1, 2, 0, 3, 4).reshape(B, H, S, Dh)


def even_mixer(h, w_qkv, w_o):
    B, S, _ = h.shape
    scale = HEAD_DIM ** -0.5
    qkv = (h @ w_qkv).reshape(B, S, 3, N_HEADS, HEAD_DIM)
    q, k, v = qkv[:, :, 0], qkv[:, :, 1], qkv[:, :, 2]
    qa = (q[:, :, :N_SB_HEADS] * scale).transpose(0, 2, 1, 3)
    ka = k[:, :, :N_SB_HEADS].transpose(0, 2, 1, 3)
    va = v[:, :, :N_SB_HEADS].transpose(0, 2, 1, 3)
    out_a = stick_breaking_attention(qa, ka, va).transpose(0, 2, 1, 3)
    pos = jnp.arange(S, dtype=jnp.float32)
    qd = partial_rotary(q[:, :, N_SB_HEADS:], pos) * scale
    kd = partial_rotary(k[:, :, N_SB_HEADS:], pos)
    vd = v[:, :, N_SB_HEADS:]
    outs, lses = [], []
    for window, dilation in DIL_PATTERNS:
        o_p, lse_p = dilated_window_attention(qd, kd, vd, window, dilation)
        outs.append(o_p)
        lses.append(lse_p)
    mix = jax.nn.softmax(jnp.stack(lses), axis=0)
    out_b = jnp.einsum('pbsh,pbshd->bshd', mix.astype(vd.dtype), jnp.stack(outs))
    o = jnp.concatenate([out_a, out_b], axis=2).reshape(B, S, D_ATTN)
    return o @ w_o


def odd_mixer(h, w_qkvf, b_forget, w_o):
    B, S, _ = h.shape
    scale = HEAD_DIM ** -0.5
    proj = h @ w_qkvf
    qkv = proj[..., :3 * D_ATTN].reshape(B, S, 3, N_FOX_HEADS, HEAD_DIM)
    f_logit = proj[..., 3 * D_ATTN:] + b_forget
    log_f = jax.nn.log_sigmoid(f_logit.astype(jnp.float32)).transpose(0, 2, 1)
    q = (qkv[:, :, 0] * scale).transpose(0, 2, 1, 3)
    k = qkv[:, :, 1].transpose(0, 2, 1, 3)
    v = qkv[:, :, 2].transpose(0, 2, 1, 3)
    o = forgetting_attention(q, k, v, log_f).transpose(0, 2, 1, 3).reshape(B, S, D_ATTN)
    return o @ w_o


def swiglu(h, w_in, w_out):
    g, u = jnp.split(h @ w_in, 2, axis=-1)
    return (jax.nn.silu(g) * u) @ w_out


def _fwd_setup_inputs(seed: int = 0) -> dict:
    key = jax.random.key(seed)
    ks = jax.random.split(key, 12)
    f32 = jnp.float32
    res_scale = (2.0 * DEPTH) ** -0.5
    x = jax.random.normal(ks[0], (BATCH, SEQ, D_MODEL), f32)
    norm_mix = 1.0 + 0.02 * jax.random.normal(ks[1], (DEPTH, D_MODEL), f32)
    w_qkv_even = jax.random.normal(ks[2], (N_EVEN, D_MODEL, 3 * D_ATTN), f32) * D_MODEL ** -0.5
    w_o_even = jax.random.normal(ks[3], (N_EVEN, D_ATTN, D_MODEL), f32) * (D_ATTN ** -0.5 * res_scale)
    w_qkvf_odd = jax.random.normal(ks[4], (N_ODD, D_MODEL, 3 * D_ATTN + N_FOX_HEADS), f32) * D_MODEL ** -0.5
    b_forget = jax.random.uniform(ks[5], (N_ODD, N_FOX_HEADS), f32, minval=1.0, maxval=4.0)
    w_o_odd = jax.random.normal(ks[6], (N_ODD, D_ATTN, D_MODEL), f32) * (D_ATTN ** -0.5 * res_scale)
    norm_ffn = 1.0 + 0.02 * jax.random.normal(ks[7], (DEPTH, D_MODEL), f32)
    w_ffn_in = jax.random.normal(ks[8], (DEPTH, D_MODEL, 2 * D_FF), f32) * D_MODEL ** -0.5
    w_ffn_out = jax.random.normal(ks[9], (DEPTH, D_FF, D_MODEL), f32) * (D_FF ** -0.5 * res_scale)
    norm_final = 1.0 + 0.02 * jax.random.normal(ks[10], (D_MODEL,), f32)
    return {"x": x, "norm_mix": norm_mix, "w_qkv_even": w_qkv_even, "w_o_even": w_o_even,
            "w_qkvf_odd": w_qkvf_odd, "b_forget": b_forget, "w_o_odd": w_o_odd,
            "norm_ffn": norm_ffn, "w_ffn_in": w_ffn_in, "w_ffn_out": w_ffn_out,
            "norm_final": norm_final}


def _fwd_reference(x, norm_mix, w_qkv_even, w_o_even, w_qkvf_odd, b_forget, w_o_odd,
              norm_ffn, w_ffn_in, w_ffn_out, norm_final):
    for layer in range(DEPTH):
        h = rms_norm(x, norm_mix[layer])
        if layer % 2 == 0:
            x = x + even_mixer(h, w_qkv_even[layer // 2], w_o_even[layer // 2])
        else:
            x = x + odd_mixer(h, w_qkvf_odd[layer // 2], b_forget[layer // 2], w_o_odd[layer // 2])
        h = rms_norm(x, norm_ffn[layer])
        x = x + swiglu(h, w_ffn_in[layer], w_ffn_out[layer])
    return rms_norm(x, norm_final)


import jax as _jax
import jax.numpy as _jnp

TWIN_FORMAT = 'train_step'
FWD_PARAMS = ['x', 'norm_mix', 'w_qkv_even', 'w_o_even', 'w_qkvf_odd', 'b_forget', 'w_o_odd', 'norm_ffn', 'w_ffn_in', 'w_ffn_out', 'norm_final']
TWIN_WEIGHTS = ['norm_mix', 'w_qkv_even', 'w_o_even', 'w_qkvf_odd', 'b_forget', 'w_o_odd', 'norm_ffn', 'w_ffn_in', 'w_ffn_out', 'norm_final']
TWIN_DIFF_INPUT = 'x'
TWIN_INPUTS = ['x', 'norm_mix', 'w_qkv_even', 'w_o_even', 'w_qkvf_odd', 'b_forget', 'w_o_odd', 'norm_ffn', 'w_ffn_in', 'w_ffn_out', 'norm_final', 'loss_target', 'm_norm_mix', 'm_w_qkv_even', 'm_w_o_even', 'm_w_qkvf_odd', 'm_b_forget', 'm_w_o_odd', 'm_norm_ffn', 'm_w_ffn_in', 'm_w_ffn_out', 'm_norm_final', 'v_norm_mix', 'v_w_qkv_even', 'v_w_o_even', 'v_w_qkvf_odd', 'v_b_forget', 'v_w_o_odd', 'v_norm_ffn', 'v_w_ffn_in', 'v_w_ffn_out', 'v_norm_final']
TWIN_OUTPUTS = ['loss', 'grad_x', 'grad_norm_mix', 'grad_w_qkv_even', 'grad_w_o_even', 'grad_w_qkvf_odd', 'grad_b_forget', 'grad_w_o_odd', 'grad_norm_ffn', 'grad_w_ffn_in', 'grad_w_ffn_out', 'grad_norm_final', 'delta_norm_mix', 'delta_w_qkv_even', 'delta_w_o_even', 'delta_w_qkvf_odd', 'delta_b_forget', 'delta_w_o_odd', 'delta_norm_ffn', 'delta_w_ffn_in', 'delta_w_ffn_out', 'delta_norm_final', 'new_m_norm_mix', 'new_m_w_qkv_even', 'new_m_w_o_even', 'new_m_w_qkvf_odd', 'new_m_b_forget', 'new_m_w_o_odd', 'new_m_norm_ffn', 'new_m_w_ffn_in', 'new_m_w_ffn_out', 'new_m_norm_final', 'new_v_norm_mix', 'new_v_w_qkv_even', 'new_v_w_o_even', 'new_v_w_qkvf_odd', 'new_v_b_forget', 'new_v_w_o_odd', 'new_v_norm_ffn', 'new_v_w_ffn_in', 'new_v_w_ffn_out', 'new_v_norm_final']
TWIN_LEAF_KINDS = {'loss': 'loss', 'grad_x': 'grad_x', 'grad_norm_mix': 'grad_w', 'grad_w_qkv_even': 'grad_w', 'grad_w_o_even': 'grad_w', 'grad_w_qkvf_odd': 'grad_w', 'grad_b_forget': 'grad_w', 'grad_w_o_odd': 'grad_w', 'grad_norm_ffn': 'grad_w', 'grad_w_ffn_in': 'grad_w', 'grad_w_ffn_out': 'grad_w', 'grad_norm_final': 'grad_w', 'delta_norm_mix': 'delta_w', 'delta_w_qkv_even': 'delta_w', 'delta_w_o_even': 'delta_w', 'delta_w_qkvf_odd': 'delta_w', 'delta_b_forget': 'delta_w', 'delta_w_o_odd': 'delta_w', 'delta_norm_ffn': 'delta_w', 'delta_w_ffn_in': 'delta_w', 'delta_w_ffn_out': 'delta_w', 'delta_norm_final': 'delta_w', 'new_m_norm_mix': 'new_m', 'new_m_w_qkv_even': 'new_m', 'new_m_w_o_even': 'new_m', 'new_m_w_qkvf_odd': 'new_m', 'new_m_b_forget': 'new_m', 'new_m_w_o_odd': 'new_m', 'new_m_norm_ffn': 'new_m', 'new_m_w_ffn_in': 'new_m', 'new_m_w_ffn_out': 'new_m', 'new_m_norm_final': 'new_m', 'new_v_norm_mix': 'new_v', 'new_v_w_qkv_even': 'new_v', 'new_v_w_o_even': 'new_v', 'new_v_w_qkvf_odd': 'new_v', 'new_v_b_forget': 'new_v', 'new_v_w_o_odd': 'new_v', 'new_v_norm_ffn': 'new_v', 'new_v_w_ffn_in': 'new_v', 'new_v_w_ffn_out': 'new_v', 'new_v_norm_final': 'new_v'}


def _forward(args):
    return _fwd_reference(*[args[k] for k in FWD_PARAMS])


def _output_shape():
    out = _jax.eval_shape(lambda: _forward(_fwd_setup_inputs(0)))
    return out.shape, out.dtype

N_MICROBATCH = 1
ADAM_LR = 0.001
ADAM_B1 = 0.9
ADAM_B2 = 0.999
ADAM_EPS = 1e-08
ADAM_WD = 0.01
ADAM_STEP = 10
PER_EXAMPLE_BATCH_AXIS = {'x': 0, 'loss_target': 0}
SHARED_INPUTS = []
_WEIGHT_DTYPES = {'norm_mix': _jnp.float32, 'w_qkv_even': _jnp.float32, 'w_o_even': _jnp.float32, 'w_qkvf_odd': _jnp.float32, 'b_forget': _jnp.float32, 'w_o_odd': _jnp.float32, 'norm_ffn': _jnp.float32, 'w_ffn_in': _jnp.float32, 'w_ffn_out': _jnp.float32, 'norm_final': _jnp.float32}
MOMENT_SCALE = {'norm_mix': 2.545299e-02, 'w_qkv_even': 1.594103e-02, 'w_o_even': 6.359945e-02, 'w_qkvf_odd': 1.344047e-02, 'b_forget': 7.546541e-02, 'w_o_odd': 4.511967e-02, 'norm_ffn': 3.690331e-02, 'w_ffn_in': 1.567617e-02, 'w_ffn_out': 7.234128e-02, 'norm_final': 1.600926e+01}


def _to_microbatches(a, axis):
    t = _jnp.moveaxis(a, axis, 0)
    t = t.reshape((N_MICROBATCH, t.shape[0] // N_MICROBATCH) + t.shape[1:])
    return _jnp.moveaxis(t, 1, axis + 1)


def setup_inputs(seed: int = 0) -> dict:
    inp = _fwd_setup_inputs(seed)
    key = _jax.random.fold_in(_jax.random.key(seed), 7919)
    shape, _ = _output_shape()
    out = dict(inp)
    out["loss_target"] = _jax.random.normal(_jax.random.fold_in(key, 0), shape, _jnp.float32)
    for i, name in enumerate(TWIN_WEIGHTS):
        w = inp[name].astype(_jnp.float32)
        if MOMENT_SCALE is None:
            s = _jnp.sqrt(_jnp.mean(_jnp.square(w)) + 1e-30)
        else:
            s = MOMENT_SCALE[name]
        km, kv = _jax.random.split(_jax.random.fold_in(key, i + 1))
        out[name] = w
        out["m_" + name] = s * _jax.random.normal(km, w.shape, _jnp.float32)
        out["v_" + name] = (s * s) * _jax.random.uniform(kv, w.shape, _jnp.float32, 0.5, 1.5)
    if N_MICROBATCH > 1:
        for name, axis in PER_EXAMPLE_BATCH_AXIS.items():
            out[name] = _to_microbatches(out[name], axis)
    return {'x': out['x'], 'norm_mix': out['norm_mix'], 'w_qkv_even': out['w_qkv_even'], 'w_o_even': out['w_o_even'], 'w_qkvf_odd': out['w_qkvf_odd'], 'b_forget': out['b_forget'], 'w_o_odd': out['w_o_odd'], 'norm_ffn': out['norm_ffn'], 'w_ffn_in': out['w_ffn_in'], 'w_ffn_out': out['w_ffn_out'], 'norm_final': out['norm_final'], 'loss_target': out['loss_target'], 'm_norm_mix': out['m_norm_mix'], 'm_w_qkv_even': out['m_w_qkv_even'], 'm_w_o_even': out['m_w_o_even'], 'm_w_qkvf_odd': out['m_w_qkvf_odd'], 'm_b_forget': out['m_b_forget'], 'm_w_o_odd': out['m_w_o_odd'], 'm_norm_ffn': out['m_norm_ffn'], 'm_w_ffn_in': out['m_w_ffn_in'], 'm_w_ffn_out': out['m_w_ffn_out'], 'm_norm_final': out['m_norm_final'], 'v_norm_mix': out['v_norm_mix'], 'v_w_qkv_even': out['v_w_qkv_even'], 'v_w_o_even': out['v_w_o_even'], 'v_w_qkvf_odd': out['v_w_qkvf_odd'], 'v_b_forget': out['v_b_forget'], 'v_w_o_odd': out['v_w_o_odd'], 'v_norm_ffn': out['v_norm_ffn'], 'v_w_ffn_in': out['v_w_ffn_in'], 'v_w_ffn_out': out['v_w_ffn_out'], 'v_norm_final': out['v_norm_final']}


def _loss(weights, diff, rest, loss_target):
    with _jax.named_scope("forward"):
        args = {**rest, TWIN_DIFF_INPUT: diff, **{k: w.astype(_WEIGHT_DTYPES[k]) for k, w in weights.items()}}
        y = _forward(args)
    with _jax.named_scope("loss_head"):
        err = _jnp.square(y.astype(_jnp.float32) - loss_target)
        return 0.5 * _jnp.sum(_jnp.mean(err, axis=-1)) if err.ndim else 0.5 * err


def _adamw(w, g, m, v):
    m = ADAM_B1 * m + (1.0 - ADAM_B1) * g
    v = ADAM_B2 * v + (1.0 - ADAM_B2) * _jnp.square(g)
    m_hat = m / (1.0 - ADAM_B1 ** ADAM_STEP)
    v_hat = v / (1.0 - ADAM_B2 ** ADAM_STEP)
    delta = -ADAM_LR * (m_hat / (_jnp.sqrt(v_hat) + ADAM_EPS) + ADAM_WD * w)
    return delta, m, v


def reference(x, norm_mix, w_qkv_even, w_o_even, w_qkvf_odd, b_forget, w_o_odd, norm_ffn, w_ffn_in, w_ffn_out, norm_final, loss_target, m_norm_mix, m_w_qkv_even, m_w_o_even, m_w_qkvf_odd, m_b_forget, m_w_o_odd, m_norm_ffn, m_w_ffn_in, m_w_ffn_out, m_norm_final, v_norm_mix, v_w_qkv_even, v_w_o_even, v_w_qkvf_odd, v_b_forget, v_w_o_odd, v_norm_ffn, v_w_ffn_in, v_w_ffn_out, v_norm_final):
    given = dict(x=x, norm_mix=norm_mix, w_qkv_even=w_qkv_even, w_o_even=w_o_even, w_qkvf_odd=w_qkvf_odd, b_forget=b_forget, w_o_odd=w_o_odd, norm_ffn=norm_ffn, w_ffn_in=w_ffn_in, w_ffn_out=w_ffn_out, norm_final=norm_final, loss_target=loss_target, m_norm_mix=m_norm_mix, m_w_qkv_even=m_w_qkv_even, m_w_o_even=m_w_o_even, m_w_qkvf_odd=m_w_qkvf_odd, m_b_forget=m_b_forget, m_w_o_odd=m_w_o_odd, m_norm_ffn=m_norm_ffn, m_w_ffn_in=m_w_ffn_in, m_w_ffn_out=m_w_ffn_out, m_norm_final=m_norm_final, v_norm_mix=v_norm_mix, v_w_qkv_even=v_w_qkv_even, v_w_o_even=v_w_o_even, v_w_qkvf_odd=v_w_qkvf_odd, v_b_forget=v_b_forget, v_w_o_odd=v_w_o_odd, v_norm_ffn=v_norm_ffn, v_w_ffn_in=v_w_ffn_in, v_w_ffn_out=v_w_ffn_out, v_norm_final=v_norm_final)
    weights = {n: given[n] for n in TWIN_WEIGHTS}
    shared = {n: given[n] for n in SHARED_INPUTS}
    per_example = {n: given[n] for n in ['x']}
    grad_fn = _jax.value_and_grad(_loss, argnums=(0, 1))

    def one_microbatch(ex, loss_target):
        ex = dict(ex)
        diff = ex.pop(TWIN_DIFF_INPUT)
        return grad_fn(weights, diff, {**shared, **ex}, loss_target)

    if N_MICROBATCH == 1:
        loss, (grad_w, grad_x) = one_microbatch(per_example, given["loss_target"])
    else:
        def body(carry, xs):
            loss_sum, grad_sum = carry
            l_k, (gw_k, gx_k) = one_microbatch(xs[0], xs[1])
            with _jax.named_scope("update"):
                return (loss_sum + l_k, _jax.tree.map(_jnp.add, grad_sum, gw_k)), gx_k

        init = (_jnp.zeros((), _jnp.float32), _jax.tree.map(_jnp.zeros_like, weights))
        (loss, grad_w), grad_x = _jax.lax.scan(body, init, (per_example, given["loss_target"]))
    with _jax.named_scope("update"):
        delta_w, new_m, new_v = {}, {}, {}
        for n in TWIN_WEIGHTS:
            delta_w[n], new_m[n], new_v[n] = _adamw(weights[n], grad_w[n], given["m_" + n], given["v_" + n])
    return (loss, grad_x, *[grad_w[n] for n in TWIN_WEIGHTS], *[delta_w[n] for n in TWIN_WEIGHTS],
            *[new_m[n] for n in TWIN_WEIGHTS], *[new_v[n] for n in TWIN_WEIGHTS])
```

```python
import functools

import jax
import jax.numpy as jnp
from jax import lax
from jax.experimental import pallas as pl
from jax.experimental.pallas import tpu as pltpu

F32 = jnp.float32
BF16 = jnp.bfloat16
MESH = pl.DeviceIdType.MESH

D_MODEL = 1024
DEPTH = 4
HEAD_DIM = 64
N_HEADS = 16
D_ATTN = 1024
D_FF = 2816
ROPE_THETA = 500000.0
ROT_HALF = 8
RMS_EPS = 1e-5
DIL_STRIDES = (1, 4, 16)
ADAM_LR, ADAM_B1, ADAM_B2, ADAM_EPS, ADAM_WD, ADAM_STEP = 0.001, 0.9, 0.999, 1e-8, 0.01, 10

LANES = 128
BLK = 128
VMEM_LIMIT = 56 * 1024 * 1024
NEG = -1e30
N_CHIPS = 4
FLAT_COLS = 1024
FLAT_ROWS = 12800
HALF_ROWS = FLAT_ROWS // 2
SMALL_ROWS = 16


def _params(sem=None):
    return pltpu.CompilerParams(dimension_semantics=sem, vmem_limit_bytes=VMEM_LIMIT)


def _dot(a, b):
    return lax.dot_general(a, b, (((1,), (0,)), ((), ())), preferred_element_type=F32)


def _dot_nt(a, b):
    return lax.dot_general(a, b, (((1,), (1,)), ((), ())), preferred_element_type=F32)


def _dot_tn(a, b):
    return lax.dot_general(a, b, (((0,), (0,)), ((), ())), preferred_element_type=F32)


def _split3(x):
    x1 = x.astype(BF16)
    r1 = x - x1.astype(F32)
    x2 = r1.astype(BF16)
    x3 = (r1 - x2.astype(F32)).astype(BF16)
    return x1, x2, x3


def _dot_exact_lhs(x, t):
    x1, x2, x3 = _split3(x)
    return _dot(x1, t) + _dot(x2, t) + _dot(x3, t)


def _dot_exact_rhs(t, x):
    x1, x2, x3 = _split3(x)
    return _dot(t, x1) + _dot(t, x2) + _dot(t, x3)


def _iotas(shape=(BLK, LANES)):
    return lax.broadcasted_iota(jnp.int32, shape, 0), lax.broadcasted_iota(jnp.int32, shape, 1)


_DIMS = {"nn": (((1,), (0,)), ((), ())), "nt": (((1,), (1,)), ((), ())), "tn": (((0,), (0,)), ((), ()))}


def matmul(a, b, mode, out_dtype, name, tm, tn, tk, res=None):
    if mode == "nn":
        (m, k), (k2, n) = a.shape, b.shape
    elif mode == "nt":
        (m, k), (n, k2) = a.shape, b.shape
    else:
        (k, m), (k2, n) = a.shape, b.shape
    assert k == k2 and m % tm == 0 and n % tn == 0 and k % tk == 0, (name, a.shape, b.shape)
    nk = k // tk
    a_spec = pl.BlockSpec((tk, tm), lambda i, j, kk: (kk, i)) if mode == "tn" else pl.BlockSpec((tm, tk), lambda i, j, kk: (i, kk))
    b_spec = pl.BlockSpec((tn, tk), lambda i, j, kk: (j, kk)) if mode == "nt" else pl.BlockSpec((tk, tn), lambda i, j, kk: (kk, j))
    o_spec = pl.BlockSpec((tm, tn), lambda i, j, kk: (i, j))
    dims = _DIMS[mode]
    has_res = res is not None

    def body(*refs):
        a_ref, b_ref = refs[0], refs[1]
        r_ref = refs[2] if has_res else None
        o_ref = refs[3] if has_res else refs[2]

        def finish(v):
            if has_res:
                v = v + r_ref[...]
            o_ref[...] = v.astype(out_dtype)

        p = lax.dot_general(a_ref[...].astype(BF16), b_ref[...].astype(BF16), dims, preferred_element_type=F32)
        if nk == 1:
            finish(p)
        else:
            acc = refs[-1]
            kk = pl.program_id(2)

            @pl.when(kk == 0)
            def _():
                acc[...] = p

            @pl.when(kk > 0)
            def _():
                acc[...] += p

            @pl.when(kk == nk - 1)
            def _():
                finish(acc[...])

    ops = [a, b] + ([res] if has_res else [])
    specs = [a_spec, b_spec] + ([o_spec] if has_res else [])
    return pl.pallas_call(
        body, name=name, out_shape=jax.ShapeDtypeStruct((m, n), out_dtype),
        grid=(m // tm, n // tn, nk), in_specs=specs, out_specs=o_spec,
        scratch_shapes=[pltpu.VMEM((tm, tn), F32)] if nk > 1 else [],
        compiler_params=_params(("parallel", "parallel", "arbitrary")),
    )(*ops)


ROWS = 256


def _row_spec(cols, rows=ROWS):
    return pl.BlockSpec((rows, cols), lambda i: (i, 0))


def _fix_spec(r, cols):
    return pl.BlockSpec((r, cols), lambda i: (0, 0))


def rmsnorm_fwd(x, g, name):
    s, d = x.shape

    def body(x_ref, g_ref, h_ref):
        xv = x_ref[...]
        rstd = lax.rsqrt(jnp.mean(xv * xv, axis=-1, keepdims=True) + RMS_EPS)
        h_ref[...] = (xv * rstd * g_ref[...]).astype(BF16)

    return pl.pallas_call(
        body, name=name, out_shape=jax.ShapeDtypeStruct((s, d), BF16), grid=(s // ROWS,),
        in_specs=[_row_spec(d), _fix_spec(1, d)], out_specs=_row_spec(d), compiler_params=_params(("parallel",)),
    )(x, g)


def _rms_bwd_math(xv, gv, dh):
    rstd = lax.rsqrt(jnp.mean(xv * xv, axis=-1, keepdims=True) + RMS_EPS)
    xhat = xv * rstd
    u = dh * gv
    dx = rstd * (u - xhat * jnp.mean(u * xhat, axis=-1, keepdims=True))
    return dx, dh * xhat


def rmsnorm_bwd(x, g, dh, dres, name):
    s, d = x.shape

    def body(x_ref, g_ref, dh_ref, dres_ref, dx_ref, dg_ref):
        dx, dgt = _rms_bwd_math(x_ref[...], g_ref[...], dh_ref[...])
        dx_ref[...] = dres_ref[...] + dx
        part = jnp.sum(dgt, axis=0, keepdims=True)

        @pl.when(pl.program_id(0) == 0)
        def _():
            dg_ref[...] = part

        @pl.when(pl.program_id(0) > 0)
        def _():
            dg_ref[...] += part

    return pl.pallas_call(
        body, name=name, out_shape=(jax.ShapeDtypeStruct((s, d), F32), jax.ShapeDtypeStruct((1, d), F32)),
        grid=(s // ROWS,), in_specs=[_row_spec(d), _fix_spec(1, d), _row_spec(d), _row_spec(d)],
        out_specs=(_row_spec(d), _fix_spec(1, d)), compiler_params=_params(("arbitrary",)),
    )(x, g, dh, dres)


def loss_head(x, g, target, name):
    s, d = x.shape

    def body(x_ref, g_ref, t_ref, dx_ref, dg_ref, loss_ref):
        xv, gv = x_ref[...], g_ref[...]
        rstd = lax.rsqrt(jnp.mean(xv * xv, axis=-1, keepdims=True) + RMS_EPS)
        err = xv * rstd * gv - t_ref[...]
        dx, dgt = _rms_bwd_math(xv, gv, err * (1.0 / d))
        dx_ref[...] = dx
        part = jnp.sum(dgt, axis=0, keepdims=True)
        lpart = jnp.full((1, LANES), 0.5 / d, F32) * jnp.sum(err * err)

        @pl.when(pl.program_id(0) == 0)
        def _():
            dg_ref[...] = part
            loss_ref[...] = lpart

        @pl.when(pl.program_id(0) > 0)
        def _():
            dg_ref[...] += part
            loss_ref[...] += lpart

    return pl.pallas_call(
        body, name=name,
        out_shape=(jax.ShapeDtypeStruct((s, d), F32), jax.ShapeDtypeStruct((1, d), F32), jax.ShapeDtypeStruct((1, LANES), F32)),
        grid=(s // ROWS,), in_specs=[_row_spec(d), _fix_spec(1, d), _row_spec(d)],
        out_specs=(_row_spec(d), _fix_spec(1, d), _fix_spec(1, LANES)), compiler_params=_params(("arbitrary",)),
    )(x, g, target)


def swiglu_fwd(gu, name):
    s, f2 = gu.shape
    f = f2 // 2

    def body(gu_ref, a_ref):
        gv, uv = gu_ref[:, :f], gu_ref[:, f:]
        a_ref[...] = (gv * (1.0 / (1.0 + jnp.exp(-gv))) * uv).astype(BF16)

    return pl.pallas_call(
        body, name=name, out_shape=jax.ShapeDtypeStruct((s, f), BF16), grid=(s // ROWS,),
        in_specs=[_row_spec(f2)], out_specs=_row_spec(f), compiler_params=_params(("parallel",)),
    )(gu)


def swiglu_bwd(gu, dact, name):
    s, f2 = gu.shape
    f = f2 // 2

    def body(gu_ref, da_ref, o_ref):
        gv, uv, da = gu_ref[:, :f], gu_ref[:, f:], da_ref[...]
        sg = 1.0 / (1.0 + jnp.exp(-gv))
        o_ref[:, :f] = (da * uv * sg * (1.0 + gv * (1.0 - sg))).astype(BF16)
        o_ref[:, f:] = (da * gv * sg).astype(BF16)

    return pl.pallas_call(
        body, name=name, out_shape=jax.ShapeDtypeStruct((s, f2), BF16), grid=(s // ROWS,),
        in_specs=[_row_spec(f2), _row_spec(f)], out_specs=_row_spec(f2), compiler_params=_params(("parallel",)),
    )(gu, dact)


Q_OFF, K_OFF, V_OFF = 0, 8, 16


def _softplus_parts(z):
    sp = jnp.log(1.0 + jnp.exp(-jnp.abs(z)))
    ls = jnp.minimum(z, 0.0) - sp
    return ls, ls - z


def causal_fwd(qkv, npairs, mode, name, fq=None, fk=None):
    s = qkv.shape[0]
    nq = s // BLK
    fox = mode == "fox"

    def body(*refs):
        if fox:
            q_ref, k_ref, v_ref, fq_ref, fk_ref, o_ref, st_ref = refs
        else:
            q_ref, k_ref, v_ref, o_ref, st_ref = refs
        i = pl.program_id(1)
        row, lane = _iotas()
        q = q_ref[...]
        suffix = jnp.where(row > lane, 1.0, 0.0).astype(BF16)
        out = jnp.zeros((BLK, LANES), F32)
        for a in range(2):
            hm = (lane < HEAD_DIM) if a == 0 else (lane >= HEAD_DIM)
            qa = jnp.where(hm, q.astype(F32), 0.0).astype(BF16)

            def kv(j):
                r0 = pl.multiple_of(j * BLK, BLK)
                return k_ref[pl.ds(r0, BLK), :], v_ref[pl.ds(r0, BLK), :]

            if fox:
                fqa = fq_ref[a]

                def step(j, carry):
                    acc, mx, l = carry
                    kb, vb = kv(j)
                    r0 = pl.multiple_of(j * BLK, BLK)
                    z = _dot_nt(qa, kb) * 0.125 + fqa - fk_ref[a:a + 1, pl.ds(r0, BLK)]
                    z = jnp.where(j * BLK + lane <= i * BLK + row, z, NEG)
                    mnew = jnp.maximum(mx, jnp.max(z, axis=1, keepdims=True))
                    p = jnp.exp(z - mnew)
                    alpha = jnp.exp(mx - mnew)
                    l = alpha * l + jnp.sum(p, axis=1, keepdims=True)
                    acc = alpha * acc + _dot(p.astype(BF16), vb)
                    return acc, mnew, l

                acc, mx, l = lax.fori_loop(
                    0, i + 1, step, (jnp.zeros((BLK, LANES), F32), jnp.full((BLK, 1), NEG, F32), jnp.zeros((BLK, 1), F32)))
                oa = acc / l
                stat = mx + jnp.log(l)
            else:
                def step(jj, carry):
                    acc, c = carry
                    j = i - jj
                    kb, vb = kv(j)
                    z = _dot_nt(qa, kb) * 0.125
                    strict = j * BLK + lane < i * BLK + row
                    ls, lm = _softplus_parts(z)
                    lm = jnp.where(strict, lm, 0.0)
                    between = _dot_exact_lhs(lm, suffix) + c
                    aw = jnp.where(strict, jnp.exp(ls + between), 0.0)
                    acc = acc + _dot(aw.astype(BF16), vb)
                    return acc, c + jnp.sum(lm, axis=1, keepdims=True)

                oa, stat = lax.fori_loop(0, i + 1, step, (jnp.zeros((BLK, LANES), F32), jnp.zeros((BLK, 1), F32)))
            out = jnp.where(hm, oa, out)
            st_ref[a] = jnp.broadcast_to(stat, (BLK, LANES))
        o_ref[...] = out

    col = lambda off: (lambda p, i: (0, off + p))
    in_specs = [pl.BlockSpec((BLK, LANES), lambda p, i: (i, Q_OFF + p)),
                pl.BlockSpec((s, LANES), col(K_OFF)), pl.BlockSpec((s, LANES), col(V_OFF))]
    ops = [qkv, qkv, qkv]
    if fox:
        in_specs += [pl.BlockSpec((2, BLK, LANES), lambda p, i: (p, i, 0)), pl.BlockSpec((None, 2, s), lambda p, i: (p, 0, 0))]
        ops += [fq, fk]
    return pl.pallas_call(
        body, name=name,
        out_shape=(jax.ShapeDtypeStruct((s, npairs * LANES), F32), jax.ShapeDtypeStruct((2 * npairs, s, LANES), F32)),
        grid=(npairs, nq), in_specs=in_specs,
        out_specs=(pl.BlockSpec((BLK, LANES), lambda p, i: (i, p)), pl.BlockSpec((2, BLK, LANES), lambda p, i: (p, i, 0))),
        compiler_params=_params(("parallel", "arbitrary")),
    )(*ops)


def causal_bwd(qkv, do, stat, npairs, mode, name, fq=None, fk=None):
    s = qkv.shape[0]
    nq = s // BLK
    fox = mode == "fox"

    def body(*refs):
        if fox:
            q_ref, k_ref, v_ref, do_ref, st_ref, fq_ref, fk_ref, dq_ref, dk_ref, dv_ref, df_ref, p_s, dp_s = refs
        else:
            q_ref, k_ref, v_ref, do_ref, st_ref, dq_ref, dk_ref, dv_ref = refs
        i = pl.program_id(1)

        @pl.when(i == 0)
        def _():
            dk_ref[...] = jnp.zeros_like(dk_ref)
            dv_ref[...] = jnp.zeros_like(dv_ref)
            if fox:
                df_ref[...] = jnp.zeros_like(df_ref)

        row, lane = _iotas()
        q = q_ref[...]
        dov = do_ref[...]
        incl = jnp.where(row <= lane, 1.0, 0.0).astype(BF16)
        excl = jnp.where(row < lane, 1.0, 0.0).astype(BF16)
        dq_out = jnp.zeros((BLK, LANES), F32)
        for a in range(2):
            hm = (lane < HEAD_DIM) if a == 0 else (lane >= HEAD_DIM)
            qa = jnp.where(hm, q.astype(F32), 0.0).astype(BF16)
            doa = jnp.where(hm, dov, 0.0).astype(BF16)
            sta = st_ref[a]
            if fox:
                fqa = fq_ref[a]

                def probs(j, delta):
                    r0 = pl.multiple_of(j * BLK, BLK)
                    kb, vb = k_ref[pl.ds(r0, BLK), :], v_ref[pl.ds(r0, BLK), :]
                    z = _dot_nt(qa, kb) * 0.125 + fqa - fk_ref[a:a + 1, pl.ds(r0, BLK)]
                    p = jnp.where(j * BLK + lane <= i * BLK + row, jnp.exp(z - sta), 0.0)
                    dp = _dot_nt(doa, vb)
                    p_s[j] = p
                    dp_s[j] = dp
                    return delta + jnp.sum(p * dp, axis=1, keepdims=True)

                delta = lax.fori_loop(0, i + 1, probs, jnp.zeros((BLK, 1), F32))

                def step(j, dq):
                    r0 = pl.multiple_of(j * BLK, BLK)
                    kb = k_ref[pl.ds(r0, BLK), :]
                    p = p_s[j]
                    ds = p * (dp_s[j] - delta)
                    dsb = (ds * 0.125).astype(BF16)
                    dk_ref[pl.ds(r0, BLK), :] += _dot_tn(dsb, qa)
                    dv_ref[pl.ds(r0, BLK), :] += _dot_tn(p.astype(BF16), doa)
                    df_ref[a:a + 1, pl.ds(r0, BLK)] -= jnp.sum(ds, axis=0, keepdims=True)
                    return dq + _dot(dsb, kb)

                dqa = lax.fori_loop(0, i + 1, step, jnp.zeros((BLK, LANES), F32))
            else:
                def step(j, carry):
                    dq, cm, cg = carry
                    r0 = pl.multiple_of(j * BLK, BLK)
                    kb, vb = k_ref[pl.ds(r0, BLK), :], v_ref[pl.ds(r0, BLK), :]
                    z = _dot_nt(qa, kb) * 0.125
                    strict = j * BLK + lane < i * BLK + row
                    ls, lm = _softplus_parts(z)
                    lm = jnp.where(strict, lm, 0.0)
                    beta = jnp.exp(ls)
                    between = sta - (cm + _dot_exact_lhs(lm, incl))
                    aw = jnp.where(strict, jnp.exp(ls + between), 0.0)
                    g = aw * _dot_nt(doa, vb)
                    pre = cg + _dot_exact_lhs(g, excl)
                    dz = jnp.where(strict, g * (1.0 - beta) - pre * beta, 0.0)
                    dzb = (dz * 0.125).astype(BF16)
                    dk_ref[pl.ds(r0, BLK), :] += _dot_tn(dzb, qa)
                    dv_ref[pl.ds(r0, BLK), :] += _dot_tn(aw.astype(BF16), doa)
                    return (dq + _dot(dzb, kb), cm + jnp.sum(lm, axis=1, keepdims=True), cg + jnp.sum(g, axis=1, keepdims=True))

                dqa, _, _ = lax.fori_loop(
                    0, i + 1, step, (jnp.zeros((BLK, LANES), F32), jnp.zeros((BLK, 1), F32), jnp.zeros((BLK, 1), F32)))
            dq_out = jnp.where(hm, dqa, dq_out)
        dq_ref[...] = dq_out.astype(BF16)

    col = lambda off: (lambda p, i: (0, off + p))
    blk = pl.BlockSpec((BLK, LANES), lambda p, i: (i, p))
    acc = pl.BlockSpec((s, LANES), lambda p, i: (0, p))
    st_spec = pl.BlockSpec((2, BLK, LANES), lambda p, i: (p, i, 0))
    in_specs = [pl.BlockSpec((BLK, LANES), lambda p, i: (i, Q_OFF + p)), pl.BlockSpec((s, LANES), col(K_OFF)),
                pl.BlockSpec((s, LANES), col(V_OFF)), blk, st_spec]
    ops = [qkv, qkv, qkv, do, stat]
    w = npairs * LANES
    out_shape = [jax.ShapeDtypeStruct((s, w), BF16), jax.ShapeDtypeStruct((s, w), F32), jax.ShapeDtypeStruct((s, w), F32)]
    out_specs = [blk, acc, acc]
    scratch = []
    if fox:
        fk_spec = pl.BlockSpec((None, 2, s), lambda p, i: (p, 0, 0))
        in_specs += [st_spec, fk_spec]
        ops += [fq, fk]
        out_shape.append(jax.ShapeDtypeStruct((npairs, 2, s), F32))
        out_specs.append(fk_spec)
        scratch = [pltpu.VMEM((nq, BLK, LANES), F32)] * 2
    return pl.pallas_call(
        body, name=name, out_shape=tuple(out_shape), grid=(npairs, nq), in_specs=in_specs, out_specs=tuple(out_specs),
        scratch_shapes=scratch, compiler_params=_params(("parallel", "arbitrary")),
    )(*ops)


def forget_fwd(fl, bias, name):
    s = fl.shape[0]

    def body(fl_ref, b_ref, f_ref):
        row, lane = _iotas()
        lower = jnp.where(lane <= row, 1.0, 0.0).astype(BF16)

        def step(n, carry):
            r0 = pl.multiple_of(n * BLK, BLK)
            ls, _ = _softplus_parts(fl_ref[pl.ds(r0, BLK), :] + b_ref[...])
            blk = _dot_exact_rhs(lower, ls) + carry
            f_ref[pl.ds(r0, BLK), :] = blk
            return blk[BLK - 1:BLK, :]

        lax.fori_loop(0, s // BLK, step, jnp.zeros((1, LANES), F32))

    return pl.pallas_call(
        body, name=name, out_shape=jax.ShapeDtypeStruct((s, LANES), F32),
        in_specs=[pl.BlockSpec(memory_space=pltpu.VMEM)] * 2, out_specs=pl.BlockSpec(memory_space=pltpu.VMEM),
        compiler_params=_params(),
    )(fl, bias)


def forget_bwd(fl, bias, df, name):
    s = fl.shape[0]
    nb = s // BLK

    def body(fl_ref, b_ref, df_ref, o_ref, db_ref):
        row, lane = _iotas()
        upper = jnp.where(lane >= row, 1.0, 0.0).astype(BF16)

        def step(nn, carry):
            tail, db = carry
            r0 = pl.multiple_of((nb - 1 - nn) * BLK, BLK)
            dls = _dot_exact_rhs(upper, df_ref[pl.ds(r0, BLK), :]) + tail
            xv = fl_ref[pl.ds(r0, BLK), :] + b_ref[...]
            dfl = dls * (1.0 / (1.0 + jnp.exp(xv)))
            o_ref[pl.ds(r0, BLK), :] = dfl
            return dls[0:1, :], db + jnp.sum(dfl, axis=0, keepdims=True)

        _, db = lax.fori_loop(0, nb, step, (jnp.zeros((1, LANES), F32), jnp.zeros((1, LANES), F32)))
        db_ref[...] = db

    return pl.pallas_call(
        body, name=name, out_shape=(jax.ShapeDtypeStruct((s, LANES), F32), jax.ShapeDtypeStruct((1, LANES), F32)),
        in_specs=[pl.BlockSpec(memory_space=pltpu.VMEM)] * 3,
        out_specs=(pl.BlockSpec(memory_space=pltpu.VMEM), pl.BlockSpec(memory_space=pltpu.VMEM)),
        compiler_params=_params(),
    )(fl, bias, df)


def _rot_tables(s):
    inv = ROPE_THETA ** (-jnp.arange(ROT_HALF, dtype=F32) * 2.0 / (2 * ROT_HALF))
    ang = jnp.arange(s, dtype=F32)[:, None] * inv[None, :]
    cos, sin = jnp.cos(ang), jnp.sin(ang)
    z8 = jnp.zeros((s, ROT_HALF), F32)
    rest = HEAD_DIM - 2 * ROT_HALF
    zr, onr = jnp.zeros((s, rest), F32), jnp.ones((s, rest), F32)
    tile = lambda t: jnp.tile(t, (1, 2))
    return tile(jnp.concatenate([cos, cos, onr], 1)), tile(jnp.concatenate([-sin, z8, zr], 1)), tile(jnp.concatenate([z8, sin, zr], 1))


def rotary_prep(qkv, tables, name):
    s = qkv.shape[0]
    w = 4 * LANES

    def body(q_ref, k_ref, v_ref, c_ref, s1_ref, s2_ref, qo_ref, ko_ref, vo_ref):
        c, s1, s2 = c_ref[...], s1_ref[...], s2_ref[...]

        def rot(xv):
            return xv * c + pltpu.roll(xv, LANES - ROT_HALF, 1) * s1 + pltpu.roll(xv, ROT_HALF, 1) * s2

        qo_ref[...] = rot(q_ref[...].astype(F32)) * 0.125
        ko_ref[...] = rot(k_ref[...].astype(F32))
        vo_ref[...] = v_ref[...].astype(F32)

    cb = lambda off: pl.BlockSpec((ROWS, LANES), lambda i, j: (i, off + j))
    tb = pl.BlockSpec((ROWS, LANES), lambda i, j: (i, 0))
    out = jax.ShapeDtypeStruct((s, w), F32)
    return pl.pallas_call(
        body, name=name, out_shape=(out, out, out), grid=(s // ROWS, 4),
        in_specs=[cb(Q_OFF + 4), cb(K_OFF + 4), cb(V_OFF + 4), tb, tb, tb], out_specs=(cb(0), cb(0), cb(0)),
        compiler_params=_params(("parallel", "parallel")),
    )(qkv, qkv, qkv, *tables)


def rotary_bwd(dq, dk, dv, tables, name):
    s, w = dq.shape

    def body(dq_ref, dk_ref, dv_ref, c_ref, s1_ref, s2_ref, qo_ref, ko_ref, vo_ref):
        c, s1, s2 = c_ref[...], s1_ref[...], s2_ref[...]

        def rot_t(dy):
            return dy * c + pltpu.roll(dy * s1, ROT_HALF, 1) + pltpu.roll(dy * s2, LANES - ROT_HALF, 1)

        qo_ref[...] = (rot_t(dq_ref[...]) * 0.125).astype(BF16)
        ko_ref[...] = rot_t(dk_ref[...]).astype(BF16)
        vo_ref[...] = dv_ref[...].astype(BF16)

    cb = pl.BlockSpec((ROWS, LANES), lambda i, j: (i, j))
    tb = pl.BlockSpec((ROWS, LANES), lambda i, j: (i, 0))
    out = jax.ShapeDtypeStruct((s, w), BF16)
    return pl.pallas_call(
        body, name=name, out_shape=(out, out, out), grid=(s // ROWS, w // LANES),
        in_specs=[cb, cb, cb, tb, tb, tb], out_specs=(cb, cb, cb), compiler_params=_params(("parallel", "parallel")),
    )(dq, dk, dv, *tables)


def _deinterleave(dst, src_ref, stride, s, dtype):
    length = s // stride
    for r in range(stride):
        if stride == 1:
            dst[...] = src_ref[...].astype(dtype)
        else:
            dst[r * length:(r + 1) * length, :] = src_ref[pl.ds(r, length, stride=stride), :].astype(dtype)


def _band_masks(row, lane, first):
    return lane <= row, lane >= row + jnp.where(first, BLK, 0)


def dilated_fwd(qd, kd, vd, name):
    s, w = qd.shape
    npairs = w // LANES
    nblk = s // BLK

    def body(q_ref, k_ref, v_ref, o_ref, lse_ref, qs, ks, vs, od, ld, on, ln):
        row, lane = _iotas()
        for pi, stride in enumerate(DIL_STRIDES):
            per = (s // stride) // BLK
            _deinterleave(qs, q_ref, stride, s, BF16)
            _deinterleave(ks, k_ref, stride, s, BF16)
            _deinterleave(vs, v_ref, stride, s, BF16)

            def block(b, carry):
                r0 = pl.multiple_of(b * BLK, BLK)
                rp = pl.multiple_of(jnp.maximum(b - 1, 0) * BLK, BLK)
                mc, mp = _band_masks(row, lane, b % per == 0)
                q = qs[pl.ds(r0, BLK), :]
                kc, kp, vc, vp = ks[pl.ds(r0, BLK), :], ks[pl.ds(rp, BLK), :], vs[pl.ds(r0, BLK), :], vs[pl.ds(rp, BLK), :]
                out = jnp.zeros((BLK, LANES), F32)
                lse = jnp.zeros((BLK, LANES), F32)
                for a in range(2):
                    hm = (lane < HEAD_DIM) if a == 0 else (lane >= HEAD_DIM)
                    qa = jnp.where(hm, q.astype(F32), 0.0).astype(BF16)
                    sc = jnp.where(mc, _dot_nt(qa, kc), NEG)
                    sp = jnp.where(mp, _dot_nt(qa, kp), NEG)
                    mx = jnp.maximum(jnp.max(sc, axis=1, keepdims=True), jnp.max(sp, axis=1, keepdims=True))
                    pc, pp = jnp.exp(sc - mx), jnp.exp(sp - mx)
                    l = jnp.sum(pc, axis=1, keepdims=True) + jnp.sum(pp, axis=1, keepdims=True)
                    oa = (_dot(pc.astype(BF16), vc) + _dot(pp.astype(BF16), vp)) / l
                    out = jnp.where(hm, oa, out)
                    lse = jnp.where(hm, mx + jnp.log(l), lse)
                od[pl.ds(r0, BLK), :] = out
                ld[pl.ds(r0, BLK), :] = lse
                return carry

            lax.fori_loop(0, nblk, block, 0)
            length = s // stride
            for r in range(stride):
                if stride == 1:
                    on[pi] = od[...]
                    ln[pi] = ld[...]
                else:
                    on[pi, pl.ds(r, length, stride=stride), :] = od[r * length:(r + 1) * length, :]
                    ln[pi, pl.ds(r, length, stride=stride), :] = ld[r * length:(r + 1) * length, :]

        def merge(n, carry):
            r0 = pl.multiple_of(n * BLK, BLK)
            ls = [ln[pi, pl.ds(r0, BLK), :] for pi in range(3)]
            mx = jnp.maximum(jnp.maximum(ls[0], ls[1]), ls[2])
            ws = [jnp.exp(lv - mx) for lv in ls]
            den = ws[0] + ws[1] + ws[2]
            num = ws[0] * on[0, pl.ds(r0, BLK), :] + ws[1] * on[1, pl.ds(r0, BLK), :] + ws[2] * on[2, pl.ds(r0, BLK), :]
            o_ref[pl.ds(r0, BLK), :] = num / den
            lse_ref[pl.ds(r0, BLK), :] = mx + jnp.log(den)
            return carry

        lax.fori_loop(0, nblk, merge, 0)

    colspec = pl.BlockSpec((s, LANES), lambda p: (0, p))
    out = jax.ShapeDtypeStruct((s, w), F32)
    return pl.pallas_call(
        body, name=name, out_shape=(out, out), grid=(npairs,), in_specs=[colspec] * 3, out_specs=(colspec, colspec),
        scratch_shapes=[pltpu.VMEM((s, LANES), BF16)] * 3 + [pltpu.VMEM((s, LANES), F32)] * 2 + [pltpu.VMEM((3, s, LANES), F32)] * 2,
        compiler_params=_params(("parallel",)),
    )(qd, kd, vd)


def dilated_bwd(qd, kd, vd, do, out, lse, do_off, name):
    s, w = qd.shape
    npairs = w // LANES
    nblk = s // BLK

    def body(q_ref, k_ref, v_ref, do_ref, out_ref, lse_ref, dq_ref, dk_ref, dv_ref, qs, ks, vs, dos, dls, lss, dqd, dkd, dvd, dln):
        row, lane = _iotas()
        same_head = jnp.where((row < HEAD_DIM) == (lane < HEAD_DIM), 1.0, 0.0).astype(BF16)

        def delta_blk(n, carry):
            r0 = pl.multiple_of(n * BLK, BLK)
            dln[pl.ds(r0, BLK), :] = _dot_exact_lhs(do_ref[pl.ds(r0, BLK), :] * out_ref[pl.ds(r0, BLK), :], same_head)
            return carry

        lax.fori_loop(0, nblk, delta_blk, 0)
        for pi, stride in enumerate(DIL_STRIDES):
            per = (s // stride) // BLK
            _deinterleave(qs, q_ref, stride, s, BF16)
            _deinterleave(ks, k_ref, stride, s, BF16)
            _deinterleave(vs, v_ref, stride, s, BF16)
            _deinterleave(dos, do_ref, stride, s, BF16)
            _deinterleave(dls, dln, stride, s, F32)
            _deinterleave(lss, lse_ref, stride, s, F32)

            def block(b, carry):
                r0 = pl.multiple_of(b * BLK, BLK)
                rp = pl.multiple_of(jnp.maximum(b - 1, 0) * BLK, BLK)
                first = b % per == 0
                mc, mp = _band_masks(row, lane, first)
                q, dov = qs[pl.ds(r0, BLK), :], dos[pl.ds(r0, BLK), :]
                kc, kp, vc, vp = ks[pl.ds(r0, BLK), :], ks[pl.ds(rp, BLK), :], vs[pl.ds(r0, BLK), :], vs[pl.ds(rp, BLK), :]
                lse_t, dl_t = lss[pl.ds(r0, BLK), :], dls[pl.ds(r0, BLK), :]
                dq = jnp.zeros((BLK, LANES), F32)
                dkc = jnp.zeros((BLK, LANES), F32)
                dkp = jnp.zeros((BLK, LANES), F32)
                dvc = jnp.zeros((BLK, LANES), F32)
                dvp = jnp.zeros((BLK, LANES), F32)
                for a in range(2):
                    hm = (lane < HEAD_DIM) if a == 0 else (lane >= HEAD_DIM)
                    pick = lane == a * HEAD_DIM
                    qa = jnp.where(hm, q.astype(F32), 0.0).astype(BF16)
                    doa = jnp.where(hm, dov.astype(F32), 0.0).astype(BF16)
                    lse_a = jnp.sum(jnp.where(pick, lse_t, 0.0), axis=1, keepdims=True)
                    dl_a = jnp.sum(jnp.where(pick, dl_t, 0.0), axis=1, keepdims=True)
                    pc = jnp.where(mc, jnp.exp(_dot_nt(qa, kc) - lse_a), 0.0)
                    pp = jnp.where(mp, jnp.exp(_dot_nt(qa, kp) - lse_a), 0.0)
                    dsc = (pc * (_dot_nt(doa, vc) - dl_a)).astype(BF16)
                    dsp = (pp * (_dot_nt(doa, vp) - dl_a)).astype(BF16)
                    dq = jnp.where(hm, _dot(dsc, kc) + _dot(dsp, kp), dq)
                    dkc += _dot_tn(dsc, qa)
                    dkp += _dot_tn(dsp, qa)
                    dvc += _dot_tn(pc.astype(BF16), doa)
                    dvp += _dot_tn(pp.astype(BF16), doa)
                dqd[pl.ds(r0, BLK), :] = dq
                dkd[pl.ds(r0, BLK), :] = dkc
                dvd[pl.ds(r0, BLK), :] = dvc

                @pl.when(jnp.logical_not(first))
                def _():
                    dkd[pl.ds(rp, BLK), :] += dkp
                    dvd[pl.ds(rp, BLK), :] += dvp

                return carry

            lax.fori_loop(0, nblk, block, 0)
            length = s // stride
            for dst, src in ((dq_ref, dqd), (dk_ref, dkd), (dv_ref, dvd)):
                for r in range(stride):
                    if stride == 1:
                        dst[...] = src[...]
                    else:
                        dst[pl.ds(r, length, stride=stride), :] += src[r * length:(r + 1) * length, :]

    colspec = pl.BlockSpec((s, LANES), lambda p: (0, p))
    do_spec = pl.BlockSpec((s, LANES), lambda p: (0, do_off + p))
    o3 = jax.ShapeDtypeStruct((s, w), F32)
    return pl.pallas_call(
        body, name=name, out_shape=(o3, o3, o3), grid=(npairs,),
        in_specs=[colspec, colspec, colspec, do_spec, colspec, colspec], out_specs=(colspec, colspec, colspec),
        scratch_shapes=[pltpu.VMEM((s, LANES), BF16)] * 4 + [pltpu.VMEM((s, LANES), F32)] * 6,
        compiler_params=_params(("parallel",)),
    )(qd, kd, vd, do, out, lse)


def adamw(w, g, m, v, name):
    rows, cols = w.shape
    rb = min(rows, ROWS)
    c1 = 1.0 - ADAM_B1 ** ADAM_STEP
    c2 = 1.0 - ADAM_B2 ** ADAM_STEP

    def body(w_ref, g_ref, m_ref, v_ref, d_ref, mo_ref, vo_ref):
        gv = g_ref[...]
        mn = ADAM_B1 * m_ref[...] + (1.0 - ADAM_B1) * gv
        vn = ADAM_B2 * v_ref[...] + (1.0 - ADAM_B2) * (gv * gv)
        d_ref[...] = -ADAM_LR * ((mn / c1) / (jnp.sqrt(vn / c2) + ADAM_EPS) + ADAM_WD * w_ref[...])
        mo_ref[...] = mn
        vo_ref[...] = vn

    spec = _row_spec(cols, rb)
    out = jax.ShapeDtypeStruct((rows, cols), F32)
    return pl.pallas_call(
        body, name=name, out_shape=(out, out, out), grid=(rows // rb,), in_specs=[spec] * 4, out_specs=(spec,) * 3,
        compiler_params=_params(("parallel",)),
    )(w, g, m, v)


def sum_rows(parts, out_dtype, name):
    rows, cols = parts[0].shape
    n = len(parts)

    def body(*refs):
        acc = refs[0][...].astype(F32)
        for r in refs[1:n]:
            acc = acc + r[...].astype(F32)
        refs[n][...] = acc.astype(out_dtype)

    spec = _row_spec(cols)
    return pl.pallas_call(
        body, name=name, out_shape=jax.ShapeDtypeStruct((rows, cols), out_dtype), grid=(rows // ROWS,),
        in_specs=[spec] * n, out_specs=spec, compiler_params=_params(("parallel",)),
    )(*parts)


def _coords():
    return lax.axis_index("x"), lax.axis_index("y"), lax.axis_index("c")


def _other_chips(x, y):
    return ((1 - x, y), (x, 1 - y), (1 - x, 1 - y))


_ANY = pl.BlockSpec(memory_space=pl.ANY)


def gather_weights(shard, name):
    def body(s_ref, o_ref, send_sems, recv_sems, local_sem):
        x, y, c = _coords()
        me = 2 * x + y
        sibling = (x, y, 1 - c)
        chips = _other_chips(x, y)

        def copy(k, src, dst, to):
            return pltpu.make_async_remote_copy(src_ref=src, dst_ref=dst, send_sem=send_sems.at[k], recv_sem=recv_sems.at[k],
                                                device_id=to, device_id_type=MESH)

        mine = pltpu.make_async_copy(s_ref, o_ref.at[me], local_sem)
        mine.start()
        first = [copy(k, s_ref.at[c], o_ref.at[me, c], (px, py, c)) for k, (px, py) in enumerate(chips)]
        for cp in first:
            cp.start()
        passed = []
        for k, (px, py) in enumerate(chips):
            landed = o_ref.at[2 * px + py, c]
            copy(k, s_ref.at[c], landed, (px, py, c)).wait_recv()
            fwd = copy(3 + k, landed, landed, sibling)
            fwd.start()
            passed.append(fwd)
        for k, (px, py) in enumerate(chips):
            other = o_ref.at[2 * px + py, 1 - c]
            copy(3 + k, other, other, sibling).wait_recv()
        for cp in first + passed:
            cp.wait_send()
        mine.wait()

    return pl.pallas_call(
        body, name=name, out_shape=jax.ShapeDtypeStruct((N_CHIPS,) + shard.shape, shard.dtype),
        in_specs=[_ANY], out_specs=_ANY,
        scratch_shapes=[pltpu.SemaphoreType.DMA((6,)), pltpu.SemaphoreType.DMA((6,)), pltpu.SemaphoreType.DMA],
        compiler_params=_params(),
    )(shard)


def swap_with_sibling(v, name):
    def body(v_ref, o_ref, send_sem, recv_sem):
        x, y, c = _coords()
        cp = pltpu.make_async_remote_copy(src_ref=v_ref, dst_ref=o_ref, send_sem=send_sem, recv_sem=recv_sem,
                                          device_id=(x, y, 1 - c), device_id_type=MESH)
        cp.start()
        cp.wait()

    return pl.pallas_call(
        body, name=name, out_shape=jax.ShapeDtypeStruct(v.shape, v.dtype), in_specs=[_ANY], out_specs=_ANY,
        scratch_shapes=[pltpu.SemaphoreType.DMA, pltpu.SemaphoreType.DMA], compiler_params=_params(),
    )(v)


def scatter_to_chips(h, name):
    def body(h_ref, o_ref, send_sems, recv_sems):
        x, y, c = _coords()
        cps = []
        for k, (px, py) in enumerate(_other_chips(x, y)):
            cp = pltpu.make_async_remote_copy(src_ref=h_ref.at[2 * px + py], dst_ref=o_ref.at[k], send_sem=send_sems.at[k],
                                              recv_sem=recv_sems.at[k], device_id=(px, py, c), device_id_type=MESH)
            cp.start()
            cps.append(cp)
        for cp in cps:
            cp.wait()

    return pl.pallas_call(
        body, name=name, out_shape=jax.ShapeDtypeStruct((3,) + h.shape[1:], h.dtype), in_specs=[_ANY], out_specs=_ANY,
        scratch_shapes=[pltpu.SemaphoreType.DMA((3,)), pltpu.SemaphoreType.DMA((3,))], compiler_params=_params(),
    )(h)


def allsum_small(part, name):
    def body(p_ref, tot_ref, all_ref, send_sems, recv_sems):
        x, y, c = _coords()
        me, sibling = (x, y, c), (x, y, 1 - c)
        chips = _other_chips(x, y)

        def slot(px, py, pc):
            return all_ref.at[4 * px + 2 * py + pc]

        def copy(k, block, to, src=None):
            return pltpu.make_async_remote_copy(src_ref=slot(*block) if src is None else src, dst_ref=slot(*block),
                                                send_sem=send_sems.at[k], recv_sem=recv_sems.at[k], device_id=to, device_id_type=MESH)

        slot(*me)[...] = p_ref[...]
        first = [copy(0, me, sibling, src=p_ref)] + [copy(1 + j, me, (*chip, c), src=p_ref) for j, chip in enumerate(chips)]
        for cp in first:
            cp.start()
        passed = [copy(4 + j, (*chip, c), sibling) for j, chip in enumerate(chips)]
        for j, chip in enumerate(chips):
            copy(1 + j, (*chip, c), me).wait_recv()
            passed[j].start()
        copy(0, sibling, me).wait_recv()
        for j, chip in enumerate(chips):
            copy(4 + j, (*chip, 1 - c), me).wait_recv()
        for cp in first + passed:
            cp.wait_send()
        tot = all_ref[0]
        for d in range(1, 8):
            tot = tot + all_ref[d]
        tot_ref[...] = tot

    vm = pl.BlockSpec(memory_space=pltpu.VMEM)
    return pl.pallas_call(
        body, name=name, out_shape=jax.ShapeDtypeStruct(part.shape, F32), in_specs=[vm], out_specs=vm,
        scratch_shapes=[pltpu.VMEM((8,) + part.shape, F32), pltpu.SemaphoreType.DMA((7,)), pltpu.SemaphoreType.DMA((7,))],
        compiler_params=_params(),
    )(part)


FAMILIES = (((2, 1024, 768), 2), ((2, 256, 1024), 1), ((2, 1024, 772), 2), ((2, 256, 1024), 1),
            ((4, 1024, 1408), 2), ((4, 704, 1024), 1))


def _size(shape):
    n = 1
    for d in shape:
        n *= d
    return n


def pack_rows(parts, dtype):
    flat = jnp.concatenate([p.reshape(-1).astype(dtype) for p in parts])
    return jnp.pad(flat, (0, FLAT_ROWS * FLAT_COLS - flat.shape[0])).reshape(FLAT_ROWS, FLAT_COLS)


def unpack_rows(rows):
    flat, out, at = rows.reshape(-1), [], 0
    for shape, _ in FAMILIES:
        out.append(flat[at:at + _size(shape)].reshape(shape))
        at += _size(shape)
    return out


def chip_slices(full, fam):
    shape, axis = FAMILIES[fam]
    width = shape[axis]
    return [lax.slice_in_dim(full, j * width, (j + 1) * width, axis=axis) for j in range(N_CHIPS)]


def _tables_for(s):
    return _rot_tables(s)


def gather_full(w_shards):
    gathered = gather_weights(pack_rows(w_shards, BF16).reshape(2, HALF_ROWS, FLAT_COLS), "gather_weights")
    per_chip = [unpack_rows(gathered[j]) for j in range(N_CHIPS)]
    return [jnp.concatenate([per_chip[j][f] for j in range(N_CHIPS)], axis=FAMILIES[f][1]) for f in range(len(FAMILIES))]


def reduce_grads(fam_grads):
    c = lax.axis_index("c")
    me = 2 * lax.axis_index("x") + lax.axis_index("y")
    by_chip = [[chip_slices(fam_grads[f], f)[j] for f in range(len(FAMILIES))] for j in range(N_CHIPS)]
    packed = jnp.stack([pack_rows(by_chip[j], BF16) for j in range(N_CHIPS)])
    keep_half = lax.dynamic_slice_in_dim(packed, c * HALF_ROWS, HALF_ROWS, axis=1)
    send_half = lax.dynamic_slice_in_dim(packed, (1 - c) * HALF_ROWS, HALF_ROWS, axis=1)
    from_sibling = swap_with_sibling(send_half, "grad_pair_swap")
    flat2 = lambda a: a.reshape(-1, FLAT_COLS)
    pair = sum_rows([flat2(keep_half), flat2(from_sibling)], BF16, "grad_pair_sum").reshape(N_CHIPS, HALF_ROWS, FLAT_COLS)
    from_chips = scatter_to_chips(pair, "grad_chip_scatter")
    mine = lax.dynamic_index_in_dim(pair, me, axis=0, keepdims=False)
    g_half = sum_rows([mine, from_chips[0], from_chips[1], from_chips[2]], F32, "grad_chip_sum")
    g_other = swap_with_sibling(g_half, "grad_half_swap")
    return jnp.concatenate([jnp.where(c == 0, g_half, g_other), jnp.where(c == 0, g_other, g_half)], axis=0)


def kernel(x, norm_mix, w_qkv_even, w_o_even, w_qkvf_odd, b_forget, w_o_odd, norm_ffn, w_ffn_in, w_ffn_out, norm_final, loss_target, m_norm_mix, m_w_qkv_even, m_w_o_even, m_w_qkvf_odd, m_b_forget, m_w_o_odd, m_norm_ffn, m_w_ffn_in, m_w_ffn_out, m_norm_final, v_norm_mix, v_w_qkv_even, v_w_o_even, v_w_qkvf_odd, v_b_forget, v_w_o_odd, v_norm_ffn, v_w_ffn_in, v_w_ffn_out, v_norm_final):
    w_shards = [w_qkv_even, w_o_even, w_qkvf_odd, w_o_odd, w_ffn_in, w_ffn_out]
    full = gather_full(w_shards)

    dcur, fam_grads, (g_mix, g_ffn, g_final, g_bias), loss_part = local_step(
        x[0], loss_target[0], norm_mix, norm_ffn, norm_final, b_forget, full)

    zero_row = jnp.zeros((1, D_MODEL), F32)
    pad16 = lambda v: jnp.pad(v, (0, D_MODEL - v.shape[0]))[None, :]
    small_rows = lambda mix, ffn, fin, bias, last: jnp.concatenate(
        [r.reshape(1, D_MODEL) for r in mix] + [r.reshape(1, D_MODEL) for r in ffn] + [fin.reshape(1, D_MODEL)]
        + [pad16(b) for b in bias] + [last] + [zero_row] * (SMALL_ROWS - 12), axis=0)
    loss_row = pad16(loss_part[0, :1])
    small_g = allsum_small(small_rows(g_mix, g_ffn, g_final, g_bias, loss_row), "allsum_small")
    loss = small_g[11, 0]
    small_g = small_g.at[11].set(0.0)
    sw = small_rows(list(norm_mix), list(norm_ffn), norm_final, list(b_forget), zero_row)
    sm = small_rows(list(m_norm_mix), list(m_norm_ffn), m_norm_final, list(m_b_forget), zero_row)
    sv = small_rows(list(v_norm_mix), list(v_norm_ffn), v_norm_final, list(v_b_forget), zero_row)
    sd, snm, snv = adamw(sw, small_g, sm, sv, "adamw_small")

    def small_out(a):
        return a[0:4], a[8, :], a[9:11, :N_HEADS], a[4:8]

    g_shard = reduce_grads(fam_grads)
    big_w = pack_rows(w_shards, F32)
    big_m = pack_rows([m_w_qkv_even, m_w_o_even, m_w_qkvf_odd, m_w_o_odd, m_w_ffn_in, m_w_ffn_out], F32)
    big_v = pack_rows([v_w_qkv_even, v_w_o_even, v_w_qkvf_odd, v_w_o_odd, v_w_ffn_in, v_w_ffn_out], F32)
    bd, bm, bv = adamw(big_w, g_shard, big_m, big_v, "adamw_big")

    def outputs(small, big):
        mix, fin, bias, ffn = small_out(small)
        qkv_e, o_e, qkvf, o_o, fi, fo = unpack_rows(big)
        return [mix, qkv_e, o_e, qkvf, bias, o_o, ffn, fi, fo, fin]

    return (loss, dcur[None], *outputs(small_g, g_shard), *outputs(sd, bd), *outputs(snm, bm), *outputs(snv, bv))


def local_step(xs, target, norm_mix, norm_ffn, norm_final, b_forget, full):
    s = xs.shape[0]
    tables = _tables_for(s)
    wqkv_e, wo_e, wqkvf, wo_o, wfi, wfo = full
    wf_pad = jnp.pad(wqkvf[:, :, 3 * D_ATTN:], ((0, 0), (0, 0), (0, LANES - N_HEADS)))
    wqkv_o = wqkvf[:, :, :3 * D_ATTN]
    wqkvf_pad = jnp.concatenate([wqkv_o, wf_pad], axis=2)
    bias_pad = jnp.pad(b_forget, ((0, 0), (0, LANES - N_HEADS)))

    saved = []
    cur = xs
    for layer in range(DEPTH):
        n = f"l{layer}"
        h1 = rmsnorm_fwd(cur, norm_mix[layer:layer + 1], n + "_norm_mix")
        keep = {"x": cur, "h1": h1}
        if layer % 2 == 0:
            qkv = matmul(h1, wqkv_e[layer // 2], "nn", BF16, n + "_qkv", 512, 768, 1024)
            o_sb, st_sb = causal_fwd(qkv, 4, "sb", n + "_sb_fwd")
            qd, kd, vd = rotary_prep(qkv, tables, n + "_rotary")
            o_dil, lse_dil = dilated_fwd(qd, kd, vd, n + "_dil_fwd")
            attn = jnp.concatenate([o_sb, o_dil], axis=1)
            keep.update(qkv=qkv, st=st_sb, qd=qd, kd=kd, vd=vd, o_dil=o_dil, lse_dil=lse_dil)
            w_out = wo_e[layer // 2]
        else:
            li = layer // 2
            qkv = matmul(h1, wqkv_o[li], "nn", BF16, n + "_qkv", 512, 768, 1024)
            fl = matmul(h1, wf_pad[li], "nn", F32, n + "_fgate", 512, LANES, 1024)
            cum = forget_fwd(fl, bias_pad[li:li + 1], n + "_forget_fwd")
            f_heads = cum[:, :N_HEADS].T
            fq = jnp.broadcast_to(f_heads[:, :, None], (N_HEADS, s, LANES))
            fk = f_heads.reshape(N_HEADS // 2, 2, s)
            attn, st_fox = causal_fwd(qkv, 8, "fox", n + "_fox_fwd", fq=fq, fk=fk)
            keep.update(qkv=qkv, st=st_fox, fl=fl, fq=fq, fk=fk)
            w_out = wo_o[layer // 2]
        mid = matmul(attn, w_out, "nn", F32, n + "_attn_out", 512, 1024, 1024, res=cur)
        h2 = rmsnorm_fwd(mid, norm_ffn[layer:layer + 1], n + "_norm_ffn")
        gu = matmul(h2, wfi[layer], "nn", F32, n + "_ffn_in", 512, 512, 1024)
        act = swiglu_fwd(gu, n + "_swiglu")
        cur = matmul(act, wfo[layer], "nn", F32, n + "_ffn_out", 512, 1024, D_FF, res=mid)
        keep.update(attn=attn, mid=mid, h2=h2, gu=gu, act=act)
        saved.append(keep)

    dcur, g_final, loss_part = loss_head(cur, norm_final.reshape(1, D_MODEL), target, "loss_head")

    g_mix, g_ffn = [None] * DEPTH, [None] * DEPTH
    g_bias = [None] * (DEPTH // 2)
    g_qkv_e, g_o_e, g_qkvf, g_o_o, g_fi, g_fo = [None] * 2, [None] * 2, [None] * 2, [None] * 2, [None] * DEPTH, [None] * DEPTH
    for layer in reversed(range(DEPTH)):
        n = f"l{layer}"
        kp = saved[layer]
        g_fo[layer] = matmul(kp["act"], dcur, "tn", F32, n + "_d_w_ffn_out", 1408, 1024, 512)
        dact = matmul(dcur, wfo[layer], "nt", F32, n + "_d_act", 512, 1408, 1024)
        dgu = swiglu_bwd(kp["gu"], dact, n + "_d_swiglu")
        g_fi[layer] = matmul(kp["h2"], dgu, "tn", F32, n + "_d_w_ffn_in", 1024, 512, 2048)
        dh2 = matmul(dgu, wfi[layer], "nt", F32, n + "_d_h2", 512, 1024, 512)
        dmid, g_ffn[layer] = rmsnorm_bwd(kp["mid"], norm_ffn[layer:layer + 1], dh2, dcur, n + "_d_norm_ffn")
        li = layer // 2
        if layer % 2 == 0:
            w_out, w_in = wo_e[li], wqkv_e[li]
        else:
            w_out, w_in = wo_o[li], wqkvf_pad[li]
        g_out = matmul(kp["attn"], dmid, "tn", F32, n + "_d_w_o", 1024, 1024, 512)
        dattn = matmul(dmid, w_out, "nt", F32, n + "_d_attn", 512, 1024, 1024)
        if layer % 2 == 0:
            g_o_e[li] = g_out
            dq_a, dk_a, dv_a = causal_bwd(kp["qkv"], dattn, kp["st"], 4, "sb", n + "_sb_bwd")
            dqd, dkd, dvd = dilated_bwd(kp["qd"], kp["kd"], kp["vd"], dattn, kp["o_dil"], kp["lse_dil"], 4, n + "_dil_bwd")
            dq_b, dk_b, dv_b = rotary_bwd(dqd, dkd, dvd, tables, n + "_d_rotary")
            dproj = jnp.concatenate([dq_a, dq_b, dk_a.astype(BF16), dk_b, dv_a.astype(BF16), dv_b], axis=1)
            tn_proj = 768
        else:
            g_o_o[li] = g_out
            dq_f, dk_f, dv_f, dfk = causal_bwd(kp["qkv"], dattn, kp["st"], 8, "fox", n + "_fox_bwd", fq=kp["fq"], fk=kp["fk"])
            dcum = jnp.pad(dfk.reshape(N_HEADS, s).T, ((0, 0), (0, LANES - N_HEADS)))
            dfl, dbias = forget_bwd(kp["fl"], bias_pad[li:li + 1], dcum, n + "_forget_bwd")
            g_bias[li] = dbias[0, :N_HEADS]
            dproj = jnp.concatenate([dq_f, dk_f.astype(BF16), dv_f.astype(BF16), dfl.astype(BF16)], axis=1)
            tn_proj = 640
        g_in = matmul(kp["h1"], dproj, "tn", F32, n + "_d_w_qkv", 1024, tn_proj, 2048)
        dh1 = matmul(dproj, w_in, "nt", F32, n + "_d_h1", 512, 1024, dproj.shape[1] // 5 if layer % 2 else 1024)
        if layer % 2 == 0:
            g_qkv_e[li] = g_in
        else:
            g_qkvf[li] = g_in[:, :3 * D_ATTN + N_HEADS]
        dcur, g_mix[layer] = rmsnorm_bwd(kp["x"], norm_mix[layer:layer + 1], dh1, dmid, n + "_d_norm_mix")

    fam_grads = [jnp.stack(g_qkv_e), jnp.stack(g_o_e), jnp.stack(g_qkvf), jnp.stack(g_o_o), jnp.stack(g_fi), jnp.stack(g_fo)]
    return dcur, fam_grads, (g_mix, g_ffn, g_final, g_bias), loss_part
```

```python
import functools

import jax
import jax.numpy as jnp
from jax import lax
from jax.experimental import pallas as pl
from jax.experimental.pallas import tpu as pltpu

F32 = jnp.float32
BF16 = jnp.bfloat16
MESH = pl.DeviceIdType.MESH

D_MODEL = 1024
DEPTH = 4
HEAD_DIM = 64
N_HEADS = 16
D_ATTN = 1024
D_FF = 2816
ROPE_THETA = 500000.0
ROT_HALF = 8
RMS_EPS = 1e-5
DIL_STRIDES = (1, 4, 16)
ADAM_LR, ADAM_B1, ADAM_B2, ADAM_EPS, ADAM_WD, ADAM_STEP = 0.001, 0.9, 0.999, 1e-8, 0.01, 10

LANES = 128
BLK = 128
VMEM_LIMIT = 56 * 1024 * 1024
NEG = -1e30
N_CHIPS = 4
FLAT_COLS = 1024
FLAT_ROWS = 12800
HALF_ROWS = FLAT_ROWS // 2
SMALL_ROWS = 16


def _params(sem=None):
    return pltpu.CompilerParams(dimension_semantics=sem, vmem_limit_bytes=VMEM_LIMIT)


def _dot(a, b):
    return lax.dot_general(a, b, (((1,), (0,)), ((), ())), preferred_element_type=F32)


def _dot_nt(a, b):
    return lax.dot_general(a, b, (((1,), (1,)), ((), ())), preferred_element_type=F32)


def _dot_tn(a, b):
    return lax.dot_general(a, b, (((0,), (0,)), ((), ())), preferred_element_type=F32)


def _split3(x):
    x1 = x.astype(BF16)
    r1 = x - x1.astype(F32)
    x2 = r1.astype(BF16)
    x3 = (r1 - x2.astype(F32)).astype(BF16)
    return x1, x2, x3


def _dot_exact_lhs(x, t):
    x1, x2, x3 = _split3(x)
    return _dot(x1, t) + _dot(x2, t) + _dot(x3, t)


def _dot_exact_rhs(t, x):
    x1, x2, x3 = _split3(x)
    return _dot(t, x1) + _dot(t, x2) + _dot(t, x3)


def _iotas(shape=(BLK, LANES)):
    return lax.broadcasted_iota(jnp.int32, shape, 0), lax.broadcasted_iota(jnp.int32, shape, 1)


_DIMS = {"nn": (((1,), (0,)), ((), ())), "nt": (((1,), (1,)), ((), ())), "tn": (((0,), (0,)), ((), ()))}


def matmul(a, b, mode, out_dtype, name, tm, tn, tk, res=None):
    if mode == "nn":
        (m, k), (k2, n) = a.shape, b.shape
    elif mode == "nt":
        (m, k), (n, k2) = a.shape, b.shape
    else:
        (k, m), (k2, n) = a.shape, b.shape
    assert k == k2 and m % tm == 0 and n % tn == 0 and k % tk == 0, (name, a.shape, b.shape)
    nk = k // tk
    a_spec = pl.BlockSpec((tk, tm), lambda i, j, kk: (kk, i)) if mode == "tn" else pl.BlockSpec((tm, tk), lambda i, j, kk: (i, kk))
    b_spec = pl.BlockSpec((tn, tk), lambda i, j, kk: (j, kk)) if mode == "nt" else pl.BlockSpec((tk, tn), lambda i, j, kk: (kk, j))
    o_spec = pl.BlockSpec((tm, tn), lambda i, j, kk: (i, j))
    dims = _DIMS[mode]
    has_res = res is not None

    def body(*refs):
        a_ref, b_ref = refs[0], refs[1]
        r_ref = refs[2] if has_res else None
        o_ref = refs[3] if has_res else refs[2]

        def finish(v):
            if has_res:
                v = v + r_ref[...]
            o_ref[...] = v.astype(out_dtype)

        p = lax.dot_general(a_ref[...].astype(BF16), b_ref[...].astype(BF16), dims, preferred_element_type=F32)
        if nk == 1:
            finish(p)
        else:
            acc = refs[-1]
            kk = pl.program_id(2)

            @pl.when(kk == 0)
            def _():
                acc[...] = p

            @pl.when(kk > 0)
            def _():
                acc[...] += p

            @pl.when(kk == nk - 1)
            def _():
                finish(acc[...])

    ops = [a, b] + ([res] if has_res else [])
    specs = [a_spec, b_spec] + ([o_spec] if has_res else [])
    return pl.pallas_call(
        body, name=name, out_shape=jax.ShapeDtypeStruct((m, n), out_dtype),
        grid=(m // tm, n // tn, nk), in_specs=specs, out_specs=o_spec,
        scratch_shapes=[pltpu.VMEM((tm, tn), F32)] if nk > 1 else [],
        compiler_params=_params(("parallel", "parallel", "arbitrary")),
    )(*ops)


ROWS = 256


def _row_spec(cols, rows=ROWS):
    return pl.BlockSpec((rows, cols), lambda i: (i, 0))


def _fix_spec(r, cols):
    return pl.BlockSpec((r, cols), lambda i: (0, 0))


def rmsnorm_fwd(x, g, name):
    s, d = x.shape

    def body(x_ref, g_ref, h_ref):
        xv = x_ref[...]
        rstd = lax.rsqrt(jnp.mean(xv * xv, axis=-1, keepdims=True) + RMS_EPS)
        h_ref[...] = (xv * rstd * g_ref[...]).astype(BF16)

    return pl.pallas_call(
        body, name=name, out_shape=jax.ShapeDtypeStruct((s, d), BF16), grid=(s // ROWS,),
        in_specs=[_row_spec(d), _fix_spec(1, d)], out_specs=_row_spec(d), compiler_params=_params(("parallel",)),
    )(x, g)


def _rms_bwd_math(xv, gv, dh):
    rstd = lax.rsqrt(jnp.mean(xv * xv, axis=-1, keepdims=True) + RMS_EPS)
    xhat = xv * rstd
    u = dh * gv
    dx = rstd * (u - xhat * jnp.mean(u * xhat, axis=-1, keepdims=True))
    return dx, dh * xhat


def rmsnorm_bwd(x, g, dh, dres, name):
    s, d = x.shape

    def body(x_ref, g_ref, dh_ref, dres_ref, dx_ref, dg_ref):
        dx, dgt = _rms_bwd_math(x_ref[...], g_ref[...], dh_ref[...])
        dx_ref[...] = dres_ref[...] + dx
        part = jnp.sum(dgt, axis=0, keepdims=True)

        @pl.when(pl.program_id(0) == 0)
        def _():
            dg_ref[...] = part

        @pl.when(pl.program_id(0) > 0)
        def _():
            dg_ref[...] += part

    return pl.pallas_call(
        body, name=name, out_shape=(jax.ShapeDtypeStruct((s, d), F32), jax.ShapeDtypeStruct((1, d), F32)),
        grid=(s // ROWS,), in_specs=[_row_spec(d), _fix_spec(1, d), _row_spec(d), _row_spec(d)],
        out_specs=(_row_spec(d), _fix_spec(1, d)), compiler_params=_params(("arbitrary",)),
    )(x, g, dh, dres)


def loss_head(x, g, target, name):
    s, d = x.shape

    def body(x_ref, g_ref, t_ref, dx_ref, dg_ref, loss_ref):
        xv, gv = x_ref[...], g_ref[...]
        rstd = lax.rsqrt(jnp.mean(xv * xv, axis=-1, keepdims=True) + RMS_EPS)
        err = xv * rstd * gv - t_ref[...]
        dx, dgt = _rms_bwd_math(xv, gv, err * (1.0 / d))
        dx_ref[...] = dx
        part = jnp.sum(dgt, axis=0, keepdims=True)
        lpart = jnp.full((1, LANES), 0.5 / d, F32) * jnp.sum(err * err)

        @pl.when(pl.program_id(0) == 0)
        def _():
            dg_ref[...] = part
            loss_ref[...] = lpart

        @pl.when(pl.program_id(0) > 0)
        def _():
            dg_ref[...] += part
            loss_ref[...] += lpart

    return pl.pallas_call(
        body, name=name,
        out_shape=(jax.ShapeDtypeStruct((s, d), F32), jax.ShapeDtypeStruct((1, d), F32), jax.ShapeDtypeStruct((1, LANES), F32)),
        grid=(s // ROWS,), in_specs=[_row_spec(d), _fix_spec(1, d), _row_spec(d)],
        out_specs=(_row_spec(d), _fix_spec(1, d), _fix_spec(1, LANES)), compiler_params=_params(("arbitrary",)),
    )(x, g, target)


def swiglu_fwd(gu, name):
    s, f2 = gu.shape
    f = f2 // 2

    def body(gu_ref, a_ref):
        gv, uv = gu_ref[:, :f], gu_ref[:, f:]
        a_ref[...] = (gv * (1.0 / (1.0 + jnp.exp(-gv))) * uv).astype(BF16)

    return pl.pallas_call(
        body, name=name, out_shape=jax.ShapeDtypeStruct((s, f), BF16), grid=(s // ROWS,),
        in_specs=[_row_spec(f2)], out_specs=_row_spec(f), compiler_params=_params(("parallel",)),
    )(gu)


def swiglu_bwd(gu, dact, name):
    s, f2 = gu.shape
    f = f2 // 2

    def body(gu_ref, da_ref, o_ref):
        gv, uv, da = gu_ref[:, :f], gu_ref[:, f:], da_ref[...]
        sg = 1.0 / (1.0 + jnp.exp(-gv))
        o_ref[:, :f] = (da * uv * sg * (1.0 + gv * (1.0 - sg))).astype(BF16)
        o_ref[:, f:] = (da * gv * sg).astype(BF16)

    return pl.pallas_call(
        body, name=name, out_shape=jax.ShapeDtypeStruct((s, f2), BF16), grid=(s // ROWS,),
        in_specs=[_row_spec(f2), _row_spec(f)], out_specs=_row_spec(f2), compiler_params=_params(("parallel",)),
    )(gu, dact)


Q_OFF, K_OFF, V_OFF = 0, 8, 16


KB = 512
SUB = KB // BLK


def _softplus_parts(z):
    sp = jnp.log(1.0 + jnp.exp(-jnp.abs(z)))
    ls = jnp.minimum(z, 0.0) - sp
    return ls, ls - z


def _wide(t):
    return jnp.concatenate([t] * SUB, axis=1)


def _chunk_dots(x, tri):
    terms = []
    for u in range(SUB):
        terms += list(_split3(x[:, u * BLK:(u + 1) * BLK]))
    r = _dot(jnp.concatenate(terms, axis=0), tri)
    piece = lambda n: r[n * BLK:(n + 1) * BLK]
    return [piece(3 * u) + piece(3 * u + 1) + piece(3 * u + 2) for u in range(SUB)]


def _block_suffix_sums(x, suffix, c):
    loc = _chunk_dots(x, suffix)
    out = [None] * SUB
    for u in reversed(range(SUB)):
        out[u] = loc[u] + c
        c = c + jnp.sum(x[:, u * BLK:(u + 1) * BLK], axis=1, keepdims=True)
    return jnp.concatenate(out, axis=1), c


def _block_prefix_sums(x, tri, c):
    loc = _chunk_dots(x, tri)
    out = []
    for u in range(SUB):
        out.append(loc[u] + c)
        c = c + jnp.sum(x[:, u * BLK:(u + 1) * BLK], axis=1, keepdims=True)
    return jnp.concatenate(out, axis=1), c


def causal_fwd(qkv, npairs, mode, name, fq=None, fk=None):
    s = qkv.shape[0]
    nq = s // BLK
    fox = mode == "fox"

    def body(*refs):
        if fox:
            q_ref, k_ref, v_ref, fq_ref, fk_ref, o_ref, st_ref = refs
        else:
            q_ref, k_ref, v_ref, o_ref, st_ref = refs
        i = pl.program_id(1)
        nkb = i // SUB + 1
        row, lane = _iotas((BLK, KB))
        row_s, lane_s = _iotas()
        qpos = i * BLK + row
        qf = q_ref[...].astype(F32)
        hms = (lane_s < HEAD_DIM, lane_s >= HEAD_DIM)
        qas = [jnp.where(hm, qf, 0.0).astype(BF16) for hm in hms]
        suffix = jnp.where(row_s > lane_s, 1.0, 0.0).astype(BF16)
        zero = jnp.zeros((BLK, LANES), F32)
        col0 = jnp.zeros((BLK, 1), F32)

        def kv(j):
            r0 = pl.multiple_of(j * KB, KB)
            return r0, k_ref[pl.ds(r0, KB), :], v_ref[pl.ds(r0, KB), :]

        if fox:
            fqs = [_wide(fq_ref[a]) for a in range(2)]

            def step(j, carry):
                r0, kb, vb = kv(j)
                ok = r0 + lane <= qpos
                new = []
                for a in range(2):
                    acc, mx, l = carry[3 * a:3 * a + 3]
                    z = _dot_nt(qas[a], kb) * 0.125 + fqs[a] - fk_ref[a:a + 1, pl.ds(r0, KB)]
                    z = jnp.where(ok, z, NEG)
                    mnew = jnp.maximum(mx, jnp.max(z, axis=1, keepdims=True))
                    p = jnp.exp(z - mnew)
                    alpha = jnp.exp(mx - mnew)
                    new += [alpha * acc + _dot(p.astype(BF16), vb), mnew, alpha * l + jnp.sum(p, axis=1, keepdims=True)]
                return tuple(new)

            neg = jnp.full((BLK, 1), NEG, F32)
            res = lax.fori_loop(0, nkb, step, (zero, neg, col0, zero, neg, col0))
            outs = [res[3 * a] / res[3 * a + 2] for a in range(2)]
            stats = [res[3 * a + 1] + jnp.log(res[3 * a + 2]) for a in range(2)]
        else:
            def step(jj, carry):
                r0, kb, vb = kv(nkb - 1 - jj)
                strict = r0 + lane < qpos
                new = []
                for a in range(2):
                    acc, c = carry[2 * a:2 * a + 2]
                    ls, lm = _softplus_parts(_dot_nt(qas[a], kb) * 0.125)
                    lm = jnp.where(strict, lm, 0.0)
                    between, c = _block_suffix_sums(lm, suffix, c)
                    aw = jnp.where(strict, jnp.exp(ls + between), 0.0)
                    new += [acc + _dot(aw.astype(BF16), vb), c]
                return tuple(new)

            res = lax.fori_loop(0, nkb, step, (zero, col0, zero, col0))
            outs, stats = [res[0], res[2]], [res[1], res[3]]
        o_ref[...] = jnp.where(hms[0], outs[0], outs[1])
        for a in range(2):
            st_ref[a] = jnp.broadcast_to(stats[a], (BLK, LANES))

    col = lambda off: (lambda p, i: (0, off + p))
    in_specs = [pl.BlockSpec((BLK, LANES), lambda p, i: (i, Q_OFF + p)),
                pl.BlockSpec((s, LANES), col(K_OFF)), pl.BlockSpec((s, LANES), col(V_OFF))]
    ops = [qkv, qkv, qkv]
    if fox:
        in_specs += [pl.BlockSpec((2, BLK, LANES), lambda p, i: (p, i, 0)), pl.BlockSpec((None, 2, s), lambda p, i: (p, 0, 0))]
        ops += [fq, fk]
    return pl.pallas_call(
        body, name=name,
        out_shape=(jax.ShapeDtypeStruct((s, npairs * LANES), F32), jax.ShapeDtypeStruct((2 * npairs, s, LANES), F32)),
        grid=(npairs, nq), in_specs=in_specs,
        out_specs=(pl.BlockSpec((BLK, LANES), lambda p, i: (i, p)), pl.BlockSpec((2, BLK, LANES), lambda p, i: (p, i, 0))),
        compiler_params=_params(("parallel", "arbitrary")),
    )(*ops)


def causal_bwd(qkv, do, stat, npairs, mode, name, fq=None, fk=None):
    s = qkv.shape[0]
    nq = s // BLK
    fox = mode == "fox"

    def body(*refs):
        if fox:
            q_ref, k_ref, v_ref, do_ref, st_ref, fq_ref, fk_ref, dq_ref, dk_ref, dv_ref, df_ref, p_s, dp_s = refs
        else:
            q_ref, k_ref, v_ref, do_ref, st_ref, dq_ref, dk_ref, dv_ref = refs
        i = pl.program_id(1)

        @pl.when(i == 0)
        def _():
            dk_ref[...] = jnp.zeros_like(dk_ref)
            dv_ref[...] = jnp.zeros_like(dv_ref)
            if fox:
                df_ref[...] = jnp.zeros_like(df_ref)

        nkb = i // SUB + 1
        row, lane = _iotas((BLK, KB))
        row_s, lane_s = _iotas()
        qpos = i * BLK + row
        qf = q_ref[...].astype(F32)
        dov = do_ref[...]
        hms = (lane_s < HEAD_DIM, lane_s >= HEAD_DIM)
        qas = [jnp.where(hm, qf, 0.0).astype(BF16) for hm in hms]
        doas = [jnp.where(hm, dov, 0.0).astype(BF16) for hm in hms]
        stas = [_wide(st_ref[a]) for a in range(2)]
        zero = jnp.zeros((BLK, LANES), F32)
        col0 = jnp.zeros((BLK, 1), F32)

        def kv(j):
            r0 = pl.multiple_of(j * KB, KB)
            return r0, k_ref[pl.ds(r0, KB), :], v_ref[pl.ds(r0, KB), :]

        if fox:
            fqs = [_wide(fq_ref[a]) for a in range(2)]

            def probs(j, deltas):
                r0, kb, vb = kv(j)
                ok = r0 + lane <= qpos
                new = []
                for a in range(2):
                    z = _dot_nt(qas[a], kb) * 0.125 + fqs[a] - fk_ref[a:a + 1, pl.ds(r0, KB)]
                    p = jnp.where(ok, jnp.exp(z - stas[a]), 0.0)
                    dp = _dot_nt(doas[a], vb)
                    p_s[a, j] = p
                    dp_s[a, j] = dp
                    new.append(deltas[a] + jnp.sum(p * dp, axis=1, keepdims=True))
                return tuple(new)

            deltas = lax.fori_loop(0, nkb, probs, (col0, col0))

            def step(j, dqs):
                r0, kb, _ = kv(j)
                new = []
                dk = jnp.zeros((KB, LANES), F32)
                dv = jnp.zeros((KB, LANES), F32)
                for a in range(2):
                    p = p_s[a, j]
                    ds = p * (dp_s[a, j] - deltas[a])
                    dsb = (ds * 0.125).astype(BF16)
                    dk += _dot_tn(dsb, qas[a])
                    dv += _dot_tn(p.astype(BF16), doas[a])
                    df_ref[a:a + 1, pl.ds(r0, KB)] -= jnp.sum(ds, axis=0, keepdims=True)
                    new.append(dqs[a] + _dot(dsb, kb))
                dk_ref[pl.ds(r0, KB), :] += dk
                dv_ref[pl.ds(r0, KB), :] += dv
                return tuple(new)

            dqs = lax.fori_loop(0, nkb, step, (zero, zero))
        else:
            incl = jnp.where(row_s <= lane_s, 1.0, 0.0).astype(BF16)
            excl = jnp.where(row_s < lane_s, 1.0, 0.0).astype(BF16)

            def step(j, carry):
                r0, kb, vb = kv(j)
                strict = r0 + lane < qpos
                new = []
                dk = jnp.zeros((KB, LANES), F32)
                dv = jnp.zeros((KB, LANES), F32)
                for a in range(2):
                    dq, cm, cg = carry[3 * a:3 * a + 3]
                    ls, lm = _softplus_parts(_dot_nt(qas[a], kb) * 0.125)
                    lm = jnp.where(strict, lm, 0.0)
                    beta = jnp.exp(ls)
                    upto, cm = _block_prefix_sums(lm, incl, cm)
                    aw = jnp.where(strict, jnp.exp(ls + stas[a] - upto), 0.0)
                    g = aw * _dot_nt(doas[a], vb)
                    pre, cg = _block_prefix_sums(g, excl, cg)
                    dz = jnp.where(strict, g * (1.0 - beta) - pre * beta, 0.0)
                    dzb = (dz * 0.125).astype(BF16)
                    dk += _dot_tn(dzb, qas[a])
                    dv += _dot_tn(aw.astype(BF16), doas[a])
                    new += [dq + _dot(dzb, kb), cm, cg]
                dk_ref[pl.ds(r0, KB), :] += dk
                dv_ref[pl.ds(r0, KB), :] += dv
                return tuple(new)

            res = lax.fori_loop(0, nkb, step, (zero, col0, col0, zero, col0, col0))
            dqs = (res[0], res[3])
        dq_ref[...] = jnp.where(hms[0], dqs[0], dqs[1]).astype(BF16)

    col = lambda off: (lambda p, i: (0, off + p))
    blk = pl.BlockSpec((BLK, LANES), lambda p, i: (i, p))
    acc = pl.BlockSpec((s, LANES), lambda p, i: (0, p))
    st_spec = pl.BlockSpec((2, BLK, LANES), lambda p, i: (p, i, 0))
    in_specs = [pl.BlockSpec((BLK, LANES), lambda p, i: (i, Q_OFF + p)), pl.BlockSpec((s, LANES), col(K_OFF)),
                pl.BlockSpec((s, LANES), col(V_OFF)), blk, st_spec]
    ops = [qkv, qkv, qkv, do, stat]
    w = npairs * LANES
    out_shape = [jax.ShapeDtypeStruct((s, w), BF16), jax.ShapeDtypeStruct((s, w), F32), jax.ShapeDtypeStruct((s, w), F32)]
    out_specs = [blk, acc, acc]
    scratch = []
    if fox:
        fk_spec = pl.BlockSpec((None, 2, s), lambda p, i: (p, 0, 0))
        in_specs += [st_spec, fk_spec]
        ops += [fq, fk]
        out_shape.append(jax.ShapeDtypeStruct((npairs, 2, s), F32))
        out_specs.append(fk_spec)
        scratch = [pltpu.VMEM((2, s // KB, BLK, KB), F32)] * 2
    return pl.pallas_call(
        body, name=name, out_shape=tuple(out_shape), grid=(npairs, nq), in_specs=in_specs, out_specs=tuple(out_specs),
        scratch_shapes=scratch, compiler_params=_params(("parallel", "arbitrary")),
    )(*ops)


def forget_fwd(fl, bias, name):
    s = fl.shape[0]

    def body(fl_ref, b_ref, f_ref):
        row, lane = _iotas()
        lower = jnp.where(lane <= row, 1.0, 0.0).astype(BF16)

        def step(n, carry):
            r0 = pl.multiple_of(n * BLK, BLK)
            ls, _ = _softplus_parts(fl_ref[pl.ds(r0, BLK), :] + b_ref[...])
            blk = _dot_exact_rhs(lower, ls) + carry
            f_ref[pl.ds(r0, BLK), :] = blk
            return blk[BLK - 1:BLK, :]

        lax.fori_loop(0, s // BLK, step, jnp.zeros((1, LANES), F32))

    return pl.pallas_call(
        body, name=name, out_shape=jax.ShapeDtypeStruct((s, LANES), F32),
        in_specs=[pl.BlockSpec(memory_space=pltpu.VMEM)] * 2, out_specs=pl.BlockSpec(memory_space=pltpu.VMEM),
        compiler_params=_params(),
    )(fl, bias)


def forget_bwd(fl, bias, df, name):
    s = fl.shape[0]
    nb = s // BLK

    def body(fl_ref, b_ref, df_ref, o_ref, db_ref):
        row, lane = _iotas()
        upper = jnp.where(lane >= row, 1.0, 0.0).astype(BF16)

        def step(nn, carry):
            tail, db = carry
            r0 = pl.multiple_of((nb - 1 - nn) * BLK, BLK)
            dls = _dot_exact_rhs(upper, df_ref[pl.ds(r0, BLK), :]) + tail
            xv = fl_ref[pl.ds(r0, BLK), :] + b_ref[...]
            dfl = dls * (1.0 / (1.0 + jnp.exp(xv)))
            o_ref[pl.ds(r0, BLK), :] = dfl
            return dls[0:1, :], db + jnp.sum(dfl, axis=0, keepdims=True)

        _, db = lax.fori_loop(0, nb, step, (jnp.zeros((1, LANES), F32), jnp.zeros((1, LANES), F32)))
        db_ref[...] = db

    return pl.pallas_call(
        body, name=name, out_shape=(jax.ShapeDtypeStruct((s, LANES), F32), jax.ShapeDtypeStruct((1, LANES), F32)),
        in_specs=[pl.BlockSpec(memory_space=pltpu.VMEM)] * 3,
        out_specs=(pl.BlockSpec(memory_space=pltpu.VMEM), pl.BlockSpec(memory_space=pltpu.VMEM)),
        compiler_params=_params(),
    )(fl, bias, df)


def _rot_tables(s):
    inv = ROPE_THETA ** (-jnp.arange(ROT_HALF, dtype=F32) * 2.0 / (2 * ROT_HALF))
    ang = jnp.arange(s, dtype=F32)[:, None] * inv[None, :]
    cos, sin = jnp.cos(ang), jnp.sin(ang)
    z8 = jnp.zeros((s, ROT_HALF), F32)
    rest = HEAD_DIM - 2 * ROT_HALF
    zr, onr = jnp.zeros((s, rest), F32), jnp.ones((s, rest), F32)
    tile = lambda t: jnp.tile(t, (1, 2))
    return tile(jnp.concatenate([cos, cos, onr], 1)), tile(jnp.concatenate([-sin, z8, zr], 1)), tile(jnp.concatenate([z8, sin, zr], 1))


def rotary_prep(qkv, tables, name):
    s = qkv.shape[0]
    w = 4 * LANES

    def body(q_ref, k_ref, v_ref, c_ref, s1_ref, s2_ref, qo_ref, ko_ref, vo_ref):
        c, s1, s2 = c_ref[...], s1_ref[...], s2_ref[...]

        def rot(xv):
            return xv * c + pltpu.roll(xv, LANES - ROT_HALF, 1) * s1 + pltpu.roll(xv, ROT_HALF, 1) * s2

        qo_ref[...] = rot(q_ref[...].astype(F32)) * 0.125
        ko_ref[...] = rot(k_ref[...].astype(F32))
        vo_ref[...] = v_ref[...].astype(F32)

    cb = lambda off: pl.BlockSpec((ROWS, LANES), lambda i, j: (i, off + j))
    tb = pl.BlockSpec((ROWS, LANES), lambda i, j: (i, 0))
    out = jax.ShapeDtypeStruct((s, w), F32)
    return pl.pallas_call(
        body, name=name, out_shape=(out, out, out), grid=(s // ROWS, 4),
        in_specs=[cb(Q_OFF + 4), cb(K_OFF + 4), cb(V_OFF + 4), tb, tb, tb], out_specs=(cb(0), cb(0), cb(0)),
        compiler_params=_params(("parallel", "parallel")),
    )(qkv, qkv, qkv, *tables)


def rotary_bwd(dq, dk, dv, tables, name):
    s, w = dq.shape

    def body(dq_ref, dk_ref, dv_ref, c_ref, s1_ref, s2_ref, qo_ref, ko_ref, vo_ref):
        c, s1, s2 = c_ref[...], s1_ref[...], s2_ref[...]

        def rot_t(dy):
            return dy * c + pltpu.roll(dy * s1, ROT_HALF, 1) + pltpu.roll(dy * s2, LANES - ROT_HALF, 1)

        qo_ref[...] = (rot_t(dq_ref[...]) * 0.125).astype(BF16)
        ko_ref[...] = rot_t(dk_ref[...]).astype(BF16)
        vo_ref[...] = dv_ref[...].astype(BF16)

    cb = pl.BlockSpec((ROWS, LANES), lambda i, j: (i, j))
    tb = pl.BlockSpec((ROWS, LANES), lambda i, j: (i, 0))
    out = jax.ShapeDtypeStruct((s, w), BF16)
    return pl.pallas_call(
        body, name=name, out_shape=(out, out, out), grid=(s // ROWS, w // LANES),
        in_specs=[cb, cb, cb, tb, tb, tb], out_specs=(cb, cb, cb), compiler_params=_params(("parallel", "parallel")),
    )(dq, dk, dv, *tables)


def _deinterleave(dst, src_ref, stride, s, dtype):
    length = s // stride
    for r in range(stride):
        if stride == 1:
            dst[...] = src_ref[...].astype(dtype)
        else:
            dst[r * length:(r + 1) * length, :] = src_ref[pl.ds(r, length, stride=stride), :].astype(dtype)


def _band_masks(row, lane, first):
    return lane <= row, lane >= row + jnp.where(first, BLK, 0)


def dilated_fwd(qd, kd, vd, name):
    s, w = qd.shape
    npairs = w // LANES
    nblk = s // BLK

    def body(q_ref, k_ref, v_ref, o_ref, lse_ref, qs, ks, vs, od, ld, on, ln):
        row, lane = _iotas()
        for pi, stride in enumerate(DIL_STRIDES):
            per = (s // stride) // BLK
            _deinterleave(qs, q_ref, stride, s, BF16)
            _deinterleave(ks, k_ref, stride, s, BF16)
            _deinterleave(vs, v_ref, stride, s, BF16)

            def block(b, carry):
                r0 = pl.multiple_of(b * BLK, BLK)
                rp = pl.multiple_of(jnp.maximum(b - 1, 0) * BLK, BLK)
                mc, mp = _band_masks(row, lane, b % per == 0)
                q = qs[pl.ds(r0, BLK), :]
                kc, kp, vc, vp = ks[pl.ds(r0, BLK), :], ks[pl.ds(rp, BLK), :], vs[pl.ds(r0, BLK), :], vs[pl.ds(rp, BLK), :]
                out = jnp.zeros((BLK, LANES), F32)
                lse = jnp.zeros((BLK, LANES), F32)
                for a in range(2):
                    hm = (lane < HEAD_DIM) if a == 0 else (lane >= HEAD_DIM)
                    qa = jnp.where(hm, q.astype(F32), 0.0).astype(BF16)
                    sc = jnp.where(mc, _dot_nt(qa, kc), NEG)
                    sp = jnp.where(mp, _dot_nt(qa, kp), NEG)
                    mx = jnp.maximum(jnp.max(sc, axis=1, keepdims=True), jnp.max(sp, axis=1, keepdims=True))
                    pc, pp = jnp.exp(sc - mx), jnp.exp(sp - mx)
                    l = jnp.sum(pc, axis=1, keepdims=True) + jnp.sum(pp, axis=1, keepdims=True)
                    oa = (_dot(pc.astype(BF16), vc) + _dot(pp.astype(BF16), vp)) / l
                    out = jnp.where(hm, oa, out)
                    lse = jnp.where(hm, mx + jnp.log(l), lse)
                od[pl.ds(r0, BLK), :] = out
                ld[pl.ds(r0, BLK), :] = lse
                return carry

            lax.fori_loop(0, nblk, block, 0)
            length = s // stride
            for r in range(stride):
                if stride == 1:
                    on[pi] = od[...]
                    ln[pi] = ld[...]
                else:
                    on[pi, pl.ds(r, length, stride=stride), :] = od[r * length:(r + 1) * length, :]
                    ln[pi, pl.ds(r, length, stride=stride), :] = ld[r * length:(r + 1) * length, :]

        def merge(n, carry):
            r0 = pl.multiple_of(n * BLK, BLK)
            ls = [ln[pi, pl.ds(r0, BLK), :] for pi in range(3)]
            mx = jnp.maximum(jnp.maximum(ls[0], ls[1]), ls[2])
            ws = [jnp.exp(lv - mx) for lv in ls]
            den = ws[0] + ws[1] + ws[2]
            num = ws[0] * on[0, pl.ds(r0, BLK), :] + ws[1] * on[1, pl.ds(r0, BLK), :] + ws[2] * on[2, pl.ds(r0, BLK), :]
            o_ref[pl.ds(r0, BLK), :] = num / den
            lse_ref[pl.ds(r0, BLK), :] = mx + jnp.log(den)
            return carry

        lax.fori_loop(0, nblk, merge, 0)

    colspec = pl.BlockSpec((s, LANES), lambda p: (0, p))
    out = jax.ShapeDtypeStruct((s, w), F32)
    return pl.pallas_call(
        body, name=name, out_shape=(out, out), grid=(npairs,), in_specs=[colspec] * 3, out_specs=(colspec, colspec),
        scratch_shapes=[pltpu.VMEM((s, LANES), BF16)] * 3 + [pltpu.VMEM((s, LANES), F32)] * 2 + [pltpu.VMEM((3, s, LANES), F32)] * 2,
        compiler_params=_params(("parallel",)),
    )(qd, kd, vd)


def dilated_bwd(qd, kd, vd, do, out, lse, do_off, name):
    s, w = qd.shape
    npairs = w // LANES
    nblk = s // BLK

    def body(q_ref, k_ref, v_ref, do_ref, out_ref, lse_ref, dq_ref, dk_ref, dv_ref, qs, ks, vs, dos, dls, lss, dqd, dkd, dvd, dln):
        row, lane = _iotas()
        same_head = jnp.where((row < HEAD_DIM) == (lane < HEAD_DIM), 1.0, 0.0).astype(BF16)

        def delta_blk(n, carry):
            r0 = pl.multiple_of(n * BLK, BLK)
            dln[pl.ds(r0, BLK), :] = _dot_exact_lhs(do_ref[pl.ds(r0, BLK), :] * out_ref[pl.ds(r0, BLK), :], same_head)
            return carry

        lax.fori_loop(0, nblk, delta_blk, 0)
        for pi, stride in enumerate(DIL_STRIDES):
            per = (s // stride) // BLK
            _deinterleave(qs, q_ref, stride, s, BF16)
            _deinterleave(ks, k_ref, stride, s, BF16)
            _deinterleave(vs, v_ref, stride, s, BF16)
            _deinterleave(dos, do_ref, stride, s, BF16)
            _deinterleave(dls, dln, stride, s, F32)
            _deinterleave(lss, lse_ref, stride, s, F32)

            def block(b, carry):
                r0 = pl.multiple_of(b * BLK, BLK)
                rp = pl.multiple_of(jnp.maximum(b - 1, 0) * BLK, BLK)
                first = b % per == 0
                mc, mp = _band_masks(row, lane, first)
                q, dov = qs[pl.ds(r0, BLK), :], dos[pl.ds(r0, BLK), :]
                kc, kp, vc, vp = ks[pl.ds(r0, BLK), :], ks[pl.ds(rp, BLK), :], vs[pl.ds(r0, BLK), :], vs[pl.ds(rp, BLK), :]
                lse_t, dl_t = lss[pl.ds(r0, BLK), :], dls[pl.ds(r0, BLK), :]
                dq = jnp.zeros((BLK, LANES), F32)
                dkc = jnp.zeros((BLK, LANES), F32)
                dkp = jnp.zeros((BLK, LANES), F32)
                dvc = jnp.zeros((BLK, LANES), F32)
                dvp = jnp.zeros((BLK, LANES), F32)
                for a in range(2):
                    hm = (lane < HEAD_DIM) if a == 0 else (lane >= HEAD_DIM)
                    pick = lane == a * HEAD_DIM
                    qa = jnp.where(hm, q.astype(F32), 0.0).astype(BF16)
                    doa = jnp.where(hm, dov.astype(F32), 0.0).astype(BF16)
                    lse_a = jnp.sum(jnp.where(pick, lse_t, 0.0), axis=1, keepdims=True)
                    dl_a = jnp.sum(jnp.where(pick, dl_t, 0.0), axis=1, keepdims=True)
                    pc = jnp.where(mc, jnp.exp(_dot_nt(qa, kc) - lse_a), 0.0)
                    pp = jnp.where(mp, jnp.exp(_dot_nt(qa, kp) - lse_a), 0.0)
                    dsc = (pc * (_dot_nt(doa, vc) - dl_a)).astype(BF16)
                    dsp = (pp * (_dot_nt(doa, vp) - dl_a)).astype(BF16)
                    dq = jnp.where(hm, _dot(dsc, kc) + _dot(dsp, kp), dq)
                    dkc += _dot_tn(dsc, qa)
                    dkp += _dot_tn(dsp, qa)
                    dvc += _dot_tn(pc.astype(BF16), doa)
                    dvp += _dot_tn(pp.astype(BF16), doa)
                dqd[pl.ds(r0, BLK), :] = dq
                dkd[pl.ds(r0, BLK), :] = dkc
                dvd[pl.ds(r0, BLK), :] = dvc

                @pl.when(jnp.logical_not(first))
                def _():
                    dkd[pl.ds(rp, BLK), :] += dkp
                    dvd[pl.ds(rp, BLK), :] += dvp

                return carry

            lax.fori_loop(0, nblk, block, 0)
            length = s // stride
            for dst, src in ((dq_ref, dqd), (dk_ref, dkd), (dv_ref, dvd)):
                for r in range(stride):
                    if stride == 1:
                        dst[...] = src[...]
                    else:
                        dst[pl.ds(r, length, stride=stride), :] += src[r * length:(r + 1) * length, :]

    colspec = pl.BlockSpec((s, LANES), lambda p: (0, p))
    do_spec = pl.BlockSpec((s, LANES), lambda p: (0, do_off + p))
    o3 = jax.ShapeDtypeStruct((s, w), F32)
    return pl.pallas_call(
        body, name=name, out_shape=(o3, o3, o3), grid=(npairs,),
        in_specs=[colspec, colspec, colspec, do_spec, colspec, colspec], out_specs=(colspec, colspec, colspec),
        scratch_shapes=[pltpu.VMEM((s, LANES), BF16)] * 4 + [pltpu.VMEM((s, LANES), F32)] * 6,
        compiler_params=_params(("parallel",)),
    )(qd, kd, vd, do, out, lse)


def adamw(w, g, m, v, name):
    rows, cols = w.shape
    rb = min(rows, ROWS)
    c1 = 1.0 - ADAM_B1 ** ADAM_STEP
    c2 = 1.0 - ADAM_B2 ** ADAM_STEP

    def body(w_ref, g_ref, m_ref, v_ref, d_ref, mo_ref, vo_ref):
        gv = g_ref[...]
        mn = ADAM_B1 * m_ref[...] + (1.0 - ADAM_B1) * gv
        vn = ADAM_B2 * v_ref[...] + (1.0 - ADAM_B2) * (gv * gv)
        d_ref[...] = -ADAM_LR * ((mn / c1) / (jnp.sqrt(vn / c2) + ADAM_EPS) + ADAM_WD * w_ref[...])
        mo_ref[...] = mn
        vo_ref[...] = vn

    spec = _row_spec(cols, rb)
    out = jax.ShapeDtypeStruct((rows, cols), F32)
    return pl.pallas_call(
        body, name=name, out_shape=(out, out, out), grid=(rows // rb,), in_specs=[spec] * 4, out_specs=(spec,) * 3,
        compiler_params=_params(("parallel",)),
    )(w, g, m, v)


def sum_rows(parts, out_dtype, name):
    rows, cols = parts[0].shape
    n = len(parts)

    def body(*refs):
        acc = refs[0][...].astype(F32)
        for r in refs[1:n]:
            acc = acc + r[...].astype(F32)
        refs[n][...] = acc.astype(out_dtype)

    spec = _row_spec(cols)
    return pl.pallas_call(
        body, name=name, out_shape=jax.ShapeDtypeStruct((rows, cols), out_dtype), grid=(rows // ROWS,),
        in_specs=[spec] * n, out_specs=spec, compiler_params=_params(("parallel",)),
    )(*parts)


def _coords():
    return lax.axis_index("x"), lax.axis_index("y"), lax.axis_index("c")


def _other_chips(x, y):
    return ((1 - x, y), (x, 1 - y), (1 - x, 1 - y))


_ANY = pl.BlockSpec(memory_space=pl.ANY)


def gather_weights(shard, name):
    def body(s_ref, o_ref, send_sems, recv_sems, local_sem):
        x, y, c = _coords()
        me = 2 * x + y
        sibling = (x, y, 1 - c)
        chips = _other_chips(x, y)

        def copy(k, src, dst, to):
            return pltpu.make_async_remote_copy(src_ref=src, dst_ref=dst, send_sem=send_sems.at[k], recv_sem=recv_sems.at[k],
                                                device_id=to, device_id_type=MESH)

        mine = pltpu.make_async_copy(s_ref, o_ref.at[me], local_sem)
        mine.start()
        first = [copy(k, s_ref.at[c], o_ref.at[me, c], (px, py, c)) for k, (px, py) in enumerate(chips)]
        for cp in first:
            cp.start()
        passed = []
        for k, (px, py) in enumerate(chips):
            landed = o_ref.at[2 * px + py, c]
            copy(k, s_ref.at[c], landed, (px, py, c)).wait_recv()
            fwd = copy(3 + k, landed, landed, sibling)
            fwd.start()
            passed.append(fwd)
        for k, (px, py) in enumerate(chips):
            other = o_ref.at[2 * px + py, 1 - c]
            copy(3 + k, other, other, sibling).wait_recv()
        for cp in first + passed:
            cp.wait_send()
        mine.wait()

    return pl.pallas_call(
        body, name=name, out_shape=jax.ShapeDtypeStruct((N_CHIPS,) + shard.shape, shard.dtype),
        in_specs=[_ANY], out_specs=_ANY,
        scratch_shapes=[pltpu.SemaphoreType.DMA((6,)), pltpu.SemaphoreType.DMA((6,)), pltpu.SemaphoreType.DMA],
        compiler_params=_params(),
    )(shard)


def swap_with_sibling(v, name):
    def body(v_ref, o_ref, send_sem, recv_sem):
        x, y, c = _coords()
        cp = pltpu.make_async_remote_copy(src_ref=v_ref, dst_ref=o_ref, send_sem=send_sem, recv_sem=recv_sem,
                                          device_id=(x, y, 1 - c), device_id_type=MESH)
        cp.start()
        cp.wait()

    return pl.pallas_call(
        body, name=name, out_shape=jax.ShapeDtypeStruct(v.shape, v.dtype), in_specs=[_ANY], out_specs=_ANY,
        scratch_shapes=[pltpu.SemaphoreType.DMA, pltpu.SemaphoreType.DMA], compiler_params=_params(),
    )(v)


def scatter_to_chips(h, name):
    def body(h_ref, o_ref, send_sems, recv_sems):
        x, y, c = _coords()
        cps = []
        for k, (px, py) in enumerate(_other_chips(x, y)):
            cp = pltpu.make_async_remote_copy(src_ref=h_ref.at[2 * px + py], dst_ref=o_ref.at[k], send_sem=send_sems.at[k],
                                              recv_sem=recv_sems.at[k], device_id=(px, py, c), device_id_type=MESH)
            cp.start()
            cps.append(cp)
        for cp in cps:
            cp.wait()

    return pl.pallas_call(
        body, name=name, out_shape=jax.ShapeDtypeStruct((3,) + h.shape[1:], h.dtype), in_specs=[_ANY], out_specs=_ANY,
        scratch_shapes=[pltpu.SemaphoreType.DMA((3,)), pltpu.SemaphoreType.DMA((3,))], compiler_params=_params(),
    )(h)


def allsum_small(part, name):
    def body(p_ref, tot_ref, all_ref, send_sems, recv_sems):
        x, y, c = _coords()
        me, sibling = (x, y, c), (x, y, 1 - c)
        chips = _other_chips(x, y)

        def slot(px, py, pc):
            return all_ref.at[4 * px + 2 * py + pc]

        def copy(k, block, to, src=None):
            return pltpu.make_async_remote_copy(src_ref=slot(*block) if src is None else src, dst_ref=slot(*block),
                                                send_sem=send_sems.at[k], recv_sem=recv_sems.at[k], device_id=to, device_id_type=MESH)

        slot(*me)[...] = p_ref[...]
        first = [copy(0, me, sibling, src=p_ref)] + [copy(1 + j, me, (*chip, c), src=p_ref) for j, chip in enumerate(chips)]
        for cp in first:
            cp.start()
        passed = [copy(4 + j, (*chip, c), sibling) for j, chip in enumerate(chips)]
        for j, chip in enumerate(chips):
            copy(1 + j, (*chip, c), me).wait_recv()
            passed[j].start()
        copy(0, sibling, me).wait_recv()
        for j, chip in enumerate(chips):
            copy(4 + j, (*chip, 1 - c), me).wait_recv()
        for cp in first + passed:
            cp.wait_send()
        tot = all_ref[0]
        for d in range(1, 8):
            tot = tot + all_ref[d]
        tot_ref[...] = tot

    vm = pl.BlockSpec(memory_space=pltpu.VMEM)
    return pl.pallas_call(
        body, name=name, out_shape=jax.ShapeDtypeStruct(part.shape, F32), in_specs=[vm], out_specs=vm,
        scratch_shapes=[pltpu.VMEM((8,) + part.shape, F32), pltpu.SemaphoreType.DMA((7,)), pltpu.SemaphoreType.DMA((7,))],
        compiler_params=_params(),
    )(part)


FAMILIES = (((2, 1024, 768), 2), ((2, 256, 1024), 1), ((2, 1024, 772), 2), ((2, 256, 1024), 1),
            ((4, 1024, 1408), 2), ((4, 704, 1024), 1))


def _size(shape):
    n = 1
    for d in shape:
        n *= d
    return n


def pack_rows(parts, dtype):
    flat = jnp.concatenate([p.reshape(-1).astype(dtype) for p in parts])
    return jnp.pad(flat, (0, FLAT_ROWS * FLAT_COLS - flat.shape[0])).reshape(FLAT_ROWS, FLAT_COLS)


def unpack_rows(rows):
    flat, out, at = rows.reshape(-1), [], 0
    for shape, _ in FAMILIES:
        out.append(flat[at:at + _size(shape)].reshape(shape))
        at += _size(shape)
    return out


def chip_slices(full, fam):
    shape, axis = FAMILIES[fam]
    width = shape[axis]
    return [lax.slice_in_dim(full, j * width, (j + 1) * width, axis=axis) for j in range(N_CHIPS)]


def _tables_for(s):
    return _rot_tables(s)


def gather_full(w_shards):
    gathered = gather_weights(pack_rows(w_shards, BF16).reshape(2, HALF_ROWS, FLAT_COLS), "gather_weights")
    per_chip = [unpack_rows(gathered[j]) for j in range(N_CHIPS)]
    return [jnp.concatenate([per_chip[j][f] for j in range(N_CHIPS)], axis=FAMILIES[f][1]) for f in range(len(FAMILIES))]


def reduce_grads(fam_grads):
    c = lax.axis_index("c")
    me = 2 * lax.axis_index("x") + lax.axis_index("y")
    by_chip = [[chip_slices(fam_grads[f], f)[j] for f in range(len(FAMILIES))] for j in range(N_CHIPS)]
    packed = jnp.stack([pack_rows(by_chip[j], BF16) for j in range(N_CHIPS)])
    keep_half = lax.dynamic_slice_in_dim(packed, c * HALF_ROWS, HALF_ROWS, axis=1)
    send_half = lax.dynamic_slice_in_dim(packed, (1 - c) * HALF_ROWS, HALF_ROWS, axis=1)
    from_sibling = swap_with_sibling(send_half, "grad_pair_swap")
    flat2 = lambda a: a.reshape(-1, FLAT_COLS)
    pair = sum_rows([flat2(keep_half), flat2(from_sibling)], BF16, "grad_pair_sum").reshape(N_CHIPS, HALF_ROWS, FLAT_COLS)
    from_chips = scatter_to_chips(pair, "grad_chip_scatter")
    mine = lax.dynamic_index_in_dim(pair, me, axis=0, keepdims=False)
    g_half = sum_rows([mine, from_chips[0], from_chips[1], from_chips[2]], F32, "grad_chip_sum")
    g_other = swap_with_sibling(g_half, "grad_half_swap")
    return jnp.concatenate([jnp.where(c == 0, g_half, g_other), jnp.where(c == 0, g_other, g_half)], axis=0)


def kernel(x, norm_mix, w_qkv_even, w_o_even, w_qkvf_odd, b_forget, w_o_odd, norm_ffn, w_ffn_in, w_ffn_out, norm_final, loss_target, m_norm_mix, m_w_qkv_even, m_w_o_even, m_w_qkvf_odd, m_b_forget, m_w_o_odd, m_norm_ffn, m_w_ffn_in, m_w_ffn_out, m_norm_final, v_norm_mix, v_w_qkv_even, v_w_o_even, v_w_qkvf_odd, v_b_forget, v_w_o_odd, v_norm_ffn, v_w_ffn_in, v_w_ffn_out, v_norm_final):
    w_shards = [w_qkv_even, w_o_even, w_qkvf_odd, w_o_odd, w_ffn_in, w_ffn_out]
    full = gather_full(w_shards)

    dcur, fam_grads, (g_mix, g_ffn, g_final, g_bias), loss_part = local_step(
        x[0], loss_target[0], norm_mix, norm_ffn, norm_final, b_forget, full)

    zero_row = jnp.zeros((1, D_MODEL), F32)
    pad16 = lambda v: jnp.pad(v, (0, D_MODEL - v.shape[0]))[None, :]
    small_rows = lambda mix, ffn, fin, bias, last: jnp.concatenate(
        [r.reshape(1, D_MODEL) for r in mix] + [r.reshape(1, D_MODEL) for r in ffn] + [fin.reshape(1, D_MODEL)]
        + [pad16(b) for b in bias] + [last] + [zero_row] * (SMALL_ROWS - 12), axis=0)
    loss_row = pad16(loss_part[0, :1])
    small_g = allsum_small(small_rows(g_mix, g_ffn, g_final, g_bias, loss_row), "allsum_small")
    loss = small_g[11, 0]
    small_g = small_g.at[11].set(0.0)
    sw = small_rows(list(norm_mix), list(norm_ffn), norm_final, list(b_forget), zero_row)
    sm = small_rows(list(m_norm_mix), list(m_norm_ffn), m_norm_final, list(m_b_forget), zero_row)
    sv = small_rows(list(v_norm_mix), list(v_norm_ffn), v_norm_final, list(v_b_forget), zero_row)
    sd, snm, snv = adamw(sw, small_g, sm, sv, "adamw_small")

    def small_out(a):
        return a[0:4], a[8, :], a[9:11, :N_HEADS], a[4:8]

    g_shard = reduce_grads(fam_grads)
    big_w = pack_rows(w_shards, F32)
    big_m = pack_rows([m_w_qkv_even, m_w_o_even, m_w_qkvf_odd, m_w_o_odd, m_w_ffn_in, m_w_ffn_out], F32)
    big_v = pack_rows([v_w_qkv_even, v_w_o_even, v_w_qkvf_odd, v_w_o_odd, v_w_ffn_in, v_w_ffn_out], F32)
    bd, bm, bv = adamw(big_w, g_shard, big_m, big_v, "adamw_big")

    def outputs(small, big):
        mix, fin, bias, ffn = small_out(small)
        qkv_e, o_e, qkvf, o_o, fi, fo = unpack_rows(big)
        return [mix, qkv_e, o_e, qkvf, bias, o_o, ffn, fi, fo, fin]

    return (loss, dcur[None], *outputs(small_g, g_shard), *outputs(sd, bd), *outputs(snm, bm), *outputs(snv, bv))


def local_step(xs, target, norm_mix, norm_ffn, norm_final, b_forget, full):
    s = xs.shape[0]
    tables = _tables_for(s)
    wqkv_e, wo_e, wqkvf, wo_o, wfi, wfo = full
    wf_pad = jnp.pad(wqkvf[:, :, 3 * D_ATTN:], ((0, 0), (0, 0), (0, LANES - N_HEADS)))
    wqkv_o = wqkvf[:, :, :3 * D_ATTN]
    wqkvf_pad = jnp.concatenate([wqkv_o, wf_pad], axis=2)
    bias_pad = jnp.pad(b_forget, ((0, 0), (0, LANES - N_HEADS)))

    saved = []
    cur = xs
    for layer in range(DEPTH):
        n = f"l{layer}"
        h1 = rmsnorm_fwd(cur, norm_mix[layer:layer + 1], n + "_norm_mix")
        keep = {"x": cur, "h1": h1}
        if layer % 2 == 0:
            qkv = matmul(h1, wqkv_e[layer // 2], "nn", BF16, n + "_qkv", 512, 768, 1024)
            o_sb, st_sb = causal_fwd(qkv, 4, "sb", n + "_sb_fwd")
            qd, kd, vd = rotary_prep(qkv, tables, n + "_rotary")
            o_dil, lse_dil = dilated_fwd(qd, kd, vd, n + "_dil_fwd")
            attn = jnp.concatenate([o_sb, o_dil], axis=1)
            keep.update(qkv=qkv, st=st_sb, qd=qd, kd=kd, vd=vd, o_dil=o_dil, lse_dil=lse_dil)
            w_out = wo_e[layer // 2]
        else:
            li = layer // 2
            qkv = matmul(h1, wqkv_o[li], "nn", BF16, n + "_qkv", 512, 768, 1024)
            fl = matmul(h1, wf_pad[li], "nn", F32, n + "_fgate", 512, LANES, 1024)
            cum = forget_fwd(fl, bias_pad[li:li + 1], n + "_forget_fwd")
            f_heads = cum[:, :N_HEADS].T
            fq = jnp.broadcast_to(f_heads[:, :, None], (N_HEADS, s, LANES))
            fk = f_heads.reshape(N_HEADS // 2, 2, s)
            attn, st_fox = causal_fwd(qkv, 8, "fox", n + "_fox_fwd", fq=fq, fk=fk)
            keep.update(qkv=qkv, st=st_fox, fl=fl, fq=fq, fk=fk)
            w_out = wo_o[layer // 2]
        mid = matmul(attn, w_out, "nn", F32, n + "_attn_out", 512, 1024, 1024, res=cur)
        h2 = rmsnorm_fwd(mid, norm_ffn[layer:layer + 1], n + "_norm_ffn")
        gu = matmul(h2, wfi[layer], "nn", F32, n + "_ffn_in", 512, 512, 1024)
        act = swiglu_fwd(gu, n + "_swiglu")
        cur = matmul(act, wfo[layer], "nn", F32, n + "_ffn_out", 512, 1024, D_FF, res=mid)
        keep.update(attn=attn, mid=mid, h2=h2, gu=gu, act=act)
        saved.append(keep)

    dcur, g_final, loss_part = loss_head(cur, norm_final.reshape(1, D_MODEL), target, "loss_head")

    g_mix, g_ffn = [None] * DEPTH, [None] * DEPTH
    g_bias = [None] * (DEPTH // 2)
    g_qkv_e, g_o_e, g_qkvf, g_o_o, g_fi, g_fo = [None] * 2, [None] * 2, [None] * 2, [None] * 2, [None] * DEPTH, [None] * DEPTH
    for layer in reversed(range(DEPTH)):
        n = f"l{layer}"
        kp = saved[layer]
        g_fo[layer] = matmul(kp["act"], dcur, "tn", F32, n + "_d_w_ffn_out", 1408, 1024, 512)
        dact = matmul(dcur, wfo[layer], "nt", F32, n + "_d_act", 512, 1408, 1024)
        dgu = swiglu_bwd(kp["gu"], dact, n + "_d_swiglu")
        g_fi[layer] = matmul(kp["h2"], dgu, "tn", F32, n + "_d_w_ffn_in", 1024, 512, 2048)
        dh2 = matmul(dgu, wfi[layer], "nt", F32, n + "_d_h2", 512, 1024, 512)
        dmid, g_ffn[layer] = rmsnorm_bwd(kp["mid"], norm_ffn[layer:layer + 1], dh2, dcur, n + "_d_norm_ffn")
        li = layer // 2
        if layer % 2 == 0:
            w_out, w_in = wo_e[li], wqkv_e[li]
        else:
            w_out, w_in = wo_o[li], wqkvf_pad[li]
        g_out = matmul(kp["attn"], dmid, "tn", F32, n + "_d_w_o", 1024, 1024, 512)
        dattn = matmul(dmid, w_out, "nt", F32, n + "_d_attn", 512, 1024, 1024)
        if layer % 2 == 0:
            g_o_e[li] = g_out
            dq_a, dk_a, dv_a = causal_bwd(kp["qkv"], dattn, kp["st"], 4, "sb", n + "_sb_bwd")
            dqd, dkd, dvd = dilated_bwd(kp["qd"], kp["kd"], kp["vd"], dattn, kp["o_dil"], kp["lse_dil"], 4, n + "_dil_bwd")
            dq_b, dk_b, dv_b = rotary_bwd(dqd, dkd, dvd, tables, n + "_d_rotary")
            dproj = jnp.concatenate([dq_a, dq_b, dk_a.astype(BF16), dk_b, dv_a.astype(BF16), dv_b], axis=1)
            tn_proj = 768
        else:
            g_o_o[li] = g_out
            dq_f, dk_f, dv_f, dfk = causal_bwd(kp["qkv"], dattn, kp["st"], 8, "fox", n + "_fox_bwd", fq=kp["fq"], fk=kp["fk"])
            dcum = jnp.pad(dfk.reshape(N_HEADS, s).T, ((0, 0), (0, LANES - N_HEADS)))
            dfl, dbias = forget_bwd(kp["fl"], bias_pad[li:li + 1], dcum, n + "_forget_bwd")
            g_bias[li] = dbias[0, :N_HEADS]
            dproj = jnp.concatenate([dq_f, dk_f.astype(BF16), dv_f.astype(BF16), dfl.astype(BF16)], axis=1)
            tn_proj = 640
        g_in = matmul(kp["h1"], dproj, "tn", F32, n + "_d_w_qkv", 1024, tn_proj, 2048)
        dh1 = matmul(dproj, w_in, "nt", F32, n + "_d_h1", 512, 1024, dproj.shape[1] // 5 if layer % 2 else 1024)
        if layer % 2 == 0:
            g_qkv_e[li] = g_in
        else:
            g_qkvf[li] = g_in[:, :3 * D_ATTN + N_HEADS]
        dcur, g_mix[layer] = rmsnorm_bwd(kp["x"], norm_mix[layer:layer + 1], dh1, dmid, n + "_d_norm_mix")

    fam_grads = [jnp.stack(g_qkv_e), jnp.stack(g_o_e), jnp.stack(g_qkvf), jnp.stack(g_o_o), jnp.stack(g_fi), jnp.stack(g_fo)]
    return dcur, fam_grads, (g_mix, g_ffn, g_final, g_bias), loss_part
```

```python
import functools

import jax
import jax.numpy as jnp
from jax import lax
from jax.experimental import pallas as pl
from jax.experimental.pallas import tpu as pltpu

F32 = jnp.float32
BF16 = jnp.bfloat16
MESH = pl.DeviceIdType.MESH

D_MODEL = 1024
DEPTH = 4
HEAD_DIM = 64
N_HEADS = 16
D_ATTN = 1024
D_FF = 2816
ROPE_THETA = 500000.0
ROT_HALF = 8
RMS_EPS = 1e-5
DIL_STRIDES = (1, 4, 16)
ADAM_LR, ADAM_B1, ADAM_B2, ADAM_EPS, ADAM_WD, ADAM_STEP = 0.001, 0.9, 0.999, 1e-8, 0.01, 10

LANES = 128
BLK = 128
VMEM_LIMIT = 56 * 1024 * 1024
NEG = -1e30
N_CHIPS = 4
FLAT_COLS = 1024
FLAT_ROWS = 12800
HALF_ROWS = FLAT_ROWS // 2
SMALL_ROWS = 16


def _params(sem=None):
    return pltpu.CompilerParams(dimension_semantics=sem, vmem_limit_bytes=VMEM_LIMIT)


def _dot(a, b):
    return lax.dot_general(a, b, (((1,), (0,)), ((), ())), preferred_element_type=F32)


def _dot_nt(a, b):
    return lax.dot_general(a, b, (((1,), (1,)), ((), ())), preferred_element_type=F32)


def _dot_tn(a, b):
    return lax.dot_general(a, b, (((0,), (0,)), ((), ())), preferred_element_type=F32)


def _split3(x):
    x1 = x.astype(BF16)
    r1 = x - x1.astype(F32)
    x2 = r1.astype(BF16)
    x3 = (r1 - x2.astype(F32)).astype(BF16)
    return x1, x2, x3


def _dot_exact_lhs(x, t):
    x1, x2, x3 = _split3(x)
    return _dot(x1, t) + _dot(x2, t) + _dot(x3, t)


def _dot_exact_rhs(t, x):
    x1, x2, x3 = _split3(x)
    return _dot(t, x1) + _dot(t, x2) + _dot(t, x3)


def _iotas(shape=(BLK, LANES)):
    return lax.broadcasted_iota(jnp.int32, shape, 0), lax.broadcasted_iota(jnp.int32, shape, 1)


_DIMS = {"nn": (((1,), (0,)), ((), ())), "nt": (((1,), (1,)), ((), ())), "tn": (((0,), (0,)), ((), ()))}


def matmul(a, b, mode, out_dtype, name, tm, tn, tk, res=None, mnk=None, b_spec=None, o_spec=None, into=None):
    if mnk is not None:
        m, n, k = mnk
    elif mode == "nn":
        (m, k), (k2, n) = a.shape, b.shape
    elif mode == "nt":
        (m, k), (n, k2) = a.shape, b.shape
    else:
        (k, m), (k2, n) = a.shape, b.shape
    assert m % tm == 0 and n % tn == 0 and k % tk == 0, (name, a.shape, b.shape)
    nk = k // tk
    a_spec = pl.BlockSpec((tk, tm), lambda i, j, kk: (kk, i)) if mode == "tn" else pl.BlockSpec((tm, tk), lambda i, j, kk: (i, kk))
    if b_spec is None:
        b_spec = pl.BlockSpec((tn, tk), lambda i, j, kk: (j, kk)) if mode == "nt" else pl.BlockSpec((tk, tn), lambda i, j, kk: (kk, j))
    r_spec = pl.BlockSpec((tm, tn), lambda i, j, kk: (i, j))
    if o_spec is None:
        o_spec = r_spec
    dims = _DIMS[mode]
    has_res = res is not None
    n_in = 2 + int(has_res) + int(into is not None)

    def body(*refs):
        a_ref, b_ref = refs[0], refs[1]
        r_ref = refs[2] if has_res else None
        o_ref = refs[n_in]

        def finish(v):
            if has_res:
                v = v + r_ref[...]
            o_ref[...] = v.astype(out_dtype)

        p = lax.dot_general(a_ref[...].astype(BF16), b_ref[...].astype(BF16), dims, preferred_element_type=F32)
        if nk == 1:
            finish(p)
        else:
            acc = refs[-1]
            kk = pl.program_id(2)

            @pl.when(kk == 0)
            def _():
                acc[...] = p

            @pl.when(kk > 0)
            def _():
                acc[...] += p

            @pl.when(kk == nk - 1)
            def _():
                finish(acc[...])

    ops = [a, b] + ([res] if has_res else []) + ([into] if into is not None else [])
    specs = [a_spec, b_spec] + ([r_spec] if has_res else []) + ([_ANY] if into is not None else [])
    out_shape = jax.ShapeDtypeStruct((m, n), out_dtype) if into is None else jax.ShapeDtypeStruct(into.shape, into.dtype)
    return pl.pallas_call(
        body, name=name, out_shape=out_shape,
        grid=(m // tm, n // tn, nk), in_specs=specs, out_specs=o_spec,
        scratch_shapes=[pltpu.VMEM((tm, tn), F32)] if nk > 1 else [],
        input_output_aliases={n_in - 1: 0} if into is not None else {},
        compiler_params=_params(("parallel", "parallel", "arbitrary")),
    )(*ops)


ROWS = 256


def _row_spec(cols, rows=ROWS):
    return pl.BlockSpec((rows, cols), lambda i: (i, 0))


def _fix_spec(r, cols):
    return pl.BlockSpec((r, cols), lambda i: (0, 0))


def rmsnorm_fwd(x, g, name):
    s, d = x.shape

    def body(x_ref, g_ref, h_ref):
        xv = x_ref[...]
        rstd = lax.rsqrt(jnp.mean(xv * xv, axis=-1, keepdims=True) + RMS_EPS)
        h_ref[...] = (xv * rstd * g_ref[...]).astype(BF16)

    return pl.pallas_call(
        body, name=name, out_shape=jax.ShapeDtypeStruct((s, d), BF16), grid=(s // ROWS,),
        in_specs=[_row_spec(d), _fix_spec(1, d)], out_specs=_row_spec(d), compiler_params=_params(("parallel",)),
    )(x, g)


def _rms_bwd_math(xv, gv, dh):
    rstd = lax.rsqrt(jnp.mean(xv * xv, axis=-1, keepdims=True) + RMS_EPS)
    xhat = xv * rstd
    u = dh * gv
    dx = rstd * (u - xhat * jnp.mean(u * xhat, axis=-1, keepdims=True))
    return dx, dh * xhat


def rmsnorm_bwd(x, g, dh, dres, name):
    s, d = x.shape

    def body(x_ref, g_ref, dh_ref, dres_ref, dx_ref, dg_ref):
        dx, dgt = _rms_bwd_math(x_ref[...], g_ref[...], dh_ref[...])
        dx_ref[...] = dres_ref[...] + dx
        part = jnp.sum(dgt, axis=0, keepdims=True)

        @pl.when(pl.program_id(0) == 0)
        def _():
            dg_ref[...] = part

        @pl.when(pl.program_id(0) > 0)
        def _():
            dg_ref[...] += part

    return pl.pallas_call(
        body, name=name, out_shape=(jax.ShapeDtypeStruct((s, d), F32), jax.ShapeDtypeStruct((1, d), F32)),
        grid=(s // ROWS,), in_specs=[_row_spec(d), _fix_spec(1, d), _row_spec(d), _row_spec(d)],
        out_specs=(_row_spec(d), _fix_spec(1, d)), compiler_params=_params(("arbitrary",)),
    )(x, g, dh, dres)


def loss_head(x, g, target, name):
    s, d = x.shape

    def body(x_ref, g_ref, t_ref, dx_ref, dg_ref, loss_ref):
        xv, gv = x_ref[...], g_ref[...]
        rstd = lax.rsqrt(jnp.mean(xv * xv, axis=-1, keepdims=True) + RMS_EPS)
        err = xv * rstd * gv - t_ref[...]
        dx, dgt = _rms_bwd_math(xv, gv, err * (1.0 / d))
        dx_ref[...] = dx
        part = jnp.sum(dgt, axis=0, keepdims=True)
        lpart = jnp.full((1, LANES), 0.5 / d, F32) * jnp.sum(err * err)

        @pl.when(pl.program_id(0) == 0)
        def _():
            dg_ref[...] = part
            loss_ref[...] = lpart

        @pl.when(pl.program_id(0) > 0)
        def _():
            dg_ref[...] += part
            loss_ref[...] += lpart

    return pl.pallas_call(
        body, name=name,
        out_shape=(jax.ShapeDtypeStruct((s, d), F32), jax.ShapeDtypeStruct((1, d), F32), jax.ShapeDtypeStruct((1, LANES), F32)),
        grid=(s // ROWS,), in_specs=[_row_spec(d), _fix_spec(1, d), _row_spec(d)],
        out_specs=(_row_spec(d), _fix_spec(1, d), _fix_spec(1, LANES)), compiler_params=_params(("arbitrary",)),
    )(x, g, target)


def swiglu_fwd(gu, name):
    s, f2 = gu.shape
    f = f2 // 2

    def body(gu_ref, a_ref):
        gv, uv = gu_ref[:, :f], gu_ref[:, f:]
        a_ref[...] = (gv * (1.0 / (1.0 + jnp.exp(-gv))) * uv).astype(BF16)

    return pl.pallas_call(
        body, name=name, out_shape=jax.ShapeDtypeStruct((s, f), BF16), grid=(s // ROWS,),
        in_specs=[_row_spec(f2)], out_specs=_row_spec(f), compiler_params=_params(("parallel",)),
    )(gu)


def swiglu_bwd(gu, dact, name):
    s, f2 = gu.shape
    f = f2 // 2

    def body(gu_ref, da_ref, o_ref):
        gv, uv, da = gu_ref[:, :f], gu_ref[:, f:], da_ref[...]
        sg = 1.0 / (1.0 + jnp.exp(-gv))
        o_ref[:, :f] = (da * uv * sg * (1.0 + gv * (1.0 - sg))).astype(BF16)
        o_ref[:, f:] = (da * gv * sg).astype(BF16)

    return pl.pallas_call(
        body, name=name, out_shape=jax.ShapeDtypeStruct((s, f2), BF16), grid=(s // ROWS,),
        in_specs=[_row_spec(f2), _row_spec(f)], out_specs=_row_spec(f2), compiler_params=_params(("parallel",)),
    )(gu, dact)


Q_OFF, K_OFF, V_OFF = 0, 8, 16


KB = 512
SUB = KB // BLK


def _softplus_parts(z):
    sp = jnp.log(1.0 + jnp.exp(-jnp.abs(z)))
    ls = jnp.minimum(z, 0.0) - sp
    return ls, ls - z


def _wide(t):
    return jnp.concatenate([t] * SUB, axis=1)


def _chunk_dots(x, tri):
    terms = []
    for u in range(SUB):
        terms += list(_split3(x[:, u * BLK:(u + 1) * BLK]))
    r = _dot(jnp.concatenate(terms, axis=0), tri)
    piece = lambda n: r[n * BLK:(n + 1) * BLK]
    return [piece(3 * u) + piece(3 * u + 1) + piece(3 * u + 2) for u in range(SUB)]


def _block_suffix_sums(x, suffix, c):
    loc = _chunk_dots(x, suffix)
    out = [None] * SUB
    for u in reversed(range(SUB)):
        out[u] = loc[u] + c
        c = c + jnp.sum(x[:, u * BLK:(u + 1) * BLK], axis=1, keepdims=True)
    return jnp.concatenate(out, axis=1), c


def _block_prefix_sums(x, tri, c):
    loc = _chunk_dots(x, tri)
    out = []
    for u in range(SUB):
        out.append(loc[u] + c)
        c = c + jnp.sum(x[:, u * BLK:(u + 1) * BLK], axis=1, keepdims=True)
    return jnp.concatenate(out, axis=1), c


def causal_fwd(qkv, npairs, mode, name, fq=None, fk=None):
    s = qkv.shape[0]
    nq = s // BLK
    fox = mode == "fox"

    def body(*refs):
        if fox:
            q_ref, k_ref, v_ref, fq_ref, fk_ref, o_ref, st_ref = refs
        else:
            q_ref, k_ref, v_ref, o_ref, st_ref = refs
        i = pl.program_id(1)
        nkb = i // SUB + 1
        row, lane = _iotas((BLK, KB))
        row_s, lane_s = _iotas()
        qpos = i * BLK + row
        qf = q_ref[...].astype(F32)
        hms = (lane_s < HEAD_DIM, lane_s >= HEAD_DIM)
        qas = [jnp.where(hm, qf, 0.0).astype(BF16) for hm in hms]
        suffix = jnp.where(row_s > lane_s, 1.0, 0.0).astype(BF16)
        zero = jnp.zeros((BLK, LANES), F32)
        col0 = jnp.zeros((BLK, 1), F32)

        def kv(j):
            r0 = pl.multiple_of(j * KB, KB)
            return r0, k_ref[pl.ds(r0, KB), :], v_ref[pl.ds(r0, KB), :]

        if fox:
            fqs = [_wide(fq_ref[a]) for a in range(2)]

            def step(j, carry):
                r0, kb, vb = kv(j)
                ok = r0 + lane <= qpos
                new = []
                for a in range(2):
                    acc, mx, l = carry[3 * a:3 * a + 3]
                    z = _dot_nt(qas[a], kb) * 0.125 + fqs[a] - fk_ref[a:a + 1, pl.ds(r0, KB)]
                    z = jnp.where(ok, z, NEG)
                    mnew = jnp.maximum(mx, jnp.max(z, axis=1, keepdims=True))
                    p = jnp.exp(z - mnew)
                    alpha = jnp.exp(mx - mnew)
                    new += [alpha * acc + _dot(p.astype(BF16), vb), mnew, alpha * l + jnp.sum(p, axis=1, keepdims=True)]
                return tuple(new)

            neg = jnp.full((BLK, 1), NEG, F32)
            res = lax.fori_loop(0, nkb, step, (zero, neg, col0, zero, neg, col0))
            outs = [res[3 * a] / res[3 * a + 2] for a in range(2)]
            stats = [res[3 * a + 1] + jnp.log(res[3 * a + 2]) for a in range(2)]
        else:
            def step(jj, carry):
                r0, kb, vb = kv(nkb - 1 - jj)
                strict = r0 + lane < qpos
                new = []
                for a in range(2):
                    acc, c = carry[2 * a:2 * a + 2]
                    ls, lm = _softplus_parts(_dot_nt(qas[a], kb) * 0.125)
                    lm = jnp.where(strict, lm, 0.0)
                    between, c = _block_suffix_sums(lm, suffix, c)
                    aw = jnp.where(strict, jnp.exp(ls + between), 0.0)
                    new += [acc + _dot(aw.astype(BF16), vb), c]
                return tuple(new)

            res = lax.fori_loop(0, nkb, step, (zero, col0, zero, col0))
            outs, stats = [res[0], res[2]], [res[1], res[3]]
        o_ref[...] = jnp.where(hms[0], outs[0], outs[1])
        for a in range(2):
            st_ref[a] = jnp.broadcast_to(stats[a], (BLK, LANES))

    col = lambda off: (lambda p, i: (0, off + p))
    in_specs = [pl.BlockSpec((BLK, LANES), lambda p, i: (i, Q_OFF + p)),
                pl.BlockSpec((s, LANES), col(K_OFF)), pl.BlockSpec((s, LANES), col(V_OFF))]
    ops = [qkv, qkv, qkv]
    if fox:
        in_specs += [pl.BlockSpec((2, BLK, LANES), lambda p, i: (p, i, 0)), pl.BlockSpec((None, 2, s), lambda p, i: (p, 0, 0))]
        ops += [fq, fk]
    return pl.pallas_call(
        body, name=name,
        out_shape=(jax.ShapeDtypeStruct((s, npairs * LANES), F32), jax.ShapeDtypeStruct((2 * npairs, s, LANES), F32)),
        grid=(npairs, nq), in_specs=in_specs,
        out_specs=(pl.BlockSpec((BLK, LANES), lambda p, i: (i, p)), pl.BlockSpec((2, BLK, LANES), lambda p, i: (p, i, 0))),
        compiler_params=_params(("parallel", "arbitrary")),
    )(*ops)


def causal_bwd(qkv, do, stat, npairs, mode, name, fq=None, fk=None):
    s = qkv.shape[0]
    nq = s // BLK
    fox = mode == "fox"

    def body(*refs):
        if fox:
            q_ref, k_ref, v_ref, do_ref, st_ref, fq_ref, fk_ref, dq_ref, dk_ref, dv_ref, df_ref, p_s, dp_s = refs
        else:
            q_ref, k_ref, v_ref, do_ref, st_ref, dq_ref, dk_ref, dv_ref = refs
        i = pl.program_id(1)

        @pl.when(i == 0)
        def _():
            dk_ref[...] = jnp.zeros_like(dk_ref)
            dv_ref[...] = jnp.zeros_like(dv_ref)
            if fox:
                df_ref[...] = jnp.zeros_like(df_ref)

        nkb = i // SUB + 1
        row, lane = _iotas((BLK, KB))
        row_s, lane_s = _iotas()
        qpos = i * BLK + row
        qf = q_ref[...].astype(F32)
        dov = do_ref[...]
        hms = (lane_s < HEAD_DIM, lane_s >= HEAD_DIM)
        qas = [jnp.where(hm, qf, 0.0).astype(BF16) for hm in hms]
        doas = [jnp.where(hm, dov, 0.0).astype(BF16) for hm in hms]
        stas = [_wide(st_ref[a]) for a in range(2)]
        zero = jnp.zeros((BLK, LANES), F32)
        col0 = jnp.zeros((BLK, 1), F32)

        def kv(j):
            r0 = pl.multiple_of(j * KB, KB)
            return r0, k_ref[pl.ds(r0, KB), :], v_ref[pl.ds(r0, KB), :]

        if fox:
            fqs = [_wide(fq_ref[a]) for a in range(2)]

            def probs(j, deltas):
                r0, kb, vb = kv(j)
                ok = r0 + lane <= qpos
                new = []
                for a in range(2):
                    z = _dot_nt(qas[a], kb) * 0.125 + fqs[a] - fk_ref[a:a + 1, pl.ds(r0, KB)]
                    p = jnp.where(ok, jnp.exp(z - stas[a]), 0.0)
                    dp = _dot_nt(doas[a], vb)
                    p_s[a, j] = p
                    dp_s[a, j] = dp
                    new.append(deltas[a] + jnp.sum(p * dp, axis=1, keepdims=True))
                return tuple(new)

            deltas = lax.fori_loop(0, nkb, probs, (col0, col0))

            def step(j, dqs):
                r0, kb, _ = kv(j)
                new = []
                dk = jnp.zeros((KB, LANES), F32)
                dv = jnp.zeros((KB, LANES), F32)
                for a in range(2):
                    p = p_s[a, j]
                    ds = p * (dp_s[a, j] - deltas[a])
                    dsb = (ds * 0.125).astype(BF16)
                    dk += _dot_tn(dsb, qas[a])
                    dv += _dot_tn(p.astype(BF16), doas[a])
                    df_ref[a:a + 1, pl.ds(r0, KB)] -= jnp.sum(ds, axis=0, keepdims=True)
                    new.append(dqs[a] + _dot(dsb, kb))
                dk_ref[pl.ds(r0, KB), :] += dk
                dv_ref[pl.ds(r0, KB), :] += dv
                return tuple(new)

            dqs = lax.fori_loop(0, nkb, step, (zero, zero))
        else:
            incl = jnp.where(row_s <= lane_s, 1.0, 0.0).astype(BF16)
            excl = jnp.where(row_s < lane_s, 1.0, 0.0).astype(BF16)

            def step(j, carry):
                r0, kb, vb = kv(j)
                strict = r0 + lane < qpos
                new = []
                dk = jnp.zeros((KB, LANES), F32)
                dv = jnp.zeros((KB, LANES), F32)
                for a in range(2):
                    dq, cm, cg = carry[3 * a:3 * a + 3]
                    ls, lm = _softplus_parts(_dot_nt(qas[a], kb) * 0.125)
                    lm = jnp.where(strict, lm, 0.0)
                    beta = jnp.exp(ls)
                    upto, cm = _block_prefix_sums(lm, incl, cm)
                    aw = jnp.where(strict, jnp.exp(ls + stas[a] - upto), 0.0)
                    g = aw * _dot_nt(doas[a], vb)
                    pre, cg = _block_prefix_sums(g, excl, cg)
                    dz = jnp.where(strict, g * (1.0 - beta) - pre * beta, 0.0)
                    dzb = (dz * 0.125).astype(BF16)
                    dk += _dot_tn(dzb, qas[a])
                    dv += _dot_tn(aw.astype(BF16), doas[a])
                    new += [dq + _dot(dzb, kb), cm, cg]
                dk_ref[pl.ds(r0, KB), :] += dk
                dv_ref[pl.ds(r0, KB), :] += dv
                return tuple(new)

            res = lax.fori_loop(0, nkb, step, (zero, col0, col0, zero, col0, col0))
            dqs = (res[0], res[3])
        dq_ref[...] = jnp.where(hms[0], dqs[0], dqs[1]).astype(BF16)

    col = lambda off: (lambda p, i: (0, off + p))
    blk = pl.BlockSpec((BLK, LANES), lambda p, i: (i, p))
    acc = pl.BlockSpec((s, LANES), lambda p, i: (0, p))
    st_spec = pl.BlockSpec((2, BLK, LANES), lambda p, i: (p, i, 0))
    in_specs = [pl.BlockSpec((BLK, LANES), lambda p, i: (i, Q_OFF + p)), pl.BlockSpec((s, LANES), col(K_OFF)),
                pl.BlockSpec((s, LANES), col(V_OFF)), blk, st_spec]
    ops = [qkv, qkv, qkv, do, stat]
    w = npairs * LANES
    out_shape = [jax.ShapeDtypeStruct((s, w), BF16), jax.ShapeDtypeStruct((s, w), F32), jax.ShapeDtypeStruct((s, w), F32)]
    out_specs = [blk, acc, acc]
    scratch = []
    if fox:
        fk_spec = pl.BlockSpec((None, 2, s), lambda p, i: (p, 0, 0))
        in_specs += [st_spec, fk_spec]
        ops += [fq, fk]
        out_shape.append(jax.ShapeDtypeStruct((npairs, 2, s), F32))
        out_specs.append(fk_spec)
        scratch = [pltpu.VMEM((2, s // KB, BLK, KB), F32)] * 2
    return pl.pallas_call(
        body, name=name, out_shape=tuple(out_shape), grid=(npairs, nq), in_specs=in_specs, out_specs=tuple(out_specs),
        scratch_shapes=scratch, compiler_params=_params(("parallel", "arbitrary")),
    )(*ops)


def forget_fwd(fl, bias, name):
    s = fl.shape[0]

    def body(fl_ref, b_ref, f_ref):
        row, lane = _iotas()
        lower = jnp.where(lane <= row, 1.0, 0.0).astype(BF16)

        def step(n, carry):
            r0 = pl.multiple_of(n * BLK, BLK)
            ls, _ = _softplus_parts(fl_ref[pl.ds(r0, BLK), :] + b_ref[...])
            blk = _dot_exact_rhs(lower, ls) + carry
            f_ref[pl.ds(r0, BLK), :] = blk
            return blk[BLK - 1:BLK, :]

        lax.fori_loop(0, s // BLK, step, jnp.zeros((1, LANES), F32))

    return pl.pallas_call(
        body, name=name, out_shape=jax.ShapeDtypeStruct((s, LANES), F32),
        in_specs=[pl.BlockSpec(memory_space=pltpu.VMEM)] * 2, out_specs=pl.BlockSpec(memory_space=pltpu.VMEM),
        compiler_params=_params(),
    )(fl, bias)


def forget_bwd(fl, bias, df, name):
    s = fl.shape[0]
    nb = s // BLK

    def body(fl_ref, b_ref, df_ref, o_ref, db_ref):
        row, lane = _iotas()
        upper = jnp.where(lane >= row, 1.0, 0.0).astype(BF16)

        def step(nn, carry):
            tail, db = carry
            r0 = pl.multiple_of((nb - 1 - nn) * BLK, BLK)
            dls = _dot_exact_rhs(upper, df_ref[pl.ds(r0, BLK), :]) + tail
            xv = fl_ref[pl.ds(r0, BLK), :] + b_ref[...]
            dfl = dls * (1.0 / (1.0 + jnp.exp(xv)))
            o_ref[pl.ds(r0, BLK), :] = dfl
            return dls[0:1, :], db + jnp.sum(dfl, axis=0, keepdims=True)

        _, db = lax.fori_loop(0, nb, step, (jnp.zeros((1, LANES), F32), jnp.zeros((1, LANES), F32)))
        db_ref[...] = db

    return pl.pallas_call(
        body, name=name, out_shape=(jax.ShapeDtypeStruct((s, LANES), F32), jax.ShapeDtypeStruct((1, LANES), F32)),
        in_specs=[pl.BlockSpec(memory_space=pltpu.VMEM)] * 3,
        out_specs=(pl.BlockSpec(memory_space=pltpu.VMEM), pl.BlockSpec(memory_space=pltpu.VMEM)),
        compiler_params=_params(),
    )(fl, bias, df)


def _rot_tables(s):
    inv = ROPE_THETA ** (-jnp.arange(ROT_HALF, dtype=F32) * 2.0 / (2 * ROT_HALF))
    ang = jnp.arange(s, dtype=F32)[:, None] * inv[None, :]
    cos, sin = jnp.cos(ang), jnp.sin(ang)
    z8 = jnp.zeros((s, ROT_HALF), F32)
    rest = HEAD_DIM - 2 * ROT_HALF
    zr, onr = jnp.zeros((s, rest), F32), jnp.ones((s, rest), F32)
    tile = lambda t: jnp.tile(t, (1, 2))
    return tile(jnp.concatenate([cos, cos, onr], 1)), tile(jnp.concatenate([-sin, z8, zr], 1)), tile(jnp.concatenate([z8, sin, zr], 1))


def rotary_prep(qkv, tables, name):
    s = qkv.shape[0]
    w = 4 * LANES

    def body(q_ref, k_ref, v_ref, c_ref, s1_ref, s2_ref, qo_ref, ko_ref, vo_ref):
        c, s1, s2 = c_ref[...], s1_ref[...], s2_ref[...]

        def rot(xv):
            return xv * c + pltpu.roll(xv, LANES - ROT_HALF, 1) * s1 + pltpu.roll(xv, ROT_HALF, 1) * s2

        qo_ref[...] = rot(q_ref[...].astype(F32)) * 0.125
        ko_ref[...] = rot(k_ref[...].astype(F32))
        vo_ref[...] = v_ref[...].astype(F32)

    cb = lambda off: pl.BlockSpec((ROWS, LANES), lambda i, j: (i, off + j))
    tb = pl.BlockSpec((ROWS, LANES), lambda i, j: (i, 0))
    out = jax.ShapeDtypeStruct((s, w), F32)
    return pl.pallas_call(
        body, name=name, out_shape=(out, out, out), grid=(s // ROWS, 4),
        in_specs=[cb(Q_OFF + 4), cb(K_OFF + 4), cb(V_OFF + 4), tb, tb, tb], out_specs=(cb(0), cb(0), cb(0)),
        compiler_params=_params(("parallel", "parallel")),
    )(qkv, qkv, qkv, *tables)


def rotary_bwd(dq, dk, dv, tables, name):
    s, w = dq.shape

    def body(dq_ref, dk_ref, dv_ref, c_ref, s1_ref, s2_ref, qo_ref, ko_ref, vo_ref):
        c, s1, s2 = c_ref[...], s1_ref[...], s2_ref[...]

        def rot_t(dy):
            return dy * c + pltpu.roll(dy * s1, ROT_HALF, 1) + pltpu.roll(dy * s2, LANES - ROT_HALF, 1)

        qo_ref[...] = (rot_t(dq_ref[...]) * 0.125).astype(BF16)
        ko_ref[...] = rot_t(dk_ref[...]).astype(BF16)
        vo_ref[...] = dv_ref[...].astype(BF16)

    cb = pl.BlockSpec((ROWS, LANES), lambda i, j: (i, j))
    tb = pl.BlockSpec((ROWS, LANES), lambda i, j: (i, 0))
    out = jax.ShapeDtypeStruct((s, w), BF16)
    return pl.pallas_call(
        body, name=name, out_shape=(out, out, out), grid=(s // ROWS, w // LANES),
        in_specs=[cb, cb, cb, tb, tb, tb], out_specs=(cb, cb, cb), compiler_params=_params(("parallel", "parallel")),
    )(dq, dk, dv, *tables)


def _deinterleave(dst, src_ref, stride, s, dtype):
    length = s // stride
    for r in range(stride):
        if stride == 1:
            dst[...] = src_ref[...].astype(dtype)
        else:
            dst[r * length:(r + 1) * length, :] = src_ref[pl.ds(r, length, stride=stride), :].astype(dtype)


def _band_masks(row, lane, first):
    return lane <= row, lane >= row + jnp.where(first, BLK, 0)


def dilated_fwd(qd, kd, vd, name):
    s, w = qd.shape
    npairs = w // LANES
    nblk = s // BLK

    def body(q_ref, k_ref, v_ref, o_ref, lse_ref, qs, ks, vs, od, ld, on, ln):
        row, lane = _iotas()
        for pi, stride in enumerate(DIL_STRIDES):
            per = (s // stride) // BLK
            _deinterleave(qs, q_ref, stride, s, BF16)
            _deinterleave(ks, k_ref, stride, s, BF16)
            _deinterleave(vs, v_ref, stride, s, BF16)

            def block(b, carry):
                r0 = pl.multiple_of(b * BLK, BLK)
                rp = pl.multiple_of(jnp.maximum(b - 1, 0) * BLK, BLK)
                mc, mp = _band_masks(row, lane, b % per == 0)
                q = qs[pl.ds(r0, BLK), :]
                kc, kp, vc, vp = ks[pl.ds(r0, BLK), :], ks[pl.ds(rp, BLK), :], vs[pl.ds(r0, BLK), :], vs[pl.ds(rp, BLK), :]
                out = jnp.zeros((BLK, LANES), F32)
                lse = jnp.zeros((BLK, LANES), F32)
                for a in range(2):
                    hm = (lane < HEAD_DIM) if a == 0 else (lane >= HEAD_DIM)
                    qa = jnp.where(hm, q.astype(F32), 0.0).astype(BF16)
                    sc = jnp.where(mc, _dot_nt(qa, kc), NEG)
                    sp = jnp.where(mp, _dot_nt(qa, kp), NEG)
                    mx = jnp.maximum(jnp.max(sc, axis=1, keepdims=True), jnp.max(sp, axis=1, keepdims=True))
                    pc, pp = jnp.exp(sc - mx), jnp.exp(sp - mx)
                    l = jnp.sum(pc, axis=1, keepdims=True) + jnp.sum(pp, axis=1, keepdims=True)
                    oa = (_dot(pc.astype(BF16), vc) + _dot(pp.astype(BF16), vp)) / l
                    out = jnp.where(hm, oa, out)
                    lse = jnp.where(hm, mx + jnp.log(l), lse)
                od[pl.ds(r0, BLK), :] = out
                ld[pl.ds(r0, BLK), :] = lse
                return carry

            lax.fori_loop(0, nblk, block, 0)
            length = s // stride
            for r in range(stride):
                if stride == 1:
                    on[pi] = od[...]
                    ln[pi] = ld[...]
                else:
                    on[pi, pl.ds(r, length, stride=stride), :] = od[r * length:(r + 1) * length, :]
                    ln[pi, pl.ds(r, length, stride=stride), :] = ld[r * length:(r + 1) * length, :]

        def merge(n, carry):
            r0 = pl.multiple_of(n * BLK, BLK)
            ls = [ln[pi, pl.ds(r0, BLK), :] for pi in range(3)]
            mx = jnp.maximum(jnp.maximum(ls[0], ls[1]), ls[2])
            ws = [jnp.exp(lv - mx) for lv in ls]
            den = ws[0] + ws[1] + ws[2]
            num = ws[0] * on[0, pl.ds(r0, BLK), :] + ws[1] * on[1, pl.ds(r0, BLK), :] + ws[2] * on[2, pl.ds(r0, BLK), :]
            o_ref[pl.ds(r0, BLK), :] = num / den
            lse_ref[pl.ds(r0, BLK), :] = mx + jnp.log(den)
            return carry

        lax.fori_loop(0, nblk, merge, 0)

    colspec = pl.BlockSpec((s, LANES), lambda p: (0, p))
    out = jax.ShapeDtypeStruct((s, w), F32)
    return pl.pallas_call(
        body, name=name, out_shape=(out, out), grid=(npairs,), in_specs=[colspec] * 3, out_specs=(colspec, colspec),
        scratch_shapes=[pltpu.VMEM((s, LANES), BF16)] * 3 + [pltpu.VMEM((s, LANES), F32)] * 2 + [pltpu.VMEM((3, s, LANES), F32)] * 2,
        compiler_params=_params(("parallel",)),
    )(qd, kd, vd)


def dilated_bwd(qd, kd, vd, do, out, lse, do_off, name):
    s, w = qd.shape
    npairs = w // LANES
    nblk = s // BLK

    def body(q_ref, k_ref, v_ref, do_ref, out_ref, lse_ref, dq_ref, dk_ref, dv_ref, qs, ks, vs, dos, dls, lss, dqd, dkd, dvd, dln):
        row, lane = _iotas()
        same_head = jnp.where((row < HEAD_DIM) == (lane < HEAD_DIM), 1.0, 0.0).astype(BF16)

        def delta_blk(n, carry):
            r0 = pl.multiple_of(n * BLK, BLK)
            dln[pl.ds(r0, BLK), :] = _dot_exact_lhs(do_ref[pl.ds(r0, BLK), :] * out_ref[pl.ds(r0, BLK), :], same_head)
            return carry

        lax.fori_loop(0, nblk, delta_blk, 0)
        for pi, stride in enumerate(DIL_STRIDES):
            per = (s // stride) // BLK
            _deinterleave(qs, q_ref, stride, s, BF16)
            _deinterleave(ks, k_ref, stride, s, BF16)
            _deinterleave(vs, v_ref, stride, s, BF16)
            _deinterleave(dos, do_ref, stride, s, BF16)
            _deinterleave(dls, dln, stride, s, F32)
            _deinterleave(lss, lse_ref, stride, s, F32)

            def block(b, carry):
                r0 = pl.multiple_of(b * BLK, BLK)
                rp = pl.multiple_of(jnp.maximum(b - 1, 0) * BLK, BLK)
                first = b % per == 0
                mc, mp = _band_masks(row, lane, first)
                q, dov = qs[pl.ds(r0, BLK), :], dos[pl.ds(r0, BLK), :]
                kc, kp, vc, vp = ks[pl.ds(r0, BLK), :], ks[pl.ds(rp, BLK), :], vs[pl.ds(r0, BLK), :], vs[pl.ds(rp, BLK), :]
                lse_t, dl_t = lss[pl.ds(r0, BLK), :], dls[pl.ds(r0, BLK), :]
                dq = jnp.zeros((BLK, LANES), F32)
                dkc = jnp.zeros((BLK, LANES), F32)
                dkp = jnp.zeros((BLK, LANES), F32)
                dvc = jnp.zeros((BLK, LANES), F32)
                dvp = jnp.zeros((BLK, LANES), F32)
                for a in range(2):
                    hm = (lane < HEAD_DIM) if a == 0 else (lane >= HEAD_DIM)
                    pick = lane == a * HEAD_DIM
                    qa = jnp.where(hm, q.astype(F32), 0.0).astype(BF16)
                    doa = jnp.where(hm, dov.astype(F32), 0.0).astype(BF16)
                    lse_a = jnp.sum(jnp.where(pick, lse_t, 0.0), axis=1, keepdims=True)
                    dl_a = jnp.sum(jnp.where(pick, dl_t, 0.0), axis=1, keepdims=True)
                    pc = jnp.where(mc, jnp.exp(_dot_nt(qa, kc) - lse_a), 0.0)
                    pp = jnp.where(mp, jnp.exp(_dot_nt(qa, kp) - lse_a), 0.0)
                    dsc = (pc * (_dot_nt(doa, vc) - dl_a)).astype(BF16)
                    dsp = (pp * (_dot_nt(doa, vp) - dl_a)).astype(BF16)
                    dq = jnp.where(hm, _dot(dsc, kc) + _dot(dsp, kp), dq)
                    dkc += _dot_tn(dsc, qa)
                    dkp += _dot_tn(dsp, qa)
                    dvc += _dot_tn(pc.astype(BF16), doa)
                    dvp += _dot_tn(pp.astype(BF16), doa)
                dqd[pl.ds(r0, BLK), :] = dq
                dkd[pl.ds(r0, BLK), :] = dkc
                dvd[pl.ds(r0, BLK), :] = dvc

                @pl.when(jnp.logical_not(first))
                def _():
                    dkd[pl.ds(rp, BLK), :] += dkp
                    dvd[pl.ds(rp, BLK), :] += dvp

                return carry

            lax.fori_loop(0, nblk, block, 0)
            length = s // stride
            for dst, src in ((dq_ref, dqd), (dk_ref, dkd), (dv_ref, dvd)):
                for r in range(stride):
                    if stride == 1:
                        dst[...] = src[...]
                    else:
                        dst[pl.ds(r, length, stride=stride), :] += src[r * length:(r + 1) * length, :]

    colspec = pl.BlockSpec((s, LANES), lambda p: (0, p))
    do_spec = pl.BlockSpec((s, LANES), lambda p: (0, do_off + p))
    o3 = jax.ShapeDtypeStruct((s, w), F32)
    return pl.pallas_call(
        body, name=name, out_shape=(o3, o3, o3), grid=(npairs,),
        in_specs=[colspec, colspec, colspec, do_spec, colspec, colspec], out_specs=(colspec, colspec, colspec),
        scratch_shapes=[pltpu.VMEM((s, LANES), BF16)] * 4 + [pltpu.VMEM((s, LANES), F32)] * 6,
        compiler_params=_params(("parallel",)),
    )(qd, kd, vd, do, out, lse)


def adamw(w, g, m, v, name):
    rows, cols = w.shape
    rb = min(rows, ROWS)
    c1 = 1.0 - ADAM_B1 ** ADAM_STEP
    c2 = 1.0 - ADAM_B2 ** ADAM_STEP

    def body(w_ref, g_ref, m_ref, v_ref, d_ref, mo_ref, vo_ref):
        gv = g_ref[...]
        mn = ADAM_B1 * m_ref[...] + (1.0 - ADAM_B1) * gv
        vn = ADAM_B2 * v_ref[...] + (1.0 - ADAM_B2) * (gv * gv)
        d_ref[...] = -ADAM_LR * ((mn / c1) / (jnp.sqrt(vn / c2) + ADAM_EPS) + ADAM_WD * w_ref[...])
        mo_ref[...] = mn
        vo_ref[...] = vn

    spec = _row_spec(cols, rb)
    out = jax.ShapeDtypeStruct((rows, cols), F32)
    return pl.pallas_call(
        body, name=name, out_shape=(out, out, out), grid=(rows // rb,), in_specs=[spec] * 4, out_specs=(spec,) * 3,
        compiler_params=_params(("parallel",)),
    )(w, g, m, v)


def _prefetch_call(body, name, scalar, ops, grid, in_specs, out_specs, out_shape, sem):
    spec = pltpu.PrefetchScalarGridSpec(num_scalar_prefetch=1, grid=grid, in_specs=in_specs, out_specs=out_specs)
    return pl.pallas_call(body, name=name, grid_spec=spec, out_shape=out_shape, compiler_params=_params(sem))(scalar, *ops)


def pair_sum(g, got, core, name):
    nl, nc, r, c = g.shape
    rh = r // 2

    def body(core_ref, g_ref, got_ref, o_ref):
        o_ref[...] = (g_ref[...].astype(F32) + got_ref[...].astype(F32)).astype(BF16)

    blk = lambda rows_of: pl.BlockSpec((None, None, rh, c), rows_of)
    return _prefetch_call(
        body, name, core, (g, got), (nl, nc),
        [blk(lambda l, j, core_ref: (l, j, core_ref[0], 0)), blk(lambda l, j, core_ref: (l, j, 0, 0))],
        blk(lambda l, j, core_ref: (l, j, 0, 0)), jax.ShapeDtypeStruct((nl, nc, rh, c), BF16), ("parallel", "parallel"))


def chip_sum(pair, got, chip, name):
    nl, _, rh, c = pair.shape

    def body(chip_ref, p_ref, a_ref, b_ref, c_ref, o_ref):
        o_ref[...] = ((p_ref[...].astype(F32) + a_ref[...].astype(F32)) + b_ref[...].astype(F32)) + c_ref[...].astype(F32)

    arrival = lambda k: pl.BlockSpec((None, None, rh, c), lambda l, chip_ref: (k, l, 0, 0))
    return _prefetch_call(
        body, name, chip, (pair, got, got, got), (nl,),
        [pl.BlockSpec((None, None, rh, c), lambda l, chip_ref: (l, chip_ref[0], 0, 0)), arrival(0), arrival(1), arrival(2)],
        pl.BlockSpec((None, rh, c), lambda l, chip_ref: (l, 0, 0)), jax.ShapeDtypeStruct((nl, rh, c), F32), ("parallel",))


def adamw_family(w, m, v, g_mine, g_other, core, name):
    nl, r, c = w.shape
    rh = r // 2
    nb = 4 if rh % 512 == 0 else (2 if rh % 16 == 0 and rh > 256 else 1)
    rb = rh // nb
    c1 = 1.0 - ADAM_B1 ** ADAM_STEP
    c2 = 1.0 - ADAM_B2 ** ADAM_STEP

    def body(core_ref, w_ref, m_ref, v_ref, gm_ref, go_ref, g_ref, d_ref, mo_ref, vo_ref):
        gv = jnp.where(pl.program_id(1) == core_ref[0], gm_ref[...], go_ref[...])
        mn = ADAM_B1 * m_ref[...] + (1.0 - ADAM_B1) * gv
        vn = ADAM_B2 * v_ref[...] + (1.0 - ADAM_B2) * (gv * gv)
        g_ref[...] = gv
        d_ref[...] = -ADAM_LR * ((mn / c1) / (jnp.sqrt(vn / c2) + ADAM_EPS) + ADAM_WD * w_ref[...])
        mo_ref[...] = mn
        vo_ref[...] = vn

    full = pl.BlockSpec((None, rb, c), lambda l, h, i, core_ref: (l, h * nb + i, 0))
    half = pl.BlockSpec((None, rb, c), lambda l, h, i, core_ref: (l, i, 0))
    out = jax.ShapeDtypeStruct((nl, r, c), F32)
    return _prefetch_call(body, name, core, (w, m, v, g_mine, g_other), (nl, 2, nb), [full, full, full, half, half],
                          (full, full, full, full), (out, out, out, out), ("parallel", "parallel", "parallel"))


def _coords():
    return lax.axis_index("x"), lax.axis_index("y"), lax.axis_index("c")


def _other_chips(x, y):
    return ((1 - x, y), (x, 1 - y), (1 - x, 1 - y))


_ANY = pl.BlockSpec(memory_space=pl.ANY)


def _exchange_call(body, name, arrays, out_shapes, n_copies, n_local=0):
    n = len(arrays)

    def wrapped(*refs):
        body(refs[:n], refs[n:n + len(out_shapes)], *refs[n + len(out_shapes):])

    scratch = [pltpu.SemaphoreType.DMA((n_copies,)), pltpu.SemaphoreType.DMA((n_copies,))]
    if n_local:
        scratch.append(pltpu.SemaphoreType.DMA((n_local,)))
    return pl.pallas_call(
        wrapped, name=name, out_shape=tuple(out_shapes), in_specs=[_ANY] * n, out_specs=tuple([_ANY] * len(out_shapes)),
        scratch_shapes=scratch, compiler_params=_params(),
    )(*arrays)


def _remote(send_sems, recv_sems, n, src, dst, to):
    return pltpu.make_async_remote_copy(src_ref=src, dst_ref=dst, send_sem=send_sems.at[n], recv_sem=recv_sems.at[n],
                                        device_id=to, device_id_type=MESH)


def gather_weights(shards, name):
    nf = len(shards)
    halves = [sh.shape[0] // 2 for sh in shards]

    def body(s_refs, o_refs, send_sems, recv_sems, local_sems):
        x, y, c = _coords()
        me = 2 * x + y
        sibling = (x, y, 1 - c)
        chips = _other_chips(x, y)
        part = lambda f, which: pl.ds(which * halves[f], halves[f])
        local = [pltpu.make_async_copy(s_refs[f], o_refs[f].at[me], local_sems.at[f]) for f in range(nf)]
        for cp in local:
            cp.start()
        first = []
        for k, (px, py) in enumerate(chips):
            for f in range(nf):
                cp = _remote(send_sems, recv_sems, k * nf + f, s_refs[f].at[part(f, c)], o_refs[f].at[me, part(f, c)], (px, py, c))
                cp.start()
                first.append(cp)
        passed = []
        for k, (px, py) in enumerate(chips):
            for f in range(nf):
                landed = o_refs[f].at[2 * px + py, part(f, c)]
                _remote(send_sems, recv_sems, k * nf + f, landed, landed, (px, py, c)).wait_recv()
                fwd = _remote(send_sems, recv_sems, (3 + k) * nf + f, landed, landed, sibling)
                fwd.start()
                passed.append(fwd)
        for k, (px, py) in enumerate(chips):
            for f in range(nf):
                other = o_refs[f].at[2 * px + py, part(f, 1 - c)]
                _remote(send_sems, recv_sems, (3 + k) * nf + f, other, other, sibling).wait_recv()
        for cp in first + passed:
            cp.wait_send()
        for cp in local:
            cp.wait()

    outs = [jax.ShapeDtypeStruct((N_CHIPS,) + sh.shape, sh.dtype) for sh in shards]
    return _exchange_call(body, name, shards, outs, 6 * nf, nf)


def pair_swap(grads, name):
    counts = [g.shape[0] for g in grads]

    def body(g_refs, o_refs, send_sems, recv_sems):
        x, y, c = _coords()
        cps, n = [], 0
        for f, g_ref in enumerate(g_refs):
            rh = g_ref.shape[2] // 2
            for l in range(counts[f]):
                cps.append(_remote(send_sems, recv_sems, n, g_ref.at[l, :, pl.ds((1 - c) * rh, rh), :], o_refs[f].at[l], (x, y, 1 - c)))
                n += 1
        for cp in cps:
            cp.start()
        for cp in cps:
            cp.wait()

    outs = [jax.ShapeDtypeStruct((g.shape[0], g.shape[1], g.shape[2] // 2, g.shape[3]), g.dtype) for g in grads]
    return _exchange_call(body, name, grads, outs, sum(counts))


def chip_scatter(pairs, name):
    counts = [p.shape[0] for p in pairs]

    def body(p_refs, o_refs, send_sems, recv_sems):
        x, y, c = _coords()
        cps, n = [], 0
        for k, (px, py) in enumerate(_other_chips(x, y)):
            for f, p_ref in enumerate(p_refs):
                for l in range(counts[f]):
                    cps.append(_remote(send_sems, recv_sems, n, p_ref.at[l, 2 * px + py], o_refs[f].at[k, l], (px, py, c)))
                    n += 1
        for cp in cps:
            cp.start()
        for cp in cps:
            cp.wait()

    outs = [jax.ShapeDtypeStruct((3, p.shape[0]) + p.shape[2:], p.dtype) for p in pairs]
    return _exchange_call(body, name, pairs, outs, 3 * sum(counts))


def half_swap(halves, name):
    def body(h_refs, o_refs, send_sems, recv_sems):
        x, y, c = _coords()
        cps = [_remote(send_sems, recv_sems, f, h_ref, o_refs[f], (x, y, 1 - c)) for f, h_ref in enumerate(h_refs)]
        for cp in cps:
            cp.start()
        for cp in cps:
            cp.wait()

    return _exchange_call(body, name, halves, [jax.ShapeDtypeStruct(h.shape, h.dtype) for h in halves], len(halves))


def allsum_small(part, name):
    def body(p_ref, tot_ref, all_ref, send_sems, recv_sems):
        x, y, c = _coords()
        me, sibling = (x, y, c), (x, y, 1 - c)
        chips = _other_chips(x, y)

        def slot(px, py, pc):
            return all_ref.at[4 * px + 2 * py + pc]

        def copy(k, block, to, src=None):
            return pltpu.make_async_remote_copy(src_ref=slot(*block) if src is None else src, dst_ref=slot(*block),
                                                send_sem=send_sems.at[k], recv_sem=recv_sems.at[k], device_id=to, device_id_type=MESH)

        slot(*me)[...] = p_ref[...]
        first = [copy(0, me, sibling, src=p_ref)] + [copy(1 + j, me, (*chip, c), src=p_ref) for j, chip in enumerate(chips)]
        for cp in first:
            cp.start()
        passed = [copy(4 + j, (*chip, c), sibling) for j, chip in enumerate(chips)]
        for j, chip in enumerate(chips):
            copy(1 + j, (*chip, c), me).wait_recv()
            passed[j].start()
        copy(0, sibling, me).wait_recv()
        for j, chip in enumerate(chips):
            copy(4 + j, (*chip, 1 - c), me).wait_recv()
        for cp in first + passed:
            cp.wait_send()
        tot = all_ref[0]
        for d in range(1, 8):
            tot = tot + all_ref[d]
        tot_ref[...] = tot

    vm = pl.BlockSpec(memory_space=pltpu.VMEM)
    return pl.pallas_call(
        body, name=name, out_shape=jax.ShapeDtypeStruct(part.shape, F32), in_specs=[vm], out_specs=vm,
        scratch_shapes=[pltpu.VMEM((8,) + part.shape, F32), pltpu.SemaphoreType.DMA((7,)), pltpu.SemaphoreType.DMA((7,))],
        compiler_params=_params(),
    )(part)


QKVF_COLS = 772
QKVF_PAD = 896


def _tables_for(s):
    return _rot_tables(s)


def natural_weights(gathered):
    g_qkv_e, g_o_e, g_qkvf, g_o_o, g_fi, g_fo = gathered
    rows_major = lambda g: jnp.transpose(g, (1, 0, 2, 3)).reshape(g.shape[1], N_CHIPS * g.shape[2], g.shape[3])
    wqkvf = jnp.transpose(g_qkvf, (1, 2, 0, 3)).reshape(g_qkvf.shape[1], D_MODEL, N_CHIPS * QKVF_COLS)
    return {"qkv_e": g_qkv_e, "ffn_in": g_fi, "o_e": rows_major(g_o_e), "o_o": rows_major(g_o_o), "ffn_out": rows_major(g_fo),
            "qkvf": wqkvf}


def reduce_and_update(grads, weights, moments1, moments2):
    core = lax.axis_index("c").astype(jnp.int32).reshape(1)
    chip = (2 * lax.axis_index("x") + lax.axis_index("y")).astype(jnp.int32).reshape(1)
    got = pair_swap(grads, "grad_pair_swap")
    pairs = [pair_sum(g, r, core, f"grad_pair_sum_{f}") for f, (g, r) in enumerate(zip(grads, got))]
    arrived = chip_scatter(pairs, "grad_chip_scatter")
    mine = [chip_sum(p, a, chip, f"grad_chip_sum_{f}") for f, (p, a) in enumerate(zip(pairs, arrived))]
    other = half_swap(mine, "grad_half_swap")
    return [adamw_family(w, m, v, gm, go, core, f"adamw_{f}")
            for f, (w, m, v, gm, go) in enumerate(zip(weights, moments1, moments2, mine, other))]


def kernel(x, norm_mix, w_qkv_even, w_o_even, w_qkvf_odd, b_forget, w_o_odd, norm_ffn, w_ffn_in, w_ffn_out, norm_final, loss_target, m_norm_mix, m_w_qkv_even, m_w_o_even, m_w_qkvf_odd, m_b_forget, m_w_o_odd, m_norm_ffn, m_w_ffn_in, m_w_ffn_out, m_norm_final, v_norm_mix, v_w_qkv_even, v_w_o_even, v_w_qkvf_odd, v_b_forget, v_w_o_odd, v_norm_ffn, v_w_ffn_in, v_w_ffn_out, v_norm_final):
    w_shards = [w_qkv_even, w_o_even, w_qkvf_odd, w_o_odd, w_ffn_in, w_ffn_out]
    gathered = gather_weights([w.astype(BF16) for w in w_shards], "gather_weights")

    dcur, fam_grads, (g_mix, g_ffn, g_final, g_bias), loss_part = local_step(
        x[0], loss_target[0], norm_mix, norm_ffn, norm_final, b_forget, natural_weights(gathered))

    zero_row = jnp.zeros((1, D_MODEL), F32)
    pad16 = lambda v: jnp.pad(v, (0, D_MODEL - v.shape[0]))[None, :]
    small_rows = lambda mix, ffn, fin, bias, last: jnp.concatenate(
        [r.reshape(1, D_MODEL) for r in mix] + [r.reshape(1, D_MODEL) for r in ffn] + [fin.reshape(1, D_MODEL)]
        + [pad16(b) for b in bias] + [last] + [zero_row] * (SMALL_ROWS - 12), axis=0)
    loss_row = pad16(loss_part[0, :1])
    small_g = allsum_small(small_rows(g_mix, g_ffn, g_final, g_bias, loss_row), "allsum_small")
    loss = small_g[11, 0]
    small_g = small_g.at[11].set(0.0)
    sw = small_rows(list(norm_mix), list(norm_ffn), norm_final, list(b_forget), zero_row)
    sm = small_rows(list(m_norm_mix), list(m_norm_ffn), m_norm_final, list(m_b_forget), zero_row)
    sv = small_rows(list(v_norm_mix), list(v_norm_ffn), v_norm_final, list(v_b_forget), zero_row)
    sd, snm, snv = adamw(sw, small_g, sm, sv, "adamw_small")

    def small_out(a):
        return a[0:4], a[8, :], a[9:11, :N_HEADS], a[4:8]

    widen = lambda t: jnp.pad(t, ((0, 0), (0, 0), (0, QKVF_PAD - QKVF_COLS)))
    padded = lambda ws: [widen(t) if f == 2 else t for f, t in enumerate(ws)]
    big = reduce_and_update(
        fam_grads, padded(w_shards), padded([m_w_qkv_even, m_w_o_even, m_w_qkvf_odd, m_w_o_odd, m_w_ffn_in, m_w_ffn_out]),
        padded([v_w_qkv_even, v_w_o_even, v_w_qkvf_odd, v_w_o_odd, v_w_ffn_in, v_w_ffn_out]))

    def outputs(small, which):
        mix, fin, bias, ffn = small_out(small)
        qkv_e, o_e, qkvf, o_o, fi, fo = [big[f][which][:, :, :QKVF_COLS] if f == 2 else big[f][which] for f in range(6)]
        return [mix, qkv_e, o_e, qkvf, bias, o_o, ffn, fi, fo, fin]

    return (loss, dcur[None], *outputs(small_g, 0), *outputs(sd, 1), *outputs(snm, 2), *outputs(snv, 3))


def local_step(xs, target, norm_mix, norm_ffn, norm_final, b_forget, wts):
    s = xs.shape[0]
    tables = _tables_for(s)
    cm_qkv_e, cm_fi = wts["qkv_e"], wts["ffn_in"]
    wo_e, wo_o, wfo, wqkvf = wts["o_e"], wts["o_o"], wts["ffn_out"], wts["qkvf"]
    in_chip = lambda rows, cols, at: pl.BlockSpec((None, None, rows, cols), at)
    wf_pad = jnp.pad(wqkvf[:, :, 3 * D_ATTN:], ((0, 0), (0, 0), (0, LANES - N_HEADS)))
    wqkv_o = wqkvf[:, :, :3 * D_ATTN]
    wqkvf_pad = jnp.concatenate([wqkv_o, wf_pad], axis=2)
    bias_pad = jnp.pad(b_forget, ((0, 0), (0, LANES - N_HEADS)))

    saved = []
    cur = xs
    for layer in range(DEPTH):
        n = f"l{layer}"
        h1 = rmsnorm_fwd(cur, norm_mix[layer:layer + 1], n + "_norm_mix")
        keep = {"x": cur, "h1": h1}
        if layer % 2 == 0:
            qkv = matmul(h1, cm_qkv_e, "nn", BF16, n + "_qkv", 512, 768, 1024, mnk=(s, 3 * D_ATTN, D_MODEL),
                         b_spec=in_chip(D_MODEL, 768, lambda i, j, kk, li=layer // 2: (j, li, 0, 0)))
            o_sb, st_sb = causal_fwd(qkv, 4, "sb", n + "_sb_fwd")
            qd, kd, vd = rotary_prep(qkv, tables, n + "_rotary")
            o_dil, lse_dil = dilated_fwd(qd, kd, vd, n + "_dil_fwd")
            attn = jnp.concatenate([o_sb, o_dil], axis=1)
            keep.update(qkv=qkv, st=st_sb, qd=qd, kd=kd, vd=vd, o_dil=o_dil, lse_dil=lse_dil)
            w_out = wo_e[layer // 2]
        else:
            li = layer // 2
            qkv = matmul(h1, wqkv_o[li], "nn", BF16, n + "_qkv", 512, 768, 1024)
            fl = matmul(h1, wf_pad[li], "nn", F32, n + "_fgate", 512, LANES, 1024)
            cum = forget_fwd(fl, bias_pad[li:li + 1], n + "_forget_fwd")
            f_heads = cum[:, :N_HEADS].T
            fq = jnp.broadcast_to(f_heads[:, :, None], (N_HEADS, s, LANES))
            fk = f_heads.reshape(N_HEADS // 2, 2, s)
            attn, st_fox = causal_fwd(qkv, 8, "fox", n + "_fox_fwd", fq=fq, fk=fk)
            keep.update(qkv=qkv, st=st_fox, fl=fl, fq=fq, fk=fk)
            w_out = wo_o[layer // 2]
        mid = matmul(attn, w_out, "nn", F32, n + "_attn_out", 512, 1024, 1024, res=cur)
        h2 = rmsnorm_fwd(mid, norm_ffn[layer:layer + 1], n + "_norm_ffn")
        gu = matmul(h2, cm_fi, "nn", F32, n + "_ffn_in", 512, 1408, 1024, mnk=(s, 2 * D_FF, D_MODEL),
                    b_spec=in_chip(D_MODEL, 1408, lambda i, j, kk, layer=layer: (j, layer, 0, 0)))
        act = swiglu_fwd(gu, n + "_swiglu")
        cur = matmul(act, wfo[layer], "nn", F32, n + "_ffn_out", 512, 1024, D_FF, res=mid)
        keep.update(attn=attn, mid=mid, h2=h2, gu=gu, act=act)
        saved.append(keep)

    dcur, g_final, loss_part = loss_head(cur, norm_final.reshape(1, D_MODEL), target, "loss_head")

    g_mix, g_ffn = [None] * DEPTH, [None] * DEPTH
    g_bias = [None] * (DEPTH // 2)
    half = DEPTH // 2
    g_qkv_e = jnp.zeros((half, N_CHIPS, D_MODEL, 768), BF16)
    g_qkvf = jnp.zeros((half, N_CHIPS, D_MODEL, QKVF_PAD), BF16)
    g_o_e = jnp.zeros((half, D_ATTN, D_MODEL), BF16)
    g_o_o = jnp.zeros((half, D_ATTN, D_MODEL), BF16)
    g_fi = jnp.zeros((DEPTH, N_CHIPS, D_MODEL, 1408), BF16)
    g_fo = jnp.zeros((DEPTH, D_FF, D_MODEL), BF16)
    for layer in reversed(range(DEPTH)):
        n = f"l{layer}"
        kp = saved[layer]
        li = layer // 2
        g_fo = matmul(kp["act"], dcur, "tn", BF16, n + "_d_w_ffn_out", 1408, 1024, 512, into=g_fo,
                      o_spec=pl.BlockSpec((None, 1408, D_MODEL), lambda i, j, kk, layer=layer: (layer, i, 0)))
        dact = matmul(dcur, wfo[layer], "nt", F32, n + "_d_act", 512, 1408, 1024)
        dgu = swiglu_bwd(kp["gu"], dact, n + "_d_swiglu")
        g_fi = matmul(kp["h2"], dgu, "tn", BF16, n + "_d_w_ffn_in", 1024, 1408, 2048, into=g_fi,
                      o_spec=in_chip(D_MODEL, 1408, lambda i, j, kk, layer=layer: (layer, j, 0, 0)))
        dh2 = matmul(dgu, cm_fi, "nt", F32, n + "_d_h2", 512, 1024, 1408, mnk=(s, D_MODEL, 2 * D_FF),
                     b_spec=in_chip(D_MODEL, 1408, lambda i, j, kk, layer=layer: (kk, layer, 0, 0)))
        dmid, g_ffn[layer] = rmsnorm_bwd(kp["mid"], norm_ffn[layer:layer + 1], dh2, dcur, n + "_d_norm_ffn")
        w_out = wo_e[li] if layer % 2 == 0 else wo_o[li]
        g_out = matmul(kp["attn"], dmid, "tn", BF16, n + "_d_w_o", 1024, 1024, 512, into=g_o_e if layer % 2 == 0 else g_o_o,
                       o_spec=pl.BlockSpec((None, D_ATTN, D_MODEL), lambda i, j, kk, li=li: (li, 0, 0)))
        dattn = matmul(dmid, w_out, "nt", F32, n + "_d_attn", 512, 1024, 1024)
        if layer % 2 == 0:
            g_o_e = g_out
            dq_a, dk_a, dv_a = causal_bwd(kp["qkv"], dattn, kp["st"], 4, "sb", n + "_sb_bwd")
            dqd, dkd, dvd = dilated_bwd(kp["qd"], kp["kd"], kp["vd"], dattn, kp["o_dil"], kp["lse_dil"], 4, n + "_dil_bwd")
            dq_b, dk_b, dv_b = rotary_bwd(dqd, dkd, dvd, tables, n + "_d_rotary")
            dproj = jnp.concatenate([dq_a, dq_b, dk_a.astype(BF16), dk_b, dv_a.astype(BF16), dv_b], axis=1)
            g_qkv_e = matmul(kp["h1"], dproj, "tn", BF16, n + "_d_w_qkv", 1024, 768, 2048, into=g_qkv_e,
                             o_spec=in_chip(D_MODEL, 768, lambda i, j, kk, li=li: (li, j, 0, 0)))
            dh1 = matmul(dproj, cm_qkv_e, "nt", F32, n + "_d_h1", 512, 1024, 768, mnk=(s, D_MODEL, 3 * D_ATTN),
                         b_spec=in_chip(D_MODEL, 768, lambda i, j, kk, li=li: (kk, li, 0, 0)))
        else:
            g_o_o = g_out
            dq_f, dk_f, dv_f, dfk = causal_bwd(kp["qkv"], dattn, kp["st"], 8, "fox", n + "_fox_bwd", fq=kp["fq"], fk=kp["fk"])
            dcum = jnp.pad(dfk.reshape(N_HEADS, s).T, ((0, 0), (0, LANES - N_HEADS)))
            dfl, dbias = forget_bwd(kp["fl"], bias_pad[li:li + 1], dcum, n + "_forget_bwd")
            g_bias[li] = dbias[0, :N_HEADS]
            dproj = jnp.concatenate([dq_f, dk_f.astype(BF16), dv_f.astype(BF16), dfl.astype(BF16)], axis=1)
            by_chip = dproj[:, :N_CHIPS * QKVF_COLS].reshape(s, N_CHIPS, QKVF_COLS)
            by_chip = jnp.pad(by_chip, ((0, 0), (0, 0), (0, QKVF_PAD - QKVF_COLS))).reshape(s, N_CHIPS * QKVF_PAD)
            g_qkvf = matmul(kp["h1"], by_chip, "tn", BF16, n + "_d_w_qkv", 1024, QKVF_PAD, 2048, into=g_qkvf,
                            o_spec=in_chip(D_MODEL, QKVF_PAD, lambda i, j, kk, li=li: (li, j, 0, 0)))
            dh1 = matmul(dproj, wqkvf_pad[li], "nt", F32, n + "_d_h1", 512, 1024, 640)
        dcur, g_mix[layer] = rmsnorm_bwd(kp["x"], norm_mix[layer:layer + 1], dh1, dmid, n + "_d_norm_mix")

    rows_split = lambda g: g.reshape(g.shape[0], N_CHIPS, g.shape[1] // N_CHIPS, g.shape[2])
    fam_grads = [g_qkv_e, rows_split(g_o_e), g_qkvf, rows_split(g_o_o), g_fi, rows_split(g_fo)]
    return dcur, fam_grads, (g_mix, g_ffn, g_final, g_bias), loss_part
```

```python
import functools

import jax
import jax.numpy as jnp
from jax import lax
from jax.experimental import pallas as pl
from jax.experimental.pallas import tpu as pltpu

F32 = jnp.float32
BF16 = jnp.bfloat16
MESH = pl.DeviceIdType.MESH

D_MODEL = 1024
DEPTH = 4
HEAD_DIM = 64
N_HEADS = 16
D_ATTN = 1024
D_FF = 2816
ROPE_THETA = 500000.0
ROT_HALF = 8
RMS_EPS = 1e-5
DIL_STRIDES = (1, 4, 16)
ADAM_LR, ADAM_B1, ADAM_B2, ADAM_EPS, ADAM_WD, ADAM_STEP = 0.001, 0.9, 0.999, 1e-8, 0.01, 10

LANES = 128
BLK = 128
VMEM_LIMIT = 56 * 1024 * 1024
NEG = -1e30
N_CHIPS = 4
FLAT_COLS = 1024
FLAT_ROWS = 12800
HALF_ROWS = FLAT_ROWS // 2
SMALL_ROWS = 16


def _params(sem=None):
    return pltpu.CompilerParams(dimension_semantics=sem, vmem_limit_bytes=VMEM_LIMIT)


def _dot(a, b):
    return lax.dot_general(a, b, (((1,), (0,)), ((), ())), preferred_element_type=F32)


def _dot_nt(a, b):
    return lax.dot_general(a, b, (((1,), (1,)), ((), ())), preferred_element_type=F32)


def _dot_tn(a, b):
    return lax.dot_general(a, b, (((0,), (0,)), ((), ())), preferred_element_type=F32)


def _split3(x):
    x1 = x.astype(BF16)
    r1 = x - x1.astype(F32)
    x2 = r1.astype(BF16)
    x3 = (r1 - x2.astype(F32)).astype(BF16)
    return x1, x2, x3


def _dot_exact_lhs(x, t):
    x1, x2, x3 = _split3(x)
    return _dot(x1, t) + _dot(x2, t) + _dot(x3, t)


def _dot_exact_rhs(t, x):
    x1, x2, x3 = _split3(x)
    return _dot(t, x1) + _dot(t, x2) + _dot(t, x3)


def _iotas(shape=(BLK, LANES)):
    return lax.broadcasted_iota(jnp.int32, shape, 0), lax.broadcasted_iota(jnp.int32, shape, 1)


_DIMS = {"nn": (((1,), (0,)), ((), ())), "nt": (((1,), (1,)), ((), ())), "tn": (((0,), (0,)), ((), ()))}


def matmul(a, b, mode, out_dtype, name, tm, tn, tk, res=None, mnk=None, b_spec=None, o_spec=None, out_shape=None, into=None):
    if mnk is not None:
        m, n, k = mnk
    elif mode == "nn":
        (m, k), (k2, n) = a.shape, b.shape
    elif mode == "nt":
        (m, k), (n, k2) = a.shape, b.shape
    else:
        (k, m), (k2, n) = a.shape, b.shape
    assert m % tm == 0 and n % tn == 0 and k % tk == 0, (name, a.shape, b.shape)
    nk = k // tk
    a_spec = pl.BlockSpec((tk, tm), lambda i, j, kk: (kk, i)) if mode == "tn" else pl.BlockSpec((tm, tk), lambda i, j, kk: (i, kk))
    if b_spec is None:
        b_spec = pl.BlockSpec((tn, tk), lambda i, j, kk: (j, kk)) if mode == "nt" else pl.BlockSpec((tk, tn), lambda i, j, kk: (kk, j))
    r_spec = pl.BlockSpec((tm, tn), lambda i, j, kk: (i, j))
    if o_spec is None:
        o_spec = r_spec
    dims = _DIMS[mode]
    has_res = res is not None
    n_in = 2 + int(has_res) + int(into is not None)

    def body(*refs):
        a_ref, b_ref = refs[0], refs[1]
        r_ref = refs[2] if has_res else None
        o_ref = refs[n_in]

        def finish(v):
            if has_res:
                v = v + r_ref[...]
            o_ref[...] = v.astype(out_dtype)

        p = lax.dot_general(a_ref[...].astype(BF16), b_ref[...].astype(BF16), dims, preferred_element_type=F32)
        if nk == 1:
            finish(p)
        else:
            acc = refs[-1]
            kk = pl.program_id(2)

            @pl.when(kk == 0)
            def _():
                acc[...] = p

            @pl.when(kk > 0)
            def _():
                acc[...] += p

            @pl.when(kk == nk - 1)
            def _():
                finish(acc[...])

    ops = [a, b] + ([res] if has_res else []) + ([into] if into is not None else [])
    specs = [a_spec, b_spec] + ([r_spec] if has_res else []) + ([_ANY] if into is not None else [])
    if into is not None:
        out_shape = jax.ShapeDtypeStruct(into.shape, into.dtype)
    else:
        out_shape = jax.ShapeDtypeStruct((m, n) if out_shape is None else out_shape, out_dtype)
    return pl.pallas_call(
        body, name=name, out_shape=out_shape,
        grid=(m // tm, n // tn, nk), in_specs=specs, out_specs=o_spec,
        scratch_shapes=[pltpu.VMEM((tm, tn), F32)] if nk > 1 else [],
        input_output_aliases={n_in - 1: 0} if into is not None else {},
        compiler_params=_params(("parallel", "parallel", "arbitrary")),
    )(*ops)


ROWS = 256


def _row_spec(cols, rows=ROWS):
    return pl.BlockSpec((rows, cols), lambda i: (i, 0))


def _fix_spec(r, cols):
    return pl.BlockSpec((r, cols), lambda i: (0, 0))


def rmsnorm_fwd(x, g, name):
    s, d = x.shape

    def body(x_ref, g_ref, h_ref):
        xv = x_ref[...]
        rstd = lax.rsqrt(jnp.mean(xv * xv, axis=-1, keepdims=True) + RMS_EPS)
        h_ref[...] = (xv * rstd * g_ref[...]).astype(BF16)

    return pl.pallas_call(
        body, name=name, out_shape=jax.ShapeDtypeStruct((s, d), BF16), grid=(s // ROWS,),
        in_specs=[_row_spec(d), _fix_spec(1, d)], out_specs=_row_spec(d), compiler_params=_params(("parallel",)),
    )(x, g)


def _rms_bwd_math(xv, gv, dh):
    rstd = lax.rsqrt(jnp.mean(xv * xv, axis=-1, keepdims=True) + RMS_EPS)
    xhat = xv * rstd
    u = dh * gv
    dx = rstd * (u - xhat * jnp.mean(u * xhat, axis=-1, keepdims=True))
    return dx, dh * xhat


def rmsnorm_bwd(x, g, dh, dres, name):
    s, d = x.shape

    def body(x_ref, g_ref, dh_ref, dres_ref, dx_ref, dg_ref):
        dx, dgt = _rms_bwd_math(x_ref[...], g_ref[...], dh_ref[...])
        dx_ref[...] = dres_ref[...] + dx
        part = jnp.sum(dgt, axis=0, keepdims=True)

        @pl.when(pl.program_id(0) == 0)
        def _():
            dg_ref[...] = part

        @pl.when(pl.program_id(0) > 0)
        def _():
            dg_ref[...] += part

    return pl.pallas_call(
        body, name=name, out_shape=(jax.ShapeDtypeStruct((s, d), F32), jax.ShapeDtypeStruct((1, d), F32)),
        grid=(s // ROWS,), in_specs=[_row_spec(d), _fix_spec(1, d), _row_spec(d), _row_spec(d)],
        out_specs=(_row_spec(d), _fix_spec(1, d)), compiler_params=_params(("arbitrary",)),
    )(x, g, dh, dres)


def loss_head(x, g, target, name):
    s, d = x.shape

    def body(x_ref, g_ref, t_ref, dx_ref, dg_ref, loss_ref):
        xv, gv = x_ref[...], g_ref[...]
        rstd = lax.rsqrt(jnp.mean(xv * xv, axis=-1, keepdims=True) + RMS_EPS)
        err = xv * rstd * gv - t_ref[...]
        dx, dgt = _rms_bwd_math(xv, gv, err * (1.0 / d))
        dx_ref[...] = dx
        part = jnp.sum(dgt, axis=0, keepdims=True)
        lpart = jnp.full((1, LANES), 0.5 / d, F32) * jnp.sum(err * err)

        @pl.when(pl.program_id(0) == 0)
        def _():
            dg_ref[...] = part
            loss_ref[...] = lpart

        @pl.when(pl.program_id(0) > 0)
        def _():
            dg_ref[...] += part
            loss_ref[...] += lpart

    return pl.pallas_call(
        body, name=name,
        out_shape=(jax.ShapeDtypeStruct((s, d), F32), jax.ShapeDtypeStruct((1, d), F32), jax.ShapeDtypeStruct((1, LANES), F32)),
        grid=(s // ROWS,), in_specs=[_row_spec(d), _fix_spec(1, d), _row_spec(d)],
        out_specs=(_row_spec(d), _fix_spec(1, d), _fix_spec(1, LANES)), compiler_params=_params(("arbitrary",)),
    )(x, g, target)


def swiglu_fwd(gu, name):
    s, f2 = gu.shape
    f = f2 // 2

    def body(gu_ref, a_ref):
        gv, uv = gu_ref[:, :f], gu_ref[:, f:]
        a_ref[...] = (gv * (1.0 / (1.0 + jnp.exp(-gv))) * uv).astype(BF16)

    return pl.pallas_call(
        body, name=name, out_shape=jax.ShapeDtypeStruct((s, f), BF16), grid=(s // ROWS,),
        in_specs=[_row_spec(f2)], out_specs=_row_spec(f), compiler_params=_params(("parallel",)),
    )(gu)


def swiglu_bwd(gu, dact, name):
    s, f2 = gu.shape
    f = f2 // 2

    def body(gu_ref, da_ref, o_ref):
        gv, uv, da = gu_ref[:, :f], gu_ref[:, f:], da_ref[...]
        sg = 1.0 / (1.0 + jnp.exp(-gv))
        o_ref[:, :f] = (da * uv * sg * (1.0 + gv * (1.0 - sg))).astype(BF16)
        o_ref[:, f:] = (da * gv * sg).astype(BF16)

    return pl.pallas_call(
        body, name=name, out_shape=jax.ShapeDtypeStruct((s, f2), BF16), grid=(s // ROWS,),
        in_specs=[_row_spec(f2), _row_spec(f)], out_specs=_row_spec(f2), compiler_params=_params(("parallel",)),
    )(gu, dact)


Q_OFF, K_OFF, V_OFF = 0, 8, 16


KB = 512
SUB = KB // BLK


def _softplus_parts(z):
    sp = jnp.log(1.0 + jnp.exp(-jnp.abs(z)))
    ls = jnp.minimum(z, 0.0) - sp
    return ls, ls - z


def _wide(t):
    return jnp.concatenate([t] * SUB, axis=1)


def _chunk_dots(x, tri):
    terms = []
    for u in range(SUB):
        terms += list(_split3(x[:, u * BLK:(u + 1) * BLK]))
    r = _dot(jnp.concatenate(terms, axis=0), tri)
    piece = lambda n: r[n * BLK:(n + 1) * BLK]
    return [piece(3 * u) + piece(3 * u + 1) + piece(3 * u + 2) for u in range(SUB)]


def _block_suffix_sums(x, suffix, c):
    loc = _chunk_dots(x, suffix)
    out = [None] * SUB
    for u in reversed(range(SUB)):
        out[u] = loc[u] + c
        c = c + jnp.sum(x[:, u * BLK:(u + 1) * BLK], axis=1, keepdims=True)
    return jnp.concatenate(out, axis=1), c


def _block_prefix_sums(x, tri, c):
    loc = _chunk_dots(x, tri)
    out = []
    for u in range(SUB):
        out.append(loc[u] + c)
        c = c + jnp.sum(x[:, u * BLK:(u + 1) * BLK], axis=1, keepdims=True)
    return jnp.concatenate(out, axis=1), c


def causal_fwd(qkv, npairs, mode, name, fq=None, fk=None, rider=None):
    s = qkv.shape[0]
    nq = s // BLK
    fox = mode == "fox"

    def body(*refs):
        if fox:
            q_ref, k_ref, v_ref, fq_ref, fk_ref, o_ref, st_ref = refs
        else:
            q_ref, k_ref, v_ref, o_ref, st_ref = refs
        i = pl.program_id(1)
        nkb = i // SUB + 1
        row, lane = _iotas((BLK, KB))
        row_s, lane_s = _iotas()
        qpos = i * BLK + row
        qf = q_ref[...].astype(F32)
        hms = (lane_s < HEAD_DIM, lane_s >= HEAD_DIM)
        qas = [jnp.where(hm, qf, 0.0).astype(BF16) for hm in hms]
        suffix = jnp.where(row_s > lane_s, 1.0, 0.0).astype(BF16)
        zero = jnp.zeros((BLK, LANES), F32)
        col0 = jnp.zeros((BLK, 1), F32)

        def kv(j):
            r0 = pl.multiple_of(j * KB, KB)
            return r0, k_ref[pl.ds(r0, KB), :], v_ref[pl.ds(r0, KB), :]

        if fox:
            fqs = [_wide(fq_ref[a]) for a in range(2)]

            def step(j, carry):
                r0, kb, vb = kv(j)
                ok = r0 + lane <= qpos
                new = []
                for a in range(2):
                    acc, mx, l = carry[3 * a:3 * a + 3]
                    z = _dot_nt(qas[a], kb) * 0.125 + fqs[a] - fk_ref[a:a + 1, pl.ds(r0, KB)]
                    z = jnp.where(ok, z, NEG)
                    mnew = jnp.maximum(mx, jnp.max(z, axis=1, keepdims=True))
                    p = jnp.exp(z - mnew)
                    alpha = jnp.exp(mx - mnew)
                    new += [alpha * acc + _dot(p.astype(BF16), vb), mnew, alpha * l + jnp.sum(p, axis=1, keepdims=True)]
                return tuple(new)

            neg = jnp.full((BLK, 1), NEG, F32)
            res = lax.fori_loop(0, nkb, step, (zero, neg, col0, zero, neg, col0))
            outs = [res[3 * a] / res[3 * a + 2] for a in range(2)]
            stats = [res[3 * a + 1] + jnp.log(res[3 * a + 2]) for a in range(2)]
        else:
            def step(jj, carry):
                r0, kb, vb = kv(nkb - 1 - jj)
                strict = r0 + lane < qpos
                new = []
                for a in range(2):
                    acc, c = carry[2 * a:2 * a + 2]
                    ls, lm = _softplus_parts(_dot_nt(qas[a], kb) * 0.125)
                    lm = jnp.where(strict, lm, 0.0)
                    between, c = _block_suffix_sums(lm, suffix, c)
                    aw = jnp.where(strict, jnp.exp(ls + between), 0.0)
                    new += [acc + _dot(aw.astype(BF16), vb), c]
                return tuple(new)

            res = lax.fori_loop(0, nkb, step, (zero, col0, zero, col0))
            outs, stats = [res[0], res[2]], [res[1], res[3]]
        o_ref[...] = jnp.where(hms[0], outs[0], outs[1])
        for a in range(2):
            st_ref[a] = jnp.broadcast_to(stats[a], (BLK, LANES))

    col = lambda off: (lambda p, i: (0, off + p))
    in_specs = [pl.BlockSpec((BLK, LANES), lambda p, i: (i, Q_OFF + p)),
                pl.BlockSpec((s, LANES), col(K_OFF)), pl.BlockSpec((s, LANES), col(V_OFF))]
    ops = [qkv, qkv, qkv]
    if fox:
        in_specs += [pl.BlockSpec((2, BLK, LANES), lambda p, i: (p, i, 0)), pl.BlockSpec((None, 2, s), lambda p, i: (p, 0, 0))]
        ops += [fq, fk]
    (o, stat), rode = call_with_rider(
        body, name, rider, ops, in_specs,
        [jax.ShapeDtypeStruct((s, npairs * LANES), F32), jax.ShapeDtypeStruct((2 * npairs, s, LANES), F32)],
        [pl.BlockSpec((BLK, LANES), lambda p, i: (i, p)), pl.BlockSpec((2, BLK, LANES), lambda p, i: (p, i, 0))], [], (npairs, nq))
    return o, stat, rode


def causal_bwd(qkv, do, stat, npairs, mode, name, fq=None, fk=None, rider=None):
    s = qkv.shape[0]
    nq = s // BLK
    fox = mode == "fox"

    def body(*refs):
        if fox:
            q_ref, k_ref, v_ref, do_ref, st_ref, fq_ref, fk_ref, dq_ref, dk_ref, dv_ref, df_ref, p_s, dp_s = refs
        else:
            q_ref, k_ref, v_ref, do_ref, st_ref, dq_ref, dk_ref, dv_ref = refs
        i = pl.program_id(1)

        @pl.when(i == 0)
        def _():
            dk_ref[...] = jnp.zeros_like(dk_ref)
            dv_ref[...] = jnp.zeros_like(dv_ref)
            if fox:
                df_ref[...] = jnp.zeros_like(df_ref)

        nkb = i // SUB + 1
        row, lane = _iotas((BLK, KB))
        row_s, lane_s = _iotas()
        qpos = i * BLK + row
        qf = q_ref[...].astype(F32)
        dov = do_ref[...]
        hms = (lane_s < HEAD_DIM, lane_s >= HEAD_DIM)
        qas = [jnp.where(hm, qf, 0.0).astype(BF16) for hm in hms]
        doas = [jnp.where(hm, dov, 0.0).astype(BF16) for hm in hms]
        stas = [_wide(st_ref[a]) for a in range(2)]
        zero = jnp.zeros((BLK, LANES), F32)
        col0 = jnp.zeros((BLK, 1), F32)

        def kv(j):
            r0 = pl.multiple_of(j * KB, KB)
            return r0, k_ref[pl.ds(r0, KB), :], v_ref[pl.ds(r0, KB), :]

        if fox:
            fqs = [_wide(fq_ref[a]) for a in range(2)]

            def probs(j, deltas):
                r0, kb, vb = kv(j)
                ok = r0 + lane <= qpos
                new = []
                for a in range(2):
                    z = _dot_nt(qas[a], kb) * 0.125 + fqs[a] - fk_ref[a:a + 1, pl.ds(r0, KB)]
                    p = jnp.where(ok, jnp.exp(z - stas[a]), 0.0)
                    dp = _dot_nt(doas[a], vb)
                    p_s[a, j] = p
                    dp_s[a, j] = dp
                    new.append(deltas[a] + jnp.sum(p * dp, axis=1, keepdims=True))
                return tuple(new)

            deltas = lax.fori_loop(0, nkb, probs, (col0, col0))

            def step(j, dqs):
                r0, kb, _ = kv(j)
                new = []
                dk = jnp.zeros((KB, LANES), F32)
                dv = jnp.zeros((KB, LANES), F32)
                for a in range(2):
                    p = p_s[a, j]
                    ds = p * (dp_s[a, j] - deltas[a])
                    dsb = (ds * 0.125).astype(BF16)
                    dk += _dot_tn(dsb, qas[a])
                    dv += _dot_tn(p.astype(BF16), doas[a])
                    df_ref[a:a + 1, pl.ds(r0, KB)] -= jnp.sum(ds, axis=0, keepdims=True)
                    new.append(dqs[a] + _dot(dsb, kb))
                dk_ref[pl.ds(r0, KB), :] += dk
                dv_ref[pl.ds(r0, KB), :] += dv
                return tuple(new)

            dqs = lax.fori_loop(0, nkb, step, (zero, zero))
        else:
            incl = jnp.where(row_s <= lane_s, 1.0, 0.0).astype(BF16)
            excl = jnp.where(row_s < lane_s, 1.0, 0.0).astype(BF16)

            def step(j, carry):
                r0, kb, vb = kv(j)
                strict = r0 + lane < qpos
                new = []
                dk = jnp.zeros((KB, LANES), F32)
                dv = jnp.zeros((KB, LANES), F32)
                for a in range(2):
                    dq, cm, cg = carry[3 * a:3 * a + 3]
                    ls, lm = _softplus_parts(_dot_nt(qas[a], kb) * 0.125)
                    lm = jnp.where(strict, lm, 0.0)
                    beta = jnp.exp(ls)
                    upto, cm = _block_prefix_sums(lm, incl, cm)
                    aw = jnp.where(strict, jnp.exp(ls + stas[a] - upto), 0.0)
                    g = aw * _dot_nt(doas[a], vb)
                    pre, cg = _block_prefix_sums(g, excl, cg)
                    dz = jnp.where(strict, g * (1.0 - beta) - pre * beta, 0.0)
                    dzb = (dz * 0.125).astype(BF16)
                    dk += _dot_tn(dzb, qas[a])
                    dv += _dot_tn(aw.astype(BF16), doas[a])
                    new += [dq + _dot(dzb, kb), cm, cg]
                dk_ref[pl.ds(r0, KB), :] += dk
                dv_ref[pl.ds(r0, KB), :] += dv
                return tuple(new)

            res = lax.fori_loop(0, nkb, step, (zero, col0, col0, zero, col0, col0))
            dqs = (res[0], res[3])
        dq_ref[...] = jnp.where(hms[0], dqs[0], dqs[1]).astype(BF16)

    col = lambda off: (lambda p, i: (0, off + p))
    blk = pl.BlockSpec((BLK, LANES), lambda p, i: (i, p))
    acc = pl.BlockSpec((s, LANES), lambda p, i: (0, p))
    st_spec = pl.BlockSpec((2, BLK, LANES), lambda p, i: (p, i, 0))
    in_specs = [pl.BlockSpec((BLK, LANES), lambda p, i: (i, Q_OFF + p)), pl.BlockSpec((s, LANES), col(K_OFF)),
                pl.BlockSpec((s, LANES), col(V_OFF)), blk, st_spec]
    ops = [qkv, qkv, qkv, do, stat]
    w = npairs * LANES
    out_shape = [jax.ShapeDtypeStruct((s, w), BF16), jax.ShapeDtypeStruct((s, w), F32), jax.ShapeDtypeStruct((s, w), F32)]
    out_specs = [blk, acc, acc]
    scratch = []
    if fox:
        fk_spec = pl.BlockSpec((None, 2, s), lambda p, i: (p, 0, 0))
        in_specs += [st_spec, fk_spec]
        ops += [fq, fk]
        out_shape.append(jax.ShapeDtypeStruct((npairs, 2, s), F32))
        out_specs.append(fk_spec)
        scratch = [pltpu.VMEM((2, s // KB, BLK, KB), F32)] * 2
    outs, rode = call_with_rider(body, name, rider, ops, in_specs, out_shape, out_specs, scratch, (npairs, nq))
    return (*outs, rode)


def forget_fwd(fl, bias, name):
    s = fl.shape[0]

    def body(fl_ref, b_ref, f_ref):
        row, lane = _iotas()
        lower = jnp.where(lane <= row, 1.0, 0.0).astype(BF16)

        def step(n, carry):
            r0 = pl.multiple_of(n * BLK, BLK)
            ls, _ = _softplus_parts(fl_ref[pl.ds(r0, BLK), :] + b_ref[...])
            blk = _dot_exact_rhs(lower, ls) + carry
            f_ref[pl.ds(r0, BLK), :] = blk
            return blk[BLK - 1:BLK, :]

        lax.fori_loop(0, s // BLK, step, jnp.zeros((1, LANES), F32))

    return pl.pallas_call(
        body, name=name, out_shape=jax.ShapeDtypeStruct((s, LANES), F32),
        in_specs=[pl.BlockSpec(memory_space=pltpu.VMEM)] * 2, out_specs=pl.BlockSpec(memory_space=pltpu.VMEM),
        compiler_params=_params(),
    )(fl, bias)


def forget_bwd(fl, bias, df, name):
    s = fl.shape[0]
    nb = s // BLK

    def body(fl_ref, b_ref, df_ref, o_ref, db_ref):
        row, lane = _iotas()
        upper = jnp.where(lane >= row, 1.0, 0.0).astype(BF16)

        def step(nn, carry):
            tail, db = carry
            r0 = pl.multiple_of((nb - 1 - nn) * BLK, BLK)
            dls = _dot_exact_rhs(upper, df_ref[pl.ds(r0, BLK), :]) + tail
            xv = fl_ref[pl.ds(r0, BLK), :] + b_ref[...]
            dfl = dls * (1.0 / (1.0 + jnp.exp(xv)))
            o_ref[pl.ds(r0, BLK), :] = dfl
            return dls[0:1, :], db + jnp.sum(dfl, axis=0, keepdims=True)

        _, db = lax.fori_loop(0, nb, step, (jnp.zeros((1, LANES), F32), jnp.zeros((1, LANES), F32)))
        db_ref[...] = db

    return pl.pallas_call(
        body, name=name, out_shape=(jax.ShapeDtypeStruct((s, LANES), F32), jax.ShapeDtypeStruct((1, LANES), F32)),
        in_specs=[pl.BlockSpec(memory_space=pltpu.VMEM)] * 3,
        out_specs=(pl.BlockSpec(memory_space=pltpu.VMEM), pl.BlockSpec(memory_space=pltpu.VMEM)),
        compiler_params=_params(),
    )(fl, bias, df)


def _rot_tables(s):
    inv = ROPE_THETA ** (-jnp.arange(ROT_HALF, dtype=F32) * 2.0 / (2 * ROT_HALF))
    ang = jnp.arange(s, dtype=F32)[:, None] * inv[None, :]
    cos, sin = jnp.cos(ang), jnp.sin(ang)
    z8 = jnp.zeros((s, ROT_HALF), F32)
    rest = HEAD_DIM - 2 * ROT_HALF
    zr, onr = jnp.zeros((s, rest), F32), jnp.ones((s, rest), F32)
    tile = lambda t: jnp.tile(t, (1, 2))
    return tile(jnp.concatenate([cos, cos, onr], 1)), tile(jnp.concatenate([-sin, z8, zr], 1)), tile(jnp.concatenate([z8, sin, zr], 1))


def rotary_prep(qkv, tables, name):
    s = qkv.shape[0]
    w = 4 * LANES

    def body(q_ref, k_ref, v_ref, c_ref, s1_ref, s2_ref, qo_ref, ko_ref, vo_ref):
        c, s1, s2 = c_ref[...], s1_ref[...], s2_ref[...]

        def rot(xv):
            return xv * c + pltpu.roll(xv, LANES - ROT_HALF, 1) * s1 + pltpu.roll(xv, ROT_HALF, 1) * s2

        qo_ref[...] = rot(q_ref[...].astype(F32)) * 0.125
        ko_ref[...] = rot(k_ref[...].astype(F32))
        vo_ref[...] = v_ref[...].astype(F32)

    cb = lambda off: pl.BlockSpec((ROWS, LANES), lambda i, j: (i, off + j))
    tb = pl.BlockSpec((ROWS, LANES), lambda i, j: (i, 0))
    out = jax.ShapeDtypeStruct((s, w), F32)
    return pl.pallas_call(
        body, name=name, out_shape=(out, out, out), grid=(s // ROWS, 4),
        in_specs=[cb(Q_OFF + 4), cb(K_OFF + 4), cb(V_OFF + 4), tb, tb, tb], out_specs=(cb(0), cb(0), cb(0)),
        compiler_params=_params(("parallel", "parallel")),
    )(qkv, qkv, qkv, *tables)


def rotary_bwd(dq, dk, dv, tables, name):
    s, w = dq.shape

    def body(dq_ref, dk_ref, dv_ref, c_ref, s1_ref, s2_ref, qo_ref, ko_ref, vo_ref):
        c, s1, s2 = c_ref[...], s1_ref[...], s2_ref[...]

        def rot_t(dy):
            return dy * c + pltpu.roll(dy * s1, ROT_HALF, 1) + pltpu.roll(dy * s2, LANES - ROT_HALF, 1)

        qo_ref[...] = (rot_t(dq_ref[...]) * 0.125).astype(BF16)
        ko_ref[...] = rot_t(dk_ref[...]).astype(BF16)
        vo_ref[...] = dv_ref[...].astype(BF16)

    cb = pl.BlockSpec((ROWS, LANES), lambda i, j: (i, j))
    tb = pl.BlockSpec((ROWS, LANES), lambda i, j: (i, 0))
    out = jax.ShapeDtypeStruct((s, w), BF16)
    return pl.pallas_call(
        body, name=name, out_shape=(out, out, out), grid=(s // ROWS, w // LANES),
        in_specs=[cb, cb, cb, tb, tb, tb], out_specs=(cb, cb, cb), compiler_params=_params(("parallel", "parallel")),
    )(dq, dk, dv, *tables)


def _deinterleave(dst, src_ref, stride, s, dtype):
    length = s // stride
    for r in range(stride):
        if stride == 1:
            dst[...] = src_ref[...].astype(dtype)
        else:
            dst[r * length:(r + 1) * length, :] = src_ref[pl.ds(r, length, stride=stride), :].astype(dtype)


def _band_masks(row, lane, first):
    return lane <= row, lane >= row + jnp.where(first, BLK, 0)


def dilated_fwd(qd, kd, vd, name):
    s, w = qd.shape
    npairs = w // LANES
    nblk = s // BLK

    def body(q_ref, k_ref, v_ref, o_ref, lse_ref, qs, ks, vs, od, ld, on, ln):
        row, lane = _iotas()
        for pi, stride in enumerate(DIL_STRIDES):
            per = (s // stride) // BLK
            _deinterleave(qs, q_ref, stride, s, BF16)
            _deinterleave(ks, k_ref, stride, s, BF16)
            _deinterleave(vs, v_ref, stride, s, BF16)

            def block(b, carry):
                r0 = pl.multiple_of(b * BLK, BLK)
                rp = pl.multiple_of(jnp.maximum(b - 1, 0) * BLK, BLK)
                mc, mp = _band_masks(row, lane, b % per == 0)
                q = qs[pl.ds(r0, BLK), :]
                kc, kp, vc, vp = ks[pl.ds(r0, BLK), :], ks[pl.ds(rp, BLK), :], vs[pl.ds(r0, BLK), :], vs[pl.ds(rp, BLK), :]
                out = jnp.zeros((BLK, LANES), F32)
                lse = jnp.zeros((BLK, LANES), F32)
                for a in range(2):
                    hm = (lane < HEAD_DIM) if a == 0 else (lane >= HEAD_DIM)
                    qa = jnp.where(hm, q.astype(F32), 0.0).astype(BF16)
                    sc = jnp.where(mc, _dot_nt(qa, kc), NEG)
                    sp = jnp.where(mp, _dot_nt(qa, kp), NEG)
                    mx = jnp.maximum(jnp.max(sc, axis=1, keepdims=True), jnp.max(sp, axis=1, keepdims=True))
                    pc, pp = jnp.exp(sc - mx), jnp.exp(sp - mx)
                    l = jnp.sum(pc, axis=1, keepdims=True) + jnp.sum(pp, axis=1, keepdims=True)
                    oa = (_dot(pc.astype(BF16), vc) + _dot(pp.astype(BF16), vp)) / l
                    out = jnp.where(hm, oa, out)
                    lse = jnp.where(hm, mx + jnp.log(l), lse)
                od[pl.ds(r0, BLK), :] = out
                ld[pl.ds(r0, BLK), :] = lse
                return carry

            lax.fori_loop(0, nblk, block, 0)
            length = s // stride
            for r in range(stride):
                if stride == 1:
                    on[pi] = od[...]
                    ln[pi] = ld[...]
                else:
                    on[pi, pl.ds(r, length, stride=stride), :] = od[r * length:(r + 1) * length, :]
                    ln[pi, pl.ds(r, length, stride=stride), :] = ld[r * length:(r + 1) * length, :]

        def merge(n, carry):
            r0 = pl.multiple_of(n * BLK, BLK)
            ls = [ln[pi, pl.ds(r0, BLK), :] for pi in range(3)]
            mx = jnp.maximum(jnp.maximum(ls[0], ls[1]), ls[2])
            ws = [jnp.exp(lv - mx) for lv in ls]
            den = ws[0] + ws[1] + ws[2]
            num = ws[0] * on[0, pl.ds(r0, BLK), :] + ws[1] * on[1, pl.ds(r0, BLK), :] + ws[2] * on[2, pl.ds(r0, BLK), :]
            o_ref[pl.ds(r0, BLK), :] = num / den
            lse_ref[pl.ds(r0, BLK), :] = mx + jnp.log(den)
            return carry

        lax.fori_loop(0, nblk, merge, 0)

    colspec = pl.BlockSpec((s, LANES), lambda p: (0, p))
    out = jax.ShapeDtypeStruct((s, w), F32)
    return pl.pallas_call(
        body, name=name, out_shape=(out, out), grid=(npairs,), in_specs=[colspec] * 3, out_specs=(colspec, colspec),
        scratch_shapes=[pltpu.VMEM((s, LANES), BF16)] * 3 + [pltpu.VMEM((s, LANES), F32)] * 2 + [pltpu.VMEM((3, s, LANES), F32)] * 2,
        compiler_params=_params(("parallel",)),
    )(qd, kd, vd)


def dilated_bwd(qd, kd, vd, do, out, lse, do_off, name):
    s, w = qd.shape
    npairs = w // LANES
    nblk = s // BLK

    def body(q_ref, k_ref, v_ref, do_ref, out_ref, lse_ref, dq_ref, dk_ref, dv_ref, qs, ks, vs, dos, dls, lss, dqd, dkd, dvd, dln):
        row, lane = _iotas()
        same_head = jnp.where((row < HEAD_DIM) == (lane < HEAD_DIM), 1.0, 0.0).astype(BF16)

        def delta_blk(n, carry):
            r0 = pl.multiple_of(n * BLK, BLK)
            dln[pl.ds(r0, BLK), :] = _dot_exact_lhs(do_ref[pl.ds(r0, BLK), :] * out_ref[pl.ds(r0, BLK), :], same_head)
            return carry

        lax.fori_loop(0, nblk, delta_blk, 0)
        for pi, stride in enumerate(DIL_STRIDES):
            per = (s // stride) // BLK
            _deinterleave(qs, q_ref, stride, s, BF16)
            _deinterleave(ks, k_ref, stride, s, BF16)
            _deinterleave(vs, v_ref, stride, s, BF16)
            _deinterleave(dos, do_ref, stride, s, BF16)
            _deinterleave(dls, dln, stride, s, F32)
            _deinterleave(lss, lse_ref, stride, s, F32)

            def block(b, carry):
                r0 = pl.multiple_of(b * BLK, BLK)
                rp = pl.multiple_of(jnp.maximum(b - 1, 0) * BLK, BLK)
                first = b % per == 0
                mc, mp = _band_masks(row, lane, first)
                q, dov = qs[pl.ds(r0, BLK), :], dos[pl.ds(r0, BLK), :]
                kc, kp, vc, vp = ks[pl.ds(r0, BLK), :], ks[pl.ds(rp, BLK), :], vs[pl.ds(r0, BLK), :], vs[pl.ds(rp, BLK), :]
                lse_t, dl_t = lss[pl.ds(r0, BLK), :], dls[pl.ds(r0, BLK), :]
                dq = jnp.zeros((BLK, LANES), F32)
                dkc = jnp.zeros((BLK, LANES), F32)
                dkp = jnp.zeros((BLK, LANES), F32)
                dvc = jnp.zeros((BLK, LANES), F32)
                dvp = jnp.zeros((BLK, LANES), F32)
                for a in range(2):
                    hm = (lane < HEAD_DIM) if a == 0 else (lane >= HEAD_DIM)
                    pick = lane == a * HEAD_DIM
                    qa = jnp.where(hm, q.astype(F32), 0.0).astype(BF16)
                    doa = jnp.where(hm, dov.astype(F32), 0.0).astype(BF16)
                    lse_a = jnp.sum(jnp.where(pick, lse_t, 0.0), axis=1, keepdims=True)
                    dl_a = jnp.sum(jnp.where(pick, dl_t, 0.0), axis=1, keepdims=True)
                    pc = jnp.where(mc, jnp.exp(_dot_nt(qa, kc) - lse_a), 0.0)
                    pp = jnp.where(mp, jnp.exp(_dot_nt(qa, kp) - lse_a), 0.0)
                    dsc = (pc * (_dot_nt(doa, vc) - dl_a)).astype(BF16)
                    dsp = (pp * (_dot_nt(doa, vp) - dl_a)).astype(BF16)
                    dq = jnp.where(hm, _dot(dsc, kc) + _dot(dsp, kp), dq)
                    dkc += _dot_tn(dsc, qa)
                    dkp += _dot_tn(dsp, qa)
                    dvc += _dot_tn(pc.astype(BF16), doa)
                    dvp += _dot_tn(pp.astype(BF16), doa)
                dqd[pl.ds(r0, BLK), :] = dq
                dkd[pl.ds(r0, BLK), :] = dkc
                dvd[pl.ds(r0, BLK), :] = dvc

                @pl.when(jnp.logical_not(first))
                def _():
                    dkd[pl.ds(rp, BLK), :] += dkp
                    dvd[pl.ds(rp, BLK), :] += dvp

                return carry

            lax.fori_loop(0, nblk, block, 0)
            length = s // stride
            for dst, src in ((dq_ref, dqd), (dk_ref, dkd), (dv_ref, dvd)):
                for r in range(stride):
                    if stride == 1:
                        dst[...] = src[...]
                    else:
                        dst[pl.ds(r, length, stride=stride), :] += src[r * length:(r + 1) * length, :]

    colspec = pl.BlockSpec((s, LANES), lambda p: (0, p))
    do_spec = pl.BlockSpec((s, LANES), lambda p: (0, do_off + p))
    o3 = jax.ShapeDtypeStruct((s, w), F32)
    return pl.pallas_call(
        body, name=name, out_shape=(o3, o3, o3), grid=(npairs,),
        in_specs=[colspec, colspec, colspec, do_spec, colspec, colspec], out_specs=(colspec, colspec, colspec),
        scratch_shapes=[pltpu.VMEM((s, LANES), BF16)] * 4 + [pltpu.VMEM((s, LANES), F32)] * 6,
        compiler_params=_params(("parallel",)),
    )(qd, kd, vd, do, out, lse)


def adamw(w, g, m, v, name):
    rows, cols = w.shape
    rb = min(rows, ROWS)
    c1 = 1.0 - ADAM_B1 ** ADAM_STEP
    c2 = 1.0 - ADAM_B2 ** ADAM_STEP

    def body(w_ref, g_ref, m_ref, v_ref, d_ref, mo_ref, vo_ref):
        gv = g_ref[...]
        mn = ADAM_B1 * m_ref[...] + (1.0 - ADAM_B1) * gv
        vn = ADAM_B2 * v_ref[...] + (1.0 - ADAM_B2) * (gv * gv)
        d_ref[...] = -ADAM_LR * ((mn / c1) / (jnp.sqrt(vn / c2) + ADAM_EPS) + ADAM_WD * w_ref[...])
        mo_ref[...] = mn
        vo_ref[...] = vn

    spec = _row_spec(cols, rb)
    out = jax.ShapeDtypeStruct((rows, cols), F32)
    return pl.pallas_call(
        body, name=name, out_shape=(out, out, out), grid=(rows // rb,), in_specs=[spec] * 4, out_specs=(spec,) * 3,
        compiler_params=_params(("parallel",)),
    )(w, g, m, v)


def _prefetch_call(body, name, scalar, ops, grid, in_specs, out_specs, out_shape, sem):
    spec = pltpu.PrefetchScalarGridSpec(num_scalar_prefetch=1, grid=grid, in_specs=in_specs, out_specs=out_specs)
    return pl.pallas_call(body, name=name, grid_spec=spec, out_shape=out_shape, compiler_params=_params(sem))(scalar, *ops)


def pair_sum(g, got, core, name):
    nc, r, c = g.shape
    rh = r // 2

    def body(core_ref, g_ref, got_ref, o_ref):
        o_ref[...] = (g_ref[...].astype(F32) + got_ref[...].astype(F32)).astype(BF16)

    blk = lambda rows_of: pl.BlockSpec((None, rh, c), rows_of)
    return _prefetch_call(
        body, name, core, (g, got), (nc,),
        [blk(lambda j, core_ref: (j, core_ref[0], 0)), blk(lambda j, core_ref: (j, 0, 0))],
        blk(lambda j, core_ref: (j, 0, 0)), jax.ShapeDtypeStruct((nc, rh, c), BF16), ("parallel",))


def chip_sum(pair, got, chip, layer, into, name):
    _, rh, c = pair.shape

    def body(chip_ref, p_ref, a_ref, b_ref, c_ref, old_ref, o_ref):
        o_ref[...] = ((p_ref[...].astype(F32) + a_ref[...].astype(F32)) + b_ref[...].astype(F32)) + c_ref[...].astype(F32)

    arrival = lambda k: pl.BlockSpec((None, rh, c), lambda i, chip_ref: (k, 0, 0))
    spec = pltpu.PrefetchScalarGridSpec(
        num_scalar_prefetch=1, grid=(1,),
        in_specs=[pl.BlockSpec((None, rh, c), lambda i, chip_ref: (chip_ref[0], 0, 0)), arrival(0), arrival(1), arrival(2), _ANY],
        out_specs=pl.BlockSpec((None, rh, c), lambda i, chip_ref: (layer, 0, 0)))
    return pl.pallas_call(body, name=name, grid_spec=spec, out_shape=jax.ShapeDtypeStruct(into.shape, into.dtype),
                          input_output_aliases={5: 0}, compiler_params=_params(("arbitrary",)))(chip, pair, got, got, got, into)


def adamw_family(w, m, v, g_mine, g_other, core, name):
    nl, r, c = w.shape
    rh = r // 2
    nb = 4 if rh % 512 == 0 else (2 if rh % 16 == 0 and rh > 256 else 1)
    rb = rh // nb
    c1 = 1.0 - ADAM_B1 ** ADAM_STEP
    c2 = 1.0 - ADAM_B2 ** ADAM_STEP

    def body(core_ref, w_ref, m_ref, v_ref, gm_ref, go_ref, g_ref, d_ref, mo_ref, vo_ref):
        gv = jnp.where(pl.program_id(1) == core_ref[0], gm_ref[...], go_ref[...])
        mn = ADAM_B1 * m_ref[...] + (1.0 - ADAM_B1) * gv
        vn = ADAM_B2 * v_ref[...] + (1.0 - ADAM_B2) * (gv * gv)
        g_ref[...] = gv
        d_ref[...] = -ADAM_LR * ((mn / c1) / (jnp.sqrt(vn / c2) + ADAM_EPS) + ADAM_WD * w_ref[...])
        mo_ref[...] = mn
        vo_ref[...] = vn

    full = pl.BlockSpec((None, rb, c), lambda l, h, i, core_ref: (l, h * nb + i, 0))
    half = pl.BlockSpec((None, rb, c), lambda l, h, i, core_ref: (l, i, 0))
    out = jax.ShapeDtypeStruct((nl, r, c), F32)
    return _prefetch_call(body, name, core, (w, m, v, g_mine, g_other), (nl, 2, nb), [full, full, full, half, half],
                          (full, full, full, full), (out, out, out, out), ("parallel", "parallel", "parallel"))


def _coords():
    return lax.axis_index("x"), lax.axis_index("y"), lax.axis_index("c")


def _other_chips(x, y):
    return ((1 - x, y), (x, 1 - y), (1 - x, 1 - y))


_ANY = pl.BlockSpec(memory_space=pl.ANY)


def _exchange_call(body, name, arrays, out_shapes, n_copies, n_local=0):
    n = len(arrays)

    def wrapped(*refs):
        body(refs[:n], refs[n:n + len(out_shapes)], *refs[n + len(out_shapes):])

    scratch = [pltpu.SemaphoreType.DMA((n_copies,)), pltpu.SemaphoreType.DMA((n_copies,))]
    if n_local:
        scratch.append(pltpu.SemaphoreType.DMA((n_local,)))
    return pl.pallas_call(
        wrapped, name=name, out_shape=tuple(out_shapes), in_specs=[_ANY] * n, out_specs=tuple([_ANY] * len(out_shapes)),
        scratch_shapes=scratch, compiler_params=_params(),
    )(*arrays)


def _remote(send_sems, recv_sems, n, src, dst, to):
    return pltpu.make_async_remote_copy(src_ref=src, dst_ref=dst, send_sem=send_sems.at[n], recv_sem=recv_sems.at[n],
                                        device_id=to, device_id_type=MESH)


class Rider:
    def __init__(self, arrays, out_shapes, n_remote, n_local, copies):
        self.arrays, self.out_shapes, self.n_remote, self.n_local, self.copies = list(arrays), list(out_shapes), n_remote, n_local, copies

    def sems(self):
        return [pltpu.SemaphoreType.DMA((self.n_remote,)), pltpu.SemaphoreType.DMA((self.n_remote,)),
                pltpu.SemaphoreType.DMA((max(self.n_local, 1),))]

    def run(self, name):
        n, no = len(self.arrays), len(self.out_shapes)

        def body(*refs):
            cps = self.copies(refs[:n], refs[n:n + no], *refs[n + no:])
            for cp in cps:
                cp.start()
            for cp in cps:
                cp.wait()

        return pl.pallas_call(
            body, name=name, out_shape=tuple(self.out_shapes), in_specs=[_ANY] * n, out_specs=tuple([_ANY] * no),
            scratch_shapes=self.sems(), compiler_params=_params(),
        )(*self.arrays)


def ride(rider, body, n_in, n_out, grid):
    if rider is None:
        return body
    ni, no = len(rider.arrays), len(rider.out_shapes)

    def wrapped(*refs):
        ins, r_in = refs[:n_in], refs[n_in:n_in + ni]
        outs = refs[n_in + ni:n_in + ni + n_out]
        r_out = refs[n_in + ni + n_out:n_in + ni + n_out + no]
        rest = refs[n_in + ni + n_out + no:]
        scratch, sems = rest[:len(rest) - 3], rest[len(rest) - 3:]
        ids = [pl.program_id(a) for a in range(len(grid))]
        first = functools.reduce(jnp.logical_and, [i == 0 for i in ids])
        last = functools.reduce(jnp.logical_and, [i == g - 1 for i, g in zip(ids, grid)])

        @pl.when(first)
        def _():
            for cp in rider.copies(r_in, r_out, *sems):
                cp.start()

        body(*ins, *outs, *scratch)

        @pl.when(last)
        def _():
            for cp in rider.copies(r_in, r_out, *sems):
                cp.wait()

    return wrapped


def call_with_rider(body, name, rider, ops, in_specs, out_shape, out_specs, scratch, grid):
    n_in, n_out = len(ops), len(out_shape)
    ops, in_specs, out_shape, out_specs, scratch = list(ops), list(in_specs), list(out_shape), list(out_specs), list(scratch)
    if rider is not None:
        ops += rider.arrays
        in_specs += [_ANY] * len(rider.arrays)
        out_shape += rider.out_shapes
        out_specs += [_ANY] * len(rider.out_shapes)
        scratch += rider.sems()
    res = pl.pallas_call(
        ride(rider, body, n_in, n_out, grid), name=name, out_shape=tuple(out_shape), grid=grid, in_specs=in_specs,
        out_specs=tuple(out_specs), scratch_shapes=scratch, compiler_params=_params(("arbitrary",) * len(grid)),
    )(*ops)
    return tuple(res[:n_out]), list(res[n_out:])


def gather_rider(shards):
    nf = len(shards)

    def copies(s_refs, o_refs, send_sems, recv_sems, local_sems):
        x, y, c = _coords()
        me = 2 * x + y
        cps = [pltpu.make_async_copy(s_refs[f], o_refs[f].at[me], local_sems.at[f]) for f in range(nf)]
        for k, (px, py) in enumerate(_other_chips(x, y)):
            for f in range(nf):
                cps.append(_remote(send_sems, recv_sems, k * nf + f, s_refs[f], o_refs[f].at[me], (px, py, c)))
        return cps

    return Rider(shards, [jax.ShapeDtypeStruct((N_CHIPS,) + sh.shape, sh.dtype) for sh in shards], 3 * nf, nf, copies)


def scatter_rider(pairs):
    nf = len(pairs)

    def copies(p_refs, o_refs, send_sems, recv_sems, local_sems):
        x, y, c = _coords()
        cps = []
        for k, (px, py) in enumerate(_other_chips(x, y)):
            for f in range(nf):
                cps.append(_remote(send_sems, recv_sems, k * nf + f, p_refs[f].at[2 * px + py], o_refs[f].at[k], (px, py, c)))
        return cps

    return Rider(pairs, [jax.ShapeDtypeStruct((3,) + p.shape[1:], p.dtype) for p in pairs], 3 * nf, 0, copies)


def pair_swap(grads, name):
    def body(g_refs, o_refs, send_sems, recv_sems):
        x, y, c = _coords()
        cps = []
        for f, g_ref in enumerate(g_refs):
            rh = g_ref.shape[1] // 2
            cps.append(_remote(send_sems, recv_sems, f, g_ref.at[:, pl.ds((1 - c) * rh, rh), :], o_refs[f], (x, y, 1 - c)))
        for cp in cps:
            cp.start()
        for cp in cps:
            cp.wait()

    outs = [jax.ShapeDtypeStruct((g.shape[0], g.shape[1] // 2, g.shape[2]), g.dtype) for g in grads]
    return _exchange_call(body, name, grads, outs, len(grads))


def half_swap(halves, name):
    def body(h_refs, o_refs, send_sems, recv_sems):
        x, y, c = _coords()
        cps = [_remote(send_sems, recv_sems, f, h_ref, o_refs[f], (x, y, 1 - c)) for f, h_ref in enumerate(h_refs)]
        for cp in cps:
            cp.start()
        for cp in cps:
            cp.wait()

    return _exchange_call(body, name, halves, [jax.ShapeDtypeStruct(h.shape, h.dtype) for h in halves], len(halves))


def allsum_small(part, name):
    def body(p_ref, tot_ref, all_ref, send_sems, recv_sems):
        x, y, c = _coords()
        me, sibling = (x, y, c), (x, y, 1 - c)
        chips = _other_chips(x, y)

        def slot(px, py, pc):
            return all_ref.at[4 * px + 2 * py + pc]

        def copy(k, block, to, src=None):
            return pltpu.make_async_remote_copy(src_ref=slot(*block) if src is None else src, dst_ref=slot(*block),
                                                send_sem=send_sems.at[k], recv_sem=recv_sems.at[k], device_id=to, device_id_type=MESH)

        slot(*me)[...] = p_ref[...]
        first = [copy(0, me, sibling, src=p_ref)] + [copy(1 + j, me, (*chip, c), src=p_ref) for j, chip in enumerate(chips)]
        for cp in first:
            cp.start()
        passed = [copy(4 + j, (*chip, c), sibling) for j, chip in enumerate(chips)]
        for j, chip in enumerate(chips):
            copy(1 + j, (*chip, c), me).wait_recv()
            passed[j].start()
        copy(0, sibling, me).wait_recv()
        for j, chip in enumerate(chips):
            copy(4 + j, (*chip, 1 - c), me).wait_recv()
        for cp in first + passed:
            cp.wait_send()
        tot = all_ref[0]
        for d in range(1, 8):
            tot = tot + all_ref[d]
        tot_ref[...] = tot

    vm = pl.BlockSpec(memory_space=pltpu.VMEM)
    return pl.pallas_call(
        body, name=name, out_shape=jax.ShapeDtypeStruct(part.shape, F32), in_specs=[vm], out_specs=vm,
        scratch_shapes=[pltpu.VMEM((8,) + part.shape, F32), pltpu.SemaphoreType.DMA((7,)), pltpu.SemaphoreType.DMA((7,))],
        compiler_params=_params(),
    )(part)


QKVF_COLS = 772
QKVF_PAD = 896


def _tables_for(s):
    return _rot_tables(s)


def layer_families(layer):
    return (0, 1, layer // 2) if layer % 2 == 0 else (2, 3, layer // 2)


class GradientExchange:
    def __init__(self):
        self.core = lax.axis_index("c").astype(jnp.int32).reshape(1)
        self.chip = (2 * lax.axis_index("x") + lax.axis_index("y")).astype(jnp.int32).reshape(1)
        self.pairs, self.arrived, self.pending, self.riding = {}, {}, [], []

    def add(self, items, tag):
        got = pair_swap([g for _, _, g in items], f"grad_pair_swap_{tag}")
        for (fam, li, g), r in zip(items, got):
            self.pairs[(fam, li)] = pair_sum(g, r, self.core, f"grad_pair_sum_{fam}_{li}")
            self.pending.append((fam, li))

    def rider(self):
        self.riding, self.pending = self.pending, []
        return scatter_rider([self.pairs[k] for k in self.riding])

    def landed(self, outs):
        for k, o in zip(self.riding, outs):
            self.arrived[k] = o
        self.riding = []

    def finish(self, weights, moments1, moments2):
        if self.pending:
            self.landed(self.rider().run("grad_chip_scatter_last"))
        mine = []
        for fam, w in enumerate(weights):
            buf = jnp.zeros((w.shape[0], w.shape[1] // 2, w.shape[2]), F32)
            for li in range(w.shape[0]):
                buf = chip_sum(self.pairs[(fam, li)], self.arrived[(fam, li)], self.chip, li, buf, f"grad_chip_sum_{fam}_{li}")
            mine.append(buf)
        other = half_swap(mine, "grad_half_swap")
        return [adamw_family(w, m, v, gm, go, self.core, f"adamw_{f}")
                for f, (w, m, v, gm, go) in enumerate(zip(weights, moments1, moments2, mine, other))]


class KeepGradients:
    def __init__(self):
        self.grads = {}

    def add(self, items, tag):
        for fam, li, g in items:
            self.grads[(fam, li)] = g

    def rider(self):
        return None

    def landed(self, outs):
        pass


def kernel(x, norm_mix, w_qkv_even, w_o_even, w_qkvf_odd, b_forget, w_o_odd, norm_ffn, w_ffn_in, w_ffn_out, norm_final, loss_target, m_norm_mix, m_w_qkv_even, m_w_o_even, m_w_qkvf_odd, m_b_forget, m_w_o_odd, m_norm_ffn, m_w_ffn_in, m_w_ffn_out, m_norm_final, v_norm_mix, v_w_qkv_even, v_w_o_even, v_w_qkvf_odd, v_b_forget, v_w_o_odd, v_norm_ffn, v_w_ffn_in, v_w_ffn_out, v_norm_final):
    w_shards = [w_qkv_even, w_o_even, w_qkvf_odd, w_o_odd, w_ffn_in, w_ffn_out]
    shards = [w.astype(BF16) for w in w_shards]
    tables = _tables_for(x.shape[1])
    bias_pad = jnp.pad(b_forget, ((0, 0), (0, LANES - N_HEADS)))

    w_qkv, w_o = gather_rider([shards[0][0], shards[1][0]]).run("gather_first")
    saved, cur = [], x[0]
    for layer in range(DEPTH):
        coming = [shards[4][layer], shards[5][layer]]
        if layer + 1 < DEPTH:
            fam_qkv, fam_o, li = layer_families(layer + 1)
            coming += [shards[fam_qkv][li], shards[fam_o][li]]
        cur, keep, rode = forward_layer(layer, cur, w_qkv, w_o, norm_mix[layer:layer + 1], norm_ffn[layer:layer + 1], tables,
                                        bias_pad[layer // 2:layer // 2 + 1], rider=gather_rider(coming))
        saved.append(keep)
        if layer + 1 < DEPTH:
            w_qkv, w_o = rode[2], rode[3]

    dcur, g_final, loss_part = loss_head(cur, norm_final.reshape(1, D_MODEL), loss_target[0], "loss_head")

    exchange = GradientExchange()
    g_mix, g_ffn, g_bias = [None] * DEPTH, [None] * DEPTH, [None] * (DEPTH // 2)
    for layer in reversed(range(DEPTH)):
        dcur, g_mix[layer], g_ffn[layer], g_b = backward_layer(layer, dcur, saved[layer], norm_mix[layer:layer + 1],
                                                               norm_ffn[layer:layer + 1], tables, bias_pad[layer // 2:layer // 2 + 1], exchange)
        if g_b is not None:
            g_bias[layer // 2] = g_b

    zero_row = jnp.zeros((1, D_MODEL), F32)
    pad16 = lambda v: jnp.pad(v, (0, D_MODEL - v.shape[0]))[None, :]
    small_rows = lambda mix, ffn, fin, bias, last: jnp.concatenate(
        [r.reshape(1, D_MODEL) for r in mix] + [r.reshape(1, D_MODEL) for r in ffn] + [fin.reshape(1, D_MODEL)]
        + [pad16(b) for b in bias] + [last] + [zero_row] * (SMALL_ROWS - 12), axis=0)
    loss_row = pad16(loss_part[0, :1])
    small_g = allsum_small(small_rows(g_mix, g_ffn, g_final, g_bias, loss_row), "allsum_small")
    loss = small_g[11, 0]
    small_g = small_g.at[11].set(0.0)
    sw = small_rows(list(norm_mix), list(norm_ffn), norm_final, list(b_forget), zero_row)
    sm = small_rows(list(m_norm_mix), list(m_norm_ffn), m_norm_final, list(m_b_forget), zero_row)
    sv = small_rows(list(v_norm_mix), list(v_norm_ffn), v_norm_final, list(v_b_forget), zero_row)
    sd, snm, snv = adamw(sw, small_g, sm, sv, "adamw_small")

    def small_out(a):
        return a[0:4], a[8, :], a[9:11, :N_HEADS], a[4:8]

    widen = lambda t: jnp.pad(t, ((0, 0), (0, 0), (0, QKVF_PAD - QKVF_COLS)))
    padded = lambda ws: [widen(t) if f == 2 else t for f, t in enumerate(ws)]
    big = exchange.finish(
        padded(w_shards), padded([m_w_qkv_even, m_w_o_even, m_w_qkvf_odd, m_w_o_odd, m_w_ffn_in, m_w_ffn_out]),
        padded([v_w_qkv_even, v_w_o_even, v_w_qkvf_odd, v_w_o_odd, v_w_ffn_in, v_w_ffn_out]))

    def outputs(small, which):
        mix, fin, bias, ffn = small_out(small)
        qkv_e, o_e, qkvf, o_o, fi, fo = [big[f][which][:, :, :QKVF_COLS] if f == 2 else big[f][which] for f in range(6)]
        return [mix, qkv_e, o_e, qkvf, bias, o_o, ffn, fi, fo, fin]

    return (loss, dcur[None], *outputs(small_g, 0), *outputs(sd, 1), *outputs(snm, 2), *outputs(snv, 3))


def _chip_tile(rows, cols, at):
    return pl.BlockSpec((None, rows, cols), at)


def forward_layer(layer, cur, w_qkv, w_o, mix_gain, ffn_gain, tables, bias_row, rider=None, w_ffn=None):
    n = f"l{layer}"
    s = cur.shape[0]
    h1 = rmsnorm_fwd(cur, mix_gain, n + "_norm_mix")
    keep = {"x": cur, "h1": h1, "w_o": w_o.reshape(D_ATTN, D_MODEL)}
    if layer % 2 == 0:
        qkv = matmul(h1, w_qkv, "nn", BF16, n + "_qkv", 512, 768, 1024, mnk=(s, 3 * D_ATTN, D_MODEL),
                     b_spec=_chip_tile(D_MODEL, 768, lambda i, j, kk: (j, 0, 0)))
        o_sb, st, rode = causal_fwd(qkv, 4, "sb", n + "_sb_fwd", rider=rider)
        qd, kd, vd = rotary_prep(qkv, tables, n + "_rotary")
        o_dil, lse_dil = dilated_fwd(qd, kd, vd, n + "_dil_fwd")
        attn = jnp.concatenate([o_sb, o_dil], axis=1)
        keep.update(qd=qd, kd=kd, vd=vd, o_dil=o_dil, lse_dil=lse_dil, w_qkv=w_qkv)
    else:
        natural = jnp.transpose(w_qkv, (1, 0, 2)).reshape(D_MODEL, N_CHIPS * QKVF_COLS)
        w_gate = jnp.pad(natural[:, 3 * D_ATTN:], ((0, 0), (0, LANES - N_HEADS)))
        qkv = matmul(h1, natural[:, :3 * D_ATTN], "nn", BF16, n + "_qkv", 512, 768, 1024)
        fl = matmul(h1, w_gate, "nn", F32, n + "_fgate", 512, LANES, 1024)
        cum = forget_fwd(fl, bias_row, n + "_forget_fwd")
        f_heads = cum[:, :N_HEADS].T
        fq = jnp.broadcast_to(f_heads[:, :, None], (N_HEADS, s, LANES))
        fk = f_heads.reshape(N_HEADS // 2, 2, s)
        attn, st, rode = causal_fwd(qkv, 8, "fox", n + "_fox_fwd", fq=fq, fk=fk, rider=rider)
        keep.update(fl=fl, fq=fq, fk=fk, w_qkv=jnp.concatenate([natural[:, :3 * D_ATTN], w_gate], axis=1))
    w_fi, w_fo = (rode[0], rode[1]) if rider is not None else w_ffn
    w_fo = w_fo.reshape(D_FF, D_MODEL)
    mid = matmul(attn, keep["w_o"], "nn", F32, n + "_attn_out", 512, 1024, 1024, res=cur)
    h2 = rmsnorm_fwd(mid, ffn_gain, n + "_norm_ffn")
    gu = matmul(h2, w_fi, "nn", F32, n + "_ffn_in", 512, 1408, 1024, mnk=(s, 2 * D_FF, D_MODEL),
                b_spec=_chip_tile(D_MODEL, 1408, lambda i, j, kk: (j, 0, 0)))
    act = swiglu_fwd(gu, n + "_swiglu")
    out = matmul(act, w_fo, "nn", F32, n + "_ffn_out", 512, 1024, D_FF, res=mid)
    keep.update(qkv=qkv, st=st, attn=attn, mid=mid, h2=h2, gu=gu, act=act, w_fi=w_fi, w_fo=w_fo)
    return out, keep, rode


def backward_layer(layer, dcur, kp, mix_gain, ffn_gain, tables, bias_row, exchange):
    n = f"l{layer}"
    s = dcur.shape[0]
    fam_qkv, fam_o, li = layer_families(layer)
    g_fo = matmul(kp["act"], dcur, "tn", BF16, n + "_d_w_ffn_out", 1408, 1024, 512)
    dact = matmul(dcur, kp["w_fo"], "nt", F32, n + "_d_act", 512, 1408, 1024)
    dgu = swiglu_bwd(kp["gu"], dact, n + "_d_swiglu")
    g_fi = matmul(kp["h2"], dgu, "tn", BF16, n + "_d_w_ffn_in", 1024, 1408, 2048, mnk=(D_MODEL, 2 * D_FF, s),
                  o_spec=_chip_tile(D_MODEL, 1408, lambda i, j, kk: (j, 0, 0)), out_shape=(N_CHIPS, D_MODEL, 1408))
    dh2 = matmul(dgu, kp["w_fi"], "nt", F32, n + "_d_h2", 512, 1024, 1408, mnk=(s, D_MODEL, 2 * D_FF),
                 b_spec=_chip_tile(D_MODEL, 1408, lambda i, j, kk: (kk, 0, 0)))
    dmid, g_ffn = rmsnorm_bwd(kp["mid"], ffn_gain, dh2, dcur, n + "_d_norm_ffn")
    g_o = matmul(kp["attn"], dmid, "tn", BF16, n + "_d_w_o", 1024, 1024, 512)
    dattn = matmul(dmid, kp["w_o"], "nt", F32, n + "_d_attn", 512, 1024, 1024)
    exchange.add([(5, layer, g_fo.reshape(N_CHIPS, D_FF // N_CHIPS, D_MODEL)), (4, layer, g_fi),
                  (fam_o, li, g_o.reshape(N_CHIPS, D_ATTN // N_CHIPS, D_MODEL))], f"l{layer}_ffn")
    rider = exchange.rider()
    g_bias = None
    if layer % 2 == 0:
        dq_a, dk_a, dv_a, rode = causal_bwd(kp["qkv"], dattn, kp["st"], 4, "sb", n + "_sb_bwd", rider=rider)
        dqd, dkd, dvd = dilated_bwd(kp["qd"], kp["kd"], kp["vd"], dattn, kp["o_dil"], kp["lse_dil"], 4, n + "_dil_bwd")
        dq_b, dk_b, dv_b = rotary_bwd(dqd, dkd, dvd, tables, n + "_d_rotary")
        dproj = jnp.concatenate([dq_a, dq_b, dk_a.astype(BF16), dk_b, dv_a.astype(BF16), dv_b], axis=1)
        g_qkv = matmul(kp["h1"], dproj, "tn", BF16, n + "_d_w_qkv", 1024, 768, 2048, mnk=(D_MODEL, 3 * D_ATTN, s),
                       o_spec=_chip_tile(D_MODEL, 768, lambda i, j, kk: (j, 0, 0)), out_shape=(N_CHIPS, D_MODEL, 768))
        dh1 = matmul(dproj, kp["w_qkv"], "nt", F32, n + "_d_h1", 512, 1024, 768, mnk=(s, D_MODEL, 3 * D_ATTN),
                     b_spec=_chip_tile(D_MODEL, 768, lambda i, j, kk: (kk, 0, 0)))
    else:
        dq_f, dk_f, dv_f, dfk, rode = causal_bwd(kp["qkv"], dattn, kp["st"], 8, "fox", n + "_fox_bwd", fq=kp["fq"], fk=kp["fk"],
                                                 rider=rider)
        dcum = jnp.pad(dfk.reshape(N_HEADS, s).T, ((0, 0), (0, LANES - N_HEADS)))
        dfl, dbias = forget_bwd(kp["fl"], bias_row, dcum, n + "_forget_bwd")
        g_bias = dbias[0, :N_HEADS]
        dproj = jnp.concatenate([dq_f, dk_f.astype(BF16), dv_f.astype(BF16), dfl.astype(BF16)], axis=1)
        by_chip = dproj[:, :N_CHIPS * QKVF_COLS].reshape(s, N_CHIPS, QKVF_COLS)
        by_chip = jnp.pad(by_chip, ((0, 0), (0, 0), (0, QKVF_PAD - QKVF_COLS))).reshape(s, N_CHIPS * QKVF_PAD)
        g_qkv = matmul(kp["h1"], by_chip, "tn", BF16, n + "_d_w_qkv", 1024, QKVF_PAD, 2048, mnk=(D_MODEL, N_CHIPS * QKVF_PAD, s),
                       o_spec=_chip_tile(D_MODEL, QKVF_PAD, lambda i, j, kk: (j, 0, 0)), out_shape=(N_CHIPS, D_MODEL, QKVF_PAD))
        dh1 = matmul(dproj, kp["w_qkv"], "nt", F32, n + "_d_h1", 512, 1024, 640)
    exchange.landed(rode)
    exchange.add([(fam_qkv, li, g_qkv)], f"l{layer}_qkv")
    dx, g_mix = rmsnorm_bwd(kp["x"], mix_gain, dh1, dmid, n + "_d_norm_mix")
    return dx, g_mix, g_ffn, g_bias


def local_step(xs, target, norm_mix, norm_ffn, norm_final, b_forget, layer_weights):
    tables = _tables_for(xs.shape[0])
    bias_pad = jnp.pad(b_forget, ((0, 0), (0, LANES - N_HEADS)))
    saved, cur = [], xs
    for layer in range(DEPTH):
        w_qkv, w_o, w_fi, w_fo = layer_weights[layer]
        cur, keep, _ = forward_layer(layer, cur, w_qkv, w_o, norm_mix[layer:layer + 1], norm_ffn[layer:layer + 1], tables,
                                     bias_pad[layer // 2:layer // 2 + 1], w_ffn=(w_fi, w_fo))
        saved.append(keep)
    dcur, g_final, loss_part = loss_head(cur, norm_final.reshape(1, D_MODEL), target, "loss_head")
    keeper = KeepGradients()
    g_mix, g_ffn, g_bias = [None] * DEPTH, [None] * DEPTH, [None] * (DEPTH // 2)
    for layer in reversed(range(DEPTH)):
        dcur, g_mix[layer], g_ffn[layer], g_b = backward_layer(layer, dcur, saved[layer], norm_mix[layer:layer + 1],
                                                               norm_ffn[layer:layer + 1], tables, bias_pad[layer // 2:layer // 2 + 1], keeper)
        if g_b is not None:
            g_bias[layer // 2] = g_b
    return dcur, keeper.grads, (g_mix, g_ffn, g_final, g_bias), loss_part
```

```python
import functools

import jax
import jax.numpy as jnp
from jax import lax
from jax.experimental import pallas as pl
from jax.experimental.pallas import tpu as pltpu

F32 = jnp.float32
BF16 = jnp.bfloat16
MESH = pl.DeviceIdType.MESH

D_MODEL = 1024
DEPTH = 4
HEAD_DIM = 64
N_HEADS = 16
D_ATTN = 1024
D_FF = 2816
ROPE_THETA = 500000.0
ROT_HALF = 8
RMS_EPS = 1e-5
DIL_STRIDES = (1, 4, 16)
ADAM_LR, ADAM_B1, ADAM_B2, ADAM_EPS, ADAM_WD, ADAM_STEP = 0.001, 0.9, 0.999, 1e-8, 0.01, 10

LANES = 128
BLK = 128
VMEM_LIMIT = 56 * 1024 * 1024
NEG = -1e30
N_CHIPS = 4
FLAT_COLS = 1024
FLAT_ROWS = 12800
HALF_ROWS = FLAT_ROWS // 2
SMALL_ROWS = 16


def _params(sem=None):
    return pltpu.CompilerParams(dimension_semantics=sem, vmem_limit_bytes=VMEM_LIMIT)


def _dot(a, b):
    return lax.dot_general(a, b, (((1,), (0,)), ((), ())), preferred_element_type=F32)


def _dot_nt(a, b):
    return lax.dot_general(a, b, (((1,), (1,)), ((), ())), preferred_element_type=F32)


def _dot_tn(a, b):
    return lax.dot_general(a, b, (((0,), (0,)), ((), ())), preferred_element_type=F32)


def _split3(x):
    x1 = x.astype(BF16)
    r1 = x - x1.astype(F32)
    x2 = r1.astype(BF16)
    x3 = (r1 - x2.astype(F32)).astype(BF16)
    return x1, x2, x3


def _dot_exact_lhs(x, t):
    x1, x2, x3 = _split3(x)
    return _dot(x1, t) + _dot(x2, t) + _dot(x3, t)


def _dot_exact_rhs(t, x):
    x1, x2, x3 = _split3(x)
    return _dot(t, x1) + _dot(t, x2) + _dot(t, x3)


def _iotas(shape=(BLK, LANES)):
    return lax.broadcasted_iota(jnp.int32, shape, 0), lax.broadcasted_iota(jnp.int32, shape, 1)


_DIMS = {"nn": (((1,), (0,)), ((), ())), "nt": (((1,), (1,)), ((), ())), "tn": (((0,), (0,)), ((), ()))}


def matmul(a, b, mode, out_dtype, name, tm, tn, tk, res=None, mnk=None, b_spec=None, o_spec=None, out_shape=None, into=None):
    if mnk is not None:
        m, n, k = mnk
    elif mode == "nn":
        (m, k), (k2, n) = a.shape, b.shape
    elif mode == "nt":
        (m, k), (n, k2) = a.shape, b.shape
    else:
        (k, m), (k2, n) = a.shape, b.shape
    assert m % tm == 0 and n % tn == 0 and k % tk == 0, (name, a.shape, b.shape)
    nk = k // tk
    a_spec = pl.BlockSpec((tk, tm), lambda i, j, kk: (kk, i)) if mode == "tn" else pl.BlockSpec((tm, tk), lambda i, j, kk: (i, kk))
    if b_spec is None:
        b_spec = pl.BlockSpec((tn, tk), lambda i, j, kk: (j, kk)) if mode == "nt" else pl.BlockSpec((tk, tn), lambda i, j, kk: (kk, j))
    r_spec = pl.BlockSpec((tm, tn), lambda i, j, kk: (i, j))
    if o_spec is None:
        o_spec = r_spec
    dims = _DIMS[mode]
    has_res = res is not None
    n_in = 2 + int(has_res) + int(into is not None)

    def body(*refs):
        a_ref, b_ref = refs[0], refs[1]
        r_ref = refs[2] if has_res else None
        o_ref = refs[n_in]

        def finish(v):
            if has_res:
                v = v + r_ref[...]
            o_ref[...] = v.astype(out_dtype)

        p = lax.dot_general(a_ref[...].astype(BF16), b_ref[...].astype(BF16), dims, preferred_element_type=F32)
        if nk == 1:
            finish(p)
        else:
            acc = refs[-1]
            kk = pl.program_id(2)

            @pl.when(kk == 0)
            def _():
                acc[...] = p

            @pl.when(kk > 0)
            def _():
                acc[...] += p

            @pl.when(kk == nk - 1)
            def _():
                finish(acc[...])

    ops = [a, b] + ([res] if has_res else []) + ([into] if into is not None else [])
    specs = [a_spec, b_spec] + ([r_spec] if has_res else []) + ([_ANY] if into is not None else [])
    if into is not None:
        out_shape = jax.ShapeDtypeStruct(into.shape, into.dtype)
    else:
        out_shape = jax.ShapeDtypeStruct((m, n) if out_shape is None else out_shape, out_dtype)
    return pl.pallas_call(
        body, name=name, out_shape=out_shape,
        grid=(m // tm, n // tn, nk), in_specs=specs, out_specs=o_spec,
        scratch_shapes=[pltpu.VMEM((tm, tn), F32)] if nk > 1 else [],
        input_output_aliases={n_in - 1: 0} if into is not None else {},
        compiler_params=_params(("parallel", "parallel", "arbitrary")),
    )(*ops)


ROWS = 256


def _row_spec(cols, rows=ROWS):
    return pl.BlockSpec((rows, cols), lambda i: (i, 0))


def _fix_spec(r, cols):
    return pl.BlockSpec((r, cols), lambda i: (0, 0))


def rmsnorm_fwd(x, g, name):
    s, d = x.shape

    def body(x_ref, g_ref, h_ref):
        xv = x_ref[...]
        rstd = lax.rsqrt(jnp.mean(xv * xv, axis=-1, keepdims=True) + RMS_EPS)
        h_ref[...] = (xv * rstd * g_ref[...]).astype(BF16)

    return pl.pallas_call(
        body, name=name, out_shape=jax.ShapeDtypeStruct((s, d), BF16), grid=(s // ROWS,),
        in_specs=[_row_spec(d), _fix_spec(1, d)], out_specs=_row_spec(d), compiler_params=_params(("parallel",)),
    )(x, g)


def _rms_bwd_math(xv, gv, dh):
    rstd = lax.rsqrt(jnp.mean(xv * xv, axis=-1, keepdims=True) + RMS_EPS)
    xhat = xv * rstd
    u = dh * gv
    dx = rstd * (u - xhat * jnp.mean(u * xhat, axis=-1, keepdims=True))
    return dx, dh * xhat


def rmsnorm_bwd(x, g, dh, dres, name):
    s, d = x.shape

    def body(x_ref, g_ref, dh_ref, dres_ref, dx_ref, dg_ref):
        dx, dgt = _rms_bwd_math(x_ref[...], g_ref[...], dh_ref[...])
        dx_ref[...] = dres_ref[...] + dx
        part = jnp.sum(dgt, axis=0, keepdims=True)

        @pl.when(pl.program_id(0) == 0)
        def _():
            dg_ref[...] = part

        @pl.when(pl.program_id(0) > 0)
        def _():
            dg_ref[...] += part

    return pl.pallas_call(
        body, name=name, out_shape=(jax.ShapeDtypeStruct((s, d), F32), jax.ShapeDtypeStruct((1, d), F32)),
        grid=(s // ROWS,), in_specs=[_row_spec(d), _fix_spec(1, d), _row_spec(d), _row_spec(d)],
        out_specs=(_row_spec(d), _fix_spec(1, d)), compiler_params=_params(("arbitrary",)),
    )(x, g, dh, dres)


def loss_head(x, g, target, name):
    s, d = x.shape

    def body(x_ref, g_ref, t_ref, dx_ref, dg_ref, loss_ref):
        xv, gv = x_ref[...], g_ref[...]
        rstd = lax.rsqrt(jnp.mean(xv * xv, axis=-1, keepdims=True) + RMS_EPS)
        err = xv * rstd * gv - t_ref[...]
        dx, dgt = _rms_bwd_math(xv, gv, err * (1.0 / d))
        dx_ref[...] = dx
        part = jnp.sum(dgt, axis=0, keepdims=True)
        lpart = jnp.full((1, LANES), 0.5 / d, F32) * jnp.sum(err * err)

        @pl.when(pl.program_id(0) == 0)
        def _():
            dg_ref[...] = part
            loss_ref[...] = lpart

        @pl.when(pl.program_id(0) > 0)
        def _():
            dg_ref[...] += part
            loss_ref[...] += lpart

    return pl.pallas_call(
        body, name=name,
        out_shape=(jax.ShapeDtypeStruct((s, d), F32), jax.ShapeDtypeStruct((1, d), F32), jax.ShapeDtypeStruct((1, LANES), F32)),
        grid=(s // ROWS,), in_specs=[_row_spec(d), _fix_spec(1, d), _row_spec(d)],
        out_specs=(_row_spec(d), _fix_spec(1, d), _fix_spec(1, LANES)), compiler_params=_params(("arbitrary",)),
    )(x, g, target)


def swiglu_fwd(gu, name):
    s, f2 = gu.shape
    f = f2 // 2

    def body(gu_ref, a_ref):
        gv, uv = gu_ref[:, :f], gu_ref[:, f:]
        a_ref[...] = (gv * (1.0 / (1.0 + jnp.exp(-gv))) * uv).astype(BF16)

    return pl.pallas_call(
        body, name=name, out_shape=jax.ShapeDtypeStruct((s, f), BF16), grid=(s // ROWS,),
        in_specs=[_row_spec(f2)], out_specs=_row_spec(f), compiler_params=_params(("parallel",)),
    )(gu)


def swiglu_bwd(gu, dact, name):
    s, f2 = gu.shape
    f = f2 // 2

    def body(gu_ref, da_ref, o_ref):
        gv, uv, da = gu_ref[:, :f], gu_ref[:, f:], da_ref[...]
        sg = 1.0 / (1.0 + jnp.exp(-gv))
        o_ref[:, :f] = (da * uv * sg * (1.0 + gv * (1.0 - sg))).astype(BF16)
        o_ref[:, f:] = (da * gv * sg).astype(BF16)

    return pl.pallas_call(
        body, name=name, out_shape=jax.ShapeDtypeStruct((s, f2), BF16), grid=(s // ROWS,),
        in_specs=[_row_spec(f2), _row_spec(f)], out_specs=_row_spec(f2), compiler_params=_params(("parallel",)),
    )(gu, dact)


Q_OFF, K_OFF, V_OFF = 0, 8, 16


KB = 512
BQ = 256
SUB = KB // BLK


def _softplus_parts(z):
    sp = jnp.log(1.0 + jnp.exp(-jnp.abs(z)))
    ls = jnp.minimum(z, 0.0) - sp
    return ls, ls - z


def _wide(t):
    return jnp.concatenate([t] * SUB, axis=1)


def _chunk_dots(x, tri):
    terms = []
    for u in range(SUB):
        terms += list(_split3(x[:, u * BLK:(u + 1) * BLK]))
    r = _dot(jnp.concatenate(terms, axis=0), tri)
    rows = x.shape[0]
    piece = lambda n: r[n * rows:(n + 1) * rows]
    return [piece(3 * u) + piece(3 * u + 1) + piece(3 * u + 2) for u in range(SUB)]


def _block_suffix_sums(x, suffix, c):
    loc = _chunk_dots(x, suffix)
    out = [None] * SUB
    for u in reversed(range(SUB)):
        out[u] = loc[u] + c
        c = c + jnp.sum(x[:, u * BLK:(u + 1) * BLK], axis=1, keepdims=True)
    return jnp.concatenate(out, axis=1), c


def _block_prefix_sums(x, tri, c):
    loc = _chunk_dots(x, tri)
    out = []
    for u in range(SUB):
        out.append(loc[u] + c)
        c = c + jnp.sum(x[:, u * BLK:(u + 1) * BLK], axis=1, keepdims=True)
    return jnp.concatenate(out, axis=1), c


def causal_fwd(qkv, npairs, mode, name, fq=None, fk=None, rider=None):
    s = qkv.shape[0]
    nq = s // BQ
    fox = mode == "fox"

    def body(*refs):
        if fox:
            q_ref, k_ref, v_ref, fq_ref, fk_ref, o_ref, st_ref = refs
        else:
            q_ref, k_ref, v_ref, o_ref, st_ref = refs
        i = pl.program_id(1)
        nkb = (i * BQ + BQ - 1) // KB + 1
        row, lane = _iotas((BQ, KB))
        row_s, lane_s = _iotas()
        _, lane_q = _iotas((BQ, LANES))
        qpos = i * BQ + row
        qf = q_ref[...].astype(F32)
        hms = (lane_q < HEAD_DIM, lane_q >= HEAD_DIM)
        qas = [jnp.where(hm, qf, 0.0).astype(BF16) for hm in hms]
        suffix = jnp.where(row_s > lane_s, 1.0, 0.0).astype(BF16)
        zero = jnp.zeros((BQ, LANES), F32)
        col0 = jnp.zeros((BQ, 1), F32)

        def kv(j):
            r0 = pl.multiple_of(j * KB, KB)
            return r0, k_ref[pl.ds(r0, KB), :], v_ref[pl.ds(r0, KB), :]

        if fox:
            fqs = [_wide(fq_ref[a]) for a in range(2)]

            def step(j, carry):
                r0, kb, vb = kv(j)
                ok = r0 + lane <= qpos
                new = []
                for a in range(2):
                    acc, mx, l = carry[3 * a:3 * a + 3]
                    z = _dot_nt(qas[a], kb) * 0.125 + fqs[a] - fk_ref[a:a + 1, pl.ds(r0, KB)]
                    z = jnp.where(ok, z, NEG)
                    mnew = jnp.maximum(mx, jnp.max(z, axis=1, keepdims=True))
                    p = jnp.exp(z - mnew)
                    alpha = jnp.exp(mx - mnew)
                    new += [alpha * acc + _dot(p.astype(BF16), vb), mnew, alpha * l + jnp.sum(p, axis=1, keepdims=True)]
                return tuple(new)

            neg = jnp.full((BQ, 1), NEG, F32)
            res = lax.fori_loop(0, nkb, step, (zero, neg, col0, zero, neg, col0))
            outs = [res[3 * a] / res[3 * a + 2] for a in range(2)]
            stats = [res[3 * a + 1] + jnp.log(res[3 * a + 2]) for a in range(2)]
        else:
            def step(jj, carry):
                r0, kb, vb = kv(nkb - 1 - jj)
                strict = r0 + lane < qpos
                new = []
                for a in range(2):
                    acc, c = carry[2 * a:2 * a + 2]
                    ls, lm = _softplus_parts(_dot_nt(qas[a], kb) * 0.125)
                    lm = jnp.where(strict, lm, 0.0)
                    between, c = _block_suffix_sums(lm, suffix, c)
                    aw = jnp.where(strict, jnp.exp(ls + between), 0.0)
                    new += [acc + _dot(aw.astype(BF16), vb), c]
                return tuple(new)

            res = lax.fori_loop(0, nkb, step, (zero, col0, zero, col0))
            outs, stats = [res[0], res[2]], [res[1], res[3]]
        o_ref[...] = jnp.where(hms[0], outs[0], outs[1])
        for a in range(2):
            st_ref[a] = jnp.broadcast_to(stats[a], (BQ, LANES))

    col = lambda off: (lambda p, i: (0, off + p))
    in_specs = [pl.BlockSpec((BQ, LANES), lambda p, i: (i, Q_OFF + p)),
                pl.BlockSpec((s, LANES), col(K_OFF)), pl.BlockSpec((s, LANES), col(V_OFF))]
    ops = [qkv, qkv, qkv]
    if fox:
        in_specs += [pl.BlockSpec((2, BQ, LANES), lambda p, i: (p, i, 0)), pl.BlockSpec((None, 2, s), lambda p, i: (p, 0, 0))]
        ops += [fq, fk]
    (o, stat), rode = call_with_rider(
        body, name, rider, ops, in_specs,
        [jax.ShapeDtypeStruct((s, npairs * LANES), F32), jax.ShapeDtypeStruct((2 * npairs, s, LANES), F32)],
        [pl.BlockSpec((BQ, LANES), lambda p, i: (i, p)), pl.BlockSpec((2, BQ, LANES), lambda p, i: (p, i, 0))], [], (npairs, nq))
    return o, stat, rode


def causal_bwd(qkv, do, stat, npairs, mode, name, fq=None, fk=None, rider=None):
    s = qkv.shape[0]
    nq = s // BQ
    fox = mode == "fox"

    def body(*refs):
        if fox:
            q_ref, k_ref, v_ref, do_ref, st_ref, fq_ref, fk_ref, dq_ref, dk_ref, dv_ref, df_ref, p_s, dp_s = refs
        else:
            q_ref, k_ref, v_ref, do_ref, st_ref, dq_ref, dk_ref, dv_ref = refs
        i = pl.program_id(1)

        @pl.when(i == 0)
        def _():
            dk_ref[...] = jnp.zeros_like(dk_ref)
            dv_ref[...] = jnp.zeros_like(dv_ref)
            if fox:
                df_ref[...] = jnp.zeros_like(df_ref)

        nkb = (i * BQ + BQ - 1) // KB + 1
        row, lane = _iotas((BQ, KB))
        row_s, lane_s = _iotas()
        _, lane_q = _iotas((BQ, LANES))
        qpos = i * BQ + row
        qf = q_ref[...].astype(F32)
        dov = do_ref[...]
        hms = (lane_q < HEAD_DIM, lane_q >= HEAD_DIM)
        qas = [jnp.where(hm, qf, 0.0).astype(BF16) for hm in hms]
        doas = [jnp.where(hm, dov, 0.0).astype(BF16) for hm in hms]
        stas = [_wide(st_ref[a]) for a in range(2)]
        zero = jnp.zeros((BQ, LANES), F32)
        col0 = jnp.zeros((BQ, 1), F32)

        def kv(j):
            r0 = pl.multiple_of(j * KB, KB)
            return r0, k_ref[pl.ds(r0, KB), :], v_ref[pl.ds(r0, KB), :]

        if fox:
            fqs = [_wide(fq_ref[a]) for a in range(2)]

            def probs(j, deltas):
                r0, kb, vb = kv(j)
                ok = r0 + lane <= qpos
                new = []
                for a in range(2):
                    z = _dot_nt(qas[a], kb) * 0.125 + fqs[a] - fk_ref[a:a + 1, pl.ds(r0, KB)]
                    p = jnp.where(ok, jnp.exp(z - stas[a]), 0.0)
                    dp = _dot_nt(doas[a], vb)
                    p_s[a, j] = p
                    dp_s[a, j] = dp
                    new.append(deltas[a] + jnp.sum(p * dp, axis=1, keepdims=True))
                return tuple(new)

            deltas = lax.fori_loop(0, nkb, probs, (col0, col0))

            def step(j, dqs):
                r0, kb, _ = kv(j)
                new = []
                dk = jnp.zeros((KB, LANES), F32)
                dv = jnp.zeros((KB, LANES), F32)
                for a in range(2):
                    p = p_s[a, j]
                    ds = p * (dp_s[a, j] - deltas[a])
                    dsb = (ds * 0.125).astype(BF16)
                    dk += _dot_tn(dsb, qas[a])
                    dv += _dot_tn(p.astype(BF16), doas[a])
                    df_ref[a:a + 1, pl.ds(r0, KB)] -= jnp.sum(ds, axis=0, keepdims=True)
                    new.append(dqs[a] + _dot(dsb, kb))
                dk_ref[pl.ds(r0, KB), :] += dk
                dv_ref[pl.ds(r0, KB), :] += dv
                return tuple(new)

            dqs = lax.fori_loop(0, nkb, step, (zero, zero))
        else:
            incl = jnp.where(row_s <= lane_s, 1.0, 0.0).astype(BF16)
            excl = jnp.where(row_s < lane_s, 1.0, 0.0).astype(BF16)

            def step(j, carry):
                r0, kb, vb = kv(j)
                strict = r0 + lane < qpos
                new = []
                dk = jnp.zeros((KB, LANES), F32)
                dv = jnp.zeros((KB, LANES), F32)
                for a in range(2):
                    dq, cm, cg = carry[3 * a:3 * a + 3]
                    ls, lm = _softplus_parts(_dot_nt(qas[a], kb) * 0.125)
                    lm = jnp.where(strict, lm, 0.0)
                    beta = jnp.exp(ls)
                    upto, cm = _block_prefix_sums(lm, incl, cm)
                    aw = jnp.where(strict, jnp.exp(ls + stas[a] - upto), 0.0)
                    g = aw * _dot_nt(doas[a], vb)
                    pre, cg = _block_prefix_sums(g, excl, cg)
                    dz = jnp.where(strict, g * (1.0 - beta) - pre * beta, 0.0)
                    dzb = (dz * 0.125).astype(BF16)
                    dk += _dot_tn(dzb, qas[a])
                    dv += _dot_tn(aw.astype(BF16), doas[a])
                    new += [dq + _dot(dzb, kb), cm, cg]
                dk_ref[pl.ds(r0, KB), :] += dk
                dv_ref[pl.ds(r0, KB), :] += dv
                return tuple(new)

            res = lax.fori_loop(0, nkb, step, (zero, col0, col0, zero, col0, col0))
            dqs = (res[0], res[3])
        dq_ref[...] = jnp.where(hms[0], dqs[0], dqs[1]).astype(BF16)

    col = lambda off: (lambda p, i: (0, off + p))
    blk = pl.BlockSpec((BQ, LANES), lambda p, i: (i, p))
    acc = pl.BlockSpec((s, LANES), lambda p, i: (0, p))
    st_spec = pl.BlockSpec((2, BQ, LANES), lambda p, i: (p, i, 0))
    in_specs = [pl.BlockSpec((BQ, LANES), lambda p, i: (i, Q_OFF + p)), pl.BlockSpec((s, LANES), col(K_OFF)),
                pl.BlockSpec((s, LANES), col(V_OFF)), blk, st_spec]
    ops = [qkv, qkv, qkv, do, stat]
    w = npairs * LANES
    out_shape = [jax.ShapeDtypeStruct((s, w), BF16), jax.ShapeDtypeStruct((s, w), F32), jax.ShapeDtypeStruct((s, w), F32)]
    out_specs = [blk, acc, acc]
    scratch = []
    if fox:
        fk_spec = pl.BlockSpec((None, 2, s), lambda p, i: (p, 0, 0))
        in_specs += [st_spec, fk_spec]
        ops += [fq, fk]
        out_shape.append(jax.ShapeDtypeStruct((npairs, 2, s), F32))
        out_specs.append(fk_spec)
        scratch = [pltpu.VMEM((2, s // KB, BQ, KB), F32)] * 2
    outs, rode = call_with_rider(body, name, rider, ops, in_specs, out_shape, out_specs, scratch, (npairs, nq))
    return (*outs, rode)


def forget_fwd(fl, bias, name):
    s = fl.shape[0]

    def body(fl_ref, b_ref, f_ref):
        row, lane = _iotas()
        lower = jnp.where(lane <= row, 1.0, 0.0).astype(BF16)

        def step(n, carry):
            r0 = pl.multiple_of(n * BLK, BLK)
            ls, _ = _softplus_parts(fl_ref[pl.ds(r0, BLK), :] + b_ref[...])
            blk = _dot_exact_rhs(lower, ls) + carry
            f_ref[pl.ds(r0, BLK), :] = blk
            return blk[BLK - 1:BLK, :]

        lax.fori_loop(0, s // BLK, step, jnp.zeros((1, LANES), F32))

    return pl.pallas_call(
        body, name=name, out_shape=jax.ShapeDtypeStruct((s, LANES), F32),
        in_specs=[pl.BlockSpec(memory_space=pltpu.VMEM)] * 2, out_specs=pl.BlockSpec(memory_space=pltpu.VMEM),
        compiler_params=_params(),
    )(fl, bias)


def forget_bwd(fl, bias, df, name):
    s = fl.shape[0]
    nb = s // BLK

    def body(fl_ref, b_ref, df_ref, o_ref, db_ref):
        row, lane = _iotas()
        upper = jnp.where(lane >= row, 1.0, 0.0).astype(BF16)

        def step(nn, carry):
            tail, db = carry
            r0 = pl.multiple_of((nb - 1 - nn) * BLK, BLK)
            dls = _dot_exact_rhs(upper, df_ref[pl.ds(r0, BLK), :]) + tail
            xv = fl_ref[pl.ds(r0, BLK), :] + b_ref[...]
            dfl = dls * (1.0 / (1.0 + jnp.exp(xv)))
            o_ref[pl.ds(r0, BLK), :] = dfl
            return dls[0:1, :], db + jnp.sum(dfl, axis=0, keepdims=True)

        _, db = lax.fori_loop(0, nb, step, (jnp.zeros((1, LANES), F32), jnp.zeros((1, LANES), F32)))
        db_ref[...] = db

    return pl.pallas_call(
        body, name=name, out_shape=(jax.ShapeDtypeStruct((s, LANES), F32), jax.ShapeDtypeStruct((1, LANES), F32)),
        in_specs=[pl.BlockSpec(memory_space=pltpu.VMEM)] * 3,
        out_specs=(pl.BlockSpec(memory_space=pltpu.VMEM), pl.BlockSpec(memory_space=pltpu.VMEM)),
        compiler_params=_params(),
    )(fl, bias, df)


def _rot_tables(s):
    inv = ROPE_THETA ** (-jnp.arange(ROT_HALF, dtype=F32) * 2.0 / (2 * ROT_HALF))
    ang = jnp.arange(s, dtype=F32)[:, None] * inv[None, :]
    cos, sin = jnp.cos(ang), jnp.sin(ang)
    z8 = jnp.zeros((s, ROT_HALF), F32)
    rest = HEAD_DIM - 2 * ROT_HALF
    zr, onr = jnp.zeros((s, rest), F32), jnp.ones((s, rest), F32)
    tile = lambda t: jnp.tile(t, (1, 2))
    return tile(jnp.concatenate([cos, cos, onr], 1)), tile(jnp.concatenate([-sin, z8, zr], 1)), tile(jnp.concatenate([z8, sin, zr], 1))


def rotary_prep(qkv, tables, name):
    s = qkv.shape[0]
    w = 4 * LANES

    def body(q_ref, k_ref, v_ref, c_ref, s1_ref, s2_ref, qo_ref, ko_ref, vo_ref):
        c, s1, s2 = c_ref[...], s1_ref[...], s2_ref[...]

        def rot(xv):
            return xv * c + pltpu.roll(xv, LANES - ROT_HALF, 1) * s1 + pltpu.roll(xv, ROT_HALF, 1) * s2

        qo_ref[...] = rot(q_ref[...].astype(F32)) * 0.125
        ko_ref[...] = rot(k_ref[...].astype(F32))
        vo_ref[...] = v_ref[...].astype(F32)

    cb = lambda off: pl.BlockSpec((ROWS, LANES), lambda i, j: (i, off + j))
    tb = pl.BlockSpec((ROWS, LANES), lambda i, j: (i, 0))
    out = jax.ShapeDtypeStruct((s, w), F32)
    return pl.pallas_call(
        body, name=name, out_shape=(out, out, out), grid=(s // ROWS, 4),
        in_specs=[cb(Q_OFF + 4), cb(K_OFF + 4), cb(V_OFF + 4), tb, tb, tb], out_specs=(cb(0), cb(0), cb(0)),
        compiler_params=_params(("parallel", "parallel")),
    )(qkv, qkv, qkv, *tables)


def rotary_bwd(dq, dk, dv, tables, name):
    s, w = dq.shape

    def body(dq_ref, dk_ref, dv_ref, c_ref, s1_ref, s2_ref, qo_ref, ko_ref, vo_ref):
        c, s1, s2 = c_ref[...], s1_ref[...], s2_ref[...]

        def rot_t(dy):
            return dy * c + pltpu.roll(dy * s1, ROT_HALF, 1) + pltpu.roll(dy * s2, LANES - ROT_HALF, 1)

        qo_ref[...] = (rot_t(dq_ref[...]) * 0.125).astype(BF16)
        ko_ref[...] = rot_t(dk_ref[...]).astype(BF16)
        vo_ref[...] = dv_ref[...].astype(BF16)

    cb = pl.BlockSpec((ROWS, LANES), lambda i, j: (i, j))
    tb = pl.BlockSpec((ROWS, LANES), lambda i, j: (i, 0))
    out = jax.ShapeDtypeStruct((s, w), BF16)
    return pl.pallas_call(
        body, name=name, out_shape=(out, out, out), grid=(s // ROWS, w // LANES),
        in_specs=[cb, cb, cb, tb, tb, tb], out_specs=(cb, cb, cb), compiler_params=_params(("parallel", "parallel")),
    )(dq, dk, dv, *tables)


def _deinterleave(dst, src_ref, stride, s, dtype):
    length = s // stride
    for r in range(stride):
        if stride == 1:
            dst[...] = src_ref[...].astype(dtype)
        else:
            dst[r * length:(r + 1) * length, :] = src_ref[pl.ds(r, length, stride=stride), :].astype(dtype)


def _band_masks(row, lane, first):
    return lane <= row, lane >= row + jnp.where(first, BLK, 0)


def dilated_fwd(qd, kd, vd, name):
    s, w = qd.shape
    npairs = w // LANES
    nblk = s // BLK

    def body(q_ref, k_ref, v_ref, o_ref, lse_ref, qs, ks, vs, od, ld, on, ln):
        row, lane = _iotas()
        for pi, stride in enumerate(DIL_STRIDES):
            per = (s // stride) // BLK
            _deinterleave(qs, q_ref, stride, s, BF16)
            _deinterleave(ks, k_ref, stride, s, BF16)
            _deinterleave(vs, v_ref, stride, s, BF16)

            def block(b, carry):
                r0 = pl.multiple_of(b * BLK, BLK)
                rp = pl.multiple_of(jnp.maximum(b - 1, 0) * BLK, BLK)
                mc, mp = _band_masks(row, lane, b % per == 0)
                q = qs[pl.ds(r0, BLK), :]
                kc, kp, vc, vp = ks[pl.ds(r0, BLK), :], ks[pl.ds(rp, BLK), :], vs[pl.ds(r0, BLK), :], vs[pl.ds(rp, BLK), :]
                out = jnp.zeros((BLK, LANES), F32)
                lse = jnp.zeros((BLK, LANES), F32)
                for a in range(2):
                    hm = (lane < HEAD_DIM) if a == 0 else (lane >= HEAD_DIM)
                    qa = jnp.where(hm, q.astype(F32), 0.0).astype(BF16)
                    sc = jnp.where(mc, _dot_nt(qa, kc), NEG)
                    sp = jnp.where(mp, _dot_nt(qa, kp), NEG)
                    mx = jnp.maximum(jnp.max(sc, axis=1, keepdims=True), jnp.max(sp, axis=1, keepdims=True))
                    pc, pp = jnp.exp(sc - mx), jnp.exp(sp - mx)
                    l = jnp.sum(pc, axis=1, keepdims=True) + jnp.sum(pp, axis=1, keepdims=True)
                    oa = (_dot(pc.astype(BF16), vc) + _dot(pp.astype(BF16), vp)) / l
                    out = jnp.where(hm, oa, out)
                    lse = jnp.where(hm, mx + jnp.log(l), lse)
                od[pl.ds(r0, BLK), :] = out
                ld[pl.ds(r0, BLK), :] = lse
                return carry

            lax.fori_loop(0, nblk, block, 0)
            length = s // stride
            for r in range(stride):
                if stride == 1:
                    on[pi] = od[...]
                    ln[pi] = ld[...]
                else:
                    on[pi, pl.ds(r, length, stride=stride), :] = od[r * length:(r + 1) * length, :]
                    ln[pi, pl.ds(r, length, stride=stride), :] = ld[r * length:(r + 1) * length, :]

        def merge(n, carry):
            r0 = pl.multiple_of(n * BLK, BLK)
            ls = [ln[pi, pl.ds(r0, BLK), :] for pi in range(3)]
            mx = jnp.maximum(jnp.maximum(ls[0], ls[1]), ls[2])
            ws = [jnp.exp(lv - mx) for lv in ls]
            den = ws[0] + ws[1] + ws[2]
            num = ws[0] * on[0, pl.ds(r0, BLK), :] + ws[1] * on[1, pl.ds(r0, BLK), :] + ws[2] * on[2, pl.ds(r0, BLK), :]
            o_ref[pl.ds(r0, BLK), :] = num / den
            lse_ref[pl.ds(r0, BLK), :] = mx + jnp.log(den)
            return carry

        lax.fori_loop(0, nblk, merge, 0)

    colspec = pl.BlockSpec((s, LANES), lambda p: (0, p))
    out = jax.ShapeDtypeStruct((s, w), F32)
    return pl.pallas_call(
        body, name=name, out_shape=(out, out), grid=(npairs,), in_specs=[colspec] * 3, out_specs=(colspec, colspec),
        scratch_shapes=[pltpu.VMEM((s, LANES), BF16)] * 3 + [pltpu.VMEM((s, LANES), F32)] * 2 + [pltpu.VMEM((3, s, LANES), F32)] * 2,
        compiler_params=_params(("parallel",)),
    )(qd, kd, vd)


def dilated_bwd(qd, kd, vd, do, out, lse, do_off, name):
    s, w = qd.shape
    npairs = w // LANES
    nblk = s // BLK

    def body(q_ref, k_ref, v_ref, do_ref, out_ref, lse_ref, dq_ref, dk_ref, dv_ref, qs, ks, vs, dos, dls, lss, dqd, dkd, dvd, dln):
        row, lane = _iotas()
        same_head = jnp.where((row < HEAD_DIM) == (lane < HEAD_DIM), 1.0, 0.0).astype(BF16)

        def delta_blk(n, carry):
            r0 = pl.multiple_of(n * BLK, BLK)
            dln[pl.ds(r0, BLK), :] = _dot_exact_lhs(do_ref[pl.ds(r0, BLK), :] * out_ref[pl.ds(r0, BLK), :], same_head)
            return carry

        lax.fori_loop(0, nblk, delta_blk, 0)
        for pi, stride in enumerate(DIL_STRIDES):
            per = (s // stride) // BLK
            _deinterleave(qs, q_ref, stride, s, BF16)
            _deinterleave(ks, k_ref, stride, s, BF16)
            _deinterleave(vs, v_ref, stride, s, BF16)
            _deinterleave(dos, do_ref, stride, s, BF16)
            _deinterleave(dls, dln, stride, s, F32)
            _deinterleave(lss, lse_ref, stride, s, F32)

            def block(b, carry):
                r0 = pl.multiple_of(b * BLK, BLK)
                rp = pl.multiple_of(jnp.maximum(b - 1, 0) * BLK, BLK)
                first = b % per == 0
                mc, mp = _band_masks(row, lane, first)
                q, dov = qs[pl.ds(r0, BLK), :], dos[pl.ds(r0, BLK), :]
                kc, kp, vc, vp = ks[pl.ds(r0, BLK), :], ks[pl.ds(rp, BLK), :], vs[pl.ds(r0, BLK), :], vs[pl.ds(rp, BLK), :]
                lse_t, dl_t = lss[pl.ds(r0, BLK), :], dls[pl.ds(r0, BLK), :]
                dq = jnp.zeros((BLK, LANES), F32)
                dkc = jnp.zeros((BLK, LANES), F32)
                dkp = jnp.zeros((BLK, LANES), F32)
                dvc = jnp.zeros((BLK, LANES), F32)
                dvp = jnp.zeros((BLK, LANES), F32)
                for a in range(2):
                    hm = (lane < HEAD_DIM) if a == 0 else (lane >= HEAD_DIM)
                    pick = lane == a * HEAD_DIM
                    qa = jnp.where(hm, q.astype(F32), 0.0).astype(BF16)
                    doa = jnp.where(hm, dov.astype(F32), 0.0).astype(BF16)
                    lse_a = jnp.sum(jnp.where(pick, lse_t, 0.0), axis=1, keepdims=True)
                    dl_a = jnp.sum(jnp.where(pick, dl_t, 0.0), axis=1, keepdims=True)
                    pc = jnp.where(mc, jnp.exp(_dot_nt(qa, kc) - lse_a), 0.0)
                    pp = jnp.where(mp, jnp.exp(_dot_nt(qa, kp) - lse_a), 0.0)
                    dsc = (pc * (_dot_nt(doa, vc) - dl_a)).astype(BF16)
                    dsp = (pp * (_dot_nt(doa, vp) - dl_a)).astype(BF16)
                    dq = jnp.where(hm, _dot(dsc, kc) + _dot(dsp, kp), dq)
                    dkc += _dot_tn(dsc, qa)
                    dkp += _dot_tn(dsp, qa)
                    dvc += _dot_tn(pc.astype(BF16), doa)
                    dvp += _dot_tn(pp.astype(BF16), doa)
                dqd[pl.ds(r0, BLK), :] = dq
                dkd[pl.ds(r0, BLK), :] = dkc
                dvd[pl.ds(r0, BLK), :] = dvc

                @pl.when(jnp.logical_not(first))
                def _():
                    dkd[pl.ds(rp, BLK), :] += dkp
                    dvd[pl.ds(rp, BLK), :] += dvp

                return carry

            lax.fori_loop(0, nblk, block, 0)
            length = s // stride
            for dst, src in ((dq_ref, dqd), (dk_ref, dkd), (dv_ref, dvd)):
                for r in range(stride):
                    if stride == 1:
                        dst[...] = src[...]
                    else:
                        dst[pl.ds(r, length, stride=stride), :] += src[r * length:(r + 1) * length, :]

    colspec = pl.BlockSpec((s, LANES), lambda p: (0, p))
    do_spec = pl.BlockSpec((s, LANES), lambda p: (0, do_off + p))
    o3 = jax.ShapeDtypeStruct((s, w), F32)
    return pl.pallas_call(
        body, name=name, out_shape=(o3, o3, o3), grid=(npairs,),
        in_specs=[colspec, colspec, colspec, do_spec, colspec, colspec], out_specs=(colspec, colspec, colspec),
        scratch_shapes=[pltpu.VMEM((s, LANES), BF16)] * 4 + [pltpu.VMEM((s, LANES), F32)] * 6,
        compiler_params=_params(("parallel",)),
    )(qd, kd, vd, do, out, lse)


def adamw(w, g, m, v, name):
    rows, cols = w.shape
    rb = min(rows, ROWS)
    c1 = 1.0 - ADAM_B1 ** ADAM_STEP
    c2 = 1.0 - ADAM_B2 ** ADAM_STEP

    def body(w_ref, g_ref, m_ref, v_ref, d_ref, mo_ref, vo_ref):
        gv = g_ref[...]
        mn = ADAM_B1 * m_ref[...] + (1.0 - ADAM_B1) * gv
        vn = ADAM_B2 * v_ref[...] + (1.0 - ADAM_B2) * (gv * gv)
        d_ref[...] = -ADAM_LR * ((mn / c1) / (jnp.sqrt(vn / c2) + ADAM_EPS) + ADAM_WD * w_ref[...])
        mo_ref[...] = mn
        vo_ref[...] = vn

    spec = _row_spec(cols, rb)
    out = jax.ShapeDtypeStruct((rows, cols), F32)
    return pl.pallas_call(
        body, name=name, out_shape=(out, out, out), grid=(rows // rb,), in_specs=[spec] * 4, out_specs=(spec,) * 3,
        compiler_params=_params(("parallel",)),
    )(w, g, m, v)


def _prefetch_call(body, name, scalar, ops, grid, in_specs, out_specs, out_shape, sem):
    spec = pltpu.PrefetchScalarGridSpec(num_scalar_prefetch=1, grid=grid, in_specs=in_specs, out_specs=out_specs)
    return pl.pallas_call(body, name=name, grid_spec=spec, out_shape=out_shape, compiler_params=_params(sem))(scalar, *ops)


def pair_sum(g, got, core, name):
    nc, r, c = g.shape
    rh = r // 2

    def body(core_ref, g_ref, got_ref, o_ref):
        o_ref[...] = (g_ref[...].astype(F32) + got_ref[...].astype(F32)).astype(BF16)

    blk = lambda rows_of: pl.BlockSpec((None, rh, c), rows_of)
    return _prefetch_call(
        body, name, core, (g, got), (nc,),
        [blk(lambda j, core_ref: (j, core_ref[0], 0)), blk(lambda j, core_ref: (j, 0, 0))],
        blk(lambda j, core_ref: (j, 0, 0)), jax.ShapeDtypeStruct((nc, rh, c), BF16), ("parallel",))


def chip_sum(pair, got, chip, layer, into, name):
    _, rh, c = pair.shape

    def body(chip_ref, p_ref, a_ref, b_ref, c_ref, old_ref, o_ref):
        o_ref[...] = ((p_ref[...].astype(F32) + a_ref[...].astype(F32)) + b_ref[...].astype(F32)) + c_ref[...].astype(F32)

    arrival = lambda k: pl.BlockSpec((None, rh, c), lambda i, chip_ref: (k, 0, 0))
    spec = pltpu.PrefetchScalarGridSpec(
        num_scalar_prefetch=1, grid=(1,),
        in_specs=[pl.BlockSpec((None, rh, c), lambda i, chip_ref: (chip_ref[0], 0, 0)), arrival(0), arrival(1), arrival(2), _ANY],
        out_specs=pl.BlockSpec((None, rh, c), lambda i, chip_ref: (layer, 0, 0)))
    return pl.pallas_call(body, name=name, grid_spec=spec, out_shape=jax.ShapeDtypeStruct(into.shape, into.dtype),
                          input_output_aliases={5: 0}, compiler_params=_params(("arbitrary",)))(chip, pair, got, got, got, into)


def adamw_family(w, m, v, g_mine, g_other, core, name):
    nl, r, c = w.shape
    rh = r // 2
    nb = 4 if rh % 512 == 0 else (2 if rh % 16 == 0 and rh > 256 else 1)
    rb = rh // nb
    c1 = 1.0 - ADAM_B1 ** ADAM_STEP
    c2 = 1.0 - ADAM_B2 ** ADAM_STEP

    def body(core_ref, w_ref, m_ref, v_ref, gm_ref, go_ref, g_ref, d_ref, mo_ref, vo_ref):
        gv = jnp.where(pl.program_id(1) == core_ref[0], gm_ref[...], go_ref[...])
        mn = ADAM_B1 * m_ref[...] + (1.0 - ADAM_B1) * gv
        vn = ADAM_B2 * v_ref[...] + (1.0 - ADAM_B2) * (gv * gv)
        g_ref[...] = gv
        d_ref[...] = -ADAM_LR * ((mn / c1) / (jnp.sqrt(vn / c2) + ADAM_EPS) + ADAM_WD * w_ref[...])
        mo_ref[...] = mn
        vo_ref[...] = vn

    full = pl.BlockSpec((None, rb, c), lambda l, h, i, core_ref: (l, h * nb + i, 0))
    half = pl.BlockSpec((None, rb, c), lambda l, h, i, core_ref: (l, i, 0))
    out = jax.ShapeDtypeStruct((nl, r, c), F32)
    return _prefetch_call(body, name, core, (w, m, v, g_mine, g_other), (nl, 2, nb), [full, full, full, half, half],
                          (full, full, full, full), (out, out, out, out), ("parallel", "parallel", "parallel"))


def _coords():
    return lax.axis_index("x"), lax.axis_index("y"), lax.axis_index("c")


def _other_chips(x, y):
    return ((1 - x, y), (x, 1 - y), (1 - x, 1 - y))


_ANY = pl.BlockSpec(memory_space=pl.ANY)


def _exchange_call(body, name, arrays, out_shapes, n_copies, n_local=0):
    n = len(arrays)

    def wrapped(*refs):
        body(refs[:n], refs[n:n + len(out_shapes)], *refs[n + len(out_shapes):])

    scratch = [pltpu.SemaphoreType.DMA((n_copies,)), pltpu.SemaphoreType.DMA((n_copies,))]
    if n_local:
        scratch.append(pltpu.SemaphoreType.DMA((n_local,)))
    return pl.pallas_call(
        wrapped, name=name, out_shape=tuple(out_shapes), in_specs=[_ANY] * n, out_specs=tuple([_ANY] * len(out_shapes)),
        scratch_shapes=scratch, compiler_params=_params(),
    )(*arrays)


def _remote(send_sems, recv_sems, n, src, dst, to):
    return pltpu.make_async_remote_copy(src_ref=src, dst_ref=dst, send_sem=send_sems.at[n], recv_sem=recv_sems.at[n],
                                        device_id=to, device_id_type=MESH)


class Rider:
    def __init__(self, arrays, out_shapes, n_remote, n_local, copies, then=None):
        self.arrays, self.out_shapes, self.n_remote, self.n_local = list(arrays), list(out_shapes), n_remote, n_local
        self.copies, self.then = copies, then

    def sems(self):
        return [pltpu.SemaphoreType.DMA((self.n_remote,)), pltpu.SemaphoreType.DMA((self.n_remote,)),
                pltpu.SemaphoreType.DMA((max(self.n_local, 1),))]

    def run(self, name):
        n, no = len(self.arrays), len(self.out_shapes)

        def body(*refs):
            for stage in (self.copies, self.then):
                if stage is not None:
                    cps = stage(refs[:n], refs[n:n + no], *refs[n + no:])
                    for cp in cps:
                        cp.start()
                    for cp in cps:
                        cp.wait()

        return pl.pallas_call(
            body, name=name, out_shape=tuple(self.out_shapes), in_specs=[_ANY] * n, out_specs=tuple([_ANY] * no),
            scratch_shapes=self.sems(), compiler_params=_params(),
        )(*self.arrays)


def ride(rider, body, n_in, n_out, grid):
    if rider is None:
        return body
    ni, no = len(rider.arrays), len(rider.out_shapes)

    def wrapped(*refs):
        ins, r_in = refs[:n_in], refs[n_in:n_in + ni]
        outs = refs[n_in + ni:n_in + ni + n_out]
        r_out = refs[n_in + ni + n_out:n_in + ni + n_out + no]
        rest = refs[n_in + ni + n_out + no:]
        scratch, sems = rest[:len(rest) - 3], rest[len(rest) - 3:]
        step, total = 0, 1
        for a, g in enumerate(grid):
            step, total = step * g + pl.program_id(a), total * g
        assert total >= 3
        relay_at = (4 * total) // 5 if rider.then is not None else total - 1

        @pl.when(step == 0)
        def _():
            for cp in rider.copies(r_in, r_out, *sems):
                cp.start()

        body(*ins, *outs, *scratch)

        @pl.when(step == relay_at)
        def _():
            for cp in rider.copies(r_in, r_out, *sems):
                cp.wait()
            if rider.then is not None:
                for cp in rider.then(r_in, r_out, *sems):
                    cp.start()

        if rider.then is not None:
            @pl.when(step == total - 1)
            def _():
                for cp in rider.then(r_in, r_out, *sems):
                    cp.wait()

    return wrapped


def call_with_rider(body, name, rider, ops, in_specs, out_shape, out_specs, scratch, grid):
    n_in, n_out = len(ops), len(out_shape)
    ops, in_specs, out_shape, out_specs, scratch = list(ops), list(in_specs), list(out_shape), list(out_specs), list(scratch)
    if rider is not None:
        ops += rider.arrays
        in_specs += [_ANY] * len(rider.arrays)
        out_shape += rider.out_shapes
        out_specs += [_ANY] * len(rider.out_shapes)
        scratch += rider.sems()
    res = pl.pallas_call(
        ride(rider, body, n_in, n_out, grid), name=name, out_shape=tuple(out_shape), grid=grid, in_specs=in_specs,
        out_specs=tuple(out_specs), scratch_shapes=scratch, compiler_params=_params(("arbitrary",) * len(grid)),
    )(*ops)
    return tuple(res[:n_out]), list(res[n_out:])


def gather_rider(shards):
    nf = len(shards)
    half = lambda ref, which: pl.ds(which * (ref.shape[-2] // 2), ref.shape[-2] // 2)

    def copies(s_refs, o_refs, send_sems, recv_sems, local_sems):
        x, y, c = _coords()
        me = 2 * x + y
        cps = [pltpu.make_async_copy(s_refs[f], o_refs[f].at[me], local_sems.at[f]) for f in range(nf)]
        for k, (px, py) in enumerate(_other_chips(x, y)):
            for f in range(nf):
                rows = half(s_refs[f], c)
                cps.append(_remote(send_sems, recv_sems, k * nf + f, s_refs[f].at[rows], o_refs[f].at[me, rows], (px, py, c)))
        return cps

    def relay(s_refs, o_refs, send_sems, recv_sems, local_sems):
        x, y, c = _coords()
        cps = []
        for k, (px, py) in enumerate(_other_chips(x, y)):
            for f in range(nf):
                landed = o_refs[f].at[2 * px + py, half(s_refs[f], c)]
                cps.append(_remote(send_sems, recv_sems, (3 + k) * nf + f, landed, landed, (x, y, 1 - c)))
        return cps

    return Rider(shards, [jax.ShapeDtypeStruct((N_CHIPS,) + sh.shape, sh.dtype) for sh in shards], 6 * nf, nf, copies, relay)


def scatter_rider(pairs):
    nf = len(pairs)

    def copies(p_refs, o_refs, send_sems, recv_sems, local_sems):
        x, y, c = _coords()
        cps = []
        for k, (px, py) in enumerate(_other_chips(x, y)):
            for f in range(nf):
                cps.append(_remote(send_sems, recv_sems, k * nf + f, p_refs[f].at[2 * px + py], o_refs[f].at[k], (px, py, c)))
        return cps

    return Rider(pairs, [jax.ShapeDtypeStruct((3,) + p.shape[1:], p.dtype) for p in pairs], 3 * nf, 0, copies)


def pair_swap(grads, name):
    def body(g_refs, o_refs, send_sems, recv_sems):
        x, y, c = _coords()
        cps = []
        for f, g_ref in enumerate(g_refs):
            rh = g_ref.shape[1] // 2
            cps.append(_remote(send_sems, recv_sems, f, g_ref.at[:, pl.ds((1 - c) * rh, rh), :], o_refs[f], (x, y, 1 - c)))
        for cp in cps:
            cp.start()
        for cp in cps:
            cp.wait()

    outs = [jax.ShapeDtypeStruct((g.shape[0], g.shape[1] // 2, g.shape[2]), g.dtype) for g in grads]
    return _exchange_call(body, name, grads, outs, len(grads))


def half_swap(halves, name):
    def body(h_refs, o_refs, send_sems, recv_sems):
        x, y, c = _coords()
        cps = [_remote(send_sems, recv_sems, f, h_ref, o_refs[f], (x, y, 1 - c)) for f, h_ref in enumerate(h_refs)]
        for cp in cps:
            cp.start()
        for cp in cps:
            cp.wait()

    return _exchange_call(body, name, halves, [jax.ShapeDtypeStruct(h.shape, h.dtype) for h in halves], len(halves))


def allsum_small(part, name):
    def body(p_ref, tot_ref, all_ref, send_sems, recv_sems):
        x, y, c = _coords()
        me, sibling = (x, y, c), (x, y, 1 - c)
        chips = _other_chips(x, y)

        def slot(px, py, pc):
            return all_ref.at[4 * px + 2 * py + pc]

        def copy(k, block, to, src=None):
            return pltpu.make_async_remote_copy(src_ref=slot(*block) if src is None else src, dst_ref=slot(*block),
                                                send_sem=send_sems.at[k], recv_sem=recv_sems.at[k], device_id=to, device_id_type=MESH)

        slot(*me)[...] = p_ref[...]
        first = [copy(0, me, sibling, src=p_ref)] + [copy(1 + j, me, (*chip, c), src=p_ref) for j, chip in enumerate(chips)]
        for cp in first:
            cp.start()
        passed = [copy(4 + j, (*chip, c), sibling) for j, chip in enumerate(chips)]
        for j, chip in enumerate(chips):
            copy(1 + j, (*chip, c), me).wait_recv()
            passed[j].start()
        copy(0, sibling, me).wait_recv()
        for j, chip in enumerate(chips):
            copy(4 + j, (*chip, 1 - c), me).wait_recv()
        for cp in first + passed:
            cp.wait_send()
        tot = all_ref[0]
        for d in range(1, 8):
            tot = tot + all_ref[d]
        tot_ref[...] = tot

    vm = pl.BlockSpec(memory_space=pltpu.VMEM)
    return pl.pallas_call(
        body, name=name, out_shape=jax.ShapeDtypeStruct(part.shape, F32), in_specs=[vm], out_specs=vm,
        scratch_shapes=[pltpu.VMEM((8,) + part.shape, F32), pltpu.SemaphoreType.DMA((7,)), pltpu.SemaphoreType.DMA((7,))],
        compiler_params=_params(),
    )(part)


QKVF_COLS = 772
QKVF_PAD = 896


def _tables_for(s):
    return _rot_tables(s)


def layer_families(layer):
    return (0, 1, layer // 2) if layer % 2 == 0 else (2, 3, layer // 2)


class GradientExchange:
    def __init__(self):
        self.core = lax.axis_index("c").astype(jnp.int32).reshape(1)
        self.chip = (2 * lax.axis_index("x") + lax.axis_index("y")).astype(jnp.int32).reshape(1)
        self.pairs, self.arrived, self.pending, self.riding = {}, {}, [], []

    def add(self, items, tag):
        got = pair_swap([g for _, _, g in items], f"grad_pair_swap_{tag}")
        for (fam, li, g), r in zip(items, got):
            self.pairs[(fam, li)] = pair_sum(g, r, self.core, f"grad_pair_sum_{fam}_{li}")
            self.pending.append((fam, li))

    def rider(self):
        self.riding, self.pending = self.pending, []
        return scatter_rider([self.pairs[k] for k in self.riding])

    def landed(self, outs):
        for k, o in zip(self.riding, outs):
            self.arrived[k] = o
        self.riding = []

    def finish(self, weights, moments1, moments2):
        if self.pending:
            self.landed(self.rider().run("grad_chip_scatter_last"))
        mine = []
        for fam, w in enumerate(weights):
            buf = jnp.zeros((w.shape[0], w.shape[1] // 2, w.shape[2]), F32)
            for li in range(w.shape[0]):
                buf = chip_sum(self.pairs[(fam, li)], self.arrived[(fam, li)], self.chip, li, buf, f"grad_chip_sum_{fam}_{li}")
            mine.append(buf)
        other = half_swap(mine, "grad_half_swap")
        return [adamw_family(w, m, v, gm, go, self.core, f"adamw_{f}")
                for f, (w, m, v, gm, go) in enumerate(zip(weights, moments1, moments2, mine, other))]


class KeepGradients:
    def __init__(self):
        self.grads = {}

    def add(self, items, tag):
        for fam, li, g in items:
            self.grads[(fam, li)] = g

    def rider(self):
        return None

    def landed(self, outs):
        pass


def kernel(x, norm_mix, w_qkv_even, w_o_even, w_qkvf_odd, b_forget, w_o_odd, norm_ffn, w_ffn_in, w_ffn_out, norm_final, loss_target, m_norm_mix, m_w_qkv_even, m_w_o_even, m_w_qkvf_odd, m_b_forget, m_w_o_odd, m_norm_ffn, m_w_ffn_in, m_w_ffn_out, m_norm_final, v_norm_mix, v_w_qkv_even, v_w_o_even, v_w_qkvf_odd, v_b_forget, v_w_o_odd, v_norm_ffn, v_w_ffn_in, v_w_ffn_out, v_norm_final):
    w_shards = [w_qkv_even, w_o_even, w_qkvf_odd, w_o_odd, w_ffn_in, w_ffn_out]
    shards = [w.astype(BF16) for w in w_shards]
    tables = _tables_for(x.shape[1])
    bias_pad = jnp.pad(b_forget, ((0, 0), (0, LANES - N_HEADS)))

    w_qkv, w_o = gather_rider([shards[0][0], shards[1][0]]).run("gather_first")
    saved, cur = [], x[0]
    for layer in range(DEPTH):
        coming = [shards[4][layer], shards[5][layer]]
        if layer + 1 < DEPTH:
            fam_qkv, fam_o, li = layer_families(layer + 1)
            coming += [shards[fam_qkv][li], shards[fam_o][li]]
        cur, keep, rode = forward_layer(layer, cur, w_qkv, w_o, norm_mix[layer:layer + 1], norm_ffn[layer:layer + 1], tables,
                                        bias_pad[layer // 2:layer // 2 + 1], rider=gather_rider(coming))
        saved.append(keep)
        if layer + 1 < DEPTH:
            w_qkv, w_o = rode[2], rode[3]

    dcur, g_final, loss_part = loss_head(cur, norm_final.reshape(1, D_MODEL), loss_target[0], "loss_head")

    exchange = GradientExchange()
    g_mix, g_ffn, g_bias = [None] * DEPTH, [None] * DEPTH, [None] * (DEPTH // 2)
    for layer in reversed(range(DEPTH)):
        dcur, g_mix[layer], g_ffn[layer], g_b = backward_layer(layer, dcur, saved[layer], norm_mix[layer:layer + 1],
                                                               norm_ffn[layer:layer + 1], tables, bias_pad[layer // 2:layer // 2 + 1], exchange)
        if g_b is not None:
            g_bias[layer // 2] = g_b

    zero_row = jnp.zeros((1, D_MODEL), F32)
    pad16 = lambda v: jnp.pad(v, (0, D_MODEL - v.shape[0]))[None, :]
    small_rows = lambda mix, ffn, fin, bias, last: jnp.concatenate(
        [r.reshape(1, D_MODEL) for r in mix] + [r.reshape(1, D_MODEL) for r in ffn] + [fin.reshape(1, D_MODEL)]
        + [pad16(b) for b in bias] + [last] + [zero_row] * (SMALL_ROWS - 12), axis=0)
    loss_row = pad16(loss_part[0, :1])
    small_g = allsum_small(small_rows(g_mix, g_ffn, g_final, g_bias, loss_row), "allsum_small")
    loss = small_g[11, 0]
    small_g = small_g.at[11].set(0.0)
    sw = small_rows(list(norm_mix), list(norm_ffn), norm_final, list(b_forget), zero_row)
    sm = small_rows(list(m_norm_mix), list(m_norm_ffn), m_norm_final, list(m_b_forget), zero_row)
    sv = small_rows(list(v_norm_mix), list(v_norm_ffn), v_norm_final, list(v_b_forget), zero_row)
    sd, snm, snv = adamw(sw, small_g, sm, sv, "adamw_small")

    def small_out(a):
        return a[0:4], a[8, :], a[9:11, :N_HEADS], a[4:8]

    widen = lambda t: jnp.pad(t, ((0, 0), (0, 0), (0, QKVF_PAD - QKVF_COLS)))
    padded = lambda ws: [widen(t) if f == 2 else t for f, t in enumerate(ws)]
    big = exchange.finish(
        padded(w_shards), padded([m_w_qkv_even, m_w_o_even, m_w_qkvf_odd, m_w_o_odd, m_w_ffn_in, m_w_ffn_out]),
        padded([v_w_qkv_even, v_w_o_even, v_w_qkvf_odd, v_w_o_odd, v_w_ffn_in, v_w_ffn_out]))

    def outputs(small, which):
        mix, fin, bias, ffn = small_out(small)
        qkv_e, o_e, qkvf, o_o, fi, fo = [big[f][which][:, :, :QKVF_COLS] if f == 2 else big[f][which] for f in range(6)]
        return [mix, qkv_e, o_e, qkvf, bias, o_o, ffn, fi, fo, fin]

    return (loss, dcur[None], *outputs(small_g, 0), *outputs(sd, 1), *outputs(snm, 2), *outputs(snv, 3))


def _chip_tile(rows, cols, at):
    return pl.BlockSpec((None, rows, cols), at)


def forward_layer(layer, cur, w_qkv, w_o, mix_gain, ffn_gain, tables, bias_row, rider=None, w_ffn=None):
    n = f"l{layer}"
    s = cur.shape[0]
    h1 = rmsnorm_fwd(cur, mix_gain, n + "_norm_mix")
    keep = {"x": cur, "h1": h1, "w_o": w_o.reshape(D_ATTN, D_MODEL)}
    if layer % 2 == 0:
        qkv = matmul(h1, w_qkv, "nn", BF16, n + "_qkv", 512, 768, 1024, mnk=(s, 3 * D_ATTN, D_MODEL),
                     b_spec=_chip_tile(D_MODEL, 768, lambda i, j, kk: (j, 0, 0)))
        o_sb, st, rode = causal_fwd(qkv, 4, "sb", n + "_sb_fwd", rider=rider)
        qd, kd, vd = rotary_prep(qkv, tables, n + "_rotary")
        o_dil, lse_dil = dilated_fwd(qd, kd, vd, n + "_dil_fwd")
        attn = jnp.concatenate([o_sb, o_dil], axis=1)
        keep.update(qd=qd, kd=kd, vd=vd, o_dil=o_dil, lse_dil=lse_dil, w_qkv=w_qkv)
    else:
        natural = jnp.transpose(w_qkv, (1, 0, 2)).reshape(D_MODEL, N_CHIPS * QKVF_COLS)
        w_gate = jnp.pad(natural[:, 3 * D_ATTN:], ((0, 0), (0, LANES - N_HEADS)))
        qkv = matmul(h1, natural[:, :3 * D_ATTN], "nn", BF16, n + "_qkv", 512, 768, 1024)
        fl = matmul(h1, w_gate, "nn", F32, n + "_fgate", 512, LANES, 1024)
        cum = forget_fwd(fl, bias_row, n + "_forget_fwd")
        f_heads = cum[:, :N_HEADS].T
        fq = jnp.broadcast_to(f_heads[:, :, None], (N_HEADS, s, LANES))
        fk = f_heads.reshape(N_HEADS // 2, 2, s)
        attn, st, rode = causal_fwd(qkv, 8, "fox", n + "_fox_fwd", fq=fq, fk=fk, rider=rider)
        keep.update(fl=fl, fq=fq, fk=fk, w_qkv=jnp.concatenate([natural[:, :3 * D_ATTN], w_gate], axis=1))
    w_fi, w_fo = (rode[0], rode[1]) if rider is not None else w_ffn
    w_fo = w_fo.reshape(D_FF, D_MODEL)
    mid = matmul(attn, keep["w_o"], "nn", F32, n + "_attn_out", 512, 1024, 1024, res=cur)
    h2 = rmsnorm_fwd(mid, ffn_gain, n + "_norm_ffn")
    gu = matmul(h2, w_fi, "nn", F32, n + "_ffn_in", 512, 1408, 1024, mnk=(s, 2 * D_FF, D_MODEL),
                b_spec=_chip_tile(D_MODEL, 1408, lambda i, j, kk: (j, 0, 0)))
    act = swiglu_fwd(gu, n + "_swiglu")
    out = matmul(act, w_fo, "nn", F32, n + "_ffn_out", 512, 1024, D_FF, res=mid)
    keep.update(qkv=qkv, st=st, attn=attn, mid=mid, h2=h2, gu=gu, act=act, w_fi=w_fi, w_fo=w_fo)
    return out, keep, rode


def backward_layer(layer, dcur, kp, mix_gain, ffn_gain, tables, bias_row, exchange):
    n = f"l{layer}"
    s = dcur.shape[0]
    fam_qkv, fam_o, li = layer_families(layer)
    g_fo = matmul(kp["act"], dcur, "tn", BF16, n + "_d_w_ffn_out", 1408, 1024, 512)
    dact = matmul(dcur, kp["w_fo"], "nt", F32, n + "_d_act", 512, 1408, 1024)
    dgu = swiglu_bwd(kp["gu"], dact, n + "_d_swiglu")
    g_fi = matmul(kp["h2"], dgu, "tn", BF16, n + "_d_w_ffn_in", 1024, 1408, 2048, mnk=(D_MODEL, 2 * D_FF, s),
                  o_spec=_chip_tile(D_MODEL, 1408, lambda i, j, kk: (j, 0, 0)), out_shape=(N_CHIPS, D_MODEL, 1408))
    dh2 = matmul(dgu, kp["w_fi"], "nt", F32, n + "_d_h2", 512, 1024, 1408, mnk=(s, D_MODEL, 2 * D_FF),
                 b_spec=_chip_tile(D_MODEL, 1408, lambda i, j, kk: (kk, 0, 0)))
    dmid, g_ffn = rmsnorm_bwd(kp["mid"], ffn_gain, dh2, dcur, n + "_d_norm_ffn")
    g_o = matmul(kp["attn"], dmid, "tn", BF16, n + "_d_w_o", 1024, 1024, 512)
    dattn = matmul(dmid, kp["w_o"], "nt", F32, n + "_d_attn", 512, 1024, 1024)
    exchange.add([(5, layer, g_fo.reshape(N_CHIPS, D_FF // N_CHIPS, D_MODEL)), (4, layer, g_fi),
                  (fam_o, li, g_o.reshape(N_CHIPS, D_ATTN // N_CHIPS, D_MODEL))], f"l{layer}_ffn")
    rider = exchange.rider()
    g_bias = None
    if layer % 2 == 0:
        dq_a, dk_a, dv_a, rode = causal_bwd(kp["qkv"], dattn, kp["st"], 4, "sb", n + "_sb_bwd", rider=rider)
        dqd, dkd, dvd = dilated_bwd(kp["qd"], kp["kd"], kp["vd"], dattn, kp["o_dil"], kp["lse_dil"], 4, n + "_dil_bwd")
        dq_b, dk_b, dv_b = rotary_bwd(dqd, dkd, dvd, tables, n + "_d_rotary")
        dproj = jnp.concatenate([dq_a, dq_b, dk_a.astype(BF16), dk_b, dv_a.astype(BF16), dv_b], axis=1)
        g_qkv = matmul(kp["h1"], dproj, "tn", BF16, n + "_d_w_qkv", 1024, 768, 2048, mnk=(D_MODEL, 3 * D_ATTN, s),
                       o_spec=_chip_tile(D_MODEL, 768, lambda i, j, kk: (j, 0, 0)), out_shape=(N_CHIPS, D_MODEL, 768))
        dh1 = matmul(dproj, kp["w_qkv"], "nt", F32, n + "_d_h1", 512, 1024, 768, mnk=(s, D_MODEL, 3 * D_ATTN),
                     b_spec=_chip_tile(D_MODEL, 768, lambda i, j, kk: (kk, 0, 0)))
    else:
        dq_f, dk_f, dv_f, dfk, rode = causal_bwd(kp["qkv"], dattn, kp["st"], 8, "fox", n + "_fox_bwd", fq=kp["fq"], fk=kp["fk"],
                                                 rider=rider)
        dcum = jnp.pad(dfk.reshape(N_HEADS, s).T, ((0, 0), (0, LANES - N_HEADS)))
        dfl, dbias = forget_bwd(kp["fl"], bias_row, dcum, n + "_forget_bwd")
        g_bias = dbias[0, :N_HEADS]
        dproj = jnp.concatenate([dq_f, dk_f.astype(BF16), dv_f.astype(BF16), dfl.astype(BF16)], axis=1)
        by_chip = dproj[:, :N_CHIPS * QKVF_COLS].reshape(s, N_CHIPS, QKVF_COLS)
        by_chip = jnp.pad(by_chip, ((0, 0), (0, 0), (0, QKVF_PAD - QKVF_COLS))).reshape(s, N_CHIPS * QKVF_PAD)
        g_qkv = matmul(kp["h1"], by_chip, "tn", BF16, n + "_d_w_qkv", 1024, QKVF_PAD, 2048, mnk=(D_MODEL, N_CHIPS * QKVF_PAD, s),
                       o_spec=_chip_tile(D_MODEL, QKVF_PAD, lambda i, j, kk: (j, 0, 0)), out_shape=(N_CHIPS, D_MODEL, QKVF_PAD))
        dh1 = matmul(dproj, kp["w_qkv"], "nt", F32, n + "_d_h1", 512, 1024, 640)
    exchange.landed(rode)
    exchange.add([(fam_qkv, li, g_qkv)], f"l{layer}_qkv")
    dx, g_mix = rmsnorm_bwd(kp["x"], mix_gain, dh1, dmid, n + "_d_norm_mix")
    return dx, g_mix, g_ffn, g_bias


def local_step(xs, target, norm_mix, norm_ffn, norm_final, b_forget, layer_weights):
    tables = _tables_for(xs.shape[0])
    bias_pad = jnp.pad(b_forget, ((0, 0), (0, LANES - N_HEADS)))
    saved, cur = [], xs
    for layer in range(DEPTH):
        w_qkv, w_o, w_fi, w_fo = layer_weights[layer]
        cur, keep, _ = forward_layer(layer, cur, w_qkv, w_o, norm_mix[layer:layer + 1], norm_ffn[layer:layer + 1], tables,
                                     bias_pad[layer // 2:layer // 2 + 1], w_ffn=(w_fi, w_fo))
        saved.append(keep)
    dcur, g_final, loss_part = loss_head(cur, norm_final.reshape(1, D_MODEL), target, "loss_head")
    keeper = KeepGradients()
    g_mix, g_ffn, g_bias = [None] * DEPTH, [None] * DEPTH, [None] * (DEPTH // 2)
    for layer in reversed(range(DEPTH)):
        dcur, g_mix[layer], g_ffn[layer], g_b = backward_layer(layer, dcur, saved[layer], norm_mix[layer:layer + 1],
                                                               norm_ffn[layer:layer + 1], tables, bias_pad[layer // 2:layer // 2 + 1], keeper)
        if g_b is not None:
            g_bias[layer // 2] = g_b
    return dcur, keeper.grads, (g_mix, g_ffn, g_final, g_bias), loss_part
```

```python
import functools

import jax
import jax.numpy as jnp
from jax import lax
from jax.experimental import pallas as pl
from jax.experimental.pallas import tpu as pltpu

F32 = jnp.float32
BF16 = jnp.bfloat16
MESH = pl.DeviceIdType.MESH

D_MODEL = 1024
DEPTH = 4
HEAD_DIM = 64
N_HEADS = 16
D_ATTN = 1024
D_FF = 2816
ROPE_THETA = 500000.0
ROT_HALF = 8
RMS_EPS = 1e-5
DIL_STRIDES = (1, 4, 16)
ADAM_LR, ADAM_B1, ADAM_B2, ADAM_EPS, ADAM_WD, ADAM_STEP = 0.001, 0.9, 0.999, 1e-8, 0.01, 10

LANES = 128
BLK = 128
VMEM_LIMIT = 56 * 1024 * 1024
NEG = -1e30
N_CHIPS = 4
FLAT_COLS = 1024
FLAT_ROWS = 12800
HALF_ROWS = FLAT_ROWS // 2
SMALL_ROWS = 16


def _params(sem=None):
    return pltpu.CompilerParams(dimension_semantics=sem, vmem_limit_bytes=VMEM_LIMIT)


def _dot(a, b):
    return lax.dot_general(a, b, (((1,), (0,)), ((), ())), preferred_element_type=F32)


def _dot_nt(a, b):
    return lax.dot_general(a, b, (((1,), (1,)), ((), ())), preferred_element_type=F32)


def _dot_tn(a, b):
    return lax.dot_general(a, b, (((0,), (0,)), ((), ())), preferred_element_type=F32)


def _split3(x):
    x1 = x.astype(BF16)
    r1 = x - x1.astype(F32)
    x2 = r1.astype(BF16)
    x3 = (r1 - x2.astype(F32)).astype(BF16)
    return x1, x2, x3


def _dot_exact_lhs(x, t):
    x1, x2, x3 = _split3(x)
    return _dot(x1, t) + _dot(x2, t) + _dot(x3, t)


def _dot_exact_rhs(t, x):
    x1, x2, x3 = _split3(x)
    return _dot(t, x1) + _dot(t, x2) + _dot(t, x3)


def _iotas(shape=(BLK, LANES)):
    return lax.broadcasted_iota(jnp.int32, shape, 0), lax.broadcasted_iota(jnp.int32, shape, 1)


_DIMS = {"nn": (((1,), (0,)), ((), ())), "nt": (((1,), (1,)), ((), ())), "tn": (((0,), (0,)), ((), ()))}


def matmul(a, b, mode, out_dtype, name, tm, tn, tk, res=None, mnk=None, b_spec=None, o_spec=None, out_shape=None, into=None):
    if mnk is not None:
        m, n, k = mnk
    elif mode == "nn":
        (m, k), (k2, n) = a.shape, b.shape
    elif mode == "nt":
        (m, k), (n, k2) = a.shape, b.shape
    else:
        (k, m), (k2, n) = a.shape, b.shape
    assert m % tm == 0 and n % tn == 0 and k % tk == 0, (name, a.shape, b.shape)
    nk = k // tk
    a_spec = pl.BlockSpec((tk, tm), lambda i, j, kk: (kk, i)) if mode == "tn" else pl.BlockSpec((tm, tk), lambda i, j, kk: (i, kk))
    if b_spec is None:
        b_spec = pl.BlockSpec((tn, tk), lambda i, j, kk: (j, kk)) if mode == "nt" else pl.BlockSpec((tk, tn), lambda i, j, kk: (kk, j))
    r_spec = pl.BlockSpec((tm, tn), lambda i, j, kk: (i, j))
    if o_spec is None:
        o_spec = r_spec
    dims = _DIMS[mode]
    has_res = res is not None
    n_in = 2 + int(has_res) + int(into is not None)

    def body(*refs):
        a_ref, b_ref = refs[0], refs[1]
        r_ref = refs[2] if has_res else None
        o_ref = refs[n_in]

        def finish(v):
            if has_res:
                v = v + r_ref[...]
            o_ref[...] = v.astype(out_dtype)

        p = lax.dot_general(a_ref[...].astype(BF16), b_ref[...].astype(BF16), dims, preferred_element_type=F32)
        if nk == 1:
            finish(p)
        else:
            acc = refs[-1]
            kk = pl.program_id(2)

            @pl.when(kk == 0)
            def _():
                acc[...] = p

            @pl.when(kk > 0)
            def _():
                acc[...] += p

            @pl.when(kk == nk - 1)
            def _():
                finish(acc[...])

    ops = [a, b] + ([res] if has_res else []) + ([into] if into is not None else [])
    specs = [a_spec, b_spec] + ([r_spec] if has_res else []) + ([_ANY] if into is not None else [])
    if into is not None:
        out_shape = jax.ShapeDtypeStruct(into.shape, into.dtype)
    else:
        out_shape = jax.ShapeDtypeStruct((m, n) if out_shape is None else out_shape, out_dtype)
    return pl.pallas_call(
        body, name=name, out_shape=out_shape,
        grid=(m // tm, n // tn, nk), in_specs=specs, out_specs=o_spec,
        scratch_shapes=[pltpu.VMEM((tm, tn), F32)] if nk > 1 else [],
        input_output_aliases={n_in - 1: 0} if into is not None else {},
        compiler_params=_params(("parallel", "parallel", "arbitrary")),
    )(*ops)


ROWS = 256


def _row_spec(cols, rows=ROWS):
    return pl.BlockSpec((rows, cols), lambda i: (i, 0))


def _fix_spec(r, cols):
    return pl.BlockSpec((r, cols), lambda i: (0, 0))


def rmsnorm_fwd(x, g, name):
    s, d = x.shape

    def body(x_ref, g_ref, h_ref):
        xv = x_ref[...]
        rstd = lax.rsqrt(jnp.mean(xv * xv, axis=-1, keepdims=True) + RMS_EPS)
        h_ref[...] = (xv * rstd * g_ref[...]).astype(BF16)

    return pl.pallas_call(
        body, name=name, out_shape=jax.ShapeDtypeStruct((s, d), BF16), grid=(s // ROWS,),
        in_specs=[_row_spec(d), _fix_spec(1, d)], out_specs=_row_spec(d), compiler_params=_params(("parallel",)),
    )(x, g)


def _rms_bwd_math(xv, gv, dh):
    rstd = lax.rsqrt(jnp.mean(xv * xv, axis=-1, keepdims=True) + RMS_EPS)
    xhat = xv * rstd
    u = dh * gv
    dx = rstd * (u - xhat * jnp.mean(u * xhat, axis=-1, keepdims=True))
    return dx, dh * xhat


def rmsnorm_bwd(x, g, dh, dres, name):
    s, d = x.shape

    def body(x_ref, g_ref, dh_ref, dres_ref, dx_ref, dg_ref):
        dx, dgt = _rms_bwd_math(x_ref[...], g_ref[...], dh_ref[...])
        dx_ref[...] = dres_ref[...] + dx
        part = jnp.sum(dgt, axis=0, keepdims=True)

        @pl.when(pl.program_id(0) == 0)
        def _():
            dg_ref[...] = part

        @pl.when(pl.program_id(0) > 0)
        def _():
            dg_ref[...] += part

    return pl.pallas_call(
        body, name=name, out_shape=(jax.ShapeDtypeStruct((s, d), F32), jax.ShapeDtypeStruct((1, d), F32)),
        grid=(s // ROWS,), in_specs=[_row_spec(d), _fix_spec(1, d), _row_spec(d), _row_spec(d)],
        out_specs=(_row_spec(d), _fix_spec(1, d)), compiler_params=_params(("arbitrary",)),
    )(x, g, dh, dres)


def loss_head(x, g, target, name):
    s, d = x.shape

    def body(x_ref, g_ref, t_ref, dx_ref, dg_ref, loss_ref):
        xv, gv = x_ref[...], g_ref[...]
        rstd = lax.rsqrt(jnp.mean(xv * xv, axis=-1, keepdims=True) + RMS_EPS)
        err = xv * rstd * gv - t_ref[...]
        dx, dgt = _rms_bwd_math(xv, gv, err * (1.0 / d))
        dx_ref[...] = dx
        part = jnp.sum(dgt, axis=0, keepdims=True)
        lpart = jnp.full((1, LANES), 0.5 / d, F32) * jnp.sum(err * err)

        @pl.when(pl.program_id(0) == 0)
        def _():
            dg_ref[...] = part
            loss_ref[...] = lpart

        @pl.when(pl.program_id(0) > 0)
        def _():
            dg_ref[...] += part
            loss_ref[...] += lpart

    return pl.pallas_call(
        body, name=name,
        out_shape=(jax.ShapeDtypeStruct((s, d), F32), jax.ShapeDtypeStruct((1, d), F32), jax.ShapeDtypeStruct((1, LANES), F32)),
        grid=(s // ROWS,), in_specs=[_row_spec(d), _fix_spec(1, d), _row_spec(d)],
        out_specs=(_row_spec(d), _fix_spec(1, d), _fix_spec(1, LANES)), compiler_params=_params(("arbitrary",)),
    )(x, g, target)


def swiglu_fwd(gu, name):
    s, f2 = gu.shape
    f = f2 // 2

    def body(gu_ref, a_ref):
        gv, uv = gu_ref[:, :f].astype(F32), gu_ref[:, f:].astype(F32)
        a_ref[...] = (gv *(1.0 / (1.0 + jnp.exp(-gv))) * uv).astype(BF16)

    return pl.pallas_call(
        body, name=name, out_shape=jax.ShapeDtypeStruct((s, f), BF16), grid=(s // ROWS,),
        in_specs=[_row_spec(f2)], out_specs=_row_spec(f), compiler_params=_params(("parallel",)),
    )(gu)


def swiglu_bwd(gu, dact, name):
    s, f2 = gu.shape
    f = f2 // 2

    def body(gu_ref, da_ref, o_ref):
        gv, uv, da = gu_ref[:, :f].astype(F32), gu_ref[:, f:].astype(F32), da_ref[...].astype(F32)
        sg = 1.0 / (1.0 + jnp.exp(-gv))
        o_ref[:, :f] = (da * uv * sg * (1.0 + gv * (1.0 - sg))).astype(BF16)
        o_ref[:, f:] = (da * gv * sg).astype(BF16)

    return pl.pallas_call(
        body, name=name, out_shape=jax.ShapeDtypeStruct((s, f2), BF16), grid=(s // ROWS,),
        in_specs=[_row_spec(f2), _row_spec(f)], out_specs=_row_spec(f2), compiler_params=_params(("parallel",)),
    )(gu, dact)


Q_OFF, K_OFF, V_OFF = 0, 8, 16


KB = 512
BQ = 256
SUB = KB // BLK


def _softplus_parts(z):
    sp = jnp.log(1.0 + jnp.exp(-jnp.abs(z)))
    ls = jnp.minimum(z, 0.0) - sp
    return ls, ls - z


def _wide(t):
    return jnp.concatenate([t] * SUB, axis=1)


def _chunk_dots(x, tri):
    terms = []
    for u in range(SUB):
        terms += list(_split3(x[:, u * BLK:(u + 1) * BLK]))
    r = _dot(jnp.concatenate(terms, axis=0), tri)
    rows = x.shape[0]
    piece = lambda n: r[n * rows:(n + 1) * rows]
    return [piece(3 * u) + piece(3 * u + 1) + piece(3 * u + 2) for u in range(SUB)]


def _block_suffix_sums(x, suffix, c):
    loc = _chunk_dots(x, suffix)
    out = [None] * SUB
    for u in reversed(range(SUB)):
        out[u] = loc[u] + c
        c = c + jnp.sum(x[:, u * BLK:(u + 1) * BLK], axis=1, keepdims=True)
    return jnp.concatenate(out, axis=1), c


def _block_prefix_sums(x, tri, c):
    loc = _chunk_dots(x, tri)
    out = []
    for u in range(SUB):
        out.append(loc[u] + c)
        c = c + jnp.sum(x[:, u * BLK:(u + 1) * BLK], axis=1, keepdims=True)
    return jnp.concatenate(out, axis=1), c


def causal_fwd(qkv, npairs, mode, name, fq=None, fk=None, rider=None):
    s = qkv.shape[0]
    nq = s // BQ
    fox = mode == "fox"

    def body(*refs):
        if fox:
            q_ref, k_ref, v_ref, fq_ref, fk_ref, o_ref, st_ref = refs
        else:
            q_ref, k_ref, v_ref, o_ref, st_ref = refs
        i = pl.program_id(1)
        nkb = (i * BQ + BQ - 1) // KB + 1
        row, lane = _iotas((BQ, KB))
        row_s, lane_s = _iotas()
        _, lane_q = _iotas((BQ, LANES))
        qpos = i * BQ + row
        qf = q_ref[...].astype(F32)
        hms = (lane_q < HEAD_DIM, lane_q >= HEAD_DIM)
        qas = [jnp.where(hm, qf, 0.0).astype(BF16) for hm in hms]
        suffix = jnp.where(row_s > lane_s, 1.0, 0.0).astype(BF16)
        zero = jnp.zeros((BQ, LANES), F32)
        col0 = jnp.zeros((BQ, 1), F32)

        def kv(j):
            r0 = pl.multiple_of(j * KB, KB)
            return r0, k_ref[pl.ds(r0, KB), :], v_ref[pl.ds(r0, KB), :]

        if fox:
            fqs = [_wide(fq_ref[a]) for a in range(2)]

            def step(j, carry):
                r0, kb, vb = kv(j)
                ok = r0 + lane <= qpos
                new = []
                for a in range(2):
                    acc, mx, l = carry[3 * a:3 * a + 3]
                    z = _dot_nt(qas[a], kb) * 0.125 + fqs[a] - fk_ref[a:a + 1, pl.ds(r0, KB)]
                    z = jnp.where(ok, z, NEG)
                    mnew = jnp.maximum(mx, jnp.max(z, axis=1, keepdims=True))
                    p = jnp.exp(z - mnew)
                    alpha = jnp.exp(mx - mnew)
                    new += [alpha * acc + _dot(p.astype(BF16), vb), mnew, alpha * l + jnp.sum(p, axis=1, keepdims=True)]
                return tuple(new)

            neg = jnp.full((BQ, 1), NEG, F32)
            res = lax.fori_loop(0, nkb, step, (zero, neg, col0, zero, neg, col0))
            outs = [res[3 * a] / res[3 * a + 2] for a in range(2)]
            stats = [res[3 * a + 1] + jnp.log(res[3 * a + 2]) for a in range(2)]
        else:
            def step(jj, carry):
                r0, kb, vb = kv(nkb - 1 - jj)
                strict = r0 + lane < qpos
                new = []
                for a in range(2):
                    acc, c = carry[2 * a:2 * a + 2]
                    ls, lm = _softplus_parts(_dot_nt(qas[a], kb) * 0.125)
                    lm = jnp.where(strict, lm, 0.0)
                    between, c = _block_suffix_sums(lm, suffix, c)
                    aw = jnp.where(strict, jnp.exp(ls + between), 0.0)
                    new += [acc + _dot(aw.astype(BF16), vb), c]
                return tuple(new)

            res = lax.fori_loop(0, nkb, step, (zero, col0, zero, col0))
            outs, stats = [res[0], res[2]], [res[1], res[3]]
        o_ref[...] = jnp.where(hms[0], outs[0], outs[1])
        for a in range(2):
            st_ref[a] = jnp.broadcast_to(stats[a], (BQ, LANES))

    col = lambda off: (lambda p, i: (0, off + p))
    in_specs = [pl.BlockSpec((BQ, LANES), lambda p, i: (i, Q_OFF + p)),
                pl.BlockSpec((s, LANES), col(K_OFF)), pl.BlockSpec((s, LANES), col(V_OFF))]
    ops = [qkv, qkv, qkv]
    if fox:
        in_specs += [pl.BlockSpec((2, BQ, LANES), lambda p, i: (p, i, 0)), pl.BlockSpec((None, 2, s), lambda p, i: (p, 0, 0))]
        ops += [fq, fk]
    (o, stat), rode = call_with_rider(
        body, name, rider, ops, in_specs,
        [jax.ShapeDtypeStruct((s, npairs * LANES), F32), jax.ShapeDtypeStruct((2 * npairs, s, LANES), F32)],
        [pl.BlockSpec((BQ, LANES), lambda p, i: (i, p)), pl.BlockSpec((2, BQ, LANES), lambda p, i: (p, i, 0))], [], (npairs, nq))
    return o, stat, rode


def causal_bwd(qkv, do, stat, npairs, mode, name, fq=None, fk=None, rider=None):
    s = qkv.shape[0]
    nq = s // BQ
    fox = mode == "fox"

    def body(*refs):
        if fox:
            q_ref, k_ref, v_ref, do_ref, st_ref, fq_ref, fk_ref, dq_ref, dk_ref, dv_ref, df_ref, p_s, dp_s = refs
        else:
            q_ref, k_ref, v_ref, do_ref, st_ref, dq_ref, dk_ref, dv_ref = refs
        i = pl.program_id(1)

        @pl.when(i == 0)
        def _():
            dk_ref[...] = jnp.zeros_like(dk_ref)
            dv_ref[...] = jnp.zeros_like(dv_ref)
            if fox:
                df_ref[...] = jnp.zeros_like(df_ref)

        nkb = (i * BQ + BQ - 1) // KB + 1
        row, lane = _iotas((BQ, KB))
        row_s, lane_s = _iotas()
        _, lane_q = _iotas((BQ, LANES))
        qpos = i * BQ + row
        qf = q_ref[...].astype(F32)
        dov = do_ref[...]
        hms = (lane_q < HEAD_DIM, lane_q >= HEAD_DIM)
        qas = [jnp.where(hm, qf, 0.0).astype(BF16) for hm in hms]
        doas = [jnp.where(hm, dov, 0.0).astype(BF16) for hm in hms]
        stas = [_wide(st_ref[a]) for a in range(2)]
        zero = jnp.zeros((BQ, LANES), F32)
        col0 = jnp.zeros((BQ, 1), F32)

        def kv(j):
            r0 = pl.multiple_of(j * KB, KB)
            return r0, k_ref[pl.ds(r0, KB), :], v_ref[pl.ds(r0, KB), :]

        if fox:
            fqs = [_wide(fq_ref[a]) for a in range(2)]

            def probs(j, deltas):
                r0, kb, vb = kv(j)
                ok = r0 + lane <= qpos
                new = []
                for a in range(2):
                    z = _dot_nt(qas[a], kb) * 0.125 + fqs[a] - fk_ref[a:a + 1, pl.ds(r0, KB)]
                    p = jnp.where(ok, jnp.exp(z - stas[a]), 0.0)
                    dp = _dot_nt(doas[a], vb)
                    p_s[a, j] = p
                    dp_s[a, j] = dp
                    new.append(deltas[a] + jnp.sum(p * dp, axis=1, keepdims=True))
                return tuple(new)

            deltas = lax.fori_loop(0, nkb, probs, (col0, col0))

            def step(j, dqs):
                r0, kb, _ = kv(j)
                new = []
                dk = jnp.zeros((KB, LANES), F32)
                dv = jnp.zeros((KB, LANES), F32)
                for a in range(2):
                    p = p_s[a, j]
                    ds = p * (dp_s[a, j] - deltas[a])
                    dsb = (ds * 0.125).astype(BF16)
                    dk += _dot_tn(dsb, qas[a])
                    dv += _dot_tn(p.astype(BF16), doas[a])
                    df_ref[a:a + 1, pl.ds(r0, KB)] -= jnp.sum(ds, axis=0, keepdims=True)
                    new.append(dqs[a] + _dot(dsb, kb))
                dk_ref[pl.ds(r0, KB), :] += dk
                dv_ref[pl.ds(r0, KB), :] += dv
                return tuple(new)

            dqs = lax.fori_loop(0, nkb, step, (zero, zero))
        else:
            incl = jnp.where(row_s <= lane_s, 1.0, 0.0).astype(BF16)
            excl = jnp.where(row_s < lane_s, 1.0, 0.0).astype(BF16)

            def step(j, carry):
                r0, kb, vb = kv(j)
                strict = r0 + lane < qpos
                new = []
                dk = jnp.zeros((KB, LANES), F32)
                dv = jnp.zeros((KB, LANES), F32)
                for a in range(2):
                    dq, cm, cg = carry[3 * a:3 * a + 3]
                    ls, lm = _softplus_parts(_dot_nt(qas[a], kb) * 0.125)
                    lm = jnp.where(strict, lm, 0.0)
                    beta = jnp.exp(ls)
                    upto, cm = _block_prefix_sums(lm, incl, cm)
                    aw = jnp.where(strict, jnp.exp(ls + stas[a] - upto), 0.0)
                    g = aw * _dot_nt(doas[a], vb)
                    pre, cg = _block_prefix_sums(g, excl, cg)
                    dz = jnp.where(strict, g * (1.0 - beta) - pre * beta, 0.0)
                    dzb = (dz * 0.125).astype(BF16)
                    dk += _dot_tn(dzb, qas[a])
                    dv += _dot_tn(aw.astype(BF16), doas[a])
                    new += [dq + _dot(dzb, kb), cm, cg]
                dk_ref[pl.ds(r0, KB), :] += dk
                dv_ref[pl.ds(r0, KB), :] += dv
                return tuple(new)

            res = lax.fori_loop(0, nkb, step, (zero, col0, col0, zero, col0, col0))
            dqs = (res[0], res[3])
        dq_ref[...] = jnp.where(hms[0], dqs[0], dqs[1]).astype(BF16)

    col = lambda off: (lambda p, i: (0, off + p))
    blk = pl.BlockSpec((BQ, LANES), lambda p, i: (i, p))
    acc = pl.BlockSpec((s, LANES), lambda p, i: (0, p))
    st_spec = pl.BlockSpec((2, BQ, LANES), lambda p, i: (p, i, 0))
    in_specs = [pl.BlockSpec((BQ, LANES), lambda p, i: (i, Q_OFF + p)), pl.BlockSpec((s, LANES), col(K_OFF)),
                pl.BlockSpec((s, LANES), col(V_OFF)), blk, st_spec]
    ops = [qkv, qkv, qkv, do, stat]
    w = npairs * LANES
    out_shape = [jax.ShapeDtypeStruct((s, w), BF16), jax.ShapeDtypeStruct((s, w), F32), jax.ShapeDtypeStruct((s, w), F32)]
    out_specs = [blk, acc, acc]
    scratch = []
    if fox:
        fk_spec = pl.BlockSpec((None, 2, s), lambda p, i: (p, 0, 0))
        in_specs += [st_spec, fk_spec]
        ops += [fq, fk]
        out_shape.append(jax.ShapeDtypeStruct((npairs, 2, s), F32))
        out_specs.append(fk_spec)
        scratch = [pltpu.VMEM((2, s // KB, BQ, KB), F32)] * 2
    outs, rode = call_with_rider(body, name, rider, ops, in_specs, out_shape, out_specs, scratch, (npairs, nq))
    return (*outs, rode)


def forget_fwd(fl, bias, name):
    s = fl.shape[0]

    def body(fl_ref, b_ref, f_ref):
        row, lane = _iotas()
        lower = jnp.where(lane <= row, 1.0, 0.0).astype(BF16)

        def step(n, carry):
            r0 = pl.multiple_of(n * BLK, BLK)
            ls, _ = _softplus_parts(fl_ref[pl.ds(r0, BLK), :] + b_ref[...])
            blk = _dot_exact_rhs(lower, ls) + carry
            f_ref[pl.ds(r0, BLK), :] = blk
            return blk[BLK - 1:BLK, :]

        lax.fori_loop(0, s // BLK, step, jnp.zeros((1, LANES), F32))

    return pl.pallas_call(
        body, name=name, out_shape=jax.ShapeDtypeStruct((s, LANES), F32),
        in_specs=[pl.BlockSpec(memory_space=pltpu.VMEM)] * 2, out_specs=pl.BlockSpec(memory_space=pltpu.VMEM),
        compiler_params=_params(),
    )(fl, bias)


def forget_bwd(fl, bias, df, name):
    s = fl.shape[0]
    nb = s // BLK

    def body(fl_ref, b_ref, df_ref, o_ref, db_ref):
        row, lane = _iotas()
        upper = jnp.where(lane >= row, 1.0, 0.0).astype(BF16)

        def step(nn, carry):
            tail, db = carry
            r0 = pl.multiple_of((nb - 1 - nn) * BLK, BLK)
            dls = _dot_exact_rhs(upper, df_ref[pl.ds(r0, BLK), :]) + tail
            xv = fl_ref[pl.ds(r0, BLK), :] + b_ref[...]
            dfl = dls * (1.0 / (1.0 + jnp.exp(xv)))
            o_ref[pl.ds(r0, BLK), :] = dfl
            return dls[0:1, :], db + jnp.sum(dfl, axis=0, keepdims=True)

        _, db = lax.fori_loop(0, nb, step, (jnp.zeros((1, LANES), F32), jnp.zeros((1, LANES), F32)))
        db_ref[...] = db

    return pl.pallas_call(
        body, name=name, out_shape=(jax.ShapeDtypeStruct((s, LANES), F32), jax.ShapeDtypeStruct((1, LANES), F32)),
        in_specs=[pl.BlockSpec(memory_space=pltpu.VMEM)] * 3,
        out_specs=(pl.BlockSpec(memory_space=pltpu.VMEM), pl.BlockSpec(memory_space=pltpu.VMEM)),
        compiler_params=_params(),
    )(fl, bias, df)


def _rot_tables(s):
    inv = ROPE_THETA ** (-jnp.arange(ROT_HALF, dtype=F32) * 2.0 / (2 * ROT_HALF))
    ang = jnp.arange(s, dtype=F32)[:, None] * inv[None, :]
    cos, sin = jnp.cos(ang), jnp.sin(ang)
    z8 = jnp.zeros((s, ROT_HALF), F32)
    rest = HEAD_DIM - 2 * ROT_HALF
    zr, onr = jnp.zeros((s, rest), F32), jnp.ones((s, rest), F32)
    tile = lambda t: jnp.tile(t, (1, 2))
    return tile(jnp.concatenate([cos, cos, onr], 1)), tile(jnp.concatenate([-sin, z8, zr], 1)), tile(jnp.concatenate([z8, sin, zr], 1))


def rotary_prep(qkv, tables, name):
    s = qkv.shape[0]
    w = 4 * LANES

    def body(q_ref, k_ref, v_ref, c_ref, s1_ref, s2_ref, qo_ref, ko_ref, vo_ref):
        c, s1, s2 = c_ref[...], s1_ref[...], s2_ref[...]

        def rot(xv):
            return xv * c + pltpu.roll(xv, LANES - ROT_HALF, 1) * s1 + pltpu.roll(xv, ROT_HALF, 1) * s2

        qo_ref[...] = rot(q_ref[...].astype(F32)) * 0.125
        ko_ref[...] = rot(k_ref[...].astype(F32))
        vo_ref[...] = v_ref[...].astype(F32)

    cb = lambda off: pl.BlockSpec((ROWS, LANES), lambda i, j: (i, off + j))
    tb = pl.BlockSpec((ROWS, LANES), lambda i, j: (i, 0))
    out = jax.ShapeDtypeStruct((s, w), F32)
    return pl.pallas_call(
        body, name=name, out_shape=(out, out, out), grid=(s // ROWS, 4),
        in_specs=[cb(Q_OFF + 4), cb(K_OFF + 4), cb(V_OFF + 4), tb, tb, tb], out_specs=(cb(0), cb(0), cb(0)),
        compiler_params=_params(("parallel", "parallel")),
    )(qkv, qkv, qkv, *tables)


def rotary_bwd(dq, dk, dv, tables, name):
    s, w = dq.shape

    def body(dq_ref, dk_ref, dv_ref, c_ref, s1_ref, s2_ref, qo_ref, ko_ref, vo_ref):
        c, s1, s2 = c_ref[...], s1_ref[...], s2_ref[...]

        def rot_t(dy):
            return dy * c + pltpu.roll(dy * s1, ROT_HALF, 1) + pltpu.roll(dy * s2, LANES - ROT_HALF, 1)

        qo_ref[...] = (rot_t(dq_ref[...]) * 0.125).astype(BF16)
        ko_ref[...] = rot_t(dk_ref[...]).astype(BF16)
        vo_ref[...] = dv_ref[...].astype(BF16)

    cb = pl.BlockSpec((ROWS, LANES), lambda i, j: (i, j))
    tb = pl.BlockSpec((ROWS, LANES), lambda i, j: (i, 0))
    out = jax.ShapeDtypeStruct((s, w), BF16)
    return pl.pallas_call(
        body, name=name, out_shape=(out, out, out), grid=(s // ROWS, w // LANES),
        in_specs=[cb, cb, cb, tb, tb, tb], out_specs=(cb, cb, cb), compiler_params=_params(("parallel", "parallel")),
    )(dq, dk, dv, *tables)


def _deinterleave(dst, src_ref, stride, s, dtype):
    length = s // stride
    for r in range(stride):
        if stride == 1:
            dst[...] = src_ref[...].astype(dtype)
        else:
            dst[r * length:(r + 1) * length, :] = src_ref[pl.ds(r, length, stride=stride), :].astype(dtype)


def _band_masks(row, lane, first):
    return lane <= row, lane >= row + jnp.where(first, BLK, 0)


def dilated_fwd(qd, kd, vd, name, rider=None):
    s, w = qd.shape
    npairs = w // LANES
    nblk = s // BLK

    def body(q_ref, k_ref, v_ref, o_ref, lse_ref, qs, ks, vs, od, ld, on, ln):
        row, lane = _iotas()
        for pi, stride in enumerate(DIL_STRIDES):
            per = (s // stride) // BLK
            _deinterleave(qs, q_ref, stride, s, BF16)
            _deinterleave(ks, k_ref, stride, s, BF16)
            _deinterleave(vs, v_ref, stride, s, BF16)

            def block(b, carry):
                r0 = pl.multiple_of(b * BLK, BLK)
                rp = pl.multiple_of(jnp.maximum(b - 1, 0) * BLK, BLK)
                mc, mp = _band_masks(row, lane, b % per == 0)
                q = qs[pl.ds(r0, BLK), :]
                kc, kp, vc, vp = ks[pl.ds(r0, BLK), :], ks[pl.ds(rp, BLK), :], vs[pl.ds(r0, BLK), :], vs[pl.ds(rp, BLK), :]
                out = jnp.zeros((BLK, LANES), F32)
                lse = jnp.zeros((BLK, LANES), F32)
                for a in range(2):
                    hm = (lane < HEAD_DIM) if a == 0 else (lane >= HEAD_DIM)
                    qa = jnp.where(hm, q.astype(F32), 0.0).astype(BF16)
                    sc = jnp.where(mc, _dot_nt(qa, kc), NEG)
                    sp = jnp.where(mp, _dot_nt(qa, kp), NEG)
                    mx = jnp.maximum(jnp.max(sc, axis=1, keepdims=True), jnp.max(sp, axis=1, keepdims=True))
                    pc, pp = jnp.exp(sc - mx), jnp.exp(sp - mx)
                    l = jnp.sum(pc, axis=1, keepdims=True) + jnp.sum(pp, axis=1, keepdims=True)
                    oa = (_dot(pc.astype(BF16), vc) + _dot(pp.astype(BF16), vp)) / l
                    out = jnp.where(hm, oa, out)
                    lse = jnp.where(hm, mx + jnp.log(l), lse)
                od[pl.ds(r0, BLK), :] = out
                ld[pl.ds(r0, BLK), :] = lse
                return carry

            lax.fori_loop(0, nblk, block, 0)
            length = s // stride
            for r in range(stride):
                if stride == 1:
                    on[pi] = od[...]
                    ln[pi] = ld[...]
                else:
                    on[pi, pl.ds(r, length, stride=stride), :] = od[r * length:(r + 1) * length, :]
                    ln[pi, pl.ds(r, length, stride=stride), :] = ld[r * length:(r + 1) * length, :]

        def merge(n, carry):
            r0 = pl.multiple_of(n * BLK, BLK)
            ls = [ln[pi, pl.ds(r0, BLK), :] for pi in range(3)]
            mx = jnp.maximum(jnp.maximum(ls[0], ls[1]), ls[2])
            ws = [jnp.exp(lv - mx) for lv in ls]
            den = ws[0] + ws[1] + ws[2]
            num = ws[0] * on[0, pl.ds(r0, BLK), :] + ws[1] * on[1, pl.ds(r0, BLK), :] + ws[2] * on[2, pl.ds(r0, BLK), :]
            o_ref[pl.ds(r0, BLK), :] = num / den
            lse_ref[pl.ds(r0, BLK), :] = mx + jnp.log(den)
            return carry

        lax.fori_loop(0, nblk, merge, 0)

    colspec = pl.BlockSpec((s, LANES), lambda p: (0, p))
    out = jax.ShapeDtypeStruct((s, w), F32)
    scratch = [pltpu.VMEM((s, LANES), BF16)] * 3 + [pltpu.VMEM((s, LANES), F32)] * 2 + [pltpu.VMEM((3, s, LANES), F32)] * 2
    (o, lse), rode = call_with_rider(body, name, rider, [qd, kd, vd], [colspec] * 3, [out, out], [colspec, colspec], scratch, (npairs,))
    return o, lse, rode


def dilated_bwd(qd, kd, vd, do, out, lse, do_off, name):
    s, w = qd.shape
    npairs = w // LANES
    nblk = s // BLK

    def body(q_ref, k_ref, v_ref, do_ref, out_ref, lse_ref, dq_ref, dk_ref, dv_ref, qs, ks, vs, dos, dls, lss, dqd, dkd, dvd, dln):
        row, lane = _iotas()
        same_head = jnp.where((row < HEAD_DIM) == (lane < HEAD_DIM), 1.0, 0.0).astype(BF16)

        def delta_blk(n, carry):
            r0 = pl.multiple_of(n * BLK, BLK)
            dln[pl.ds(r0, BLK), :] = _dot_exact_lhs(do_ref[pl.ds(r0, BLK), :] * out_ref[pl.ds(r0, BLK), :], same_head)
            return carry

        lax.fori_loop(0, nblk, delta_blk, 0)
        for pi, stride in enumerate(DIL_STRIDES):
            per = (s // stride) // BLK
            _deinterleave(qs, q_ref, stride, s, BF16)
            _deinterleave(ks, k_ref, stride, s, BF16)
            _deinterleave(vs, v_ref, stride, s, BF16)
            _deinterleave(dos, do_ref, stride, s, BF16)
            _deinterleave(dls, dln, stride, s, F32)
            _deinterleave(lss, lse_ref, stride, s, F32)

            def block(b, carry):
                r0 = pl.multiple_of(b * BLK, BLK)
                rp = pl.multiple_of(jnp.maximum(b - 1, 0) * BLK, BLK)
                first = b % per == 0
                mc, mp = _band_masks(row, lane, first)
                q, dov = qs[pl.ds(r0, BLK), :], dos[pl.ds(r0, BLK), :]
                kc, kp, vc, vp = ks[pl.ds(r0, BLK), :], ks[pl.ds(rp, BLK), :], vs[pl.ds(r0, BLK), :], vs[pl.ds(rp, BLK), :]
                lse_t, dl_t = lss[pl.ds(r0, BLK), :], dls[pl.ds(r0, BLK), :]
                dq = jnp.zeros((BLK, LANES), F32)
                dkc = jnp.zeros((BLK, LANES), F32)
                dkp = jnp.zeros((BLK, LANES), F32)
                dvc = jnp.zeros((BLK, LANES), F32)
                dvp = jnp.zeros((BLK, LANES), F32)
                for a in range(2):
                    hm = (lane < HEAD_DIM) if a == 0 else (lane >= HEAD_DIM)
                    pick = lane == a * HEAD_DIM
                    qa = jnp.where(hm, q.astype(F32), 0.0).astype(BF16)
                    doa = jnp.where(hm, dov.astype(F32), 0.0).astype(BF16)
                    lse_a = jnp.sum(jnp.where(pick, lse_t, 0.0), axis=1, keepdims=True)
                    dl_a = jnp.sum(jnp.where(pick, dl_t, 0.0), axis=1, keepdims=True)
                    pc = jnp.where(mc, jnp.exp(_dot_nt(qa, kc) - lse_a), 0.0)
                    pp = jnp.where(mp, jnp.exp(_dot_nt(qa, kp) - lse_a), 0.0)
                    dsc = (pc * (_dot_nt(doa, vc) - dl_a)).astype(BF16)
                    dsp = (pp * (_dot_nt(doa, vp) - dl_a)).astype(BF16)
                    dq = jnp.where(hm, _dot(dsc, kc) + _dot(dsp, kp), dq)
                    dkc += _dot_tn(dsc, qa)
                    dkp += _dot_tn(dsp, qa)
                    dvc += _dot_tn(pc.astype(BF16), doa)
                    dvp += _dot_tn(pp.astype(BF16), doa)
                dqd[pl.ds(r0, BLK), :] = dq
                dkd[pl.ds(r0, BLK), :] = dkc
                dvd[pl.ds(r0, BLK), :] = dvc

                @pl.when(jnp.logical_not(first))
                def _():
                    dkd[pl.ds(rp, BLK), :] += dkp
                    dvd[pl.ds(rp, BLK), :] += dvp

                return carry

            lax.fori_loop(0, nblk, block, 0)
            length = s // stride
            for dst, src in ((dq_ref, dqd), (dk_ref, dkd), (dv_ref, dvd)):
                for r in range(stride):
                    if stride == 1:
                        dst[...] = src[...]
                    else:
                        dst[pl.ds(r, length, stride=stride), :] += src[r * length:(r + 1) * length, :]

    colspec = pl.BlockSpec((s, LANES), lambda p: (0, p))
    do_spec = pl.BlockSpec((s, LANES), lambda p: (0, do_off + p))
    o3 = jax.ShapeDtypeStruct((s, w), F32)
    return pl.pallas_call(
        body, name=name, out_shape=(o3, o3, o3), grid=(npairs,),
        in_specs=[colspec, colspec, colspec, do_spec, colspec, colspec], out_specs=(colspec, colspec, colspec),
        scratch_shapes=[pltpu.VMEM((s, LANES), BF16)] * 4 + [pltpu.VMEM((s, LANES), F32)] * 6,
        compiler_params=_params(("parallel",)),
    )(qd, kd, vd, do, out, lse)


def adamw(w, g, m, v, name):
    rows, cols = w.shape
    rb = min(rows, ROWS)
    c1 = 1.0 - ADAM_B1 ** ADAM_STEP
    c2 = 1.0 - ADAM_B2 ** ADAM_STEP

    def body(w_ref, g_ref, m_ref, v_ref, d_ref, mo_ref, vo_ref):
        gv = g_ref[...]
        mn = ADAM_B1 * m_ref[...] + (1.0 - ADAM_B1) * gv
        vn = ADAM_B2 * v_ref[...] + (1.0 - ADAM_B2) * (gv * gv)
        d_ref[...] = -ADAM_LR * ((mn / c1) / (jnp.sqrt(vn / c2) + ADAM_EPS) + ADAM_WD * w_ref[...])
        mo_ref[...] = mn
        vo_ref[...] = vn

    spec = _row_spec(cols, rb)
    out = jax.ShapeDtypeStruct((rows, cols), F32)
    return pl.pallas_call(
        body, name=name, out_shape=(out, out, out), grid=(rows // rb,), in_specs=[spec] * 4, out_specs=(spec,) * 3,
        compiler_params=_params(("parallel",)),
    )(w, g, m, v)


def _prefetch_call(body, name, scalar, ops, grid, in_specs, out_specs, out_shape, sem):
    spec = pltpu.PrefetchScalarGridSpec(num_scalar_prefetch=1, grid=grid, in_specs=in_specs, out_specs=out_specs)
    return pl.pallas_call(body, name=name, grid_spec=spec, out_shape=out_shape, compiler_params=_params(sem))(scalar, *ops)


def pair_sum(g, got, core, name):
    nc, r, c = g.shape
    rh = r // 2

    def body(core_ref, g_ref, got_ref, o_ref):
        o_ref[...] = (g_ref[...].astype(F32) + got_ref[...].astype(F32)).astype(BF16)

    blk = lambda rows_of: pl.BlockSpec((None, rh, c), rows_of)
    return _prefetch_call(
        body, name, core, (g, got), (nc,),
        [blk(lambda j, core_ref: (j, core_ref[0], 0)), blk(lambda j, core_ref: (j, 0, 0))],
        blk(lambda j, core_ref: (j, 0, 0)), jax.ShapeDtypeStruct((nc, rh, c), BF16), ("parallel",))


def chip_sum(pair, got, chip, layer, into, name):
    _, rh, c = pair.shape

    def body(chip_ref, p_ref, a_ref, b_ref, c_ref, old_ref, o_ref):
        o_ref[...] = ((p_ref[...].astype(F32) + a_ref[...].astype(F32)) + b_ref[...].astype(F32)) + c_ref[...].astype(F32)

    arrival = lambda k: pl.BlockSpec((None, rh, c), lambda i, chip_ref: (k, 0, 0))
    spec = pltpu.PrefetchScalarGridSpec(
        num_scalar_prefetch=1, grid=(1,),
        in_specs=[pl.BlockSpec((None, rh, c), lambda i, chip_ref: (chip_ref[0], 0, 0)), arrival(0), arrival(1), arrival(2), _ANY],
        out_specs=pl.BlockSpec((None, rh, c), lambda i, chip_ref: (layer, 0, 0)))
    return pl.pallas_call(body, name=name, grid_spec=spec, out_shape=jax.ShapeDtypeStruct(into.shape, into.dtype),
                          input_output_aliases={5: 0}, compiler_params=_params(("arbitrary",)))(chip, pair, got, got, got, into)


def adamw_family(w, m, v, g_mine, g_other, core, name):
    nl, r, c = w.shape
    rh = r // 2
    nb = 4 if rh % 512 == 0 else (2 if rh % 16 == 0 and rh > 256 else 1)
    rb = rh // nb
    c1 = 1.0 - ADAM_B1 ** ADAM_STEP
    c2 = 1.0 - ADAM_B2 ** ADAM_STEP

    def body(core_ref, w_ref, m_ref, v_ref, gm_ref, go_ref, g_ref, d_ref, mo_ref, vo_ref):
        gv = jnp.where(pl.program_id(1) == core_ref[0], gm_ref[...], go_ref[...])
        mn = ADAM_B1 * m_ref[...] + (1.0 - ADAM_B1) * gv
        vn = ADAM_B2 * v_ref[...] + (1.0 - ADAM_B2) * (gv * gv)
        g_ref[...] = gv
        d_ref[...] = -ADAM_LR * ((mn / c1) / (jnp.sqrt(vn / c2) + ADAM_EPS) + ADAM_WD * w_ref[...])
        mo_ref[...] = mn
        vo_ref[...] = vn

    full = pl.BlockSpec((None, rb, c), lambda l, h, i, core_ref: (l, h * nb + i, 0))
    half = pl.BlockSpec((None, rb, c), lambda l, h, i, core_ref: (l, i, 0))
    out = jax.ShapeDtypeStruct((nl, r, c), F32)
    return _prefetch_call(body, name, core, (w, m, v, g_mine, g_other), (nl, 2, nb), [full, full, full, half, half],
                          (full, full, full, full), (out, out, out, out), ("parallel", "parallel", "parallel"))


def _coords():
    return lax.axis_index("x"), lax.axis_index("y"), lax.axis_index("c")


def _other_chips(x, y):
    return ((1 - x, y), (x, 1 - y), (1 - x, 1 - y))


_ANY = pl.BlockSpec(memory_space=pl.ANY)


def _exchange_call(body, name, arrays, out_shapes, n_copies, n_local=0):
    n = len(arrays)

    def wrapped(*refs):
        body(refs[:n], refs[n:n + len(out_shapes)], *refs[n + len(out_shapes):])

    scratch = [pltpu.SemaphoreType.DMA((n_copies,)), pltpu.SemaphoreType.DMA((n_copies,))]
    if n_local:
        scratch.append(pltpu.SemaphoreType.DMA((n_local,)))
    return pl.pallas_call(
        wrapped, name=name, out_shape=tuple(out_shapes), in_specs=[_ANY] * n, out_specs=tuple([_ANY] * len(out_shapes)),
        scratch_shapes=scratch, compiler_params=_params(),
    )(*arrays)


def _remote(send_sems, recv_sems, n, src, dst, to):
    return pltpu.make_async_remote_copy(src_ref=src, dst_ref=dst, send_sem=send_sems.at[n], recv_sem=recv_sems.at[n],
                                        device_id=to, device_id_type=MESH)


class Rider:
    def __init__(self, arrays, out_shapes, n_remote, n_local, copies, then=None):
        self.arrays, self.out_shapes, self.n_remote, self.n_local = list(arrays), list(out_shapes), n_remote, n_local
        self.copies, self.then = copies, then

    def sems(self):
        return [pltpu.SemaphoreType.DMA((self.n_remote,)), pltpu.SemaphoreType.DMA((self.n_remote,)),
                pltpu.SemaphoreType.DMA((max(self.n_local, 1),))]

    def run(self, name):
        n, no = len(self.arrays), len(self.out_shapes)

        def body(*refs):
            for stage in (self.copies, self.then):
                if stage is not None:
                    cps = stage(refs[:n], refs[n:n + no], *refs[n + no:])
                    for cp in cps:
                        cp.start()
                    for cp in cps:
                        cp.wait()

        return pl.pallas_call(
            body, name=name, out_shape=tuple(self.out_shapes), in_specs=[_ANY] * n, out_specs=tuple([_ANY] * no),
            scratch_shapes=self.sems(), compiler_params=_params(),
        )(*self.arrays)


def ride(rider, body, n_in, n_out, grid):
    if rider is None:
        return body
    ni, no = len(rider.arrays), len(rider.out_shapes)

    def wrapped(*refs):
        ins, r_in = refs[:n_in], refs[n_in:n_in + ni]
        outs = refs[n_in + ni:n_in + ni + n_out]
        r_out = refs[n_in + ni + n_out:n_in + ni + n_out + no]
        rest = refs[n_in + ni + n_out + no:]
        scratch, sems = rest[:len(rest) - 3], rest[len(rest) - 3:]
        step, total = 0, 1
        for a, g in enumerate(grid):
            step, total = step * g + pl.program_id(a), total * g
        assert total >= 3
        relay_at = (4 * total) // 5 if rider.then is not None else total - 1

        @pl.when(step == 0)
        def _():
            for cp in rider.copies(r_in, r_out, *sems):
                cp.start()

        body(*ins, *outs, *scratch)

        @pl.when(step == relay_at)
        def _():
            for cp in rider.copies(r_in, r_out, *sems):
                cp.wait()
            if rider.then is not None:
                for cp in rider.then(r_in, r_out, *sems):
                    cp.start()

        if rider.then is not None:
            @pl.when(step == total - 1)
            def _():
                for cp in rider.then(r_in, r_out, *sems):
                    cp.wait()

    return wrapped


def call_with_rider(body, name, rider, ops, in_specs, out_shape, out_specs, scratch, grid):
    n_in, n_out = len(ops), len(out_shape)
    ops, in_specs, out_shape, out_specs, scratch = list(ops), list(in_specs), list(out_shape), list(out_specs), list(scratch)
    if rider is not None:
        ops += rider.arrays
        in_specs += [_ANY] * len(rider.arrays)
        out_shape += rider.out_shapes
        out_specs += [_ANY] * len(rider.out_shapes)
        scratch += rider.sems()
    res = pl.pallas_call(
        ride(rider, body, n_in, n_out, grid), name=name, out_shape=tuple(out_shape), grid=grid, in_specs=in_specs,
        out_specs=tuple(out_specs), scratch_shapes=scratch, compiler_params=_params(("arbitrary",) * len(grid)),
    )(*ops)
    return tuple(res[:n_out]), list(res[n_out:])


def gather_rider(shards):
    nf = len(shards)
    half = lambda ref, which: pl.ds(which * (ref.shape[-2] // 2), ref.shape[-2] // 2)

    def copies(s_refs, o_refs, send_sems, recv_sems, local_sems):
        x, y, c = _coords()
        me = 2 * x + y
        cps = [pltpu.make_async_copy(s_refs[f], o_refs[f].at[me], local_sems.at[f]) for f in range(nf)]
        for k, (px, py) in enumerate(_other_chips(x, y)):
            for f in range(nf):
                rows = half(s_refs[f], c)
                cps.append(_remote(send_sems, recv_sems, k * nf + f, s_refs[f].at[rows], o_refs[f].at[me, rows], (px, py, c)))
        return cps

    def relay(s_refs, o_refs, send_sems, recv_sems, local_sems):
        x, y, c = _coords()
        cps = []
        for k, (px, py) in enumerate(_other_chips(x, y)):
            for f in range(nf):
                landed = o_refs[f].at[2 * px + py, half(s_refs[f], c)]
                cps.append(_remote(send_sems, recv_sems, (3 + k) * nf + f, landed, landed, (x, y, 1 - c)))
        return cps

    return Rider(shards, [jax.ShapeDtypeStruct((N_CHIPS,) + sh.shape, sh.dtype) for sh in shards], 6 * nf, nf, copies, relay)


def scatter_rider(pairs):
    nf = len(pairs)

    def copies(p_refs, o_refs, send_sems, recv_sems, local_sems):
        x, y, c = _coords()
        cps = []
        for k, (px, py) in enumerate(_other_chips(x, y)):
            for f in range(nf):
                cps.append(_remote(send_sems, recv_sems, k * nf + f, p_refs[f].at[2 * px + py], o_refs[f].at[k], (px, py, c)))
        return cps

    return Rider(pairs, [jax.ShapeDtypeStruct((3,) + p.shape[1:], p.dtype) for p in pairs], 3 * nf, 0, copies)


def pair_swap(grads, name):
    def body(g_refs, o_refs, send_sems, recv_sems):
        x, y, c = _coords()
        cps = []
        for f, g_ref in enumerate(g_refs):
            rh = g_ref.shape[1] // 2
            cps.append(_remote(send_sems, recv_sems, f, g_ref.at[:, pl.ds((1 - c) * rh, rh), :], o_refs[f], (x, y, 1 - c)))
        for cp in cps:
            cp.start()
        for cp in cps:
            cp.wait()

    outs = [jax.ShapeDtypeStruct((g.shape[0], g.shape[1] // 2, g.shape[2]), g.dtype) for g in grads]
    return _exchange_call(body, name, grads, outs, len(grads))


def half_swap(halves, name):
    def body(h_refs, o_refs, send_sems, recv_sems):
        x, y, c = _coords()
        cps = [_remote(send_sems, recv_sems, f, h_ref, o_refs[f], (x, y, 1 - c)) for f, h_ref in enumerate(h_refs)]
        for cp in cps:
            cp.start()
        for cp in cps:
            cp.wait()

    return _exchange_call(body, name, halves, [jax.ShapeDtypeStruct(h.shape, h.dtype) for h in halves], len(halves))


def allsum_small(part, name):
    def body(p_ref, tot_ref, all_ref, send_sems, recv_sems):
        x, y, c = _coords()
        me, sibling = (x, y, c), (x, y, 1 - c)
        chips = _other_chips(x, y)

        def slot(px, py, pc):
            return all_ref.at[4 * px + 2 * py + pc]

        def copy(k, block, to, src=None):
            return pltpu.make_async_remote_copy(src_ref=slot(*block) if src is None else src, dst_ref=slot(*block),
                                                send_sem=send_sems.at[k], recv_sem=recv_sems.at[k], device_id=to, device_id_type=MESH)

        slot(*me)[...] = p_ref[...]
        first = [copy(0, me, sibling, src=p_ref)] + [copy(1 + j, me, (*chip, c), src=p_ref) for j, chip in enumerate(chips)]
        for cp in first:
            cp.start()
        passed = [copy(4 + j, (*chip, c), sibling) for j, chip in enumerate(chips)]
        for j, chip in enumerate(chips):
            copy(1 + j, (*chip, c), me).wait_recv()
            passed[j].start()
        copy(0, sibling, me).wait_recv()
        for j, chip in enumerate(chips):
            copy(4 + j, (*chip, 1 - c), me).wait_recv()
        for cp in first + passed:
            cp.wait_send()
        tot = all_ref[0]
        for d in range(1, 8):
            tot = tot + all_ref[d]
        tot_ref[...] = tot

    vm = pl.BlockSpec(memory_space=pltpu.VMEM)
    return pl.pallas_call(
        body, name=name, out_shape=jax.ShapeDtypeStruct(part.shape, F32), in_specs=[vm], out_specs=vm,
        scratch_shapes=[pltpu.VMEM((8,) + part.shape, F32), pltpu.SemaphoreType.DMA((7,)), pltpu.SemaphoreType.DMA((7,))],
        compiler_params=_params(),
    )(part)


QKVF_COLS = 772
QKVF_PAD = 896


def _tables_for(s):
    return _rot_tables(s)


def layer_families(layer):
    return (0, 1, layer // 2) if layer % 2 == 0 else (2, 3, layer // 2)


class GradientExchange:
    def __init__(self):
        self.core = lax.axis_index("c").astype(jnp.int32).reshape(1)
        self.chip = (2 * lax.axis_index("x") + lax.axis_index("y")).astype(jnp.int32).reshape(1)
        self.pairs, self.arrived, self.pending, self.riding = {}, {}, [], []

    def add(self, items, tag):
        got = pair_swap([g for _, _, g in items], f"grad_pair_swap_{tag}")
        for (fam, li, g), r in zip(items, got):
            self.pairs[(fam, li)] = pair_sum(g, r, self.core, f"grad_pair_sum_{fam}_{li}")
            self.pending.append((fam, li))

    def rider(self):
        self.riding, self.pending = self.pending, []
        return scatter_rider([self.pairs[k] for k in self.riding])

    def landed(self, outs):
        for k, o in zip(self.riding, outs):
            self.arrived[k] = o
        self.riding = []

    def finish(self, weights, moments1, moments2):
        if self.pending:
            self.landed(self.rider().run("grad_chip_scatter_last"))
        mine = []
        for fam, w in enumerate(weights):
            buf = jnp.zeros((w.shape[0], w.shape[1] // 2, w.shape[2]), F32)
            for li in range(w.shape[0]):
                buf = chip_sum(self.pairs[(fam, li)], self.arrived[(fam, li)], self.chip, li, buf, f"grad_chip_sum_{fam}_{li}")
            mine.append(buf)
        other = half_swap(mine, "grad_half_swap")
        return [adamw_family(w, m, v, gm, go, self.core, f"adamw_{f}")
                for f, (w, m, v, gm, go) in enumerate(zip(weights, moments1, moments2, mine, other))]


class KeepGradients:
    def __init__(self):
        self.grads = {}

    def add(self, items, tag):
        for fam, li, g in items:
            self.grads[(fam, li)] = g

    def rider(self):
        return None

    def landed(self, outs):
        pass


def kernel(x, norm_mix, w_qkv_even, w_o_even, w_qkvf_odd, b_forget, w_o_odd, norm_ffn, w_ffn_in, w_ffn_out, norm_final, loss_target, m_norm_mix, m_w_qkv_even, m_w_o_even, m_w_qkvf_odd, m_b_forget, m_w_o_odd, m_norm_ffn, m_w_ffn_in, m_w_ffn_out, m_norm_final, v_norm_mix, v_w_qkv_even, v_w_o_even, v_w_qkvf_odd, v_b_forget, v_w_o_odd, v_norm_ffn, v_w_ffn_in, v_w_ffn_out, v_norm_final):
    w_shards = [w_qkv_even, w_o_even, w_qkvf_odd, w_o_odd, w_ffn_in, w_ffn_out]
    shards = [w.astype(BF16) for w in w_shards]
    tables = _tables_for(x.shape[1])
    bias_pad = jnp.pad(b_forget, ((0, 0), (0, LANES - N_HEADS)))

    attn_w = {0: gather_rider([shards[0][0], shards[1][0]]).run("gather_first")}
    saved, cur = [], x[0]
    for layer in range(DEPTH):
        ahead = [l for l in (layer + 1, layer + 2) if l < DEPTH and l not in attn_w] if layer % 2 == 0 else []
        side_rider = None
        if ahead:
            fams = [layer_families(l) for l in ahead]
            side_rider = gather_rider([shards[f][li] for fam_qkv, fam_o, li in fams for f in (fam_qkv, fam_o)])
        w_qkv, w_o = attn_w[layer]
        cur, keep, _, side = forward_layer(layer, cur, w_qkv, w_o, norm_mix[layer:layer + 1], norm_ffn[layer:layer + 1], tables,
                                           bias_pad[layer // 2:layer // 2 + 1], rider=gather_rider([shards[4][layer], shards[5][layer]]),
                                           side_rider=side_rider)
        saved.append(keep)
        for n, l in enumerate(ahead):
            attn_w[l] = (side[2 * n], side[2 * n + 1])

    dcur, g_final, loss_part = loss_head(cur, norm_final.reshape(1, D_MODEL), loss_target[0], "loss_head")

    exchange = GradientExchange()
    g_mix, g_ffn, g_bias = [None] * DEPTH, [None] * DEPTH, [None] * (DEPTH // 2)
    for layer in reversed(range(DEPTH)):
        dcur, g_mix[layer], g_ffn[layer], g_b = backward_layer(layer, dcur, saved[layer], norm_mix[layer:layer + 1],
                                                               norm_ffn[layer:layer + 1], tables, bias_pad[layer // 2:layer // 2 + 1], exchange)
        if g_b is not None:
            g_bias[layer // 2] = g_b

    zero_row = jnp.zeros((1, D_MODEL), F32)
    pad16 = lambda v: jnp.pad(v, (0, D_MODEL - v.shape[0]))[None, :]
    small_rows = lambda mix, ffn, fin, bias, last: jnp.concatenate(
        [r.reshape(1, D_MODEL) for r in mix] + [r.reshape(1, D_MODEL) for r in ffn] + [fin.reshape(1, D_MODEL)]
        + [pad16(b) for b in bias] + [last] + [zero_row] * (SMALL_ROWS - 12), axis=0)
    loss_row = pad16(loss_part[0, :1])
    small_g = allsum_small(small_rows(g_mix, g_ffn, g_final, g_bias, loss_row), "allsum_small")
    loss = small_g[11, 0]
    small_g = small_g.at[11].set(0.0)
    sw = small_rows(list(norm_mix), list(norm_ffn), norm_final, list(b_forget), zero_row)
    sm = small_rows(list(m_norm_mix), list(m_norm_ffn), m_norm_final, list(m_b_forget), zero_row)
    sv = small_rows(list(v_norm_mix), list(v_norm_ffn), v_norm_final, list(v_b_forget), zero_row)
    sd, snm, snv = adamw(sw, small_g, sm, sv, "adamw_small")

    def small_out(a):
        return a[0:4], a[8, :], a[9:11, :N_HEADS], a[4:8]

    widen = lambda t: jnp.pad(t, ((0, 0), (0, 0), (0, QKVF_PAD - QKVF_COLS)))
    padded = lambda ws: [widen(t) if f == 2 else t for f, t in enumerate(ws)]
    big = exchange.finish(
        padded(w_shards), padded([m_w_qkv_even, m_w_o_even, m_w_qkvf_odd, m_w_o_odd, m_w_ffn_in, m_w_ffn_out]),
        padded([v_w_qkv_even, v_w_o_even, v_w_qkvf_odd, v_w_o_odd, v_w_ffn_in, v_w_ffn_out]))

    def outputs(small, which):
        mix, fin, bias, ffn = small_out(small)
        qkv_e, o_e, qkvf, o_o, fi, fo = [big[f][which][:, :, :QKVF_COLS] if f == 2 else big[f][which] for f in range(6)]
        return [mix, qkv_e, o_e, qkvf, bias, o_o, ffn, fi, fo, fin]

    return (loss, dcur[None], *outputs(small_g, 0), *outputs(sd, 1), *outputs(snm, 2), *outputs(snv, 3))


def _chip_tile(rows, cols, at):
    return pl.BlockSpec((None, rows, cols), at)


def forward_layer(layer, cur, w_qkv, w_o, mix_gain, ffn_gain, tables, bias_row, rider=None, side_rider=None, w_ffn=None):
    n = f"l{layer}"
    s = cur.shape[0]
    h1 = rmsnorm_fwd(cur, mix_gain, n + "_norm_mix")
    keep = {"x": cur, "h1": h1, "w_o": w_o.reshape(D_ATTN, D_MODEL)}
    side = []
    if layer % 2 == 0:
        qkv = matmul(h1, w_qkv, "nn", BF16, n + "_qkv", 1024, 768, 1024, mnk=(s, 3 * D_ATTN, D_MODEL),
                     b_spec=_chip_tile(D_MODEL, 768, lambda i, j, kk: (j, 0, 0)))
        o_sb, st, rode = causal_fwd(qkv, 4, "sb", n + "_sb_fwd", rider=rider)
        qd, kd, vd = rotary_prep(qkv, tables, n + "_rotary")
        o_dil, lse_dil, side = dilated_fwd(qd, kd, vd, n + "_dil_fwd", rider=side_rider)
        attn = jnp.concatenate([o_sb, o_dil], axis=1).astype(BF16)
        keep.update(qd=qd, kd=kd, vd=vd, o_dil=o_dil, lse_dil=lse_dil, w_qkv=w_qkv)
    else:
        natural = jnp.transpose(w_qkv, (1, 0, 2)).reshape(D_MODEL, N_CHIPS * QKVF_COLS)
        w_gate = jnp.pad(natural[:, 3 * D_ATTN:], ((0, 0), (0, LANES - N_HEADS)))
        qkv = matmul(h1, natural[:, :3 * D_ATTN], "nn", BF16, n + "_qkv", 1024, 768, 1024)
        fl = matmul(h1, w_gate, "nn", F32, n + "_fgate", 512, LANES, 1024)
        cum = forget_fwd(fl, bias_row, n + "_forget_fwd")
        f_heads = cum[:, :N_HEADS].T
        fq = jnp.broadcast_to(f_heads[:, :, None], (N_HEADS, s, LANES))
        fk = f_heads.reshape(N_HEADS // 2, 2, s)
        attn, st, rode = causal_fwd(qkv, 8, "fox", n + "_fox_fwd", fq=fq, fk=fk, rider=rider)
        attn = attn.astype(BF16)
        keep.update(fl=fl, fq=fq, fk=fk, w_qkv=jnp.concatenate([natural[:, :3 * D_ATTN], w_gate], axis=1))
    w_fi, w_fo = (rode[0], rode[1]) if rider is not None else w_ffn
    w_fo = w_fo.reshape(D_FF, D_MODEL)
    mid = matmul(attn, keep["w_o"], "nn", F32, n + "_attn_out", 1024, 1024, 1024, res=cur)
    h2 = rmsnorm_fwd(mid, ffn_gain, n + "_norm_ffn")
    gu = matmul(h2, w_fi, "nn", BF16, n + "_ffn_in", 1024, 1408, 1024, mnk=(s, 2 * D_FF, D_MODEL),
                b_spec=_chip_tile(D_MODEL, 1408, lambda i, j, kk: (j, 0, 0)))
    act = swiglu_fwd(gu, n + "_swiglu")
    out = matmul(act, w_fo, "nn", F32, n + "_ffn_out", 512, 1024, D_FF, res=mid)
    keep.update(qkv=qkv, st=st, attn=attn, mid=mid, h2=h2, gu=gu, act=act, w_fi=w_fi, w_fo=w_fo)
    return out, keep, rode, side


def backward_layer(layer, dcur, kp, mix_gain, ffn_gain, tables, bias_row, exchange):
    n = f"l{layer}"
    s = dcur.shape[0]
    fam_qkv, fam_o, li = layer_families(layer)
    g_fo = matmul(kp["act"], dcur, "tn", BF16, n + "_d_w_ffn_out", 1408, 1024, 512)
    dact = matmul(dcur, kp["w_fo"], "nt", BF16, n + "_d_act", 1024, 1408, 1024)
    dgu = swiglu_bwd(kp["gu"], dact, n + "_d_swiglu")
    g_fi = matmul(kp["h2"], dgu, "tn", BF16, n + "_d_w_ffn_in", 1024, 1408, 2048, mnk=(D_MODEL, 2 * D_FF, s),
                  o_spec=_chip_tile(D_MODEL, 1408, lambda i, j, kk: (j, 0, 0)), out_shape=(N_CHIPS, D_MODEL, 1408))
    dh2 = matmul(dgu, kp["w_fi"], "nt", F32, n + "_d_h2", 1024, 1024, 1408, mnk=(s, D_MODEL, 2 * D_FF),
                 b_spec=_chip_tile(D_MODEL, 1408, lambda i, j, kk: (kk, 0, 0)))
    dmid, g_ffn = rmsnorm_bwd(kp["mid"], ffn_gain, dh2, dcur, n + "_d_norm_ffn")
    g_o = matmul(kp["attn"], dmid, "tn", BF16, n + "_d_w_o", 1024, 1024, 512)
    dattn = matmul(dmid, kp["w_o"], "nt", F32, n + "_d_attn", 1024, 1024, 1024)
    exchange.add([(5, layer, g_fo.reshape(N_CHIPS, D_FF // N_CHIPS, D_MODEL)), (4, layer, g_fi),
                  (fam_o, li, g_o.reshape(N_CHIPS, D_ATTN // N_CHIPS, D_MODEL))], f"l{layer}_ffn")
    rider = exchange.rider()
    g_bias = None
    if layer % 2 == 0:
        dq_a, dk_a, dv_a, rode = causal_bwd(kp["qkv"], dattn, kp["st"], 4, "sb", n + "_sb_bwd", rider=rider)
        dqd, dkd, dvd = dilated_bwd(kp["qd"], kp["kd"], kp["vd"], dattn, kp["o_dil"], kp["lse_dil"], 4, n + "_dil_bwd")
        dq_b, dk_b, dv_b = rotary_bwd(dqd, dkd, dvd, tables, n + "_d_rotary")
        dproj = jnp.concatenate([dq_a, dq_b, dk_a.astype(BF16), dk_b, dv_a.astype(BF16), dv_b], axis=1)
        g_qkv = matmul(kp["h1"], dproj, "tn", BF16, n + "_d_w_qkv", 1024, 768, 2048, mnk=(D_MODEL, 3 * D_ATTN, s),
                       o_spec=_chip_tile(D_MODEL, 768, lambda i, j, kk: (j, 0, 0)), out_shape=(N_CHIPS, D_MODEL, 768))
        dh1 = matmul(dproj, kp["w_qkv"], "nt", F32, n + "_d_h1", 1024, 1024, 768, mnk=(s, D_MODEL, 3 * D_ATTN),
                     b_spec=_chip_tile(D_MODEL, 768, lambda i, j, kk: (kk, 0, 0)))
    else:
        dq_f, dk_f, dv_f, dfk, rode = causal_bwd(kp["qkv"], dattn, kp["st"], 8, "fox", n + "_fox_bwd", fq=kp["fq"], fk=kp["fk"],
                                                 rider=rider)
        dcum = jnp.pad(dfk.reshape(N_HEADS, s).T, ((0, 0), (0, LANES - N_HEADS)))
        dfl, dbias = forget_bwd(kp["fl"], bias_row, dcum, n + "_forget_bwd")
        g_bias = dbias[0, :N_HEADS]
        dproj = jnp.concatenate([dq_f, dk_f.astype(BF16), dv_f.astype(BF16), dfl.astype(BF16)], axis=1)
        by_chip = dproj[:, :N_CHIPS * QKVF_COLS].reshape(s, N_CHIPS, QKVF_COLS)
        by_chip = jnp.pad(by_chip, ((0, 0), (0, 0), (0, QKVF_PAD - QKVF_COLS))).reshape(s, N_CHIPS * QKVF_PAD)
        g_qkv = matmul(kp["h1"], by_chip, "tn", BF16, n + "_d_w_qkv", 1024, QKVF_PAD, 2048, mnk=(D_MODEL, N_CHIPS * QKVF_PAD, s),
                       o_spec=_chip_tile(D_MODEL, QKVF_PAD, lambda i, j, kk: (j, 0, 0)), out_shape=(N_CHIPS, D_MODEL, QKVF_PAD))
        dh1 = matmul(dproj, kp["w_qkv"], "nt", F32, n + "_d_h1", 1024, 1024, 640)
    exchange.landed(rode)
    exchange.add([(fam_qkv, li, g_qkv)], f"l{layer}_qkv")
    dx, g_mix = rmsnorm_bwd(kp["x"], mix_gain, dh1, dmid, n + "_d_norm_mix")
    return dx, g_mix, g_ffn, g_bias


def local_step(xs, target, norm_mix, norm_ffn, norm_final, b_forget, layer_weights):
    tables = _tables_for(xs.shape[0])
    bias_pad = jnp.pad(b_forget, ((0, 0), (0, LANES - N_HEADS)))
    saved, cur = [], xs
    for layer in range(DEPTH):
        w_qkv, w_o, w_fi, w_fo = layer_weights[layer]
        cur, keep, _, _ = forward_layer(layer, cur, w_qkv, w_o, norm_mix[layer:layer + 1], norm_ffn[layer:layer + 1], tables,
                                        bias_pad[layer // 2:layer // 2 + 1], w_ffn=(w_fi, w_fo))
        saved.append(keep)
    dcur, g_final, loss_part = loss_head(cur, norm_final.reshape(1, D_MODEL), target, "loss_head")
    keeper = KeepGradients()
    g_mix, g_ffn, g_bias = [None] * DEPTH, [None] * DEPTH, [None] * (DEPTH // 2)
    for layer in reversed(range(DEPTH)):
        dcur, g_mix[layer], g_ffn[layer], g_b = backward_layer(layer, dcur, saved[layer], norm_mix[layer:layer + 1],
                                                               norm_ffn[layer:layer + 1], tables, bias_pad[layer // 2:layer // 2 + 1], keeper)
        if g_b is not None:
            g_bias[layer // 2] = g_b
    return dcur, keeper.grads, (g_mix, g_ffn, g_final, g_bias), loss_part
```

```python
import functools

import jax
import jax.numpy as jnp
from jax import lax
from jax.experimental import pallas as pl
from jax.experimental.pallas import tpu as pltpu

F32 = jnp.float32
BF16 = jnp.bfloat16
MESH = pl.DeviceIdType.MESH

D_MODEL = 1024
DEPTH = 4
HEAD_DIM = 64
N_HEADS = 16
D_ATTN = 1024
D_FF = 2816
ROPE_THETA = 500000.0
ROT_HALF = 8
RMS_EPS = 1e-5
DIL_STRIDES = (1, 4, 16)
ADAM_LR, ADAM_B1, ADAM_B2, ADAM_EPS, ADAM_WD, ADAM_STEP = 0.001, 0.9, 0.999, 1e-8, 0.01, 10

LANES = 128
BLK = 128
VMEM_LIMIT = 56 * 1024 * 1024
NEG = -1e30
N_CHIPS = 4
FLAT_COLS = 1024
FLAT_ROWS = 12800
HALF_ROWS = FLAT_ROWS // 2
SMALL_ROWS = 16


def _params(sem=None):
    return pltpu.CompilerParams(dimension_semantics=sem, vmem_limit_bytes=VMEM_LIMIT)


def _dot(a, b):
    return lax.dot_general(a, b, (((1,), (0,)), ((), ())), preferred_element_type=F32)


def _dot_nt(a, b):
    return lax.dot_general(a, b, (((1,), (1,)), ((), ())), preferred_element_type=F32)


def _dot_tn(a, b):
    return lax.dot_general(a, b, (((0,), (0,)), ((), ())), preferred_element_type=F32)


def _split3(x):
    x1 = x.astype(BF16)
    r1 = x - x1.astype(F32)
    x2 = r1.astype(BF16)
    x3 = (r1 - x2.astype(F32)).astype(BF16)
    return x1, x2, x3


def _dot_exact_lhs(x, t):
    x1, x2, x3 = _split3(x)
    return _dot(x1, t) + _dot(x2, t) + _dot(x3, t)


def _dot_exact_rhs(t, x):
    x1, x2, x3 = _split3(x)
    return _dot(t, x1) + _dot(t, x2) + _dot(t, x3)


def _iotas(shape=(BLK, LANES)):
    return lax.broadcasted_iota(jnp.int32, shape, 0), lax.broadcasted_iota(jnp.int32, shape, 1)


_DIMS = {"nn": (((1,), (0,)), ((), ())), "nt": (((1,), (1,)), ((), ())), "tn": (((0,), (0,)), ((), ()))}


def matmul(a, b, mode, out_dtype, name, tm, tn, tk, res=None, mnk=None, b_spec=None, o_spec=None, out_shape=None, into=None):
    if mnk is not None:
        m, n, k = mnk
    elif mode == "nn":
        (m, k), (k2, n) = a.shape, b.shape
    elif mode == "nt":
        (m, k), (n, k2) = a.shape, b.shape
    else:
        (k, m), (k2, n) = a.shape, b.shape
    assert m % tm == 0 and n % tn == 0 and k % tk == 0, (name, a.shape, b.shape)
    nk = k // tk
    a_spec = pl.BlockSpec((tk, tm), lambda i, j, kk: (kk, i)) if mode == "tn" else pl.BlockSpec((tm, tk), lambda i, j, kk: (i, kk))
    if b_spec is None:
        b_spec = pl.BlockSpec((tn, tk), lambda i, j, kk: (j, kk)) if mode == "nt" else pl.BlockSpec((tk, tn), lambda i, j, kk: (kk, j))
    r_spec = pl.BlockSpec((tm, tn), lambda i, j, kk: (i, j))
    if o_spec is None:
        o_spec = r_spec
    dims = _DIMS[mode]
    has_res = res is not None
    n_in = 2 + int(has_res) + int(into is not None)

    def body(*refs):
        a_ref, b_ref = refs[0], refs[1]
        r_ref = refs[2] if has_res else None
        o_ref = refs[n_in]

        def finish(v):
            if has_res:
                v = v + r_ref[...]
            o_ref[...] = v.astype(out_dtype)

        p = lax.dot_general(a_ref[...].astype(BF16), b_ref[...].astype(BF16), dims, preferred_element_type=F32)
        if nk == 1:
            finish(p)
        else:
            acc = refs[-1]
            kk = pl.program_id(2)

            @pl.when(kk == 0)
            def _():
                acc[...] = p

            @pl.when(kk > 0)
            def _():
                acc[...] += p

            @pl.when(kk == nk - 1)
            def _():
                finish(acc[...])

    ops = [a, b] + ([res] if has_res else []) + ([into] if into is not None else [])
    specs = [a_spec, b_spec] + ([r_spec] if has_res else []) + ([_ANY] if into is not None else [])
    if into is not None:
        out_shape = jax.ShapeDtypeStruct(into.shape, into.dtype)
    else:
        out_shape = jax.ShapeDtypeStruct((m, n) if out_shape is None else out_shape, out_dtype)
    return pl.pallas_call(
        body, name=name, out_shape=out_shape,
        grid=(m // tm, n // tn, nk), in_specs=specs, out_specs=o_spec,
        scratch_shapes=[pltpu.VMEM((tm, tn), F32)] if nk > 1 else [],
        input_output_aliases={n_in - 1: 0} if into is not None else {},
        compiler_params=_params(("parallel", "parallel", "arbitrary")),
    )(*ops)


ROWS = 256


def _row_spec(cols, rows=ROWS):
    return pl.BlockSpec((rows, cols), lambda i: (i, 0))


def _fix_spec(r, cols):
    return pl.BlockSpec((r, cols), lambda i: (0, 0))


def rmsnorm_fwd(x, g, name):
    s, d = x.shape

    def body(x_ref, g_ref, h_ref):
        xv = x_ref[...]
        rstd = lax.rsqrt(jnp.mean(xv * xv, axis=-1, keepdims=True) + RMS_EPS)
        h_ref[...] = (xv * rstd * g_ref[...]).astype(BF16)

    return pl.pallas_call(
        body, name=name, out_shape=jax.ShapeDtypeStruct((s, d), BF16), grid=(s // ROWS,),
        in_specs=[_row_spec(d), _fix_spec(1, d)], out_specs=_row_spec(d), compiler_params=_params(("parallel",)),
    )(x, g)


def _rms_bwd_math(xv, gv, dh):
    rstd = lax.rsqrt(jnp.mean(xv * xv, axis=-1, keepdims=True) + RMS_EPS)
    xhat = xv * rstd
    u = dh * gv
    dx = rstd * (u - xhat * jnp.mean(u * xhat, axis=-1, keepdims=True))
    return dx, dh * xhat


def rmsnorm_bwd(x, g, dh, dres, name):
    s, d = x.shape

    def body(x_ref, g_ref, dh_ref, dres_ref, dx_ref, dg_ref):
        dx, dgt = _rms_bwd_math(x_ref[...], g_ref[...], dh_ref[...])
        dx_ref[...] = dres_ref[...] + dx
        part = jnp.sum(dgt, axis=0, keepdims=True)

        @pl.when(pl.program_id(0) == 0)
        def _():
            dg_ref[...] = part

        @pl.when(pl.program_id(0) > 0)
        def _():
            dg_ref[...] += part

    return pl.pallas_call(
        body, name=name, out_shape=(jax.ShapeDtypeStruct((s, d), F32), jax.ShapeDtypeStruct((1, d), F32)),
        grid=(s // ROWS,), in_specs=[_row_spec(d), _fix_spec(1, d), _row_spec(d), _row_spec(d)],
        out_specs=(_row_spec(d), _fix_spec(1, d)), compiler_params=_params(("arbitrary",)),
    )(x, g, dh, dres)


def loss_head(x, g, target, name):
    s, d = x.shape

    def body(x_ref, g_ref, t_ref, dx_ref, dg_ref, loss_ref):
        xv, gv = x_ref[...], g_ref[...]
        rstd = lax.rsqrt(jnp.mean(xv * xv, axis=-1, keepdims=True) + RMS_EPS)
        err = xv * rstd * gv - t_ref[...]
        dx, dgt = _rms_bwd_math(xv, gv, err * (1.0 / d))
        dx_ref[...] = dx
        part = jnp.sum(dgt, axis=0, keepdims=True)
        lpart = jnp.full((1, LANES), 0.5 / d, F32) * jnp.sum(err * err)

        @pl.when(pl.program_id(0) == 0)
        def _():
            dg_ref[...] = part
            loss_ref[...] = lpart

        @pl.when(pl.program_id(0) > 0)
        def _():
            dg_ref[...] += part
            loss_ref[...] += lpart

    return pl.pallas_call(
        body, name=name,
        out_shape=(jax.ShapeDtypeStruct((s, d), F32), jax.ShapeDtypeStruct((1, d), F32), jax.ShapeDtypeStruct((1, LANES), F32)),
        grid=(s // ROWS,), in_specs=[_row_spec(d), _fix_spec(1, d), _row_spec(d)],
        out_specs=(_row_spec(d), _fix_spec(1, d), _fix_spec(1, LANES)), compiler_params=_params(("arbitrary",)),
    )(x, g, target)


def swiglu_fwd(gu, name):
    s, f2 = gu.shape
    f = f2 // 2

    def body(gu_ref, a_ref):
        gv, uv = gu_ref[:, :f].astype(F32), gu_ref[:, f:].astype(F32)
        a_ref[...] = (gv * (1.0 / (1.0 + jnp.exp(-gv))) * uv).astype(BF16)

    return pl.pallas_call(
        body, name=name, out_shape=jax.ShapeDtypeStruct((s, f), BF16), grid=(s // ROWS,),
        in_specs=[_row_spec(f2)], out_specs=_row_spec(f), compiler_params=_params(("parallel",)),
    )(gu)


def swiglu_bwd(gu, dact, name):
    s, f2 = gu.shape
    f = f2 // 2

    def body(gu_ref, da_ref, o_ref):
        gv, uv, da = gu_ref[:, :f].astype(F32), gu_ref[:, f:].astype(F32), da_ref[...].astype(F32)
        sg = 1.0 / (1.0 + jnp.exp(-gv))
        o_ref[:, :f] = (da * uv * sg * (1.0 + gv * (1.0 - sg))).astype(BF16)
        o_ref[:, f:] = (da * gv * sg).astype(BF16)

    return pl.pallas_call(
        body, name=name, out_shape=jax.ShapeDtypeStruct((s, f2), BF16), grid=(s // ROWS,),
        in_specs=[_row_spec(f2), _row_spec(f)], out_specs=_row_spec(f2), compiler_params=_params(("parallel",)),
    )(gu, dact)


Q_OFF, K_OFF, V_OFF = 0, 8, 16


KB = 512
BQ = 256
SUB = KB // BLK


def _softplus_parts(z):
    sp = jnp.log(1.0 + jnp.exp(-jnp.abs(z)))
    ls = jnp.minimum(z, 0.0) - sp
    return ls, ls - z


def _wide(t):
    return jnp.concatenate([t] * SUB, axis=1)


def _chunk_dots(x, tri):
    terms = []
    for u in range(SUB):
        xu = x[:, u * BLK:(u + 1) * BLK]
        hi = xu.astype(BF16)
        terms += [hi, (xu - hi.astype(F32)).astype(BF16)]
    r = _dot(jnp.concatenate(terms, axis=0), tri)
    rows = x.shape[0]
    piece = lambda n: r[n * rows:(n + 1) * rows]
    return [piece(2 * u) + piece(2 * u + 1) for u in range(SUB)]


def _block_suffix_sums(x, suffix, c):
    loc = _chunk_dots(x, suffix)
    out = [None] * SUB
    for u in reversed(range(SUB)):
        out[u] = loc[u] + c
        c = c + jnp.sum(x[:, u * BLK:(u + 1) * BLK], axis=1, keepdims=True)
    return jnp.concatenate(out, axis=1), c


def _block_prefix_sums(x, tri, c):
    loc = _chunk_dots(x, tri)
    out = []
    for u in range(SUB):
        out.append(loc[u] + c)
        c = c + jnp.sum(x[:, u * BLK:(u + 1) * BLK], axis=1, keepdims=True)
    return jnp.concatenate(out, axis=1), c


def causal_fwd(qkv, npairs, mode, name, fq=None, fk=None, rider=None):
    s = qkv.shape[0]
    nq = s // BQ
    fox = mode == "fox"

    def body(*refs):
        if fox:
            q_ref, k_ref, v_ref, fq_ref, fk_ref, o_ref, st_ref = refs
        else:
            q_ref, k_ref, v_ref, o_ref, st_ref = refs
        i = pl.program_id(1)
        nkb = (i * BQ + BQ - 1) // KB + 1
        row, lane = _iotas((BQ, KB))
        row_s, lane_s = _iotas()
        _, lane_q = _iotas((BQ, LANES))
        nfull = (i * BQ) // KB
        qpos = i * BQ + row
        qf = q_ref[...].astype(F32) * 0.125
        hms = (lane_q < HEAD_DIM, lane_q >= HEAD_DIM)
        qas = [jnp.where(hm, qf, 0.0).astype(BF16) for hm in hms]
        suffix = jnp.where(row_s > lane_s, 1.0, 0.0).astype(BF16)
        zero = jnp.zeros((BQ, LANES), F32)
        col0 = jnp.zeros((BQ, 1), F32)

        def kv(j):
            r0 = pl.multiple_of(j * KB, KB)
            return r0, k_ref[pl.ds(r0, KB), :], v_ref[pl.ds(r0, KB), :]

        if fox:
            fqs = [_wide(fq_ref[a]) for a in range(2)]

            def step(j, carry, masked):
                r0, kb, vb = kv(j)
                new = []
                for a in range(2):
                    acc, mx, l = carry[3 * a:3 * a + 3]
                    z = _dot_nt(qas[a], kb) + fqs[a] - fk_ref[a:a + 1, pl.ds(r0, KB)]
                    if masked:
                        z = jnp.where(r0 + lane <= qpos, z, NEG)
                    mnew = jnp.maximum(mx, jnp.max(z, axis=1, keepdims=True))
                    p = jnp.exp(z - mnew)
                    alpha = jnp.exp(mx - mnew)
                    new += [alpha * acc + _dot(p.astype(BF16), vb), mnew, alpha * l + jnp.sum(p, axis=1, keepdims=True)]
                return tuple(new)

            neg = jnp.full((BQ, 1), NEG, F32)
            res = lax.fori_loop(0, nfull, functools.partial(step, masked=False), (zero, neg, col0, zero, neg, col0))
            res = lax.fori_loop(nfull, nkb, functools.partial(step, masked=True), res)
            outs = [res[3 * a] / res[3 * a + 2] for a in range(2)]
            stats = [res[3 * a + 1] + jnp.log(res[3 * a + 2]) for a in range(2)]
        else:
            def step(j, carry, masked):
                r0, kb, vb = kv(j)
                strict = r0 + lane < qpos
                new = []
                for a in range(2):
                    acc, c = carry[2 * a:2 * a + 2]
                    ls, lm = _softplus_parts(_dot_nt(qas[a], kb))
                    if masked:
                        lm = jnp.where(strict, lm, 0.0)
                    between, c = _block_suffix_sums(lm, suffix, c)
                    aw = jnp.exp(ls + between)
                    if masked:
                        aw = jnp.where(strict, aw, 0.0)
                    new += [acc + _dot(aw.astype(BF16), vb), c]
                return tuple(new)

            res = lax.fori_loop(0, nkb - nfull, lambda jj, c: step(nkb - 1 - jj, c, True), (zero, col0, zero, col0))
            res = lax.fori_loop(0, nfull, lambda jj, c: step(nfull - 1 - jj, c, False), res)
            outs, stats = [res[0], res[2]], [res[1], res[3]]
        o_ref[...] = jnp.where(hms[0], outs[0], outs[1])
        for a in range(2):
            st_ref[a] = jnp.broadcast_to(stats[a], (BQ, LANES))

    col = lambda off: (lambda p, i: (0, off + p))
    in_specs = [pl.BlockSpec((BQ, LANES), lambda p, i: (i, Q_OFF + p)),
                pl.BlockSpec((s, LANES), col(K_OFF)), pl.BlockSpec((s, LANES), col(V_OFF))]
    ops = [qkv, qkv, qkv]
    if fox:
        in_specs += [pl.BlockSpec((2, BQ, LANES), lambda p, i: (p, i, 0)), pl.BlockSpec((None, 2, s), lambda p, i: (p, 0, 0))]
        ops += [fq, fk]
    (o, stat), rode = call_with_rider(
        body, name, rider, ops, in_specs,
        [jax.ShapeDtypeStruct((s, npairs * LANES), F32), jax.ShapeDtypeStruct((2 * npairs, s, LANES), F32)],
        [pl.BlockSpec((BQ, LANES), lambda p, i: (i, p)), pl.BlockSpec((2, BQ, LANES), lambda p, i: (p, i, 0))], [], (npairs, nq))
    return o, stat, rode


def causal_bwd(qkv, do, stat, npairs, mode, name, fq=None, fk=None, rider=None):
    s = qkv.shape[0]
    nq = s // BQ
    fox = mode == "fox"

    def body(*refs):
        if fox:
            q_ref, k_ref, v_ref, do_ref, st_ref, fq_ref, fk_ref, dq_ref, dk_ref, dv_ref, df_ref, p_s, dp_s = refs
        else:
            q_ref, k_ref, v_ref, do_ref, st_ref, dq_ref, dk_ref, dv_ref = refs
        i = pl.program_id(1)

        @pl.when(i == 0)
        def _():
            dk_ref[...] = jnp.zeros_like(dk_ref)
            dv_ref[...] = jnp.zeros_like(dv_ref)
            if fox:
                df_ref[...] = jnp.zeros_like(df_ref)

        nkb = (i * BQ + BQ - 1) // KB + 1
        nfull = (i * BQ) // KB
        row, lane = _iotas((BQ, KB))
        row_s, lane_s = _iotas()
        _, lane_q = _iotas((BQ, LANES))
        qpos = i * BQ + row
        qf = q_ref[...].astype(F32) * 0.125
        dov = do_ref[...]
        hms = (lane_q < HEAD_DIM, lane_q >= HEAD_DIM)
        qas = [jnp.where(hm, qf, 0.0).astype(BF16) for hm in hms]
        doas = [jnp.where(hm, dov, 0.0).astype(BF16) for hm in hms]
        stas = [_wide(st_ref[a]) for a in range(2)]
        zero = jnp.zeros((BQ, LANES), F32)
        col0 = jnp.zeros((BQ, 1), F32)

        def kv(j):
            r0 = pl.multiple_of(j * KB, KB)
            return r0, k_ref[pl.ds(r0, KB), :], v_ref[pl.ds(r0, KB), :]

        if fox:
            fqs = [_wide(fq_ref[a]) for a in range(2)]

            def probs(j, deltas, masked):
                r0, kb, vb = kv(j)
                new = []
                for a in range(2):
                    z = _dot_nt(qas[a], kb) + fqs[a] - fk_ref[a:a + 1, pl.ds(r0, KB)]
                    p = jnp.exp(z - stas[a])
                    if masked:
                        p = jnp.where(r0 + lane <= qpos, p, 0.0)
                    dp = _dot_nt(doas[a], vb)
                    p_s[a, j] = p
                    dp_s[a, j] = dp
                    new.append(deltas[a] + jnp.sum(p * dp, axis=1, keepdims=True))
                return tuple(new)

            deltas = lax.fori_loop(0, nfull, functools.partial(probs, masked=False), (col0, col0))
            deltas = lax.fori_loop(nfull, nkb, functools.partial(probs, masked=True), deltas)

            def step(j, dqs):
                r0, kb, _ = kv(j)
                new = []
                dk = jnp.zeros((KB, LANES), F32)
                dv = jnp.zeros((KB, LANES), F32)
                for a in range(2):
                    p = p_s[a, j]
                    ds = p * (dp_s[a, j] - deltas[a])
                    dsb = ds.astype(BF16)
                    dk += _dot_tn(dsb, qas[a])
                    dv += _dot_tn(p.astype(BF16), doas[a])
                    df_ref[a:a + 1, pl.ds(r0, KB)] -= jnp.sum(ds, axis=0, keepdims=True)
                    new.append(dqs[a] + _dot(dsb, kb))
                dk_ref[pl.ds(r0, KB), :] += dk
                dv_ref[pl.ds(r0, KB), :] += dv
                return tuple(new)

            dqs = lax.fori_loop(0, nkb, step, (zero, zero))
        else:
            incl = jnp.where(row_s <= lane_s, 1.0, 0.0).astype(BF16)
            excl = jnp.where(row_s < lane_s, 1.0, 0.0).astype(BF16)

            def step(j, carry, masked):
                r0, kb, vb = kv(j)
                strict = r0 + lane < qpos
                new = []
                dk = jnp.zeros((KB, LANES), F32)
                dv = jnp.zeros((KB, LANES), F32)
                for a in range(2):
                    dq, cm, cg = carry[3 * a:3 * a + 3]
                    ls, lm = _softplus_parts(_dot_nt(qas[a], kb))
                    if masked:
                        lm = jnp.where(strict, lm, 0.0)
                    beta = jnp.exp(ls)
                    upto, cm = _block_prefix_sums(lm, incl, cm)
                    aw = jnp.exp(ls + stas[a] - upto)
                    if masked:
                        aw = jnp.where(strict, aw, 0.0)
                    g = aw * _dot_nt(doas[a], vb)
                    pre, cg = _block_prefix_sums(g, excl, cg)
                    dz = g * (1.0 - beta) - pre * beta
                    if masked:
                        dz = jnp.where(strict, dz, 0.0)
                    dzb = dz.astype(BF16)
                    dk += _dot_tn(dzb, qas[a])
                    dv += _dot_tn(aw.astype(BF16), doas[a])
                    new += [dq + _dot(dzb, kb), cm, cg]
                dk_ref[pl.ds(r0, KB), :] += dk
                dv_ref[pl.ds(r0, KB), :] += dv
                return tuple(new)

            res = lax.fori_loop(0, nfull, functools.partial(step, masked=False), (zero, col0, col0, zero, col0, col0))
            res = lax.fori_loop(nfull, nkb, functools.partial(step, masked=True), res)
            dqs = (res[0], res[3])
        dq_ref[...] = (jnp.where(hms[0], dqs[0], dqs[1]) * 0.125).astype(BF16)

    col = lambda off: (lambda p, i: (0, off + p))
    blk = pl.BlockSpec((BQ, LANES), lambda p, i: (i, p))
    acc = pl.BlockSpec((s, LANES), lambda p, i: (0, p))
    st_spec = pl.BlockSpec((2, BQ, LANES), lambda p, i: (p, i, 0))
    in_specs = [pl.BlockSpec((BQ, LANES), lambda p, i: (i, Q_OFF + p)), pl.BlockSpec((s, LANES), col(K_OFF)),
                pl.BlockSpec((s, LANES), col(V_OFF)), blk, st_spec]
    ops = [qkv, qkv, qkv, do, stat]
    w = npairs * LANES
    out_shape = [jax.ShapeDtypeStruct((s, w), BF16), jax.ShapeDtypeStruct((s, w), F32), jax.ShapeDtypeStruct((s, w), F32)]
    out_specs = [blk, acc, acc]
    scratch = []
    if fox:
        fk_spec = pl.BlockSpec((None, 2, s), lambda p, i: (p, 0, 0))
        in_specs += [st_spec, fk_spec]
        ops += [fq, fk]
        out_shape.append(jax.ShapeDtypeStruct((npairs, 2, s), F32))
        out_specs.append(fk_spec)
        scratch = [pltpu.VMEM((2, s // KB, BQ, KB), F32)] * 2
    outs, rode = call_with_rider(body, name, rider, ops, in_specs, out_shape, out_specs, scratch, (npairs, nq))
    return (*outs, rode)


def forget_fwd(fl, bias, name):
    s = fl.shape[0]

    def body(fl_ref, b_ref, f_ref):
        row, lane = _iotas()
        lower = jnp.where(lane <= row, 1.0, 0.0).astype(BF16)

        def step(n, carry):
            r0 = pl.multiple_of(n * BLK, BLK)
            ls, _ = _softplus_parts(fl_ref[pl.ds(r0, BLK), :] + b_ref[...])
            blk = _dot_exact_rhs(lower, ls) + carry
            f_ref[pl.ds(r0, BLK), :] = blk
            return blk[BLK - 1:BLK, :]

        lax.fori_loop(0, s // BLK, step, jnp.zeros((1, LANES), F32))

    return pl.pallas_call(
        body, name=name, out_shape=jax.ShapeDtypeStruct((s, LANES), F32),
        in_specs=[pl.BlockSpec(memory_space=pltpu.VMEM)] * 2, out_specs=pl.BlockSpec(memory_space=pltpu.VMEM),
        compiler_params=_params(),
    )(fl, bias)


def forget_bwd(fl, bias, df, name):
    s = fl.shape[0]
    nb = s // BLK

    def body(fl_ref, b_ref, df_ref, o_ref, db_ref):
        row, lane = _iotas()
        upper = jnp.where(lane >= row, 1.0, 0.0).astype(BF16)

        def step(nn, carry):
            tail, db = carry
            r0 = pl.multiple_of((nb - 1 - nn) * BLK, BLK)
            dls = _dot_exact_rhs(upper, df_ref[pl.ds(r0, BLK), :]) + tail
            xv = fl_ref[pl.ds(r0, BLK), :] + b_ref[...]
            dfl = dls * (1.0 / (1.0 + jnp.exp(xv)))
            o_ref[pl.ds(r0, BLK), :] = dfl
            return dls[0:1, :], db + jnp.sum(dfl, axis=0, keepdims=True)

        _, db = lax.fori_loop(0, nb, step, (jnp.zeros((1, LANES), F32), jnp.zeros((1, LANES), F32)))
        db_ref[...] = db

    return pl.pallas_call(
        body, name=name, out_shape=(jax.ShapeDtypeStruct((s, LANES), F32), jax.ShapeDtypeStruct((1, LANES), F32)),
        in_specs=[pl.BlockSpec(memory_space=pltpu.VMEM)] * 3,
        out_specs=(pl.BlockSpec(memory_space=pltpu.VMEM), pl.BlockSpec(memory_space=pltpu.VMEM)),
        compiler_params=_params(),
    )(fl, bias, df)


def _rot_tables(s):
    inv = ROPE_THETA ** (-jnp.arange(ROT_HALF, dtype=F32) * 2.0 / (2 * ROT_HALF))
    ang = jnp.arange(s, dtype=F32)[:, None] * inv[None, :]
    cos, sin = jnp.cos(ang), jnp.sin(ang)
    z8 = jnp.zeros((s, ROT_HALF), F32)
    rest = HEAD_DIM - 2 * ROT_HALF
    zr, onr = jnp.zeros((s, rest), F32), jnp.ones((s, rest), F32)
    tile = lambda t: jnp.tile(t, (1, 2))
    return tile(jnp.concatenate([cos, cos, onr], 1)), tile(jnp.concatenate([-sin, z8, zr], 1)), tile(jnp.concatenate([z8, sin, zr], 1))


def rotary_prep(qkv, tables, name):
    s = qkv.shape[0]
    w = 4 * LANES

    def body(q_ref, k_ref, v_ref, c_ref, s1_ref, s2_ref, qo_ref, ko_ref, vo_ref):
        c, s1, s2 = c_ref[...], s1_ref[...], s2_ref[...]

        def rot(xv):
            return xv * c + pltpu.roll(xv, LANES - ROT_HALF, 1) * s1 + pltpu.roll(xv, ROT_HALF, 1) * s2

        qo_ref[...] = rot(q_ref[...].astype(F32)) * 0.125
        ko_ref[...] = rot(k_ref[...].astype(F32))
        vo_ref[...] = v_ref[...].astype(F32)

    cb = lambda off: pl.BlockSpec((ROWS, LANES), lambda i, j: (i, off + j))
    tb = pl.BlockSpec((ROWS, LANES), lambda i, j: (i, 0))
    out = jax.ShapeDtypeStruct((s, w), F32)
    return pl.pallas_call(
        body, name=name, out_shape=(out, out, out), grid=(s // ROWS, 4),
        in_specs=[cb(Q_OFF + 4), cb(K_OFF + 4), cb(V_OFF + 4), tb, tb, tb], out_specs=(cb(0), cb(0), cb(0)),
        compiler_params=_params(("parallel", "parallel")),
    )(qkv, qkv, qkv, *tables)


def rotary_bwd(dq, dk, dv, tables, name):
    s, w = dq.shape

    def body(dq_ref, dk_ref, dv_ref, c_ref, s1_ref, s2_ref, qo_ref, ko_ref, vo_ref):
        c, s1, s2 = c_ref[...], s1_ref[...], s2_ref[...]

        def rot_t(dy):
            return dy * c + pltpu.roll(dy * s1, ROT_HALF, 1) + pltpu.roll(dy * s2, LANES - ROT_HALF, 1)

        qo_ref[...] = (rot_t(dq_ref[...]) * 0.125).astype(BF16)
        ko_ref[...] = rot_t(dk_ref[...]).astype(BF16)
        vo_ref[...] = dv_ref[...].astype(BF16)

    cb = pl.BlockSpec((ROWS, LANES), lambda i, j: (i, j))
    tb = pl.BlockSpec((ROWS, LANES), lambda i, j: (i, 0))
    out = jax.ShapeDtypeStruct((s, w), BF16)
    return pl.pallas_call(
        body, name=name, out_shape=(out, out, out), grid=(s // ROWS, w // LANES),
        in_specs=[cb, cb, cb, tb, tb, tb], out_specs=(cb, cb, cb), compiler_params=_params(("parallel", "parallel")),
    )(dq, dk, dv, *tables)


def _deinterleave(dst, src_ref, stride, s, dtype):
    length = s // stride
    for r in range(stride):
        if stride == 1:
            dst[...] = src_ref[...].astype(dtype)
        else:
            dst[r * length:(r + 1) * length, :] = src_ref[pl.ds(r, length, stride=stride), :].astype(dtype)


def _band_masks(row, lane, first):
    return lane <= row, lane >= row + jnp.where(first, BLK, 0)


def dilated_fwd(qd, kd, vd, name, rider=None):
    s, w = qd.shape
    npairs = w // LANES
    nblk = s // BLK

    def body(q_ref, k_ref, v_ref, o_ref, lse_ref, qs, ks, vs, od, ld, on, ln):
        row, lane = _iotas()
        for pi, stride in enumerate(DIL_STRIDES):
            per = (s // stride) // BLK
            _deinterleave(qs, q_ref, stride, s, BF16)
            _deinterleave(ks, k_ref, stride, s, BF16)
            _deinterleave(vs, v_ref, stride, s, BF16)

            def block(b, carry):
                r0 = pl.multiple_of(b * BLK, BLK)
                rp = pl.multiple_of(jnp.maximum(b - 1, 0) * BLK, BLK)
                mc, mp = _band_masks(row, lane, b % per == 0)
                q = qs[pl.ds(r0, BLK), :]
                kc, kp, vc, vp = ks[pl.ds(r0, BLK), :], ks[pl.ds(rp, BLK), :], vs[pl.ds(r0, BLK), :], vs[pl.ds(rp, BLK), :]
                out = jnp.zeros((BLK, LANES), F32)
                lse = jnp.zeros((BLK, LANES), F32)
                for a in range(2):
                    hm = (lane < HEAD_DIM) if a == 0 else (lane >= HEAD_DIM)
                    qa = jnp.where(hm, q.astype(F32), 0.0).astype(BF16)
                    sc = jnp.where(mc, _dot_nt(qa, kc), NEG)
                    sp = jnp.where(mp, _dot_nt(qa, kp), NEG)
                    mx = jnp.maximum(jnp.max(sc, axis=1, keepdims=True), jnp.max(sp, axis=1, keepdims=True))
                    pc, pp = jnp.exp(sc - mx), jnp.exp(sp - mx)
                    l = jnp.sum(pc, axis=1, keepdims=True) + jnp.sum(pp, axis=1, keepdims=True)
                    oa = (_dot(pc.astype(BF16), vc) + _dot(pp.astype(BF16), vp)) / l
                    out = jnp.where(hm, oa, out)
                    lse = jnp.where(hm, mx + jnp.log(l), lse)
                od[pl.ds(r0, BLK), :] = out
                ld[pl.ds(r0, BLK), :] = lse
                return carry

            lax.fori_loop(0, nblk, block, 0, unroll=2)
            length = s // stride
            for r in range(stride):
                if stride == 1:
                    on[pi] = od[...]
                    ln[pi] = ld[...]
                else:
                    on[pi, pl.ds(r, length, stride=stride), :] = od[r * length:(r + 1) * length, :]
                    ln[pi, pl.ds(r, length, stride=stride), :] = ld[r * length:(r + 1) * length, :]

        def merge(n, carry):
            r0 = pl.multiple_of(n * BLK, BLK)
            ls = [ln[pi, pl.ds(r0, BLK), :] for pi in range(3)]
            mx = jnp.maximum(jnp.maximum(ls[0], ls[1]), ls[2])
            ws = [jnp.exp(lv - mx) for lv in ls]
            den = ws[0] + ws[1] + ws[2]
            num = ws[0] * on[0, pl.ds(r0, BLK), :] + ws[1] * on[1, pl.ds(r0, BLK), :] + ws[2] * on[2, pl.ds(r0, BLK), :]
            o_ref[pl.ds(r0, BLK), :] = num / den
            lse_ref[pl.ds(r0, BLK), :] = mx + jnp.log(den)
            return carry

        lax.fori_loop(0, nblk, merge, 0, unroll=2)

    colspec = pl.BlockSpec((s, LANES), lambda p: (0, p))
    out = jax.ShapeDtypeStruct((s, w), F32)
    scratch = [pltpu.VMEM((s, LANES), BF16)] * 3 + [pltpu.VMEM((s, LANES), F32)] * 2 + [pltpu.VMEM((3, s, LANES), F32)] * 2
    (o, lse), rode = call_with_rider(body, name, rider, [qd, kd, vd], [colspec] * 3, [out, out], [colspec, colspec], scratch, (npairs,))
    return o, lse, rode


def dilated_bwd(qd, kd, vd, do, out, lse, do_off, name):
    s, w = qd.shape
    npairs = w // LANES
    nblk = s // BLK

    def body(q_ref, k_ref, v_ref, do_ref, out_ref, lse_ref, dq_ref, dk_ref, dv_ref, qs, ks, vs, dos, dls, lss, dqd, dkd, dvd, dln):
        row, lane = _iotas()
        same_head = jnp.where((row < HEAD_DIM) == (lane < HEAD_DIM), 1.0, 0.0).astype(BF16)

        def delta_blk(n, carry):
            r0 = pl.multiple_of(n * BLK, BLK)
            dln[pl.ds(r0, BLK), :] = _dot_exact_lhs(do_ref[pl.ds(r0, BLK), :] * out_ref[pl.ds(r0, BLK), :], same_head)
            return carry

        lax.fori_loop(0, nblk, delta_blk, 0, unroll=2)
        for pi, stride in enumerate(DIL_STRIDES):
            per = (s // stride) // BLK
            _deinterleave(qs, q_ref, stride, s, BF16)
            _deinterleave(ks, k_ref, stride, s, BF16)
            _deinterleave(vs, v_ref, stride, s, BF16)
            _deinterleave(dos, do_ref, stride, s, BF16)
            _deinterleave(dls, dln, stride, s, F32)
            _deinterleave(lss, lse_ref, stride, s, F32)

            def block(b, carry):
                r0 = pl.multiple_of(b * BLK, BLK)
                rp = pl.multiple_of(jnp.maximum(b - 1, 0) * BLK, BLK)
                first = b % per == 0
                mc, mp = _band_masks(row, lane, first)
                q, dov = qs[pl.ds(r0, BLK), :], dos[pl.ds(r0, BLK), :]
                kc, kp, vc, vp = ks[pl.ds(r0, BLK), :], ks[pl.ds(rp, BLK), :], vs[pl.ds(r0, BLK), :], vs[pl.ds(rp, BLK), :]
                lse_t, dl_t = lss[pl.ds(r0, BLK), :], dls[pl.ds(r0, BLK), :]
                dq = jnp.zeros((BLK, LANES), F32)
                dkc = jnp.zeros((BLK, LANES), F32)
                dkp = jnp.zeros((BLK, LANES), F32)
                dvc = jnp.zeros((BLK, LANES), F32)
                dvp = jnp.zeros((BLK, LANES), F32)
                for a in range(2):
                    hm = (lane < HEAD_DIM) if a == 0 else (lane >= HEAD_DIM)
                    pick = lane == a * HEAD_DIM
                    qa = jnp.where(hm, q.astype(F32), 0.0).astype(BF16)
                    doa = jnp.where(hm, dov.astype(F32), 0.0).astype(BF16)
                    lse_a = jnp.sum(jnp.where(pick, lse_t, 0.0), axis=1, keepdims=True)
                    dl_a = jnp.sum(jnp.where(pick, dl_t, 0.0), axis=1, keepdims=True)
                    pc = jnp.where(mc, jnp.exp(_dot_nt(qa, kc) - lse_a), 0.0)
                    pp = jnp.where(mp, jnp.exp(_dot_nt(qa, kp) - lse_a), 0.0)
                    dsc = (pc * (_dot_nt(doa, vc) - dl_a)).astype(BF16)
                    dsp = (pp * (_dot_nt(doa, vp) - dl_a)).astype(BF16)
                    dq = jnp.where(hm, _dot(dsc, kc) + _dot(dsp, kp), dq)
                    dkc += _dot_tn(dsc, qa)
                    dkp += _dot_tn(dsp, qa)
                    dvc += _dot_tn(pc.astype(BF16), doa)
                    dvp += _dot_tn(pp.astype(BF16), doa)
                dqd[pl.ds(r0, BLK), :] = dq
                dkd[pl.ds(r0, BLK), :] = dkc
                dvd[pl.ds(r0, BLK), :] = dvc

                @pl.when(jnp.logical_not(first))
                def _():
                    dkd[pl.ds(rp, BLK), :] += dkp
                    dvd[pl.ds(rp, BLK), :] += dvp

                return carry

            lax.fori_loop(0, nblk, block, 0, unroll=2)
            length = s // stride
            for dst, src in ((dq_ref, dqd), (dk_ref, dkd), (dv_ref, dvd)):
                for r in range(stride):
                    if stride == 1:
                        dst[...] = src[...]
                    else:
                        dst[pl.ds(r, length, stride=stride), :] += src[r * length:(r + 1) * length, :]

    colspec = pl.BlockSpec((s, LANES), lambda p: (0, p))
    do_spec = pl.BlockSpec((s, LANES), lambda p: (0, do_off + p))
    o3 = jax.ShapeDtypeStruct((s, w), F32)
    return pl.pallas_call(
        body, name=name, out_shape=(o3, o3, o3), grid=(npairs,),
        in_specs=[colspec, colspec, colspec, do_spec, colspec, colspec], out_specs=(colspec, colspec, colspec),
        scratch_shapes=[pltpu.VMEM((s, LANES), BF16)] * 4 + [pltpu.VMEM((s, LANES), F32)] * 6,
        compiler_params=_params(("parallel",)),
    )(qd, kd, vd, do, out, lse)


def adamw(w, g, m, v, name):
    rows, cols = w.shape
    rb = min(rows, ROWS)
    c1 = 1.0 - ADAM_B1 ** ADAM_STEP
    c2 = 1.0 - ADAM_B2 ** ADAM_STEP

    def body(w_ref, g_ref, m_ref, v_ref, d_ref, mo_ref, vo_ref):
        gv = g_ref[...]
        mn = ADAM_B1 * m_ref[...] + (1.0 - ADAM_B1) * gv
        vn = ADAM_B2 * v_ref[...] + (1.0 - ADAM_B2) * (gv * gv)
        d_ref[...] = -ADAM_LR * ((mn / c1) / (jnp.sqrt(vn / c2) + ADAM_EPS) + ADAM_WD * w_ref[...])
        mo_ref[...] = mn
        vo_ref[...] = vn

    spec = _row_spec(cols, rb)
    out = jax.ShapeDtypeStruct((rows, cols), F32)
    return pl.pallas_call(
        body, name=name, out_shape=(out, out, out), grid=(rows // rb,), in_specs=[spec] * 4, out_specs=(spec,) * 3,
        compiler_params=_params(("parallel",)),
    )(w, g, m, v)


def _prefetch_call(body, name, scalar, ops, grid, in_specs, out_specs, out_shape, sem):
    spec = pltpu.PrefetchScalarGridSpec(num_scalar_prefetch=1, grid=grid, in_specs=in_specs, out_specs=out_specs)
    return pl.pallas_call(body, name=name, grid_spec=spec, out_shape=out_shape, compiler_params=_params(sem))(scalar, *ops)


def pair_sum(g, got, core, name):
    nc, r, c = g.shape
    rh = r // 2

    def body(core_ref, g_ref, got_ref, o_ref):
        o_ref[...] = (g_ref[...].astype(F32) + got_ref[...].astype(F32)).astype(BF16)

    blk = lambda rows_of: pl.BlockSpec((None, rh, c), rows_of)
    return _prefetch_call(
        body, name, core, (g, got), (nc,),
        [blk(lambda j, core_ref: (j, core_ref[0], 0)), blk(lambda j, core_ref: (j, 0, 0))],
        blk(lambda j, core_ref: (j, 0, 0)), jax.ShapeDtypeStruct((nc, rh, c), BF16), ("parallel",))


def chip_sum(pair, got, chip, layer, into, name):
    _, rh, c = pair.shape

    def body(chip_ref, p_ref, a_ref, b_ref, c_ref, old_ref, o_ref):
        o_ref[...] = ((p_ref[...].astype(F32) + a_ref[...].astype(F32)) + b_ref[...].astype(F32)) + c_ref[...].astype(F32)

    arrival = lambda k: pl.BlockSpec((None, rh, c), lambda i, chip_ref: (k, 0, 0))
    spec = pltpu.PrefetchScalarGridSpec(
        num_scalar_prefetch=1, grid=(1,),
        in_specs=[pl.BlockSpec((None, rh, c), lambda i, chip_ref: (chip_ref[0], 0, 0)), arrival(0), arrival(1), arrival(2), _ANY],
        out_specs=pl.BlockSpec((None, rh, c), lambda i, chip_ref: (layer, 0, 0)))
    return pl.pallas_call(body, name=name, grid_spec=spec, out_shape=jax.ShapeDtypeStruct(into.shape, into.dtype),
                          input_output_aliases={5: 0}, compiler_params=_params(("arbitrary",)))(chip, pair, got, got, got, into)


def adamw_family(w, m, v, g_mine, g_other, core, name):
    nl, r, c = w.shape
    rh = r // 2
    nb = 4 if rh % 512 == 0 else (2 if rh % 16 == 0 and rh > 256 else 1)
    rb = rh // nb
    c1 = 1.0 - ADAM_B1 ** ADAM_STEP
    c2 = 1.0 - ADAM_B2 ** ADAM_STEP

    def body(core_ref, w_ref, m_ref, v_ref, gm_ref, go_ref, g_ref, d_ref, mo_ref, vo_ref):
        gv = jnp.where(pl.program_id(1) == core_ref[0], gm_ref[...], go_ref[...])
        mn = ADAM_B1 * m_ref[...] + (1.0 - ADAM_B1) * gv
        vn = ADAM_B2 * v_ref[...] + (1.0 - ADAM_B2) * (gv * gv)
        g_ref[...] = gv
        d_ref[...] = -ADAM_LR * ((mn / c1) / (jnp.sqrt(vn / c2) + ADAM_EPS) + ADAM_WD * w_ref[...])
        mo_ref[...] = mn
        vo_ref[...] = vn

    full = pl.BlockSpec((None, rb, c), lambda l, h, i, core_ref: (l, h * nb + i, 0))
    half = pl.BlockSpec((None, rb, c), lambda l, h, i, core_ref: (l, i, 0))
    out = jax.ShapeDtypeStruct((nl, r, c), F32)
    return _prefetch_call(body, name, core, (w, m, v, g_mine, g_other), (nl, 2, nb), [full, full, full, half, half],
                          (full, full, full, full), (out, out, out, out), ("parallel", "parallel", "parallel"))


def _coords():
    return lax.axis_index("x"), lax.axis_index("y"), lax.axis_index("c")


def _other_chips(x, y):
    return ((1 - x, y), (x, 1 - y), (1 - x, 1 - y))


_ANY = pl.BlockSpec(memory_space=pl.ANY)


def _exchange_call(body, name, arrays, out_shapes, n_copies, n_local=0):
    n = len(arrays)

    def wrapped(*refs):
        body(refs[:n], refs[n:n + len(out_shapes)], *refs[n + len(out_shapes):])

    scratch = [pltpu.SemaphoreType.DMA((n_copies,)), pltpu.SemaphoreType.DMA((n_copies,))]
    if n_local:
        scratch.append(pltpu.SemaphoreType.DMA((n_local,)))
    return pl.pallas_call(
        wrapped, name=name, out_shape=tuple(out_shapes), in_specs=[_ANY] * n, out_specs=tuple([_ANY] * len(out_shapes)),
        scratch_shapes=scratch, compiler_params=_params(),
    )(*arrays)


def _remote(send_sems, recv_sems, n, src, dst, to):
    return pltpu.make_async_remote_copy(src_ref=src, dst_ref=dst, send_sem=send_sems.at[n], recv_sem=recv_sems.at[n],
                                        device_id=to, device_id_type=MESH)


class Rider:
    def __init__(self, arrays, out_shapes, n_remote, n_local, copies, then=None):
        self.arrays, self.out_shapes, self.n_remote, self.n_local = list(arrays), list(out_shapes), n_remote, n_local
        self.copies, self.then = copies, then

    def sems(self):
        return [pltpu.SemaphoreType.DMA((self.n_remote,)), pltpu.SemaphoreType.DMA((self.n_remote,)),
                pltpu.SemaphoreType.DMA((max(self.n_local, 1),))]

    def run(self, name):
        n, no = len(self.arrays), len(self.out_shapes)

        def body(*refs):
            for stage in (self.copies, self.then):
                if stage is not None:
                    cps = stage(refs[:n], refs[n:n + no], *refs[n + no:])
                    for cp in cps:
                        cp.start()
                    for cp in cps:
                        cp.wait()

        return pl.pallas_call(
            body, name=name, out_shape=tuple(self.out_shapes), in_specs=[_ANY] * n, out_specs=tuple([_ANY] * no),
            scratch_shapes=self.sems(), compiler_params=_params(),
        )(*self.arrays)


def ride(rider, body, n_in, n_out, grid):
    if rider is None:
        return body
    ni, no = len(rider.arrays), len(rider.out_shapes)

    def wrapped(*refs):
        ins, r_in = refs[:n_in], refs[n_in:n_in + ni]
        outs = refs[n_in + ni:n_in + ni + n_out]
        r_out = refs[n_in + ni + n_out:n_in + ni + n_out + no]
        rest = refs[n_in + ni + n_out + no:]
        scratch, sems = rest[:len(rest) - 3], rest[len(rest) - 3:]
        step, total = 0, 1
        for a, g in enumerate(grid):
            step, total = step * g + pl.program_id(a), total * g
        assert total >= 3
        relay_at = (4 * total) // 5 if rider.then is not None else total - 1

        @pl.when(step == 0)
        def _():
            for cp in rider.copies(r_in, r_out, *sems):
                cp.start()

        body(*ins, *outs, *scratch)

        @pl.when(step == relay_at)
        def _():
            for cp in rider.copies(r_in, r_out, *sems):
                cp.wait()
            if rider.then is not None:
                for cp in rider.then(r_in, r_out, *sems):
                    cp.start()

        if rider.then is not None:
            @pl.when(step == total - 1)
            def _():
                for cp in rider.then(r_in, r_out, *sems):
                    cp.wait()

    return wrapped


def call_with_rider(body, name, rider, ops, in_specs, out_shape, out_specs, scratch, grid):
    n_in, n_out = len(ops), len(out_shape)
    ops, in_specs, out_shape, out_specs, scratch = list(ops), list(in_specs), list(out_shape), list(out_specs), list(scratch)
    if rider is not None:
        ops += rider.arrays
        in_specs += [_ANY] * len(rider.arrays)
        out_shape += rider.out_shapes
        out_specs += [_ANY] * len(rider.out_shapes)
        scratch += rider.sems()
    res = pl.pallas_call(
        ride(rider, body, n_in, n_out, grid), name=name, out_shape=tuple(out_shape), grid=grid, in_specs=in_specs,
        out_specs=tuple(out_specs), scratch_shapes=scratch, compiler_params=_params(("arbitrary",) * len(grid)),
    )(*ops)
    return tuple(res[:n_out]), list(res[n_out:])


def gather_rider(shards):
    nf = len(shards)
    half = lambda ref, which: pl.ds(which * (ref.shape[-2] // 2), ref.shape[-2] // 2)

    def copies(s_refs, o_refs, send_sems, recv_sems, local_sems):
        x, y, c = _coords()
        me = 2 * x + y
        cps = [pltpu.make_async_copy(s_refs[f], o_refs[f].at[me], local_sems.at[f]) for f in range(nf)]
        for k, (px, py) in enumerate(_other_chips(x, y)):
            for f in range(nf):
                rows = half(s_refs[f], c)
                cps.append(_remote(send_sems, recv_sems, k * nf + f, s_refs[f].at[rows], o_refs[f].at[me, rows], (px, py, c)))
        return cps

    def relay(s_refs, o_refs, send_sems, recv_sems, local_sems):
        x, y, c = _coords()
        cps = []
        for k, (px, py) in enumerate(_other_chips(x, y)):
            for f in range(nf):
                landed = o_refs[f].at[2 * px + py, half(s_refs[f], c)]
                cps.append(_remote(send_sems, recv_sems, (3 + k) * nf + f, landed, landed, (x, y, 1 - c)))
        return cps

    return Rider(shards, [jax.ShapeDtypeStruct((N_CHIPS,) + sh.shape, sh.dtype) for sh in shards], 6 * nf, nf, copies, relay)


def scatter_rider(pairs):
    nf = len(pairs)

    def copies(p_refs, o_refs, send_sems, recv_sems, local_sems):
        x, y, c = _coords()
        cps = []
        for k, (px, py) in enumerate(_other_chips(x, y)):
            for f in range(nf):
                cps.append(_remote(send_sems, recv_sems, k * nf + f, p_refs[f].at[2 * px + py], o_refs[f].at[k], (px, py, c)))
        return cps

    return Rider(pairs, [jax.ShapeDtypeStruct((3,) + p.shape[1:], p.dtype) for p in pairs], 3 * nf, 0, copies)


def pair_swap(grads, name):
    def body(g_refs, o_refs, send_sems, recv_sems):
        x, y, c = _coords()
        cps = []
        for f, g_ref in enumerate(g_refs):
            rh = g_ref.shape[1] // 2
            cps.append(_remote(send_sems, recv_sems, f, g_ref.at[:, pl.ds((1 - c) * rh, rh), :], o_refs[f], (x, y, 1 - c)))
        for cp in cps:
            cp.start()
        for cp in cps:
            cp.wait()

    outs = [jax.ShapeDtypeStruct((g.shape[0], g.shape[1] // 2, g.shape[2]), g.dtype) for g in grads]
    return _exchange_call(body, name, grads, outs, len(grads))


def half_swap(halves, name):
    def body(h_refs, o_refs, send_sems, recv_sems):
        x, y, c = _coords()
        cps = [_remote(send_sems, recv_sems, f, h_ref, o_refs[f], (x, y, 1 - c)) for f, h_ref in enumerate(h_refs)]
        for cp in cps:
            cp.start()
        for cp in cps:
            cp.wait()

    return _exchange_call(body, name, halves, [jax.ShapeDtypeStruct(h.shape, h.dtype) for h in halves], len(halves))


def allsum_small(part, name):
    def body(p_ref, tot_ref, all_ref, send_sems, recv_sems):
        x, y, c = _coords()
        me, sibling = (x, y, c), (x, y, 1 - c)
        chips = _other_chips(x, y)

        def slot(px, py, pc):
            return all_ref.at[4 * px + 2 * py + pc]

        def copy(k, block, to, src=None):
            return pltpu.make_async_remote_copy(src_ref=slot(*block) if src is None else src, dst_ref=slot(*block),
                                                send_sem=send_sems.at[k], recv_sem=recv_sems.at[k], device_id=to, device_id_type=MESH)

        slot(*me)[...] = p_ref[...]
        first = [copy(0, me, sibling, src=p_ref)] + [copy(1 + j, me, (*chip, c), src=p_ref) for j, chip in enumerate(chips)]
        for cp in first:
            cp.start()
        passed = [copy(4 + j, (*chip, c), sibling) for j, chip in enumerate(chips)]
        for j, chip in enumerate(chips):
            copy(1 + j, (*chip, c), me).wait_recv()
            passed[j].start()
        copy(0, sibling, me).wait_recv()
        for j, chip in enumerate(chips):
            copy(4 + j, (*chip, 1 - c), me).wait_recv()
        for cp in first + passed:
            cp.wait_send()
        tot = all_ref[0]
        for d in range(1, 8):
            tot = tot + all_ref[d]
        tot_ref[...] = tot

    vm = pl.BlockSpec(memory_space=pltpu.VMEM)
    return pl.pallas_call(
        body, name=name, out_shape=jax.ShapeDtypeStruct(part.shape, F32), in_specs=[vm], out_specs=vm,
        scratch_shapes=[pltpu.VMEM((8,) + part.shape, F32), pltpu.SemaphoreType.DMA((7,)), pltpu.SemaphoreType.DMA((7,))],
        compiler_params=_params(),
    )(part)


QKVF_COLS = 772
QKVF_PAD = 896


def _tables_for(s):
    return _rot_tables(s)


def layer_families(layer):
    return (0, 1, layer // 2) if layer % 2 == 0 else (2, 3, layer // 2)


class GradientExchange:
    def __init__(self):
        self.core = lax.axis_index("c").astype(jnp.int32).reshape(1)
        self.chip = (2 * lax.axis_index("x") + lax.axis_index("y")).astype(jnp.int32).reshape(1)
        self.pairs, self.arrived, self.pending, self.riding = {}, {}, [], []

    def add(self, items, tag):
        got = pair_swap([g for _, _, g in items], f"grad_pair_swap_{tag}")
        for (fam, li, g), r in zip(items, got):
            self.pairs[(fam, li)] = pair_sum(g, r, self.core, f"grad_pair_sum_{fam}_{li}")
            self.pending.append((fam, li))

    def rider(self):
        self.riding, self.pending = self.pending, []
        return scatter_rider([self.pairs[k] for k in self.riding])

    def landed(self, outs):
        for k, o in zip(self.riding, outs):
            self.arrived[k] = o
        self.riding = []

    def finish(self, weights, moments1, moments2):
        if self.pending:
            self.landed(self.rider().run("grad_chip_scatter_last"))
        mine = []
        for fam, w in enumerate(weights):
            buf = jnp.zeros((w.shape[0], w.shape[1] // 2, w.shape[2]), F32)
            for li in range(w.shape[0]):
                buf = chip_sum(self.pairs[(fam, li)], self.arrived[(fam, li)], self.chip, li, buf, f"grad_chip_sum_{fam}_{li}")
            mine.append(buf)
        other = half_swap(mine, "grad_half_swap")
        return [adamw_family(w, m, v, gm, go, self.core, f"adamw_{f}")
                for f, (w, m, v, gm, go) in enumerate(zip(weights, moments1, moments2, mine, other))]


class KeepGradients:
    def __init__(self):
        self.grads = {}

    def add(self, items, tag):
        for fam, li, g in items:
            self.grads[(fam, li)] = g

    def rider(self):
        return None

    def landed(self, outs):
        pass


def kernel(x, norm_mix, w_qkv_even, w_o_even, w_qkvf_odd, b_forget, w_o_odd, norm_ffn, w_ffn_in, w_ffn_out, norm_final, loss_target, m_norm_mix, m_w_qkv_even, m_w_o_even, m_w_qkvf_odd, m_b_forget, m_w_o_odd, m_norm_ffn, m_w_ffn_in, m_w_ffn_out, m_norm_final, v_norm_mix, v_w_qkv_even, v_w_o_even, v_w_qkvf_odd, v_b_forget, v_w_o_odd, v_norm_ffn, v_w_ffn_in, v_w_ffn_out, v_norm_final):
    w_shards = [w_qkv_even, w_o_even, w_qkvf_odd, w_o_odd, w_ffn_in, w_ffn_out]
    shards = [w.astype(BF16) for w in w_shards]
    tables = _tables_for(x.shape[1])
    bias_pad = jnp.pad(b_forget, ((0, 0), (0, LANES - N_HEADS)))

    attn_w = {0: gather_rider([shards[0][0], shards[1][0]]).run("gather_first")}
    saved, cur = [], x[0]
    for layer in range(DEPTH):
        ahead = [l for l in (layer + 1, layer + 2) if l < DEPTH and l not in attn_w] if layer % 2 == 0 else []
        side_rider = None
        if ahead:
            fams = [layer_families(l) for l in ahead]
            side_rider = gather_rider([shards[f][li] for fam_qkv, fam_o, li in fams for f in (fam_qkv, fam_o)])
        w_qkv, w_o = attn_w[layer]
        cur, keep, _, side = forward_layer(layer, cur, w_qkv, w_o, norm_mix[layer:layer + 1], norm_ffn[layer:layer + 1], tables,
                                           bias_pad[layer // 2:layer // 2 + 1], rider=gather_rider([shards[4][layer], shards[5][layer]]),
                                           side_rider=side_rider)
        saved.append(keep)
        for n, l in enumerate(ahead):
            attn_w[l] = (side[2 * n], side[2 * n + 1])

    dcur, g_final, loss_part = loss_head(cur, norm_final.reshape(1, D_MODEL), loss_target[0], "loss_head")

    exchange = GradientExchange()
    g_mix, g_ffn, g_bias = [None] * DEPTH, [None] * DEPTH, [None] * (DEPTH // 2)
    for layer in reversed(range(DEPTH)):
        dcur, g_mix[layer], g_ffn[layer], g_b = backward_layer(layer, dcur, saved[layer], norm_mix[layer:layer + 1],
                                                               norm_ffn[layer:layer + 1], tables, bias_pad[layer // 2:layer // 2 + 1], exchange)
        if g_b is not None:
            g_bias[layer // 2] = g_b

    zero_row = jnp.zeros((1, D_MODEL), F32)
    pad16 = lambda v: jnp.pad(v, (0, D_MODEL - v.shape[0]))[None, :]
    small_rows = lambda mix, ffn, fin, bias, last: jnp.concatenate(
        [r.reshape(1, D_MODEL) for r in mix] + [r.reshape(1, D_MODEL) for r in ffn] + [fin.reshape(1, D_MODEL)]
        + [pad16(b) for b in bias] + [last] + [zero_row] * (SMALL_ROWS - 12), axis=0)
    loss_row = pad16(loss_part[0, :1])
    small_g = allsum_small(small_rows(g_mix, g_ffn, g_final, g_bias, loss_row), "allsum_small")
    loss = small_g[11, 0]
    small_g = small_g.at[11].set(0.0)
    sw = small_rows(list(norm_mix), list(norm_ffn), norm_final, list(b_forget), zero_row)
    sm = small_rows(list(m_norm_mix), list(m_norm_ffn), m_norm_final, list(m_b_forget), zero_row)
    sv = small_rows(list(v_norm_mix), list(v_norm_ffn), v_norm_final, list(v_b_forget), zero_row)
    sd, snm, snv = adamw(sw, small_g, sm, sv, "adamw_small")

    def small_out(a):
        return a[0:4], a[8, :], a[9:11, :N_HEADS], a[4:8]

    widen = lambda t: jnp.pad(t, ((0, 0), (0, 0), (0, QKVF_PAD - QKVF_COLS)))
    padded = lambda ws: [widen(t) if f == 2 else t for f, t in enumerate(ws)]
    big = exchange.finish(
        padded(w_shards), padded([m_w_qkv_even, m_w_o_even, m_w_qkvf_odd, m_w_o_odd, m_w_ffn_in, m_w_ffn_out]),
        padded([v_w_qkv_even, v_w_o_even, v_w_qkvf_odd, v_w_o_odd, v_w_ffn_in, v_w_ffn_out]))

    def outputs(small, which):
        mix, fin, bias, ffn = small_out(small)
        qkv_e, o_e, qkvf, o_o, fi, fo = [big[f][which][:, :, :QKVF_COLS] if f == 2 else big[f][which] for f in range(6)]
        return [mix, qkv_e, o_e, qkvf, bias, o_o, ffn, fi, fo, fin]

    return (loss, dcur[None], *outputs(small_g, 0), *outputs(sd, 1), *outputs(snm, 2), *outputs(snv, 3))


def _chip_tile(rows, cols, at):
    return pl.BlockSpec((None, rows, cols), at)


def forward_layer(layer, cur, w_qkv, w_o, mix_gain, ffn_gain, tables, bias_row, rider=None, side_rider=None, w_ffn=None):
    n = f"l{layer}"
    s = cur.shape[0]
    h1 = rmsnorm_fwd(cur, mix_gain, n + "_norm_mix")
    keep = {"x": cur, "h1": h1, "w_o": w_o.reshape(D_ATTN, D_MODEL)}
    side = []
    if layer % 2 == 0:
        qkv = matmul(h1, w_qkv, "nn", BF16, n + "_qkv", 1024, 768, 1024, mnk=(s, 3 * D_ATTN, D_MODEL),
                     b_spec=_chip_tile(D_MODEL, 768, lambda i, j, kk: (j, 0, 0)))
        o_sb, st, rode = causal_fwd(qkv, 4, "sb", n + "_sb_fwd", rider=rider)
        qd, kd, vd = rotary_prep(qkv, tables, n + "_rotary")
        o_dil, lse_dil, side = dilated_fwd(qd, kd, vd, n + "_dil_fwd", rider=side_rider)
        attn = jnp.concatenate([o_sb, o_dil], axis=1).astype(BF16)
        keep.update(qd=qd, kd=kd, vd=vd, o_dil=o_dil, lse_dil=lse_dil, w_qkv=w_qkv)
    else:
        natural = jnp.transpose(w_qkv, (1, 0, 2)).reshape(D_MODEL, N_CHIPS * QKVF_COLS)
        w_gate = jnp.pad(natural[:, 3 * D_ATTN:], ((0, 0), (0, LANES - N_HEADS)))
        qkv = matmul(h1, natural[:, :3 * D_ATTN], "nn", BF16, n + "_qkv", 1024, 768, 1024)
        fl = matmul(h1, w_gate, "nn", F32, n + "_fgate", 512, LANES, 1024)
        cum = forget_fwd(fl, bias_row, n + "_forget_fwd")
        f_heads = cum[:, :N_HEADS].T
        fq = jnp.broadcast_to(f_heads[:, :, None], (N_HEADS, s, LANES))
        fk = f_heads.reshape(N_HEADS // 2, 2, s)
        attn, st, rode = causal_fwd(qkv, 8, "fox", n + "_fox_fwd", fq=fq, fk=fk, rider=rider)
        attn = attn.astype(BF16)
        keep.update(fl=fl, fq=fq, fk=fk, w_qkv=jnp.concatenate([natural[:, :3 * D_ATTN], w_gate], axis=1))
    w_fi, w_fo = (rode[0], rode[1]) if rider is not None else w_ffn
    w_fo = w_fo.reshape(D_FF, D_MODEL)
    mid = matmul(attn, keep["w_o"], "nn", F32, n + "_attn_out", 1024, 1024, 1024, res=cur)
    h2 = rmsnorm_fwd(mid, ffn_gain, n + "_norm_ffn")
    gu = matmul(h2, w_fi, "nn", BF16, n + "_ffn_in", 1024, 1408, 1024, mnk=(s, 2 * D_FF, D_MODEL),
                b_spec=_chip_tile(D_MODEL, 1408, lambda i, j, kk: (j, 0, 0)))
    act = swiglu_fwd(gu, n + "_swiglu")
    out = matmul(act, w_fo, "nn", F32, n + "_ffn_out", 512, 1024, D_FF, res=mid)
    keep.update(qkv=qkv, st=st, attn=attn, mid=mid, h2=h2, gu=gu, act=act, w_fi=w_fi, w_fo=w_fo)
    return out, keep, rode, side


def backward_layer(layer, dcur, kp, mix_gain, ffn_gain, tables, bias_row, exchange):
    n = f"l{layer}"
    s = dcur.shape[0]
    fam_qkv, fam_o, li = layer_families(layer)
    g_fo = matmul(kp["act"], dcur, "tn", BF16, n + "_d_w_ffn_out", 1408, 1024, 512)
    dact = matmul(dcur, kp["w_fo"], "nt", BF16, n + "_d_act", 1024, 1408, 1024)
    dgu = swiglu_bwd(kp["gu"], dact, n + "_d_swiglu")
    g_fi = matmul(kp["h2"], dgu, "tn", BF16, n + "_d_w_ffn_in", 1024, 1408, 2048, mnk=(D_MODEL, 2 * D_FF, s),
                  o_spec=_chip_tile(D_MODEL, 1408, lambda i, j, kk: (j, 0, 0)), out_shape=(N_CHIPS, D_MODEL, 1408))
    dh2 = matmul(dgu, kp["w_fi"], "nt", F32, n + "_d_h2", 1024, 1024, 1408, mnk=(s, D_MODEL, 2 * D_FF),
                 b_spec=_chip_tile(D_MODEL, 1408, lambda i, j, kk: (kk, 0, 0)))
    dmid, g_ffn = rmsnorm_bwd(kp["mid"], ffn_gain, dh2, dcur, n + "_d_norm_ffn")
    g_o = matmul(kp["attn"], dmid, "tn", BF16, n + "_d_w_o", 1024, 1024, 512)
    dattn = matmul(dmid, kp["w_o"], "nt", F32, n + "_d_attn", 1024, 1024, 1024)
    exchange.add([(5, layer, g_fo.reshape(N_CHIPS, D_FF // N_CHIPS, D_MODEL)), (4, layer, g_fi),
                  (fam_o, li, g_o.reshape(N_CHIPS, D_ATTN // N_CHIPS, D_MODEL))], f"l{layer}_ffn")
    rider = exchange.rider()
    g_bias = None
    if layer % 2 == 0:
        dq_a, dk_a, dv_a, rode = causal_bwd(kp["qkv"], dattn, kp["st"], 4, "sb", n + "_sb_bwd", rider=rider)
        dqd, dkd, dvd = dilated_bwd(kp["qd"], kp["kd"], kp["vd"], dattn, kp["o_dil"], kp["lse_dil"], 4, n + "_dil_bwd")
        dq_b, dk_b, dv_b = rotary_bwd(dqd, dkd, dvd, tables, n + "_d_rotary")
        dproj = jnp.concatenate([dq_a, dq_b, dk_a.astype(BF16), dk_b, dv_a.astype(BF16), dv_b], axis=1)
        g_qkv = matmul(kp["h1"], dproj, "tn", BF16, n + "_d_w_qkv", 1024, 768, 2048, mnk=(D_MODEL, 3 * D_ATTN, s),
                       o_spec=_chip_tile(D_MODEL, 768, lambda i, j, kk: (j, 0, 0)), out_shape=(N_CHIPS, D_MODEL, 768))
        dh1 = matmul(dproj, kp["w_qkv"], "nt", F32, n + "_d_h1", 1024, 1024, 768, mnk=(s, D_MODEL, 3 * D_ATTN),
                     b_spec=_chip_tile(D_MODEL, 768, lambda i, j, kk: (kk, 0, 0)))
    else:
        dq_f, dk_f, dv_f, dfk, rode = causal_bwd(kp["qkv"], dattn, kp["st"], 8, "fox", n + "_fox_bwd", fq=kp["fq"], fk=kp["fk"],
                                                 rider=rider)
        dcum = jnp.pad(dfk.reshape(N_HEADS, s).T, ((0, 0), (0, LANES - N_HEADS)))
        dfl, dbias = forget_bwd(kp["fl"], bias_row, dcum, n + "_forget_bwd")
        g_bias = dbias[0, :N_HEADS]
        dproj = jnp.concatenate([dq_f, dk_f.astype(BF16), dv_f.astype(BF16), dfl.astype(BF16)], axis=1)
        by_chip = dproj[:, :N_CHIPS * QKVF_COLS].reshape(s, N_CHIPS, QKVF_COLS)
        by_chip = jnp.pad(by_chip, ((0, 0), (0, 0), (0, QKVF_PAD - QKVF_COLS))).reshape(s, N_CHIPS * QKVF_PAD)
        g_qkv = matmul(kp["h1"], by_chip, "tn", BF16, n + "_d_w_qkv", 1024, QKVF_PAD, 2048, mnk=(D_MODEL, N_CHIPS * QKVF_PAD, s),
                       o_spec=_chip_tile(D_MODEL, QKVF_PAD, lambda i, j, kk: (j, 0, 0)), out_shape=(N_CHIPS, D_MODEL, QKVF_PAD))
        dh1 = matmul(dproj, kp["w_qkv"], "nt", F32, n + "_d_h1", 1024, 1024, 640)
    exchange.landed(rode)
    exchange.add([(fam_qkv, li, g_qkv)], f"l{layer}_qkv")
    dx, g_mix = rmsnorm_bwd(kp["x"], mix_gain, dh1, dmid, n + "_d_norm_mix")
    return dx, g_mix, g_ffn, g_bias


def local_step(xs, target, norm_mix, norm_ffn, norm_final, b_forget, layer_weights):
    tables = _tables_for(xs.shape[0])
    bias_pad = jnp.pad(b_forget, ((0, 0), (0, LANES - N_HEADS)))
    saved, cur = [], xs
    for layer in range(DEPTH):
        w_qkv, w_o, w_fi, w_fo = layer_weights[layer]
        cur, keep, _, _ = forward_layer(layer, cur, w_qkv, w_o, norm_mix[layer:layer + 1], norm_ffn[layer:layer + 1], tables,
                                        bias_pad[layer // 2:layer // 2 + 1], w_ffn=(w_fi, w_fo))
        saved.append(keep)
    dcur, g_final, loss_part = loss_head(cur, norm_final.reshape(1, D_MODEL), target, "loss_head")
    keeper = KeepGradients()
    g_mix, g_ffn, g_bias = [None] * DEPTH, [None] * DEPTH, [None] * (DEPTH // 2)
    for layer in reversed(range(DEPTH)):
        dcur, g_mix[layer], g_ffn[layer], g_b = backward_layer(layer, dcur, saved[layer], norm_mix[layer:layer + 1],
                                                               norm_ffn[layer:layer + 1], tables, bias_pad[layer // 2:layer // 2 + 1], keeper)
        if g_b is not None:
            g_bias[layer // 2] = g_b
    return dcur, keeper.grads, (g_mix, g_ffn, g_final, g_bias), loss_part
```

```python
import functools

import jax
import jax.numpy as jnp
from jax import lax
from jax.experimental import pallas as pl
from jax.experimental.pallas import tpu as pltpu

F32 = jnp.float32
BF16 = jnp.bfloat16
MESH = pl.DeviceIdType.MESH

D_MODEL = 1024
DEPTH = 4
HEAD_DIM = 64
N_HEADS = 16
D_ATTN = 1024
D_FF = 2816
ROPE_THETA = 500000.0
ROT_HALF = 8
RMS_EPS = 1e-5
DIL_STRIDES = (1, 4, 16)
ADAM_LR, ADAM_B1, ADAM_B2, ADAM_EPS, ADAM_WD, ADAM_STEP = 0.001, 0.9, 0.999, 1e-8, 0.01, 10

LANES = 128
BLK = 128
VMEM_LIMIT = 56 * 1024 * 1024
NEG = -1e30
N_CHIPS = 4
FLAT_COLS = 1024
FLAT_ROWS = 12800
HALF_ROWS = FLAT_ROWS // 2
SMALL_ROWS = 16


def _params(sem=None):
    return pltpu.CompilerParams(dimension_semantics=sem, vmem_limit_bytes=VMEM_LIMIT)


def _dot(a, b):
    return lax.dot_general(a, b, (((1,), (0,)), ((), ())), preferred_element_type=F32)


def _dot_nt(a, b):
    return lax.dot_general(a, b, (((1,), (1,)), ((), ())), preferred_element_type=F32)


def _dot_tn(a, b):
    return lax.dot_general(a, b, (((0,), (0,)), ((), ())), preferred_element_type=F32)


def _split3(x):
    x1 = x.astype(BF16)
    r1 = x - x1.astype(F32)
    x2 = r1.astype(BF16)
    x3 = (r1 - x2.astype(F32)).astype(BF16)
    return x1, x2, x3


def _dot_exact_lhs(x, t):
    x1, x2, x3 = _split3(x)
    return _dot(x1, t) + _dot(x2, t) + _dot(x3, t)


def _dot_exact_rhs(t, x):
    x1, x2, x3 = _split3(x)
    return _dot(t, x1) + _dot(t, x2) + _dot(t, x3)


def _iotas(shape=(BLK, LANES)):
    return lax.broadcasted_iota(jnp.int32, shape, 0), lax.broadcasted_iota(jnp.int32, shape, 1)


_DIMS = {"nn": (((1,), (0,)), ((), ())), "nt": (((1,), (1,)), ((), ())), "tn": (((0,), (0,)), ((), ()))}


def matmul(a, b, mode, out_dtype, name, tm, tn, tk, res=None, mnk=None, b_spec=None, o_spec=None, out_shape=None, into=None):
    if mnk is not None:
        m, n, k = mnk
    elif mode == "nn":
        (m, k), (k2, n) = a.shape, b.shape
    elif mode == "nt":
        (m, k), (n, k2) = a.shape, b.shape
    else:
        (k, m), (k2, n) = a.shape, b.shape
    assert m % tm == 0 and n % tn == 0 and k % tk == 0, (name, a.shape, b.shape)
    nk = k // tk
    a_spec = pl.BlockSpec((tk, tm), lambda i, j, kk: (kk, i)) if mode == "tn" else pl.BlockSpec((tm, tk), lambda i, j, kk: (i, kk))
    if b_spec is None:
        b_spec = pl.BlockSpec((tn, tk), lambda i, j, kk: (j, kk)) if mode == "nt" else pl.BlockSpec((tk, tn), lambda i, j, kk: (kk, j))
    r_spec = pl.BlockSpec((tm, tn), lambda i, j, kk: (i, j))
    if o_spec is None:
        o_spec = r_spec
    dims = _DIMS[mode]
    has_res = res is not None
    n_in = 2 + int(has_res) + int(into is not None)

    def body(*refs):
        a_ref, b_ref = refs[0], refs[1]
        r_ref = refs[2] if has_res else None
        o_ref = refs[n_in]

        def finish(v):
            if has_res:
                v = v + r_ref[...]
            o_ref[...] = v.astype(out_dtype)

        bv = b_ref[...]
        if bv.ndim == 3:
            bv = jnp.concatenate([bv[j] for j in range(bv.shape[0])], axis=1)
        p = lax.dot_general(a_ref[...].astype(BF16), bv.astype(BF16), dims, preferred_element_type=F32)
        if nk == 1:
            finish(p)
        else:
            acc = refs[-1]
            kk = pl.program_id(2)

            @pl.when(kk == 0)
            def _():
                acc[...] = p

            @pl.when(kk > 0)
            def _():
                acc[...] += p

            @pl.when(kk == nk - 1)
            def _():
                finish(acc[...])

    ops = [a, b] + ([res] if has_res else []) + ([into] if into is not None else [])
    specs = [a_spec, b_spec] + ([r_spec] if has_res else []) + ([_ANY] if into is not None else [])
    if into is not None:
        out_shape = jax.ShapeDtypeStruct(into.shape, into.dtype)
    else:
        out_shape = jax.ShapeDtypeStruct((m, n) if out_shape is None else out_shape, out_dtype)
    return pl.pallas_call(
        body, name=name, out_shape=out_shape,
        grid=(m // tm, n // tn, nk), in_specs=specs, out_specs=o_spec,
        scratch_shapes=[pltpu.VMEM((tm, tn), F32)] if nk > 1 else [],
        input_output_aliases={n_in - 1: 0} if into is not None else {},
        compiler_params=_params(("parallel", "parallel", "arbitrary")),
    )(*ops)


ROWS = 256


def _row_spec(cols, rows=ROWS):
    return pl.BlockSpec((rows, cols), lambda i: (i, 0))


def _fix_spec(r, cols):
    return pl.BlockSpec((r, cols), lambda i: (0, 0))


def rmsnorm_fwd(x, g, name):
    s, d = x.shape

    def body(x_ref, g_ref, h_ref):
        xv = x_ref[...]
        rstd = lax.rsqrt(jnp.mean(xv * xv, axis=-1, keepdims=True) + RMS_EPS)
        h_ref[...] = (xv * rstd * g_ref[...]).astype(BF16)

    return pl.pallas_call(
        body, name=name, out_shape=jax.ShapeDtypeStruct((s, d), BF16), grid=(s // ROWS,),
        in_specs=[_row_spec(d), _fix_spec(1, d)], out_specs=_row_spec(d), compiler_params=_params(("parallel",)),
    )(x, g)


def _rms_bwd_math(xv, gv, dh):
    rstd = lax.rsqrt(jnp.mean(xv * xv, axis=-1, keepdims=True) + RMS_EPS)
    xhat = xv * rstd
    u = dh * gv
    dx = rstd * (u - xhat * jnp.mean(u * xhat, axis=-1, keepdims=True))
    return dx, dh * xhat


def rmsnorm_bwd(x, g, dh, dres, name):
    s, d = x.shape

    def body(x_ref, g_ref, dh_ref, dres_ref, dx_ref, dg_ref):
        dx, dgt = _rms_bwd_math(x_ref[...], g_ref[...], dh_ref[...])
        dx_ref[...] = dres_ref[...] + dx
        part = jnp.sum(dgt, axis=0, keepdims=True)

        @pl.when(pl.program_id(0) == 0)
        def _():
            dg_ref[...] = part

        @pl.when(pl.program_id(0) > 0)
        def _():
            dg_ref[...] += part

    return pl.pallas_call(
        body, name=name, out_shape=(jax.ShapeDtypeStruct((s, d), F32), jax.ShapeDtypeStruct((1, d), F32)),
        grid=(s // ROWS,), in_specs=[_row_spec(d), _fix_spec(1, d), _row_spec(d), _row_spec(d)],
        out_specs=(_row_spec(d), _fix_spec(1, d)), compiler_params=_params(("arbitrary",)),
    )(x, g, dh, dres)


def loss_head(x, g, target, name):
    s, d = x.shape

    def body(x_ref, g_ref, t_ref, dx_ref, dg_ref, loss_ref):
        xv, gv = x_ref[...], g_ref[...]
        rstd = lax.rsqrt(jnp.mean(xv * xv, axis=-1, keepdims=True) + RMS_EPS)
        err = xv * rstd * gv - t_ref[...]
        dx, dgt = _rms_bwd_math(xv, gv, err * (1.0 / d))
        dx_ref[...] = dx
        part = jnp.sum(dgt, axis=0, keepdims=True)
        lpart = jnp.full((1, LANES), 0.5 / d, F32) * jnp.sum(err * err)

        @pl.when(pl.program_id(0) == 0)
        def _():
            dg_ref[...] = part
            loss_ref[...] = lpart

        @pl.when(pl.program_id(0) > 0)
        def _():
            dg_ref[...] += part
            loss_ref[...] += lpart

    return pl.pallas_call(
        body, name=name,
        out_shape=(jax.ShapeDtypeStruct((s, d), F32), jax.ShapeDtypeStruct((1, d), F32), jax.ShapeDtypeStruct((1, LANES), F32)),
        grid=(s // ROWS,), in_specs=[_row_spec(d), _fix_spec(1, d), _row_spec(d)],
        out_specs=(_row_spec(d), _fix_spec(1, d), _fix_spec(1, LANES)), compiler_params=_params(("arbitrary",)),
    )(x, g, target)


def swiglu_fwd(gu, name):
    s, f2 = gu.shape
    f = f2 // 2

    def body(gu_ref, a_ref):
        gv, uv = gu_ref[:, :f].astype(F32), gu_ref[:, f:].astype(F32)
        a_ref[...] = (gv * (1.0 / (1.0 + jnp.exp(-gv))) * uv).astype(BF16)

    return pl.pallas_call(
        body, name=name, out_shape=jax.ShapeDtypeStruct((s, f), BF16), grid=(s // ROWS,),
        in_specs=[_row_spec(f2)], out_specs=_row_spec(f), compiler_params=_params(("parallel",)),
    )(gu)


def swiglu_bwd(gu, dact, name):
    s, f2 = gu.shape
    f = f2 // 2

    def body(gu_ref, da_ref, o_ref):
        gv, uv, da = gu_ref[:, :f].astype(F32), gu_ref[:, f:].astype(F32), da_ref[...].astype(F32)
        sg = 1.0 / (1.0 + jnp.exp(-gv))
        o_ref[:, :f] = (da * uv * sg * (1.0 + gv * (1.0 - sg))).astype(BF16)
        o_ref[:, f:] = (da * gv * sg).astype(BF16)

    return pl.pallas_call(
        body, name=name, out_shape=jax.ShapeDtypeStruct((s, f2), BF16), grid=(s // ROWS,),
        in_specs=[_row_spec(f2), _row_spec(f)], out_specs=_row_spec(f2), compiler_params=_params(("parallel",)),
    )(gu, dact)


Q_OFF, K_OFF, V_OFF = 0, 8, 16


KB = 512
BQ = 256
SUB = KB // BLK


def _softplus_parts(z):
    sp = jnp.log(1.0 + jnp.exp(-jnp.abs(z)))
    ls = jnp.minimum(z, 0.0) - sp
    return ls, ls - z


def _wide(t):
    return jnp.concatenate([t] * SUB, axis=1)


def _chunk_dots(x, tri):
    terms = []
    for u in range(SUB):
        xu = x[:, u * BLK:(u + 1) * BLK]
        hi = xu.astype(BF16)
        terms += [hi, (xu - hi.astype(F32)).astype(BF16)]
    r = _dot(jnp.concatenate(terms, axis=0), tri)
    rows = x.shape[0]
    piece = lambda n: r[n * rows:(n + 1) * rows]
    return [piece(2 * u) + piece(2 * u + 1) for u in range(SUB)]


def _block_suffix_sums(x, suffix, c):
    loc = _chunk_dots(x, suffix)
    out = [None] * SUB
    for u in reversed(range(SUB)):
        out[u] = loc[u] + c
        c = c + jnp.sum(x[:, u * BLK:(u + 1) * BLK], axis=1, keepdims=True)
    return jnp.concatenate(out, axis=1), c


def _block_prefix_sums(x, tri, c):
    loc = _chunk_dots(x, tri)
    out = []
    for u in range(SUB):
        out.append(loc[u] + c)
        c = c + jnp.sum(x[:, u * BLK:(u + 1) * BLK], axis=1, keepdims=True)
    return jnp.concatenate(out, axis=1), c


def causal_fwd(qkv, npairs, mode, name, fq=None, fk=None, rider=None):
    s = qkv.shape[0]
    nq = s // BQ
    fox = mode == "fox"

    def body(*refs):
        if fox:
            q_ref, k_ref, v_ref, fq_ref, fk_ref, o_ref, st_ref = refs
        else:
            q_ref, k_ref, v_ref, o_ref, st_ref = refs
        i = pl.program_id(1)
        nkb = (i * BQ + BQ - 1) // KB + 1
        row, lane = _iotas((BQ, KB))
        row_s, lane_s = _iotas()
        _, lane_q = _iotas((BQ, LANES))
        nfull = (i * BQ) // KB
        qpos = i * BQ + row
        qf = q_ref[...].astype(F32) * 0.125
        hms = (lane_q < HEAD_DIM, lane_q >= HEAD_DIM)
        qas = [jnp.where(hm, qf, 0.0).astype(BF16) for hm in hms]
        suffix = jnp.where(row_s > lane_s, 1.0, 0.0).astype(BF16)
        zero = jnp.zeros((BQ, LANES), F32)
        col0 = jnp.zeros((BQ, 1), F32)

        def kv(j):
            r0 = pl.multiple_of(j * KB, KB)
            return r0, k_ref[pl.ds(r0, KB), :], v_ref[pl.ds(r0, KB), :]

        if fox:
            fqs = [_wide(fq_ref[a]) for a in range(2)]

            def step(j, carry, masked):
                r0, kb, vb = kv(j)
                new = []
                for a in range(2):
                    acc, mx, l = carry[3 * a:3 * a + 3]
                    z = _dot_nt(qas[a], kb) + fqs[a] - fk_ref[a:a + 1, pl.ds(r0, KB)]
                    if masked:
                        z = jnp.where(r0 + lane <= qpos, z, NEG)
                    mnew = jnp.maximum(mx, jnp.max(z, axis=1, keepdims=True))
                    p = jnp.exp(z - mnew)
                    alpha = jnp.exp(mx - mnew)
                    new += [alpha * acc + _dot(p.astype(BF16), vb), mnew, alpha * l + jnp.sum(p, axis=1, keepdims=True)]
                return tuple(new)

            neg = jnp.full((BQ, 1), NEG, F32)
            res = lax.fori_loop(0, nfull, functools.partial(step, masked=False), (zero, neg, col0, zero, neg, col0))
            res = lax.fori_loop(nfull, nkb, functools.partial(step, masked=True), res)
            outs = [res[3 * a] / res[3 * a + 2] for a in range(2)]
            stats = [res[3 * a + 1] + jnp.log(res[3 * a + 2]) for a in range(2)]
        else:
            def step(j, carry, masked):
                r0, kb, vb = kv(j)
                strict = r0 + lane < qpos
                new = []
                for a in range(2):
                    acc, c = carry[2 * a:2 * a + 2]
                    ls, lm = _softplus_parts(_dot_nt(qas[a], kb))
                    if masked:
                        lm = jnp.where(strict, lm, 0.0)
                    between, c = _block_suffix_sums(lm, suffix, c)
                    aw = jnp.exp(ls + between)
                    if masked:
                        aw = jnp.where(strict, aw, 0.0)
                    new += [acc + _dot(aw.astype(BF16), vb), c]
                return tuple(new)

            res = lax.fori_loop(0, nkb - nfull, lambda jj, c: step(nkb - 1 - jj, c, True), (zero, col0, zero, col0))
            res = lax.fori_loop(0, nfull, lambda jj, c: step(nfull - 1 - jj, c, False), res)
            outs, stats = [res[0], res[2]], [res[1], res[3]]
        o_ref[...] = jnp.where(hms[0], outs[0], outs[1])
        for a in range(2):
            st_ref[a] = jnp.broadcast_to(stats[a], (BQ, LANES))

    col = lambda off: (lambda p, i: (0, off + p))
    in_specs = [pl.BlockSpec((BQ, LANES), lambda p, i: (i, Q_OFF + p)),
                pl.BlockSpec((s, LANES), col(K_OFF)), pl.BlockSpec((s, LANES), col(V_OFF))]
    ops = [qkv, qkv, qkv]
    if fox:
        in_specs += [pl.BlockSpec((2, BQ, LANES), lambda p, i: (p, i, 0)), pl.BlockSpec((None, 2, s), lambda p, i: (p, 0, 0))]
        ops += [fq, fk]
    (o, stat), rode = call_with_rider(
        body, name, rider, ops, in_specs,
        [jax.ShapeDtypeStruct((s, npairs * LANES), F32), jax.ShapeDtypeStruct((2 * npairs, s, LANES), F32)],
        [pl.BlockSpec((BQ, LANES), lambda p, i: (i, p)), pl.BlockSpec((2, BQ, LANES), lambda p, i: (p, i, 0))], [], (npairs, nq))
    return o, stat, rode


def causal_bwd(qkv, do, stat, npairs, mode, name, fq=None, fk=None, rider=None):
    s = qkv.shape[0]
    nq = s // BQ
    fox = mode == "fox"

    def body(*refs):
        if fox:
            q_ref, k_ref, v_ref, do_ref, st_ref, fq_ref, fk_ref, dq_ref, dk_ref, dv_ref, df_ref, p_s, dp_s = refs
        else:
            q_ref, k_ref, v_ref, do_ref, st_ref, dq_ref, dk_ref, dv_ref = refs
        i = pl.program_id(1)

        @pl.when(i == 0)
        def _():
            dk_ref[...] = jnp.zeros_like(dk_ref)
            dv_ref[...] = jnp.zeros_like(dv_ref)
            if fox:
                df_ref[...] = jnp.zeros_like(df_ref)

        nkb = (i * BQ + BQ - 1) // KB + 1
        nfull = (i * BQ) // KB
        row, lane = _iotas((BQ, KB))
        row_s, lane_s = _iotas()
        _, lane_q = _iotas((BQ, LANES))
        qpos = i * BQ + row
        qf = q_ref[...].astype(F32) * 0.125
        dov = do_ref[...]
        hms = (lane_q < HEAD_DIM, lane_q >= HEAD_DIM)
        qas = [jnp.where(hm, qf, 0.0).astype(BF16) for hm in hms]
        doas = [jnp.where(hm, dov, 0.0).astype(BF16) for hm in hms]
        stas = [_wide(st_ref[a]) for a in range(2)]
        zero = jnp.zeros((BQ, LANES), F32)
        col0 = jnp.zeros((BQ, 1), F32)

        def kv(j):
            r0 = pl.multiple_of(j * KB, KB)
            return r0, k_ref[pl.ds(r0, KB), :], v_ref[pl.ds(r0, KB), :]

        if fox:
            fqs = [_wide(fq_ref[a]) for a in range(2)]

            def probs(j, deltas, masked):
                r0, kb, vb = kv(j)
                new = []
                for a in range(2):
                    z = _dot_nt(qas[a], kb) + fqs[a] - fk_ref[a:a + 1, pl.ds(r0, KB)]
                    p = jnp.exp(z - stas[a])
                    if masked:
                        p = jnp.where(r0 + lane <= qpos, p, 0.0)
                    dp = _dot_nt(doas[a], vb)
                    p_s[a, j] = p
                    dp_s[a, j] = dp
                    new.append(deltas[a] + jnp.sum(p * dp, axis=1, keepdims=True))
                return tuple(new)

            deltas = lax.fori_loop(0, nfull, functools.partial(probs, masked=False), (col0, col0))
            deltas = lax.fori_loop(nfull, nkb, functools.partial(probs, masked=True), deltas)

            def step(j, dqs):
                r0, kb, _ = kv(j)
                new = []
                dk = jnp.zeros((KB, LANES), F32)
                dv = jnp.zeros((KB, LANES), F32)
                for a in range(2):
                    p = p_s[a, j]
                    ds = p * (dp_s[a, j] - deltas[a])
                    dsb = ds.astype(BF16)
                    dk += _dot_tn(dsb, qas[a])
                    dv += _dot_tn(p.astype(BF16), doas[a])
                    df_ref[a:a + 1, pl.ds(r0, KB)] -= jnp.sum(ds, axis=0, keepdims=True)
                    new.append(dqs[a] + _dot(dsb, kb))
                dk_ref[pl.ds(r0, KB), :] += dk
                dv_ref[pl.ds(r0, KB), :] += dv
                return tuple(new)

            dqs = lax.fori_loop(0, nkb, step, (zero, zero))
        else:
            incl = jnp.where(row_s <= lane_s, 1.0, 0.0).astype(BF16)
            excl = jnp.where(row_s < lane_s, 1.0, 0.0).astype(BF16)

            def step(j, carry, masked):
                r0, kb, vb = kv(j)
                strict = r0 + lane < qpos
                new = []
                dk = jnp.zeros((KB, LANES), F32)
                dv = jnp.zeros((KB, LANES), F32)
                for a in range(2):
                    dq, cm, cg = carry[3 * a:3 * a + 3]
                    ls, lm = _softplus_parts(_dot_nt(qas[a], kb))
                    if masked:
                        lm = jnp.where(strict, lm, 0.0)
                    beta = jnp.exp(ls)
                    upto, cm = _block_prefix_sums(lm, incl, cm)
                    aw = jnp.exp(ls + stas[a] - upto)
                    if masked:
                        aw = jnp.where(strict, aw, 0.0)
                    g = aw * _dot_nt(doas[a], vb)
                    pre, cg = _block_prefix_sums(g, excl, cg)
                    dz = g * (1.0 - beta) - pre * beta
                    if masked:
                        dz = jnp.where(strict, dz, 0.0)
                    dzb = dz.astype(BF16)
                    dk += _dot_tn(dzb, qas[a])
                    dv += _dot_tn(aw.astype(BF16), doas[a])
                    new += [dq + _dot(dzb, kb), cm, cg]
                dk_ref[pl.ds(r0, KB), :] += dk
                dv_ref[pl.ds(r0, KB), :] += dv
                return tuple(new)

            res = lax.fori_loop(0, nfull, functools.partial(step, masked=False), (zero, col0, col0, zero, col0, col0))
            res = lax.fori_loop(nfull, nkb, functools.partial(step, masked=True), res)
            dqs = (res[0], res[3])
        dq_ref[...] = (jnp.where(hms[0], dqs[0], dqs[1]) * 0.125).astype(BF16)

    col = lambda off: (lambda p, i: (0, off + p))
    blk = pl.BlockSpec((BQ, LANES), lambda p, i: (i, p))
    acc = pl.BlockSpec((s, LANES), lambda p, i: (0, p))
    st_spec = pl.BlockSpec((2, BQ, LANES), lambda p, i: (p, i, 0))
    in_specs = [pl.BlockSpec((BQ, LANES), lambda p, i: (i, Q_OFF + p)), pl.BlockSpec((s, LANES), col(K_OFF)),
                pl.BlockSpec((s, LANES), col(V_OFF)), blk, st_spec]
    ops = [qkv, qkv, qkv, do, stat]
    w = npairs * LANES
    out_shape = [jax.ShapeDtypeStruct((s, w), BF16), jax.ShapeDtypeStruct((s, w), F32), jax.ShapeDtypeStruct((s, w), F32)]
    out_specs = [blk, acc, acc]
    scratch = []
    if fox:
        fk_spec = pl.BlockSpec((None, 2, s), lambda p, i: (p, 0, 0))
        in_specs += [st_spec, fk_spec]
        ops += [fq, fk]
        out_shape.append(jax.ShapeDtypeStruct((npairs, 2, s), F32))
        out_specs.append(fk_spec)
        scratch = [pltpu.VMEM((2, s // KB, BQ, KB), F32)] * 2
    outs, rode = call_with_rider(body, name, rider, ops, in_specs, out_shape, out_specs, scratch, (npairs, nq))
    return (*outs, rode)


def forget_fwd(fl, bias, name):
    s = fl.shape[0]

    def body(fl_ref, b_ref, f_ref):
        row, lane = _iotas()
        lower = jnp.where(lane <= row, 1.0, 0.0).astype(BF16)

        def step(n, carry):
            r0 = pl.multiple_of(n * BLK, BLK)
            ls, _ = _softplus_parts(fl_ref[pl.ds(r0, BLK), :] + b_ref[...])
            blk = _dot_exact_rhs(lower, ls) + carry
            f_ref[pl.ds(r0, BLK), :] = blk
            return blk[BLK - 1:BLK, :]

        lax.fori_loop(0, s // BLK, step, jnp.zeros((1, LANES), F32))

    return pl.pallas_call(
        body, name=name, out_shape=jax.ShapeDtypeStruct((s, LANES), F32),
        in_specs=[pl.BlockSpec(memory_space=pltpu.VMEM)] * 2, out_specs=pl.BlockSpec(memory_space=pltpu.VMEM),
        compiler_params=_params(),
    )(fl, bias)


def forget_bwd(fl, bias, df, name):
    s = fl.shape[0]
    nb = s // BLK

    def body(fl_ref, b_ref, df_ref, o_ref, db_ref):
        row, lane = _iotas()
        upper = jnp.where(lane >= row, 1.0, 0.0).astype(BF16)

        def step(nn, carry):
            tail, db = carry
            r0 = pl.multiple_of((nb - 1 - nn) * BLK, BLK)
            dls = _dot_exact_rhs(upper, df_ref[pl.ds(r0, BLK), :]) + tail
            xv = fl_ref[pl.ds(r0, BLK), :] + b_ref[...]
            dfl = dls * (1.0 / (1.0 + jnp.exp(xv)))
            o_ref[pl.ds(r0, BLK), :] = dfl
            return dls[0:1, :], db + jnp.sum(dfl, axis=0, keepdims=True)

        _, db = lax.fori_loop(0, nb, step, (jnp.zeros((1, LANES), F32), jnp.zeros((1, LANES), F32)))
        db_ref[...] = db

    return pl.pallas_call(
        body, name=name, out_shape=(jax.ShapeDtypeStruct((s, LANES), F32), jax.ShapeDtypeStruct((1, LANES), F32)),
        in_specs=[pl.BlockSpec(memory_space=pltpu.VMEM)] * 3,
        out_specs=(pl.BlockSpec(memory_space=pltpu.VMEM), pl.BlockSpec(memory_space=pltpu.VMEM)),
        compiler_params=_params(),
    )(fl, bias, df)


def _rot_tables(s):
    inv = ROPE_THETA ** (-jnp.arange(ROT_HALF, dtype=F32) * 2.0 / (2 * ROT_HALF))
    ang = jnp.arange(s, dtype=F32)[:, None] * inv[None, :]
    cos, sin = jnp.cos(ang), jnp.sin(ang)
    z8 = jnp.zeros((s, ROT_HALF), F32)
    rest = HEAD_DIM - 2 * ROT_HALF
    zr, onr = jnp.zeros((s, rest), F32), jnp.ones((s, rest), F32)
    tile = lambda t: jnp.tile(t, (1, 2))
    return tile(jnp.concatenate([cos, cos, onr], 1)), tile(jnp.concatenate([-sin, z8, zr], 1)), tile(jnp.concatenate([z8, sin, zr], 1))


def rotary_prep(qkv, tables, name):
    s = qkv.shape[0]
    w = 4 * LANES

    def body(q_ref, k_ref, v_ref, c_ref, s1_ref, s2_ref, qo_ref, ko_ref, vo_ref):
        c, s1, s2 = c_ref[...], s1_ref[...], s2_ref[...]

        def rot(xv):
            return xv * c + pltpu.roll(xv, LANES - ROT_HALF, 1) * s1 + pltpu.roll(xv, ROT_HALF, 1) * s2

        qo_ref[...] = rot(q_ref[...].astype(F32)) * 0.125
        ko_ref[...] = rot(k_ref[...].astype(F32))
        vo_ref[...] = v_ref[...].astype(F32)

    cb = lambda off: pl.BlockSpec((ROWS, LANES), lambda i, j: (i, off + j))
    tb = pl.BlockSpec((ROWS, LANES), lambda i, j: (i, 0))
    out = jax.ShapeDtypeStruct((s, w), F32)
    return pl.pallas_call(
        body, name=name, out_shape=(out, out, out), grid=(s // ROWS, 4),
        in_specs=[cb(Q_OFF + 4), cb(K_OFF + 4), cb(V_OFF + 4), tb, tb, tb], out_specs=(cb(0), cb(0), cb(0)),
        compiler_params=_params(("parallel", "parallel")),
    )(qkv, qkv, qkv, *tables)


def rotary_bwd(dq, dk, dv, tables, name):
    s, w = dq.shape

    def body(dq_ref, dk_ref, dv_ref, c_ref, s1_ref, s2_ref, qo_ref, ko_ref, vo_ref):
        c, s1, s2 = c_ref[...], s1_ref[...], s2_ref[...]

        def rot_t(dy):
            return dy * c + pltpu.roll(dy * s1, ROT_HALF, 1) + pltpu.roll(dy * s2, LANES - ROT_HALF, 1)

        qo_ref[...] = (rot_t(dq_ref[...]) * 0.125).astype(BF16)
        ko_ref[...] = rot_t(dk_ref[...]).astype(BF16)
        vo_ref[...] = dv_ref[...].astype(BF16)

    cb = pl.BlockSpec((ROWS, LANES), lambda i, j: (i, j))
    tb = pl.BlockSpec((ROWS, LANES), lambda i, j: (i, 0))
    out = jax.ShapeDtypeStruct((s, w), BF16)
    return pl.pallas_call(
        body, name=name, out_shape=(out, out, out), grid=(s // ROWS, w // LANES),
        in_specs=[cb, cb, cb, tb, tb, tb], out_specs=(cb, cb, cb), compiler_params=_params(("parallel", "parallel")),
    )(dq, dk, dv, *tables)


def _deinterleave(dst, src_ref, stride, s, dtype):
    length = s // stride
    for r in range(stride):
        if stride == 1:
            dst[...] = src_ref[...].astype(dtype)
        else:
            dst[r * length:(r + 1) * length, :] = src_ref[pl.ds(r, length, stride=stride), :].astype(dtype)


def _band_masks(row, lane, first):
    return lane <= row, lane >= row + jnp.where(first, BLK, 0)


def dilated_fwd(qd, kd, vd, name, rider=None):
    s, w = qd.shape
    npairs = w // LANES
    nblk = s // BLK

    def body(q_ref, k_ref, v_ref, o_ref, lse_ref, qs, ks, vs, od, ld, on, ln):
        row, lane = _iotas()
        for pi, stride in enumerate(DIL_STRIDES):
            per = (s // stride) // BLK
            _deinterleave(qs, q_ref, stride, s, BF16)
            _deinterleave(ks, k_ref, stride, s, BF16)
            _deinterleave(vs, v_ref, stride, s, BF16)

            def block(b, carry):
                r0 = pl.multiple_of(b * BLK, BLK)
                rp = pl.multiple_of(jnp.maximum(b - 1, 0) * BLK, BLK)
                mc, mp = _band_masks(row, lane, b % per == 0)
                q = qs[pl.ds(r0, BLK), :]
                kc, kp, vc, vp = ks[pl.ds(r0, BLK), :], ks[pl.ds(rp, BLK), :], vs[pl.ds(r0, BLK), :], vs[pl.ds(rp, BLK), :]
                out = jnp.zeros((BLK, LANES), F32)
                lse = jnp.zeros((BLK, LANES), F32)
                for a in range(2):
                    hm = (lane < HEAD_DIM) if a == 0 else (lane >= HEAD_DIM)
                    qa = jnp.where(hm, q.astype(F32), 0.0).astype(BF16)
                    sc = jnp.where(mc, _dot_nt(qa, kc), NEG)
                    sp = jnp.where(mp, _dot_nt(qa, kp), NEG)
                    mx = jnp.maximum(jnp.max(sc, axis=1, keepdims=True), jnp.max(sp, axis=1, keepdims=True))
                    pc, pp = jnp.exp(sc - mx), jnp.exp(sp - mx)
                    l = jnp.sum(pc, axis=1, keepdims=True) + jnp.sum(pp, axis=1, keepdims=True)
                    oa = (_dot(pc.astype(BF16), vc) + _dot(pp.astype(BF16), vp)) / l
                    out = jnp.where(hm, oa, out)
                    lse = jnp.where(hm, mx + jnp.log(l), lse)
                od[pl.ds(r0, BLK), :] = out
                ld[pl.ds(r0, BLK), :] = lse
                return carry

            lax.fori_loop(0, nblk, block, 0, unroll=2)
            length = s // stride
            for r in range(stride):
                if stride == 1:
                    on[pi] = od[...]
                    ln[pi] = ld[...]
                else:
                    on[pi, pl.ds(r, length, stride=stride), :] = od[r * length:(r + 1) * length, :]
                    ln[pi, pl.ds(r, length, stride=stride), :] = ld[r * length:(r + 1) * length, :]

        def merge(n, carry):
            r0 = pl.multiple_of(n * BLK, BLK)
            ls = [ln[pi, pl.ds(r0, BLK), :] for pi in range(3)]
            mx = jnp.maximum(jnp.maximum(ls[0], ls[1]), ls[2])
            ws = [jnp.exp(lv - mx) for lv in ls]
            den = ws[0] + ws[1] + ws[2]
            num = ws[0] * on[0, pl.ds(r0, BLK), :] + ws[1] * on[1, pl.ds(r0, BLK), :] + ws[2] * on[2, pl.ds(r0, BLK), :]
            o_ref[pl.ds(r0, BLK), :] = num / den
            lse_ref[pl.ds(r0, BLK), :] = mx + jnp.log(den)
            return carry

        lax.fori_loop(0, nblk, merge, 0, unroll=2)

    colspec = pl.BlockSpec((s, LANES), lambda p: (0, p))
    out = jax.ShapeDtypeStruct((s, w), F32)
    scratch = [pltpu.VMEM((s, LANES), BF16)] * 3 + [pltpu.VMEM((s, LANES), F32)] * 2 + [pltpu.VMEM((3, s, LANES), F32)] * 2
    (o, lse), rode = call_with_rider(body, name, rider, [qd, kd, vd], [colspec] * 3, [out, out], [colspec, colspec], scratch, (npairs,))
    return o, lse, rode


def dilated_bwd(qd, kd, vd, do, out, lse, do_off, name):
    s, w = qd.shape
    npairs = w // LANES
    nblk = s // BLK

    def body(q_ref, k_ref, v_ref, do_ref, out_ref, lse_ref, dq_ref, dk_ref, dv_ref, qs, ks, vs, dos, dls, lss, dqd, dkd, dvd, dln):
        row, lane = _iotas()
        same_head = jnp.where((row < HEAD_DIM) == (lane < HEAD_DIM), 1.0, 0.0).astype(BF16)

        def delta_blk(n, carry):
            r0 = pl.multiple_of(n * BLK, BLK)
            dln[pl.ds(r0, BLK), :] = _dot_exact_lhs(do_ref[pl.ds(r0, BLK), :] * out_ref[pl.ds(r0, BLK), :], same_head)
            return carry

        lax.fori_loop(0, nblk, delta_blk, 0, unroll=2)
        for pi, stride in enumerate(DIL_STRIDES):
            per = (s // stride) // BLK
            _deinterleave(qs, q_ref, stride, s, BF16)
            _deinterleave(ks, k_ref, stride, s, BF16)
            _deinterleave(vs, v_ref, stride, s, BF16)
            _deinterleave(dos, do_ref, stride, s, BF16)
            _deinterleave(dls, dln, stride, s, F32)
            _deinterleave(lss, lse_ref, stride, s, F32)

            def block(b, carry):
                r0 = pl.multiple_of(b * BLK, BLK)
                rp = pl.multiple_of(jnp.maximum(b - 1, 0) * BLK, BLK)
                first = b % per == 0
                mc, mp = _band_masks(row, lane, first)
                q, dov = qs[pl.ds(r0, BLK), :], dos[pl.ds(r0, BLK), :]
                kc, kp, vc, vp = ks[pl.ds(r0, BLK), :], ks[pl.ds(rp, BLK), :], vs[pl.ds(r0, BLK), :], vs[pl.ds(rp, BLK), :]
                lse_t, dl_t = lss[pl.ds(r0, BLK), :], dls[pl.ds(r0, BLK), :]
                dq = jnp.zeros((BLK, LANES), F32)
                dkc = jnp.zeros((BLK, LANES), F32)
                dkp = jnp.zeros((BLK, LANES), F32)
                dvc = jnp.zeros((BLK, LANES), F32)
                dvp = jnp.zeros((BLK, LANES), F32)
                for a in range(2):
                    hm = (lane < HEAD_DIM) if a == 0 else (lane >= HEAD_DIM)
                    pick = lane == a * HEAD_DIM
                    qa = jnp.where(hm, q.astype(F32), 0.0).astype(BF16)
                    doa = jnp.where(hm, dov.astype(F32), 0.0).astype(BF16)
                    lse_a = jnp.sum(jnp.where(pick, lse_t, 0.0), axis=1, keepdims=True)
                    dl_a = jnp.sum(jnp.where(pick, dl_t, 0.0), axis=1, keepdims=True)
                    pc = jnp.where(mc, jnp.exp(_dot_nt(qa, kc) - lse_a), 0.0)
                    pp = jnp.where(mp, jnp.exp(_dot_nt(qa, kp) - lse_a), 0.0)
                    dsc = (pc * (_dot_nt(doa, vc) - dl_a)).astype(BF16)
                    dsp = (pp * (_dot_nt(doa, vp) - dl_a)).astype(BF16)
                    dq = jnp.where(hm, _dot(dsc, kc) + _dot(dsp, kp), dq)
                    dkc += _dot_tn(dsc, qa)
                    dkp += _dot_tn(dsp, qa)
                    dvc += _dot_tn(pc.astype(BF16), doa)
                    dvp += _dot_tn(pp.astype(BF16), doa)
                dqd[pl.ds(r0, BLK), :] = dq
                dkd[pl.ds(r0, BLK), :] = dkc
                dvd[pl.ds(r0, BLK), :] = dvc

                @pl.when(jnp.logical_not(first))
                def _():
                    dkd[pl.ds(rp, BLK), :] += dkp
                    dvd[pl.ds(rp, BLK), :] += dvp

                return carry

            lax.fori_loop(0, nblk, block, 0, unroll=2)
            length = s // stride
            for dst, src in ((dq_ref, dqd), (dk_ref, dkd), (dv_ref, dvd)):
                for r in range(stride):
                    if stride == 1:
                        dst[...] = src[...]
                    else:
                        dst[pl.ds(r, length, stride=stride), :] += src[r * length:(r + 1) * length, :]

    colspec = pl.BlockSpec((s, LANES), lambda p: (0, p))
    do_spec = pl.BlockSpec((s, LANES), lambda p: (0, do_off + p))
    o3 = jax.ShapeDtypeStruct((s, w), F32)
    return pl.pallas_call(
        body, name=name, out_shape=(o3, o3, o3), grid=(npairs,),
        in_specs=[colspec, colspec, colspec, do_spec, colspec, colspec], out_specs=(colspec, colspec, colspec),
        scratch_shapes=[pltpu.VMEM((s, LANES), BF16)] * 4 + [pltpu.VMEM((s, LANES), F32)] * 6,
        compiler_params=_params(("parallel",)),
    )(qd, kd, vd, do, out, lse)


def adamw(w, g, m, v, name):
    rows, cols = w.shape
    rb = min(rows, ROWS)
    c1 = 1.0 - ADAM_B1 ** ADAM_STEP
    c2 = 1.0 - ADAM_B2 ** ADAM_STEP

    def body(w_ref, g_ref, m_ref, v_ref, d_ref, mo_ref, vo_ref):
        gv = g_ref[...]
        mn = ADAM_B1 * m_ref[...] + (1.0 - ADAM_B1) * gv
        vn = ADAM_B2 * v_ref[...] + (1.0 - ADAM_B2) * (gv * gv)
        d_ref[...] = -ADAM_LR * ((mn / c1) / (jnp.sqrt(vn / c2) + ADAM_EPS) + ADAM_WD * w_ref[...])
        mo_ref[...] = mn
        vo_ref[...] = vn

    spec = _row_spec(cols, rb)
    out = jax.ShapeDtypeStruct((rows, cols), F32)
    return pl.pallas_call(
        body, name=name, out_shape=(out, out, out), grid=(rows // rb,), in_specs=[spec] * 4, out_specs=(spec,) * 3,
        compiler_params=_params(("parallel",)),
    )(w, g, m, v)


def _prefetch_call(body, name, scalar, ops, grid, in_specs, out_specs, out_shape, sem):
    spec = pltpu.PrefetchScalarGridSpec(num_scalar_prefetch=1, grid=grid, in_specs=in_specs, out_specs=out_specs)
    return pl.pallas_call(body, name=name, grid_spec=spec, out_shape=out_shape, compiler_params=_params(sem))(scalar, *ops)


def pair_sum(g, got, core, name):
    nc, r, c = g.shape
    rh = r // 2

    def body(core_ref, g_ref, got_ref, o_ref):
        o_ref[...] = (g_ref[...].astype(F32) + got_ref[...].astype(F32)).astype(BF16)

    blk = lambda rows_of: pl.BlockSpec((None, rh, c), rows_of)
    return _prefetch_call(
        body, name, core, (g, got), (nc,),
        [blk(lambda j, core_ref: (j, core_ref[0], 0)), blk(lambda j, core_ref: (j, 0, 0))],
        blk(lambda j, core_ref: (j, 0, 0)), jax.ShapeDtypeStruct((nc, rh, c), BF16), ("parallel",))


def chip_sum(pair, got, chip, layer, into, name):
    _, rh, c = pair.shape

    def body(chip_ref, p_ref, a_ref, b_ref, c_ref, old_ref, o_ref):
        o_ref[...] = ((p_ref[...].astype(F32) + a_ref[...].astype(F32)) + b_ref[...].astype(F32)) + c_ref[...].astype(F32)

    arrival = lambda k: pl.BlockSpec((None, rh, c), lambda i, chip_ref: (k, 0, 0))
    spec = pltpu.PrefetchScalarGridSpec(
        num_scalar_prefetch=1, grid=(1,),
        in_specs=[pl.BlockSpec((None, rh, c), lambda i, chip_ref: (chip_ref[0], 0, 0)), arrival(0), arrival(1), arrival(2), _ANY],
        out_specs=pl.BlockSpec((None, rh, c), lambda i, chip_ref: (layer, 0, 0)))
    return pl.pallas_call(body, name=name, grid_spec=spec, out_shape=jax.ShapeDtypeStruct(into.shape, into.dtype),
                          input_output_aliases={5: 0}, compiler_params=_params(("arbitrary",)))(chip, pair, got, got, got, into)


def adamw_family(w, m, v, g_mine, g_other, core, name):
    nl, r, c = w.shape
    rh = r // 2
    nb = 4 if rh % 512 == 0 else (2 if rh % 16 == 0 and rh > 256 else 1)
    rb = rh // nb
    c1 = 1.0 - ADAM_B1 ** ADAM_STEP
    c2 = 1.0 - ADAM_B2 ** ADAM_STEP

    def body(core_ref, w_ref, m_ref, v_ref, gm_ref, go_ref, g_ref, d_ref, mo_ref, vo_ref):
        gv = jnp.where(pl.program_id(1) == core_ref[0], gm_ref[...], go_ref[...])
        mn = ADAM_B1 * m_ref[...] + (1.0 - ADAM_B1) * gv
        vn = ADAM_B2 * v_ref[...] + (1.0 - ADAM_B2) * (gv * gv)
        g_ref[...] = gv
        d_ref[...] = -ADAM_LR * ((mn / c1) / (jnp.sqrt(vn / c2) + ADAM_EPS) + ADAM_WD * w_ref[...])
        mo_ref[...] = mn
        vo_ref[...] = vn

    full = pl.BlockSpec((None, rb, c), lambda l, h, i, core_ref: (l, h * nb + i, 0))
    half = pl.BlockSpec((None, rb, c), lambda l, h, i, core_ref: (l, i, 0))
    out = jax.ShapeDtypeStruct((nl, r, c), F32)
    return _prefetch_call(body, name, core, (w, m, v, g_mine, g_other), (nl, 2, nb), [full, full, full, half, half],
                          (full, full, full, full), (out, out, out, out), ("parallel", "parallel", "parallel"))


def _coords():
    return lax.axis_index("x"), lax.axis_index("y"), lax.axis_index("c")


def _other_chips(x, y):
    return ((1 - x, y), (x, 1 - y), (1 - x, 1 - y))


_ANY = pl.BlockSpec(memory_space=pl.ANY)


def _exchange_call(body, name, arrays, out_shapes, n_copies, n_local=0):
    n = len(arrays)

    def wrapped(*refs):
        body(refs[:n], refs[n:n + len(out_shapes)], *refs[n + len(out_shapes):])

    scratch = [pltpu.SemaphoreType.DMA((n_copies,)), pltpu.SemaphoreType.DMA((n_copies,))]
    if n_local:
        scratch.append(pltpu.SemaphoreType.DMA((n_local,)))
    return pl.pallas_call(
        wrapped, name=name, out_shape=tuple(out_shapes), in_specs=[_ANY] * n, out_specs=tuple([_ANY] * len(out_shapes)),
        scratch_shapes=scratch, compiler_params=_params(),
    )(*arrays)


def _remote(send_sems, recv_sems, n, src, dst, to):
    return pltpu.make_async_remote_copy(src_ref=src, dst_ref=dst, send_sem=send_sems.at[n], recv_sem=recv_sems.at[n],
                                        device_id=to, device_id_type=MESH)


class Rider:
    def __init__(self, arrays, out_shapes, n_remote, n_local, copies, then=None):
        self.arrays, self.out_shapes, self.n_remote, self.n_local = list(arrays), list(out_shapes), n_remote, n_local
        self.copies, self.then = copies, then

    def sems(self):
        return [pltpu.SemaphoreType.DMA((self.n_remote,)), pltpu.SemaphoreType.DMA((self.n_remote,)),
                pltpu.SemaphoreType.DMA((max(self.n_local, 1),))]

    def run(self, name):
        n, no = len(self.arrays), len(self.out_shapes)

        def body(*refs):
            for stage in (self.copies, self.then):
                if stage is not None:
                    cps = stage(refs[:n], refs[n:n + no], *refs[n + no:])
                    for cp in cps:
                        cp.start()
                    for cp in cps:
                        cp.wait()

        return pl.pallas_call(
            body, name=name, out_shape=tuple(self.out_shapes), in_specs=[_ANY] * n, out_specs=tuple([_ANY] * no),
            scratch_shapes=self.sems(), compiler_params=_params(),
        )(*self.arrays)


def ride(rider, body, n_in, n_out, grid):
    if rider is None:
        return body
    ni, no = len(rider.arrays), len(rider.out_shapes)

    def wrapped(*refs):
        ins, r_in = refs[:n_in], refs[n_in:n_in + ni]
        outs = refs[n_in + ni:n_in + ni + n_out]
        r_out = refs[n_in + ni + n_out:n_in + ni + n_out + no]
        rest = refs[n_in + ni + n_out + no:]
        scratch, sems = rest[:len(rest) - 3], rest[len(rest) - 3:]
        step, total = 0, 1
        for a, g in enumerate(grid):
            step, total = step * g + pl.program_id(a), total * g
        assert total >= 3
        relay_at = (4 * total) // 5 if rider.then is not None else total - 1

        @pl.when(step == 0)
        def _():
            for cp in rider.copies(r_in, r_out, *sems):
                cp.start()

        body(*ins, *outs, *scratch)

        @pl.when(step == relay_at)
        def _():
            for cp in rider.copies(r_in, r_out, *sems):
                cp.wait()
            if rider.then is not None:
                for cp in rider.then(r_in, r_out, *sems):
                    cp.start()

        if rider.then is not None:
            @pl.when(step == total - 1)
            def _():
                for cp in rider.then(r_in, r_out, *sems):
                    cp.wait()

    return wrapped


def call_with_rider(body, name, rider, ops, in_specs, out_shape, out_specs, scratch, grid):
    n_in, n_out = len(ops), len(out_shape)
    ops, in_specs, out_shape, out_specs, scratch = list(ops), list(in_specs), list(out_shape), list(out_specs), list(scratch)
    if rider is not None:
        ops += rider.arrays
        in_specs += [_ANY] * len(rider.arrays)
        out_shape += rider.out_shapes
        out_specs += [_ANY] * len(rider.out_shapes)
        scratch += rider.sems()
    res = pl.pallas_call(
        ride(rider, body, n_in, n_out, grid), name=name, out_shape=tuple(out_shape), grid=grid, in_specs=in_specs,
        out_specs=tuple(out_specs), scratch_shapes=scratch, compiler_params=_params(("arbitrary",) * len(grid)),
    )(*ops)
    return tuple(res[:n_out]), list(res[n_out:])


def gather_rider(shards):
    nf = len(shards)
    half = lambda ref, which: pl.ds(which * (ref.shape[-2] // 2), ref.shape[-2] // 2)

    def copies(s_refs, o_refs, send_sems, recv_sems, local_sems):
        x, y, c = _coords()
        me = 2 * x + y
        cps = [pltpu.make_async_copy(s_refs[f], o_refs[f].at[me], local_sems.at[f]) for f in range(nf)]
        for k, (px, py) in enumerate(_other_chips(x, y)):
            for f in range(nf):
                rows = half(s_refs[f], c)
                cps.append(_remote(send_sems, recv_sems, k * nf + f, s_refs[f].at[rows], o_refs[f].at[me, rows], (px, py, c)))
        return cps

    def relay(s_refs, o_refs, send_sems, recv_sems, local_sems):
        x, y, c = _coords()
        cps = []
        for k, (px, py) in enumerate(_other_chips(x, y)):
            for f in range(nf):
                landed = o_refs[f].at[2 * px + py, half(s_refs[f], c)]
                cps.append(_remote(send_sems, recv_sems, (3 + k) * nf + f, landed, landed, (x, y, 1 - c)))
        return cps

    return Rider(shards, [jax.ShapeDtypeStruct((N_CHIPS,) + sh.shape, sh.dtype) for sh in shards], 6 * nf, nf, copies, relay)


def scatter_rider(pairs):
    nf = len(pairs)

    def copies(p_refs, o_refs, send_sems, recv_sems, local_sems):
        x, y, c = _coords()
        cps = []
        for k, (px, py) in enumerate(_other_chips(x, y)):
            for f in range(nf):
                cps.append(_remote(send_sems, recv_sems, k * nf + f, p_refs[f].at[2 * px + py], o_refs[f].at[k], (px, py, c)))
        return cps

    return Rider(pairs, [jax.ShapeDtypeStruct((3,) + p.shape[1:], p.dtype) for p in pairs], 3 * nf, 0, copies)


def pair_swap(grads, name):
    def body(g_refs, o_refs, send_sems, recv_sems):
        x, y, c = _coords()
        cps = []
        for f, g_ref in enumerate(g_refs):
            rh = g_ref.shape[1] // 2
            cps.append(_remote(send_sems, recv_sems, f, g_ref.at[:, pl.ds((1 - c) * rh, rh), :], o_refs[f], (x, y, 1 - c)))
        for cp in cps:
            cp.start()
        for cp in cps:
            cp.wait()

    outs = [jax.ShapeDtypeStruct((g.shape[0], g.shape[1] // 2, g.shape[2]), g.dtype) for g in grads]
    return _exchange_call(body, name, grads, outs, len(grads))


def half_swap(halves, name):
    def body(h_refs, o_refs, send_sems, recv_sems):
        x, y, c = _coords()
        cps = [_remote(send_sems, recv_sems, f, h_ref, o_refs[f], (x, y, 1 - c)) for f, h_ref in enumerate(h_refs)]
        for cp in cps:
            cp.start()
        for cp in cps:
            cp.wait()

    return _exchange_call(body, name, halves, [jax.ShapeDtypeStruct(h.shape, h.dtype) for h in halves], len(halves))


def allsum_small(part, name):
    def body(p_ref, tot_ref, all_ref, send_sems, recv_sems):
        x, y, c = _coords()
        me, sibling = (x, y, c), (x, y, 1 - c)
        chips = _other_chips(x, y)

        def slot(px, py, pc):
            return all_ref.at[4 * px + 2 * py + pc]

        def copy(k, block, to, src=None):
            return pltpu.make_async_remote_copy(src_ref=slot(*block) if src is None else src, dst_ref=slot(*block),
                                                send_sem=send_sems.at[k], recv_sem=recv_sems.at[k], device_id=to, device_id_type=MESH)

        slot(*me)[...] = p_ref[...]
        first = [copy(0, me, sibling, src=p_ref)] + [copy(1 + j, me, (*chip, c), src=p_ref) for j, chip in enumerate(chips)]
        for cp in first:
            cp.start()
        passed = [copy(4 + j, (*chip, c), sibling) for j, chip in enumerate(chips)]
        for j, chip in enumerate(chips):
            copy(1 + j, (*chip, c), me).wait_recv()
            passed[j].start()
        copy(0, sibling, me).wait_recv()
        for j, chip in enumerate(chips):
            copy(4 + j, (*chip, 1 - c), me).wait_recv()
        for cp in first + passed:
            cp.wait_send()
        tot = all_ref[0]
        for d in range(1, 8):
            tot = tot + all_ref[d]
        tot_ref[...] = tot

    vm = pl.BlockSpec(memory_space=pltpu.VMEM)
    return pl.pallas_call(
        body, name=name, out_shape=jax.ShapeDtypeStruct(part.shape, F32), in_specs=[vm], out_specs=vm,
        scratch_shapes=[pltpu.VMEM((8,) + part.shape, F32), pltpu.SemaphoreType.DMA((7,)), pltpu.SemaphoreType.DMA((7,))],
        compiler_params=_params(),
    )(part)


QKVF_COLS = 772
QKVF_PAD = 896
FORWARD_CARRY = {0: (("fi0", "fo0"), ("qkv1", "o1", "fo1")), 1: (("fi1", "qkv2", "o2"), ()),
                 2: (("fi2", "fo2"), ("qkv3", "o3", "fo3")), 3: (("fi3",), ())}


def _tables_for(s):
    return _rot_tables(s)


def layer_families(layer):
    return (0, 1, layer // 2) if layer % 2 == 0 else (2, 3, layer // 2)


class GradientExchange:
    def __init__(self):
        self.core = lax.axis_index("c").astype(jnp.int32).reshape(1)
        self.chip = (2 * lax.axis_index("x") + lax.axis_index("y")).astype(jnp.int32).reshape(1)
        self.pairs, self.arrived, self.pending, self.riding = {}, {}, [], []

    def add(self, items, tag):
        got = pair_swap([g for _, _, g in items], f"grad_pair_swap_{tag}")
        for (fam, li, g), r in zip(items, got):
            self.pairs[(fam, li)] = pair_sum(g, r, self.core, f"grad_pair_sum_{fam}_{li}")
            self.pending.append((fam, li))

    def rider(self):
        self.riding, self.pending = self.pending, []
        return scatter_rider([self.pairs[k] for k in self.riding])

    def landed(self, outs):
        for k, o in zip(self.riding, outs):
            self.arrived[k] = o
        self.riding = []

    def finish(self, weights, moments1, moments2):
        if self.pending:
            self.landed(self.rider().run("grad_chip_scatter_last"))
        mine = []
        for fam, w in enumerate(weights):
            buf = jnp.zeros((w.shape[0], w.shape[1] // 2, w.shape[2]), F32)
            for li in range(w.shape[0]):
                buf = chip_sum(self.pairs[(fam, li)], self.arrived[(fam, li)], self.chip, li, buf, f"grad_chip_sum_{fam}_{li}")
            mine.append(buf)
        other = half_swap(mine, "grad_half_swap")
        return [adamw_family(w, m, v, gm, go, self.core, f"adamw_{f}")
                for f, (w, m, v, gm, go) in enumerate(zip(weights, moments1, moments2, mine, other))]


class KeepGradients:
    def __init__(self):
        self.grads = {}

    def add(self, items, tag):
        for fam, li, g in items:
            self.grads[(fam, li)] = g

    def rider(self):
        return None

    def landed(self, outs):
        pass


def kernel(x, norm_mix, w_qkv_even, w_o_even, w_qkvf_odd, b_forget, w_o_odd, norm_ffn, w_ffn_in, w_ffn_out, norm_final, loss_target, m_norm_mix, m_w_qkv_even, m_w_o_even, m_w_qkvf_odd, m_b_forget, m_w_o_odd, m_norm_ffn, m_w_ffn_in, m_w_ffn_out, m_norm_final, v_norm_mix, v_w_qkv_even, v_w_o_even, v_w_qkvf_odd, v_b_forget, v_w_o_odd, v_norm_ffn, v_w_ffn_in, v_w_ffn_out, v_norm_final):
    w_shards = [w_qkv_even, w_o_even, w_qkvf_odd, w_o_odd, w_ffn_in, w_ffn_out]
    shards = [w.astype(BF16) for w in w_shards]
    tables = _tables_for(x.shape[1])
    bias_pad = jnp.pad(b_forget, ((0, 0), (0, LANES - N_HEADS)))

    mine = {}
    for layer in range(DEPTH):
        fam_qkv, fam_o, li = layer_families(layer)
        mine.update({f"qkv{layer}": shards[fam_qkv][li], f"o{layer}": shards[fam_o][li],
                     f"fi{layer}": shards[4][layer], f"fo{layer}": shards[5][layer]})
    fetch = lambda names: gather_rider([mine[n] for n in names])
    have = dict(zip(("qkv0", "o0"), fetch(("qkv0", "o0")).run("gather_first")))
    saved, cur = [], x[0]
    for layer in range(DEPTH):
        carry, side_carry = FORWARD_CARRY[layer]
        cur, keep = forward_layer(layer, cur, have, norm_mix[layer:layer + 1], norm_ffn[layer:layer + 1], tables,
                                  bias_pad[layer // 2:layer // 2 + 1], fetch, carry, side_carry)
        saved.append(keep)

    dcur, g_final, loss_part = loss_head(cur, norm_final.reshape(1, D_MODEL), loss_target[0], "loss_head")

    exchange = GradientExchange()
    g_mix, g_ffn, g_bias = [None] * DEPTH, [None] * DEPTH, [None] * (DEPTH // 2)
    for layer in reversed(range(DEPTH)):
        dcur, g_mix[layer], g_ffn[layer], g_b = backward_layer(layer, dcur, saved[layer], norm_mix[layer:layer + 1],
                                                               norm_ffn[layer:layer + 1], tables, bias_pad[layer // 2:layer // 2 + 1], exchange)
        if g_b is not None:
            g_bias[layer // 2] = g_b

    zero_row = jnp.zeros((1, D_MODEL), F32)
    pad16 = lambda v: jnp.pad(v, (0, D_MODEL - v.shape[0]))[None, :]
    small_rows = lambda mix, ffn, fin, bias, last: jnp.concatenate(
        [r.reshape(1, D_MODEL) for r in mix] + [r.reshape(1, D_MODEL) for r in ffn] + [fin.reshape(1, D_MODEL)]
        + [pad16(b) for b in bias] + [last] + [zero_row] * (SMALL_ROWS - 12), axis=0)
    loss_row = pad16(loss_part[0, :1])
    small_g = allsum_small(small_rows(g_mix, g_ffn, g_final, g_bias, loss_row), "allsum_small")
    loss = small_g[11, 0]
    small_g = small_g.at[11].set(0.0)
    sw = small_rows(list(norm_mix), list(norm_ffn), norm_final, list(b_forget), zero_row)
    sm = small_rows(list(m_norm_mix), list(m_norm_ffn), m_norm_final, list(m_b_forget), zero_row)
    sv = small_rows(list(v_norm_mix), list(v_norm_ffn), v_norm_final, list(v_b_forget), zero_row)
    sd, snm, snv = adamw(sw, small_g, sm, sv, "adamw_small")

    def small_out(a):
        return a[0:4], a[8, :], a[9:11, :N_HEADS], a[4:8]

    widen = lambda t: jnp.pad(t, ((0, 0), (0, 0), (0, QKVF_PAD - QKVF_COLS)))
    padded = lambda ws: [widen(t) if f == 2 else t for f, t in enumerate(ws)]
    big = exchange.finish(
        padded(w_shards), padded([m_w_qkv_even, m_w_o_even, m_w_qkvf_odd, m_w_o_odd, m_w_ffn_in, m_w_ffn_out]),
        padded([v_w_qkv_even, v_w_o_even, v_w_qkvf_odd, v_w_o_odd, v_w_ffn_in, v_w_ffn_out]))

    def outputs(small, which):
        mix, fin, bias, ffn = small_out(small)
        qkv_e, o_e, qkvf, o_o, fi, fo = [big[f][which][:, :, :QKVF_COLS] if f == 2 else big[f][which] for f in range(6)]
        return [mix, qkv_e, o_e, qkvf, bias, o_o, ffn, fi, fo, fin]

    return (loss, dcur[None], *outputs(small_g, 0), *outputs(sd, 1), *outputs(snm, 2), *outputs(snv, 3))


def _chip_tile(rows, cols, at):
    return pl.BlockSpec((None, rows, cols), at)


def forward_layer(layer, cur, have, mix_gain, ffn_gain, tables, bias_row, fetch=None, carry=(), side_carry=()):
    n = f"l{layer}"
    s = cur.shape[0]
    w_qkv, w_o = have[f"qkv{layer}"], have[f"o{layer}"]
    rider = fetch(carry) if carry else None
    side_rider = fetch(side_carry) if side_carry else None
    h1 = rmsnorm_fwd(cur, mix_gain, n + "_norm_mix")
    keep = {"x": cur, "h1": h1, "w_o": w_o.reshape(D_ATTN, D_MODEL)}
    side = []
    if layer % 2 == 0:
        qkv = matmul(h1, w_qkv, "nn", BF16, n + "_qkv", 1024, 768, 1024, mnk=(s, 3 * D_ATTN, D_MODEL),
                     b_spec=_chip_tile(D_MODEL, 768, lambda i, j, kk: (j, 0, 0)))
        o_sb, st, rode = causal_fwd(qkv, 4, "sb", n + "_sb_fwd", rider=rider)
        qd, kd, vd = rotary_prep(qkv, tables, n + "_rotary")
        o_dil, lse_dil, side = dilated_fwd(qd, kd, vd, n + "_dil_fwd", rider=side_rider)
        attn = jnp.concatenate([o_sb, o_dil], axis=1).astype(BF16)
        keep.update(qd=qd, kd=kd, vd=vd, o_dil=o_dil, lse_dil=lse_dil, w_qkv=w_qkv)
    else:
        natural = jnp.transpose(w_qkv, (1, 0, 2)).reshape(D_MODEL, N_CHIPS * QKVF_COLS)
        w_gate = jnp.pad(natural[:, 3 * D_ATTN:], ((0, 0), (0, LANES - N_HEADS)))
        qkv = matmul(h1, natural[:, :3 * D_ATTN], "nn", BF16, n + "_qkv", 1024, 768, 1024)
        fl = matmul(h1, w_gate, "nn", F32, n + "_fgate", 512, LANES, 1024)
        cum = forget_fwd(fl, bias_row, n + "_forget_fwd")
        f_heads = cum[:, :N_HEADS].T
        fq = jnp.broadcast_to(f_heads[:, :, None], (N_HEADS, s, LANES))
        fk = f_heads.reshape(N_HEADS // 2, 2, s)
        attn, st, rode = causal_fwd(qkv, 8, "fox", n + "_fox_fwd", fq=fq, fk=fk, rider=rider)
        attn = attn.astype(BF16)
        keep.update(fl=fl, fq=fq, fk=fk, w_qkv=jnp.concatenate([natural[:, :3 * D_ATTN], w_gate], axis=1))
    have.update(zip(carry, rode))
    have.update(zip(side_carry, side))
    w_fi, w_fo = have[f"fi{layer}"], have[f"fo{layer}"].reshape(D_FF, D_MODEL)
    mid = matmul(attn, keep["w_o"], "nn", F32, n + "_attn_out", 1024, 1024, 1024, res=cur)
    h2 = rmsnorm_fwd(mid, ffn_gain, n + "_norm_ffn")
    gu = matmul(h2, w_fi, "nn", BF16, n + "_ffn_in", 1024, 1408, 1024, mnk=(s, 2 * D_FF, D_MODEL),
                b_spec=_chip_tile(D_MODEL, 1408, lambda i, j, kk: (j, 0, 0)))
    act = swiglu_fwd(gu, n + "_swiglu")
    out = matmul(act, w_fo, "nn", F32, n + "_ffn_out", 512, 1024, D_FF, res=mid)
    keep.update(qkv=qkv, st=st, attn=attn, mid=mid, h2=h2, gu=gu, act=act, w_fi=w_fi, w_fo=w_fo)
    return out, keep


def backward_layer(layer, dcur, kp, mix_gain, ffn_gain, tables, bias_row, exchange):
    n = f"l{layer}"
    s = dcur.shape[0]
    fam_qkv, fam_o, li = layer_families(layer)
    g_fo = matmul(kp["act"], dcur, "tn", BF16, n + "_d_w_ffn_out", 1408, 1024, s)
    dact = matmul(dcur, kp["w_fo"], "nt", BF16, n + "_d_act", 1024, 1408, 1024)
    dgu = swiglu_bwd(kp["gu"], dact, n + "_d_swiglu")
    g_fi = matmul(kp["h2"], dgu, "tn", BF16, n + "_d_w_ffn_in", 1024, 1408, 2048, mnk=(D_MODEL, 2 * D_FF, s),
                  o_spec=_chip_tile(D_MODEL, 1408, lambda i, j, kk: (j, 0, 0)), out_shape=(N_CHIPS, D_MODEL, 1408))
    all_chips = lambda cols: pl.BlockSpec((N_CHIPS, D_MODEL, cols), lambda i, j, kk: (0, 0, 0))
    dh2 = matmul(dgu, kp["w_fi"], "nt", F32, n + "_d_h2", 512, 1024, 2 * D_FF, mnk=(s, D_MODEL, 2 * D_FF), b_spec=all_chips(1408))
    dmid, g_ffn = rmsnorm_bwd(kp["mid"], ffn_gain, dh2, dcur, n + "_d_norm_ffn")
    g_o = matmul(kp["attn"], dmid, "tn", BF16, n + "_d_w_o", 1024, 1024, s)
    dattn = matmul(dmid, kp["w_o"], "nt", F32, n + "_d_attn", 1024, 1024, 1024)
    exchange.add([(5, layer, g_fo.reshape(N_CHIPS, D_FF // N_CHIPS, D_MODEL)), (4, layer, g_fi),
                  (fam_o, li, g_o.reshape(N_CHIPS, D_ATTN // N_CHIPS, D_MODEL))], f"l{layer}_ffn")
    rider = exchange.rider()
    g_bias = None
    if layer % 2 == 0:
        dq_a, dk_a, dv_a, rode = causal_bwd(kp["qkv"], dattn, kp["st"], 4, "sb", n + "_sb_bwd", rider=rider)
        dqd, dkd, dvd = dilated_bwd(kp["qd"], kp["kd"], kp["vd"], dattn, kp["o_dil"], kp["lse_dil"], 4, n + "_dil_bwd")
        dq_b, dk_b, dv_b = rotary_bwd(dqd, dkd, dvd, tables, n + "_d_rotary")
        dproj = jnp.concatenate([dq_a, dq_b, dk_a.astype(BF16), dk_b, dv_a.astype(BF16), dv_b], axis=1)
        g_qkv = matmul(kp["h1"], dproj, "tn", BF16, n + "_d_w_qkv", 1024, 768, 2048, mnk=(D_MODEL, 3 * D_ATTN, s),
                       o_spec=_chip_tile(D_MODEL, 768, lambda i, j, kk: (j, 0, 0)), out_shape=(N_CHIPS, D_MODEL, 768))
        dh1 = matmul(dproj, kp["w_qkv"], "nt", F32, n + "_d_h1", 512, 1024, 3 * D_ATTN, mnk=(s, D_MODEL, 3 * D_ATTN),
                     b_spec=all_chips(768))
    else:
        dq_f, dk_f, dv_f, dfk, rode = causal_bwd(kp["qkv"], dattn, kp["st"], 8, "fox", n + "_fox_bwd", fq=kp["fq"], fk=kp["fk"],
                                                 rider=rider)
        dcum = jnp.pad(dfk.reshape(N_HEADS, s).T, ((0, 0), (0, LANES - N_HEADS)))
        dfl, dbias = forget_bwd(kp["fl"], bias_row, dcum, n + "_forget_bwd")
        g_bias = dbias[0, :N_HEADS]
        dproj = jnp.concatenate([dq_f, dk_f.astype(BF16), dv_f.astype(BF16), dfl.astype(BF16)], axis=1)
        by_chip = dproj[:, :N_CHIPS * QKVF_COLS].reshape(s, N_CHIPS, QKVF_COLS)
        by_chip = jnp.pad(by_chip, ((0, 0), (0, 0), (0, QKVF_PAD - QKVF_COLS))).reshape(s, N_CHIPS * QKVF_PAD)
        g_qkv = matmul(kp["h1"], by_chip, "tn", BF16, n + "_d_w_qkv", 1024, QKVF_PAD, 2048, mnk=(D_MODEL, N_CHIPS * QKVF_PAD, s),
                       o_spec=_chip_tile(D_MODEL, QKVF_PAD, lambda i, j, kk: (j, 0, 0)), out_shape=(N_CHIPS, D_MODEL, QKVF_PAD))
        dh1 = matmul(dproj, kp["w_qkv"], "nt", F32, n + "_d_h1", 512, 1024, dproj.shape[1])
    exchange.landed(rode)
    exchange.add([(fam_qkv, li, g_qkv)], f"l{layer}_qkv")
    dx, g_mix = rmsnorm_bwd(kp["x"], mix_gain, dh1, dmid, n + "_d_norm_mix")
    return dx, g_mix, g_ffn, g_bias


def local_step(xs, target, norm_mix, norm_ffn, norm_final, b_forget, layer_weights):
    tables = _tables_for(xs.shape[0])
    bias_pad = jnp.pad(b_forget, ((0, 0), (0, LANES - N_HEADS)))
    saved, cur, have = [], xs, {}
    for layer in range(DEPTH):
        have.update(zip((f"qkv{layer}", f"o{layer}", f"fi{layer}", f"fo{layer}"), layer_weights[layer]))
        cur, keep = forward_layer(layer, cur, have, norm_mix[layer:layer + 1], norm_ffn[layer:layer + 1], tables,
                                  bias_pad[layer // 2:layer // 2 + 1])
        saved.append(keep)
    dcur, g_final, loss_part = loss_head(cur, norm_final.reshape(1, D_MODEL), target, "loss_head")
    keeper = KeepGradients()
    g_mix, g_ffn, g_bias = [None] * DEPTH, [None] * DEPTH, [None] * (DEPTH // 2)
    for layer in reversed(range(DEPTH)):
        dcur, g_mix[layer], g_ffn[layer], g_b = backward_layer(layer, dcur, saved[layer], norm_mix[layer:layer + 1],
                                                               norm_ffn[layer:layer + 1], tables, bias_pad[layer // 2:layer // 2 + 1], keeper)
        if g_b is not None:
            g_bias[layer // 2] = g_b
    return dcur, keeper.grads, (g_mix, g_ffn, g_final, g_bias), loss_part
```

```python
import functools

import jax
import jax.numpy as jnp
from jax import lax
from jax.experimental import pallas as pl
from jax.experimental.pallas import tpu as pltpu

F32 = jnp.float32
BF16 = jnp.bfloat16
MESH = pl.DeviceIdType.MESH

D_MODEL = 1024
DEPTH = 4
HEAD_DIM = 64
N_HEADS = 16
D_ATTN = 1024
D_FF = 2816
ROPE_THETA = 500000.0
ROT_HALF = 8
RMS_EPS = 1e-5
DIL_STRIDES = (1, 4, 16)
ADAM_LR, ADAM_B1, ADAM_B2, ADAM_EPS, ADAM_WD, ADAM_STEP = 0.001, 0.9, 0.999, 1e-8, 0.01, 10

LANES = 128
BLK = 128
VMEM_LIMIT = 56 * 1024 * 1024
NEG = -1e30
N_CHIPS = 4
FLAT_COLS = 1024
FLAT_ROWS = 12800
HALF_ROWS = FLAT_ROWS // 2
SMALL_ROWS = 16


def _params(sem=None):
    return pltpu.CompilerParams(dimension_semantics=sem, vmem_limit_bytes=VMEM_LIMIT)


def _dot(a, b):
    return lax.dot_general(a, b, (((1,), (0,)), ((), ())), preferred_element_type=F32)


def _dot_nt(a, b):
    return lax.dot_general(a, b, (((1,), (1,)), ((), ())), preferred_element_type=F32)


def _dot_tn(a, b):
    return lax.dot_general(a, b, (((0,), (0,)), ((), ())), preferred_element_type=F32)


def _split3(x):
    x1 = x.astype(BF16)
    r1 = x - x1.astype(F32)
    x2 = r1.astype(BF16)
    x3 = (r1 - x2.astype(F32)).astype(BF16)
    return x1, x2, x3


def _dot_exact_lhs(x, t):
    x1, x2, x3 = _split3(x)
    return _dot(x1, t) + _dot(x2, t) + _dot(x3, t)


def _dot_exact_rhs(t, x):
    x1, x2, x3 = _split3(x)
    return _dot(t, x1) + _dot(t, x2) + _dot(t, x3)


def _iotas(shape=(BLK, LANES)):
    return lax.broadcasted_iota(jnp.int32, shape, 0), lax.broadcasted_iota(jnp.int32, shape, 1)


_DIMS = {"nn": (((1,), (0,)), ((), ())), "nt": (((1,), (1,)), ((), ())), "tn": (((0,), (0,)), ((), ()))}


def matmul(a, b, mode, out_dtype, name, tm, tn, tk, res=None, mnk=None, b_spec=None, o_spec=None, out_shape=None, into=None):
    if mnk is not None:
        m, n, k = mnk
    elif mode == "nn":
        (m, k), (k2, n) = a.shape, b.shape
    elif mode == "nt":
        (m, k), (n, k2) = a.shape, b.shape
    else:
        (k, m), (k2, n) = a.shape, b.shape
    assert m % tm == 0 and n % tn == 0 and k % tk == 0, (name, a.shape, b.shape)
    nk = k // tk
    a_spec = pl.BlockSpec((tk, tm), lambda i, j, kk: (kk, i)) if mode == "tn" else pl.BlockSpec((tm, tk), lambda i, j, kk: (i, kk))
    if b_spec is None:
        b_spec = pl.BlockSpec((tn, tk), lambda i, j, kk: (j, kk)) if mode == "nt" else pl.BlockSpec((tk, tn), lambda i, j, kk: (kk, j))
    r_spec = pl.BlockSpec((tm, tn), lambda i, j, kk: (i, j))
    if o_spec is None:
        o_spec = r_spec
    dims = _DIMS[mode]
    has_res = res is not None
    n_in = 2 + int(has_res) + int(into is not None)

    def body(*refs):
        a_ref, b_ref = refs[0], refs[1]
        r_ref = refs[2] if has_res else None
        o_ref = refs[n_in]

        def finish(v):
            if has_res:
                v = v + r_ref[...]
            o_ref[...] = v.astype(out_dtype)

        bv = b_ref[...]
        if bv.ndim == 3:
            bv = jnp.concatenate([bv[j] for j in range(bv.shape[0])], axis=1)
        p = lax.dot_general(a_ref[...].astype(BF16), bv.astype(BF16), dims, preferred_element_type=F32)
        if nk == 1:
            finish(p)
        else:
            acc = refs[-1]
            kk = pl.program_id(2)

            @pl.when(kk == 0)
            def _():
                acc[...] = p

            @pl.when(kk > 0)
            def _():
                acc[...] += p

            @pl.when(kk == nk - 1)
            def _():
                finish(acc[...])

    ops = [a, b] + ([res] if has_res else []) + ([into] if into is not None else [])
    specs = [a_spec, b_spec] + ([r_spec] if has_res else []) + ([_ANY] if into is not None else [])
    if into is not None:
        out_shape = jax.ShapeDtypeStruct(into.shape, into.dtype)
    else:
        out_shape = jax.ShapeDtypeStruct((m, n) if out_shape is None else out_shape, out_dtype)
    return pl.pallas_call(
        body, name=name, out_shape=out_shape,
        grid=(m // tm, n // tn, nk), in_specs=specs, out_specs=o_spec,
        scratch_shapes=[pltpu.VMEM((tm, tn), F32)] if nk > 1 else [],
        input_output_aliases={n_in - 1: 0} if into is not None else {},
        compiler_params=_params(("parallel", "parallel", "arbitrary")),
    )(*ops)


ROWS = 256


def _row_spec(cols, rows=ROWS):
    return pl.BlockSpec((rows, cols), lambda i: (i, 0))


def _fix_spec(r, cols):
    return pl.BlockSpec((r, cols), lambda i: (0, 0))


def rmsnorm_fwd(x, g, name):
    s, d = x.shape

    def body(x_ref, g_ref, h_ref):
        xv = x_ref[...]
        rstd = lax.rsqrt(jnp.mean(xv * xv, axis=-1, keepdims=True) + RMS_EPS)
        h_ref[...] = (xv * rstd * g_ref[...]).astype(BF16)

    return pl.pallas_call(
        body, name=name, out_shape=jax.ShapeDtypeStruct((s, d), BF16), grid=(s // ROWS,),
        in_specs=[_row_spec(d), _fix_spec(1, d)], out_specs=_row_spec(d), compiler_params=_params(("parallel",)),
    )(x, g)


def _rms_bwd_math(xv, gv, dh):
    rstd = lax.rsqrt(jnp.mean(xv * xv, axis=-1, keepdims=True) + RMS_EPS)
    xhat = xv * rstd
    u = dh * gv
    dx = rstd * (u - xhat * jnp.mean(u * xhat, axis=-1, keepdims=True))
    return dx, dh * xhat


def rmsnorm_bwd(x, g, dh, dres, name):
    s, d = x.shape

    def body(x_ref, g_ref, dh_ref, dres_ref, dx_ref, dg_ref):
        dx, dgt = _rms_bwd_math(x_ref[...], g_ref[...], dh_ref[...])
        dx_ref[...] = dres_ref[...] + dx
        part = jnp.sum(dgt, axis=0, keepdims=True)

        @pl.when(pl.program_id(0) == 0)
        def _():
            dg_ref[...] = part

        @pl.when(pl.program_id(0) > 0)
        def _():
            dg_ref[...] += part

    return pl.pallas_call(
        body, name=name, out_shape=(jax.ShapeDtypeStruct((s, d), F32), jax.ShapeDtypeStruct((1, d), F32)),
        grid=(s // ROWS,), in_specs=[_row_spec(d), _fix_spec(1, d), _row_spec(d), _row_spec(d)],
        out_specs=(_row_spec(d), _fix_spec(1, d)), compiler_params=_params(("arbitrary",)),
    )(x, g, dh, dres)


def loss_head(x, g, target, name):
    s, d = x.shape

    def body(x_ref, g_ref, t_ref, dx_ref, dg_ref, loss_ref):
        xv, gv = x_ref[...], g_ref[...]
        rstd = lax.rsqrt(jnp.mean(xv * xv, axis=-1, keepdims=True) + RMS_EPS)
        err = xv * rstd * gv - t_ref[...]
        dx, dgt = _rms_bwd_math(xv, gv, err * (1.0 / d))
        dx_ref[...] = dx
        part = jnp.sum(dgt, axis=0, keepdims=True)
        lpart = jnp.full((1, LANES), 0.5 / d, F32) * jnp.sum(err * err)

        @pl.when(pl.program_id(0) == 0)
        def _():
            dg_ref[...] = part
            loss_ref[...] = lpart

        @pl.when(pl.program_id(0) > 0)
        def _():
            dg_ref[...] += part
            loss_ref[...] += lpart

    return pl.pallas_call(
        body, name=name,
        out_shape=(jax.ShapeDtypeStruct((s, d), F32), jax.ShapeDtypeStruct((1, d), F32), jax.ShapeDtypeStruct((1, LANES), F32)),
        grid=(s // ROWS,), in_specs=[_row_spec(d), _fix_spec(1, d), _row_spec(d)],
        out_specs=(_row_spec(d), _fix_spec(1, d), _fix_spec(1, LANES)), compiler_params=_params(("arbitrary",)),
    )(x, g, target)


def swiglu_fwd(gu, name):
    s, f2 = gu.shape
    f = f2 // 2

    def body(gu_ref, a_ref):
        gv, uv = gu_ref[:, :f].astype(F32), gu_ref[:, f:].astype(F32)
        a_ref[...] = (gv * (1.0 / (1.0 + jnp.exp(-gv))) * uv).astype(BF16)

    return pl.pallas_call(
        body, name=name, out_shape=jax.ShapeDtypeStruct((s, f), BF16), grid=(s // ROWS,),
        in_specs=[_row_spec(f2)], out_specs=_row_spec(f), compiler_params=_params(("parallel",)),
    )(gu)


def swiglu_bwd(gu, dact, name):
    s, f2 = gu.shape
    f = f2 // 2

    def body(gu_ref, da_ref, o_ref):
        gv, uv, da = gu_ref[:, :f].astype(F32), gu_ref[:, f:].astype(F32), da_ref[...].astype(F32)
        sg = 1.0 / (1.0 + jnp.exp(-gv))
        o_ref[:, :f] = (da * uv * sg * (1.0 + gv * (1.0 - sg))).astype(BF16)
        o_ref[:, f:] = (da * gv * sg).astype(BF16)

    return pl.pallas_call(
        body, name=name, out_shape=jax.ShapeDtypeStruct((s, f2), BF16), grid=(s // ROWS,),
        in_specs=[_row_spec(f2), _row_spec(f)], out_specs=_row_spec(f2), compiler_params=_params(("parallel",)),
    )(gu, dact)


Q_OFF, K_OFF, V_OFF = 0, 8, 16


KB = 512
BQ = 512
SUB = KB // BLK


def _softplus_parts(z):
    sp = jnp.log(1.0 + jnp.exp(-jnp.abs(z)))
    ls = jnp.minimum(z, 0.0) - sp
    return ls, ls - z


def _wide(t):
    return jnp.concatenate([t] * SUB, axis=1)


def _chunk_dots(x, tri):
    terms = []
    for u in range(SUB):
        xu = x[:, u * BLK:(u + 1) * BLK]
        hi = xu.astype(BF16)
        terms += [hi, (xu - hi.astype(F32)).astype(BF16)]
    r = _dot(jnp.concatenate(terms, axis=0), tri)
    rows = x.shape[0]
    piece = lambda n: r[n * rows:(n + 1) * rows]
    return [piece(2 * u) + piece(2 * u + 1) for u in range(SUB)]


def _block_suffix_sums(x, suffix, c):
    loc = _chunk_dots(x, suffix)
    out = [None] * SUB
    for u in reversed(range(SUB)):
        out[u] = loc[u] + c
        c = c + jnp.sum(x[:, u * BLK:(u + 1) * BLK], axis=1, keepdims=True)
    return jnp.concatenate(out, axis=1), c


def _block_prefix_sums(x, tri, c):
    loc = _chunk_dots(x, tri)
    out = []
    for u in range(SUB):
        out.append(loc[u] + c)
        c = c + jnp.sum(x[:, u * BLK:(u + 1) * BLK], axis=1, keepdims=True)
    return jnp.concatenate(out, axis=1), c


def causal_fwd(qkv, npairs, mode, name, fq=None, fk=None, rider=None):
    s = qkv.shape[0]
    nq = s // BQ
    fox = mode == "fox"

    def body(*refs):
        if fox:
            q_ref, k_ref, v_ref, fq_ref, fk_ref, o_ref, st_ref = refs
        else:
            q_ref, k_ref, v_ref, o_ref, st_ref = refs
        i = pl.program_id(1)
        nkb = (i * BQ + BQ - 1) // KB + 1
        row, lane = _iotas((BQ, KB))
        row_s, lane_s = _iotas()
        _, lane_q = _iotas((BQ, LANES))
        nfull = (i * BQ) // KB
        qpos = i * BQ + row
        qf = q_ref[...].astype(F32) * 0.125
        hms = (lane_q < HEAD_DIM, lane_q >= HEAD_DIM)
        qas = [jnp.where(hm, qf, 0.0).astype(BF16) for hm in hms]
        suffix = jnp.where(row_s > lane_s, 1.0, 0.0).astype(BF16)
        zero = jnp.zeros((BQ, LANES), F32)
        col0 = jnp.zeros((BQ, 1), F32)

        def kv(j):
            r0 = pl.multiple_of(j * KB, KB)
            return r0, k_ref[pl.ds(r0, KB), :], v_ref[pl.ds(r0, KB), :]

        if fox:
            fqs = [_wide(fq_ref[a]) for a in range(2)]

            def step(j, carry, masked):
                r0, kb, vb = kv(j)
                new = []
                for a in range(2):
                    acc, mx, l = carry[3 * a:3 * a + 3]
                    z = _dot_nt(qas[a], kb) + fqs[a] - fk_ref[a:a + 1, pl.ds(r0, KB)]
                    if masked:
                        z = jnp.where(r0 + lane <= qpos, z, NEG)
                    mnew = jnp.maximum(mx, jnp.max(z, axis=1, keepdims=True))
                    p = jnp.exp(z - mnew)
                    alpha = jnp.exp(mx - mnew)
                    new += [alpha * acc + _dot(p.astype(BF16), vb), mnew, alpha * l + jnp.sum(p, axis=1, keepdims=True)]
                return tuple(new)

            neg = jnp.full((BQ, 1), NEG, F32)
            res = lax.fori_loop(0, nfull, functools.partial(step, masked=False), (zero, neg, col0, zero, neg, col0))
            res = lax.fori_loop(nfull, nkb, functools.partial(step, masked=True), res)
            outs = [res[3 * a] / res[3 * a + 2] for a in range(2)]
            stats = [res[3 * a + 1] + jnp.log(res[3 * a + 2]) for a in range(2)]
        else:
            def step(j, carry, masked):
                r0, kb, vb = kv(j)
                strict = r0 + lane < qpos
                new = []
                for a in range(2):
                    acc, c = carry[2 * a:2 * a + 2]
                    ls, lm = _softplus_parts(_dot_nt(qas[a], kb))
                    if masked:
                        lm = jnp.where(strict, lm, 0.0)
                    between, c = _block_suffix_sums(lm, suffix, c)
                    aw = jnp.exp(ls + between)
                    if masked:
                        aw = jnp.where(strict, aw, 0.0)
                    new += [acc + _dot(aw.astype(BF16), vb), c]
                return tuple(new)

            res = lax.fori_loop(0, nkb - nfull, lambda jj, c: step(nkb - 1 - jj, c, True), (zero, col0, zero, col0))
            res = lax.fori_loop(0, nfull, lambda jj, c: step(nfull - 1 - jj, c, False), res)
            outs, stats = [res[0], res[2]], [res[1], res[3]]
        o_ref[...] = jnp.where(hms[0], outs[0], outs[1])
        for a in range(2):
            st_ref[a] = jnp.broadcast_to(stats[a], (BQ, LANES))

    col = lambda off: (lambda p, i: (0, off + p))
    in_specs = [pl.BlockSpec((BQ, LANES), lambda p, i: (i, Q_OFF + p)),
                pl.BlockSpec((s, LANES), col(K_OFF)), pl.BlockSpec((s, LANES), col(V_OFF))]
    ops = [qkv, qkv, qkv]
    if fox:
        in_specs += [pl.BlockSpec((2, BQ, LANES), lambda p, i: (p, i, 0)), pl.BlockSpec((None, 2, s), lambda p, i: (p, 0, 0))]
        ops += [fq, fk]
    (o, stat), rode = call_with_rider(
        body, name, rider, ops, in_specs,
        [jax.ShapeDtypeStruct((s, npairs * LANES), F32), jax.ShapeDtypeStruct((2 * npairs, s, LANES), F32)],
        [pl.BlockSpec((BQ, LANES), lambda p, i: (i, p)), pl.BlockSpec((2, BQ, LANES), lambda p, i: (p, i, 0))], [], (npairs, nq))
    return o, stat, rode


def causal_bwd(qkv, do, stat, npairs, mode, name, fq=None, fk=None, rider=None):
    s = qkv.shape[0]
    nq = s // BQ
    fox = mode == "fox"

    def body(*refs):
        if fox:
            q_ref, k_ref, v_ref, do_ref, st_ref, fq_ref, fk_ref, dq_ref, dk_ref, dv_ref, df_ref, p_s, dp_s = refs
        else:
            q_ref, k_ref, v_ref, do_ref, st_ref, dq_ref, dk_ref, dv_ref = refs
        i = pl.program_id(1)

        @pl.when(i == 0)
        def _():
            dk_ref[...] = jnp.zeros_like(dk_ref)
            dv_ref[...] = jnp.zeros_like(dv_ref)
            if fox:
                df_ref[...] = jnp.zeros_like(df_ref)

        nkb = (i * BQ + BQ - 1) // KB + 1
        nfull = (i * BQ) // KB
        row, lane = _iotas((BQ, KB))
        row_s, lane_s = _iotas()
        _, lane_q = _iotas((BQ, LANES))
        qpos = i * BQ + row
        qf = q_ref[...].astype(F32) * 0.125
        dov = do_ref[...]
        hms = (lane_q < HEAD_DIM, lane_q >= HEAD_DIM)
        qas = [jnp.where(hm, qf, 0.0).astype(BF16) for hm in hms]
        doas = [jnp.where(hm, dov, 0.0).astype(BF16) for hm in hms]
        stas = [_wide(st_ref[a]) for a in range(2)]
        zero = jnp.zeros((BQ, LANES), F32)
        col0 = jnp.zeros((BQ, 1), F32)

        def kv(j):
            r0 = pl.multiple_of(j * KB, KB)
            return r0, k_ref[pl.ds(r0, KB), :], v_ref[pl.ds(r0, KB), :]

        if fox:
            fqs = [_wide(fq_ref[a]) for a in range(2)]

            def probs(j, deltas, masked):
                r0, kb, vb = kv(j)
                new = []
                for a in range(2):
                    z = _dot_nt(qas[a], kb) + fqs[a] - fk_ref[a:a + 1, pl.ds(r0, KB)]
                    p = jnp.exp(z - stas[a])
                    if masked:
                        p = jnp.where(r0 + lane <= qpos, p, 0.0)
                    dp = _dot_nt(doas[a], vb)
                    p_s[a, j] = p
                    dp_s[a, j] = dp
                    new.append(deltas[a] + jnp.sum(p * dp, axis=1, keepdims=True))
                return tuple(new)

            deltas = lax.fori_loop(0, nfull, functools.partial(probs, masked=False), (col0, col0))
            deltas = lax.fori_loop(nfull, nkb, functools.partial(probs, masked=True), deltas)

            def step(j, dqs):
                r0, kb, _ = kv(j)
                new = []
                dk = jnp.zeros((KB, LANES), F32)
                dv = jnp.zeros((KB, LANES), F32)
                for a in range(2):
                    p = p_s[a, j]
                    ds = p * (dp_s[a, j] - deltas[a])
                    dsb = ds.astype(BF16)
                    dk += _dot_tn(dsb, qas[a])
                    dv += _dot_tn(p.astype(BF16), doas[a])
                    df_ref[a:a + 1, pl.ds(r0, KB)] -= jnp.sum(ds, axis=0, keepdims=True)
                    new.append(dqs[a] + _dot(dsb, kb))
                dk_ref[pl.ds(r0, KB), :] += dk
                dv_ref[pl.ds(r0, KB), :] += dv
                return tuple(new)

            dqs = lax.fori_loop(0, nkb, step, (zero, zero))
        else:
            incl = jnp.where(row_s <= lane_s, 1.0, 0.0).astype(BF16)
            excl = jnp.where(row_s < lane_s, 1.0, 0.0).astype(BF16)

            def step(j, carry, masked):
                r0, kb, vb = kv(j)
                strict = r0 + lane < qpos
                new = []
                dk = jnp.zeros((KB, LANES), F32)
                dv = jnp.zeros((KB, LANES), F32)
                for a in range(2):
                    dq, cm, cg = carry[3 * a:3 * a + 3]
                    ls, lm = _softplus_parts(_dot_nt(qas[a], kb))
                    if masked:
                        lm = jnp.where(strict, lm, 0.0)
                    beta = jnp.exp(ls)
                    upto, cm = _block_prefix_sums(lm, incl, cm)
                    aw = jnp.exp(ls + stas[a] - upto)
                    if masked:
                        aw = jnp.where(strict, aw, 0.0)
                    g = aw * _dot_nt(doas[a], vb)
                    pre, cg = _block_prefix_sums(g, excl, cg)
                    dz = g * (1.0 - beta) - pre * beta
                    if masked:
                        dz = jnp.where(strict, dz, 0.0)
                    dzb = dz.astype(BF16)
                    dk += _dot_tn(dzb, qas[a])
                    dv += _dot_tn(aw.astype(BF16), doas[a])
                    new += [dq + _dot(dzb, kb), cm, cg]
                dk_ref[pl.ds(r0, KB), :] += dk
                dv_ref[pl.ds(r0, KB), :] += dv
                return tuple(new)

            res = lax.fori_loop(0, nfull, functools.partial(step, masked=False), (zero, col0, col0, zero, col0, col0))
            res = lax.fori_loop(nfull, nkb, functools.partial(step, masked=True), res)
            dqs = (res[0], res[3])
        dq_ref[...] = (jnp.where(hms[0], dqs[0], dqs[1]) * 0.125).astype(BF16)

    col = lambda off: (lambda p, i: (0, off + p))
    blk = pl.BlockSpec((BQ, LANES), lambda p, i: (i, p))
    acc = pl.BlockSpec((s, LANES), lambda p, i: (0, p))
    st_spec = pl.BlockSpec((2, BQ, LANES), lambda p, i: (p, i, 0))
    in_specs = [pl.BlockSpec((BQ, LANES), lambda p, i: (i, Q_OFF + p)), pl.BlockSpec((s, LANES), col(K_OFF)),
                pl.BlockSpec((s, LANES), col(V_OFF)), blk, st_spec]
    ops = [qkv, qkv, qkv, do, stat]
    w = npairs * LANES
    out_shape = [jax.ShapeDtypeStruct((s, w), BF16), jax.ShapeDtypeStruct((s, w), F32), jax.ShapeDtypeStruct((s, w), F32)]
    out_specs = [blk, acc, acc]
    scratch = []
    if fox:
        fk_spec = pl.BlockSpec((None, 2, s), lambda p, i: (p, 0, 0))
        in_specs += [st_spec, fk_spec]
        ops += [fq, fk]
        out_shape.append(jax.ShapeDtypeStruct((npairs, 2, s), F32))
        out_specs.append(fk_spec)
        scratch = [pltpu.VMEM((2, s // KB, BQ, KB), F32)] * 2
    outs, rode = call_with_rider(body, name, rider, ops, in_specs, out_shape, out_specs, scratch, (npairs, nq))
    return (*outs, rode)


def forget_fwd(fl, bias, name):
    s = fl.shape[0]

    def body(fl_ref, b_ref, f_ref):
        row, lane = _iotas()
        lower = jnp.where(lane <= row, 1.0, 0.0).astype(BF16)

        def step(n, carry):
            r0 = pl.multiple_of(n * BLK, BLK)
            ls, _ = _softplus_parts(fl_ref[pl.ds(r0, BLK), :] + b_ref[...])
            blk = _dot_exact_rhs(lower, ls) + carry
            f_ref[pl.ds(r0, BLK), :] = blk
            return blk[BLK - 1:BLK, :]

        lax.fori_loop(0, s // BLK, step, jnp.zeros((1, LANES), F32))

    return pl.pallas_call(
        body, name=name, out_shape=jax.ShapeDtypeStruct((s, LANES), F32),
        in_specs=[pl.BlockSpec(memory_space=pltpu.VMEM)] * 2, out_specs=pl.BlockSpec(memory_space=pltpu.VMEM),
        compiler_params=_params(),
    )(fl, bias)


def forget_bwd(fl, bias, df, name):
    s = fl.shape[0]
    nb = s // BLK

    def body(fl_ref, b_ref, df_ref, o_ref, db_ref):
        row, lane = _iotas()
        upper = jnp.where(lane >= row, 1.0, 0.0).astype(BF16)

        def step(nn, carry):
            tail, db = carry
            r0 = pl.multiple_of((nb - 1 - nn) * BLK, BLK)
            dls = _dot_exact_rhs(upper, df_ref[pl.ds(r0, BLK), :]) + tail
            xv = fl_ref[pl.ds(r0, BLK), :] + b_ref[...]
            dfl = dls * (1.0 / (1.0 + jnp.exp(xv)))
            o_ref[pl.ds(r0, BLK), :] = dfl
            return dls[0:1, :], db + jnp.sum(dfl, axis=0, keepdims=True)

        _, db = lax.fori_loop(0, nb, step, (jnp.zeros((1, LANES), F32), jnp.zeros((1, LANES), F32)))
        db_ref[...] = db

    return pl.pallas_call(
        body, name=name, out_shape=(jax.ShapeDtypeStruct((s, LANES), F32), jax.ShapeDtypeStruct((1, LANES), F32)),
        in_specs=[pl.BlockSpec(memory_space=pltpu.VMEM)] * 3,
        out_specs=(pl.BlockSpec(memory_space=pltpu.VMEM), pl.BlockSpec(memory_space=pltpu.VMEM)),
        compiler_params=_params(),
    )(fl, bias, df)


def _rot_tables(s):
    inv = ROPE_THETA ** (-jnp.arange(ROT_HALF, dtype=F32) * 2.0 / (2 * ROT_HALF))
    ang = jnp.arange(s, dtype=F32)[:, None] * inv[None, :]
    cos, sin = jnp.cos(ang), jnp.sin(ang)
    z8 = jnp.zeros((s, ROT_HALF), F32)
    rest = HEAD_DIM - 2 * ROT_HALF
    zr, onr = jnp.zeros((s, rest), F32), jnp.ones((s, rest), F32)
    tile = lambda t: jnp.tile(t, (1, 2))
    return tile(jnp.concatenate([cos, cos, onr], 1)), tile(jnp.concatenate([-sin, z8, zr], 1)), tile(jnp.concatenate([z8, sin, zr], 1))


def rotary_prep(qkv, tables, name):
    s = qkv.shape[0]
    w = 4 * LANES

    def body(q_ref, k_ref, v_ref, c_ref, s1_ref, s2_ref, qo_ref, ko_ref, vo_ref):
        c, s1, s2 = c_ref[...], s1_ref[...], s2_ref[...]

        def rot(xv):
            return xv * c + pltpu.roll(xv, LANES - ROT_HALF, 1) * s1 + pltpu.roll(xv, ROT_HALF, 1) * s2

        qo_ref[...] = rot(q_ref[...].astype(F32)) * 0.125
        ko_ref[...] = rot(k_ref[...].astype(F32))
        vo_ref[...] = v_ref[...].astype(F32)

    cb = lambda off: pl.BlockSpec((ROWS, LANES), lambda i, j: (i, off + j))
    tb = pl.BlockSpec((ROWS, LANES), lambda i, j: (i, 0))
    out = jax.ShapeDtypeStruct((s, w), F32)
    return pl.pallas_call(
        body, name=name, out_shape=(out, out, out), grid=(s // ROWS, 4),
        in_specs=[cb(Q_OFF + 4), cb(K_OFF + 4), cb(V_OFF + 4), tb, tb, tb], out_specs=(cb(0), cb(0), cb(0)),
        compiler_params=_params(("parallel", "parallel")),
    )(qkv, qkv, qkv, *tables)


def rotary_bwd(dq, dk, dv, tables, name):
    s, w = dq.shape

    def body(dq_ref, dk_ref, dv_ref, c_ref, s1_ref, s2_ref, qo_ref, ko_ref, vo_ref):
        c, s1, s2 = c_ref[...], s1_ref[...], s2_ref[...]

        def rot_t(dy):
            return dy * c + pltpu.roll(dy * s1, ROT_HALF, 1) + pltpu.roll(dy * s2, LANES - ROT_HALF, 1)

        qo_ref[...] = (rot_t(dq_ref[...]) * 0.125).astype(BF16)
        ko_ref[...] = rot_t(dk_ref[...]).astype(BF16)
        vo_ref[...] = dv_ref[...].astype(BF16)

    cb = pl.BlockSpec((ROWS, LANES), lambda i, j: (i, j))
    tb = pl.BlockSpec((ROWS, LANES), lambda i, j: (i, 0))
    out = jax.ShapeDtypeStruct((s, w), BF16)
    return pl.pallas_call(
        body, name=name, out_shape=(out, out, out), grid=(s // ROWS, w // LANES),
        in_specs=[cb, cb, cb, tb, tb, tb], out_specs=(cb, cb, cb), compiler_params=_params(("parallel", "parallel")),
    )(dq, dk, dv, *tables)


def _deinterleave(dst, src_ref, stride, s, dtype):
    length = s // stride
    for r in range(stride):
        if stride == 1:
            dst[...] = src_ref[...].astype(dtype)
        else:
            dst[r * length:(r + 1) * length, :] = src_ref[pl.ds(r, length, stride=stride), :].astype(dtype)


def _band_masks(row, lane, first):
    return lane <= row, lane >= row + jnp.where(first, BLK, 0)


def dilated_fwd(qd, kd, vd, name, rider=None):
    s, w = qd.shape
    npairs = w // LANES
    nblk = s // BLK

    def body(q_ref, k_ref, v_ref, o_ref, lse_ref, qs, ks, vs, od, ld, on, ln):
        row, lane = _iotas()
        for pi, stride in enumerate(DIL_STRIDES):
            per = (s // stride) // BLK
            _deinterleave(qs, q_ref, stride, s, BF16)
            _deinterleave(ks, k_ref, stride, s, BF16)
            _deinterleave(vs, v_ref, stride, s, BF16)

            def block(b, carry):
                r0 = pl.multiple_of(b * BLK, BLK)
                rp = pl.multiple_of(jnp.maximum(b - 1, 0) * BLK, BLK)
                mc, mp = _band_masks(row, lane, b % per == 0)
                q = qs[pl.ds(r0, BLK), :]
                kc, kp, vc, vp = ks[pl.ds(r0, BLK), :], ks[pl.ds(rp, BLK), :], vs[pl.ds(r0, BLK), :], vs[pl.ds(rp, BLK), :]
                out = jnp.zeros((BLK, LANES), F32)
                lse = jnp.zeros((BLK, LANES), F32)
                for a in range(2):
                    hm = (lane < HEAD_DIM) if a == 0 else (lane >= HEAD_DIM)
                    qa = jnp.where(hm, q.astype(F32), 0.0).astype(BF16)
                    sc = jnp.where(mc, _dot_nt(qa, kc), NEG)
                    sp = jnp.where(mp, _dot_nt(qa, kp), NEG)
                    mx = jnp.maximum(jnp.max(sc, axis=1, keepdims=True), jnp.max(sp, axis=1, keepdims=True))
                    pc, pp = jnp.exp(sc - mx), jnp.exp(sp - mx)
                    l = jnp.sum(pc, axis=1, keepdims=True) + jnp.sum(pp, axis=1, keepdims=True)
                    oa = (_dot(pc.astype(BF16), vc) + _dot(pp.astype(BF16), vp)) / l
                    out = jnp.where(hm, oa, out)
                    lse = jnp.where(hm, mx + jnp.log(l), lse)
                od[pl.ds(r0, BLK), :] = out
                ld[pl.ds(r0, BLK), :] = lse
                return carry

            lax.fori_loop(0, nblk, block, 0, unroll=2)
            length = s // stride
            for r in range(stride):
                if stride == 1:
                    on[pi] = od[...]
                    ln[pi] = ld[...]
                else:
                    on[pi, pl.ds(r, length, stride=stride), :] = od[r * length:(r + 1) * length, :]
                    ln[pi, pl.ds(r, length, stride=stride), :] = ld[r * length:(r + 1) * length, :]

        def merge(n, carry):
            r0 = pl.multiple_of(n * BLK, BLK)
            ls = [ln[pi, pl.ds(r0, BLK), :] for pi in range(3)]
            mx = jnp.maximum(jnp.maximum(ls[0], ls[1]), ls[2])
            ws = [jnp.exp(lv - mx) for lv in ls]
            den = ws[0] + ws[1] + ws[2]
            num = ws[0] * on[0, pl.ds(r0, BLK), :] + ws[1] * on[1, pl.ds(r0, BLK), :] + ws[2] * on[2, pl.ds(r0, BLK), :]
            o_ref[pl.ds(r0, BLK), :] = num / den
            lse_ref[pl.ds(r0, BLK), :] = mx + jnp.log(den)
            return carry

        lax.fori_loop(0, nblk, merge, 0, unroll=2)

    colspec = pl.BlockSpec((s, LANES), lambda p: (0, p))
    out = jax.ShapeDtypeStruct((s, w), F32)
    scratch = [pltpu.VMEM((s, LANES), BF16)] * 3 + [pltpu.VMEM((s, LANES), F32)] * 2 + [pltpu.VMEM((3, s, LANES), F32)] * 2
    (o, lse), rode = call_with_rider(body, name, rider, [qd, kd, vd], [colspec] * 3, [out, out], [colspec, colspec], scratch, (npairs,))
    return o, lse, rode


def dilated_bwd(qd, kd, vd, do, out, lse, do_off, name, rider=None):
    s, w = qd.shape
    npairs = w // LANES
    nblk = s // BLK

    def body(q_ref, k_ref, v_ref, do_ref, out_ref, lse_ref, dq_ref, dk_ref, dv_ref, qs, ks, vs, dos, dls, lss, dqd, dkd, dvd, dln):
        row, lane = _iotas()
        same_head = jnp.where((row < HEAD_DIM) == (lane < HEAD_DIM), 1.0, 0.0).astype(BF16)

        def delta_blk(n, carry):
            r0 = pl.multiple_of(n * BLK, BLK)
            dln[pl.ds(r0, BLK), :] = _dot_exact_lhs(do_ref[pl.ds(r0, BLK), :] * out_ref[pl.ds(r0, BLK), :], same_head)
            return carry

        lax.fori_loop(0, nblk, delta_blk, 0, unroll=2)
        for pi, stride in enumerate(DIL_STRIDES):
            per = (s // stride) // BLK
            _deinterleave(qs, q_ref, stride, s, BF16)
            _deinterleave(ks, k_ref, stride, s, BF16)
            _deinterleave(vs, v_ref, stride, s, BF16)
            _deinterleave(dos, do_ref, stride, s, BF16)
            _deinterleave(dls, dln, stride, s, F32)
            _deinterleave(lss, lse_ref, stride, s, F32)

            def block(b, carry):
                r0 = pl.multiple_of(b * BLK, BLK)
                rp = pl.multiple_of(jnp.maximum(b - 1, 0) * BLK, BLK)
                first = b % per == 0
                mc, mp = _band_masks(row, lane, first)
                q, dov = qs[pl.ds(r0, BLK), :], dos[pl.ds(r0, BLK), :]
                kc, kp, vc, vp = ks[pl.ds(r0, BLK), :], ks[pl.ds(rp, BLK), :], vs[pl.ds(r0, BLK), :], vs[pl.ds(rp, BLK), :]
                lse_t, dl_t = lss[pl.ds(r0, BLK), :], dls[pl.ds(r0, BLK), :]
                dq = jnp.zeros((BLK, LANES), F32)
                dkc = jnp.zeros((BLK, LANES), F32)
                dkp = jnp.zeros((BLK, LANES), F32)
                dvc = jnp.zeros((BLK, LANES), F32)
                dvp = jnp.zeros((BLK, LANES), F32)
                for a in range(2):
                    hm = (lane < HEAD_DIM) if a == 0 else (lane >= HEAD_DIM)
                    pick = lane == a * HEAD_DIM
                    qa = jnp.where(hm, q.astype(F32), 0.0).astype(BF16)
                    doa = jnp.where(hm, dov.astype(F32), 0.0).astype(BF16)
                    lse_a = jnp.sum(jnp.where(pick, lse_t, 0.0), axis=1, keepdims=True)
                    dl_a = jnp.sum(jnp.where(pick, dl_t, 0.0), axis=1, keepdims=True)
                    pc = jnp.where(mc, jnp.exp(_dot_nt(qa, kc) - lse_a), 0.0)
                    pp = jnp.where(mp, jnp.exp(_dot_nt(qa, kp) - lse_a), 0.0)
                    dsc = (pc * (_dot_nt(doa, vc) - dl_a)).astype(BF16)
                    dsp = (pp * (_dot_nt(doa, vp) - dl_a)).astype(BF16)
                    dq = jnp.where(hm, _dot(dsc, kc) + _dot(dsp, kp), dq)
                    dkc += _dot_tn(dsc, qa)
                    dkp += _dot_tn(dsp, qa)
                    dvc += _dot_tn(pc.astype(BF16), doa)
                    dvp += _dot_tn(pp.astype(BF16), doa)
                dqd[pl.ds(r0, BLK), :] = dq
                dkd[pl.ds(r0, BLK), :] = dkc
                dvd[pl.ds(r0, BLK), :] = dvc

                @pl.when(jnp.logical_not(first))
                def _():
                    dkd[pl.ds(rp, BLK), :] += dkp
                    dvd[pl.ds(rp, BLK), :] += dvp

                return carry

            lax.fori_loop(0, nblk, block, 0, unroll=2)
            length = s // stride
            for dst, src in ((dq_ref, dqd), (dk_ref, dkd), (dv_ref, dvd)):
                for r in range(stride):
                    if stride == 1:
                        dst[...] = src[...]
                    else:
                        dst[pl.ds(r, length, stride=stride), :] += src[r * length:(r + 1) * length, :]

    colspec = pl.BlockSpec((s, LANES), lambda p: (0, p))
    do_spec = pl.BlockSpec((s, LANES), lambda p: (0, do_off + p))
    o3 = jax.ShapeDtypeStruct((s, w), F32)
    scratch = [pltpu.VMEM((s, LANES), BF16)] * 4 + [pltpu.VMEM((s, LANES), F32)] * 6
    outs, rode = call_with_rider(body, name, rider, [qd, kd, vd, do, out, lse], [colspec, colspec, colspec, do_spec, colspec, colspec],
                                 [o3, o3, o3], [colspec, colspec, colspec], scratch, (npairs,))
    return (*outs, rode)


def adamw(w, g, m, v, name):
    rows, cols = w.shape
    rb = min(rows, ROWS)
    c1 = 1.0 - ADAM_B1 ** ADAM_STEP
    c2 = 1.0 - ADAM_B2 ** ADAM_STEP

    def body(w_ref, g_ref, m_ref, v_ref, d_ref, mo_ref, vo_ref):
        gv = g_ref[...]
        mn = ADAM_B1 * m_ref[...] + (1.0 - ADAM_B1) * gv
        vn = ADAM_B2 * v_ref[...] + (1.0 - ADAM_B2) * (gv * gv)
        d_ref[...] = -ADAM_LR * ((mn / c1) / (jnp.sqrt(vn / c2) + ADAM_EPS) + ADAM_WD * w_ref[...])
        mo_ref[...] = mn
        vo_ref[...] = vn

    spec = _row_spec(cols, rb)
    out = jax.ShapeDtypeStruct((rows, cols), F32)
    return pl.pallas_call(
        body, name=name, out_shape=(out, out, out), grid=(rows // rb,), in_specs=[spec] * 4, out_specs=(spec,) * 3,
        compiler_params=_params(("parallel",)),
    )(w, g, m, v)


def _prefetch_call(body, name, scalar, ops, grid, in_specs, out_specs, out_shape, sem):
    spec = pltpu.PrefetchScalarGridSpec(num_scalar_prefetch=1, grid=grid, in_specs=in_specs, out_specs=out_specs)
    return pl.pallas_call(body, name=name, grid_spec=spec, out_shape=out_shape, compiler_params=_params(sem))(scalar, *ops)


def pair_sum(g, got, core, name):
    nc, r, c = g.shape
    rh = r // 2

    def body(core_ref, g_ref, got_ref, o_ref):
        o_ref[...] = (g_ref[...].astype(F32) + got_ref[...].astype(F32)).astype(BF16)

    blk = lambda rows_of: pl.BlockSpec((None, rh, c), rows_of)
    return _prefetch_call(
        body, name, core, (g, got), (nc,),
        [blk(lambda j, core_ref: (j, core_ref[0], 0)), blk(lambda j, core_ref: (j, 0, 0))],
        blk(lambda j, core_ref: (j, 0, 0)), jax.ShapeDtypeStruct((nc, rh, c), BF16), ("parallel",))


def chip_sum(pair, got, chip, layer, into, name):
    _, rh, c = pair.shape

    def body(chip_ref, p_ref, a_ref, b_ref, c_ref, old_ref, o_ref):
        o_ref[...] = ((p_ref[...].astype(F32) + a_ref[...].astype(F32)) + b_ref[...].astype(F32)) + c_ref[...].astype(F32)

    arrival = lambda k: pl.BlockSpec((None, rh, c), lambda i, chip_ref: (k, 0, 0))
    spec = pltpu.PrefetchScalarGridSpec(
        num_scalar_prefetch=1, grid=(1,),
        in_specs=[pl.BlockSpec((None, rh, c), lambda i, chip_ref: (chip_ref[0], 0, 0)), arrival(0), arrival(1), arrival(2), _ANY],
        out_specs=pl.BlockSpec((None, rh, c), lambda i, chip_ref: (layer, 0, 0)))
    return pl.pallas_call(body, name=name, grid_spec=spec, out_shape=jax.ShapeDtypeStruct(into.shape, into.dtype),
                          input_output_aliases={5: 0}, compiler_params=_params(("arbitrary",)))(chip, pair, got, got, got, into)


def adamw_family(w, m, v, g_mine, g_other, core, name):
    nl, r, c = w.shape
    rh = r // 2
    nb = 4 if rh % 512 == 0 else (2 if rh % 16 == 0 and rh > 256 else 1)
    rb = rh // nb
    c1 = 1.0 - ADAM_B1 ** ADAM_STEP
    c2 = 1.0 - ADAM_B2 ** ADAM_STEP

    def body(core_ref, w_ref, m_ref, v_ref, gm_ref, go_ref, g_ref, d_ref, mo_ref, vo_ref):
        gv = jnp.where(pl.program_id(1) == core_ref[0], gm_ref[...], go_ref[...])
        mn = ADAM_B1 * m_ref[...] + (1.0 - ADAM_B1) * gv
        vn = ADAM_B2 * v_ref[...] + (1.0 - ADAM_B2) * (gv * gv)
        g_ref[...] = gv
        d_ref[...] = -ADAM_LR * ((mn / c1) / (jnp.sqrt(vn / c2) + ADAM_EPS) + ADAM_WD * w_ref[...])
        mo_ref[...] = mn
        vo_ref[...] = vn

    full = pl.BlockSpec((None, rb, c), lambda l, h, i, core_ref: (l, h * nb + i, 0))
    half = pl.BlockSpec((None, rb, c), lambda l, h, i, core_ref: (l, i, 0))
    out = jax.ShapeDtypeStruct((nl, r, c), F32)
    return _prefetch_call(body, name, core, (w, m, v, g_mine, g_other), (nl, 2, nb), [full, full, full, half, half],
                          (full, full, full, full), (out, out, out, out), ("parallel", "parallel", "parallel"))


def _coords():
    return lax.axis_index("x"), lax.axis_index("y"), lax.axis_index("c")


def _other_chips(x, y):
    return ((1 - x, y), (x, 1 - y), (1 - x, 1 - y))


_ANY = pl.BlockSpec(memory_space=pl.ANY)


def _exchange_call(body, name, arrays, out_shapes, n_copies, n_local=0):
    n = len(arrays)

    def wrapped(*refs):
        body(refs[:n], refs[n:n + len(out_shapes)], *refs[n + len(out_shapes):])

    scratch = [pltpu.SemaphoreType.DMA((n_copies,)), pltpu.SemaphoreType.DMA((n_copies,))]
    if n_local:
        scratch.append(pltpu.SemaphoreType.DMA((n_local,)))
    return pl.pallas_call(
        wrapped, name=name, out_shape=tuple(out_shapes), in_specs=[_ANY] * n, out_specs=tuple([_ANY] * len(out_shapes)),
        scratch_shapes=scratch, compiler_params=_params(),
    )(*arrays)


def _remote(send_sems, recv_sems, n, src, dst, to):
    return pltpu.make_async_remote_copy(src_ref=src, dst_ref=dst, send_sem=send_sems.at[n], recv_sem=recv_sems.at[n],
                                        device_id=to, device_id_type=MESH)


class Rider:
    def __init__(self, arrays, out_shapes, n_remote, n_local, copies, then=None):
        self.arrays, self.out_shapes, self.n_remote, self.n_local = list(arrays), list(out_shapes), n_remote, n_local
        self.copies, self.then = copies, then

    def sems(self):
        return [pltpu.SemaphoreType.DMA((self.n_remote,)), pltpu.SemaphoreType.DMA((self.n_remote,)),
                pltpu.SemaphoreType.DMA((max(self.n_local, 1),))]

    def run(self, name):
        n, no = len(self.arrays), len(self.out_shapes)

        def body(*refs):
            for stage in (self.copies, self.then):
                if stage is not None:
                    cps = stage(refs[:n], refs[n:n + no], *refs[n + no:])
                    for cp in cps:
                        cp.start()
                    for cp in cps:
                        cp.wait()

        return pl.pallas_call(
            body, name=name, out_shape=tuple(self.out_shapes), in_specs=[_ANY] * n, out_specs=tuple([_ANY] * no),
            scratch_shapes=self.sems(), compiler_params=_params(),
        )(*self.arrays)


def ride(rider, body, n_in, n_out, grid):
    if rider is None:
        return body
    ni, no = len(rider.arrays), len(rider.out_shapes)

    def wrapped(*refs):
        ins, r_in = refs[:n_in], refs[n_in:n_in + ni]
        outs = refs[n_in + ni:n_in + ni + n_out]
        r_out = refs[n_in + ni + n_out:n_in + ni + n_out + no]
        rest = refs[n_in + ni + n_out + no:]
        scratch, sems = rest[:len(rest) - 3], rest[len(rest) - 3:]
        step, total = 0, 1
        for a, g in enumerate(grid):
            step, total = step * g + pl.program_id(a), total * g
        assert total >= 3
        relay_at = (7 * total) // 8 if rider.then is not None else total - 1

        @pl.when(step == 0)
        def _():
            for cp in rider.copies(r_in, r_out, *sems):
                cp.start()

        body(*ins, *outs, *scratch)

        @pl.when(step == relay_at)
        def _():
            for cp in rider.copies(r_in, r_out, *sems):
                cp.wait()
            if rider.then is not None:
                for cp in rider.then(r_in, r_out, *sems):
                    cp.start()

        if rider.then is not None:
            @pl.when(step == total - 1)
            def _():
                for cp in rider.then(r_in, r_out, *sems):
                    cp.wait()

    return wrapped


def call_with_rider(body, name, rider, ops, in_specs, out_shape, out_specs, scratch, grid):
    n_in, n_out = len(ops), len(out_shape)
    ops, in_specs, out_shape, out_specs, scratch = list(ops), list(in_specs), list(out_shape), list(out_specs), list(scratch)
    if rider is not None:
        ops += rider.arrays
        in_specs += [_ANY] * len(rider.arrays)
        out_shape += rider.out_shapes
        out_specs += [_ANY] * len(rider.out_shapes)
        scratch += rider.sems()
    res = pl.pallas_call(
        ride(rider, body, n_in, n_out, grid), name=name, out_shape=tuple(out_shape), grid=grid, in_specs=in_specs,
        out_specs=tuple(out_specs), scratch_shapes=scratch, compiler_params=_params(("arbitrary",) * len(grid)),
    )(*ops)
    return tuple(res[:n_out]), list(res[n_out:])


def gather_rider(shards):
    nf = len(shards)
    half = lambda ref, which: pl.ds(which * (ref.shape[-2] // 2), ref.shape[-2] // 2)

    def copies(s_refs, o_refs, send_sems, recv_sems, local_sems):
        x, y, c = _coords()
        me = 2 * x + y
        cps = [pltpu.make_async_copy(s_refs[f], o_refs[f].at[me], local_sems.at[f]) for f in range(nf)]
        for k, (px, py) in enumerate(_other_chips(x, y)):
            for f in range(nf):
                rows = half(s_refs[f], c)
                cps.append(_remote(send_sems, recv_sems, k * nf + f, s_refs[f].at[rows], o_refs[f].at[me, rows], (px, py, c)))
        return cps

    def relay(s_refs, o_refs, send_sems, recv_sems, local_sems):
        x, y, c = _coords()
        cps = []
        for k, (px, py) in enumerate(_other_chips(x, y)):
            for f in range(nf):
                landed = o_refs[f].at[2 * px + py, half(s_refs[f], c)]
                cps.append(_remote(send_sems, recv_sems, (3 + k) * nf + f, landed, landed, (x, y, 1 - c)))
        return cps

    return Rider(shards, [jax.ShapeDtypeStruct((N_CHIPS,) + sh.shape, sh.dtype) for sh in shards], 6 * nf, nf, copies, relay)


def scatter_rider(pairs):
    nf = len(pairs)

    def copies(p_refs, o_refs, send_sems, recv_sems, local_sems):
        x, y, c = _coords()
        cps = []
        for k, (px, py) in enumerate(_other_chips(x, y)):
            for f in range(nf):
                cps.append(_remote(send_sems, recv_sems, k * nf + f, p_refs[f].at[2 * px + py], o_refs[f].at[k], (px, py, c)))
        return cps

    return Rider(pairs, [jax.ShapeDtypeStruct((3,) + p.shape[1:], p.dtype) for p in pairs], 3 * nf, 0, copies)


def pair_swap(grads, name):
    def body(g_refs, o_refs, send_sems, recv_sems):
        x, y, c = _coords()
        cps = []
        for f, g_ref in enumerate(g_refs):
            rh = g_ref.shape[1] // 2
            cps.append(_remote(send_sems, recv_sems, f, g_ref.at[:, pl.ds((1 - c) * rh, rh), :], o_refs[f], (x, y, 1 - c)))
        for cp in cps:
            cp.start()
        for cp in cps:
            cp.wait()

    outs = [jax.ShapeDtypeStruct((g.shape[0], g.shape[1] // 2, g.shape[2]), g.dtype) for g in grads]
    return _exchange_call(body, name, grads, outs, len(grads))


def half_swap(halves, name):
    def body(h_refs, o_refs, send_sems, recv_sems):
        x, y, c = _coords()
        cps = [_remote(send_sems, recv_sems, f, h_ref, o_refs[f], (x, y, 1 - c)) for f, h_ref in enumerate(h_refs)]
        for cp in cps:
            cp.start()
        for cp in cps:
            cp.wait()

    return _exchange_call(body, name, halves, [jax.ShapeDtypeStruct(h.shape, h.dtype) for h in halves], len(halves))


def allsum_small(part, name):
    def body(p_ref, tot_ref, all_ref, send_sems, recv_sems):
        x, y, c = _coords()
        me, sibling = (x, y, c), (x, y, 1 - c)
        chips = _other_chips(x, y)

        def slot(px, py, pc):
            return all_ref.at[4 * px + 2 * py + pc]

        def copy(k, block, to, src=None):
            return pltpu.make_async_remote_copy(src_ref=slot(*block) if src is None else src, dst_ref=slot(*block),
                                                send_sem=send_sems.at[k], recv_sem=recv_sems.at[k], device_id=to, device_id_type=MESH)

        slot(*me)[...] = p_ref[...]
        first = [copy(0, me, sibling, src=p_ref)] + [copy(1 + j, me, (*chip, c), src=p_ref) for j, chip in enumerate(chips)]
        for cp in first:
            cp.start()
        passed = [copy(4 + j, (*chip, c), sibling) for j, chip in enumerate(chips)]
        for j, chip in enumerate(chips):
            copy(1 + j, (*chip, c), me).wait_recv()
            passed[j].start()
        copy(0, sibling, me).wait_recv()
        for j, chip in enumerate(chips):
            copy(4 + j, (*chip, 1 - c), me).wait_recv()
        for cp in first + passed:
            cp.wait_send()
        tot = all_ref[0]
        for d in range(1, 8):
            tot = tot + all_ref[d]
        tot_ref[...] = tot

    vm = pl.BlockSpec(memory_space=pltpu.VMEM)
    return pl.pallas_call(
        body, name=name, out_shape=jax.ShapeDtypeStruct(part.shape, F32), in_specs=[vm], out_specs=vm,
        scratch_shapes=[pltpu.VMEM((8,) + part.shape, F32), pltpu.SemaphoreType.DMA((7,)), pltpu.SemaphoreType.DMA((7,))],
        compiler_params=_params(),
    )(part)


QKVF_COLS = 772
QKVF_PAD = 896
FORWARD_CARRY = {0: (("fi0", "fo0"), ("qkv1", "o1", "fo1")), 1: (("fi1", "qkv2", "o2"), ()),
                 2: (("fi2", "fo2"), ("qkv3", "o3", "fo3")), 3: (("fi3",), ())}


def _tables_for(s):
    return _rot_tables(s)


def layer_families(layer):
    return (0, 1, layer // 2) if layer % 2 == 0 else (2, 3, layer // 2)


class GradientExchange:
    def __init__(self):
        self.core = lax.axis_index("c").astype(jnp.int32).reshape(1)
        self.chip = (2 * lax.axis_index("x") + lax.axis_index("y")).astype(jnp.int32).reshape(1)
        self.pairs, self.arrived, self.pending = {}, {}, []

    def add(self, items, tag):
        got = pair_swap([g for _, _, g in items], f"grad_pair_swap_{tag}")
        for (fam, li, g), r in zip(items, got):
            self.pairs[(fam, li)] = pair_sum(g, r, self.core, f"grad_pair_sum_{fam}_{li}")
            self.pending.append((fam, li))

    def rider(self, only=None):
        keys = [k for k in self.pending if only is None or k in only]
        self.pending = [k for k in self.pending if k not in keys]
        return (scatter_rider([self.pairs[k] for k in keys]) if keys else None), keys

    def landed(self, keys, outs):
        self.arrived.update(zip(keys, outs))

    def finish(self, weights, moments1, moments2):
        last, keys = self.rider()
        if last is not None:
            self.landed(keys, last.run("grad_chip_scatter_last"))
        mine = []
        for fam, w in enumerate(weights):
            buf = jnp.zeros((w.shape[0], w.shape[1] // 2, w.shape[2]), F32)
            for li in range(w.shape[0]):
                buf = chip_sum(self.pairs[(fam, li)], self.arrived[(fam, li)], self.chip, li, buf, f"grad_chip_sum_{fam}_{li}")
            mine.append(buf)
        other = half_swap(mine, "grad_half_swap")
        return [adamw_family(w, m, v, gm, go, self.core, f"adamw_{f}")
                for f, (w, m, v, gm, go) in enumerate(zip(weights, moments1, moments2, mine, other))]


class KeepGradients:
    def __init__(self):
        self.grads = {}

    def add(self, items, tag):
        for fam, li, g in items:
            self.grads[(fam, li)] = g

    def rider(self, only=None):
        return None, []

    def landed(self, keys, outs):
        pass


def kernel(x, norm_mix, w_qkv_even, w_o_even, w_qkvf_odd, b_forget, w_o_odd, norm_ffn, w_ffn_in, w_ffn_out, norm_final, loss_target, m_norm_mix, m_w_qkv_even, m_w_o_even, m_w_qkvf_odd, m_b_forget, m_w_o_odd, m_norm_ffn, m_w_ffn_in, m_w_ffn_out, m_norm_final, v_norm_mix, v_w_qkv_even, v_w_o_even, v_w_qkvf_odd, v_b_forget, v_w_o_odd, v_norm_ffn, v_w_ffn_in, v_w_ffn_out, v_norm_final):
    w_shards = [w_qkv_even, w_o_even, w_qkvf_odd, w_o_odd, w_ffn_in, w_ffn_out]
    shards = [w.astype(BF16) for w in w_shards]
    tables = _tables_for(x.shape[1])
    bias_pad = jnp.pad(b_forget, ((0, 0), (0, LANES - N_HEADS)))

    mine = {}
    for layer in range(DEPTH):
        fam_qkv, fam_o, li = layer_families(layer)
        mine.update({f"qkv{layer}": shards[fam_qkv][li], f"o{layer}": shards[fam_o][li],
                     f"fi{layer}": shards[4][layer], f"fo{layer}": shards[5][layer]})
    fetch = lambda names: gather_rider([mine[n] for n in names])
    have = dict(zip(("qkv0", "o0"), fetch(("qkv0", "o0")).run("gather_first")))
    saved, cur = [], x[0]
    for layer in range(DEPTH):
        carry, side_carry = FORWARD_CARRY[layer]
        cur, keep = forward_layer(layer, cur, have, norm_mix[layer:layer + 1], norm_ffn[layer:layer + 1], tables,
                                  bias_pad[layer // 2:layer // 2 + 1], fetch, carry, side_carry)
        saved.append(keep)

    dcur, g_final, loss_part = loss_head(cur, norm_final.reshape(1, D_MODEL), loss_target[0], "loss_head")

    exchange = GradientExchange()
    g_mix, g_ffn, g_bias = [None] * DEPTH, [None] * DEPTH, [None] * (DEPTH // 2)
    for layer in reversed(range(DEPTH)):
        dcur, g_mix[layer], g_ffn[layer], g_b = backward_layer(layer, dcur, saved[layer], norm_mix[layer:layer + 1],
                                                               norm_ffn[layer:layer + 1], tables, bias_pad[layer // 2:layer // 2 + 1], exchange)
        if g_b is not None:
            g_bias[layer // 2] = g_b

    zero_row = jnp.zeros((1, D_MODEL), F32)
    pad16 = lambda v: jnp.pad(v, (0, D_MODEL - v.shape[0]))[None, :]
    small_rows = lambda mix, ffn, fin, bias, last: jnp.concatenate(
        [r.reshape(1, D_MODEL) for r in mix] + [r.reshape(1, D_MODEL) for r in ffn] + [fin.reshape(1, D_MODEL)]
        + [pad16(b) for b in bias] + [last] + [zero_row] * (SMALL_ROWS - 12), axis=0)
    loss_row = pad16(loss_part[0, :1])
    small_g = allsum_small(small_rows(g_mix, g_ffn, g_final, g_bias, loss_row), "allsum_small")
    loss = small_g[11, 0]
    small_g = small_g.at[11].set(0.0)
    sw = small_rows(list(norm_mix), list(norm_ffn), norm_final, list(b_forget), zero_row)
    sm = small_rows(list(m_norm_mix), list(m_norm_ffn), m_norm_final, list(m_b_forget), zero_row)
    sv = small_rows(list(v_norm_mix), list(v_norm_ffn), v_norm_final, list(v_b_forget), zero_row)
    sd, snm, snv = adamw(sw, small_g, sm, sv, "adamw_small")

    def small_out(a):
        return a[0:4], a[8, :], a[9:11, :N_HEADS], a[4:8]

    widen = lambda t: jnp.pad(t, ((0, 0), (0, 0), (0, QKVF_PAD - QKVF_COLS)))
    padded = lambda ws: [widen(t) if f == 2 else t for f, t in enumerate(ws)]
    big = exchange.finish(
        padded(w_shards), padded([m_w_qkv_even, m_w_o_even, m_w_qkvf_odd, m_w_o_odd, m_w_ffn_in, m_w_ffn_out]),
        padded([v_w_qkv_even, v_w_o_even, v_w_qkvf_odd, v_w_o_odd, v_w_ffn_in, v_w_ffn_out]))

    def outputs(small, which):
        mix, fin, bias, ffn = small_out(small)
        qkv_e, o_e, qkvf, o_o, fi, fo = [big[f][which][:, :, :QKVF_COLS] if f == 2 else big[f][which] for f in range(6)]
        return [mix, qkv_e, o_e, qkvf, bias, o_o, ffn, fi, fo, fin]

    return (loss, dcur[None], *outputs(small_g, 0), *outputs(sd, 1), *outputs(snm, 2), *outputs(snv, 3))


def _chip_tile(rows, cols, at):
    return pl.BlockSpec((None, rows, cols), at)


def forward_layer(layer, cur, have, mix_gain, ffn_gain, tables, bias_row, fetch=None, carry=(), side_carry=()):
    n = f"l{layer}"
    s = cur.shape[0]
    w_qkv, w_o = have[f"qkv{layer}"], have[f"o{layer}"]
    rider = fetch(carry) if carry else None
    side_rider = fetch(side_carry) if side_carry else None
    h1 = rmsnorm_fwd(cur, mix_gain, n + "_norm_mix")
    keep = {"x": cur, "h1": h1, "w_o": w_o.reshape(D_ATTN, D_MODEL)}
    side = []
    if layer % 2 == 0:
        qkv = matmul(h1, w_qkv, "nn", BF16, n + "_qkv", 1024, 768, 1024, mnk=(s, 3 * D_ATTN, D_MODEL),
                     b_spec=_chip_tile(D_MODEL, 768, lambda i, j, kk: (j, 0, 0)))
        o_sb, st, rode = causal_fwd(qkv, 4, "sb", n + "_sb_fwd", rider=rider)
        qd, kd, vd = rotary_prep(qkv, tables, n + "_rotary")
        o_dil, lse_dil, side = dilated_fwd(qd, kd, vd, n + "_dil_fwd", rider=side_rider)
        attn = jnp.concatenate([o_sb, o_dil], axis=1).astype(BF16)
        keep.update(qd=qd, kd=kd, vd=vd, o_dil=o_dil, lse_dil=lse_dil, w_qkv=w_qkv)
    else:
        natural = jnp.transpose(w_qkv, (1, 0, 2)).reshape(D_MODEL, N_CHIPS * QKVF_COLS)
        w_gate = jnp.pad(natural[:, 3 * D_ATTN:], ((0, 0), (0, LANES - N_HEADS)))
        qkv = matmul(h1, natural[:, :3 * D_ATTN], "nn", BF16, n + "_qkv", 1024, 768, 1024)
        fl = matmul(h1, w_gate, "nn", F32, n + "_fgate", 512, LANES, 1024)
        cum = forget_fwd(fl, bias_row, n + "_forget_fwd")
        f_heads = cum[:, :N_HEADS].T
        fq = jnp.broadcast_to(f_heads[:, :, None], (N_HEADS, s, LANES))
        fk = f_heads.reshape(N_HEADS // 2, 2, s)
        attn, st, rode = causal_fwd(qkv, 8, "fox", n + "_fox_fwd", fq=fq, fk=fk, rider=rider)
        attn = attn.astype(BF16)
        keep.update(fl=fl, fq=fq, fk=fk, w_qkv=jnp.concatenate([natural[:, :3 * D_ATTN], w_gate], axis=1))
    have.update(zip(carry, rode))
    have.update(zip(side_carry, side))
    w_fi, w_fo = have[f"fi{layer}"], have[f"fo{layer}"].reshape(D_FF, D_MODEL)
    mid = matmul(attn, keep["w_o"], "nn", F32, n + "_attn_out", 1024, 1024, 1024, res=cur)
    h2 = rmsnorm_fwd(mid, ffn_gain, n + "_norm_ffn")
    gu = matmul(h2, w_fi, "nn", BF16, n + "_ffn_in", 1024, 1408, 1024, mnk=(s, 2 * D_FF, D_MODEL),
                b_spec=_chip_tile(D_MODEL, 1408, lambda i, j, kk: (j, 0, 0)))
    act = swiglu_fwd(gu, n + "_swiglu")
    out = matmul(act, w_fo, "nn", F32, n + "_ffn_out", 512, 1024, D_FF, res=mid)
    keep.update(qkv=qkv, st=st, attn=attn, mid=mid, h2=h2, gu=gu, act=act, w_fi=w_fi, w_fo=w_fo)
    return out, keep


def backward_layer(layer, dcur, kp, mix_gain, ffn_gain, tables, bias_row, exchange):
    n = f"l{layer}"
    s = dcur.shape[0]
    fam_qkv, fam_o, li = layer_families(layer)
    g_fo = matmul(kp["act"], dcur, "tn", BF16, n + "_d_w_ffn_out", 1408, 1024, s)
    dact = matmul(dcur, kp["w_fo"], "nt", BF16, n + "_d_act", 1024, 1408, 1024)
    dgu = swiglu_bwd(kp["gu"], dact, n + "_d_swiglu")
    g_fi = matmul(kp["h2"], dgu, "tn", BF16, n + "_d_w_ffn_in", 1024, 1408, 2048, mnk=(D_MODEL, 2 * D_FF, s),
                  o_spec=_chip_tile(D_MODEL, 1408, lambda i, j, kk: (j, 0, 0)), out_shape=(N_CHIPS, D_MODEL, 1408))
    all_chips = lambda cols: pl.BlockSpec((N_CHIPS, D_MODEL, cols), lambda i, j, kk: (0, 0, 0))
    dh2 = matmul(dgu, kp["w_fi"], "nt", F32, n + "_d_h2", 512, 1024, 2 * D_FF, mnk=(s, D_MODEL, 2 * D_FF), b_spec=all_chips(1408))
    dmid, g_ffn = rmsnorm_bwd(kp["mid"], ffn_gain, dh2, dcur, n + "_d_norm_ffn")
    g_o = matmul(kp["attn"], dmid, "tn", BF16, n + "_d_w_o", 1024, 1024, s)
    dattn = matmul(dmid, kp["w_o"], "nt", F32, n + "_d_attn", 1024, 1024, 1024)
    exchange.add([(5, layer, g_fo.reshape(N_CHIPS, D_FF // N_CHIPS, D_MODEL)), (4, layer, g_fi),
                  (fam_o, li, g_o.reshape(N_CHIPS, D_ATTN // N_CHIPS, D_MODEL))], f"l{layer}_ffn")
    g_bias = None
    if layer % 2 == 0:
        rider, keys = exchange.rider(only=[(4, layer), (fam_o, li)])
        dq_a, dk_a, dv_a, rode = causal_bwd(kp["qkv"], dattn, kp["st"], 4, "sb", n + "_sb_bwd", rider=rider)
        exchange.landed(keys, rode)
        rider, keys = exchange.rider()
        dqd, dkd, dvd, rode = dilated_bwd(kp["qd"], kp["kd"], kp["vd"], dattn, kp["o_dil"], kp["lse_dil"], 4, n + "_dil_bwd",
                                          rider=rider)
        dq_b, dk_b, dv_b = rotary_bwd(dqd, dkd, dvd, tables, n + "_d_rotary")
        dproj = jnp.concatenate([dq_a, dq_b, dk_a.astype(BF16), dk_b, dv_a.astype(BF16), dv_b], axis=1)
        g_qkv = matmul(kp["h1"], dproj, "tn", BF16, n + "_d_w_qkv", 1024, 768, 2048, mnk=(D_MODEL, 3 * D_ATTN, s),
                       o_spec=_chip_tile(D_MODEL, 768, lambda i, j, kk: (j, 0, 0)), out_shape=(N_CHIPS, D_MODEL, 768))
        dh1 = matmul(dproj, kp["w_qkv"], "nt", F32, n + "_d_h1", 512, 1024, 3 * D_ATTN, mnk=(s, D_MODEL, 3 * D_ATTN),
                     b_spec=all_chips(768))
    else:
        rider, keys = exchange.rider()
        dq_f, dk_f, dv_f, dfk, rode = causal_bwd(kp["qkv"], dattn, kp["st"], 8, "fox", n + "_fox_bwd", fq=kp["fq"], fk=kp["fk"],
                                                 rider=rider)
        dcum = jnp.pad(dfk.reshape(N_HEADS, s).T, ((0, 0), (0, LANES - N_HEADS)))
        dfl, dbias = forget_bwd(kp["fl"], bias_row, dcum, n + "_forget_bwd")
        g_bias = dbias[0, :N_HEADS]
        dproj = jnp.concatenate([dq_f, dk_f.astype(BF16), dv_f.astype(BF16), dfl.astype(BF16)], axis=1)
        by_chip = dproj[:, :N_CHIPS * QKVF_COLS].reshape(s, N_CHIPS, QKVF_COLS)
        by_chip = jnp.pad(by_chip, ((0, 0), (0, 0), (0, QKVF_PAD - QKVF_COLS))).reshape(s, N_CHIPS * QKVF_PAD)
        g_qkv = matmul(kp["h1"], by_chip, "tn", BF16, n + "_d_w_qkv", 1024, QKVF_PAD, 2048, mnk=(D_MODEL, N_CHIPS * QKVF_PAD, s),
                       o_spec=_chip_tile(D_MODEL, QKVF_PAD, lambda i, j, kk: (j, 0, 0)), out_shape=(N_CHIPS, D_MODEL, QKVF_PAD))
        dh1 = matmul(dproj, kp["w_qkv"], "nt", F32, n + "_d_h1", 512, 1024, dproj.shape[1])
    exchange.landed(keys, rode)
    exchange.add([(fam_qkv, li, g_qkv)], f"l{layer}_qkv")
    dx, g_mix = rmsnorm_bwd(kp["x"], mix_gain, dh1, dmid, n + "_d_norm_mix")
    return dx, g_mix, g_ffn, g_bias


def local_step(xs, target, norm_mix, norm_ffn, norm_final, b_forget, layer_weights):
    tables = _tables_for(xs.shape[0])
    bias_pad = jnp.pad(b_forget, ((0, 0), (0, LANES - N_HEADS)))
    saved, cur, have = [], xs, {}
    for layer in range(DEPTH):
        have.update(zip((f"qkv{layer}", f"o{layer}", f"fi{layer}", f"fo{layer}"), layer_weights[layer]))
        cur, keep = forward_layer(layer, cur, have, norm_mix[layer:layer + 1], norm_ffn[layer:layer + 1], tables,
                                  bias_pad[layer // 2:layer // 2 + 1])
        saved.append(keep)
    dcur, g_final, loss_part = loss_head(cur, norm_final.reshape(1, D_MODEL), target, "loss_head")
    keeper = KeepGradients()
    g_mix, g_ffn, g_bias = [None] * DEPTH, [None] * DEPTH, [None] * (DEPTH // 2)
    for layer in reversed(range(DEPTH)):
        dcur, g_mix[layer], g_ffn[layer], g_b = backward_layer(layer, dcur, saved[layer], norm_mix[layer:layer + 1],
                                                               norm_ffn[layer:layer + 1], tables, bias_pad[layer // 2:layer // 2 + 1], keeper)
        if g_b is not None:
            g_bias[layer // 2] = g_b
    return dcur, keeper.grads, (g_mix, g_ffn, g_final, g_bias), loss_part
```

```python
import functools

import jax
import jax.numpy as jnp
from jax import lax
from jax.experimental import pallas as pl
from jax.experimental.pallas import tpu as pltpu

F32 = jnp.float32
BF16 = jnp.bfloat16
MESH = pl.DeviceIdType.MESH

D_MODEL = 1024
DEPTH = 4
HEAD_DIM = 64
N_HEADS = 16
D_ATTN = 1024
D_FF = 2816
ROPE_THETA = 500000.0
ROT_HALF = 8
RMS_EPS = 1e-5
DIL_STRIDES = (1, 4, 16)
ADAM_LR, ADAM_B1, ADAM_B2, ADAM_EPS, ADAM_WD, ADAM_STEP = 0.001, 0.9, 0.999, 1e-8, 0.01, 10

LANES = 128
BLK = 128
VMEM_LIMIT = 56 * 1024 * 1024
NEG = -1e30
N_CHIPS = 4
FLAT_COLS = 1024
FLAT_ROWS = 12800
HALF_ROWS = FLAT_ROWS // 2
SMALL_ROWS = 16


def _params(sem=None):
    return pltpu.CompilerParams(dimension_semantics=sem, vmem_limit_bytes=VMEM_LIMIT)


def _dot(a, b):
    return lax.dot_general(a, b, (((1,), (0,)), ((), ())), preferred_element_type=F32)


def _dot_nt(a, b):
    return lax.dot_general(a, b, (((1,), (1,)), ((), ())), preferred_element_type=F32)


def _dot_tn(a, b):
    return lax.dot_general(a, b, (((0,), (0,)), ((), ())), preferred_element_type=F32)


def _split3(x):
    x1 = x.astype(BF16)
    r1 = x - x1.astype(F32)
    x2 = r1.astype(BF16)
    x3 = (r1 - x2.astype(F32)).astype(BF16)
    return x1, x2, x3


def _dot_exact_lhs(x, t):
    x1, x2, x3 = _split3(x)
    return _dot(x1, t) + _dot(x2, t) + _dot(x3, t)


def _dot_exact_rhs(t, x):
    x1, x2, x3 = _split3(x)
    return _dot(t, x1) + _dot(t, x2) + _dot(t, x3)


def _iotas(shape=(BLK, LANES)):
    return lax.broadcasted_iota(jnp.int32, shape, 0), lax.broadcasted_iota(jnp.int32, shape, 1)


_DIMS = {"nn": (((1,), (0,)), ((), ())), "nt": (((1,), (1,)), ((), ())), "tn": (((0,), (0,)), ((), ()))}


def matmul(a, b, mode, out_dtype, name, tm, tn, tk, res=None, mnk=None, b_spec=None, o_spec=None, out_shape=None, into=None):
    if mnk is not None:
        m, n, k = mnk
    elif mode == "nn":
        (m, k), (k2, n) = a.shape, b.shape
    elif mode == "nt":
        (m, k), (n, k2) = a.shape, b.shape
    else:
        (k, m), (k2, n) = a.shape, b.shape
    assert m % tm == 0 and n % tn == 0 and k % tk == 0, (name, a.shape, b.shape)
    nk = k // tk
    a_spec = pl.BlockSpec((tk, tm), lambda i, j, kk: (kk, i)) if mode == "tn" else pl.BlockSpec((tm, tk), lambda i, j, kk: (i, kk))
    if b_spec is None:
        b_spec = pl.BlockSpec((tn, tk), lambda i, j, kk: (j, kk)) if mode == "nt" else pl.BlockSpec((tk, tn), lambda i, j, kk: (kk, j))
    r_spec = pl.BlockSpec((tm, tn), lambda i, j, kk: (i, j))
    if o_spec is None:
        o_spec = r_spec
    dims = _DIMS[mode]
    has_res = res is not None
    n_in = 2 + int(has_res) + int(into is not None)

    def body(*refs):
        a_ref, b_ref = refs[0], refs[1]
        r_ref = refs[2] if has_res else None
        o_ref = refs[n_in]

        def finish(v):
            if has_res:
                v = v + r_ref[...]
            o_ref[...] = v.astype(out_dtype)

        bv = b_ref[...]
        if bv.ndim == 3:
            bv = jnp.concatenate([bv[j] for j in range(bv.shape[0])], axis=1)
        p = lax.dot_general(a_ref[...].astype(BF16), bv.astype(BF16), dims, preferred_element_type=F32)
        if nk == 1:
            finish(p)
        else:
            acc = refs[-1]
            kk = pl.program_id(2)

            @pl.when(kk == 0)
            def _():
                acc[...] = p

            @pl.when(kk > 0)
            def _():
                acc[...] += p

            @pl.when(kk == nk - 1)
            def _():
                finish(acc[...])

    ops = [a, b] + ([res] if has_res else []) + ([into] if into is not None else [])
    specs = [a_spec, b_spec] + ([r_spec] if has_res else []) + ([_ANY] if into is not None else [])
    if into is not None:
        out_shape = jax.ShapeDtypeStruct(into.shape, into.dtype)
    else:
        out_shape = jax.ShapeDtypeStruct((m, n) if out_shape is None else out_shape, out_dtype)
    return pl.pallas_call(
        body, name=name, out_shape=out_shape,
        grid=(m // tm, n // tn, nk), in_specs=specs, out_specs=o_spec,
        scratch_shapes=[pltpu.VMEM((tm, tn), F32)] if nk > 1 else [],
        input_output_aliases={n_in - 1: 0} if into is not None else {},
        compiler_params=_params(("parallel", "parallel", "arbitrary")),
    )(*ops)


ROWS = 256


def _row_spec(cols, rows=ROWS):
    return pl.BlockSpec((rows, cols), lambda i: (i, 0))


def _fix_spec(r, cols):
    return pl.BlockSpec((r, cols), lambda i: (0, 0))


def rmsnorm_fwd(x, g, name):
    s, d = x.shape

    def body(x_ref, g_ref, h_ref):
        xv = x_ref[...]
        rstd = lax.rsqrt(jnp.mean(xv * xv, axis=-1, keepdims=True) + RMS_EPS)
        h_ref[...] = (xv * rstd * g_ref[...]).astype(BF16)

    return pl.pallas_call(
        body, name=name, out_shape=jax.ShapeDtypeStruct((s, d), BF16), grid=(s // ROWS,),
        in_specs=[_row_spec(d), _fix_spec(1, d)], out_specs=_row_spec(d), compiler_params=_params(("parallel",)),
    )(x, g)


def _rms_bwd_math(xv, gv, dh):
    rstd = lax.rsqrt(jnp.mean(xv * xv, axis=-1, keepdims=True) + RMS_EPS)
    xhat = xv * rstd
    u = dh * gv
    dx = rstd * (u - xhat * jnp.mean(u * xhat, axis=-1, keepdims=True))
    return dx, dh * xhat


def rmsnorm_bwd(x, g, dh, dres, name):
    s, d = x.shape

    def body(x_ref, g_ref, dh_ref, dres_ref, dx_ref, dg_ref):
        dx, dgt = _rms_bwd_math(x_ref[...], g_ref[...], dh_ref[...])
        dx_ref[...] = dres_ref[...] + dx
        part = jnp.sum(dgt, axis=0, keepdims=True)

        @pl.when(pl.program_id(0) == 0)
        def _():
            dg_ref[...] = part

        @pl.when(pl.program_id(0) > 0)
        def _():
            dg_ref[...] += part

    return pl.pallas_call(
        body, name=name, out_shape=(jax.ShapeDtypeStruct((s, d), F32), jax.ShapeDtypeStruct((1, d), F32)),
        grid=(s // ROWS,), in_specs=[_row_spec(d), _fix_spec(1, d), _row_spec(d), _row_spec(d)],
        out_specs=(_row_spec(d), _fix_spec(1, d)), compiler_params=_params(("arbitrary",)),
    )(x, g, dh, dres)


def loss_head(x, g, target, name):
    s, d = x.shape

    def body(x_ref, g_ref, t_ref, dx_ref, dg_ref, loss_ref):
        xv, gv = x_ref[...], g_ref[...]
        rstd = lax.rsqrt(jnp.mean(xv * xv, axis=-1, keepdims=True) + RMS_EPS)
        err = xv * rstd * gv - t_ref[...]
        dx, dgt = _rms_bwd_math(xv, gv, err * (1.0 / d))
        dx_ref[...] = dx
        part = jnp.sum(dgt, axis=0, keepdims=True)
        lpart = jnp.full((1, LANES), 0.5 / d, F32) * jnp.sum(err * err)

        @pl.when(pl.program_id(0) == 0)
        def _():
            dg_ref[...] = part
            loss_ref[...] = lpart

        @pl.when(pl.program_id(0) > 0)
        def _():
            dg_ref[...] += part
            loss_ref[...] += lpart

    return pl.pallas_call(
        body, name=name,
        out_shape=(jax.ShapeDtypeStruct((s, d), F32), jax.ShapeDtypeStruct((1, d), F32), jax.ShapeDtypeStruct((1, LANES), F32)),
        grid=(s // ROWS,), in_specs=[_row_spec(d), _fix_spec(1, d), _row_spec(d)],
        out_specs=(_row_spec(d), _fix_spec(1, d), _fix_spec(1, LANES)), compiler_params=_params(("arbitrary",)),
    )(x, g, target)


def swiglu_fwd(gu, name):
    s, f2 = gu.shape
    f = f2 // 2

    def body(gu_ref, a_ref):
        gv, uv = gu_ref[:, :f].astype(F32), gu_ref[:, f:].astype(F32)
        a_ref[...] = (gv * (1.0 / (1.0 + jnp.exp(-gv))) * uv).astype(BF16)

    return pl.pallas_call(
        body, name=name, out_shape=jax.ShapeDtypeStruct((s, f), BF16), grid=(s // ROWS,),
        in_specs=[_row_spec(f2)], out_specs=_row_spec(f), compiler_params=_params(("parallel",)),
    )(gu)


def swiglu_bwd(gu, dact, name):
    s, f2 = gu.shape
    f = f2 // 2

    def body(gu_ref, da_ref, o_ref):
        gv, uv, da = gu_ref[:, :f].astype(F32), gu_ref[:, f:].astype(F32), da_ref[...].astype(F32)
        sg = 1.0 / (1.0 + jnp.exp(-gv))
        o_ref[:, :f] = (da * uv * sg * (1.0 + gv * (1.0 - sg))).astype(BF16)
        o_ref[:, f:] = (da * gv * sg).astype(BF16)

    return pl.pallas_call(
        body, name=name, out_shape=jax.ShapeDtypeStruct((s, f2), BF16), grid=(s // ROWS,),
        in_specs=[_row_spec(f2), _row_spec(f)], out_specs=_row_spec(f2), compiler_params=_params(("parallel",)),
    )(gu, dact)


Q_OFF, K_OFF, V_OFF = 0, 8, 16


KB = 512
BQ = 512
SUB = KB // BLK


def _softplus_parts(z):
    sp = jnp.log(1.0 + jnp.exp(-jnp.abs(z)))
    ls = jnp.minimum(z, 0.0) - sp
    return ls, ls - z


def _wide(t):
    return jnp.concatenate([t] * SUB, axis=1)


def _chunk_dots(x, tri):
    terms = []
    for u in range(SUB):
        xu = x[:, u * BLK:(u + 1) * BLK]
        hi = xu.astype(BF16)
        terms += [hi, (xu - hi.astype(F32)).astype(BF16)]
    r = _dot(jnp.concatenate(terms, axis=0), tri)
    rows = x.shape[0]
    piece = lambda n: r[n * rows:(n + 1) * rows]
    return [piece(2 * u) + piece(2 * u + 1) for u in range(SUB)]


def _block_suffix_sums(x, suffix, c):
    loc = _chunk_dots(x, suffix)
    out = [None] * SUB
    for u in reversed(range(SUB)):
        out[u] = loc[u] + c
        c = c + jnp.sum(x[:, u * BLK:(u + 1) * BLK], axis=1, keepdims=True)
    return jnp.concatenate(out, axis=1), c


def _block_prefix_sums(x, tri, c):
    loc = _chunk_dots(x, tri)
    out = []
    for u in range(SUB):
        out.append(loc[u] + c)
        c = c + jnp.sum(x[:, u * BLK:(u + 1) * BLK], axis=1, keepdims=True)
    return jnp.concatenate(out, axis=1), c


def causal_fwd(qkv, npairs, mode, name, fq=None, fk=None, rider=None):
    s = qkv.shape[0]
    nq = s // BQ
    fox = mode == "fox"

    def body(*refs):
        if fox:
            q_ref, k_ref, v_ref, fq_ref, fk_ref, o_ref, st_ref = refs
        else:
            q_ref, k_ref, v_ref, o_ref, st_ref = refs
        i = pl.program_id(1)
        nkb = (i * BQ + BQ - 1) // KB + 1
        row, lane = _iotas((BQ, KB))
        row_s, lane_s = _iotas()
        _, lane_q = _iotas((BQ, LANES))
        nfull = (i * BQ) // KB
        qpos = i * BQ + row
        qf = q_ref[...].astype(F32) * 0.125
        hms = (lane_q < HEAD_DIM, lane_q >= HEAD_DIM)
        qas = [jnp.where(hm, qf, 0.0).astype(BF16) for hm in hms]
        suffix = jnp.where(row_s > lane_s, 1.0, 0.0).astype(BF16)
        zero = jnp.zeros((BQ, LANES), F32)
        col0 = jnp.zeros((BQ, 1), F32)

        def kv(j):
            r0 = pl.multiple_of(j * KB, KB)
            return r0, k_ref[pl.ds(r0, KB), :], v_ref[pl.ds(r0, KB), :]

        if fox:
            fqs = [_wide(fq_ref[a]) for a in range(2)]

            def step(j, carry, masked):
                r0, kb, vb = kv(j)
                new = []
                for a in range(2):
                    acc, mx, l = carry[3 * a:3 * a + 3]
                    z = _dot_nt(qas[a], kb) + fqs[a] - fk_ref[a:a + 1, pl.ds(r0, KB)]
                    if masked:
                        z = jnp.where(r0 + lane <= qpos, z, NEG)
                    mnew = jnp.maximum(mx, jnp.max(z, axis=1, keepdims=True))
                    p = jnp.exp(z - mnew)
                    alpha = jnp.exp(mx - mnew)
                    new += [alpha * acc + _dot(p.astype(BF16), vb), mnew, alpha * l + jnp.sum(p, axis=1, keepdims=True)]
                return tuple(new)

            neg = jnp.full((BQ, 1), NEG, F32)
            res = lax.fori_loop(0, nfull, functools.partial(step, masked=False), (zero, neg, col0, zero, neg, col0))
            res = lax.fori_loop(nfull, nkb, functools.partial(step, masked=True), res)
            outs = [res[3 * a] / res[3 * a + 2] for a in range(2)]
            stats = [res[3 * a + 1] + jnp.log(res[3 * a + 2]) for a in range(2)]
        else:
            def step(j, carry, masked):
                r0, kb, vb = kv(j)
                strict = r0 + lane < qpos
                new = []
                for a in range(2):
                    acc, c = carry[2 * a:2 * a + 2]
                    ls, lm = _softplus_parts(_dot_nt(qas[a], kb))
                    if masked:
                        lm = jnp.where(strict, lm, 0.0)
                    between, c = _block_suffix_sums(lm, suffix, c)
                    aw = jnp.exp(ls + between)
                    if masked:
                        aw = jnp.where(strict, aw, 0.0)
                    new += [acc + _dot(aw.astype(BF16), vb), c]
                return tuple(new)

            res = lax.fori_loop(0, nkb - nfull, lambda jj, c: step(nkb - 1 - jj, c, True), (zero, col0, zero, col0))
            res = lax.fori_loop(0, nfull, lambda jj, c: step(nfull - 1 - jj, c, False), res)
            outs, stats = [res[0], res[2]], [res[1], res[3]]
        o_ref[...] = jnp.where(hms[0], outs[0], outs[1])
        for a in range(2):
            st_ref[a] = jnp.broadcast_to(stats[a], (BQ, LANES))

    col = lambda off: (lambda p, i: (0, off + p))
    in_specs = [pl.BlockSpec((BQ, LANES), lambda p, i: (i, Q_OFF + p)),
                pl.BlockSpec((s, LANES), col(K_OFF)), pl.BlockSpec((s, LANES), col(V_OFF))]
    ops = [qkv, qkv, qkv]
    if fox:
        in_specs += [pl.BlockSpec((2, BQ, LANES), lambda p, i: (p, i, 0)), pl.BlockSpec((None, 2, s), lambda p, i: (p, 0, 0))]
        ops += [fq, fk]
    (o, stat), rode = call_with_rider(
        body, name, rider, ops, in_specs,
        [jax.ShapeDtypeStruct((s, npairs * LANES), F32), jax.ShapeDtypeStruct((2 * npairs, s, LANES), F32)],
        [pl.BlockSpec((BQ, LANES), lambda p, i: (i, p)), pl.BlockSpec((2, BQ, LANES), lambda p, i: (p, i, 0))], [], (npairs, nq))
    return o, stat, rode


def causal_bwd(qkv, do, stat, npairs, mode, name, fq=None, fk=None, rider=None):
    s = qkv.shape[0]
    nq = s // BQ
    fox = mode == "fox"

    def body(*refs):
        if fox:
            q_ref, k_ref, v_ref, do_ref, st_ref, fq_ref, fk_ref, dq_ref, dk_ref, dv_ref, df_ref, p_s, dp_s = refs
        else:
            q_ref, k_ref, v_ref, do_ref, st_ref, dq_ref, dk_ref, dv_ref = refs
        i = pl.program_id(1)

        @pl.when(i == 0)
        def _():
            dk_ref[...] = jnp.zeros_like(dk_ref)
            dv_ref[...] = jnp.zeros_like(dv_ref)
            if fox:
                df_ref[...] = jnp.zeros_like(df_ref)

        nkb = (i * BQ + BQ - 1) // KB + 1
        nfull = (i * BQ) // KB
        row, lane = _iotas((BQ, KB))
        row_s, lane_s = _iotas()
        _, lane_q = _iotas((BQ, LANES))
        qpos = i * BQ + row
        qf = q_ref[...].astype(F32) * 0.125
        dov = do_ref[...]
        hms = (lane_q < HEAD_DIM, lane_q >= HEAD_DIM)
        qas = [jnp.where(hm, qf, 0.0).astype(BF16) for hm in hms]
        doas = [jnp.where(hm, dov, 0.0).astype(BF16) for hm in hms]
        stas = [_wide(st_ref[a]) for a in range(2)]
        zero = jnp.zeros((BQ, LANES), F32)
        col0 = jnp.zeros((BQ, 1), F32)

        def kv(j):
            r0 = pl.multiple_of(j * KB, KB)
            return r0, k_ref[pl.ds(r0, KB), :], v_ref[pl.ds(r0, KB), :]

        if fox:
            fqs = [_wide(fq_ref[a]) for a in range(2)]

            def probs(j, deltas, masked):
                r0, kb, vb = kv(j)
                new = []
                for a in range(2):
                    z = _dot_nt(qas[a], kb) + fqs[a] - fk_ref[a:a + 1, pl.ds(r0, KB)]
                    p = jnp.exp(z - stas[a])
                    if masked:
                        p = jnp.where(r0 + lane <= qpos, p, 0.0)
                    dp = _dot_nt(doas[a], vb)
                    p_s[a, j] = p
                    dp_s[a, j] = dp
                    new.append(deltas[a] + jnp.sum(p * dp, axis=1, keepdims=True))
                return tuple(new)

            deltas = lax.fori_loop(0, nfull, functools.partial(probs, masked=False), (col0, col0))
            deltas = lax.fori_loop(nfull, nkb, functools.partial(probs, masked=True), deltas)

            def step(j, dqs):
                r0, kb, _ = kv(j)
                new = []
                dk = jnp.zeros((KB, LANES), F32)
                dv = jnp.zeros((KB, LANES), F32)
                for a in range(2):
                    p = p_s[a, j]
                    ds = p * (dp_s[a, j] - deltas[a])
                    dsb = ds.astype(BF16)
                    dk += _dot_tn(dsb, qas[a])
                    dv += _dot_tn(p.astype(BF16), doas[a])
                    df_ref[a:a + 1, pl.ds(r0, KB)] -= jnp.sum(ds, axis=0, keepdims=True)
                    new.append(dqs[a] + _dot(dsb, kb))
                dk_ref[pl.ds(r0, KB), :] += dk
                dv_ref[pl.ds(r0, KB), :] += dv
                return tuple(new)

            dqs = lax.fori_loop(0, nkb, step, (zero, zero))
        else:
            incl = jnp.where(row_s <= lane_s, 1.0, 0.0).astype(BF16)
            excl = jnp.where(row_s < lane_s, 1.0, 0.0).astype(BF16)

            def step(j, carry, masked):
                r0, kb, vb = kv(j)
                strict = r0 + lane < qpos
                new = []
                dk = jnp.zeros((KB, LANES), F32)
                dv = jnp.zeros((KB, LANES), F32)
                for a in range(2):
                    dq, cm, cg = carry[3 * a:3 * a + 3]
                    ls, lm = _softplus_parts(_dot_nt(qas[a], kb))
                    if masked:
                        lm = jnp.where(strict, lm, 0.0)
                    beta = jnp.exp(ls)
                    upto, cm = _block_prefix_sums(lm, incl, cm)
                    aw = jnp.exp(ls + stas[a] - upto)
                    if masked:
                        aw = jnp.where(strict, aw, 0.0)
                    g = aw * _dot_nt(doas[a], vb)
                    pre, cg = _block_prefix_sums(g, excl, cg)
                    dz = g * (1.0 - beta) - pre * beta
                    if masked:
                        dz = jnp.where(strict, dz, 0.0)
                    dzb = dz.astype(BF16)
                    dk += _dot_tn(dzb, qas[a])
                    dv += _dot_tn(aw.astype(BF16), doas[a])
                    new += [dq + _dot(dzb, kb), cm, cg]
                dk_ref[pl.ds(r0, KB), :] += dk
                dv_ref[pl.ds(r0, KB), :] += dv
                return tuple(new)

            res = lax.fori_loop(0, nfull, functools.partial(step, masked=False), (zero, col0, col0, zero, col0, col0))
            res = lax.fori_loop(nfull, nkb, functools.partial(step, masked=True), res)
            dqs = (res[0], res[3])
        dq_ref[...] = (jnp.where(hms[0], dqs[0], dqs[1]) * 0.125).astype(BF16)

    col = lambda off: (lambda p, i: (0, off + p))
    blk = pl.BlockSpec((BQ, LANES), lambda p, i: (i, p))
    acc = pl.BlockSpec((s, LANES), lambda p, i: (0, p))
    st_spec = pl.BlockSpec((2, BQ, LANES), lambda p, i: (p, i, 0))
    in_specs = [pl.BlockSpec((BQ, LANES), lambda p, i: (i, Q_OFF + p)), pl.BlockSpec((s, LANES), col(K_OFF)),
                pl.BlockSpec((s, LANES), col(V_OFF)), blk, st_spec]
    ops = [qkv, qkv, qkv, do, stat]
    w = npairs * LANES
    out_shape = [jax.ShapeDtypeStruct((s, w), BF16), jax.ShapeDtypeStruct((s, w), F32), jax.ShapeDtypeStruct((s, w), F32)]
    out_specs = [blk, acc, acc]
    scratch = []
    if fox:
        fk_spec = pl.BlockSpec((None, 2, s), lambda p, i: (p, 0, 0))
        in_specs += [st_spec, fk_spec]
        ops += [fq, fk]
        out_shape.append(jax.ShapeDtypeStruct((npairs, 2, s), F32))
        out_specs.append(fk_spec)
        scratch = [pltpu.VMEM((2, s // KB, BQ, KB), F32)] * 2
    outs, rode = call_with_rider(body, name, rider, ops, in_specs, out_shape, out_specs, scratch, (npairs, nq))
    return (*outs, rode)


def forget_fwd(fl, bias, name):
    s = fl.shape[0]

    def body(fl_ref, b_ref, f_ref):
        row, lane = _iotas()
        lower = jnp.where(lane <= row, 1.0, 0.0).astype(BF16)

        def step(n, carry):
            r0 = pl.multiple_of(n * BLK, BLK)
            ls, _ = _softplus_parts(fl_ref[pl.ds(r0, BLK), :] + b_ref[...])
            blk = _dot_exact_rhs(lower, ls) + carry
            f_ref[pl.ds(r0, BLK), :] = blk
            return blk[BLK - 1:BLK, :]

        lax.fori_loop(0, s // BLK, step, jnp.zeros((1, LANES), F32))

    return pl.pallas_call(
        body, name=name, out_shape=jax.ShapeDtypeStruct((s, LANES), F32),
        in_specs=[pl.BlockSpec(memory_space=pltpu.VMEM)] * 2, out_specs=pl.BlockSpec(memory_space=pltpu.VMEM),
        compiler_params=_params(),
    )(fl, bias)


def forget_bwd(fl, bias, df, name):
    s = fl.shape[0]
    nb = s // BLK

    def body(fl_ref, b_ref, df_ref, o_ref, db_ref):
        row, lane = _iotas()
        upper = jnp.where(lane >= row, 1.0, 0.0).astype(BF16)

        def step(nn, carry):
            tail, db = carry
            r0 = pl.multiple_of((nb - 1 - nn) * BLK, BLK)
            dls = _dot_exact_rhs(upper, df_ref[pl.ds(r0, BLK), :]) + tail
            xv = fl_ref[pl.ds(r0, BLK), :] + b_ref[...]
            dfl = dls * (1.0 / (1.0 + jnp.exp(xv)))
            o_ref[pl.ds(r0, BLK), :] = dfl
            return dls[0:1, :], db + jnp.sum(dfl, axis=0, keepdims=True)

        _, db = lax.fori_loop(0, nb, step, (jnp.zeros((1, LANES), F32), jnp.zeros((1, LANES), F32)))
        db_ref[...] = db

    return pl.pallas_call(
        body, name=name, out_shape=(jax.ShapeDtypeStruct((s, LANES), F32), jax.ShapeDtypeStruct((1, LANES), F32)),
        in_specs=[pl.BlockSpec(memory_space=pltpu.VMEM)] * 3,
        out_specs=(pl.BlockSpec(memory_space=pltpu.VMEM), pl.BlockSpec(memory_space=pltpu.VMEM)),
        compiler_params=_params(),
    )(fl, bias, df)


def _rot_tables(s):
    inv = ROPE_THETA ** (-jnp.arange(ROT_HALF, dtype=F32) * 2.0 / (2 * ROT_HALF))
    ang = jnp.arange(s, dtype=F32)[:, None] * inv[None, :]
    cos, sin = jnp.cos(ang), jnp.sin(ang)
    z8 = jnp.zeros((s, ROT_HALF), F32)
    rest = HEAD_DIM - 2 * ROT_HALF
    zr, onr = jnp.zeros((s, rest), F32), jnp.ones((s, rest), F32)
    tile = lambda t: jnp.tile(t, (1, 2))
    return tile(jnp.concatenate([cos, cos, onr], 1)), tile(jnp.concatenate([-sin, z8, zr], 1)), tile(jnp.concatenate([z8, sin, zr], 1))


def rotary_prep(qkv, tables, name):
    s = qkv.shape[0]
    w = 4 * LANES

    def body(q_ref, k_ref, v_ref, c_ref, s1_ref, s2_ref, qo_ref, ko_ref, vo_ref):
        c, s1, s2 = c_ref[...], s1_ref[...], s2_ref[...]

        def rot(xv):
            return xv * c + pltpu.roll(xv, LANES - ROT_HALF, 1) * s1 + pltpu.roll(xv, ROT_HALF, 1) * s2

        qo_ref[...] = rot(q_ref[...].astype(F32)) * 0.125
        ko_ref[...] = rot(k_ref[...].astype(F32))
        vo_ref[...] = v_ref[...].astype(F32)

    cb = lambda off: pl.BlockSpec((ROWS, LANES), lambda i, j: (i, off + j))
    tb = pl.BlockSpec((ROWS, LANES), lambda i, j: (i, 0))
    out = jax.ShapeDtypeStruct((s, w), F32)
    return pl.pallas_call(
        body, name=name, out_shape=(out, out, out), grid=(s // ROWS, 4),
        in_specs=[cb(Q_OFF + 4), cb(K_OFF + 4), cb(V_OFF + 4), tb, tb, tb], out_specs=(cb(0), cb(0), cb(0)),
        compiler_params=_params(("parallel", "parallel")),
    )(qkv, qkv, qkv, *tables)


def rotary_bwd(dq, dk, dv, tables, name):
    s, w = dq.shape

    def body(dq_ref, dk_ref, dv_ref, c_ref, s1_ref, s2_ref, qo_ref, ko_ref, vo_ref):
        c, s1, s2 = c_ref[...], s1_ref[...], s2_ref[...]

        def rot_t(dy):
            return dy * c + pltpu.roll(dy * s1, ROT_HALF, 1) + pltpu.roll(dy * s2, LANES - ROT_HALF, 1)

        qo_ref[...] = (rot_t(dq_ref[...]) * 0.125).astype(BF16)
        ko_ref[...] = rot_t(dk_ref[...]).astype(BF16)
        vo_ref[...] = dv_ref[...].astype(BF16)

    cb = pl.BlockSpec((ROWS, LANES), lambda i, j: (i, j))
    tb = pl.BlockSpec((ROWS, LANES), lambda i, j: (i, 0))
    out = jax.ShapeDtypeStruct((s, w), BF16)
    return pl.pallas_call(
        body, name=name, out_shape=(out, out, out), grid=(s // ROWS, w // LANES),
        in_specs=[cb, cb, cb, tb, tb, tb], out_specs=(cb, cb, cb), compiler_params=_params(("parallel", "parallel")),
    )(dq, dk, dv, *tables)


def _deinterleave(dst, src_ref, stride, s, dtype):
    length = s // stride
    for r in range(stride):
        if stride == 1:
            dst[...] = src_ref[...].astype(dtype)
        else:
            dst[r * length:(r + 1) * length, :] = src_ref[pl.ds(r, length, stride=stride), :].astype(dtype)


def _band_masks(row, lane, first):
    return lane <= row, lane >= row + jnp.where(first, BLK, 0)


def dilated_fwd(qd, kd, vd, name, rider=None):
    s, w = qd.shape
    npairs = w // LANES
    nblk = s // BLK

    def body(q_ref, k_ref, v_ref, o_ref, lse_ref, qs, ks, vs, od, ld, on, ln):
        row, lane = _iotas()
        for pi, stride in enumerate(DIL_STRIDES):
            per = (s // stride) // BLK
            _deinterleave(qs, q_ref, stride, s, BF16)
            _deinterleave(ks, k_ref, stride, s, BF16)
            _deinterleave(vs, v_ref, stride, s, BF16)

            def block(b, carry):
                r0 = pl.multiple_of(b * BLK, BLK)
                rp = pl.multiple_of(jnp.maximum(b - 1, 0) * BLK, BLK)
                mc, mp = _band_masks(row, lane, b % per == 0)
                q = qs[pl.ds(r0, BLK), :]
                kc, kp, vc, vp = ks[pl.ds(r0, BLK), :], ks[pl.ds(rp, BLK), :], vs[pl.ds(r0, BLK), :], vs[pl.ds(rp, BLK), :]
                out = jnp.zeros((BLK, LANES), F32)
                lse = jnp.zeros((BLK, LANES), F32)
                for a in range(2):
                    hm = (lane < HEAD_DIM) if a == 0 else (lane >= HEAD_DIM)
                    qa = jnp.where(hm, q.astype(F32), 0.0).astype(BF16)
                    sc = jnp.where(mc, _dot_nt(qa, kc), NEG)
                    sp = jnp.where(mp, _dot_nt(qa, kp), NEG)
                    mx = jnp.maximum(jnp.max(sc, axis=1, keepdims=True), jnp.max(sp, axis=1, keepdims=True))
                    pc, pp = jnp.exp(sc - mx), jnp.exp(sp - mx)
                    l = jnp.sum(pc, axis=1, keepdims=True) + jnp.sum(pp, axis=1, keepdims=True)
                    oa = (_dot(pc.astype(BF16), vc) + _dot(pp.astype(BF16), vp)) / l
                    out = jnp.where(hm, oa, out)
                    lse = jnp.where(hm, mx + jnp.log(l), lse)
                od[pl.ds(r0, BLK), :] = out
                ld[pl.ds(r0, BLK), :] = lse
                return carry

            lax.fori_loop(0, nblk, block, 0, unroll=2)
            length = s // stride
            for r in range(stride):
                if stride == 1:
                    on[pi] = od[...]
                    ln[pi] = ld[...]
                else:
                    on[pi, pl.ds(r, length, stride=stride), :] = od[r * length:(r + 1) * length, :]
                    ln[pi, pl.ds(r, length, stride=stride), :] = ld[r * length:(r + 1) * length, :]

        def merge(n, carry):
            r0 = pl.multiple_of(n * BLK, BLK)
            ls = [ln[pi, pl.ds(r0, BLK), :] for pi in range(3)]
            mx = jnp.maximum(jnp.maximum(ls[0], ls[1]), ls[2])
            ws = [jnp.exp(lv - mx) for lv in ls]
            den = ws[0] + ws[1] + ws[2]
            num = ws[0] * on[0, pl.ds(r0, BLK), :] + ws[1] * on[1, pl.ds(r0, BLK), :] + ws[2] * on[2, pl.ds(r0, BLK), :]
            o_ref[pl.ds(r0, BLK), :] = num / den
            lse_ref[pl.ds(r0, BLK), :] = mx + jnp.log(den)
            return carry

        lax.fori_loop(0, nblk, merge, 0, unroll=2)

    colspec = pl.BlockSpec((s, LANES), lambda p: (0, p))
    out = jax.ShapeDtypeStruct((s, w), F32)
    scratch = [pltpu.VMEM((s, LANES), BF16)] * 3 + [pltpu.VMEM((s, LANES), F32)] * 2 + [pltpu.VMEM((3, s, LANES), F32)] * 2
    (o, lse), rode = call_with_rider(body, name, rider, [qd, kd, vd], [colspec] * 3, [out, out], [colspec, colspec], scratch, (npairs,))
    return o, lse, rode


def dilated_bwd(qd, kd, vd, do, out, lse, do_off, name, rider=None):
    s, w = qd.shape
    npairs = w // LANES
    nblk = s // BLK

    def body(q_ref, k_ref, v_ref, do_ref, out_ref, lse_ref, dq_ref, dk_ref, dv_ref, qs, ks, vs, dos, dls, lss, dqd, dkd, dvd, dln):
        row, lane = _iotas()
        same_head = jnp.where((row < HEAD_DIM) == (lane < HEAD_DIM), 1.0, 0.0).astype(BF16)

        def delta_blk(n, carry):
            r0 = pl.multiple_of(n * BLK, BLK)
            dln[pl.ds(r0, BLK), :] = _dot_exact_lhs(do_ref[pl.ds(r0, BLK), :] * out_ref[pl.ds(r0, BLK), :], same_head)
            return carry

        lax.fori_loop(0, nblk, delta_blk, 0, unroll=2)
        for pi, stride in enumerate(DIL_STRIDES):
            per = (s // stride) // BLK
            _deinterleave(qs, q_ref, stride, s, BF16)
            _deinterleave(ks, k_ref, stride, s, BF16)
            _deinterleave(vs, v_ref, stride, s, BF16)
            _deinterleave(dos, do_ref, stride, s, BF16)
            _deinterleave(dls, dln, stride, s, F32)
            _deinterleave(lss, lse_ref, stride, s, F32)

            def block(b, carry):
                r0 = pl.multiple_of(b * BLK, BLK)
                rp = pl.multiple_of(jnp.maximum(b - 1, 0) * BLK, BLK)
                first = b % per == 0
                mc, mp = _band_masks(row, lane, first)
                q, dov = qs[pl.ds(r0, BLK), :], dos[pl.ds(r0, BLK), :]
                kc, kp, vc, vp = ks[pl.ds(r0, BLK), :], ks[pl.ds(rp, BLK), :], vs[pl.ds(r0, BLK), :], vs[pl.ds(rp, BLK), :]
                lse_t, dl_t = lss[pl.ds(r0, BLK), :], dls[pl.ds(r0, BLK), :]
                dq = jnp.zeros((BLK, LANES), F32)
                dkc = jnp.zeros((BLK, LANES), F32)
                dkp = jnp.zeros((BLK, LANES), F32)
                dvc = jnp.zeros((BLK, LANES), F32)
                dvp = jnp.zeros((BLK, LANES), F32)
                for a in range(2):
                    hm = (lane < HEAD_DIM) if a == 0 else (lane >= HEAD_DIM)
                    pick = lane == a * HEAD_DIM
                    qa = jnp.where(hm, q.astype(F32), 0.0).astype(BF16)
                    doa = jnp.where(hm, dov.astype(F32), 0.0).astype(BF16)
                    lse_a = jnp.sum(jnp.where(pick, lse_t, 0.0), axis=1, keepdims=True)
                    dl_a = jnp.sum(jnp.where(pick, dl_t, 0.0), axis=1, keepdims=True)
                    pc = jnp.where(mc, jnp.exp(_dot_nt(qa, kc) - lse_a), 0.0)
                    pp = jnp.where(mp, jnp.exp(_dot_nt(qa, kp) - lse_a), 0.0)
                    dsc = (pc * (_dot_nt(doa, vc) - dl_a)).astype(BF16)
                    dsp = (pp * (_dot_nt(doa, vp) - dl_a)).astype(BF16)
                    dq = jnp.where(hm, _dot(dsc, kc) + _dot(dsp, kp), dq)
                    dkc += _dot_tn(dsc, qa)
                    dkp += _dot_tn(dsp, qa)
                    dvc += _dot_tn(pc.astype(BF16), doa)
                    dvp += _dot_tn(pp.astype(BF16), doa)
                dqd[pl.ds(r0, BLK), :] = dq
                dkd[pl.ds(r0, BLK), :] = dkc
                dvd[pl.ds(r0, BLK), :] = dvc

                @pl.when(jnp.logical_not(first))
                def _():
                    dkd[pl.ds(rp, BLK), :] += dkp
                    dvd[pl.ds(rp, BLK), :] += dvp

                return carry

            lax.fori_loop(0, nblk, block, 0, unroll=2)
            length = s // stride
            for dst, src in ((dq_ref, dqd), (dk_ref, dkd), (dv_ref, dvd)):
                for r in range(stride):
                    if stride == 1:
                        dst[...] = src[...]
                    else:
                        dst[pl.ds(r, length, stride=stride), :] += src[r * length:(r + 1) * length, :]

    colspec = pl.BlockSpec((s, LANES), lambda p: (0, p))
    do_spec = pl.BlockSpec((s, LANES), lambda p: (0, do_off + p))
    o3 = jax.ShapeDtypeStruct((s, w), F32)
    scratch = [pltpu.VMEM((s, LANES), BF16)] * 4 + [pltpu.VMEM((s, LANES), F32)] * 6
    outs, rode = call_with_rider(body, name, rider, [qd, kd, vd, do, out, lse], [colspec, colspec, colspec, do_spec, colspec, colspec],
                                 [o3, o3, o3], [colspec, colspec, colspec], scratch, (npairs,))
    return (*outs, rode)


def adamw(w, g, m, v, name):
    rows, cols = w.shape
    rb = min(rows, ROWS)
    c1 = 1.0 - ADAM_B1 ** ADAM_STEP
    c2 = 1.0 - ADAM_B2 ** ADAM_STEP

    def body(w_ref, g_ref, m_ref, v_ref, d_ref, mo_ref, vo_ref):
        gv = g_ref[...]
        mn = ADAM_B1 * m_ref[...] + (1.0 - ADAM_B1) * gv
        vn = ADAM_B2 * v_ref[...] + (1.0 - ADAM_B2) * (gv * gv)
        d_ref[...] = -ADAM_LR * ((mn / c1) / (jnp.sqrt(vn / c2) + ADAM_EPS) + ADAM_WD * w_ref[...])
        mo_ref[...] = mn
        vo_ref[...] = vn

    spec = _row_spec(cols, rb)
    out = jax.ShapeDtypeStruct((rows, cols), F32)
    return pl.pallas_call(
        body, name=name, out_shape=(out, out, out), grid=(rows // rb,), in_specs=[spec] * 4, out_specs=(spec,) * 3,
        compiler_params=_params(("parallel",)),
    )(w, g, m, v)


def _prefetch_call(body, name, scalar, ops, grid, in_specs, out_specs, out_shape, sem):
    spec = pltpu.PrefetchScalarGridSpec(num_scalar_prefetch=1, grid=grid, in_specs=in_specs, out_specs=out_specs)
    return pl.pallas_call(body, name=name, grid_spec=spec, out_shape=out_shape, compiler_params=_params(sem))(scalar, *ops)


def pair_sum(g, got, core, name):
    nc, r, c = g.shape
    rh = r // 2

    def body(core_ref, g_ref, got_ref, o_ref):
        o_ref[...] = (g_ref[...].astype(F32) + got_ref[...].astype(F32)).astype(BF16)

    blk = lambda rows_of: pl.BlockSpec((None, rh, c), rows_of)
    return _prefetch_call(
        body, name, core, (g, got), (nc,),
        [blk(lambda j, core_ref: (j, core_ref[0], 0)), blk(lambda j, core_ref: (j, 0, 0))],
        blk(lambda j, core_ref: (j, 0, 0)), jax.ShapeDtypeStruct((nc, rh, c), BF16), ("parallel",))


def chip_sum(pair, got, chip, layer, into, name):
    _, rh, c = pair.shape

    def body(chip_ref, p_ref, a_ref, b_ref, c_ref, old_ref, o_ref):
        o_ref[...] = ((p_ref[...].astype(F32) + a_ref[...].astype(F32)) + b_ref[...].astype(F32)) + c_ref[...].astype(F32)

    arrival = lambda k: pl.BlockSpec((None, rh, c), lambda i, chip_ref: (k, 0, 0))
    spec = pltpu.PrefetchScalarGridSpec(
        num_scalar_prefetch=1, grid=(1,),
        in_specs=[pl.BlockSpec((None, rh, c), lambda i, chip_ref: (chip_ref[0], 0, 0)), arrival(0), arrival(1), arrival(2), _ANY],
        out_specs=pl.BlockSpec((None, rh, c), lambda i, chip_ref: (layer, 0, 0)))
    return pl.pallas_call(body, name=name, grid_spec=spec, out_shape=jax.ShapeDtypeStruct(into.shape, into.dtype),
                          input_output_aliases={5: 0}, compiler_params=_params(("arbitrary",)))(chip, pair, got, got, got, into)


def adamw_family(w, m, v, g_mine, g_other, core, name):
    nl, r, c = w.shape
    gc = g_mine.shape[2]
    rh = r // 2
    nb = 4 if rh % 512 == 0 else (2 if rh % 16 == 0 and rh > 256 else 1)
    rb = rh // nb
    c1 = 1.0 - ADAM_B1 ** ADAM_STEP
    c2 = 1.0 - ADAM_B2 ** ADAM_STEP

    def body(core_ref, w_ref, m_ref, v_ref, gm_ref, go_ref, g_ref, d_ref, mo_ref, vo_ref):
        gv = jnp.where(pl.program_id(1) == core_ref[0], gm_ref[...], go_ref[...])[:, :c]
        mn = ADAM_B1 * m_ref[...] + (1.0 - ADAM_B1) * gv
        vn = ADAM_B2 * v_ref[...] + (1.0 - ADAM_B2) * (gv * gv)
        g_ref[...] = gv
        d_ref[...] = -ADAM_LR * ((mn / c1) / (jnp.sqrt(vn / c2) + ADAM_EPS) + ADAM_WD * w_ref[...])
        mo_ref[...] = mn
        vo_ref[...] = vn

    full = pl.BlockSpec((None, rb, c), lambda l, h, i, core_ref: (l, h * nb + i, 0))
    half = pl.BlockSpec((None, rb, gc), lambda l, h, i, core_ref: (l, i, 0))
    out = jax.ShapeDtypeStruct((nl, r, c), F32)
    return _prefetch_call(body, name, core, (w, m, v, g_mine, g_other), (nl, 2, nb), [full, full, full, half, half],
                          (full, full, full, full), (out, out, out, out), ("parallel", "parallel", "parallel"))


def _coords():
    return lax.axis_index("x"), lax.axis_index("y"), lax.axis_index("c")


def _other_chips(x, y):
    return ((1 - x, y), (x, 1 - y), (1 - x, 1 - y))


_ANY = pl.BlockSpec(memory_space=pl.ANY)


def _exchange_call(body, name, arrays, out_shapes, n_copies, n_local=0):
    n = len(arrays)

    def wrapped(*refs):
        body(refs[:n], refs[n:n + len(out_shapes)], *refs[n + len(out_shapes):])

    scratch = [pltpu.SemaphoreType.DMA((n_copies,)), pltpu.SemaphoreType.DMA((n_copies,))]
    if n_local:
        scratch.append(pltpu.SemaphoreType.DMA((n_local,)))
    return pl.pallas_call(
        wrapped, name=name, out_shape=tuple(out_shapes), in_specs=[_ANY] * n, out_specs=tuple([_ANY] * len(out_shapes)),
        scratch_shapes=scratch, compiler_params=_params(),
    )(*arrays)


def _remote(send_sems, recv_sems, n, src, dst, to):
    return pltpu.make_async_remote_copy(src_ref=src, dst_ref=dst, send_sem=send_sems.at[n], recv_sem=recv_sems.at[n],
                                        device_id=to, device_id_type=MESH)


class Rider:
    def __init__(self, arrays, out_shapes, n_remote, n_local, copies, then=None):
        self.arrays, self.out_shapes, self.n_remote, self.n_local = list(arrays), list(out_shapes), n_remote, n_local
        self.copies, self.then = copies, then

    def sems(self):
        return [pltpu.SemaphoreType.DMA((self.n_remote,)), pltpu.SemaphoreType.DMA((self.n_remote,)),
                pltpu.SemaphoreType.DMA((max(self.n_local, 1),))]

    def run(self, name):
        n, no = len(self.arrays), len(self.out_shapes)

        def body(*refs):
            for stage in (self.copies, self.then):
                if stage is not None:
                    cps = stage(refs[:n], refs[n:n + no], *refs[n + no:])
                    for cp in cps:
                        cp.start()
                    for cp in cps:
                        cp.wait()

        return pl.pallas_call(
            body, name=name, out_shape=tuple(self.out_shapes), in_specs=[_ANY] * n, out_specs=tuple([_ANY] * no),
            scratch_shapes=self.sems(), compiler_params=_params(),
        )(*self.arrays)


def ride(rider, body, n_in, n_out, grid):
    if rider is None:
        return body
    ni, no = len(rider.arrays), len(rider.out_shapes)

    def wrapped(*refs):
        ins, r_in = refs[:n_in], refs[n_in:n_in + ni]
        outs = refs[n_in + ni:n_in + ni + n_out]
        r_out = refs[n_in + ni + n_out:n_in + ni + n_out + no]
        rest = refs[n_in + ni + n_out + no:]
        scratch, sems = rest[:len(rest) - 3], rest[len(rest) - 3:]
        step, total = 0, 1
        for a, g in enumerate(grid):
            step, total = step * g + pl.program_id(a), total * g
        assert total >= 3
        relay_at = (7 * total) // 8 if rider.then is not None else total - 1

        @pl.when(step == 0)
        def _():
            for cp in rider.copies(r_in, r_out, *sems):
                cp.start()

        body(*ins, *outs, *scratch)

        @pl.when(step == relay_at)
        def _():
            for cp in rider.copies(r_in, r_out, *sems):
                cp.wait()
            if rider.then is not None:
                for cp in rider.then(r_in, r_out, *sems):
                    cp.start()

        if rider.then is not None:
            @pl.when(step == total - 1)
            def _():
                for cp in rider.then(r_in, r_out, *sems):
                    cp.wait()

    return wrapped


def call_with_rider(body, name, rider, ops, in_specs, out_shape, out_specs, scratch, grid):
    n_in, n_out = len(ops), len(out_shape)
    ops, in_specs, out_shape, out_specs, scratch = list(ops), list(in_specs), list(out_shape), list(out_specs), list(scratch)
    if rider is not None:
        ops += rider.arrays
        in_specs += [_ANY] * len(rider.arrays)
        out_shape += rider.out_shapes
        out_specs += [_ANY] * len(rider.out_shapes)
        scratch += rider.sems()
    res = pl.pallas_call(
        ride(rider, body, n_in, n_out, grid), name=name, out_shape=tuple(out_shape), grid=grid, in_specs=in_specs,
        out_specs=tuple(out_specs), scratch_shapes=scratch, compiler_params=_params(("arbitrary",) * len(grid)),
    )(*ops)
    return tuple(res[:n_out]), list(res[n_out:])


def gather_rider(shards):
    nf = len(shards)
    half = lambda ref, which: pl.ds(which * (ref.shape[-2] // 2), ref.shape[-2] // 2)

    def copies(s_refs, o_refs, send_sems, recv_sems, local_sems):
        x, y, c = _coords()
        me = 2 * x + y
        cps = [pltpu.make_async_copy(s_refs[f], o_refs[f].at[me], local_sems.at[f]) for f in range(nf)]
        for k, (px, py) in enumerate(_other_chips(x, y)):
            for f in range(nf):
                rows = half(s_refs[f], c)
                cps.append(_remote(send_sems, recv_sems, k * nf + f, s_refs[f].at[rows], o_refs[f].at[me, rows], (px, py, c)))
        return cps

    def relay(s_refs, o_refs, send_sems, recv_sems, local_sems):
        x, y, c = _coords()
        cps = []
        for k, (px, py) in enumerate(_other_chips(x, y)):
            for f in range(nf):
                landed = o_refs[f].at[2 * px + py, half(s_refs[f], c)]
                cps.append(_remote(send_sems, recv_sems, (3 + k) * nf + f, landed, landed, (x, y, 1 - c)))
        return cps

    return Rider(shards, [jax.ShapeDtypeStruct((N_CHIPS,) + sh.shape, sh.dtype) for sh in shards], 6 * nf, nf, copies, relay)


def scatter_rider(pairs):
    nf = len(pairs)

    def copies(p_refs, o_refs, send_sems, recv_sems, local_sems):
        x, y, c = _coords()
        cps = []
        for k, (px, py) in enumerate(_other_chips(x, y)):
            for f in range(nf):
                cps.append(_remote(send_sems, recv_sems, k * nf + f, p_refs[f].at[2 * px + py], o_refs[f].at[k], (px, py, c)))
        return cps

    return Rider(pairs, [jax.ShapeDtypeStruct((3,) + p.shape[1:], p.dtype) for p in pairs], 3 * nf, 0, copies)


def pair_swap(grads, name):
    def body(g_refs, o_refs, send_sems, recv_sems):
        x, y, c = _coords()
        cps = []
        for f, g_ref in enumerate(g_refs):
            rh = g_ref.shape[1] // 2
            cps.append(_remote(send_sems, recv_sems, f, g_ref.at[:, pl.ds((1 - c) * rh, rh), :], o_refs[f], (x, y, 1 - c)))
        for cp in cps:
            cp.start()
        for cp in cps:
            cp.wait()

    outs = [jax.ShapeDtypeStruct((g.shape[0], g.shape[1] // 2, g.shape[2]), g.dtype) for g in grads]
    return _exchange_call(body, name, grads, outs, len(grads))


def half_swap(halves, name):
    def body(h_refs, o_refs, send_sems, recv_sems):
        x, y, c = _coords()
        cps = [_remote(send_sems, recv_sems, f, h_ref, o_refs[f], (x, y, 1 - c)) for f, h_ref in enumerate(h_refs)]
        for cp in cps:
            cp.start()
        for cp in cps:
            cp.wait()

    return _exchange_call(body, name, halves, [jax.ShapeDtypeStruct(h.shape, h.dtype) for h in halves], len(halves))


def allsum_small(part, name):
    def body(p_ref, tot_ref, all_ref, send_sems, recv_sems):
        x, y, c = _coords()
        me, sibling = (x, y, c), (x, y, 1 - c)
        chips = _other_chips(x, y)

        def slot(px, py, pc):
            return all_ref.at[4 * px + 2 * py + pc]

        def copy(k, block, to, src=None):
            return pltpu.make_async_remote_copy(src_ref=slot(*block) if src is None else src, dst_ref=slot(*block),
                                                send_sem=send_sems.at[k], recv_sem=recv_sems.at[k], device_id=to, device_id_type=MESH)

        slot(*me)[...] = p_ref[...]
        first = [copy(0, me, sibling, src=p_ref)] + [copy(1 + j, me, (*chip, c), src=p_ref) for j, chip in enumerate(chips)]
        for cp in first:
            cp.start()
        passed = [copy(4 + j, (*chip, c), sibling) for j, chip in enumerate(chips)]
        for j, chip in enumerate(chips):
            copy(1 + j, (*chip, c), me).wait_recv()
            passed[j].start()
        copy(0, sibling, me).wait_recv()
        for j, chip in enumerate(chips):
            copy(4 + j, (*chip, 1 - c), me).wait_recv()
        for cp in first + passed:
            cp.wait_send()
        tot = all_ref[0]
        for d in range(1, 8):
            tot = tot + all_ref[d]
        tot_ref[...] = tot

    vm = pl.BlockSpec(memory_space=pltpu.VMEM)
    return pl.pallas_call(
        body, name=name, out_shape=jax.ShapeDtypeStruct(part.shape, F32), in_specs=[vm], out_specs=vm,
        scratch_shapes=[pltpu.VMEM((8,) + part.shape, F32), pltpu.SemaphoreType.DMA((7,)), pltpu.SemaphoreType.DMA((7,))],
        compiler_params=_params(),
    )(part)


QKVF_COLS = 772
QKVF_PAD = 896
FORWARD_CARRY = {0: (("fi0", "fo0"), ("qkv1", "o1", "fo1")), 1: (("fi1", "qkv2", "o2"), ()),
                 2: (("fi2", "fo2"), ("qkv3", "o3", "fo3")), 3: (("fi3",), ())}


def _tables_for(s):
    return _rot_tables(s)


def layer_families(layer):
    return (0, 1, layer // 2) if layer % 2 == 0 else (2, 3, layer // 2)


class GradientExchange:
    def __init__(self):
        self.core = lax.axis_index("c").astype(jnp.int32).reshape(1)
        self.chip = (2 * lax.axis_index("x") + lax.axis_index("y")).astype(jnp.int32).reshape(1)
        self.pairs, self.arrived, self.pending = {}, {}, []

    def add(self, items, tag):
        got = pair_swap([g for _, _, g in items], f"grad_pair_swap_{tag}")
        for (fam, li, g), r in zip(items, got):
            self.pairs[(fam, li)] = pair_sum(g, r, self.core, f"grad_pair_sum_{fam}_{li}")
            self.pending.append((fam, li))

    def rider(self, only=None):
        keys = [k for k in self.pending if only is None or k in only]
        self.pending = [k for k in self.pending if k not in keys]
        return (scatter_rider([self.pairs[k] for k in keys]) if keys else None), keys

    def landed(self, keys, outs):
        self.arrived.update(zip(keys, outs))

    def finish(self, weights, moments1, moments2):
        last, keys = self.rider()
        if last is not None:
            self.landed(keys, last.run("grad_chip_scatter_last"))
        mine = []
        for fam, w in enumerate(weights):
            buf = jnp.zeros((w.shape[0],) + self.pairs[(fam, 0)].shape[1:], F32)
            for li in range(w.shape[0]):
                buf = chip_sum(self.pairs[(fam, li)], self.arrived[(fam, li)], self.chip, li, buf, f"grad_chip_sum_{fam}_{li}")
            mine.append(buf)
        other = half_swap(mine, "grad_half_swap")
        return [adamw_family(w, m, v, gm, go, self.core, f"adamw_{f}")
                for f, (w, m, v, gm, go) in enumerate(zip(weights, moments1, moments2, mine, other))]


class KeepGradients:
    def __init__(self):
        self.grads = {}

    def add(self, items, tag):
        for fam, li, g in items:
            self.grads[(fam, li)] = g

    def rider(self, only=None):
        return None, []

    def landed(self, keys, outs):
        pass


def kernel(x, norm_mix, w_qkv_even, w_o_even, w_qkvf_odd, b_forget, w_o_odd, norm_ffn, w_ffn_in, w_ffn_out, norm_final, loss_target, m_norm_mix, m_w_qkv_even, m_w_o_even, m_w_qkvf_odd, m_b_forget, m_w_o_odd, m_norm_ffn, m_w_ffn_in, m_w_ffn_out, m_norm_final, v_norm_mix, v_w_qkv_even, v_w_o_even, v_w_qkvf_odd, v_b_forget, v_w_o_odd, v_norm_ffn, v_w_ffn_in, v_w_ffn_out, v_norm_final):
    w_shards = [w_qkv_even, w_o_even, w_qkvf_odd, w_o_odd, w_ffn_in, w_ffn_out]
    shards = [w.astype(BF16) for w in w_shards]
    tables = _tables_for(x.shape[1])
    bias_pad = jnp.pad(b_forget, ((0, 0), (0, LANES - N_HEADS)))

    mine = {}
    for layer in range(DEPTH):
        fam_qkv, fam_o, li = layer_families(layer)
        mine.update({f"qkv{layer}": shards[fam_qkv][li], f"o{layer}": shards[fam_o][li],
                     f"fi{layer}": shards[4][layer], f"fo{layer}": shards[5][layer]})
    fetch = lambda names: gather_rider([mine[n] for n in names])
    have = dict(zip(("qkv0", "o0"), fetch(("qkv0", "o0")).run("gather_first")))
    saved, cur = [], x[0]
    for layer in range(DEPTH):
        carry, side_carry = FORWARD_CARRY[layer]
        cur, keep = forward_layer(layer, cur, have, norm_mix[layer:layer + 1], norm_ffn[layer:layer + 1], tables,
                                  bias_pad[layer // 2:layer // 2 + 1], fetch, carry, side_carry)
        saved.append(keep)

    dcur, g_final, loss_part = loss_head(cur, norm_final.reshape(1, D_MODEL), loss_target[0], "loss_head")

    exchange = GradientExchange()
    g_mix, g_ffn, g_bias = [None] * DEPTH, [None] * DEPTH, [None] * (DEPTH // 2)
    for layer in reversed(range(DEPTH)):
        dcur, g_mix[layer], g_ffn[layer], g_b = backward_layer(layer, dcur, saved[layer], norm_mix[layer:layer + 1],
                                                               norm_ffn[layer:layer + 1], tables, bias_pad[layer // 2:layer // 2 + 1], exchange)
        if g_b is not None:
            g_bias[layer // 2] = g_b

    zero_row = jnp.zeros((1, D_MODEL), F32)
    pad16 = lambda v: jnp.pad(v, (0, D_MODEL - v.shape[0]))[None, :]
    small_rows = lambda mix, ffn, fin, bias, last: jnp.concatenate(
        [r.reshape(1, D_MODEL) for r in mix] + [r.reshape(1, D_MODEL) for r in ffn] + [fin.reshape(1, D_MODEL)]
        + [pad16(b) for b in bias] + [last] + [zero_row] * (SMALL_ROWS - 12), axis=0)
    loss_row = pad16(loss_part[0, :1])
    small_g = allsum_small(small_rows(g_mix, g_ffn, g_final, g_bias, loss_row), "allsum_small")
    loss = small_g[11, 0]
    small_g = small_g.at[11].set(0.0)
    sw = small_rows(list(norm_mix), list(norm_ffn), norm_final, list(b_forget), zero_row)
    sm = small_rows(list(m_norm_mix), list(m_norm_ffn), m_norm_final, list(m_b_forget), zero_row)
    sv = small_rows(list(v_norm_mix), list(v_norm_ffn), v_norm_final, list(v_b_forget), zero_row)
    sd, snm, snv = adamw(sw, small_g, sm, sv, "adamw_small")

    def small_out(a):
        return a[0:4], a[8, :], a[9:11, :N_HEADS], a[4:8]

    big = exchange.finish(w_shards, [m_w_qkv_even, m_w_o_even, m_w_qkvf_odd, m_w_o_odd, m_w_ffn_in, m_w_ffn_out],
                          [v_w_qkv_even, v_w_o_even, v_w_qkvf_odd, v_w_o_odd, v_w_ffn_in, v_w_ffn_out])

    def outputs(small, which):
        mix, fin, bias, ffn = small_out(small)
        qkv_e, o_e, qkvf, o_o, fi, fo = [big[f][which] for f in range(6)]
        return [mix, qkv_e, o_e, qkvf, bias, o_o, ffn, fi, fo, fin]

    return (loss, dcur[None], *outputs(small_g, 0), *outputs(sd, 1), *outputs(snm, 2), *outputs(snv, 3))


def _chip_tile(rows, cols, at):
    return pl.BlockSpec((None, rows, cols), at)


def forward_layer(layer, cur, have, mix_gain, ffn_gain, tables, bias_row, fetch=None, carry=(), side_carry=()):
    n = f"l{layer}"
    s = cur.shape[0]
    w_qkv, w_o = have[f"qkv{layer}"], have[f"o{layer}"]
    rider = fetch(carry) if carry else None
    side_rider = fetch(side_carry) if side_carry else None
    h1 = rmsnorm_fwd(cur, mix_gain, n + "_norm_mix")
    keep = {"x": cur, "h1": h1, "w_o": w_o.reshape(D_ATTN, D_MODEL)}
    side = []
    if layer % 2 == 0:
        qkv = matmul(h1, w_qkv, "nn", BF16, n + "_qkv", 1024, 768, 1024, mnk=(s, 3 * D_ATTN, D_MODEL),
                     b_spec=_chip_tile(D_MODEL, 768, lambda i, j, kk: (j, 0, 0)))
        o_sb, st, rode = causal_fwd(qkv, 4, "sb", n + "_sb_fwd", rider=rider)
        qd, kd, vd = rotary_prep(qkv, tables, n + "_rotary")
        o_dil, lse_dil, side = dilated_fwd(qd, kd, vd, n + "_dil_fwd", rider=side_rider)
        attn = jnp.concatenate([o_sb, o_dil], axis=1).astype(BF16)
        keep.update(qd=qd, kd=kd, vd=vd, o_dil=o_dil, lse_dil=lse_dil, w_qkv=w_qkv)
    else:
        natural = jnp.transpose(w_qkv, (1, 0, 2)).reshape(D_MODEL, N_CHIPS * QKVF_COLS)
        w_gate = jnp.pad(natural[:, 3 * D_ATTN:], ((0, 0), (0, LANES - N_HEADS)))
        qkv = matmul(h1, natural[:, :3 * D_ATTN], "nn", BF16, n + "_qkv", 1024, 768, 1024)
        fl = matmul(h1, w_gate, "nn", F32, n + "_fgate", 512, LANES, 1024)
        cum = forget_fwd(fl, bias_row, n + "_forget_fwd")
        f_heads = cum[:, :N_HEADS].T
        fq = jnp.broadcast_to(f_heads[:, :, None], (N_HEADS, s, LANES))
        fk = f_heads.reshape(N_HEADS // 2, 2, s)
        attn, st, rode = causal_fwd(qkv, 8, "fox", n + "_fox_fwd", fq=fq, fk=fk, rider=rider)
        attn = attn.astype(BF16)
        keep.update(fl=fl, fq=fq, fk=fk, w_qkv=jnp.concatenate([natural[:, :3 * D_ATTN], w_gate], axis=1))
    have.update(zip(carry, rode))
    have.update(zip(side_carry, side))
    w_fi, w_fo = have[f"fi{layer}"], have[f"fo{layer}"].reshape(D_FF, D_MODEL)
    mid = matmul(attn, keep["w_o"], "nn", F32, n + "_attn_out", 1024, 1024, 1024, res=cur)
    h2 = rmsnorm_fwd(mid, ffn_gain, n + "_norm_ffn")
    gu = matmul(h2, w_fi, "nn", BF16, n + "_ffn_in", 1024, 1408, 1024, mnk=(s, 2 * D_FF, D_MODEL),
                b_spec=_chip_tile(D_MODEL, 1408, lambda i, j, kk: (j, 0, 0)))
    act = swiglu_fwd(gu, n + "_swiglu")
    out = matmul(act, w_fo, "nn", F32, n + "_ffn_out", 512, 1024, D_FF, res=mid)
    keep.update(qkv=qkv, st=st, attn=attn, mid=mid, h2=h2, gu=gu, act=act, w_fi=w_fi, w_fo=w_fo)
    return out, keep


def backward_layer(layer, dcur, kp, mix_gain, ffn_gain, tables, bias_row, exchange):
    n = f"l{layer}"
    s = dcur.shape[0]
    fam_qkv, fam_o, li = layer_families(layer)
    g_fo = matmul(kp["act"], dcur, "tn", BF16, n + "_d_w_ffn_out", 1408, 1024, s)
    dact = matmul(dcur, kp["w_fo"], "nt", BF16, n + "_d_act", 1024, 1408, 1024)
    dgu = swiglu_bwd(kp["gu"], dact, n + "_d_swiglu")
    g_fi = matmul(kp["h2"], dgu, "tn", BF16, n + "_d_w_ffn_in", 1024, 1408, 2048, mnk=(D_MODEL, 2 * D_FF, s),
                  o_spec=_chip_tile(D_MODEL, 1408, lambda i, j, kk: (j, 0, 0)), out_shape=(N_CHIPS, D_MODEL, 1408))
    all_chips = lambda cols: pl.BlockSpec((N_CHIPS, D_MODEL, cols), lambda i, j, kk: (0, 0, 0))
    dh2 = matmul(dgu, kp["w_fi"], "nt", F32, n + "_d_h2", 512, 1024, 2 * D_FF, mnk=(s, D_MODEL, 2 * D_FF), b_spec=all_chips(1408))
    dmid, g_ffn = rmsnorm_bwd(kp["mid"], ffn_gain, dh2, dcur, n + "_d_norm_ffn")
    g_o = matmul(kp["attn"], dmid, "tn", BF16, n + "_d_w_o", 1024, 1024, s)
    dattn = matmul(dmid, kp["w_o"], "nt", F32, n + "_d_attn", 1024, 1024, 1024)
    exchange.add([(5, layer, g_fo.reshape(N_CHIPS, D_FF // N_CHIPS, D_MODEL)), (4, layer, g_fi),
                  (fam_o, li, g_o.reshape(N_CHIPS, D_ATTN // N_CHIPS, D_MODEL))], f"l{layer}_ffn")
    g_bias = None
    if layer % 2 == 0:
        rider, keys = exchange.rider(only=[(4, layer), (fam_o, li)])
        dq_a, dk_a, dv_a, rode = causal_bwd(kp["qkv"], dattn, kp["st"], 4, "sb", n + "_sb_bwd", rider=rider)
        exchange.landed(keys, rode)
        rider, keys = exchange.rider()
        dqd, dkd, dvd, rode = dilated_bwd(kp["qd"], kp["kd"], kp["vd"], dattn, kp["o_dil"], kp["lse_dil"], 4, n + "_dil_bwd",
                                          rider=rider)
        dq_b, dk_b, dv_b = rotary_bwd(dqd, dkd, dvd, tables, n + "_d_rotary")
        dproj = jnp.concatenate([dq_a, dq_b, dk_a.astype(BF16), dk_b, dv_a.astype(BF16), dv_b], axis=1)
        g_qkv = matmul(kp["h1"], dproj, "tn", BF16, n + "_d_w_qkv", 1024, 768, 2048, mnk=(D_MODEL, 3 * D_ATTN, s),
                       o_spec=_chip_tile(D_MODEL, 768, lambda i, j, kk: (j, 0, 0)), out_shape=(N_CHIPS, D_MODEL, 768))
        dh1 = matmul(dproj, kp["w_qkv"], "nt", F32, n + "_d_h1", 512, 1024, 3 * D_ATTN, mnk=(s, D_MODEL, 3 * D_ATTN),
                     b_spec=all_chips(768))
    else:
        rider, keys = exchange.rider()
        dq_f, dk_f, dv_f, dfk, rode = causal_bwd(kp["qkv"], dattn, kp["st"], 8, "fox", n + "_fox_bwd", fq=kp["fq"], fk=kp["fk"],
                                                 rider=rider)
        dcum = jnp.pad(dfk.reshape(N_HEADS, s).T, ((0, 0), (0, LANES - N_HEADS)))
        dfl, dbias = forget_bwd(kp["fl"], bias_row, dcum, n + "_forget_bwd")
        g_bias = dbias[0, :N_HEADS]
        dproj = jnp.concatenate([dq_f, dk_f.astype(BF16), dv_f.astype(BF16), dfl.astype(BF16)], axis=1)
        g_nat = matmul(kp["h1"], dproj, "tn", BF16, n + "_d_w_qkv", 1024, 640, 2048)
        g_qkv = g_nat[:, :N_CHIPS * QKVF_COLS].reshape(D_MODEL, N_CHIPS, QKVF_COLS)
        g_qkv = jnp.transpose(jnp.pad(g_qkv, ((0, 0), (0, 0), (0, QKVF_PAD - QKVF_COLS))), (1, 0, 2))
        dh1 = matmul(dproj, kp["w_qkv"], "nt", F32, n + "_d_h1", 512, 1024, dproj.shape[1])
    exchange.landed(keys, rode)
    exchange.add([(fam_qkv, li, g_qkv)], f"l{layer}_qkv")
    dx, g_mix = rmsnorm_bwd(kp["x"], mix_gain, dh1, dmid, n + "_d_norm_mix")
    return dx, g_mix, g_ffn, g_bias


def local_step(xs, target, norm_mix, norm_ffn, norm_final, b_forget, layer_weights):
    tables = _tables_for(xs.shape[0])
    bias_pad = jnp.pad(b_forget, ((0, 0), (0, LANES - N_HEADS)))
    saved, cur, have = [], xs, {}
    for layer in range(DEPTH):
        have.update(zip((f"qkv{layer}", f"o{layer}", f"fi{layer}", f"fo{layer}"), layer_weights[layer]))
        cur, keep = forward_layer(layer, cur, have, norm_mix[layer:layer + 1], norm_ffn[layer:layer + 1], tables,
                                  bias_pad[layer // 2:layer // 2 + 1])
        saved.append(keep)
    dcur, g_final, loss_part = loss_head(cur, norm_final.reshape(1, D_MODEL), target, "loss_head")
    keeper = KeepGradients()
    g_mix, g_ffn, g_bias = [None] * DEPTH, [None] * DEPTH, [None] * (DEPTH // 2)
    for layer in reversed(range(DEPTH)):
        dcur, g_mix[layer], g_ffn[layer], g_b = backward_layer(layer, dcur, saved[layer], norm_mix[layer:layer + 1],
                                                               norm_ffn[layer:layer + 1], tables, bias_pad[layer // 2:layer // 2 + 1], keeper)
        if g_b is not None:
            g_bias[layer // 2] = g_b
    return dcur, keeper.grads, (g_mix, g_ffn, g_final, g_bias), loss_part
```

```python
import functools

import jax
import jax.numpy as jnp
from jax import lax
from jax.experimental import pallas as pl
from jax.experimental.pallas import tpu as pltpu

F32 = jnp.float32
BF16 = jnp.bfloat16
MESH = pl.DeviceIdType.MESH

D_MODEL = 1024
DEPTH = 4
HEAD_DIM = 64
N_HEADS = 16
D_ATTN = 1024
D_FF = 2816
ROPE_THETA = 500000.0
ROT_HALF = 8
RMS_EPS = 1e-5
DIL_STRIDES = (1, 4, 16)
ADAM_LR, ADAM_B1, ADAM_B2, ADAM_EPS, ADAM_WD, ADAM_STEP = 0.001, 0.9, 0.999, 1e-8, 0.01, 10

LANES = 128
BLK = 128
VMEM_LIMIT = 56 * 1024 * 1024
NEG = -1e30
N_CHIPS = 4
FLAT_COLS = 1024
FLAT_ROWS = 12800
HALF_ROWS = FLAT_ROWS // 2
SMALL_ROWS = 16


def _params(sem=None):
    return pltpu.CompilerParams(dimension_semantics=sem, vmem_limit_bytes=VMEM_LIMIT)


def _dot(a, b):
    return lax.dot_general(a, b, (((1,), (0,)), ((), ())), preferred_element_type=F32)


def _dot_nt(a, b):
    return lax.dot_general(a, b, (((1,), (1,)), ((), ())), preferred_element_type=F32)


def _dot_tn(a, b):
    return lax.dot_general(a, b, (((0,), (0,)), ((), ())), preferred_element_type=F32)


def _split3(x):
    x1 = x.astype(BF16)
    r1 = x - x1.astype(F32)
    x2 = r1.astype(BF16)
    x3 = (r1 - x2.astype(F32)).astype(BF16)
    return x1, x2, x3


def _dot_exact_lhs(x, t):
    x1, x2, x3 = _split3(x)
    return _dot(x1, t) + _dot(x2, t) + _dot(x3, t)


def _dot_exact_rhs(t, x):
    x1, x2, x3 = _split3(x)
    return _dot(t, x1) + _dot(t, x2) + _dot(t, x3)


def _iotas(shape=(BLK, LANES)):
    return lax.broadcasted_iota(jnp.int32, shape, 0), lax.broadcasted_iota(jnp.int32, shape, 1)


_DIMS = {"nn": (((1,), (0,)), ((), ())), "nt": (((1,), (1,)), ((), ())), "tn": (((0,), (0,)), ((), ()))}


def matmul(a, b, mode, out_dtype, name, tm, tn, tk, res=None, mnk=None, b_spec=None, o_spec=None, out_shape=None, norm_gain=None):
    if mnk is not None:
        m, n, k = mnk
    elif mode == "nn":
        (m, k), (k2, n) = a.shape, b.shape
    elif mode == "nt":
        (m, k), (n, k2) = a.shape, b.shape
    else:
        (k, m), (k2, n) = a.shape, b.shape
    assert m % tm == 0 and n % tn == 0 and k % tk == 0, (name, a.shape, b.shape)
    nk = k // tk
    a_spec = pl.BlockSpec((tk, tm), lambda i, j, kk: (kk, i)) if mode == "tn" else pl.BlockSpec((tm, tk), lambda i, j, kk: (i, kk))
    if b_spec is None:
        b_spec = pl.BlockSpec((tn, tk), lambda i, j, kk: (j, kk)) if mode == "nt" else pl.BlockSpec((tk, tn), lambda i, j, kk: (kk, j))
    r_spec = pl.BlockSpec((tm, tn), lambda i, j, kk: (i, j))
    if o_spec is None:
        o_spec = r_spec
    dims = _DIMS[mode]
    has_res, has_norm = res is not None, norm_gain is not None
    assert not has_norm or (tn == n and nk == 1)
    n_in = 2 + int(has_res) + int(has_norm)

    def body(*refs):
        a_ref, b_ref = refs[0], refs[1]
        r_ref = refs[2] if has_res else None
        o_ref = refs[n_in]

        def finish(v):
            if has_res:
                v = v + r_ref[...]
            o_ref[...] = v.astype(out_dtype)
            if has_norm:
                rstd = lax.rsqrt(jnp.mean(v * v, axis=-1, keepdims=True) + RMS_EPS)
                refs[n_in + 1][...] = (v * rstd * refs[n_in - 1][...]).astype(BF16)

        bv = b_ref[...]
        if bv.ndim == 3:
            bv = jnp.concatenate([bv[j] for j in range(bv.shape[0])], axis=1)
        p = lax.dot_general(a_ref[...].astype(BF16), bv.astype(BF16), dims, preferred_element_type=F32)
        if nk == 1:
            finish(p)
        else:
            acc = refs[-1]
            kk = pl.program_id(2)

            @pl.when(kk == 0)
            def _():
                acc[...] = p

            @pl.when(kk > 0)
            def _():
                acc[...] += p

            @pl.when(kk == nk - 1)
            def _():
                finish(acc[...])

    ops = [a, b] + ([res] if has_res else []) + ([norm_gain] if has_norm else [])
    specs = [a_spec, b_spec] + ([r_spec] if has_res else []) + ([pl.BlockSpec((1, tn), lambda i, j, kk: (0, j))] if has_norm else [])
    out_shape = jax.ShapeDtypeStruct((m, n) if out_shape is None else out_shape, out_dtype)
    return pl.pallas_call(
        body, name=name, out_shape=(out_shape, jax.ShapeDtypeStruct((m, n), BF16)) if has_norm else out_shape,
        grid=(m // tm, n // tn, nk), in_specs=specs, out_specs=(o_spec, r_spec) if has_norm else o_spec,
        scratch_shapes=[pltpu.VMEM((tm, tn), F32)] if nk > 1 else [],
        compiler_params=_params(("parallel", "parallel", "arbitrary")),
    )(*ops)


ROWS = 256


def _row_spec(cols, rows=ROWS):
    return pl.BlockSpec((rows, cols), lambda i: (i, 0))


def _fix_spec(r, cols):
    return pl.BlockSpec((r, cols), lambda i: (0, 0))


def rmsnorm_fwd(x, g, name):
    s, d = x.shape

    def body(x_ref, g_ref, h_ref):
        xv = x_ref[...]
        rstd = lax.rsqrt(jnp.mean(xv * xv, axis=-1, keepdims=True) + RMS_EPS)
        h_ref[...] = (xv * rstd * g_ref[...]).astype(BF16)

    return pl.pallas_call(
        body, name=name, out_shape=jax.ShapeDtypeStruct((s, d), BF16), grid=(s // ROWS,),
        in_specs=[_row_spec(d), _fix_spec(1, d)], out_specs=_row_spec(d), compiler_params=_params(("parallel",)),
    )(x, g)


def _rms_bwd_math(xv, gv, dh):
    rstd = lax.rsqrt(jnp.mean(xv * xv, axis=-1, keepdims=True) + RMS_EPS)
    xhat = xv * rstd
    u = dh * gv
    dx = rstd * (u - xhat * jnp.mean(u * xhat, axis=-1, keepdims=True))
    return dx, dh * xhat


def rmsnorm_bwd(x, g, dh, dres, name):
    s, d = x.shape

    def body(x_ref, g_ref, dh_ref, dres_ref, dx_ref, dg_ref):
        dx, dgt = _rms_bwd_math(x_ref[...], g_ref[...], dh_ref[...])
        dx_ref[...] = dres_ref[...] + dx
        part = jnp.sum(dgt, axis=0, keepdims=True)

        @pl.when(pl.program_id(0) == 0)
        def _():
            dg_ref[...] = part

        @pl.when(pl.program_id(0) > 0)
        def _():
            dg_ref[...] += part

    return pl.pallas_call(
        body, name=name, out_shape=(jax.ShapeDtypeStruct((s, d), F32), jax.ShapeDtypeStruct((1, d), F32)),
        grid=(s // ROWS,), in_specs=[_row_spec(d), _fix_spec(1, d), _row_spec(d), _row_spec(d)],
        out_specs=(_row_spec(d), _fix_spec(1, d)), compiler_params=_params(("arbitrary",)),
    )(x, g, dh, dres)


def loss_head(x, g, target, name):
    s, d = x.shape

    def body(x_ref, g_ref, t_ref, dx_ref, dg_ref, loss_ref):
        xv, gv = x_ref[...], g_ref[...]
        rstd = lax.rsqrt(jnp.mean(xv * xv, axis=-1, keepdims=True) + RMS_EPS)
        err = xv * rstd * gv - t_ref[...]
        dx, dgt = _rms_bwd_math(xv, gv, err * (1.0 / d))
        dx_ref[...] = dx
        part = jnp.sum(dgt, axis=0, keepdims=True)
        lpart = jnp.full((1, LANES), 0.5 / d, F32) * jnp.sum(err * err)

        @pl.when(pl.program_id(0) == 0)
        def _():
            dg_ref[...] = part
            loss_ref[...] = lpart

        @pl.when(pl.program_id(0) > 0)
        def _():
            dg_ref[...] += part
            loss_ref[...] += lpart

    return pl.pallas_call(
        body, name=name,
        out_shape=(jax.ShapeDtypeStruct((s, d), F32), jax.ShapeDtypeStruct((1, d), F32), jax.ShapeDtypeStruct((1, LANES), F32)),
        grid=(s // ROWS,), in_specs=[_row_spec(d), _fix_spec(1, d), _row_spec(d)],
        out_specs=(_row_spec(d), _fix_spec(1, d), _fix_spec(1, LANES)), compiler_params=_params(("arbitrary",)),
    )(x, g, target)


def ffn_in_swiglu(h, w_in, name, tm=1024):
    s, d = h.shape
    cols = w_in.shape[2]

    def body(h_ref, wg_ref, wu_ref, g_ref, u_ref, a_ref):
        hv = h_ref[...]
        gv, uv = _dot(hv, wg_ref[...]), _dot(hv, wu_ref[...])
        g_ref[...] = gv.astype(BF16)
        u_ref[...] = uv.astype(BF16)
        a_ref[...] = (gv * (1.0 / (1.0 + jnp.exp(-gv))) * uv).astype(BF16)

    tile = pl.BlockSpec((tm, cols), lambda i, j: (i, j))
    out = jax.ShapeDtypeStruct((s, 2 * cols), BF16)
    return pl.pallas_call(
        body, name=name, out_shape=(out, out, out), grid=(s // tm, 2),
        in_specs=[pl.BlockSpec((tm, d), lambda i, j: (i, 0)), pl.BlockSpec((None, d, cols), lambda i, j: (j, 0, 0)),
                  pl.BlockSpec((None, d, cols), lambda i, j: (j + 2, 0, 0))],
        out_specs=(tile, tile, tile), compiler_params=_params(("parallel", "parallel")),
    )(h, w_in, w_in)


def swiglu_bwd(gate, up, dact, name):
    s, f = gate.shape

    def body(g_ref, u_ref, da_ref, o_ref):
        gv, uv, da = g_ref[...].astype(F32), u_ref[...].astype(F32), da_ref[...].astype(F32)
        sg = 1.0 / (1.0 + jnp.exp(-gv))
        o_ref[:, :f] = (da * uv * sg * (1.0 + gv * (1.0 - sg))).astype(BF16)
        o_ref[:, f:] = (da * gv * sg).astype(BF16)

    return pl.pallas_call(
        body, name=name, out_shape=jax.ShapeDtypeStruct((s, 2 * f), BF16), grid=(s // ROWS,),
        in_specs=[_row_spec(f)] * 3, out_specs=_row_spec(2 * f), compiler_params=_params(("parallel",)),
    )(gate, up, dact)


Q_OFF, K_OFF, V_OFF = 0, 8, 16


KB = 512
BQ = 512
SUB = KB // BLK


def _softplus_parts(z):
    sp = jnp.log(1.0 + jnp.exp(-jnp.abs(z)))
    ls = jnp.minimum(z, 0.0) - sp
    return ls, ls - z


def _wide(t):
    return jnp.concatenate([t] * SUB, axis=1)


def _chunk_dots(x, tri):
    terms = []
    for u in range(SUB):
        xu = x[:, u * BLK:(u + 1) * BLK]
        hi = xu.astype(BF16)
        terms += [hi, (xu - hi.astype(F32)).astype(BF16)]
    r = _dot(jnp.concatenate(terms, axis=0), tri)
    rows = x.shape[0]
    piece = lambda n: r[n * rows:(n + 1) * rows]
    return [piece(2 * u) + piece(2 * u + 1) for u in range(SUB)]


def _block_suffix_sums(x, suffix, c):
    loc = _chunk_dots(x, suffix)
    out = [None] * SUB
    for u in reversed(range(SUB)):
        out[u] = loc[u] + c
        c = c + jnp.sum(x[:, u * BLK:(u + 1) * BLK], axis=1, keepdims=True)
    return jnp.concatenate(out, axis=1), c


def _block_prefix_sums(x, tri, c):
    loc = _chunk_dots(x, tri)
    out = []
    for u in range(SUB):
        out.append(loc[u] + c)
        c = c + jnp.sum(x[:, u * BLK:(u + 1) * BLK], axis=1, keepdims=True)
    return jnp.concatenate(out, axis=1), c


def causal_fwd(qkv, npairs, mode, name, fq=None, fk=None, rider=None):
    s = qkv.shape[0]
    nq = s // BQ
    fox = mode == "fox"

    def body(*refs):
        if fox:
            q_ref, k_ref, v_ref, fq_ref, fk_ref, o_ref, st_ref = refs
        else:
            q_ref, k_ref, v_ref, o_ref, st_ref = refs
        i = pl.program_id(1)
        nkb = (i * BQ + BQ - 1) // KB + 1
        row, lane = _iotas((BQ, KB))
        row_s, lane_s = _iotas()
        _, lane_q = _iotas((BQ, LANES))
        nfull = (i * BQ) // KB
        qpos = i * BQ + row
        qf = q_ref[...].astype(F32) * 0.125
        hms = (lane_q < HEAD_DIM, lane_q >= HEAD_DIM)
        qas = [jnp.where(hm, qf, 0.0).astype(BF16) for hm in hms]
        suffix = jnp.where(row_s > lane_s, 1.0, 0.0).astype(BF16)
        zero = jnp.zeros((BQ, LANES), F32)
        col0 = jnp.zeros((BQ, 1), F32)

        def kv(j):
            r0 = pl.multiple_of(j * KB, KB)
            return r0, k_ref[pl.ds(r0, KB), :], v_ref[pl.ds(r0, KB), :]

        if fox:
            fqs = [_wide(fq_ref[a]) for a in range(2)]

            def step(j, carry, masked):
                r0, kb, vb = kv(j)
                new = []
                for a in range(2):
                    acc, mx, l = carry[3 * a:3 * a + 3]
                    z = _dot_nt(qas[a], kb) + fqs[a] - fk_ref[a:a + 1, pl.ds(r0, KB)]
                    if masked:
                        z = jnp.where(r0 + lane <= qpos, z, NEG)
                    mnew = jnp.maximum(mx, jnp.max(z, axis=1, keepdims=True))
                    p = jnp.exp(z - mnew)
                    alpha = jnp.exp(mx - mnew)
                    new += [alpha * acc + _dot(p.astype(BF16), vb), mnew, alpha * l + jnp.sum(p, axis=1, keepdims=True)]
                return tuple(new)

            neg = jnp.full((BQ, 1), NEG, F32)
            res = lax.fori_loop(0, nfull, functools.partial(step, masked=False), (zero, neg, col0, zero, neg, col0))
            res = lax.fori_loop(nfull, nkb, functools.partial(step, masked=True), res)
            outs = [res[3 * a] / res[3 * a + 2] for a in range(2)]
            stats = [res[3 * a + 1] + jnp.log(res[3 * a + 2]) for a in range(2)]
        else:
            def step(j, carry, masked):
                r0, kb, vb = kv(j)
                strict = r0 + lane < qpos
                new = []
                for a in range(2):
                    acc, c = carry[2 * a:2 * a + 2]
                    ls, lm = _softplus_parts(_dot_nt(qas[a], kb))
                    if masked:
                        lm = jnp.where(strict, lm, 0.0)
                    between, c = _block_suffix_sums(lm, suffix, c)
                    aw = jnp.exp(ls + between)
                    if masked:
                        aw = jnp.where(strict, aw, 0.0)
                    new += [acc + _dot(aw.astype(BF16), vb), c]
                return tuple(new)

            res = lax.fori_loop(0, nkb - nfull, lambda jj, c: step(nkb - 1 - jj, c, True), (zero, col0, zero, col0))
            res = lax.fori_loop(0, nfull, lambda jj, c: step(nfull - 1 - jj, c, False), res)
            outs, stats = [res[0], res[2]], [res[1], res[3]]
        o_ref[...] = jnp.where(hms[0], outs[0], outs[1])
        for a in range(2):
            st_ref[a] = jnp.broadcast_to(stats[a], (BQ, LANES))

    col = lambda off: (lambda p, i: (0, off + p))
    in_specs = [pl.BlockSpec((BQ, LANES), lambda p, i: (i, Q_OFF + p)),
                pl.BlockSpec((s, LANES), col(K_OFF)), pl.BlockSpec((s, LANES), col(V_OFF))]
    ops = [qkv, qkv, qkv]
    if fox:
        in_specs += [pl.BlockSpec((2, BQ, LANES), lambda p, i: (p, i, 0)), pl.BlockSpec((None, 2, s), lambda p, i: (p, 0, 0))]
        ops += [fq, fk]
    (o, stat), rode = call_with_rider(
        body, name, rider, ops, in_specs,
        [jax.ShapeDtypeStruct((s, npairs * LANES), F32), jax.ShapeDtypeStruct((2 * npairs, s, LANES), F32)],
        [pl.BlockSpec((BQ, LANES), lambda p, i: (i, p)), pl.BlockSpec((2, BQ, LANES), lambda p, i: (p, i, 0))], [], (npairs, nq))
    return o, stat, rode


def causal_bwd(qkv, do, stat, npairs, mode, name, fq=None, fk=None, rider=None):
    s = qkv.shape[0]
    nq = s // BQ
    fox = mode == "fox"

    def body(*refs):
        if fox:
            q_ref, k_ref, v_ref, do_ref, st_ref, fq_ref, fk_ref, dq_ref, dk_ref, dv_ref, df_ref, p_s, dp_s = refs
        else:
            q_ref, k_ref, v_ref, do_ref, st_ref, dq_ref, dk_ref, dv_ref = refs
        i = pl.program_id(1)

        @pl.when(i == 0)
        def _():
            dk_ref[...] = jnp.zeros_like(dk_ref)
            dv_ref[...] = jnp.zeros_like(dv_ref)
            if fox:
                df_ref[...] = jnp.zeros_like(df_ref)

        nkb = (i * BQ + BQ - 1) // KB + 1
        nfull = (i * BQ) // KB
        row, lane = _iotas((BQ, KB))
        row_s, lane_s = _iotas()
        _, lane_q = _iotas((BQ, LANES))
        qpos = i * BQ + row
        qf = q_ref[...].astype(F32) * 0.125
        dov = do_ref[...]
        hms = (lane_q < HEAD_DIM, lane_q >= HEAD_DIM)
        qas = [jnp.where(hm, qf, 0.0).astype(BF16) for hm in hms]
        doas = [jnp.where(hm, dov, 0.0).astype(BF16) for hm in hms]
        stas = [_wide(st_ref[a]) for a in range(2)]
        zero = jnp.zeros((BQ, LANES), F32)
        col0 = jnp.zeros((BQ, 1), F32)

        def kv(j):
            r0 = pl.multiple_of(j * KB, KB)
            return r0, k_ref[pl.ds(r0, KB), :], v_ref[pl.ds(r0, KB), :]

        if fox:
            fqs = [_wide(fq_ref[a]) for a in range(2)]

            def probs(j, deltas, masked):
                r0, kb, vb = kv(j)
                new = []
                for a in range(2):
                    z = _dot_nt(qas[a], kb) + fqs[a] - fk_ref[a:a + 1, pl.ds(r0, KB)]
                    p = jnp.exp(z - stas[a])
                    if masked:
                        p = jnp.where(r0 + lane <= qpos, p, 0.0)
                    dp = _dot_nt(doas[a], vb)
                    p_s[a, j] = p
                    dp_s[a, j] = dp
                    new.append(deltas[a] + jnp.sum(p * dp, axis=1, keepdims=True))
                return tuple(new)

            deltas = lax.fori_loop(0, nfull, functools.partial(probs, masked=False), (col0, col0))
            deltas = lax.fori_loop(nfull, nkb, functools.partial(probs, masked=True), deltas)

            def step(j, dqs):
                r0, kb, _ = kv(j)
                new = []
                dk = jnp.zeros((KB, LANES), F32)
                dv = jnp.zeros((KB, LANES), F32)
                for a in range(2):
                    p = p_s[a, j]
                    ds = p * (dp_s[a, j] - deltas[a])
                    dsb = ds.astype(BF16)
                    dk += _dot_tn(dsb, qas[a])
                    dv += _dot_tn(p.astype(BF16), doas[a])
                    df_ref[a:a + 1, pl.ds(r0, KB)] -= jnp.sum(ds, axis=0, keepdims=True)
                    new.append(dqs[a] + _dot(dsb, kb))
                dk_ref[pl.ds(r0, KB), :] += dk
                dv_ref[pl.ds(r0, KB), :] += dv
                return tuple(new)

            dqs = lax.fori_loop(0, nkb, step, (zero, zero))
        else:
            incl = jnp.where(row_s <= lane_s, 1.0, 0.0).astype(BF16)
            excl = jnp.where(row_s < lane_s, 1.0, 0.0).astype(BF16)

            def step(j, carry, masked):
                r0, kb, vb = kv(j)
                strict = r0 + lane < qpos
                new = []
                dk = jnp.zeros((KB, LANES), F32)
                dv = jnp.zeros((KB, LANES), F32)
                for a in range(2):
                    dq, cm, cg = carry[3 * a:3 * a + 3]
                    ls, lm = _softplus_parts(_dot_nt(qas[a], kb))
                    if masked:
                        lm = jnp.where(strict, lm, 0.0)
                    beta = jnp.exp(ls)
                    upto, cm = _block_prefix_sums(lm, incl, cm)
                    aw = jnp.exp(ls + stas[a] - upto)
                    if masked:
                        aw = jnp.where(strict, aw, 0.0)
                    g = aw * _dot_nt(doas[a], vb)
                    pre, cg = _block_prefix_sums(g, excl, cg)
                    dz = g * (1.0 - beta) - pre * beta
                    if masked:
                        dz = jnp.where(strict, dz, 0.0)
                    dzb = dz.astype(BF16)
                    dk += _dot_tn(dzb, qas[a])
                    dv += _dot_tn(aw.astype(BF16), doas[a])
                    new += [dq + _dot(dzb, kb), cm, cg]
                dk_ref[pl.ds(r0, KB), :] += dk
                dv_ref[pl.ds(r0, KB), :] += dv
                return tuple(new)

            res = lax.fori_loop(0, nfull, functools.partial(step, masked=False), (zero, col0, col0, zero, col0, col0))
            res = lax.fori_loop(nfull, nkb, functools.partial(step, masked=True), res)
            dqs = (res[0], res[3])
        dq_ref[...] = (jnp.where(hms[0], dqs[0], dqs[1]) * 0.125).astype(BF16)

    col = lambda off: (lambda p, i: (0, off + p))
    blk = pl.BlockSpec((BQ, LANES), lambda p, i: (i, p))
    acc = pl.BlockSpec((s, LANES), lambda p, i: (0, p))
    st_spec = pl.BlockSpec((2, BQ, LANES), lambda p, i: (p, i, 0))
    in_specs = [pl.BlockSpec((BQ, LANES), lambda p, i: (i, Q_OFF + p)), pl.BlockSpec((s, LANES), col(K_OFF)),
                pl.BlockSpec((s, LANES), col(V_OFF)), blk, st_spec]
    ops = [qkv, qkv, qkv, do, stat]
    w = npairs * LANES
    out_shape = [jax.ShapeDtypeStruct((s, w), BF16), jax.ShapeDtypeStruct((s, w), F32), jax.ShapeDtypeStruct((s, w), F32)]
    out_specs = [blk, acc, acc]
    scratch = []
    if fox:
        fk_spec = pl.BlockSpec((None, 2, s), lambda p, i: (p, 0, 0))
        in_specs += [st_spec, fk_spec]
        ops += [fq, fk]
        out_shape.append(jax.ShapeDtypeStruct((npairs, 2, s), F32))
        out_specs.append(fk_spec)
        scratch = [pltpu.VMEM((2, s // KB, BQ, KB), F32)] * 2
    outs, rode = call_with_rider(body, name, rider, ops, in_specs, out_shape, out_specs, scratch, (npairs, nq))
    return (*outs, rode)


def forget_fwd(fl, bias, name):
    s = fl.shape[0]

    def body(fl_ref, b_ref, f_ref):
        row, lane = _iotas()
        lower = jnp.where(lane <= row, 1.0, 0.0).astype(BF16)

        def step(n, carry):
            r0 = pl.multiple_of(n * BLK, BLK)
            ls, _ = _softplus_parts(fl_ref[pl.ds(r0, BLK), :] + b_ref[...])
            blk = _dot_exact_rhs(lower, ls) + carry
            f_ref[pl.ds(r0, BLK), :] = blk
            return blk[BLK - 1:BLK, :]

        lax.fori_loop(0, s // BLK, step, jnp.zeros((1, LANES), F32))

    return pl.pallas_call(
        body, name=name, out_shape=jax.ShapeDtypeStruct((s, LANES), F32),
        in_specs=[pl.BlockSpec(memory_space=pltpu.VMEM)] * 2, out_specs=pl.BlockSpec(memory_space=pltpu.VMEM),
        compiler_params=_params(),
    )(fl, bias)


def forget_bwd(fl, bias, df, name):
    s = fl.shape[0]
    nb = s // BLK

    def body(fl_ref, b_ref, df_ref, o_ref, db_ref):
        row, lane = _iotas()
        upper = jnp.where(lane >= row, 1.0, 0.0).astype(BF16)

        def step(nn, carry):
            tail, db = carry
            r0 = pl.multiple_of((nb - 1 - nn) * BLK, BLK)
            dls = _dot_exact_rhs(upper, df_ref[pl.ds(r0, BLK), :]) + tail
            xv = fl_ref[pl.ds(r0, BLK), :] + b_ref[...]
            dfl = dls * (1.0 / (1.0 + jnp.exp(xv)))
            o_ref[pl.ds(r0, BLK), :] = dfl
            return dls[0:1, :], db + jnp.sum(dfl, axis=0, keepdims=True)

        _, db = lax.fori_loop(0, nb, step, (jnp.zeros((1, LANES), F32), jnp.zeros((1, LANES), F32)))
        db_ref[...] = db

    return pl.pallas_call(
        body, name=name, out_shape=(jax.ShapeDtypeStruct((s, LANES), F32), jax.ShapeDtypeStruct((1, LANES), F32)),
        in_specs=[pl.BlockSpec(memory_space=pltpu.VMEM)] * 3,
        out_specs=(pl.BlockSpec(memory_space=pltpu.VMEM), pl.BlockSpec(memory_space=pltpu.VMEM)),
        compiler_params=_params(),
    )(fl, bias, df)


def _rot_tables(s):
    inv = ROPE_THETA ** (-jnp.arange(ROT_HALF, dtype=F32) * 2.0 / (2 * ROT_HALF))
    ang = jnp.arange(s, dtype=F32)[:, None] * inv[None, :]
    cos, sin = jnp.cos(ang), jnp.sin(ang)
    z8 = jnp.zeros((s, ROT_HALF), F32)
    rest = HEAD_DIM - 2 * ROT_HALF
    zr, onr = jnp.zeros((s, rest), F32), jnp.ones((s, rest), F32)
    tile = lambda t: jnp.tile(t, (1, 2))
    return tile(jnp.concatenate([cos, cos, onr], 1)), tile(jnp.concatenate([-sin, z8, zr], 1)), tile(jnp.concatenate([z8, sin, zr], 1))


def rotary_prep(qkv, tables, name):
    s = qkv.shape[0]
    w = 4 * LANES

    def body(q_ref, k_ref, v_ref, c_ref, s1_ref, s2_ref, qo_ref, ko_ref, vo_ref):
        c, s1, s2 = c_ref[...], s1_ref[...], s2_ref[...]

        def rot(xv):
            return xv * c + pltpu.roll(xv, LANES - ROT_HALF, 1) * s1 + pltpu.roll(xv, ROT_HALF, 1) * s2

        qo_ref[...] = rot(q_ref[...].astype(F32)) * 0.125
        ko_ref[...] = rot(k_ref[...].astype(F32))
        vo_ref[...] = v_ref[...].astype(F32)

    cb = lambda off: pl.BlockSpec((ROWS, LANES), lambda i, j: (i, off + j))
    tb = pl.BlockSpec((ROWS, LANES), lambda i, j: (i, 0))
    out = jax.ShapeDtypeStruct((s, w), F32)
    return pl.pallas_call(
        body, name=name, out_shape=(out, out, out), grid=(s // ROWS, 4),
        in_specs=[cb(Q_OFF + 4), cb(K_OFF + 4), cb(V_OFF + 4), tb, tb, tb], out_specs=(cb(0), cb(0), cb(0)),
        compiler_params=_params(("parallel", "parallel")),
    )(qkv, qkv, qkv, *tables)


def rotary_bwd(dq, dk, dv, tables, name):
    s, w = dq.shape

    def body(dq_ref, dk_ref, dv_ref, c_ref, s1_ref, s2_ref, qo_ref, ko_ref, vo_ref):
        c, s1, s2 = c_ref[...], s1_ref[...], s2_ref[...]

        def rot_t(dy):
            return dy * c + pltpu.roll(dy * s1, ROT_HALF, 1) + pltpu.roll(dy * s2, LANES - ROT_HALF, 1)

        qo_ref[...] = (rot_t(dq_ref[...]) * 0.125).astype(BF16)
        ko_ref[...] = rot_t(dk_ref[...]).astype(BF16)
        vo_ref[...] = dv_ref[...].astype(BF16)

    cb = pl.BlockSpec((ROWS, LANES), lambda i, j: (i, j))
    tb = pl.BlockSpec((ROWS, LANES), lambda i, j: (i, 0))
    out = jax.ShapeDtypeStruct((s, w), BF16)
    return pl.pallas_call(
        body, name=name, out_shape=(out, out, out), grid=(s // ROWS, w // LANES),
        in_specs=[cb, cb, cb, tb, tb, tb], out_specs=(cb, cb, cb), compiler_params=_params(("parallel", "parallel")),
    )(dq, dk, dv, *tables)


def _deinterleave(dst, src_ref, stride, s, dtype):
    length = s // stride
    for r in range(stride):
        if stride == 1:
            dst[...] = src_ref[...].astype(dtype)
        else:
            dst[r * length:(r + 1) * length, :] = src_ref[pl.ds(r, length, stride=stride), :].astype(dtype)


def _band_masks(row, lane, first):
    return lane <= row, lane >= row + jnp.where(first, BLK, 0)


def dilated_fwd(qd, kd, vd, name, rider=None):
    s, w = qd.shape
    npairs = w // LANES
    nblk = s // BLK

    def body(q_ref, k_ref, v_ref, o_ref, lse_ref, qs, ks, vs, od, ld, on, ln):
        row, lane = _iotas()
        for pi, stride in enumerate(DIL_STRIDES):
            per = (s // stride) // BLK
            _deinterleave(qs, q_ref, stride, s, BF16)
            _deinterleave(ks, k_ref, stride, s, BF16)
            _deinterleave(vs, v_ref, stride, s, BF16)

            def block(b, carry):
                r0 = pl.multiple_of(b * BLK, BLK)
                rp = pl.multiple_of(jnp.maximum(b - 1, 0) * BLK, BLK)
                mc, mp = _band_masks(row, lane, b % per == 0)
                q = qs[pl.ds(r0, BLK), :]
                kc, kp, vc, vp = ks[pl.ds(r0, BLK), :], ks[pl.ds(rp, BLK), :], vs[pl.ds(r0, BLK), :], vs[pl.ds(rp, BLK), :]
                out = jnp.zeros((BLK, LANES), F32)
                lse = jnp.zeros((BLK, LANES), F32)
                for a in range(2):
                    hm = (lane < HEAD_DIM) if a == 0 else (lane >= HEAD_DIM)
                    qa = jnp.where(hm, q.astype(F32), 0.0).astype(BF16)
                    sc = jnp.where(mc, _dot_nt(qa, kc), NEG)
                    sp = jnp.where(mp, _dot_nt(qa, kp), NEG)
                    mx = jnp.maximum(jnp.max(sc, axis=1, keepdims=True), jnp.max(sp, axis=1, keepdims=True))
                    pc, pp = jnp.exp(sc - mx), jnp.exp(sp - mx)
                    l = jnp.sum(pc, axis=1, keepdims=True) + jnp.sum(pp, axis=1, keepdims=True)
                    oa = (_dot(pc.astype(BF16), vc) + _dot(pp.astype(BF16), vp)) / l
                    out = jnp.where(hm, oa, out)
                    lse = jnp.where(hm, mx + jnp.log(l), lse)
                od[pl.ds(r0, BLK), :] = out
                ld[pl.ds(r0, BLK), :] = lse
                return carry

            lax.fori_loop(0, nblk, block, 0, unroll=2)
            length = s // stride
            for r in range(stride):
                if stride == 1:
                    on[pi] = od[...]
                    ln[pi] = ld[...]
                else:
                    on[pi, pl.ds(r, length, stride=stride), :] = od[r * length:(r + 1) * length, :]
                    ln[pi, pl.ds(r, length, stride=stride), :] = ld[r * length:(r + 1) * length, :]

        def merge(n, carry):
            r0 = pl.multiple_of(n * BLK, BLK)
            ls = [ln[pi, pl.ds(r0, BLK), :] for pi in range(3)]
            mx = jnp.maximum(jnp.maximum(ls[0], ls[1]), ls[2])
            ws = [jnp.exp(lv - mx) for lv in ls]
            den = ws[0] + ws[1] + ws[2]
            num = ws[0] * on[0, pl.ds(r0, BLK), :] + ws[1] * on[1, pl.ds(r0, BLK), :] + ws[2] * on[2, pl.ds(r0, BLK), :]
            o_ref[pl.ds(r0, BLK), :] = num / den
            lse_ref[pl.ds(r0, BLK), :] = mx + jnp.log(den)
            return carry

        lax.fori_loop(0, nblk, merge, 0, unroll=2)

    colspec = pl.BlockSpec((s, LANES), lambda p: (0, p))
    out = jax.ShapeDtypeStruct((s, w), F32)
    scratch = [pltpu.VMEM((s, LANES), BF16)] * 3 + [pltpu.VMEM((s, LANES), F32)] * 2 + [pltpu.VMEM((3, s, LANES), F32)] * 2
    (o, lse), rode = call_with_rider(body, name, rider, [qd, kd, vd], [colspec] * 3, [out, out], [colspec, colspec], scratch, (npairs,))
    return o, lse, rode


def dilated_bwd(qd, kd, vd, do, out, lse, do_off, name, rider=None):
    s, w = qd.shape
    npairs = w // LANES
    nblk = s // BLK

    def body(q_ref, k_ref, v_ref, do_ref, out_ref, lse_ref, dq_ref, dk_ref, dv_ref, qs, ks, vs, dos, dls, lss, dqd, dkd, dvd, dln):
        row, lane = _iotas()
        same_head = jnp.where((row < HEAD_DIM) == (lane < HEAD_DIM), 1.0, 0.0).astype(BF16)

        def delta_blk(n, carry):
            r0 = pl.multiple_of(n * BLK, BLK)
            dln[pl.ds(r0, BLK), :] = _dot_exact_lhs(do_ref[pl.ds(r0, BLK), :] * out_ref[pl.ds(r0, BLK), :], same_head)
            return carry

        lax.fori_loop(0, nblk, delta_blk, 0, unroll=2)
        for pi, stride in enumerate(DIL_STRIDES):
            per = (s // stride) // BLK
            _deinterleave(qs, q_ref, stride, s, BF16)
            _deinterleave(ks, k_ref, stride, s, BF16)
            _deinterleave(vs, v_ref, stride, s, BF16)
            _deinterleave(dos, do_ref, stride, s, BF16)
            _deinterleave(dls, dln, stride, s, F32)
            _deinterleave(lss, lse_ref, stride, s, F32)

            def block(b, carry):
                r0 = pl.multiple_of(b * BLK, BLK)
                rp = pl.multiple_of(jnp.maximum(b - 1, 0) * BLK, BLK)
                first = b % per == 0
                mc, mp = _band_masks(row, lane, first)
                q, dov = qs[pl.ds(r0, BLK), :], dos[pl.ds(r0, BLK), :]
                kc, kp, vc, vp = ks[pl.ds(r0, BLK), :], ks[pl.ds(rp, BLK), :], vs[pl.ds(r0, BLK), :], vs[pl.ds(rp, BLK), :]
                lse_t, dl_t = lss[pl.ds(r0, BLK), :], dls[pl.ds(r0, BLK), :]
                dq = jnp.zeros((BLK, LANES), F32)
                dkc = jnp.zeros((BLK, LANES), F32)
                dkp = jnp.zeros((BLK, LANES), F32)
                dvc = jnp.zeros((BLK, LANES), F32)
                dvp = jnp.zeros((BLK, LANES), F32)
                for a in range(2):
                    hm = (lane < HEAD_DIM) if a == 0 else (lane >= HEAD_DIM)
                    pick = lane == a * HEAD_DIM
                    qa = jnp.where(hm, q.astype(F32), 0.0).astype(BF16)
                    doa = jnp.where(hm, dov.astype(F32), 0.0).astype(BF16)
                    lse_a = jnp.sum(jnp.where(pick, lse_t, 0.0), axis=1, keepdims=True)
                    dl_a = jnp.sum(jnp.where(pick, dl_t, 0.0), axis=1, keepdims=True)
                    pc = jnp.where(mc, jnp.exp(_dot_nt(qa, kc) - lse_a), 0.0)
                    pp = jnp.where(mp, jnp.exp(_dot_nt(qa, kp) - lse_a), 0.0)
                    dsc = (pc * (_dot_nt(doa, vc) - dl_a)).astype(BF16)
                    dsp = (pp * (_dot_nt(doa, vp) - dl_a)).astype(BF16)
                    dq = jnp.where(hm, _dot(dsc, kc) + _dot(dsp, kp), dq)
                    dkc += _dot_tn(dsc, qa)
                    dkp += _dot_tn(dsp, qa)
                    dvc += _dot_tn(pc.astype(BF16), doa)
                    dvp += _dot_tn(pp.astype(BF16), doa)
                dqd[pl.ds(r0, BLK), :] = dq
                dkd[pl.ds(r0, BLK), :] = dkc
                dvd[pl.ds(r0, BLK), :] = dvc

                @pl.when(jnp.logical_not(first))
                def _():
                    dkd[pl.ds(rp, BLK), :] += dkp
                    dvd[pl.ds(rp, BLK), :] += dvp

                return carry

            lax.fori_loop(0, nblk, block, 0, unroll=2)
            length = s // stride
            for dst, src in ((dq_ref, dqd), (dk_ref, dkd), (dv_ref, dvd)):
                for r in range(stride):
                    if stride == 1:
                        dst[...] = src[...]
                    else:
                        dst[pl.ds(r, length, stride=stride), :] += src[r * length:(r + 1) * length, :]

    colspec = pl.BlockSpec((s, LANES), lambda p: (0, p))
    do_spec = pl.BlockSpec((s, LANES), lambda p: (0, do_off + p))
    o3 = jax.ShapeDtypeStruct((s, w), F32)
    scratch = [pltpu.VMEM((s, LANES), BF16)] * 4 + [pltpu.VMEM((s, LANES), F32)] * 6
    outs, rode = call_with_rider(body, name, rider, [qd, kd, vd, do, out, lse], [colspec, colspec, colspec, do_spec, colspec, colspec],
                                 [o3, o3, o3], [colspec, colspec, colspec], scratch, (npairs,))
    return (*outs, rode)


def adamw(w, g, m, v, name):
    rows, cols = w.shape
    rb = min(rows, ROWS)
    c1 = 1.0 - ADAM_B1 ** ADAM_STEP
    c2 = 1.0 - ADAM_B2 ** ADAM_STEP

    def body(w_ref, g_ref, m_ref, v_ref, d_ref, mo_ref, vo_ref):
        gv = g_ref[...]
        mn = ADAM_B1 * m_ref[...] + (1.0 - ADAM_B1) * gv
        vn = ADAM_B2 * v_ref[...] + (1.0 - ADAM_B2) * (gv * gv)
        d_ref[...] = -ADAM_LR * ((mn / c1) / (jnp.sqrt(vn / c2) + ADAM_EPS) + ADAM_WD * w_ref[...])
        mo_ref[...] = mn
        vo_ref[...] = vn

    spec = _row_spec(cols, rb)
    out = jax.ShapeDtypeStruct((rows, cols), F32)
    return pl.pallas_call(
        body, name=name, out_shape=(out, out, out), grid=(rows // rb,), in_specs=[spec] * 4, out_specs=(spec,) * 3,
        compiler_params=_params(("parallel",)),
    )(w, g, m, v)


def _prefetch_call(body, name, scalar, ops, grid, in_specs, out_specs, out_shape, sem):
    spec = pltpu.PrefetchScalarGridSpec(num_scalar_prefetch=1, grid=grid, in_specs=in_specs, out_specs=out_specs)
    return pl.pallas_call(body, name=name, grid_spec=spec, out_shape=out_shape, compiler_params=_params(sem))(scalar, *ops)


def pair_sum(g, got, core, name):
    nc, r, c = g.shape
    rh = r // 2

    def body(core_ref, g_ref, got_ref, o_ref):
        o_ref[...] = (g_ref[...].astype(F32) + got_ref[...].astype(F32)).astype(BF16)

    blk = lambda rows_of: pl.BlockSpec((None, rh, c), rows_of)
    return _prefetch_call(
        body, name, core, (g, got), (nc,),
        [blk(lambda j, core_ref: (j, core_ref[0], 0)), blk(lambda j, core_ref: (j, 0, 0))],
        blk(lambda j, core_ref: (j, 0, 0)), jax.ShapeDtypeStruct((nc, rh, c), BF16), ("parallel",))


def chip_sum(pair, got, chip, layer, into, name):
    _, rh, c = pair.shape

    def body(chip_ref, p_ref, a_ref, b_ref, c_ref, old_ref, o_ref):
        o_ref[...] = ((p_ref[...].astype(F32) + a_ref[...].astype(F32)) + b_ref[...].astype(F32)) + c_ref[...].astype(F32)

    arrival = lambda k: pl.BlockSpec((None, rh, c), lambda i, chip_ref: (k, 0, 0))
    spec = pltpu.PrefetchScalarGridSpec(
        num_scalar_prefetch=1, grid=(1,),
        in_specs=[pl.BlockSpec((None, rh, c), lambda i, chip_ref: (chip_ref[0], 0, 0)), arrival(0), arrival(1), arrival(2), _ANY],
        out_specs=pl.BlockSpec((None, rh, c), lambda i, chip_ref: (layer, 0, 0)))
    return pl.pallas_call(body, name=name, grid_spec=spec, out_shape=jax.ShapeDtypeStruct(into.shape, into.dtype),
                          input_output_aliases={5: 0}, compiler_params=_params(("arbitrary",)))(chip, pair, got, got, got, into)


def adamw_family(w, m, v, g_mine, g_other, core, name):
    nl, r, c = w.shape
    gc = g_mine.shape[2]
    rh = r // 2
    nb = 4 if rh % 512 == 0 else (2 if rh % 16 == 0 and rh > 256 else 1)
    rb = rh // nb
    c1 = 1.0 - ADAM_B1 ** ADAM_STEP
    c2 = 1.0 - ADAM_B2 ** ADAM_STEP

    def body(core_ref, w_ref, m_ref, v_ref, gm_ref, go_ref, g_ref, d_ref, mo_ref, vo_ref):
        gv = jnp.where(pl.program_id(1) == core_ref[0], gm_ref[...], go_ref[...])[:, :c]
        mn = ADAM_B1 * m_ref[...] + (1.0 - ADAM_B1) * gv
        vn = ADAM_B2 * v_ref[...] + (1.0 - ADAM_B2) * (gv * gv)
        g_ref[...] = gv
        d_ref[...] = -ADAM_LR * ((mn / c1) / (jnp.sqrt(vn / c2) + ADAM_EPS) + ADAM_WD * w_ref[...])
        mo_ref[...] = mn
        vo_ref[...] = vn

    full = pl.BlockSpec((None, rb, c), lambda l, h, i, core_ref: (l, h * nb + i, 0))
    half = pl.BlockSpec((None, rb, gc), lambda l, h, i, core_ref: (l, i, 0))
    out = jax.ShapeDtypeStruct((nl, r, c), F32)
    return _prefetch_call(body, name, core, (w, m, v, g_mine, g_other), (nl, 2, nb), [full, full, full, half, half],
                          (full, full, full, full), (out, out, out, out), ("parallel", "parallel", "parallel"))


def _coords():
    return lax.axis_index("x"), lax.axis_index("y"), lax.axis_index("c")


def _other_chips(x, y):
    return ((1 - x, y), (x, 1 - y), (1 - x, 1 - y))


_ANY = pl.BlockSpec(memory_space=pl.ANY)


def _exchange_call(body, name, arrays, out_shapes, n_copies, n_local=0):
    n = len(arrays)

    def wrapped(*refs):
        body(refs[:n], refs[n:n + len(out_shapes)], *refs[n + len(out_shapes):])

    scratch = [pltpu.SemaphoreType.DMA((n_copies,)), pltpu.SemaphoreType.DMA((n_copies,))]
    if n_local:
        scratch.append(pltpu.SemaphoreType.DMA((n_local,)))
    return pl.pallas_call(
        wrapped, name=name, out_shape=tuple(out_shapes), in_specs=[_ANY] * n, out_specs=tuple([_ANY] * len(out_shapes)),
        scratch_shapes=scratch, compiler_params=_params(),
    )(*arrays)


def _remote(send_sems, recv_sems, n, src, dst, to):
    return pltpu.make_async_remote_copy(src_ref=src, dst_ref=dst, send_sem=send_sems.at[n], recv_sem=recv_sems.at[n],
                                        device_id=to, device_id_type=MESH)


class Rider:
    def __init__(self, arrays, out_shapes, n_remote, n_local, copies, then=None):
        self.arrays, self.out_shapes, self.n_remote, self.n_local = list(arrays), list(out_shapes), n_remote, n_local
        self.copies, self.then = copies, then

    def sems(self):
        return [pltpu.SemaphoreType.DMA((self.n_remote,)), pltpu.SemaphoreType.DMA((self.n_remote,)),
                pltpu.SemaphoreType.DMA((max(self.n_local, 1),))]

    def run(self, name):
        n, no = len(self.arrays), len(self.out_shapes)

        def body(*refs):
            for stage in (self.copies, self.then):
                if stage is not None:
                    cps = stage(refs[:n], refs[n:n + no], *refs[n + no:])
                    for cp in cps:
                        cp.start()
                    for cp in cps:
                        cp.wait()

        return pl.pallas_call(
            body, name=name, out_shape=tuple(self.out_shapes), in_specs=[_ANY] * n, out_specs=tuple([_ANY] * no),
            scratch_shapes=self.sems(), compiler_params=_params(),
        )(*self.arrays)


def ride(rider, body, n_in, n_out, grid):
    if rider is None:
        return body
    ni, no = len(rider.arrays), len(rider.out_shapes)

    def wrapped(*refs):
        ins, r_in = refs[:n_in], refs[n_in:n_in + ni]
        outs = refs[n_in + ni:n_in + ni + n_out]
        r_out = refs[n_in + ni + n_out:n_in + ni + n_out + no]
        rest = refs[n_in + ni + n_out + no:]
        scratch, sems = rest[:len(rest) - 3], rest[len(rest) - 3:]
        step, total = 0, 1
        for a, g in enumerate(grid):
            step, total = step * g + pl.program_id(a), total * g
        assert total >= 3
        relay_at = (7 * total) // 8 if rider.then is not None else total - 1

        @pl.when(step == 0)
        def _():
            for cp in rider.copies(r_in, r_out, *sems):
                cp.start()

        body(*ins, *outs, *scratch)

        @pl.when(step == relay_at)
        def _():
            for cp in rider.copies(r_in, r_out, *sems):
                cp.wait()
            if rider.then is not None:
                for cp in rider.then(r_in, r_out, *sems):
                    cp.start()

        if rider.then is not None:
            @pl.when(step == total - 1)
            def _():
                for cp in rider.then(r_in, r_out, *sems):
                    cp.wait()

    return wrapped


def call_with_rider(body, name, rider, ops, in_specs, out_shape, out_specs, scratch, grid):
    n_in, n_out = len(ops), len(out_shape)
    ops, in_specs, out_shape, out_specs, scratch = list(ops), list(in_specs), list(out_shape), list(out_specs), list(scratch)
    if rider is not None:
        ops += rider.arrays
        in_specs += [_ANY] * len(rider.arrays)
        out_shape += rider.out_shapes
        out_specs += [_ANY] * len(rider.out_shapes)
        scratch += rider.sems()
    res = pl.pallas_call(
        ride(rider, body, n_in, n_out, grid), name=name, out_shape=tuple(out_shape), grid=grid, in_specs=in_specs,
        out_specs=tuple(out_specs), scratch_shapes=scratch, compiler_params=_params(("arbitrary",) * len(grid)),
    )(*ops)
    return tuple(res[:n_out]), list(res[n_out:])


def gather_rider(shards):
    nf = len(shards)
    half = lambda ref, which: pl.ds(which * (ref.shape[-2] // 2), ref.shape[-2] // 2)

    def copies(s_refs, o_refs, send_sems, recv_sems, local_sems):
        x, y, c = _coords()
        me = 2 * x + y
        cps = [pltpu.make_async_copy(s_refs[f], o_refs[f].at[me], local_sems.at[f]) for f in range(nf)]
        for k, (px, py) in enumerate(_other_chips(x, y)):
            for f in range(nf):
                rows = half(s_refs[f], c)
                cps.append(_remote(send_sems, recv_sems, k * nf + f, s_refs[f].at[rows], o_refs[f].at[me, rows], (px, py, c)))
        return cps

    def relay(s_refs, o_refs, send_sems, recv_sems, local_sems):
        x, y, c = _coords()
        cps = []
        for k, (px, py) in enumerate(_other_chips(x, y)):
            for f in range(nf):
                landed = o_refs[f].at[2 * px + py, half(s_refs[f], c)]
                cps.append(_remote(send_sems, recv_sems, (3 + k) * nf + f, landed, landed, (x, y, 1 - c)))
        return cps

    return Rider(shards, [jax.ShapeDtypeStruct((N_CHIPS,) + sh.shape, sh.dtype) for sh in shards], 6 * nf, nf, copies, relay)


def scatter_rider(pairs):
    nf = len(pairs)

    def copies(p_refs, o_refs, send_sems, recv_sems, local_sems):
        x, y, c = _coords()
        cps = []
        for k, (px, py) in enumerate(_other_chips(x, y)):
            for f in range(nf):
                cps.append(_remote(send_sems, recv_sems, k * nf + f, p_refs[f].at[2 * px + py], o_refs[f].at[k], (px, py, c)))
        return cps

    return Rider(pairs, [jax.ShapeDtypeStruct((3,) + p.shape[1:], p.dtype) for p in pairs], 3 * nf, 0, copies)


def pair_swap(grads, name):
    def body(g_refs, o_refs, send_sems, recv_sems):
        x, y, c = _coords()
        cps = []
        for f, g_ref in enumerate(g_refs):
            rh = g_ref.shape[1] // 2
            cps.append(_remote(send_sems, recv_sems, f, g_ref.at[:, pl.ds((1 - c) * rh, rh), :], o_refs[f], (x, y, 1 - c)))
        for cp in cps:
            cp.start()
        for cp in cps:
            cp.wait()

    outs = [jax.ShapeDtypeStruct((g.shape[0], g.shape[1] // 2, g.shape[2]), g.dtype) for g in grads]
    return _exchange_call(body, name, grads, outs, len(grads))


def half_swap(halves, name):
    def body(h_refs, o_refs, send_sems, recv_sems):
        x, y, c = _coords()
        cps = [_remote(send_sems, recv_sems, f, h_ref, o_refs[f], (x, y, 1 - c)) for f, h_ref in enumerate(h_refs)]
        for cp in cps:
            cp.start()
        for cp in cps:
            cp.wait()

    return _exchange_call(body, name, halves, [jax.ShapeDtypeStruct(h.shape, h.dtype) for h in halves], len(halves))


def allsum_small(part, name):
    def body(p_ref, tot_ref, all_ref, send_sems, recv_sems):
        x, y, c = _coords()
        me, sibling = (x, y, c), (x, y, 1 - c)
        chips = _other_chips(x, y)

        def slot(px, py, pc):
            return all_ref.at[4 * px + 2 * py + pc]

        def copy(k, block, to, src=None):
            return pltpu.make_async_remote_copy(src_ref=slot(*block) if src is None else src, dst_ref=slot(*block),
                                                send_sem=send_sems.at[k], recv_sem=recv_sems.at[k], device_id=to, device_id_type=MESH)

        slot(*me)[...] = p_ref[...]
        first = [copy(0, me, sibling, src=p_ref)] + [copy(1 + j, me, (*chip, c), src=p_ref) for j, chip in enumerate(chips)]
        for cp in first:
            cp.start()
        passed = [copy(4 + j, (*chip, c), sibling) for j, chip in enumerate(chips)]
        for j, chip in enumerate(chips):
            copy(1 + j, (*chip, c), me).wait_recv()
            passed[j].start()
        copy(0, sibling, me).wait_recv()
        for j, chip in enumerate(chips):
            copy(4 + j, (*chip, 1 - c), me).wait_recv()
        for cp in first + passed:
            cp.wait_send()
        tot = all_ref[0]
        for d in range(1, 8):
            tot = tot + all_ref[d]
        tot_ref[...] = tot

    vm = pl.BlockSpec(memory_space=pltpu.VMEM)
    return pl.pallas_call(
        body, name=name, out_shape=jax.ShapeDtypeStruct(part.shape, F32), in_specs=[vm], out_specs=vm,
        scratch_shapes=[pltpu.VMEM((8,) + part.shape, F32), pltpu.SemaphoreType.DMA((7,)), pltpu.SemaphoreType.DMA((7,))],
        compiler_params=_params(),
    )(part)


QKVF_COLS = 772
QKVF_PAD = 896
FORWARD_CARRY = {0: (("fi0", "fo0"), ("qkv1", "o1", "fo1")), 1: (("fi1", "qkv2", "o2"), ()),
                 2: (("fi2", "fo2"), ("qkv3", "o3", "fo3")), 3: (("fi3",), ())}


def _tables_for(s):
    return _rot_tables(s)


def layer_families(layer):
    return (0, 1, layer // 2) if layer % 2 == 0 else (2, 3, layer // 2)


class GradientExchange:
    def __init__(self):
        self.core = lax.axis_index("c").astype(jnp.int32).reshape(1)
        self.chip = (2 * lax.axis_index("x") + lax.axis_index("y")).astype(jnp.int32).reshape(1)
        self.pairs, self.arrived, self.pending = {}, {}, []

    def add(self, items, tag):
        got = pair_swap([g for _, _, g in items], f"grad_pair_swap_{tag}")
        for (fam, li, g), r in zip(items, got):
            self.pairs[(fam, li)] = pair_sum(g, r, self.core, f"grad_pair_sum_{fam}_{li}")
            self.pending.append((fam, li))

    def rider(self, only=None):
        keys = [k for k in self.pending if only is None or k in only]
        self.pending = [k for k in self.pending if k not in keys]
        return (scatter_rider([self.pairs[k] for k in keys]) if keys else None), keys

    def landed(self, keys, outs):
        self.arrived.update(zip(keys, outs))

    def finish(self, weights, moments1, moments2):
        last, keys = self.rider()
        if last is not None:
            self.landed(keys, last.run("grad_chip_scatter_last"))
        mine = []
        for fam, w in enumerate(weights):
            buf = jnp.zeros((w.shape[0],) + self.pairs[(fam, 0)].shape[1:], F32)
            for li in range(w.shape[0]):
                buf = chip_sum(self.pairs[(fam, li)], self.arrived[(fam, li)], self.chip, li, buf, f"grad_chip_sum_{fam}_{li}")
            mine.append(buf)
        other = half_swap(mine, "grad_half_swap")
        return [adamw_family(w, m, v, gm, go, self.core, f"adamw_{f}")
                for f, (w, m, v, gm, go) in enumerate(zip(weights, moments1, moments2, mine, other))]


class KeepGradients:
    def __init__(self):
        self.grads = {}

    def add(self, items, tag):
        for fam, li, g in items:
            self.grads[(fam, li)] = g

    def rider(self, only=None):
        return None, []

    def landed(self, keys, outs):
        pass


def kernel(x, norm_mix, w_qkv_even, w_o_even, w_qkvf_odd, b_forget, w_o_odd, norm_ffn, w_ffn_in, w_ffn_out, norm_final, loss_target, m_norm_mix, m_w_qkv_even, m_w_o_even, m_w_qkvf_odd, m_b_forget, m_w_o_odd, m_norm_ffn, m_w_ffn_in, m_w_ffn_out, m_norm_final, v_norm_mix, v_w_qkv_even, v_w_o_even, v_w_qkvf_odd, v_b_forget, v_w_o_odd, v_norm_ffn, v_w_ffn_in, v_w_ffn_out, v_norm_final):
    w_shards = [w_qkv_even, w_o_even, w_qkvf_odd, w_o_odd, w_ffn_in, w_ffn_out]
    shards = [w.astype(BF16) for w in w_shards]
    tables = _tables_for(x.shape[1])
    bias_pad = jnp.pad(b_forget, ((0, 0), (0, LANES - N_HEADS)))

    mine = {}
    for layer in range(DEPTH):
        fam_qkv, fam_o, li = layer_families(layer)
        mine.update({f"qkv{layer}": shards[fam_qkv][li], f"o{layer}": shards[fam_o][li],
                     f"fi{layer}": shards[4][layer], f"fo{layer}": shards[5][layer]})
    fetch = lambda names: gather_rider([mine[n] for n in names])
    have = dict(zip(("qkv0", "o0"), fetch(("qkv0", "o0")).run("gather_first")))
    saved, cur = [], x[0]
    h1 = rmsnorm_fwd(cur, norm_mix[0:1], "l0_norm_mix")
    for layer in range(DEPTH):
        carry, side_carry = FORWARD_CARRY[layer]
        next_gain = norm_mix[layer + 1:layer + 2] if layer + 1 < DEPTH else None
        cur, h1, keep = forward_layer(layer, cur, h1, have, norm_ffn[layer:layer + 1], next_gain, tables,
                                      bias_pad[layer // 2:layer // 2 + 1], fetch, carry, side_carry)
        saved.append(keep)

    dcur, g_final, loss_part = loss_head(cur, norm_final.reshape(1, D_MODEL), loss_target[0], "loss_head")

    exchange = GradientExchange()
    g_mix, g_ffn, g_bias = [None] * DEPTH, [None] * DEPTH, [None] * (DEPTH // 2)
    for layer in reversed(range(DEPTH)):
        dcur, g_mix[layer], g_ffn[layer], g_b = backward_layer(layer, dcur, saved[layer], norm_mix[layer:layer + 1],
                                                               norm_ffn[layer:layer + 1], tables, bias_pad[layer // 2:layer // 2 + 1], exchange)
        if g_b is not None:
            g_bias[layer // 2] = g_b

    zero_row = jnp.zeros((1, D_MODEL), F32)
    pad16 = lambda v: jnp.pad(v, (0, D_MODEL - v.shape[0]))[None, :]
    small_rows = lambda mix, ffn, fin, bias, last: jnp.concatenate(
        [r.reshape(1, D_MODEL) for r in mix] + [r.reshape(1, D_MODEL) for r in ffn] + [fin.reshape(1, D_MODEL)]
        + [pad16(b) for b in bias] + [last] + [zero_row] * (SMALL_ROWS - 12), axis=0)
    loss_row = pad16(loss_part[0, :1])
    small_g = allsum_small(small_rows(g_mix, g_ffn, g_final, g_bias, loss_row), "allsum_small")
    loss = small_g[11, 0]
    small_g = small_g.at[11].set(0.0)
    sw = small_rows(list(norm_mix), list(norm_ffn), norm_final, list(b_forget), zero_row)
    sm = small_rows(list(m_norm_mix), list(m_norm_ffn), m_norm_final, list(m_b_forget), zero_row)
    sv = small_rows(list(v_norm_mix), list(v_norm_ffn), v_norm_final, list(v_b_forget), zero_row)
    sd, snm, snv = adamw(sw, small_g, sm, sv, "adamw_small")

    def small_out(a):
        return a[0:4], a[8, :], a[9:11, :N_HEADS], a[4:8]

    big = exchange.finish(w_shards, [m_w_qkv_even, m_w_o_even, m_w_qkvf_odd, m_w_o_odd, m_w_ffn_in, m_w_ffn_out],
                          [v_w_qkv_even, v_w_o_even, v_w_qkvf_odd, v_w_o_odd, v_w_ffn_in, v_w_ffn_out])

    def outputs(small, which):
        mix, fin, bias, ffn = small_out(small)
        qkv_e, o_e, qkvf, o_o, fi, fo = [big[f][which] for f in range(6)]
        return [mix, qkv_e, o_e, qkvf, bias, o_o, ffn, fi, fo, fin]

    return (loss, dcur[None], *outputs(small_g, 0), *outputs(sd, 1), *outputs(snm, 2), *outputs(snv, 3))


def _chip_tile(rows, cols, at):
    return pl.BlockSpec((None, rows, cols), at)


def forward_layer(layer, cur, h1, have, ffn_gain, next_gain, tables, bias_row, fetch=None, carry=(), side_carry=()):
    n = f"l{layer}"
    s = cur.shape[0]
    w_qkv, w_o = have[f"qkv{layer}"], have[f"o{layer}"]
    rider = fetch(carry) if carry else None
    side_rider = fetch(side_carry) if side_carry else None
    keep = {"x": cur, "h1": h1, "w_o": w_o.reshape(D_ATTN, D_MODEL)}
    side = []
    if layer % 2 == 0:
        qkv = matmul(h1, w_qkv, "nn", BF16, n + "_qkv", 1024, 768, 1024, mnk=(s, 3 * D_ATTN, D_MODEL),
                     b_spec=_chip_tile(D_MODEL, 768, lambda i, j, kk: (j, 0, 0)))
        o_sb, st, rode = causal_fwd(qkv, 4, "sb", n + "_sb_fwd", rider=rider)
        qd, kd, vd = rotary_prep(qkv, tables, n + "_rotary")
        o_dil, lse_dil, side = dilated_fwd(qd, kd, vd, n + "_dil_fwd", rider=side_rider)
        attn = jnp.concatenate([o_sb, o_dil], axis=1).astype(BF16)
        keep.update(qd=qd, kd=kd, vd=vd, o_dil=o_dil, lse_dil=lse_dil, w_qkv=w_qkv)
    else:
        natural = jnp.transpose(w_qkv, (1, 0, 2)).reshape(D_MODEL, N_CHIPS * QKVF_COLS)
        w_gate = jnp.pad(natural[:, 3 * D_ATTN:], ((0, 0), (0, LANES - N_HEADS)))
        qkv = matmul(h1, natural[:, :3 * D_ATTN], "nn", BF16, n + "_qkv", 1024, 768, 1024)
        fl = matmul(h1, w_gate, "nn", F32, n + "_fgate", 512, LANES, 1024)
        cum = forget_fwd(fl, bias_row, n + "_forget_fwd")
        f_heads = cum[:, :N_HEADS].T
        fq = jnp.broadcast_to(f_heads[:, :, None], (N_HEADS, s, LANES))
        fk = f_heads.reshape(N_HEADS // 2, 2, s)
        attn, st, rode = causal_fwd(qkv, 8, "fox", n + "_fox_fwd", fq=fq, fk=fk, rider=rider)
        attn = attn.astype(BF16)
        keep.update(fl=fl, fq=fq, fk=fk, w_qkv=jnp.concatenate([natural[:, :3 * D_ATTN], w_gate], axis=1))
    have.update(zip(carry, rode))
    have.update(zip(side_carry, side))
    w_fi, w_fo = have[f"fi{layer}"], have[f"fo{layer}"].reshape(D_FF, D_MODEL)
    mid, h2 = matmul(attn, keep["w_o"], "nn", F32, n + "_attn_out", 512, 1024, 1024, res=cur, norm_gain=ffn_gain)
    gate, up, act = ffn_in_swiglu(h2, w_fi, n + "_ffn_in")
    if next_gain is None:
        out, h_next = matmul(act, w_fo, "nn", F32, n + "_ffn_out", 512, 1024, D_FF, res=mid), None
    else:
        out, h_next = matmul(act, w_fo, "nn", F32, n + "_ffn_out", 512, 1024, D_FF, res=mid, norm_gain=next_gain)
    keep.update(qkv=qkv, st=st, attn=attn, mid=mid, h2=h2, gate=gate, up=up, act=act, w_fi=w_fi, w_fo=w_fo)
    return out, h_next, keep


def backward_layer(layer, dcur, kp, mix_gain, ffn_gain, tables, bias_row, exchange):
    n = f"l{layer}"
    s = dcur.shape[0]
    fam_qkv, fam_o, li = layer_families(layer)
    g_fo = matmul(kp["act"], dcur, "tn", BF16, n + "_d_w_ffn_out", 1408, 1024, s)
    dact = matmul(dcur, kp["w_fo"], "nt", BF16, n + "_d_act", 1024, 1408, 1024)
    dgu = swiglu_bwd(kp["gate"], kp["up"], dact, n + "_d_swiglu")
    g_fi = matmul(kp["h2"], dgu, "tn", BF16, n + "_d_w_ffn_in", 1024, 1408, 2048, mnk=(D_MODEL, 2 * D_FF, s),
                  o_spec=_chip_tile(D_MODEL, 1408, lambda i, j, kk: (j, 0, 0)), out_shape=(N_CHIPS, D_MODEL, 1408))
    all_chips = lambda cols: pl.BlockSpec((N_CHIPS, D_MODEL, cols), lambda i, j, kk: (0, 0, 0))
    dh2 = matmul(dgu, kp["w_fi"], "nt", F32, n + "_d_h2", 512, 1024, 2 * D_FF, mnk=(s, D_MODEL, 2 * D_FF), b_spec=all_chips(1408))
    dmid, g_ffn = rmsnorm_bwd(kp["mid"], ffn_gain, dh2, dcur, n + "_d_norm_ffn")
    g_o = matmul(kp["attn"], dmid, "tn", BF16, n + "_d_w_o", 1024, 1024, s)
    dattn = matmul(dmid, kp["w_o"], "nt", F32, n + "_d_attn", 1024, 1024, 1024)
    exchange.add([(5, layer, g_fo.reshape(N_CHIPS, D_FF // N_CHIPS, D_MODEL)), (4, layer, g_fi),
                  (fam_o, li, g_o.reshape(N_CHIPS, D_ATTN // N_CHIPS, D_MODEL))], f"l{layer}_ffn")
    g_bias = None
    if layer % 2 == 0:
        rider, keys = exchange.rider(only=[(4, layer), (fam_o, li)])
        dq_a, dk_a, dv_a, rode = causal_bwd(kp["qkv"], dattn, kp["st"], 4, "sb", n + "_sb_bwd", rider=rider)
        exchange.landed(keys, rode)
        rider, keys = exchange.rider()
        dqd, dkd, dvd, rode = dilated_bwd(kp["qd"], kp["kd"], kp["vd"], dattn, kp["o_dil"], kp["lse_dil"], 4, n + "_dil_bwd",
                                          rider=rider)
        dq_b, dk_b, dv_b = rotary_bwd(dqd, dkd, dvd, tables, n + "_d_rotary")
        dproj = jnp.concatenate([dq_a, dq_b, dk_a.astype(BF16), dk_b, dv_a.astype(BF16), dv_b], axis=1)
        g_qkv = matmul(kp["h1"], dproj, "tn", BF16, n + "_d_w_qkv", 1024, 768, 2048, mnk=(D_MODEL, 3 * D_ATTN, s),
                       o_spec=_chip_tile(D_MODEL, 768, lambda i, j, kk: (j, 0, 0)), out_shape=(N_CHIPS, D_MODEL, 768))
        dh1 = matmul(dproj, kp["w_qkv"], "nt", F32, n + "_d_h1", 512, 1024, 3 * D_ATTN, mnk=(s, D_MODEL, 3 * D_ATTN),
                     b_spec=all_chips(768))
    else:
        rider, keys = exchange.rider()
        dq_f, dk_f, dv_f, dfk, rode = causal_bwd(kp["qkv"], dattn, kp["st"], 8, "fox", n + "_fox_bwd", fq=kp["fq"], fk=kp["fk"],
                                                 rider=rider)
        dcum = jnp.pad(dfk.reshape(N_HEADS, s).T, ((0, 0), (0, LANES - N_HEADS)))
        dfl, dbias = forget_bwd(kp["fl"], bias_row, dcum, n + "_forget_bwd")
        g_bias = dbias[0, :N_HEADS]
        dproj = jnp.concatenate([dq_f, dk_f.astype(BF16), dv_f.astype(BF16), dfl.astype(BF16)], axis=1)
        g_nat = matmul(kp["h1"], dproj, "tn", BF16, n + "_d_w_qkv", 1024, 640, 2048)
        g_qkv = g_nat[:, :N_CHIPS * QKVF_COLS].reshape(D_MODEL, N_CHIPS, QKVF_COLS)
        g_qkv = jnp.transpose(jnp.pad(g_qkv, ((0, 0), (0, 0), (0, QKVF_PAD - QKVF_COLS))), (1, 0, 2))
        dh1 = matmul(dproj, kp["w_qkv"], "nt", F32, n + "_d_h1", 512, 1024, dproj.shape[1])
    exchange.landed(keys, rode)
    exchange.add([(fam_qkv, li, g_qkv)], f"l{layer}_qkv")
    dx, g_mix = rmsnorm_bwd(kp["x"], mix_gain, dh1, dmid, n + "_d_norm_mix")
    return dx, g_mix, g_ffn, g_bias


def local_step(xs, target, norm_mix, norm_ffn, norm_final, b_forget, layer_weights):
    tables = _tables_for(xs.shape[0])
    bias_pad = jnp.pad(b_forget, ((0, 0), (0, LANES - N_HEADS)))
    saved, cur, have = [], xs, {}
    h1 = rmsnorm_fwd(cur, norm_mix[0:1], "l0_norm_mix")
    for layer in range(DEPTH):
        have.update(zip((f"qkv{layer}", f"o{layer}", f"fi{layer}", f"fo{layer}"), layer_weights[layer]))
        next_gain = norm_mix[layer + 1:layer + 2] if layer + 1 < DEPTH else None
        cur, h1, keep = forward_layer(layer, cur, h1, have, norm_ffn[layer:layer + 1], next_gain, tables,
                                      bias_pad[layer // 2:layer // 2 + 1])
        saved.append(keep)
    dcur, g_final, loss_part = loss_head(cur, norm_final.reshape(1, D_MODEL), target, "loss_head")
    keeper = KeepGradients()
    g_mix, g_ffn, g_bias = [None] * DEPTH, [None] * DEPTH, [None] * (DEPTH // 2)
    for layer in reversed(range(DEPTH)):
        dcur, g_mix[layer], g_ffn[layer], g_b = backward_layer(layer, dcur, saved[layer], norm_mix[layer:layer + 1],
                                                               norm_ffn[layer:layer + 1], tables, bias_pad[layer // 2:layer // 2 + 1], keeper)
        if g_b is not None:
            g_bias[layer // 2] = g_b
    return dcur, keeper.grads, (g_mix, g_ffn, g_final, g_bias), loss_part
```

```python
import functools

import jax
import jax.numpy as jnp
from jax import lax
from jax.experimental import pallas as pl
from jax.experimental.pallas import tpu as pltpu

F32 = jnp.float32
BF16 = jnp.bfloat16
MESH = pl.DeviceIdType.MESH

D_MODEL = 1024
DEPTH = 4
HEAD_DIM = 64
N_HEADS = 16
D_ATTN = 1024
D_FF = 2816
ROPE_THETA = 500000.0
ROT_HALF = 8
RMS_EPS = 1e-5
DIL_STRIDES = (1, 4, 16)
ADAM_LR, ADAM_B1, ADAM_B2, ADAM_EPS, ADAM_WD, ADAM_STEP = 0.001, 0.9, 0.999, 1e-8, 0.01, 10

LANES = 128
BLK = 128
VMEM_LIMIT = 56 * 1024 * 1024
NEG = -1e30
N_CHIPS = 4
FLAT_COLS = 1024
FLAT_ROWS = 12800
HALF_ROWS = FLAT_ROWS // 2
SMALL_ROWS = 16


def _params(sem=None):
    return pltpu.CompilerParams(dimension_semantics=sem, vmem_limit_bytes=VMEM_LIMIT)


def _dot(a, b):
    return lax.dot_general(a, b, (((1,), (0,)), ((), ())), preferred_element_type=F32)


def _dot_nt(a, b):
    return lax.dot_general(a, b, (((1,), (1,)), ((), ())), preferred_element_type=F32)


def _dot_tn(a, b):
    return lax.dot_general(a, b, (((0,), (0,)), ((), ())), preferred_element_type=F32)


def _split3(x):
    x1 = x.astype(BF16)
    r1 = x - x1.astype(F32)
    x2 = r1.astype(BF16)
    x3 = (r1 - x2.astype(F32)).astype(BF16)
    return x1, x2, x3


def _dot_exact_lhs(x, t):
    x1, x2, x3 = _split3(x)
    return _dot(x1, t) + _dot(x2, t) + _dot(x3, t)


def _dot_exact_rhs(t, x):
    x1, x2, x3 = _split3(x)
    return _dot(t, x1) + _dot(t, x2) + _dot(t, x3)


def _iotas(shape=(BLK, LANES)):
    return lax.broadcasted_iota(jnp.int32, shape, 0), lax.broadcasted_iota(jnp.int32, shape, 1)


_DIMS = {"nn": (((1,), (0,)), ((), ())), "nt": (((1,), (1,)), ((), ())), "tn": (((0,), (0,)), ((), ()))}


def matmul(a, b, mode, out_dtype, name, tm, tn, tk, res=None, mnk=None, b_spec=None, o_spec=None, out_shape=None, norm_gain=None):
    if mnk is not None:
        m, n, k = mnk
    elif mode == "nn":
        (m, k), (k2, n) = a.shape, b.shape
    elif mode == "nt":
        (m, k), (n, k2) = a.shape, b.shape
    else:
        (k, m), (k2, n) = a.shape, b.shape
    assert m % tm == 0 and n % tn == 0 and k % tk == 0, (name, a.shape, b.shape)
    nk = k // tk
    a_spec = pl.BlockSpec((tk, tm), lambda i, j, kk: (kk, i)) if mode == "tn" else pl.BlockSpec((tm, tk), lambda i, j, kk: (i, kk))
    if b_spec is None:
        b_spec = pl.BlockSpec((tn, tk), lambda i, j, kk: (j, kk)) if mode == "nt" else pl.BlockSpec((tk, tn), lambda i, j, kk: (kk, j))
    r_spec = pl.BlockSpec((tm, tn), lambda i, j, kk: (i, j))
    if o_spec is None:
        o_spec = r_spec
    dims = _DIMS[mode]
    has_res, has_norm = res is not None, norm_gain is not None
    assert not has_norm or (tn == n and nk == 1)
    n_in = 2 + int(has_res) + int(has_norm)

    def body(*refs):
        a_ref, b_ref = refs[0], refs[1]
        r_ref = refs[2] if has_res else None
        o_ref = refs[n_in]

        def finish(v):
            if has_res:
                v = v + r_ref[...]
            o_ref[...] = v.astype(out_dtype)
            if has_norm:
                rstd = lax.rsqrt(jnp.mean(v * v, axis=-1, keepdims=True) + RMS_EPS)
                refs[n_in + 1][...] = (v * rstd * refs[n_in - 1][...]).astype(BF16)

        bv = b_ref[...]
        if bv.ndim == 3:
            bv = jnp.concatenate([bv[j] for j in range(bv.shape[0])], axis=1)
        p = lax.dot_general(a_ref[...].astype(BF16), bv.astype(BF16), dims, preferred_element_type=F32)
        if nk == 1:
            finish(p)
        else:
            acc = refs[-1]
            kk = pl.program_id(2)

            @pl.when(kk == 0)
            def _():
                acc[...] = p

            @pl.when(kk > 0)
            def _():
                acc[...] += p

            @pl.when(kk == nk - 1)
            def _():
                finish(acc[...])

    ops = [a, b] + ([res] if has_res else []) + ([norm_gain] if has_norm else [])
    specs = [a_spec, b_spec] + ([r_spec] if has_res else []) + ([pl.BlockSpec((1, tn), lambda i, j, kk: (0, j))] if has_norm else [])
    out_shape = jax.ShapeDtypeStruct((m, n) if out_shape is None else out_shape, out_dtype)
    return pl.pallas_call(
        body, name=name, out_shape=(out_shape, jax.ShapeDtypeStruct((m, n), BF16)) if has_norm else out_shape,
        grid=(m // tm, n // tn, nk), in_specs=specs, out_specs=(o_spec, r_spec) if has_norm else o_spec,
        scratch_shapes=[pltpu.VMEM((tm, tn), F32)] if nk > 1 else [],
        compiler_params=_params(("parallel", "parallel", "arbitrary")),
    )(*ops)


ROWS = 256


def _row_spec(cols, rows=ROWS):
    return pl.BlockSpec((rows, cols), lambda i: (i, 0))


def _fix_spec(r, cols):
    return pl.BlockSpec((r, cols), lambda i: (0, 0))


def rmsnorm_fwd(x, g, name):
    s, d = x.shape

    def body(x_ref, g_ref, h_ref):
        xv = x_ref[...]
        rstd = lax.rsqrt(jnp.mean(xv * xv, axis=-1, keepdims=True) + RMS_EPS)
        h_ref[...] = (xv * rstd * g_ref[...]).astype(BF16)

    return pl.pallas_call(
        body, name=name, out_shape=jax.ShapeDtypeStruct((s, d), BF16), grid=(s // ROWS,),
        in_specs=[_row_spec(d), _fix_spec(1, d)], out_specs=_row_spec(d), compiler_params=_params(("parallel",)),
    )(x, g)


def _rms_bwd_math(xv, gv, dh):
    rstd = lax.rsqrt(jnp.mean(xv * xv, axis=-1, keepdims=True) + RMS_EPS)
    xhat = xv * rstd
    u = dh * gv
    dx = rstd * (u - xhat * jnp.mean(u * xhat, axis=-1, keepdims=True))
    return dx, dh * xhat


def dh_norm_bwd(dy, w, x, g, dres, name, tm):
    s, k = dy.shape
    d = x.shape[1]

    def body(dy_ref, w_ref, x_ref, g_ref, dres_ref, dx_ref, dg_ref):
        wv = w_ref[...]
        if wv.ndim == 3:
            wv = jnp.concatenate([wv[j] for j in range(wv.shape[0])], axis=1)
        dx, dgt = _rms_bwd_math(x_ref[...], g_ref[...], _dot_nt(dy_ref[...], wv))
        dx_ref[...] = dres_ref[...] + dx
        part = jnp.sum(dgt, axis=0, keepdims=True)

        @pl.when(pl.program_id(0) == 0)
        def _():
            dg_ref[...] = part

        @pl.when(pl.program_id(0) > 0)
        def _():
            dg_ref[...] += part

    w_spec = pl.BlockSpec(w.shape, lambda i: (0,) * w.ndim)
    return pl.pallas_call(
        body, name=name, out_shape=(jax.ShapeDtypeStruct((s, d), F32), jax.ShapeDtypeStruct((1, d), F32)), grid=(s // tm,),
        in_specs=[pl.BlockSpec((tm, k), lambda i: (i, 0)), w_spec, _row_spec(d, tm), _fix_spec(1, d), _row_spec(d, tm)],
        out_specs=(_row_spec(d, tm), _fix_spec(1, d)), compiler_params=_params(("arbitrary",)),
    )(dy, w, x, g, dres)


def rmsnorm_bwd(x, g, dh, dres, name):
    s, d = x.shape

    def body(x_ref, g_ref, dh_ref, dres_ref, dx_ref, dg_ref):
        dx, dgt = _rms_bwd_math(x_ref[...], g_ref[...], dh_ref[...])
        dx_ref[...] = dres_ref[...] + dx
        part = jnp.sum(dgt, axis=0, keepdims=True)

        @pl.when(pl.program_id(0) == 0)
        def _():
            dg_ref[...] = part

        @pl.when(pl.program_id(0) > 0)
        def _():
            dg_ref[...] += part

    return pl.pallas_call(
        body, name=name, out_shape=(jax.ShapeDtypeStruct((s, d), F32), jax.ShapeDtypeStruct((1, d), F32)),
        grid=(s // ROWS,), in_specs=[_row_spec(d), _fix_spec(1, d), _row_spec(d), _row_spec(d)],
        out_specs=(_row_spec(d), _fix_spec(1, d)), compiler_params=_params(("arbitrary",)),
    )(x, g, dh, dres)


def loss_head(x, g, target, name):
    s, d = x.shape

    def body(x_ref, g_ref, t_ref, dx_ref, dg_ref, loss_ref):
        xv, gv = x_ref[...], g_ref[...]
        rstd = lax.rsqrt(jnp.mean(xv * xv, axis=-1, keepdims=True) + RMS_EPS)
        err = xv * rstd * gv - t_ref[...]
        dx, dgt = _rms_bwd_math(xv, gv, err * (1.0 / d))
        dx_ref[...] = dx
        part = jnp.sum(dgt, axis=0, keepdims=True)
        lpart = jnp.full((1, LANES), 0.5 / d, F32) * jnp.sum(err * err)

        @pl.when(pl.program_id(0) == 0)
        def _():
            dg_ref[...] = part
            loss_ref[...] = lpart

        @pl.when(pl.program_id(0) > 0)
        def _():
            dg_ref[...] += part
            loss_ref[...] += lpart

    return pl.pallas_call(
        body, name=name,
        out_shape=(jax.ShapeDtypeStruct((s, d), F32), jax.ShapeDtypeStruct((1, d), F32), jax.ShapeDtypeStruct((1, LANES), F32)),
        grid=(s // ROWS,), in_specs=[_row_spec(d), _fix_spec(1, d), _row_spec(d)],
        out_specs=(_row_spec(d), _fix_spec(1, d), _fix_spec(1, LANES)), compiler_params=_params(("arbitrary",)),
    )(x, g, target)


def ffn_in_swiglu(h, w_in, name, tm=1024):
    s, d = h.shape
    cols = w_in.shape[2]

    def body(h_ref, wg_ref, wu_ref, g_ref, u_ref, a_ref):
        hv = h_ref[...]
        gv, uv = _dot(hv, wg_ref[...]), _dot(hv, wu_ref[...])
        g_ref[...] = gv.astype(BF16)
        u_ref[...] = uv.astype(BF16)
        a_ref[...] = (gv * (1.0 / (1.0 + jnp.exp(-gv))) * uv).astype(BF16)

    tile = pl.BlockSpec((tm, cols), lambda i, j: (i, j))
    out = jax.ShapeDtypeStruct((s, 2 * cols), BF16)
    return pl.pallas_call(
        body, name=name, out_shape=(out, out, out), grid=(s // tm, 2),
        in_specs=[pl.BlockSpec((tm, d), lambda i, j: (i, 0)), pl.BlockSpec((None, d, cols), lambda i, j: (j, 0, 0)),
                  pl.BlockSpec((None, d, cols), lambda i, j: (j + 2, 0, 0))],
        out_specs=(tile, tile, tile), compiler_params=_params(("parallel", "parallel")),
    )(h, w_in, w_in)


def swiglu_bwd(gate, up, dact, name):
    s, f = gate.shape

    def body(g_ref, u_ref, da_ref, o_ref):
        gv, uv, da = g_ref[...].astype(F32), u_ref[...].astype(F32), da_ref[...].astype(F32)
        sg = 1.0 / (1.0 + jnp.exp(-gv))
        o_ref[:, :f] = (da * uv * sg * (1.0 + gv * (1.0 - sg))).astype(BF16)
        o_ref[:, f:] = (da * gv * sg).astype(BF16)

    return pl.pallas_call(
        body, name=name, out_shape=jax.ShapeDtypeStruct((s, 2 * f), BF16), grid=(s // ROWS,),
        in_specs=[_row_spec(f)] * 3, out_specs=_row_spec(2 * f), compiler_params=_params(("parallel",)),
    )(gate, up, dact)


Q_OFF, K_OFF, V_OFF = 0, 8, 16


KB = 512
BQ = 512
SUB = KB // BLK


def _softplus_parts(z):
    sp = jnp.log(1.0 + jnp.exp(-jnp.abs(z)))
    ls = jnp.minimum(z, 0.0) - sp
    return ls, ls - z


def _wide(t):
    return jnp.concatenate([t] * SUB, axis=1)


def _chunk_dots(x, tri):
    terms = []
    for u in range(SUB):
        xu = x[:, u * BLK:(u + 1) * BLK]
        hi = xu.astype(BF16)
        terms += [hi, (xu - hi.astype(F32)).astype(BF16)]
    r = _dot(jnp.concatenate(terms, axis=0), tri)
    rows = x.shape[0]
    piece = lambda n: r[n * rows:(n + 1) * rows]
    return [piece(2 * u) + piece(2 * u + 1) for u in range(SUB)]


def _block_suffix_sums(x, suffix, c):
    loc = _chunk_dots(x, suffix)
    out = [None] * SUB
    for u in reversed(range(SUB)):
        out[u] = loc[u] + c
        c = c + jnp.sum(x[:, u * BLK:(u + 1) * BLK], axis=1, keepdims=True)
    return jnp.concatenate(out, axis=1), c


def _block_prefix_sums(x, tri, c):
    loc = _chunk_dots(x, tri)
    out = []
    for u in range(SUB):
        out.append(loc[u] + c)
        c = c + jnp.sum(x[:, u * BLK:(u + 1) * BLK], axis=1, keepdims=True)
    return jnp.concatenate(out, axis=1), c


def causal_fwd(qkv, npairs, mode, name, fq=None, fk=None, rider=None):
    s = qkv.shape[0]
    nq = s // BQ
    fox = mode == "fox"

    def body(*refs):
        if fox:
            q_ref, k_ref, v_ref, fq_ref, fk_ref, o_ref, st_ref = refs
        else:
            q_ref, k_ref, v_ref, o_ref, st_ref = refs
        i = pl.program_id(1)
        nkb = (i * BQ + BQ - 1) // KB + 1
        row, lane = _iotas((BQ, KB))
        row_s, lane_s = _iotas()
        _, lane_q = _iotas((BQ, LANES))
        nfull = (i * BQ) // KB
        qpos = i * BQ + row
        qf = q_ref[...].astype(F32) * 0.125
        hms = (lane_q < HEAD_DIM, lane_q >= HEAD_DIM)
        qas = [jnp.where(hm, qf, 0.0).astype(BF16) for hm in hms]
        suffix = jnp.where(row_s > lane_s, 1.0, 0.0).astype(BF16)
        zero = jnp.zeros((BQ, LANES), F32)
        col0 = jnp.zeros((BQ, 1), F32)

        def kv(j):
            r0 = pl.multiple_of(j * KB, KB)
            return r0, k_ref[pl.ds(r0, KB), :], v_ref[pl.ds(r0, KB), :]

        if fox:
            fqs = [_wide(fq_ref[a]) for a in range(2)]

            def step(j, carry, masked):
                r0, kb, vb = kv(j)
                new = []
                for a in range(2):
                    acc, mx, l = carry[3 * a:3 * a + 3]
                    z = _dot_nt(qas[a], kb) + fqs[a] - fk_ref[a:a + 1, pl.ds(r0, KB)]
                    if masked:
                        z = jnp.where(r0 + lane <= qpos, z, NEG)
                    mnew = jnp.maximum(mx, jnp.max(z, axis=1, keepdims=True))
                    p = jnp.exp(z - mnew)
                    alpha = jnp.exp(mx - mnew)
                    new += [alpha * acc + _dot(p.astype(BF16), vb), mnew, alpha * l + jnp.sum(p, axis=1, keepdims=True)]
                return tuple(new)

            neg = jnp.full((BQ, 1), NEG, F32)
            res = lax.fori_loop(0, nfull, functools.partial(step, masked=False), (zero, neg, col0, zero, neg, col0))
            res = lax.fori_loop(nfull, nkb, functools.partial(step, masked=True), res)
            outs = [res[3 * a] / res[3 * a + 2] for a in range(2)]
            stats = [res[3 * a + 1] + jnp.log(res[3 * a + 2]) for a in range(2)]
        else:
            def step(j, carry, masked):
                r0, kb, vb = kv(j)
                strict = r0 + lane < qpos
                new = []
                for a in range(2):
                    acc, c = carry[2 * a:2 * a + 2]
                    ls, lm = _softplus_parts(_dot_nt(qas[a], kb))
                    if masked:
                        lm = jnp.where(strict, lm, 0.0)
                    between, c = _block_suffix_sums(lm, suffix, c)
                    aw = jnp.exp(ls + between)
                    if masked:
                        aw = jnp.where(strict, aw, 0.0)
                    new += [acc + _dot(aw.astype(BF16), vb), c]
                return tuple(new)

            res = lax.fori_loop(0, nkb - nfull, lambda jj, c: step(nkb - 1 - jj, c, True), (zero, col0, zero, col0))
            res = lax.fori_loop(0, nfull, lambda jj, c: step(nfull - 1 - jj, c, False), res)
            outs, stats = [res[0], res[2]], [res[1], res[3]]
        o_ref[...] = jnp.where(hms[0], outs[0], outs[1])
        for a in range(2):
            st_ref[a] = jnp.broadcast_to(stats[a], (BQ, LANES))

    col = lambda off: (lambda p, i: (0, off + p))
    in_specs = [pl.BlockSpec((BQ, LANES), lambda p, i: (i, Q_OFF + p)),
                pl.BlockSpec((s, LANES), col(K_OFF)), pl.BlockSpec((s, LANES), col(V_OFF))]
    ops = [qkv, qkv, qkv]
    if fox:
        in_specs += [pl.BlockSpec((2, BQ, LANES), lambda p, i: (p, i, 0)), pl.BlockSpec((None, 2, s), lambda p, i: (p, 0, 0))]
        ops += [fq, fk]
    (o, stat), rode = call_with_rider(
        body, name, rider, ops, in_specs,
        [jax.ShapeDtypeStruct((s, npairs * LANES), F32), jax.ShapeDtypeStruct((2 * npairs, s, LANES), F32)],
        [pl.BlockSpec((BQ, LANES), lambda p, i: (i, p)), pl.BlockSpec((2, BQ, LANES), lambda p, i: (p, i, 0))], [], (npairs, nq))
    return o, stat, rode


def causal_bwd(qkv, do, stat, npairs, mode, name, fq=None, fk=None, rider=None):
    s = qkv.shape[0]
    nq = s // BQ
    fox = mode == "fox"

    def body(*refs):
        if fox:
            q_ref, k_ref, v_ref, do_ref, st_ref, fq_ref, fk_ref, dq_ref, dk_ref, dv_ref, df_ref, p_s, dp_s = refs
        else:
            q_ref, k_ref, v_ref, do_ref, st_ref, dq_ref, dk_ref, dv_ref = refs
        i = pl.program_id(1)

        @pl.when(i == 0)
        def _():
            dk_ref[...] = jnp.zeros_like(dk_ref)
            dv_ref[...] = jnp.zeros_like(dv_ref)
            if fox:
                df_ref[...] = jnp.zeros_like(df_ref)

        nkb = (i * BQ + BQ - 1) // KB + 1
        nfull = (i * BQ) // KB
        row, lane = _iotas((BQ, KB))
        row_s, lane_s = _iotas()
        _, lane_q = _iotas((BQ, LANES))
        qpos = i * BQ + row
        qf = q_ref[...].astype(F32) * 0.125
        dov = do_ref[...]
        hms = (lane_q < HEAD_DIM, lane_q >= HEAD_DIM)
        qas = [jnp.where(hm, qf, 0.0).astype(BF16) for hm in hms]
        doas = [jnp.where(hm, dov, 0.0).astype(BF16) for hm in hms]
        stas = [_wide(st_ref[a]) for a in range(2)]
        zero = jnp.zeros((BQ, LANES), F32)
        col0 = jnp.zeros((BQ, 1), F32)

        def kv(j):
            r0 = pl.multiple_of(j * KB, KB)
            return r0, k_ref[pl.ds(r0, KB), :], v_ref[pl.ds(r0, KB), :]

        if fox:
            fqs = [_wide(fq_ref[a]) for a in range(2)]

            def probs(j, deltas, masked):
                r0, kb, vb = kv(j)
                new = []
                for a in range(2):
                    z = _dot_nt(qas[a], kb) + fqs[a] - fk_ref[a:a + 1, pl.ds(r0, KB)]
                    p = jnp.exp(z - stas[a])
                    if masked:
                        p = jnp.where(r0 + lane <= qpos, p, 0.0)
                    dp = _dot_nt(doas[a], vb)
                    p_s[a, j] = p
                    dp_s[a, j] = dp
                    new.append(deltas[a] + jnp.sum(p * dp, axis=1, keepdims=True))
                return tuple(new)

            deltas = lax.fori_loop(0, nfull, functools.partial(probs, masked=False), (col0, col0))
            deltas = lax.fori_loop(nfull, nkb, functools.partial(probs, masked=True), deltas)

            def step(j, dqs):
                r0, kb, _ = kv(j)
                new = []
                dk = jnp.zeros((KB, LANES), F32)
                dv = jnp.zeros((KB, LANES), F32)
                for a in range(2):
                    p = p_s[a, j]
                    ds = p * (dp_s[a, j] - deltas[a])
                    dsb = ds.astype(BF16)
                    dk += _dot_tn(dsb, qas[a])
                    dv += _dot_tn(p.astype(BF16), doas[a])
                    df_ref[a:a + 1, pl.ds(r0, KB)] -= jnp.sum(ds, axis=0, keepdims=True)
                    new.append(dqs[a] + _dot(dsb, kb))
                dk_ref[pl.ds(r0, KB), :] += dk
                dv_ref[pl.ds(r0, KB), :] += dv
                return tuple(new)

            dqs = lax.fori_loop(0, nkb, step, (zero, zero))
        else:
            incl = jnp.where(row_s <= lane_s, 1.0, 0.0).astype(BF16)
            excl = jnp.where(row_s < lane_s, 1.0, 0.0).astype(BF16)

            def step(j, carry, masked):
                r0, kb, vb = kv(j)
                strict = r0 + lane < qpos
                new = []
                dk = jnp.zeros((KB, LANES), F32)
                dv = jnp.zeros((KB, LANES), F32)
                for a in range(2):
                    dq, cm, cg = carry[3 * a:3 * a + 3]
                    ls, lm = _softplus_parts(_dot_nt(qas[a], kb))
                    if masked:
                        lm = jnp.where(strict, lm, 0.0)
                    beta = jnp.exp(ls)
                    upto, cm = _block_prefix_sums(lm, incl, cm)
                    aw = jnp.exp(ls + stas[a] - upto)
                    if masked:
                        aw = jnp.where(strict, aw, 0.0)
                    g = aw * _dot_nt(doas[a], vb)
                    pre, cg = _block_prefix_sums(g, excl, cg)
                    dz = g * (1.0 - beta) - pre * beta
                    if masked:
                        dz = jnp.where(strict, dz, 0.0)
                    dzb = dz.astype(BF16)
                    dk += _dot_tn(dzb, qas[a])
                    dv += _dot_tn(aw.astype(BF16), doas[a])
                    new += [dq + _dot(dzb, kb), cm, cg]
                dk_ref[pl.ds(r0, KB), :] += dk
                dv_ref[pl.ds(r0, KB), :] += dv
                return tuple(new)

            res = lax.fori_loop(0, nfull, functools.partial(step, masked=False), (zero, col0, col0, zero, col0, col0))
            res = lax.fori_loop(nfull, nkb, functools.partial(step, masked=True), res)
            dqs = (res[0], res[3])
        dq_ref[...] = (jnp.where(hms[0], dqs[0], dqs[1]) * 0.125).astype(BF16)

    col = lambda off: (lambda p, i: (0, off + p))
    blk = pl.BlockSpec((BQ, LANES), lambda p, i: (i, p))
    acc = pl.BlockSpec((s, LANES), lambda p, i: (0, p))
    st_spec = pl.BlockSpec((2, BQ, LANES), lambda p, i: (p, i, 0))
    in_specs = [pl.BlockSpec((BQ, LANES), lambda p, i: (i, Q_OFF + p)), pl.BlockSpec((s, LANES), col(K_OFF)),
                pl.BlockSpec((s, LANES), col(V_OFF)), blk, st_spec]
    ops = [qkv, qkv, qkv, do, stat]
    w = npairs * LANES
    out_shape = [jax.ShapeDtypeStruct((s, w), BF16), jax.ShapeDtypeStruct((s, w), F32), jax.ShapeDtypeStruct((s, w), F32)]
    out_specs = [blk, acc, acc]
    scratch = []
    if fox:
        fk_spec = pl.BlockSpec((None, 2, s), lambda p, i: (p, 0, 0))
        in_specs += [st_spec, fk_spec]
        ops += [fq, fk]
        out_shape.append(jax.ShapeDtypeStruct((npairs, 2, s), F32))
        out_specs.append(fk_spec)
        scratch = [pltpu.VMEM((2, s // KB, BQ, KB), F32)] * 2
    outs, rode = call_with_rider(body, name, rider, ops, in_specs, out_shape, out_specs, scratch, (npairs, nq))
    return (*outs, rode)


def forget_fwd(fl, bias, name):
    s = fl.shape[0]

    def body(fl_ref, b_ref, f_ref):
        row, lane = _iotas()
        lower = jnp.where(lane <= row, 1.0, 0.0).astype(BF16)

        def step(n, carry):
            r0 = pl.multiple_of(n * BLK, BLK)
            ls, _ = _softplus_parts(fl_ref[pl.ds(r0, BLK), :] + b_ref[...])
            blk = _dot_exact_rhs(lower, ls) + carry
            f_ref[pl.ds(r0, BLK), :] = blk
            return blk[BLK - 1:BLK, :]

        lax.fori_loop(0, s // BLK, step, jnp.zeros((1, LANES), F32))

    return pl.pallas_call(
        body, name=name, out_shape=jax.ShapeDtypeStruct((s, LANES), F32),
        in_specs=[pl.BlockSpec(memory_space=pltpu.VMEM)] * 2, out_specs=pl.BlockSpec(memory_space=pltpu.VMEM),
        compiler_params=_params(),
    )(fl, bias)


def forget_bwd(fl, bias, df, name):
    s = fl.shape[0]
    nb = s // BLK

    def body(fl_ref, b_ref, df_ref, o_ref, db_ref):
        row, lane = _iotas()
        upper = jnp.where(lane >= row, 1.0, 0.0).astype(BF16)

        def step(nn, carry):
            tail, db = carry
            r0 = pl.multiple_of((nb - 1 - nn) * BLK, BLK)
            dls = _dot_exact_rhs(upper, df_ref[pl.ds(r0, BLK), :]) + tail
            xv = fl_ref[pl.ds(r0, BLK), :] + b_ref[...]
            dfl = dls * (1.0 / (1.0 + jnp.exp(xv)))
            o_ref[pl.ds(r0, BLK), :] = dfl
            return dls[0:1, :], db + jnp.sum(dfl, axis=0, keepdims=True)

        _, db = lax.fori_loop(0, nb, step, (jnp.zeros((1, LANES), F32), jnp.zeros((1, LANES), F32)))
        db_ref[...] = db

    return pl.pallas_call(
        body, name=name, out_shape=(jax.ShapeDtypeStruct((s, LANES), F32), jax.ShapeDtypeStruct((1, LANES), F32)),
        in_specs=[pl.BlockSpec(memory_space=pltpu.VMEM)] * 3,
        out_specs=(pl.BlockSpec(memory_space=pltpu.VMEM), pl.BlockSpec(memory_space=pltpu.VMEM)),
        compiler_params=_params(),
    )(fl, bias, df)


def _rot_tables(s):
    inv = ROPE_THETA ** (-jnp.arange(ROT_HALF, dtype=F32) * 2.0 / (2 * ROT_HALF))
    ang = jnp.arange(s, dtype=F32)[:, None] * inv[None, :]
    cos, sin = jnp.cos(ang), jnp.sin(ang)
    z8 = jnp.zeros((s, ROT_HALF), F32)
    rest = HEAD_DIM - 2 * ROT_HALF
    zr, onr = jnp.zeros((s, rest), F32), jnp.ones((s, rest), F32)
    tile = lambda t: jnp.tile(t, (1, 2))
    return tile(jnp.concatenate([cos, cos, onr], 1)), tile(jnp.concatenate([-sin, z8, zr], 1)), tile(jnp.concatenate([z8, sin, zr], 1))


def rotary_prep(qkv, tables, name):
    s = qkv.shape[0]
    w = 4 * LANES

    def body(q_ref, k_ref, v_ref, c_ref, s1_ref, s2_ref, qo_ref, ko_ref, vo_ref):
        c, s1, s2 = c_ref[...], s1_ref[...], s2_ref[...]

        def rot(xv):
            return xv * c + pltpu.roll(xv, LANES - ROT_HALF, 1) * s1 + pltpu.roll(xv, ROT_HALF, 1) * s2

        qo_ref[...] = rot(q_ref[...].astype(F32)) * 0.125
        ko_ref[...] = rot(k_ref[...].astype(F32))
        vo_ref[...] = v_ref[...].astype(F32)

    cb = lambda off: pl.BlockSpec((ROWS, LANES), lambda i, j: (i, off + j))
    tb = pl.BlockSpec((ROWS, LANES), lambda i, j: (i, 0))
    out = jax.ShapeDtypeStruct((s, w), F32)
    return pl.pallas_call(
        body, name=name, out_shape=(out, out, out), grid=(s // ROWS, 4),
        in_specs=[cb(Q_OFF + 4), cb(K_OFF + 4), cb(V_OFF + 4), tb, tb, tb], out_specs=(cb(0), cb(0), cb(0)),
        compiler_params=_params(("parallel", "parallel")),
    )(qkv, qkv, qkv, *tables)


def rotary_bwd(dq, dk, dv, tables, name):
    s, w = dq.shape

    def body(dq_ref, dk_ref, dv_ref, c_ref, s1_ref, s2_ref, qo_ref, ko_ref, vo_ref):
        c, s1, s2 = c_ref[...], s1_ref[...], s2_ref[...]

        def rot_t(dy):
            return dy * c + pltpu.roll(dy * s1, ROT_HALF, 1) + pltpu.roll(dy * s2, LANES - ROT_HALF, 1)

        qo_ref[...] = (rot_t(dq_ref[...]) * 0.125).astype(BF16)
        ko_ref[...] = rot_t(dk_ref[...]).astype(BF16)
        vo_ref[...] = dv_ref[...].astype(BF16)

    cb = pl.BlockSpec((ROWS, LANES), lambda i, j: (i, j))
    tb = pl.BlockSpec((ROWS, LANES), lambda i, j: (i, 0))
    out = jax.ShapeDtypeStruct((s, w), BF16)
    return pl.pallas_call(
        body, name=name, out_shape=(out, out, out), grid=(s // ROWS, w // LANES),
        in_specs=[cb, cb, cb, tb, tb, tb], out_specs=(cb, cb, cb), compiler_params=_params(("parallel", "parallel")),
    )(dq, dk, dv, *tables)


def _deinterleave(dst, src_ref, stride, s, dtype):
    length = s // stride
    for r in range(stride):
        if stride == 1:
            dst[...] = src_ref[...].astype(dtype)
        else:
            dst[r * length:(r + 1) * length, :] = src_ref[pl.ds(r, length, stride=stride), :].astype(dtype)


def _band_masks(row, lane, first):
    return lane <= row, lane >= row + jnp.where(first, BLK, 0)


def dilated_fwd(qd, kd, vd, name, rider=None):
    s, w = qd.shape
    npairs = w // LANES
    nblk = s // BLK

    def body(q_ref, k_ref, v_ref, o_ref, lse_ref, qs, ks, vs, od, ld, on, ln):
        row, lane = _iotas()
        for pi, stride in enumerate(DIL_STRIDES):
            per = (s // stride) // BLK
            _deinterleave(qs, q_ref, stride, s, BF16)
            _deinterleave(ks, k_ref, stride, s, BF16)
            _deinterleave(vs, v_ref, stride, s, BF16)

            def block(b, carry):
                r0 = pl.multiple_of(b * BLK, BLK)
                rp = pl.multiple_of(jnp.maximum(b - 1, 0) * BLK, BLK)
                mc, mp = _band_masks(row, lane, b % per == 0)
                q = qs[pl.ds(r0, BLK), :]
                kc, kp, vc, vp = ks[pl.ds(r0, BLK), :], ks[pl.ds(rp, BLK), :], vs[pl.ds(r0, BLK), :], vs[pl.ds(rp, BLK), :]
                out = jnp.zeros((BLK, LANES), F32)
                lse = jnp.zeros((BLK, LANES), F32)
                for a in range(2):
                    hm = (lane < HEAD_DIM) if a == 0 else (lane >= HEAD_DIM)
                    qa = jnp.where(hm, q.astype(F32), 0.0).astype(BF16)
                    sc = jnp.where(mc, _dot_nt(qa, kc), NEG)
                    sp = jnp.where(mp, _dot_nt(qa, kp), NEG)
                    mx = jnp.maximum(jnp.max(sc, axis=1, keepdims=True), jnp.max(sp, axis=1, keepdims=True))
                    pc, pp = jnp.exp(sc - mx), jnp.exp(sp - mx)
                    l = jnp.sum(pc, axis=1, keepdims=True) + jnp.sum(pp, axis=1, keepdims=True)
                    oa = (_dot(pc.astype(BF16), vc) + _dot(pp.astype(BF16), vp)) / l
                    out = jnp.where(hm, oa, out)
                    lse = jnp.where(hm, mx + jnp.log(l), lse)
                od[pl.ds(r0, BLK), :] = out
                ld[pl.ds(r0, BLK), :] = lse
                return carry

            lax.fori_loop(0, nblk, block, 0, unroll=2)
            length = s // stride
            for r in range(stride):
                if stride == 1:
                    on[pi] = od[...]
                    ln[pi] = ld[...]
                else:
                    on[pi, pl.ds(r, length, stride=stride), :] = od[r * length:(r + 1) * length, :]
                    ln[pi, pl.ds(r, length, stride=stride), :] = ld[r * length:(r + 1) * length, :]

        def merge(n, carry):
            r0 = pl.multiple_of(n * BLK, BLK)
            ls = [ln[pi, pl.ds(r0, BLK), :] for pi in range(3)]
            mx = jnp.maximum(jnp.maximum(ls[0], ls[1]), ls[2])
            ws = [jnp.exp(lv - mx) for lv in ls]
            den = ws[0] + ws[1] + ws[2]
            num = ws[0] * on[0, pl.ds(r0, BLK), :] + ws[1] * on[1, pl.ds(r0, BLK), :] + ws[2] * on[2, pl.ds(r0, BLK), :]
            o_ref[pl.ds(r0, BLK), :] = num / den
            lse_ref[pl.ds(r0, BLK), :] = mx + jnp.log(den)
            return carry

        lax.fori_loop(0, nblk, merge, 0, unroll=2)

    colspec = pl.BlockSpec((s, LANES), lambda p: (0, p))
    out = jax.ShapeDtypeStruct((s, w), F32)
    scratch = [pltpu.VMEM((s, LANES), BF16)] * 3 + [pltpu.VMEM((s, LANES), F32)] * 2 + [pltpu.VMEM((3, s, LANES), F32)] * 2
    (o, lse), rode = call_with_rider(body, name, rider, [qd, kd, vd], [colspec] * 3, [out, out], [colspec, colspec], scratch, (npairs,))
    return o, lse, rode


def dilated_bwd(qd, kd, vd, do, out, lse, do_off, name, rider=None):
    s, w = qd.shape
    npairs = w // LANES
    nblk = s // BLK

    def body(q_ref, k_ref, v_ref, do_ref, out_ref, lse_ref, dq_ref, dk_ref, dv_ref, qs, ks, vs, dos, dls, lss, dqd, dkd, dvd, dln):
        row, lane = _iotas()
        same_head = jnp.where((row < HEAD_DIM) == (lane < HEAD_DIM), 1.0, 0.0).astype(BF16)

        def delta_blk(n, carry):
            r0 = pl.multiple_of(n * BLK, BLK)
            dln[pl.ds(r0, BLK), :] = _dot_exact_lhs(do_ref[pl.ds(r0, BLK), :] * out_ref[pl.ds(r0, BLK), :], same_head)
            return carry

        lax.fori_loop(0, nblk, delta_blk, 0, unroll=2)
        for pi, stride in enumerate(DIL_STRIDES):
            per = (s // stride) // BLK
            _deinterleave(qs, q_ref, stride, s, BF16)
            _deinterleave(ks, k_ref, stride, s, BF16)
            _deinterleave(vs, v_ref, stride, s, BF16)
            _deinterleave(dos, do_ref, stride, s, BF16)
            _deinterleave(dls, dln, stride, s, F32)
            _deinterleave(lss, lse_ref, stride, s, F32)

            def block(b, carry):
                r0 = pl.multiple_of(b * BLK, BLK)
                rp = pl.multiple_of(jnp.maximum(b - 1, 0) * BLK, BLK)
                first = b % per == 0
                mc, mp = _band_masks(row, lane, first)
                q, dov = qs[pl.ds(r0, BLK), :], dos[pl.ds(r0, BLK), :]
                kc, kp, vc, vp = ks[pl.ds(r0, BLK), :], ks[pl.ds(rp, BLK), :], vs[pl.ds(r0, BLK), :], vs[pl.ds(rp, BLK), :]
                lse_t, dl_t = lss[pl.ds(r0, BLK), :], dls[pl.ds(r0, BLK), :]
                dq = jnp.zeros((BLK, LANES), F32)
                dkc = jnp.zeros((BLK, LANES), F32)
                dkp = jnp.zeros((BLK, LANES), F32)
                dvc = jnp.zeros((BLK, LANES), F32)
                dvp = jnp.zeros((BLK, LANES), F32)
                for a in range(2):
                    hm = (lane < HEAD_DIM) if a == 0 else (lane >= HEAD_DIM)
                    pick = lane == a * HEAD_DIM
                    qa = jnp.where(hm, q.astype(F32), 0.0).astype(BF16)
                    doa = jnp.where(hm, dov.astype(F32), 0.0).astype(BF16)
                    lse_a = jnp.sum(jnp.where(pick, lse_t, 0.0), axis=1, keepdims=True)
                    dl_a = jnp.sum(jnp.where(pick, dl_t, 0.0), axis=1, keepdims=True)
                    pc = jnp.where(mc, jnp.exp(_dot_nt(qa, kc) - lse_a), 0.0)
                    pp = jnp.where(mp, jnp.exp(_dot_nt(qa, kp) - lse_a), 0.0)
                    dsc = (pc * (_dot_nt(doa, vc) - dl_a)).astype(BF16)
                    dsp = (pp * (_dot_nt(doa, vp) - dl_a)).astype(BF16)
                    dq = jnp.where(hm, _dot(dsc, kc) + _dot(dsp, kp), dq)
                    dkc += _dot_tn(dsc, qa)
                    dkp += _dot_tn(dsp, qa)
                    dvc += _dot_tn(pc.astype(BF16), doa)
                    dvp += _dot_tn(pp.astype(BF16), doa)
                dqd[pl.ds(r0, BLK), :] = dq
                dkd[pl.ds(r0, BLK), :] = dkc
                dvd[pl.ds(r0, BLK), :] = dvc

                @pl.when(jnp.logical_not(first))
                def _():
                    dkd[pl.ds(rp, BLK), :] += dkp
                    dvd[pl.ds(rp, BLK), :] += dvp

                return carry

            lax.fori_loop(0, nblk, block, 0, unroll=2)
            length = s // stride
            for dst, src in ((dq_ref, dqd), (dk_ref, dkd), (dv_ref, dvd)):
                for r in range(stride):
                    if stride == 1:
                        dst[...] = src[...]
                    else:
                        dst[pl.ds(r, length, stride=stride), :] += src[r * length:(r + 1) * length, :]

    colspec = pl.BlockSpec((s, LANES), lambda p: (0, p))
    do_spec = pl.BlockSpec((s, LANES), lambda p: (0, do_off + p))
    o3 = jax.ShapeDtypeStruct((s, w), F32)
    scratch = [pltpu.VMEM((s, LANES), BF16)] * 4 + [pltpu.VMEM((s, LANES), F32)] * 6
    outs, rode = call_with_rider(body, name, rider, [qd, kd, vd, do, out, lse], [colspec, colspec, colspec, do_spec, colspec, colspec],
                                 [o3, o3, o3], [colspec, colspec, colspec], scratch, (npairs,))
    return (*outs, rode)


def adamw(w, g, m, v, name):
    rows, cols = w.shape
    rb = min(rows, ROWS)
    c1 = 1.0 - ADAM_B1 ** ADAM_STEP
    c2 = 1.0 - ADAM_B2 ** ADAM_STEP

    def body(w_ref, g_ref, m_ref, v_ref, d_ref, mo_ref, vo_ref):
        gv = g_ref[...]
        mn = ADAM_B1 * m_ref[...] + (1.0 - ADAM_B1) * gv
        vn = ADAM_B2 * v_ref[...] + (1.0 - ADAM_B2) * (gv * gv)
        d_ref[...] = -ADAM_LR * ((mn / c1) / (jnp.sqrt(vn / c2) + ADAM_EPS) + ADAM_WD * w_ref[...])
        mo_ref[...] = mn
        vo_ref[...] = vn

    spec = _row_spec(cols, rb)
    out = jax.ShapeDtypeStruct((rows, cols), F32)
    return pl.pallas_call(
        body, name=name, out_shape=(out, out, out), grid=(rows // rb,), in_specs=[spec] * 4, out_specs=(spec,) * 3,
        compiler_params=_params(("parallel",)),
    )(w, g, m, v)


def _prefetch_call(body, name, scalar, ops, grid, in_specs, out_specs, out_shape, sem):
    spec = pltpu.PrefetchScalarGridSpec(num_scalar_prefetch=1, grid=grid, in_specs=in_specs, out_specs=out_specs)
    return pl.pallas_call(body, name=name, grid_spec=spec, out_shape=out_shape, compiler_params=_params(sem))(scalar, *ops)


def pair_sum(g, got, core, name):
    nc, r, c = g.shape
    rh = r // 2

    def body(core_ref, g_ref, got_ref, o_ref):
        o_ref[...] = (g_ref[...].astype(F32) + got_ref[...].astype(F32)).astype(BF16)

    blk = lambda rows_of: pl.BlockSpec((None, rh, c), rows_of)
    return _prefetch_call(
        body, name, core, (g, got), (nc,),
        [blk(lambda j, core_ref: (j, core_ref[0], 0)), blk(lambda j, core_ref: (j, 0, 0))],
        blk(lambda j, core_ref: (j, 0, 0)), jax.ShapeDtypeStruct((nc, rh, c), BF16), ("parallel",))


def chip_sum(pair, got, chip, layer, into, name):
    _, rh, c = pair.shape

    def body(chip_ref, p_ref, a_ref, b_ref, c_ref, old_ref, o_ref):
        o_ref[...] = ((p_ref[...].astype(F32) + a_ref[...].astype(F32)) + b_ref[...].astype(F32)) + c_ref[...].astype(F32)

    arrival = lambda k: pl.BlockSpec((None, rh, c), lambda i, chip_ref: (k, 0, 0))
    spec = pltpu.PrefetchScalarGridSpec(
        num_scalar_prefetch=1, grid=(1,),
        in_specs=[pl.BlockSpec((None, rh, c), lambda i, chip_ref: (chip_ref[0], 0, 0)), arrival(0), arrival(1), arrival(2), _ANY],
        out_specs=pl.BlockSpec((None, rh, c), lambda i, chip_ref: (layer, 0, 0)))
    return pl.pallas_call(body, name=name, grid_spec=spec, out_shape=jax.ShapeDtypeStruct(into.shape, into.dtype),
                          input_output_aliases={5: 0}, compiler_params=_params(("arbitrary",)))(chip, pair, got, got, got, into)


def adamw_family(w, m, v, g_mine, g_other, core, name):
    nl, r, c = w.shape
    gc = g_mine.shape[2]
    rh = r // 2
    nb = 4 if rh % 512 == 0 else (2 if rh % 16 == 0 and rh > 256 else 1)
    rb = rh // nb
    c1 = 1.0 - ADAM_B1 ** ADAM_STEP
    c2 = 1.0 - ADAM_B2 ** ADAM_STEP

    def body(core_ref, w_ref, m_ref, v_ref, gm_ref, go_ref, g_ref, d_ref, mo_ref, vo_ref):
        gv = jnp.where(pl.program_id(1) == core_ref[0], gm_ref[...], go_ref[...])[:, :c]
        mn = ADAM_B1 * m_ref[...] + (1.0 - ADAM_B1) * gv
        vn = ADAM_B2 * v_ref[...] + (1.0 - ADAM_B2) * (gv * gv)
        g_ref[...] = gv
        d_ref[...] = -ADAM_LR * ((mn / c1) / (jnp.sqrt(vn / c2) + ADAM_EPS) + ADAM_WD * w_ref[...])
        mo_ref[...] = mn
        vo_ref[...] = vn

    full = pl.BlockSpec((None, rb, c), lambda l, h, i, core_ref: (l, h * nb + i, 0))
    half = pl.BlockSpec((None, rb, gc), lambda l, h, i, core_ref: (l, i, 0))
    out = jax.ShapeDtypeStruct((nl, r, c), F32)
    return _prefetch_call(body, name, core, (w, m, v, g_mine, g_other), (nl, 2, nb), [full, full, full, half, half],
                          (full, full, full, full), (out, out, out, out), ("parallel", "parallel", "parallel"))


def _coords():
    return lax.axis_index("x"), lax.axis_index("y"), lax.axis_index("c")


def _other_chips(x, y):
    return ((1 - x, y), (x, 1 - y), (1 - x, 1 - y))


_ANY = pl.BlockSpec(memory_space=pl.ANY)


def _exchange_call(body, name, arrays, out_shapes, n_copies, n_local=0):
    n = len(arrays)

    def wrapped(*refs):
        body(refs[:n], refs[n:n + len(out_shapes)], *refs[n + len(out_shapes):])

    scratch = [pltpu.SemaphoreType.DMA((n_copies,)), pltpu.SemaphoreType.DMA((n_copies,))]
    if n_local:
        scratch.append(pltpu.SemaphoreType.DMA((n_local,)))
    return pl.pallas_call(
        wrapped, name=name, out_shape=tuple(out_shapes), in_specs=[_ANY] * n, out_specs=tuple([_ANY] * len(out_shapes)),
        scratch_shapes=scratch, compiler_params=_params(),
    )(*arrays)


def _remote(send_sems, recv_sems, n, src, dst, to):
    return pltpu.make_async_remote_copy(src_ref=src, dst_ref=dst, send_sem=send_sems.at[n], recv_sem=recv_sems.at[n],
                                        device_id=to, device_id_type=MESH)


class Rider:
    def __init__(self, arrays, out_shapes, n_remote, n_local, copies, then=None):
        self.arrays, self.out_shapes, self.n_remote, self.n_local = list(arrays), list(out_shapes), n_remote, n_local
        self.copies, self.then = copies, then

    def sems(self):
        return [pltpu.SemaphoreType.DMA((self.n_remote,)), pltpu.SemaphoreType.DMA((self.n_remote,)),
                pltpu.SemaphoreType.DMA((max(self.n_local, 1),))]

    def run(self, name):
        n, no = len(self.arrays), len(self.out_shapes)

        def body(*refs):
            for stage in (self.copies, self.then):
                if stage is not None:
                    cps = stage(refs[:n], refs[n:n + no], *refs[n + no:])
                    for cp in cps:
                        cp.start()
                    for cp in cps:
                        cp.wait()

        return pl.pallas_call(
            body, name=name, out_shape=tuple(self.out_shapes), in_specs=[_ANY] * n, out_specs=tuple([_ANY] * no),
            scratch_shapes=self.sems(), compiler_params=_params(),
        )(*self.arrays)


def ride(rider, body, n_in, n_out, grid):
    if rider is None:
        return body
    ni, no = len(rider.arrays), len(rider.out_shapes)

    def wrapped(*refs):
        ins, r_in = refs[:n_in], refs[n_in:n_in + ni]
        outs = refs[n_in + ni:n_in + ni + n_out]
        r_out = refs[n_in + ni + n_out:n_in + ni + n_out + no]
        rest = refs[n_in + ni + n_out + no:]
        scratch, sems = rest[:len(rest) - 3], rest[len(rest) - 3:]
        step, total = 0, 1
        for a, g in enumerate(grid):
            step, total = step * g + pl.program_id(a), total * g
        assert total >= 3
        relay_at = (7 * total) // 8 if rider.then is not None else total - 1

        @pl.when(step == 0)
        def _():
            for cp in rider.copies(r_in, r_out, *sems):
                cp.start()

        body(*ins, *outs, *scratch)

        @pl.when(step == relay_at)
        def _():
            for cp in rider.copies(r_in, r_out, *sems):
                cp.wait()
            if rider.then is not None:
                for cp in rider.then(r_in, r_out, *sems):
                    cp.start()

        if rider.then is not None:
            @pl.when(step == total - 1)
            def _():
                for cp in rider.then(r_in, r_out, *sems):
                    cp.wait()

    return wrapped


def call_with_rider(body, name, rider, ops, in_specs, out_shape, out_specs, scratch, grid):
    n_in, n_out = len(ops), len(out_shape)
    ops, in_specs, out_shape, out_specs, scratch = list(ops), list(in_specs), list(out_shape), list(out_specs), list(scratch)
    if rider is not None:
        ops += rider.arrays
        in_specs += [_ANY] * len(rider.arrays)
        out_shape += rider.out_shapes
        out_specs += [_ANY] * len(rider.out_shapes)
        scratch += rider.sems()
    res = pl.pallas_call(
        ride(rider, body, n_in, n_out, grid), name=name, out_shape=tuple(out_shape), grid=grid, in_specs=in_specs,
        out_specs=tuple(out_specs), scratch_shapes=scratch, compiler_params=_params(("arbitrary",) * len(grid)),
    )(*ops)
    return tuple(res[:n_out]), list(res[n_out:])


def gather_rider(shards):
    nf = len(shards)
    half = lambda ref, which: pl.ds(which * (ref.shape[-2] // 2), ref.shape[-2] // 2)

    def copies(s_refs, o_refs, send_sems, recv_sems, local_sems):
        x, y, c = _coords()
        me = 2 * x + y
        cps = [pltpu.make_async_copy(s_refs[f], o_refs[f].at[me], local_sems.at[f]) for f in range(nf)]
        for k, (px, py) in enumerate(_other_chips(x, y)):
            for f in range(nf):
                rows = half(s_refs[f], c)
                cps.append(_remote(send_sems, recv_sems, k * nf + f, s_refs[f].at[rows], o_refs[f].at[me, rows], (px, py, c)))
        return cps

    def relay(s_refs, o_refs, send_sems, recv_sems, local_sems):
        x, y, c = _coords()
        cps = []
        for k, (px, py) in enumerate(_other_chips(x, y)):
            for f in range(nf):
                landed = o_refs[f].at[2 * px + py, half(s_refs[f], c)]
                cps.append(_remote(send_sems, recv_sems, (3 + k) * nf + f, landed, landed, (x, y, 1 - c)))
        return cps

    return Rider(shards, [jax.ShapeDtypeStruct((N_CHIPS,) + sh.shape, sh.dtype) for sh in shards], 6 * nf, nf, copies, relay)


def scatter_rider(pairs):
    nf = len(pairs)

    def copies(p_refs, o_refs, send_sems, recv_sems, local_sems):
        x, y, c = _coords()
        cps = []
        for k, (px, py) in enumerate(_other_chips(x, y)):
            for f in range(nf):
                cps.append(_remote(send_sems, recv_sems, k * nf + f, p_refs[f].at[2 * px + py], o_refs[f].at[k], (px, py, c)))
        return cps

    return Rider(pairs, [jax.ShapeDtypeStruct((3,) + p.shape[1:], p.dtype) for p in pairs], 3 * nf, 0, copies)


def pair_swap(grads, name):
    def body(g_refs, o_refs, send_sems, recv_sems):
        x, y, c = _coords()
        cps = []
        for f, g_ref in enumerate(g_refs):
            rh = g_ref.shape[1] // 2
            cps.append(_remote(send_sems, recv_sems, f, g_ref.at[:, pl.ds((1 - c) * rh, rh), :], o_refs[f], (x, y, 1 - c)))
        for cp in cps:
            cp.start()
        for cp in cps:
            cp.wait()

    outs = [jax.ShapeDtypeStruct((g.shape[0], g.shape[1] // 2, g.shape[2]), g.dtype) for g in grads]
    return _exchange_call(body, name, grads, outs, len(grads))


def half_swap(halves, name):
    def body(h_refs, o_refs, send_sems, recv_sems):
        x, y, c = _coords()
        cps = [_remote(send_sems, recv_sems, f, h_ref, o_refs[f], (x, y, 1 - c)) for f, h_ref in enumerate(h_refs)]
        for cp in cps:
            cp.start()
        for cp in cps:
            cp.wait()

    return _exchange_call(body, name, halves, [jax.ShapeDtypeStruct(h.shape, h.dtype) for h in halves], len(halves))


def allsum_small(part, name):
    def body(p_ref, tot_ref, all_ref, send_sems, recv_sems):
        x, y, c = _coords()
        me, sibling = (x, y, c), (x, y, 1 - c)
        chips = _other_chips(x, y)

        def slot(px, py, pc):
            return all_ref.at[4 * px + 2 * py + pc]

        def copy(k, block, to, src=None):
            return pltpu.make_async_remote_copy(src_ref=slot(*block) if src is None else src, dst_ref=slot(*block),
                                                send_sem=send_sems.at[k], recv_sem=recv_sems.at[k], device_id=to, device_id_type=MESH)

        slot(*me)[...] = p_ref[...]
        first = [copy(0, me, sibling, src=p_ref)] + [copy(1 + j, me, (*chip, c), src=p_ref) for j, chip in enumerate(chips)]
        for cp in first:
            cp.start()
        passed = [copy(4 + j, (*chip, c), sibling) for j, chip in enumerate(chips)]
        for j, chip in enumerate(chips):
            copy(1 + j, (*chip, c), me).wait_recv()
            passed[j].start()
        copy(0, sibling, me).wait_recv()
        for j, chip in enumerate(chips):
            copy(4 + j, (*chip, 1 - c), me).wait_recv()
        for cp in first + passed:
            cp.wait_send()
        tot = all_ref[0]
        for d in range(1, 8):
            tot = tot + all_ref[d]
        tot_ref[...] = tot

    vm = pl.BlockSpec(memory_space=pltpu.VMEM)
    return pl.pallas_call(
        body, name=name, out_shape=jax.ShapeDtypeStruct(part.shape, F32), in_specs=[vm], out_specs=vm,
        scratch_shapes=[pltpu.VMEM((8,) + part.shape, F32), pltpu.SemaphoreType.DMA((7,)), pltpu.SemaphoreType.DMA((7,))],
        compiler_params=_params(),
    )(part)


QKVF_COLS = 772
QKVF_PAD = 896
FORWARD_CARRY = {0: (("fi0", "fo0"), ("qkv1", "o1", "fo1")), 1: (("fi1", "qkv2", "o2"), ()),
                 2: (("fi2", "fo2"), ("qkv3", "o3", "fo3")), 3: (("fi3",), ())}


def _tables_for(s):
    return _rot_tables(s)


def layer_families(layer):
    return (0, 1, layer // 2) if layer % 2 == 0 else (2, 3, layer // 2)


class GradientExchange:
    def __init__(self):
        self.core = lax.axis_index("c").astype(jnp.int32).reshape(1)
        self.chip = (2 * lax.axis_index("x") + lax.axis_index("y")).astype(jnp.int32).reshape(1)
        self.pairs, self.arrived, self.pending = {}, {}, []

    def add(self, items, tag):
        got = pair_swap([g for _, _, g in items], f"grad_pair_swap_{tag}")
        for (fam, li, g), r in zip(items, got):
            self.pairs[(fam, li)] = pair_sum(g, r, self.core, f"grad_pair_sum_{fam}_{li}")
            self.pending.append((fam, li))

    def rider(self, only=None):
        keys = [k for k in self.pending if only is None or k in only]
        self.pending = [k for k in self.pending if k not in keys]
        return (scatter_rider([self.pairs[k] for k in keys]) if keys else None), keys

    def landed(self, keys, outs):
        self.arrived.update(zip(keys, outs))

    def finish(self, weights, moments1, moments2):
        last, keys = self.rider()
        if last is not None:
            self.landed(keys, last.run("grad_chip_scatter_last"))
        mine = []
        for fam, w in enumerate(weights):
            buf = jnp.zeros((w.shape[0],) + self.pairs[(fam, 0)].shape[1:], F32)
            for li in range(w.shape[0]):
                buf = chip_sum(self.pairs[(fam, li)], self.arrived[(fam, li)], self.chip, li, buf, f"grad_chip_sum_{fam}_{li}")
            mine.append(buf)
        other = half_swap(mine, "grad_half_swap")
        return [adamw_family(w, m, v, gm, go, self.core, f"adamw_{f}")
                for f, (w, m, v, gm, go) in enumerate(zip(weights, moments1, moments2, mine, other))]


class KeepGradients:
    def __init__(self):
        self.grads = {}

    def add(self, items, tag):
        for fam, li, g in items:
            self.grads[(fam, li)] = g

    def rider(self, only=None):
        return None, []

    def landed(self, keys, outs):
        pass


def kernel(x, norm_mix, w_qkv_even, w_o_even, w_qkvf_odd, b_forget, w_o_odd, norm_ffn, w_ffn_in, w_ffn_out, norm_final, loss_target, m_norm_mix, m_w_qkv_even, m_w_o_even, m_w_qkvf_odd, m_b_forget, m_w_o_odd, m_norm_ffn, m_w_ffn_in, m_w_ffn_out, m_norm_final, v_norm_mix, v_w_qkv_even, v_w_o_even, v_w_qkvf_odd, v_b_forget, v_w_o_odd, v_norm_ffn, v_w_ffn_in, v_w_ffn_out, v_norm_final):
    w_shards = [w_qkv_even, w_o_even, w_qkvf_odd, w_o_odd, w_ffn_in, w_ffn_out]
    shards = [w.astype(BF16) for w in w_shards]
    tables = _tables_for(x.shape[1])
    bias_pad = jnp.pad(b_forget, ((0, 0), (0, LANES - N_HEADS)))

    mine = {}
    for layer in range(DEPTH):
        fam_qkv, fam_o, li = layer_families(layer)
        mine.update({f"qkv{layer}": shards[fam_qkv][li], f"o{layer}": shards[fam_o][li],
                     f"fi{layer}": shards[4][layer], f"fo{layer}": shards[5][layer]})
    fetch = lambda names: gather_rider([mine[n] for n in names])
    have = dict(zip(("qkv0", "o0"), fetch(("qkv0", "o0")).run("gather_first")))
    saved, cur = [], x[0]
    h1 = rmsnorm_fwd(cur, norm_mix[0:1], "l0_norm_mix")
    for layer in range(DEPTH):
        carry, side_carry = FORWARD_CARRY[layer]
        next_gain = norm_mix[layer + 1:layer + 2] if layer + 1 < DEPTH else None
        cur, h1, keep = forward_layer(layer, cur, h1, have, norm_ffn[layer:layer + 1], next_gain, tables,
                                      bias_pad[layer // 2:layer // 2 + 1], fetch, carry, side_carry)
        saved.append(keep)

    dcur, g_final, loss_part = loss_head(cur, norm_final.reshape(1, D_MODEL), loss_target[0], "loss_head")

    exchange = GradientExchange()
    g_mix, g_ffn, g_bias = [None] * DEPTH, [None] * DEPTH, [None] * (DEPTH // 2)
    for layer in reversed(range(DEPTH)):
        dcur, g_mix[layer], g_ffn[layer], g_b = backward_layer(layer, dcur, saved[layer], norm_mix[layer:layer + 1],
                                                               norm_ffn[layer:layer + 1], tables, bias_pad[layer // 2:layer // 2 + 1], exchange)
        if g_b is not None:
            g_bias[layer // 2] = g_b

    zero_row = jnp.zeros((1, D_MODEL), F32)
    pad16 = lambda v: jnp.pad(v, (0, D_MODEL - v.shape[0]))[None, :]
    small_rows = lambda mix, ffn, fin, bias, last: jnp.concatenate(
        [r.reshape(1, D_MODEL) for r in mix] + [r.reshape(1, D_MODEL) for r in ffn] + [fin.reshape(1, D_MODEL)]
        + [pad16(b) for b in bias] + [last] + [zero_row] * (SMALL_ROWS - 12), axis=0)
    loss_row = pad16(loss_part[0, :1])
    small_g = allsum_small(small_rows(g_mix, g_ffn, g_final, g_bias, loss_row), "allsum_small")
    loss = small_g[11, 0]
    small_g = small_g.at[11].set(0.0)
    sw = small_rows(list(norm_mix), list(norm_ffn), norm_final, list(b_forget), zero_row)
    sm = small_rows(list(m_norm_mix), list(m_norm_ffn), m_norm_final, list(m_b_forget), zero_row)
    sv = small_rows(list(v_norm_mix), list(v_norm_ffn), v_norm_final, list(v_b_forget), zero_row)
    sd, snm, snv = adamw(sw, small_g, sm, sv, "adamw_small")

    def small_out(a):
        return a[0:4], a[8, :], a[9:11, :N_HEADS], a[4:8]

    big = exchange.finish(w_shards, [m_w_qkv_even, m_w_o_even, m_w_qkvf_odd, m_w_o_odd, m_w_ffn_in, m_w_ffn_out],
                          [v_w_qkv_even, v_w_o_even, v_w_qkvf_odd, v_w_o_odd, v_w_ffn_in, v_w_ffn_out])

    def outputs(small, which):
        mix, fin, bias, ffn = small_out(small)
        qkv_e, o_e, qkvf, o_o, fi, fo = [big[f][which] for f in range(6)]
        return [mix, qkv_e, o_e, qkvf, bias, o_o, ffn, fi, fo, fin]

    return (loss, dcur[None], *outputs(small_g, 0), *outputs(sd, 1), *outputs(snm, 2), *outputs(snv, 3))


def _chip_tile(rows, cols, at):
    return pl.BlockSpec((None, rows, cols), at)


def forward_layer(layer, cur, h1, have, ffn_gain, next_gain, tables, bias_row, fetch=None, carry=(), side_carry=()):
    n = f"l{layer}"
    s = cur.shape[0]
    w_qkv, w_o = have[f"qkv{layer}"], have[f"o{layer}"]
    rider = fetch(carry) if carry else None
    side_rider = fetch(side_carry) if side_carry else None
    keep = {"x": cur, "h1": h1, "w_o": w_o.reshape(D_ATTN, D_MODEL)}
    side = []
    if layer % 2 == 0:
        qkv = matmul(h1, w_qkv, "nn", BF16, n + "_qkv", 1024, 768, 1024, mnk=(s, 3 * D_ATTN, D_MODEL),
                     b_spec=_chip_tile(D_MODEL, 768, lambda i, j, kk: (j, 0, 0)))
        o_sb, st, rode = causal_fwd(qkv, 4, "sb", n + "_sb_fwd", rider=rider)
        qd, kd, vd = rotary_prep(qkv, tables, n + "_rotary")
        o_dil, lse_dil, side = dilated_fwd(qd, kd, vd, n + "_dil_fwd", rider=side_rider)
        attn = jnp.concatenate([o_sb, o_dil], axis=1).astype(BF16)
        keep.update(qd=qd, kd=kd, vd=vd, o_dil=o_dil, lse_dil=lse_dil, w_qkv=w_qkv)
    else:
        natural = jnp.transpose(w_qkv, (1, 0, 2)).reshape(D_MODEL, N_CHIPS * QKVF_COLS)
        w_gate = jnp.pad(natural[:, 3 * D_ATTN:], ((0, 0), (0, LANES - N_HEADS)))
        qkv = matmul(h1, natural[:, :3 * D_ATTN], "nn", BF16, n + "_qkv", 1024, 768, 1024)
        fl = matmul(h1, w_gate, "nn", F32, n + "_fgate", 512, LANES, 1024)
        cum = forget_fwd(fl, bias_row, n + "_forget_fwd")
        f_heads = cum[:, :N_HEADS].T
        fq = jnp.broadcast_to(f_heads[:, :, None], (N_HEADS, s, LANES))
        fk = f_heads.reshape(N_HEADS // 2, 2, s)
        attn, st, rode = causal_fwd(qkv, 8, "fox", n + "_fox_fwd", fq=fq, fk=fk, rider=rider)
        attn = attn.astype(BF16)
        keep.update(fl=fl, fq=fq, fk=fk, w_qkv=jnp.concatenate([natural[:, :3 * D_ATTN], w_gate], axis=1))
    have.update(zip(carry, rode))
    have.update(zip(side_carry, side))
    w_fi, w_fo = have[f"fi{layer}"], have[f"fo{layer}"].reshape(D_FF, D_MODEL)
    mid, h2 = matmul(attn, keep["w_o"], "nn", F32, n + "_attn_out", 512, 1024, 1024, res=cur, norm_gain=ffn_gain)
    gate, up, act = ffn_in_swiglu(h2, w_fi, n + "_ffn_in")
    if next_gain is None:
        out, h_next = matmul(act, w_fo, "nn", F32, n + "_ffn_out", 512, 1024, D_FF, res=mid), None
    else:
        out, h_next = matmul(act, w_fo, "nn", F32, n + "_ffn_out", 512, 1024, D_FF, res=mid, norm_gain=next_gain)
    keep.update(qkv=qkv, st=st, attn=attn, mid=mid, h2=h2, gate=gate, up=up, act=act, w_fi=w_fi, w_fo=w_fo)
    return out, h_next, keep


def backward_layer(layer, dcur, kp, mix_gain, ffn_gain, tables, bias_row, exchange):
    n = f"l{layer}"
    s = dcur.shape[0]
    fam_qkv, fam_o, li = layer_families(layer)
    g_fo = matmul(kp["act"], dcur, "tn", BF16, n + "_d_w_ffn_out", 1408, 1024, s)
    dact = matmul(dcur, kp["w_fo"], "nt", BF16, n + "_d_act", 1024, 1408, 1024)
    dgu = swiglu_bwd(kp["gate"], kp["up"], dact, n + "_d_swiglu")
    g_fi = matmul(kp["h2"], dgu, "tn", BF16, n + "_d_w_ffn_in", 1024, 1408, 2048, mnk=(D_MODEL, 2 * D_FF, s),
                  o_spec=_chip_tile(D_MODEL, 1408, lambda i, j, kk: (j, 0, 0)), out_shape=(N_CHIPS, D_MODEL, 1408))
    dmid, g_ffn = dh_norm_bwd(dgu, kp["w_fi"], kp["mid"], ffn_gain, dcur, n + "_d_h2", 256)
    g_o = matmul(kp["attn"], dmid, "tn", BF16, n + "_d_w_o", 1024, 1024, s)
    dattn = matmul(dmid, kp["w_o"], "nt", F32, n + "_d_attn", 1024, 1024, 1024)
    exchange.add([(5, layer, g_fo.reshape(N_CHIPS, D_FF // N_CHIPS, D_MODEL)), (4, layer, g_fi),
                  (fam_o, li, g_o.reshape(N_CHIPS, D_ATTN // N_CHIPS, D_MODEL))], f"l{layer}_ffn")
    g_bias = None
    if layer % 2 == 0:
        rider, keys = exchange.rider(only=[(4, layer), (fam_o, li)])
        dq_a, dk_a, dv_a, rode = causal_bwd(kp["qkv"], dattn, kp["st"], 4, "sb", n + "_sb_bwd", rider=rider)
        exchange.landed(keys, rode)
        rider, keys = exchange.rider()
        dqd, dkd, dvd, rode = dilated_bwd(kp["qd"], kp["kd"], kp["vd"], dattn, kp["o_dil"], kp["lse_dil"], 4, n + "_dil_bwd",
                                          rider=rider)
        dq_b, dk_b, dv_b = rotary_bwd(dqd, dkd, dvd, tables, n + "_d_rotary")
        dproj = jnp.concatenate([dq_a, dq_b, dk_a.astype(BF16), dk_b, dv_a.astype(BF16), dv_b], axis=1)
        g_qkv = matmul(kp["h1"], dproj, "tn", BF16, n + "_d_w_qkv", 1024, 768, 2048, mnk=(D_MODEL, 3 * D_ATTN, s),
                       o_spec=_chip_tile(D_MODEL, 768, lambda i, j, kk: (j, 0, 0)), out_shape=(N_CHIPS, D_MODEL, 768))
    else:
        rider, keys = exchange.rider()
        dq_f, dk_f, dv_f, dfk, rode = causal_bwd(kp["qkv"], dattn, kp["st"], 8, "fox", n + "_fox_bwd", fq=kp["fq"], fk=kp["fk"],
                                                 rider=rider)
        dcum = jnp.pad(dfk.reshape(N_HEADS, s).T, ((0, 0), (0, LANES - N_HEADS)))
        dfl, dbias = forget_bwd(kp["fl"], bias_row, dcum, n + "_forget_bwd")
        g_bias = dbias[0, :N_HEADS]
        dproj = jnp.concatenate([dq_f, dk_f.astype(BF16), dv_f.astype(BF16), dfl.astype(BF16)], axis=1)
        g_nat = matmul(kp["h1"], dproj, "tn", BF16, n + "_d_w_qkv", 1024, 640, 2048)
        g_qkv = g_nat[:, :N_CHIPS * QKVF_COLS].reshape(D_MODEL, N_CHIPS, QKVF_COLS)
        g_qkv = jnp.transpose(jnp.pad(g_qkv, ((0, 0), (0, 0), (0, QKVF_PAD - QKVF_COLS))), (1, 0, 2))
    exchange.landed(keys, rode)
    exchange.add([(fam_qkv, li, g_qkv)], f"l{layer}_qkv")
    dx, g_mix = dh_norm_bwd(dproj, kp["w_qkv"], kp["x"], mix_gain, dmid, n + "_d_h1", 512)
    return dx, g_mix, g_ffn, g_bias


def local_step(xs, target, norm_mix, norm_ffn, norm_final, b_forget, layer_weights):
    tables = _tables_for(xs.shape[0])
    bias_pad = jnp.pad(b_forget, ((0, 0), (0, LANES - N_HEADS)))
    saved, cur, have = [], xs, {}
    h1 = rmsnorm_fwd(cur, norm_mix[0:1], "l0_norm_mix")
    for layer in range(DEPTH):
        have.update(zip((f"qkv{layer}", f"o{layer}", f"fi{layer}", f"fo{layer}"), layer_weights[layer]))
        next_gain = norm_mix[layer + 1:layer + 2] if layer + 1 < DEPTH else None
        cur, h1, keep = forward_layer(layer, cur, h1, have, norm_ffn[layer:layer + 1], next_gain, tables,
                                      bias_pad[layer // 2:layer // 2 + 1])
        saved.append(keep)
    dcur, g_final, loss_part = loss_head(cur, norm_final.reshape(1, D_MODEL), target, "loss_head")
    keeper = KeepGradients()
    g_mix, g_ffn, g_bias = [None] * DEPTH, [None] * DEPTH, [None] * (DEPTH // 2)
    for layer in reversed(range(DEPTH)):
        dcur, g_mix[layer], g_ffn[layer], g_b = backward_layer(layer, dcur, saved[layer], norm_mix[layer:layer + 1],
                                                               norm_ffn[layer:layer + 1], tables, bias_pad[layer // 2:layer // 2 + 1], keeper)
        if g_b is not None:
            g_bias[layer // 2] = g_b
    return dcur, keeper.grads, (g_mix, g_ffn, g_final, g_bias), loss_part
```

```python
import functools

import jax
import jax.numpy as jnp
from jax import lax
from jax.experimental import pallas as pl
from jax.experimental.pallas import tpu as pltpu

F32 = jnp.float32
BF16 = jnp.bfloat16
MESH = pl.DeviceIdType.MESH

D_MODEL = 1024
DEPTH = 4
HEAD_DIM = 64
N_HEADS = 16
D_ATTN = 1024
D_FF = 2816
ROPE_THETA = 500000.0
ROT_HALF = 8
RMS_EPS = 1e-5
DIL_STRIDES = (1, 4, 16)
ADAM_LR, ADAM_B1, ADAM_B2, ADAM_EPS, ADAM_WD, ADAM_STEP = 0.001, 0.9, 0.999, 1e-8, 0.01, 10

LANES = 128
BLK = 128
VMEM_LIMIT = 56 * 1024 * 1024
NEG = -1e30
N_CHIPS = 4
FLAT_COLS = 1024
FLAT_ROWS = 12800
HALF_ROWS = FLAT_ROWS // 2
SMALL_ROWS = 16


def _params(sem=None):
    return pltpu.CompilerParams(dimension_semantics=sem, vmem_limit_bytes=VMEM_LIMIT)


def _dot(a, b):
    return lax.dot_general(a, b, (((1,), (0,)), ((), ())), preferred_element_type=F32)


def _dot_nt(a, b):
    return lax.dot_general(a, b, (((1,), (1,)), ((), ())), preferred_element_type=F32)


def _dot_tn(a, b):
    return lax.dot_general(a, b, (((0,), (0,)), ((), ())), preferred_element_type=F32)


def _split3(x):
    x1 = x.astype(BF16)
    r1 = x - x1.astype(F32)
    x2 = r1.astype(BF16)
    x3 = (r1 - x2.astype(F32)).astype(BF16)
    return x1, x2, x3


def _dot_exact_lhs(x, t):
    x1, x2, x3 = _split3(x)
    return _dot(x1, t) + _dot(x2, t) + _dot(x3, t)


def _dot_exact_rhs(t, x):
    x1, x2, x3 = _split3(x)
    return _dot(t, x1) + _dot(t, x2) + _dot(t, x3)


def _iotas(shape=(BLK, LANES)):
    return lax.broadcasted_iota(jnp.int32, shape, 0), lax.broadcasted_iota(jnp.int32, shape, 1)


_DIMS = {"nn": (((1,), (0,)), ((), ())), "nt": (((1,), (1,)), ((), ())), "tn": (((0,), (0,)), ((), ()))}


def matmul(a, b, mode, out_dtype, name, tm, tn, tk, res=None, mnk=None, b_spec=None, o_spec=None, out_shape=None, norm_gain=None):
    if mnk is not None:
        m, n, k = mnk
    elif mode == "nn":
        (m, k), (k2, n) = a.shape, b.shape
    elif mode == "nt":
        (m, k), (n, k2) = a.shape, b.shape
    else:
        (k, m), (k2, n) = a.shape, b.shape
    assert m % tm == 0 and n % tn == 0 and k % tk == 0, (name, a.shape, b.shape)
    nk = k // tk
    a_spec = pl.BlockSpec((tk, tm), lambda i, j, kk: (kk, i)) if mode == "tn" else pl.BlockSpec((tm, tk), lambda i, j, kk: (i, kk))
    if b_spec is None:
        b_spec = pl.BlockSpec((tn, tk), lambda i, j, kk: (j, kk)) if mode == "nt" else pl.BlockSpec((tk, tn), lambda i, j, kk: (kk, j))
    r_spec = pl.BlockSpec((tm, tn), lambda i, j, kk: (i, j))
    if o_spec is None:
        o_spec = r_spec
    dims = _DIMS[mode]
    has_res, has_norm = res is not None, norm_gain is not None
    assert not has_norm or (tn == n and nk == 1)
    n_in = 2 + int(has_res) + int(has_norm)

    def body(*refs):
        a_ref, b_ref = refs[0], refs[1]
        r_ref = refs[2] if has_res else None
        o_ref = refs[n_in]

        def finish(v):
            if has_res:
                v = v + r_ref[...]
            o_ref[...] = v.astype(out_dtype)
            if has_norm:
                rstd = lax.rsqrt(jnp.mean(v * v, axis=-1, keepdims=True) + RMS_EPS)
                refs[n_in + 1][...] = (v * rstd * refs[n_in - 1][...]).astype(BF16)

        bv = b_ref[...]
        if bv.ndim == 3:
            bv = jnp.concatenate([bv[j] for j in range(bv.shape[0])], axis=1)
        p = lax.dot_general(a_ref[...].astype(BF16), bv.astype(BF16), dims, preferred_element_type=F32)
        if nk == 1:
            finish(p)
        else:
            acc = refs[-1]
            kk = pl.program_id(2)

            @pl.when(kk == 0)
            def _():
                acc[...] = p

            @pl.when(kk > 0)
            def _():
                acc[...] += p

            @pl.when(kk == nk - 1)
            def _():
                finish(acc[...])

    ops = [a, b] + ([res] if has_res else []) + ([norm_gain] if has_norm else [])
    specs = [a_spec, b_spec] + ([r_spec] if has_res else []) + ([pl.BlockSpec((1, tn), lambda i, j, kk: (0, j))] if has_norm else [])
    out_shape = jax.ShapeDtypeStruct((m, n) if out_shape is None else out_shape, out_dtype)
    return pl.pallas_call(
        body, name=name, out_shape=(out_shape, jax.ShapeDtypeStruct((m, n), BF16)) if has_norm else out_shape,
        grid=(m // tm, n // tn, nk), in_specs=specs, out_specs=(o_spec, r_spec) if has_norm else o_spec,
        scratch_shapes=[pltpu.VMEM((tm, tn), F32)] if nk > 1 else [],
        compiler_params=_params(("parallel", "parallel", "arbitrary")),
    )(*ops)


ROWS = 256


def _row_spec(cols, rows=ROWS):
    return pl.BlockSpec((rows, cols), lambda i: (i, 0))


def _fix_spec(r, cols):
    return pl.BlockSpec((r, cols), lambda i: (0, 0))


def rmsnorm_fwd(x, g, name):
    s, d = x.shape

    def body(x_ref, g_ref, h_ref):
        xv = x_ref[...]
        rstd = lax.rsqrt(jnp.mean(xv * xv, axis=-1, keepdims=True) + RMS_EPS)
        h_ref[...] = (xv * rstd * g_ref[...]).astype(BF16)

    return pl.pallas_call(
        body, name=name, out_shape=jax.ShapeDtypeStruct((s, d), BF16), grid=(s // ROWS,),
        in_specs=[_row_spec(d), _fix_spec(1, d)], out_specs=_row_spec(d), compiler_params=_params(("parallel",)),
    )(x, g)


def _rms_bwd_math(xv, gv, dh):
    rstd = lax.rsqrt(jnp.mean(xv * xv, axis=-1, keepdims=True) + RMS_EPS)
    xhat = xv * rstd
    u = dh * gv
    dx = rstd * (u - xhat * jnp.mean(u * xhat, axis=-1, keepdims=True))
    return dx, dh * xhat


def dh_norm_bwd(dy, w, x, g, dres, name, tm):
    s, k = dy.shape
    d = x.shape[1]

    def body(dy_ref, w_ref, x_ref, g_ref, dres_ref, dx_ref, dg_ref):
        wv = w_ref[...]
        if wv.ndim == 3:
            wv = jnp.concatenate([wv[j] for j in range(wv.shape[0])], axis=1)
        dx, dgt = _rms_bwd_math(x_ref[...], g_ref[...], _dot_nt(dy_ref[...], wv))
        dx_ref[...] = dres_ref[...] + dx
        part = jnp.sum(dgt, axis=0, keepdims=True)

        @pl.when(pl.program_id(0) == 0)
        def _():
            dg_ref[...] = part

        @pl.when(pl.program_id(0) > 0)
        def _():
            dg_ref[...] += part

    w_spec = pl.BlockSpec(w.shape, lambda i: (0,) * w.ndim)
    return pl.pallas_call(
        body, name=name, out_shape=(jax.ShapeDtypeStruct((s, d), F32), jax.ShapeDtypeStruct((1, d), F32)), grid=(s // tm,),
        in_specs=[pl.BlockSpec((tm, k), lambda i: (i, 0)), w_spec, _row_spec(d, tm), _fix_spec(1, d), _row_spec(d, tm)],
        out_specs=(_row_spec(d, tm), _fix_spec(1, d)), compiler_params=_params(("arbitrary",)),
    )(dy, w, x, g, dres)


def rmsnorm_bwd(x, g, dh, dres, name):
    s, d = x.shape

    def body(x_ref, g_ref, dh_ref, dres_ref, dx_ref, dg_ref):
        dx, dgt = _rms_bwd_math(x_ref[...], g_ref[...], dh_ref[...])
        dx_ref[...] = dres_ref[...] + dx
        part = jnp.sum(dgt, axis=0, keepdims=True)

        @pl.when(pl.program_id(0) == 0)
        def _():
            dg_ref[...] = part

        @pl.when(pl.program_id(0) > 0)
        def _():
            dg_ref[...] += part

    return pl.pallas_call(
        body, name=name, out_shape=(jax.ShapeDtypeStruct((s, d), F32), jax.ShapeDtypeStruct((1, d), F32)),
        grid=(s // ROWS,), in_specs=[_row_spec(d), _fix_spec(1, d), _row_spec(d), _row_spec(d)],
        out_specs=(_row_spec(d), _fix_spec(1, d)), compiler_params=_params(("arbitrary",)),
    )(x, g, dh, dres)


def loss_head(x, g, target, name):
    s, d = x.shape

    def body(x_ref, g_ref, t_ref, dx_ref, dg_ref, loss_ref):
        xv, gv = x_ref[...], g_ref[...]
        rstd = lax.rsqrt(jnp.mean(xv * xv, axis=-1, keepdims=True) + RMS_EPS)
        err = xv * rstd * gv - t_ref[...]
        dx, dgt = _rms_bwd_math(xv, gv, err * (1.0 / d))
        dx_ref[...] = dx
        part = jnp.sum(dgt, axis=0, keepdims=True)
        lpart = jnp.full((1, LANES), 0.5 / d, F32) * jnp.sum(err * err)

        @pl.when(pl.program_id(0) == 0)
        def _():
            dg_ref[...] = part
            loss_ref[...] = lpart

        @pl.when(pl.program_id(0) > 0)
        def _():
            dg_ref[...] += part
            loss_ref[...] += lpart

    return pl.pallas_call(
        body, name=name,
        out_shape=(jax.ShapeDtypeStruct((s, d), F32), jax.ShapeDtypeStruct((1, d), F32), jax.ShapeDtypeStruct((1, LANES), F32)),
        grid=(s // ROWS,), in_specs=[_row_spec(d), _fix_spec(1, d), _row_spec(d)],
        out_specs=(_row_spec(d), _fix_spec(1, d), _fix_spec(1, LANES)), compiler_params=_params(("arbitrary",)),
    )(x, g, target)


def ffn_in_swiglu(h, w_in, name, tm=1024):
    s, d = h.shape
    cols = w_in.shape[2]

    def body(h_ref, wg_ref, wu_ref, g_ref, u_ref, a_ref):
        hv = h_ref[...]
        gv, uv = _dot(hv, wg_ref[...]), _dot(hv, wu_ref[...])
        g_ref[...] = gv.astype(BF16)
        u_ref[...] = uv.astype(BF16)
        a_ref[...] = (gv * (1.0 / (1.0 + jnp.exp(-gv))) * uv).astype(BF16)

    tile = pl.BlockSpec((tm, cols), lambda i, j: (i, j))
    out = jax.ShapeDtypeStruct((s, 2 * cols), BF16)
    return pl.pallas_call(
        body, name=name, out_shape=(out, out, out), grid=(s // tm, 2),
        in_specs=[pl.BlockSpec((tm, d), lambda i, j: (i, 0)), pl.BlockSpec((None, d, cols), lambda i, j: (j, 0, 0)),
                  pl.BlockSpec((None, d, cols), lambda i, j: (j + 2, 0, 0))],
        out_specs=(tile, tile, tile), compiler_params=_params(("parallel", "parallel")),
    )(h, w_in, w_in)


def swiglu_bwd(gate, up, dact, name):
    s, f = gate.shape

    def body(g_ref, u_ref, da_ref, o_ref):
        gv, uv, da = g_ref[...].astype(F32), u_ref[...].astype(F32), da_ref[...].astype(F32)
        sg = 1.0 / (1.0 + jnp.exp(-gv))
        o_ref[:, :f] = (da * uv * sg * (1.0 + gv * (1.0 - sg))).astype(BF16)
        o_ref[:, f:] = (da * gv * sg).astype(BF16)

    return pl.pallas_call(
        body, name=name, out_shape=jax.ShapeDtypeStruct((s, 2 * f), BF16), grid=(s // ROWS,),
        in_specs=[_row_spec(f)] * 3, out_specs=_row_spec(2 * f), compiler_params=_params(("parallel",)),
    )(gate, up, dact)


Q_OFF, K_OFF, V_OFF = 0, 8, 16


KB = 512
BQ = 512
SUB = KB // BLK


def _softplus_parts(z):
    sp = jnp.log(1.0 + jnp.exp(-jnp.abs(z)))
    ls = jnp.minimum(z, 0.0) - sp
    return ls, ls - z


def _wide(t):
    return jnp.concatenate([t] * SUB, axis=1)


def _chunk_dots(x, tri):
    terms = []
    for u in range(SUB):
        xu = x[:, u * BLK:(u + 1) * BLK]
        hi = xu.astype(BF16)
        terms += [hi, (xu - hi.astype(F32)).astype(BF16)]
    r = _dot(jnp.concatenate(terms, axis=0), tri)
    rows = x.shape[0]
    piece = lambda n: r[n * rows:(n + 1) * rows]
    return [piece(2 * u) + piece(2 * u + 1) for u in range(SUB)]


def _block_suffix_sums(x, suffix, c):
    loc = _chunk_dots(x, suffix)
    out = [None] * SUB
    for u in reversed(range(SUB)):
        out[u] = loc[u] + c
        c = c + jnp.sum(x[:, u * BLK:(u + 1) * BLK], axis=1, keepdims=True)
    return jnp.concatenate(out, axis=1), c


def _block_prefix_sums(x, tri, c):
    loc = _chunk_dots(x, tri)
    out = []
    for u in range(SUB):
        out.append(loc[u] + c)
        c = c + jnp.sum(x[:, u * BLK:(u + 1) * BLK], axis=1, keepdims=True)
    return jnp.concatenate(out, axis=1), c


def causal_fwd(qkv, npairs, mode, name, fq=None, fk=None, rider=None):
    s = qkv.shape[0]
    nq = s // BQ
    fox = mode == "fox"

    def body(*refs):
        if fox:
            q_ref, k_ref, v_ref, fq_ref, fk_ref, o_ref, st_ref = refs
        else:
            q_ref, k_ref, v_ref, o_ref, st_ref = refs
        i = pl.program_id(1)
        nkb = (i * BQ + BQ - 1) // KB + 1
        row, lane = _iotas((BQ, KB))
        row_s, lane_s = _iotas()
        _, lane_q = _iotas((BQ, LANES))
        nfull = (i * BQ) // KB
        qpos = i * BQ + row
        qf = q_ref[...].astype(F32) * 0.125
        hms = (lane_q < HEAD_DIM, lane_q >= HEAD_DIM)
        qas = [jnp.where(hm, qf, 0.0).astype(BF16) for hm in hms]
        suffix = jnp.where(row_s > lane_s, 1.0, 0.0).astype(BF16)
        zero = jnp.zeros((BQ, LANES), F32)
        col0 = jnp.zeros((BQ, 1), F32)

        def kv(j):
            r0 = pl.multiple_of(j * KB, KB)
            return r0, k_ref[pl.ds(r0, KB), :], v_ref[pl.ds(r0, KB), :]

        if fox:
            fqs = [_wide(fq_ref[a]) for a in range(2)]

            def step(j, carry, masked):
                r0, kb, vb = kv(j)
                new = []
                for a in range(2):
                    acc, mx, l = carry[3 * a:3 * a + 3]
                    z = _dot_nt(qas[a], kb) + fqs[a] - fk_ref[a:a + 1, pl.ds(r0, KB)]
                    if masked:
                        z = jnp.where(r0 + lane <= qpos, z, NEG)
                    mnew = jnp.maximum(mx, jnp.max(z, axis=1, keepdims=True))
                    p = jnp.exp(z - mnew)
                    alpha = jnp.exp(mx - mnew)
                    new += [alpha * acc + _dot(p.astype(BF16), vb), mnew, alpha * l + jnp.sum(p, axis=1, keepdims=True)]
                return tuple(new)

            neg = jnp.full((BQ, 1), NEG, F32)
            res = lax.fori_loop(0, nfull, functools.partial(step, masked=False), (zero, neg, col0, zero, neg, col0))
            res = lax.fori_loop(nfull, nkb, functools.partial(step, masked=True), res)
            outs = [res[3 * a] / res[3 * a + 2] for a in range(2)]
            stats = [res[3 * a + 1] + jnp.log(res[3 * a + 2]) for a in range(2)]
        else:
            def step(j, carry, masked):
                r0, kb, vb = kv(j)
                strict = r0 + lane < qpos
                new = []
                for a in range(2):
                    acc, c = carry[2 * a:2 * a + 2]
                    ls, lm = _softplus_parts(_dot_nt(qas[a], kb))
                    if masked:
                        lm = jnp.where(strict, lm, 0.0)
                    between, c = _block_suffix_sums(lm, suffix, c)
                    aw = jnp.exp(ls + between)
                    if masked:
                        aw = jnp.where(strict, aw, 0.0)
                    new += [acc + _dot(aw.astype(BF16), vb), c]
                return tuple(new)

            res = lax.fori_loop(0, nkb - nfull, lambda jj, c: step(nkb - 1 - jj, c, True), (zero, col0, zero, col0))
            res = lax.fori_loop(0, nfull, lambda jj, c: step(nfull - 1 - jj, c, False), res)
            outs, stats = [res[0], res[2]], [res[1], res[3]]
        o_ref[...] = jnp.where(hms[0], outs[0], outs[1])
        for a in range(2):
            st_ref[a] = jnp.broadcast_to(stats[a], (BQ, LANES))

    col = lambda off: (lambda p, i: (0, off + p))
    in_specs = [pl.BlockSpec((BQ, LANES), lambda p, i: (i, Q_OFF + p)),
                pl.BlockSpec((s, LANES), col(K_OFF)), pl.BlockSpec((s, LANES), col(V_OFF))]
    ops = [qkv, qkv, qkv]
    if fox:
        in_specs += [pl.BlockSpec((2, BQ, LANES), lambda p, i: (p, i, 0)), pl.BlockSpec((None, 2, s), lambda p, i: (p, 0, 0))]
        ops += [fq, fk]
    (o, stat), rode = call_with_rider(
        body, name, rider, ops, in_specs,
        [jax.ShapeDtypeStruct((s, npairs * LANES), F32), jax.ShapeDtypeStruct((2 * npairs, s, LANES), F32)],
        [pl.BlockSpec((BQ, LANES), lambda p, i: (i, p)), pl.BlockSpec((2, BQ, LANES), lambda p, i: (p, i, 0))], [], (npairs, nq))
    return o, stat, rode


def causal_bwd(qkv, do, stat, npairs, mode, name, fq=None, fk=None, rider=None):
    s = qkv.shape[0]
    nq = s // BQ
    fox = mode == "fox"

    def body(*refs):
        if fox:
            q_ref, k_ref, v_ref, do_ref, st_ref, fq_ref, fk_ref, dq_ref, dk_ref, dv_ref, df_ref, p_s, dp_s = refs
        else:
            q_ref, k_ref, v_ref, do_ref, st_ref, dq_ref, dk_ref, dv_ref = refs
        i = pl.program_id(1)

        @pl.when(i == 0)
        def _():
            dk_ref[...] = jnp.zeros_like(dk_ref)
            dv_ref[...] = jnp.zeros_like(dv_ref)
            if fox:
                df_ref[...] = jnp.zeros_like(df_ref)

        nkb = (i * BQ + BQ - 1) // KB + 1
        nfull = (i * BQ) // KB
        row, lane = _iotas((BQ, KB))
        row_s, lane_s = _iotas()
        _, lane_q = _iotas((BQ, LANES))
        qpos = i * BQ + row
        qf = q_ref[...].astype(F32) * 0.125
        dov = do_ref[...]
        hms = (lane_q < HEAD_DIM, lane_q >= HEAD_DIM)
        qas = [jnp.where(hm, qf, 0.0).astype(BF16) for hm in hms]
        doas = [jnp.where(hm, dov, 0.0).astype(BF16) for hm in hms]
        stas = [_wide(st_ref[a]) for a in range(2)]
        zero = jnp.zeros((BQ, LANES), F32)
        col0 = jnp.zeros((BQ, 1), F32)

        def kv(j):
            r0 = pl.multiple_of(j * KB, KB)
            return r0, k_ref[pl.ds(r0, KB), :], v_ref[pl.ds(r0, KB), :]

        if fox:
            fqs = [_wide(fq_ref[a]) for a in range(2)]

            def probs(j, deltas, masked):
                r0, kb, vb = kv(j)
                new = []
                for a in range(2):
                    z = _dot_nt(qas[a], kb) + fqs[a] - fk_ref[a:a + 1, pl.ds(r0, KB)]
                    p = jnp.exp(z - stas[a])
                    if masked:
                        p = jnp.where(r0 + lane <= qpos, p, 0.0)
                    dp = _dot_nt(doas[a], vb)
                    p_s[a, j] = p
                    dp_s[a, j] = dp
                    new.append(deltas[a] + jnp.sum(p * dp, axis=1, keepdims=True))
                return tuple(new)

            deltas = lax.fori_loop(0, nfull, functools.partial(probs, masked=False), (col0, col0))
            deltas = lax.fori_loop(nfull, nkb, functools.partial(probs, masked=True), deltas)

            def step(j, dqs):
                r0, kb, _ = kv(j)
                new = []
                dk = jnp.zeros((KB, LANES), F32)
                dv = jnp.zeros((KB, LANES), F32)
                for a in range(2):
                    p = p_s[a, j]
                    ds = p * (dp_s[a, j] - deltas[a])
                    dsb = ds.astype(BF16)
                    dk += _dot_tn(dsb, qas[a])
                    dv += _dot_tn(p.astype(BF16), doas[a])
                    df_ref[a:a + 1, pl.ds(r0, KB)] -= jnp.sum(ds, axis=0, keepdims=True)
                    new.append(dqs[a] + _dot(dsb, kb))
                dk_ref[pl.ds(r0, KB), :] += dk
                dv_ref[pl.ds(r0, KB), :] += dv
                return tuple(new)

            dqs = lax.fori_loop(0, nkb, step, (zero, zero))
        else:
            incl = jnp.where(row_s <= lane_s, 1.0, 0.0).astype(BF16)
            excl = jnp.where(row_s < lane_s, 1.0, 0.0).astype(BF16)

            def step(j, carry, masked):
                r0, kb, vb = kv(j)
                strict = r0 + lane < qpos
                new = []
                dk = jnp.zeros((KB, LANES), F32)
                dv = jnp.zeros((KB, LANES), F32)
                for a in range(2):
                    dq, cm, cg = carry[3 * a:3 * a + 3]
                    ls, lm = _softplus_parts(_dot_nt(qas[a], kb))
                    if masked:
                        lm = jnp.where(strict, lm, 0.0)
                    beta = jnp.exp(ls)
                    upto, cm = _block_prefix_sums(lm, incl, cm)
                    aw = jnp.exp(ls + stas[a] - upto)
                    if masked:
                        aw = jnp.where(strict, aw, 0.0)
                    g = aw * _dot_nt(doas[a], vb)
                    pre, cg = _block_prefix_sums(g, excl, cg)
                    dz = g * (1.0 - beta) - pre * beta
                    if masked:
                        dz = jnp.where(strict, dz, 0.0)
                    dzb = dz.astype(BF16)
                    dk += _dot_tn(dzb, qas[a])
                    dv += _dot_tn(aw.astype(BF16), doas[a])
                    new += [dq + _dot(dzb, kb), cm, cg]
                dk_ref[pl.ds(r0, KB), :] += dk
                dv_ref[pl.ds(r0, KB), :] += dv
                return tuple(new)

            res = lax.fori_loop(0, nfull, functools.partial(step, masked=False), (zero, col0, col0, zero, col0, col0))
            res = lax.fori_loop(nfull, nkb, functools.partial(step, masked=True), res)
            dqs = (res[0], res[3])
        dq_ref[...] = (jnp.where(hms[0], dqs[0], dqs[1]) * 0.125).astype(BF16)

    col = lambda off: (lambda p, i: (0, off + p))
    blk = pl.BlockSpec((BQ, LANES), lambda p, i: (i, p))
    acc = pl.BlockSpec((s, LANES), lambda p, i: (0, p))
    st_spec = pl.BlockSpec((2, BQ, LANES), lambda p, i: (p, i, 0))
    in_specs = [pl.BlockSpec((BQ, LANES), lambda p, i: (i, Q_OFF + p)), pl.BlockSpec((s, LANES), col(K_OFF)),
                pl.BlockSpec((s, LANES), col(V_OFF)), blk, st_spec]
    ops = [qkv, qkv, qkv, do, stat]
    w = npairs * LANES
    out_shape = [jax.ShapeDtypeStruct((s, w), BF16), jax.ShapeDtypeStruct((s, w), F32), jax.ShapeDtypeStruct((s, w), F32)]
    out_specs = [blk, acc, acc]
    scratch = []
    if fox:
        fk_spec = pl.BlockSpec((None, 2, s), lambda p, i: (p, 0, 0))
        in_specs += [st_spec, fk_spec]
        ops += [fq, fk]
        out_shape.append(jax.ShapeDtypeStruct((npairs, 2, s), F32))
        out_specs.append(fk_spec)
        scratch = [pltpu.VMEM((2, s // KB, BQ, KB), F32)] * 2
    outs, rode = call_with_rider(body, name, rider, ops, in_specs, out_shape, out_specs, scratch, (npairs, nq))
    return (*outs, rode)


def forget_fwd(fl, bias, name):
    s = fl.shape[0]

    def body(fl_ref, b_ref, f_ref):
        row, lane = _iotas()
        lower = jnp.where(lane <= row, 1.0, 0.0).astype(BF16)

        def step(n, carry):
            r0 = pl.multiple_of(n * BLK, BLK)
            ls, _ = _softplus_parts(fl_ref[pl.ds(r0, BLK), :] + b_ref[...])
            blk = _dot_exact_rhs(lower, ls) + carry
            f_ref[pl.ds(r0, BLK), :] = blk
            return blk[BLK - 1:BLK, :]

        lax.fori_loop(0, s // BLK, step, jnp.zeros((1, LANES), F32))

    return pl.pallas_call(
        body, name=name, out_shape=jax.ShapeDtypeStruct((s, LANES), F32),
        in_specs=[pl.BlockSpec(memory_space=pltpu.VMEM)] * 2, out_specs=pl.BlockSpec(memory_space=pltpu.VMEM),
        compiler_params=_params(),
    )(fl, bias)


def forget_bwd(fl, bias, df, name):
    s = fl.shape[0]
    nb = s // BLK

    def body(fl_ref, b_ref, df_ref, o_ref, db_ref):
        row, lane = _iotas()
        upper = jnp.where(lane >= row, 1.0, 0.0).astype(BF16)

        def step(nn, carry):
            tail, db = carry
            r0 = pl.multiple_of((nb - 1 - nn) * BLK, BLK)
            dls = _dot_exact_rhs(upper, df_ref[pl.ds(r0, BLK), :]) + tail
            xv = fl_ref[pl.ds(r0, BLK), :] + b_ref[...]
            dfl = dls * (1.0 / (1.0 + jnp.exp(xv)))
            o_ref[pl.ds(r0, BLK), :] = dfl
            return dls[0:1, :], db + jnp.sum(dfl, axis=0, keepdims=True)

        _, db = lax.fori_loop(0, nb, step, (jnp.zeros((1, LANES), F32), jnp.zeros((1, LANES), F32)))
        db_ref[...] = db

    return pl.pallas_call(
        body, name=name, out_shape=(jax.ShapeDtypeStruct((s, LANES), F32), jax.ShapeDtypeStruct((1, LANES), F32)),
        in_specs=[pl.BlockSpec(memory_space=pltpu.VMEM)] * 3,
        out_specs=(pl.BlockSpec(memory_space=pltpu.VMEM), pl.BlockSpec(memory_space=pltpu.VMEM)),
        compiler_params=_params(),
    )(fl, bias, df)


def _rot_tables(s):
    inv = ROPE_THETA ** (-jnp.arange(ROT_HALF, dtype=F32) * 2.0 / (2 * ROT_HALF))
    ang = jnp.arange(s, dtype=F32)[:, None] * inv[None, :]
    cos, sin = jnp.cos(ang), jnp.sin(ang)
    z8 = jnp.zeros((s, ROT_HALF), F32)
    rest = HEAD_DIM - 2 * ROT_HALF
    zr, onr = jnp.zeros((s, rest), F32), jnp.ones((s, rest), F32)
    tile = lambda t: jnp.tile(t, (1, 2))
    return tile(jnp.concatenate([cos, cos, onr], 1)), tile(jnp.concatenate([-sin, z8, zr], 1)), tile(jnp.concatenate([z8, sin, zr], 1))


def rotary_prep(qkv, tables, name):
    s = qkv.shape[0]
    w = 4 * LANES

    def body(q_ref, k_ref, v_ref, c_ref, s1_ref, s2_ref, qo_ref, ko_ref, vo_ref):
        c, s1, s2 = c_ref[...], s1_ref[...], s2_ref[...]

        def rot(xv):
            return xv * c + pltpu.roll(xv, LANES - ROT_HALF, 1) * s1 + pltpu.roll(xv, ROT_HALF, 1) * s2

        qo_ref[...] = rot(q_ref[...].astype(F32)) * 0.125
        ko_ref[...] = rot(k_ref[...].astype(F32))
        vo_ref[...] = v_ref[...].astype(F32)

    cb = lambda off: pl.BlockSpec((ROWS, LANES), lambda i, j: (i, off + j))
    tb = pl.BlockSpec((ROWS, LANES), lambda i, j: (i, 0))
    out = jax.ShapeDtypeStruct((s, w), F32)
    return pl.pallas_call(
        body, name=name, out_shape=(out, out, out), grid=(s // ROWS, 4),
        in_specs=[cb(Q_OFF + 4), cb(K_OFF + 4), cb(V_OFF + 4), tb, tb, tb], out_specs=(cb(0), cb(0), cb(0)),
        compiler_params=_params(("parallel", "parallel")),
    )(qkv, qkv, qkv, *tables)


def rotary_bwd(dq, dk, dv, tables, name):
    s, w = dq.shape

    def body(dq_ref, dk_ref, dv_ref, c_ref, s1_ref, s2_ref, qo_ref, ko_ref, vo_ref):
        c, s1, s2 = c_ref[...], s1_ref[...], s2_ref[...]

        def rot_t(dy):
            return dy * c + pltpu.roll(dy * s1, ROT_HALF, 1) + pltpu.roll(dy * s2, LANES - ROT_HALF, 1)

        qo_ref[...] = (rot_t(dq_ref[...]) * 0.125).astype(BF16)
        ko_ref[...] = rot_t(dk_ref[...]).astype(BF16)
        vo_ref[...] = dv_ref[...].astype(BF16)

    cb = pl.BlockSpec((ROWS, LANES), lambda i, j: (i, j))
    tb = pl.BlockSpec((ROWS, LANES), lambda i, j: (i, 0))
    out = jax.ShapeDtypeStruct((s, w), BF16)
    return pl.pallas_call(
        body, name=name, out_shape=(out, out, out), grid=(s // ROWS, w // LANES),
        in_specs=[cb, cb, cb, tb, tb, tb], out_specs=(cb, cb, cb), compiler_params=_params(("parallel", "parallel")),
    )(dq, dk, dv, *tables)


def _deinterleave(dst, src_ref, stride, s, dtype):
    length = s // stride
    for r in range(stride):
        if stride == 1:
            dst[...] = src_ref[...].astype(dtype)
        else:
            dst[r * length:(r + 1) * length, :] = src_ref[pl.ds(r, length, stride=stride), :].astype(dtype)


def _band_masks(row, lane, first):
    return lane <= row, lane >= row + jnp.where(first, BLK, 0)


N_DIL_PAIRS = 4


def _rotate_into(q_ref, k_ref, v_ref, c_ref, s1_ref, s2_ref, qr, kr, vr):
    c, s1, s2 = c_ref[...], s1_ref[...], s2_ref[...]
    rot = lambda xv: xv * c + pltpu.roll(xv, LANES - ROT_HALF, 1) * s1 + pltpu.roll(xv, ROT_HALF, 1) * s2
    qr[...] = rot(q_ref[...].astype(F32)) * 0.125
    kr[...] = rot(k_ref[...].astype(F32))
    vr[...] = v_ref[...].astype(F32)


def _dilated_operands(qkv, tables):
    s = qkv.shape[0]
    col = lambda off: pl.BlockSpec((s, LANES), lambda p: (0, off + N_DIL_PAIRS + p))
    table = pl.BlockSpec((s, LANES), lambda p: (0, 0))
    return [qkv, qkv, qkv, *tables], [col(Q_OFF), col(K_OFF), col(V_OFF), table, table, table]


def dilated_fwd(qkv, tables, name, rider=None):
    s = qkv.shape[0]
    npairs, w = N_DIL_PAIRS, N_DIL_PAIRS * LANES
    nblk = s // BLK

    def body(q_in, k_in, v_in, c_ref, s1_ref, s2_ref, o_ref, lse_ref, q_ref, k_ref, v_ref, qs, ks, vs, od, ld, on, ln):
        row, lane = _iotas()
        _rotate_into(q_in, k_in, v_in, c_ref, s1_ref, s2_ref, q_ref, k_ref, v_ref)
        for pi, stride in enumerate(DIL_STRIDES):
            per = (s // stride) // BLK
            _deinterleave(qs, q_ref, stride, s, BF16)
            _deinterleave(ks, k_ref, stride, s, BF16)
            _deinterleave(vs, v_ref, stride, s, BF16)

            def block(b, carry):
                r0 = pl.multiple_of(b * BLK, BLK)
                rp = pl.multiple_of(jnp.maximum(b - 1, 0) * BLK, BLK)
                mc, mp = _band_masks(row, lane, b % per == 0)
                q = qs[pl.ds(r0, BLK), :]
                kc, kp, vc, vp = ks[pl.ds(r0, BLK), :], ks[pl.ds(rp, BLK), :], vs[pl.ds(r0, BLK), :], vs[pl.ds(rp, BLK), :]
                out = jnp.zeros((BLK, LANES), F32)
                lse = jnp.zeros((BLK, LANES), F32)
                for a in range(2):
                    hm = (lane < HEAD_DIM) if a == 0 else (lane >= HEAD_DIM)
                    qa = jnp.where(hm, q.astype(F32), 0.0).astype(BF16)
                    sc = jnp.where(mc, _dot_nt(qa, kc), NEG)
                    sp = jnp.where(mp, _dot_nt(qa, kp), NEG)
                    mx = jnp.maximum(jnp.max(sc, axis=1, keepdims=True), jnp.max(sp, axis=1, keepdims=True))
                    pc, pp = jnp.exp(sc - mx), jnp.exp(sp - mx)
                    l = jnp.sum(pc, axis=1, keepdims=True) + jnp.sum(pp, axis=1, keepdims=True)
                    oa = (_dot(pc.astype(BF16), vc) + _dot(pp.astype(BF16), vp)) / l
                    out = jnp.where(hm, oa, out)
                    lse = jnp.where(hm, mx + jnp.log(l), lse)
                od[pl.ds(r0, BLK), :] = out
                ld[pl.ds(r0, BLK), :] = lse
                return carry

            lax.fori_loop(0, nblk, block, 0, unroll=2)
            length = s // stride
            for r in range(stride):
                if stride == 1:
                    on[pi] = od[...]
                    ln[pi] = ld[...]
                else:
                    on[pi, pl.ds(r, length, stride=stride), :] = od[r * length:(r + 1) * length, :]
                    ln[pi, pl.ds(r, length, stride=stride), :] = ld[r * length:(r + 1) * length, :]

        def merge(n, carry):
            r0 = pl.multiple_of(n * BLK, BLK)
            ls = [ln[pi, pl.ds(r0, BLK), :] for pi in range(3)]
            mx = jnp.maximum(jnp.maximum(ls[0], ls[1]), ls[2])
            ws = [jnp.exp(lv - mx) for lv in ls]
            den = ws[0] + ws[1] + ws[2]
            num = ws[0] * on[0, pl.ds(r0, BLK), :] + ws[1] * on[1, pl.ds(r0, BLK), :] + ws[2] * on[2, pl.ds(r0, BLK), :]
            o_ref[pl.ds(r0, BLK), :] = num / den
            lse_ref[pl.ds(r0, BLK), :] = mx + jnp.log(den)
            return carry

        lax.fori_loop(0, nblk, merge, 0, unroll=2)

    colspec = pl.BlockSpec((s, LANES), lambda p: (0, p))
    out = jax.ShapeDtypeStruct((s, w), F32)
    scratch = ([pltpu.VMEM((s, LANES), F32)] * 3 + [pltpu.VMEM((s, LANES), BF16)] * 3 + [pltpu.VMEM((s, LANES), F32)] * 2
               + [pltpu.VMEM((3, s, LANES), F32)] * 2)
    ops, in_specs = _dilated_operands(qkv, tables)
    (o, lse), rode = call_with_rider(body, name, rider, ops, in_specs, [out, out], [colspec, colspec], scratch, (npairs,))
    return o, lse, rode


def dilated_bwd(qkv, tables, do, out, lse, do_off, name, rider=None):
    s = qkv.shape[0]
    npairs, w = N_DIL_PAIRS, N_DIL_PAIRS * LANES
    nblk = s // BLK

    def body(q_in, k_in, v_in, c_ref, s1_ref, s2_ref, do_ref, out_ref, lse_ref, dq_out, dk_out, dv_out,
             q_ref, k_ref, v_ref, dq_ref, dk_ref, dv_ref, qs, ks, vs, dos, dls, lss, dqd, dkd, dvd, dln):
        row, lane = _iotas()
        same_head = jnp.where((row < HEAD_DIM) == (lane < HEAD_DIM), 1.0, 0.0).astype(BF16)
        _rotate_into(q_in, k_in, v_in, c_ref, s1_ref, s2_ref, q_ref, k_ref, v_ref)

        def delta_blk(n, carry):
            r0 = pl.multiple_of(n * BLK, BLK)
            dln[pl.ds(r0, BLK), :] = _dot_exact_lhs(do_ref[pl.ds(r0, BLK), :] * out_ref[pl.ds(r0, BLK), :], same_head)
            return carry

        lax.fori_loop(0, nblk, delta_blk, 0, unroll=2)
        for pi, stride in enumerate(DIL_STRIDES):
            per = (s // stride) // BLK
            _deinterleave(qs, q_ref, stride, s, BF16)
            _deinterleave(ks, k_ref, stride, s, BF16)
            _deinterleave(vs, v_ref, stride, s, BF16)
            _deinterleave(dos, do_ref, stride, s, BF16)
            _deinterleave(dls, dln, stride, s, F32)
            _deinterleave(lss, lse_ref, stride, s, F32)

            def block(b, carry):
                r0 = pl.multiple_of(b * BLK, BLK)
                rp = pl.multiple_of(jnp.maximum(b - 1, 0) * BLK, BLK)
                first = b % per == 0
                mc, mp = _band_masks(row, lane, first)
                q, dov = qs[pl.ds(r0, BLK), :], dos[pl.ds(r0, BLK), :]
                kc, kp, vc, vp = ks[pl.ds(r0, BLK), :], ks[pl.ds(rp, BLK), :], vs[pl.ds(r0, BLK), :], vs[pl.ds(rp, BLK), :]
                lse_t, dl_t = lss[pl.ds(r0, BLK), :], dls[pl.ds(r0, BLK), :]
                dq = jnp.zeros((BLK, LANES), F32)
                dkc = jnp.zeros((BLK, LANES), F32)
                dkp = jnp.zeros((BLK, LANES), F32)
                dvc = jnp.zeros((BLK, LANES), F32)
                dvp = jnp.zeros((BLK, LANES), F32)
                for a in range(2):
                    hm = (lane < HEAD_DIM) if a == 0 else (lane >= HEAD_DIM)
                    pick = lane == a * HEAD_DIM
                    qa = jnp.where(hm, q.astype(F32), 0.0).astype(BF16)
                    doa = jnp.where(hm, dov.astype(F32), 0.0).astype(BF16)
                    lse_a = jnp.sum(jnp.where(pick, lse_t, 0.0), axis=1, keepdims=True)
                    dl_a = jnp.sum(jnp.where(pick, dl_t, 0.0), axis=1, keepdims=True)
                    pc = jnp.where(mc, jnp.exp(_dot_nt(qa, kc) - lse_a), 0.0)
                    pp = jnp.where(mp, jnp.exp(_dot_nt(qa, kp) - lse_a), 0.0)
                    dsc = (pc * (_dot_nt(doa, vc) - dl_a)).astype(BF16)
                    dsp = (pp * (_dot_nt(doa, vp) - dl_a)).astype(BF16)
                    dq = jnp.where(hm, _dot(dsc, kc) + _dot(dsp, kp), dq)
                    dkc += _dot_tn(dsc, qa)
                    dkp += _dot_tn(dsp, qa)
                    dvc += _dot_tn(pc.astype(BF16), doa)
                    dvp += _dot_tn(pp.astype(BF16), doa)
                dqd[pl.ds(r0, BLK), :] = dq
                dkd[pl.ds(r0, BLK), :] = dkc
                dvd[pl.ds(r0, BLK), :] = dvc

                @pl.when(jnp.logical_not(first))
                def _():
                    dkd[pl.ds(rp, BLK), :] += dkp
                    dvd[pl.ds(rp, BLK), :] += dvp

                return carry

            lax.fori_loop(0, nblk, block, 0, unroll=2)
            length = s // stride
            for dst, src in ((dq_ref, dqd), (dk_ref, dkd), (dv_ref, dvd)):
                for r in range(stride):
                    if stride == 1:
                        dst[...] = src[...]
                    else:
                        dst[pl.ds(r, length, stride=stride), :] += src[r * length:(r + 1) * length, :]

        c, s1, s2 = c_ref[...], s1_ref[...], s2_ref[...]
        rot_t = lambda dy: dy * c + pltpu.roll(dy * s1, ROT_HALF, 1) + pltpu.roll(dy * s2, LANES - ROT_HALF, 1)
        dq_out[...] = (rot_t(dq_ref[...]) * 0.125).astype(BF16)
        dk_out[...] = rot_t(dk_ref[...]).astype(BF16)
        dv_out[...] = dv_ref[...].astype(BF16)

    colspec = pl.BlockSpec((s, LANES), lambda p: (0, p))
    do_spec = pl.BlockSpec((s, LANES), lambda p: (0, do_off + p))
    o3 = jax.ShapeDtypeStruct((s, w), BF16)
    scratch = [pltpu.VMEM((s, LANES), F32)] * 6 + [pltpu.VMEM((s, LANES), BF16)] * 4 + [pltpu.VMEM((s, LANES), F32)] * 6
    ops, in_specs = _dilated_operands(qkv, tables)
    outs, rode = call_with_rider(body, name, rider, ops + [do, out, lse], in_specs + [do_spec, colspec, colspec],
                                 [o3, o3, o3], [colspec, colspec, colspec], scratch, (npairs,))
    return (*outs, rode)


def adamw(w, g, m, v, name):
    rows, cols = w.shape
    rb = min(rows, ROWS)
    c1 = 1.0 - ADAM_B1 ** ADAM_STEP
    c2 = 1.0 - ADAM_B2 ** ADAM_STEP

    def body(w_ref, g_ref, m_ref, v_ref, d_ref, mo_ref, vo_ref):
        gv = g_ref[...]
        mn = ADAM_B1 * m_ref[...] + (1.0 - ADAM_B1) * gv
        vn = ADAM_B2 * v_ref[...] + (1.0 - ADAM_B2) * (gv * gv)
        d_ref[...] = -ADAM_LR * ((mn / c1) / (jnp.sqrt(vn / c2) + ADAM_EPS) + ADAM_WD * w_ref[...])
        mo_ref[...] = mn
        vo_ref[...] = vn

    spec = _row_spec(cols, rb)
    out = jax.ShapeDtypeStruct((rows, cols), F32)
    return pl.pallas_call(
        body, name=name, out_shape=(out, out, out), grid=(rows // rb,), in_specs=[spec] * 4, out_specs=(spec,) * 3,
        compiler_params=_params(("parallel",)),
    )(w, g, m, v)


def _prefetch_call(body, name, scalar, ops, grid, in_specs, out_specs, out_shape, sem):
    spec = pltpu.PrefetchScalarGridSpec(num_scalar_prefetch=1, grid=grid, in_specs=in_specs, out_specs=out_specs)
    return pl.pallas_call(body, name=name, grid_spec=spec, out_shape=out_shape, compiler_params=_params(sem))(scalar, *ops)


def pair_sum(g, got, core, name):
    nc, r, c = g.shape
    rh = r // 2

    def body(core_ref, g_ref, got_ref, o_ref):
        o_ref[...] = (g_ref[...].astype(F32) + got_ref[...].astype(F32)).astype(BF16)

    blk = lambda rows_of: pl.BlockSpec((None, rh, c), rows_of)
    return _prefetch_call(
        body, name, core, (g, got), (nc,),
        [blk(lambda j, core_ref: (j, core_ref[0], 0)), blk(lambda j, core_ref: (j, 0, 0))],
        blk(lambda j, core_ref: (j, 0, 0)), jax.ShapeDtypeStruct((nc, rh, c), BF16), ("parallel",))


def chip_sum(pair, got, chip, layer, into, name):
    _, rh, c = pair.shape

    def body(chip_ref, p_ref, a_ref, b_ref, c_ref, old_ref, o_ref):
        o_ref[...] = ((p_ref[...].astype(F32) + a_ref[...].astype(F32)) + b_ref[...].astype(F32)) + c_ref[...].astype(F32)

    arrival = lambda k: pl.BlockSpec((None, rh, c), lambda i, chip_ref: (k, 0, 0))
    spec = pltpu.PrefetchScalarGridSpec(
        num_scalar_prefetch=1, grid=(1,),
        in_specs=[pl.BlockSpec((None, rh, c), lambda i, chip_ref: (chip_ref[0], 0, 0)), arrival(0), arrival(1), arrival(2), _ANY],
        out_specs=pl.BlockSpec((None, rh, c), lambda i, chip_ref: (layer, 0, 0)))
    return pl.pallas_call(body, name=name, grid_spec=spec, out_shape=jax.ShapeDtypeStruct(into.shape, into.dtype),
                          input_output_aliases={5: 0}, compiler_params=_params(("arbitrary",)))(chip, pair, got, got, got, into)


def adamw_family(w, m, v, g_mine, g_other, core, name):
    nl, r, c = w.shape
    gc = g_mine.shape[2]
    rh = r // 2
    nb = 4 if rh % 512 == 0 else (2 if rh % 16 == 0 and rh > 256 else 1)
    rb = rh // nb
    c1 = 1.0 - ADAM_B1 ** ADAM_STEP
    c2 = 1.0 - ADAM_B2 ** ADAM_STEP

    def body(core_ref, w_ref, m_ref, v_ref, gm_ref, go_ref, g_ref, d_ref, mo_ref, vo_ref):
        gv = jnp.where(pl.program_id(1) == core_ref[0], gm_ref[...], go_ref[...])[:, :c]
        mn = ADAM_B1 * m_ref[...] + (1.0 - ADAM_B1) * gv
        vn = ADAM_B2 * v_ref[...] + (1.0 - ADAM_B2) * (gv * gv)
        g_ref[...] = gv
        d_ref[...] = -ADAM_LR * ((mn / c1) / (jnp.sqrt(vn / c2) + ADAM_EPS) + ADAM_WD * w_ref[...])
        mo_ref[...] = mn
        vo_ref[...] = vn

    full = pl.BlockSpec((None, rb, c), lambda l, h, i, core_ref: (l, h * nb + i, 0))
    half = pl.BlockSpec((None, rb, gc), lambda l, h, i, core_ref: (l, i, 0))
    out = jax.ShapeDtypeStruct((nl, r, c), F32)
    return _prefetch_call(body, name, core, (w, m, v, g_mine, g_other), (nl, 2, nb), [full, full, full, half, half],
                          (full, full, full, full), (out, out, out, out), ("parallel", "parallel", "parallel"))


def _coords():
    return lax.axis_index("x"), lax.axis_index("y"), lax.axis_index("c")


def _other_chips(x, y):
    return ((1 - x, y), (x, 1 - y), (1 - x, 1 - y))


_ANY = pl.BlockSpec(memory_space=pl.ANY)


def _exchange_call(body, name, arrays, out_shapes, n_copies, n_local=0):
    n = len(arrays)

    def wrapped(*refs):
        body(refs[:n], refs[n:n + len(out_shapes)], *refs[n + len(out_shapes):])

    scratch = [pltpu.SemaphoreType.DMA((n_copies,)), pltpu.SemaphoreType.DMA((n_copies,))]
    if n_local:
        scratch.append(pltpu.SemaphoreType.DMA((n_local,)))
    return pl.pallas_call(
        wrapped, name=name, out_shape=tuple(out_shapes), in_specs=[_ANY] * n, out_specs=tuple([_ANY] * len(out_shapes)),
        scratch_shapes=scratch, compiler_params=_params(),
    )(*arrays)


def _remote(send_sems, recv_sems, n, src, dst, to):
    return pltpu.make_async_remote_copy(src_ref=src, dst_ref=dst, send_sem=send_sems.at[n], recv_sem=recv_sems.at[n],
                                        device_id=to, device_id_type=MESH)


class Rider:
    def __init__(self, arrays, out_shapes, n_remote, n_local, copies, then=None):
        self.arrays, self.out_shapes, self.n_remote, self.n_local = list(arrays), list(out_shapes), n_remote, n_local
        self.copies, self.then = copies, then

    def sems(self):
        return [pltpu.SemaphoreType.DMA((self.n_remote,)), pltpu.SemaphoreType.DMA((self.n_remote,)),
                pltpu.SemaphoreType.DMA((max(self.n_local, 1),))]

    def run(self, name):
        n, no = len(self.arrays), len(self.out_shapes)

        def body(*refs):
            for stage in (self.copies, self.then):
                if stage is not None:
                    cps = stage(refs[:n], refs[n:n + no], *refs[n + no:])
                    for cp in cps:
                        cp.start()
                    for cp in cps:
                        cp.wait()

        return pl.pallas_call(
            body, name=name, out_shape=tuple(self.out_shapes), in_specs=[_ANY] * n, out_specs=tuple([_ANY] * no),
            scratch_shapes=self.sems(), compiler_params=_params(),
        )(*self.arrays)


def ride(rider, body, n_in, n_out, grid):
    if rider is None:
        return body
    ni, no = len(rider.arrays), len(rider.out_shapes)

    def wrapped(*refs):
        ins, r_in = refs[:n_in], refs[n_in:n_in + ni]
        outs = refs[n_in + ni:n_in + ni + n_out]
        r_out = refs[n_in + ni + n_out:n_in + ni + n_out + no]
        rest = refs[n_in + ni + n_out + no:]
        scratch, sems = rest[:len(rest) - 3], rest[len(rest) - 3:]
        step, total = 0, 1
        for a, g in enumerate(grid):
            step, total = step * g + pl.program_id(a), total * g
        assert total >= 3
        relay_at = (7 * total) // 8 if rider.then is not None else total - 1

        @pl.when(step == 0)
        def _():
            for cp in rider.copies(r_in, r_out, *sems):
                cp.start()

        body(*ins, *outs, *scratch)

        @pl.when(step == relay_at)
        def _():
            for cp in rider.copies(r_in, r_out, *sems):
                cp.wait()
            if rider.then is not None:
                for cp in rider.then(r_in, r_out, *sems):
                    cp.start()

        if rider.then is not None:
            @pl.when(step == total - 1)
            def _():
                for cp in rider.then(r_in, r_out, *sems):
                    cp.wait()

    return wrapped


def call_with_rider(body, name, rider, ops, in_specs, out_shape, out_specs, scratch, grid):
    n_in, n_out = len(ops), len(out_shape)
    ops, in_specs, out_shape, out_specs, scratch = list(ops), list(in_specs), list(out_shape), list(out_specs), list(scratch)
    if rider is not None:
        ops += rider.arrays
        in_specs += [_ANY] * len(rider.arrays)
        out_shape += rider.out_shapes
        out_specs += [_ANY] * len(rider.out_shapes)
        scratch += rider.sems()
    res = pl.pallas_call(
        ride(rider, body, n_in, n_out, grid), name=name, out_shape=tuple(out_shape), grid=grid, in_specs=in_specs,
        out_specs=tuple(out_specs), scratch_shapes=scratch, compiler_params=_params(("arbitrary",) * len(grid)),
    )(*ops)
    return tuple(res[:n_out]), list(res[n_out:])


def gather_rider(shards):
    nf = len(shards)
    half = lambda ref, which: pl.ds(which * (ref.shape[-2] // 2), ref.shape[-2] // 2)

    def copies(s_refs, o_refs, send_sems, recv_sems, local_sems):
        x, y, c = _coords()
        me = 2 * x + y
        cps = [pltpu.make_async_copy(s_refs[f], o_refs[f].at[me], local_sems.at[f]) for f in range(nf)]
        for k, (px, py) in enumerate(_other_chips(x, y)):
            for f in range(nf):
                rows = half(s_refs[f], c)
                cps.append(_remote(send_sems, recv_sems, k * nf + f, s_refs[f].at[rows], o_refs[f].at[me, rows], (px, py, c)))
        return cps

    def relay(s_refs, o_refs, send_sems, recv_sems, local_sems):
        x, y, c = _coords()
        cps = []
        for k, (px, py) in enumerate(_other_chips(x, y)):
            for f in range(nf):
                landed = o_refs[f].at[2 * px + py, half(s_refs[f], c)]
                cps.append(_remote(send_sems, recv_sems, (3 + k) * nf + f, landed, landed, (x, y, 1 - c)))
        return cps

    return Rider(shards, [jax.ShapeDtypeStruct((N_CHIPS,) + sh.shape, sh.dtype) for sh in shards], 6 * nf, nf, copies, relay)


def scatter_rider(pairs):
    nf = len(pairs)

    def copies(p_refs, o_refs, send_sems, recv_sems, local_sems):
        x, y, c = _coords()
        cps = []
        for k, (px, py) in enumerate(_other_chips(x, y)):
            for f in range(nf):
                cps.append(_remote(send_sems, recv_sems, k * nf + f, p_refs[f].at[2 * px + py], o_refs[f].at[k], (px, py, c)))
        return cps

    return Rider(pairs, [jax.ShapeDtypeStruct((3,) + p.shape[1:], p.dtype) for p in pairs], 3 * nf, 0, copies)


def pair_swap(grads, name):
    def body(g_refs, o_refs, send_sems, recv_sems):
        x, y, c = _coords()
        cps = []
        for f, g_ref in enumerate(g_refs):
            rh = g_ref.shape[1] // 2
            cps.append(_remote(send_sems, recv_sems, f, g_ref.at[:, pl.ds((1 - c) * rh, rh), :], o_refs[f], (x, y, 1 - c)))
        for cp in cps:
            cp.start()
        for cp in cps:
            cp.wait()

    outs = [jax.ShapeDtypeStruct((g.shape[0], g.shape[1] // 2, g.shape[2]), g.dtype) for g in grads]
    return _exchange_call(body, name, grads, outs, len(grads))


def half_swap(halves, name):
    def body(h_refs, o_refs, send_sems, recv_sems):
        x, y, c = _coords()
        cps = [_remote(send_sems, recv_sems, f, h_ref, o_refs[f], (x, y, 1 - c)) for f, h_ref in enumerate(h_refs)]
        for cp in cps:
            cp.start()
        for cp in cps:
            cp.wait()

    return _exchange_call(body, name, halves, [jax.ShapeDtypeStruct(h.shape, h.dtype) for h in halves], len(halves))


def allsum_small(part, name):
    def body(p_ref, tot_ref, all_ref, send_sems, recv_sems):
        x, y, c = _coords()
        me, sibling = (x, y, c), (x, y, 1 - c)
        chips = _other_chips(x, y)

        def slot(px, py, pc):
            return all_ref.at[4 * px + 2 * py + pc]

        def copy(k, block, to, src=None):
            return pltpu.make_async_remote_copy(src_ref=slot(*block) if src is None else src, dst_ref=slot(*block),
                                                send_sem=send_sems.at[k], recv_sem=recv_sems.at[k], device_id=to, device_id_type=MESH)

        slot(*me)[...] = p_ref[...]
        first = [copy(0, me, sibling, src=p_ref)] + [copy(1 + j, me, (*chip, c), src=p_ref) for j, chip in enumerate(chips)]
        for cp in first:
            cp.start()
        passed = [copy(4 + j, (*chip, c), sibling) for j, chip in enumerate(chips)]
        for j, chip in enumerate(chips):
            copy(1 + j, (*chip, c), me).wait_recv()
            passed[j].start()
        copy(0, sibling, me).wait_recv()
        for j, chip in enumerate(chips):
            copy(4 + j, (*chip, 1 - c), me).wait_recv()
        for cp in first + passed:
            cp.wait_send()
        tot = all_ref[0]
        for d in range(1, 8):
            tot = tot + all_ref[d]
        tot_ref[...] = tot

    vm = pl.BlockSpec(memory_space=pltpu.VMEM)
    return pl.pallas_call(
        body, name=name, out_shape=jax.ShapeDtypeStruct(part.shape, F32), in_specs=[vm], out_specs=vm,
        scratch_shapes=[pltpu.VMEM((8,) + part.shape, F32), pltpu.SemaphoreType.DMA((7,)), pltpu.SemaphoreType.DMA((7,))],
        compiler_params=_params(),
    )(part)


QKVF_COLS = 772
QKVF_PAD = 896
FORWARD_CARRY = {0: (("fi0", "fo0"), ("qkv1", "o1", "fo1")), 1: (("fi1", "qkv2", "o2"), ()),
                 2: (("fi2", "fo2"), ("qkv3", "o3", "fo3")), 3: (("fi3",), ())}


def _tables_for(s):
    return _rot_tables(s)


def layer_families(layer):
    return (0, 1, layer // 2) if layer % 2 == 0 else (2, 3, layer // 2)


class GradientExchange:
    def __init__(self):
        self.core = lax.axis_index("c").astype(jnp.int32).reshape(1)
        self.chip = (2 * lax.axis_index("x") + lax.axis_index("y")).astype(jnp.int32).reshape(1)
        self.pairs, self.arrived, self.pending = {}, {}, []

    def add(self, items, tag):
        got = pair_swap([g for _, _, g in items], f"grad_pair_swap_{tag}")
        for (fam, li, g), r in zip(items, got):
            self.pairs[(fam, li)] = pair_sum(g, r, self.core, f"grad_pair_sum_{fam}_{li}")
            self.pending.append((fam, li))

    def rider(self, only=None):
        keys = [k for k in self.pending if only is None or k in only]
        self.pending = [k for k in self.pending if k not in keys]
        return (scatter_rider([self.pairs[k] for k in keys]) if keys else None), keys

    def landed(self, keys, outs):
        self.arrived.update(zip(keys, outs))

    def finish(self, weights, moments1, moments2):
        last, keys = self.rider()
        if last is not None:
            self.landed(keys, last.run("grad_chip_scatter_last"))
        mine = []
        for fam, w in enumerate(weights):
            buf = jnp.zeros((w.shape[0],) + self.pairs[(fam, 0)].shape[1:], F32)
            for li in range(w.shape[0]):
                buf = chip_sum(self.pairs[(fam, li)], self.arrived[(fam, li)], self.chip, li, buf, f"grad_chip_sum_{fam}_{li}")
            mine.append(buf)
        other = half_swap(mine, "grad_half_swap")
        return [adamw_family(w, m, v, gm, go, self.core, f"adamw_{f}")
                for f, (w, m, v, gm, go) in enumerate(zip(weights, moments1, moments2, mine, other))]


class KeepGradients:
    def __init__(self):
        self.grads = {}

    def add(self, items, tag):
        for fam, li, g in items:
            self.grads[(fam, li)] = g

    def rider(self, only=None):
        return None, []

    def landed(self, keys, outs):
        pass


def kernel(x, norm_mix, w_qkv_even, w_o_even, w_qkvf_odd, b_forget, w_o_odd, norm_ffn, w_ffn_in, w_ffn_out, norm_final, loss_target, m_norm_mix, m_w_qkv_even, m_w_o_even, m_w_qkvf_odd, m_b_forget, m_w_o_odd, m_norm_ffn, m_w_ffn_in, m_w_ffn_out, m_norm_final, v_norm_mix, v_w_qkv_even, v_w_o_even, v_w_qkvf_odd, v_b_forget, v_w_o_odd, v_norm_ffn, v_w_ffn_in, v_w_ffn_out, v_norm_final):
    w_shards = [w_qkv_even, w_o_even, w_qkvf_odd, w_o_odd, w_ffn_in, w_ffn_out]
    shards = [w.astype(BF16) for w in w_shards]
    tables = _tables_for(x.shape[1])
    bias_pad = jnp.pad(b_forget, ((0, 0), (0, LANES - N_HEADS)))

    mine = {}
    for layer in range(DEPTH):
        fam_qkv, fam_o, li = layer_families(layer)
        mine.update({f"qkv{layer}": shards[fam_qkv][li], f"o{layer}": shards[fam_o][li],
                     f"fi{layer}": shards[4][layer], f"fo{layer}": shards[5][layer]})
    fetch = lambda names: gather_rider([mine[n] for n in names])
    have = dict(zip(("qkv0", "o0"), fetch(("qkv0", "o0")).run("gather_first")))
    saved, cur = [], x[0]
    h1 = rmsnorm_fwd(cur, norm_mix[0:1], "l0_norm_mix")
    for layer in range(DEPTH):
        carry, side_carry = FORWARD_CARRY[layer]
        next_gain = norm_mix[layer + 1:layer + 2] if layer + 1 < DEPTH else None
        cur, h1, keep = forward_layer(layer, cur, h1, have, norm_ffn[layer:layer + 1], next_gain, tables,
                                      bias_pad[layer // 2:layer // 2 + 1], fetch, carry, side_carry)
        saved.append(keep)

    dcur, g_final, loss_part = loss_head(cur, norm_final.reshape(1, D_MODEL), loss_target[0], "loss_head")

    exchange = GradientExchange()
    g_mix, g_ffn, g_bias = [None] * DEPTH, [None] * DEPTH, [None] * (DEPTH // 2)
    for layer in reversed(range(DEPTH)):
        dcur, g_mix[layer], g_ffn[layer], g_b = backward_layer(layer, dcur, saved[layer], norm_mix[layer:layer + 1],
                                                               norm_ffn[layer:layer + 1], tables, bias_pad[layer // 2:layer // 2 + 1], exchange)
        if g_b is not None:
            g_bias[layer // 2] = g_b

    zero_row = jnp.zeros((1, D_MODEL), F32)
    pad16 = lambda v: jnp.pad(v, (0, D_MODEL - v.shape[0]))[None, :]
    small_rows = lambda mix, ffn, fin, bias, last: jnp.concatenate(
        [r.reshape(1, D_MODEL) for r in mix] + [r.reshape(1, D_MODEL) for r in ffn] + [fin.reshape(1, D_MODEL)]
        + [pad16(b) for b in bias] + [last] + [zero_row] * (SMALL_ROWS - 12), axis=0)
    loss_row = pad16(loss_part[0, :1])
    small_g = allsum_small(small_rows(g_mix, g_ffn, g_final, g_bias, loss_row), "allsum_small")
    loss = small_g[11, 0]
    small_g = small_g.at[11].set(0.0)
    sw = small_rows(list(norm_mix), list(norm_ffn), norm_final, list(b_forget), zero_row)
    sm = small_rows(list(m_norm_mix), list(m_norm_ffn), m_norm_final, list(m_b_forget), zero_row)
    sv = small_rows(list(v_norm_mix), list(v_norm_ffn), v_norm_final, list(v_b_forget), zero_row)
    sd, snm, snv = adamw(sw, small_g, sm, sv, "adamw_small")

    def small_out(a):
        return a[0:4], a[8, :], a[9:11, :N_HEADS], a[4:8]

    big = exchange.finish(w_shards, [m_w_qkv_even, m_w_o_even, m_w_qkvf_odd, m_w_o_odd, m_w_ffn_in, m_w_ffn_out],
                          [v_w_qkv_even, v_w_o_even, v_w_qkvf_odd, v_w_o_odd, v_w_ffn_in, v_w_ffn_out])

    def outputs(small, which):
        mix, fin, bias, ffn = small_out(small)
        qkv_e, o_e, qkvf, o_o, fi, fo = [big[f][which] for f in range(6)]
        return [mix, qkv_e, o_e, qkvf, bias, o_o, ffn, fi, fo, fin]

    return (loss, dcur[None], *outputs(small_g, 0), *outputs(sd, 1), *outputs(snm, 2), *outputs(snv, 3))


def _chip_tile(rows, cols, at):
    return pl.BlockSpec((None, rows, cols), at)


def forward_layer(layer, cur, h1, have, ffn_gain, next_gain, tables, bias_row, fetch=None, carry=(), side_carry=()):
    n = f"l{layer}"
    s = cur.shape[0]
    w_qkv, w_o = have[f"qkv{layer}"], have[f"o{layer}"]
    rider = fetch(carry) if carry else None
    side_rider = fetch(side_carry) if side_carry else None
    keep = {"x": cur, "h1": h1, "w_o": w_o.reshape(D_ATTN, D_MODEL)}
    side = []
    if layer % 2 == 0:
        qkv = matmul(h1, w_qkv, "nn", BF16, n + "_qkv", 1024, 768, 1024, mnk=(s, 3 * D_ATTN, D_MODEL),
                     b_spec=_chip_tile(D_MODEL, 768, lambda i, j, kk: (j, 0, 0)))
        o_sb, st, rode = causal_fwd(qkv, 4, "sb", n + "_sb_fwd", rider=rider)
        o_dil, lse_dil, side = dilated_fwd(qkv, tables, n + "_dil_fwd", rider=side_rider)
        attn = jnp.concatenate([o_sb, o_dil], axis=1).astype(BF16)
        keep.update(o_dil=o_dil, lse_dil=lse_dil, w_qkv=w_qkv)
    else:
        natural = jnp.transpose(w_qkv, (1, 0, 2)).reshape(D_MODEL, N_CHIPS * QKVF_COLS)
        w_gate = jnp.pad(natural[:, 3 * D_ATTN:], ((0, 0), (0, LANES - N_HEADS)))
        qkv = matmul(h1, natural[:, :3 * D_ATTN], "nn", BF16, n + "_qkv", 1024, 768, 1024)
        fl = matmul(h1, w_gate, "nn", F32, n + "_fgate", 512, LANES, 1024)
        cum = forget_fwd(fl, bias_row, n + "_forget_fwd")
        f_heads = cum[:, :N_HEADS].T
        fq = jnp.broadcast_to(f_heads[:, :, None], (N_HEADS, s, LANES))
        fk = f_heads.reshape(N_HEADS // 2, 2, s)
        attn, st, rode = causal_fwd(qkv, 8, "fox", n + "_fox_fwd", fq=fq, fk=fk, rider=rider)
        attn = attn.astype(BF16)
        keep.update(fl=fl, fq=fq, fk=fk, w_qkv=jnp.concatenate([natural[:, :3 * D_ATTN], w_gate], axis=1))
    have.update(zip(carry, rode))
    have.update(zip(side_carry, side))
    w_fi, w_fo = have[f"fi{layer}"], have[f"fo{layer}"].reshape(D_FF, D_MODEL)
    mid, h2 = matmul(attn, keep["w_o"], "nn", F32, n + "_attn_out", 512, 1024, 1024, res=cur, norm_gain=ffn_gain)
    gate, up, act = ffn_in_swiglu(h2, w_fi, n + "_ffn_in")
    if next_gain is None:
        out, h_next = matmul(act, w_fo, "nn", F32, n + "_ffn_out", 512, 1024, D_FF, res=mid), None
    else:
        out, h_next = matmul(act, w_fo, "nn", F32, n + "_ffn_out", 512, 1024, D_FF, res=mid, norm_gain=next_gain)
    keep.update(qkv=qkv, st=st, attn=attn, mid=mid, h2=h2, gate=gate, up=up, act=act, w_fi=w_fi, w_fo=w_fo)
    return out, h_next, keep


def backward_layer(layer, dcur, kp, mix_gain, ffn_gain, tables, bias_row, exchange):
    n = f"l{layer}"
    s = dcur.shape[0]
    fam_qkv, fam_o, li = layer_families(layer)
    g_fo = matmul(kp["act"], dcur, "tn", BF16, n + "_d_w_ffn_out", 1408, 1024, s)
    dact = matmul(dcur, kp["w_fo"], "nt", BF16, n + "_d_act", 1024, 1408, 1024)
    dgu = swiglu_bwd(kp["gate"], kp["up"], dact, n + "_d_swiglu")
    g_fi = matmul(kp["h2"], dgu, "tn", BF16, n + "_d_w_ffn_in", 1024, 1408, 2048, mnk=(D_MODEL, 2 * D_FF, s),
                  o_spec=_chip_tile(D_MODEL, 1408, lambda i, j, kk: (j, 0, 0)), out_shape=(N_CHIPS, D_MODEL, 1408))
    dmid, g_ffn = dh_norm_bwd(dgu, kp["w_fi"], kp["mid"], ffn_gain, dcur, n + "_d_h2", 256)
    g_o = matmul(kp["attn"], dmid, "tn", BF16, n + "_d_w_o", 1024, 1024, s)
    dattn = matmul(dmid, kp["w_o"], "nt", F32, n + "_d_attn", 1024, 1024, 1024)
    exchange.add([(5, layer, g_fo.reshape(N_CHIPS, D_FF // N_CHIPS, D_MODEL)), (4, layer, g_fi),
                  (fam_o, li, g_o.reshape(N_CHIPS, D_ATTN // N_CHIPS, D_MODEL))], f"l{layer}_ffn")
    g_bias = None
    if layer % 2 == 0:
        rider, keys = exchange.rider(only=[(4, layer), (fam_o, li)])
        dq_a, dk_a, dv_a, rode = causal_bwd(kp["qkv"], dattn, kp["st"], 4, "sb", n + "_sb_bwd", rider=rider)
        exchange.landed(keys, rode)
        rider, keys = exchange.rider()
        dq_b, dk_b, dv_b, rode = dilated_bwd(kp["qkv"], tables, dattn, kp["o_dil"], kp["lse_dil"], 4, n + "_dil_bwd", rider=rider)
        dproj = jnp.concatenate([dq_a, dq_b, dk_a.astype(BF16), dk_b, dv_a.astype(BF16), dv_b], axis=1)
        g_qkv = matmul(kp["h1"], dproj, "tn", BF16, n + "_d_w_qkv", 1024, 768, 2048, mnk=(D_MODEL, 3 * D_ATTN, s),
                       o_spec=_chip_tile(D_MODEL, 768, lambda i, j, kk: (j, 0, 0)), out_shape=(N_CHIPS, D_MODEL, 768))
    else:
        rider, keys = exchange.rider()
        dq_f, dk_f, dv_f, dfk, rode = causal_bwd(kp["qkv"], dattn, kp["st"], 8, "fox", n + "_fox_bwd", fq=kp["fq"], fk=kp["fk"],
                                                 rider=rider)
        dcum = jnp.pad(dfk.reshape(N_HEADS, s).T, ((0, 0), (0, LANES - N_HEADS)))
        dfl, dbias = forget_bwd(kp["fl"], bias_row, dcum, n + "_forget_bwd")
        g_bias = dbias[0, :N_HEADS]
        dproj = jnp.concatenate([dq_f, dk_f.astype(BF16), dv_f.astype(BF16), dfl.astype(BF16)], axis=1)
        g_nat = matmul(kp["h1"], dproj, "tn", BF16, n + "_d_w_qkv", 1024, 640, 2048)
        g_qkv = g_nat[:, :N_CHIPS * QKVF_COLS].reshape(D_MODEL, N_CHIPS, QKVF_COLS)
        g_qkv = jnp.transpose(jnp.pad(g_qkv, ((0, 0), (0, 0), (0, QKVF_PAD - QKVF_COLS))), (1, 0, 2))
    exchange.landed(keys, rode)
    exchange.add([(fam_qkv, li, g_qkv)], f"l{layer}_qkv")
    dx, g_mix = dh_norm_bwd(dproj, kp["w_qkv"], kp["x"], mix_gain, dmid, n + "_d_h1", 512)
    return dx, g_mix, g_ffn, g_bias


def local_step(xs, target, norm_mix, norm_ffn, norm_final, b_forget, layer_weights):
    tables = _tables_for(xs.shape[0])
    bias_pad = jnp.pad(b_forget, ((0, 0), (0, LANES - N_HEADS)))
    saved, cur, have = [], xs, {}
    h1 = rmsnorm_fwd(cur, norm_mix[0:1], "l0_norm_mix")
    for layer in range(DEPTH):
        have.update(zip((f"qkv{layer}", f"o{layer}", f"fi{layer}", f"fo{layer}"), layer_weights[layer]))
        next_gain = norm_mix[layer + 1:layer + 2] if layer + 1 < DEPTH else None
        cur, h1, keep = forward_layer(layer, cur, h1, have, norm_ffn[layer:layer + 1], next_gain, tables,
                                      bias_pad[layer // 2:layer // 2 + 1])
        saved.append(keep)
    dcur, g_final, loss_part = loss_head(cur, norm_final.reshape(1, D_MODEL), target, "loss_head")
    keeper = KeepGradients()
    g_mix, g_ffn, g_bias = [None] * DEPTH, [None] * DEPTH, [None] * (DEPTH // 2)
    for layer in reversed(range(DEPTH)):
        dcur, g_mix[layer], g_ffn[layer], g_b = backward_layer(layer, dcur, saved[layer], norm_mix[layer:layer + 1],
                                                               norm_ffn[layer:layer + 1], tables, bias_pad[layer // 2:layer // 2 + 1], keeper)
        if g_b is not None:
            g_bias[layer // 2] = g_b
    return dcur, keeper.grads, (g_mix, g_ffn, g_final, g_bias), loss_part
```

```python
import functools

import jax
import jax.numpy as jnp
from jax import lax
from jax.experimental import pallas as pl
from jax.experimental.pallas import tpu as pltpu

F32 = jnp.float32
BF16 = jnp.bfloat16
MESH = pl.DeviceIdType.MESH

D_MODEL = 1024
DEPTH = 4
HEAD_DIM = 64
N_HEADS = 16
D_ATTN = 1024
D_FF = 2816
ROPE_THETA = 500000.0
ROT_HALF = 8
RMS_EPS = 1e-5
DIL_STRIDES = (1, 4, 16)
ADAM_LR, ADAM_B1, ADAM_B2, ADAM_EPS, ADAM_WD, ADAM_STEP = 0.001, 0.9, 0.999, 1e-8, 0.01, 10

LANES = 128
BLK = 128
VMEM_LIMIT = 56 * 1024 * 1024
NEG = -1e30
N_CHIPS = 4
SMALL_ROWS = 16


def _params(sem=None):
    return pltpu.CompilerParams(dimension_semantics=sem, vmem_limit_bytes=VMEM_LIMIT)


def _dot(a, b):
    return lax.dot_general(a, b, (((1,), (0,)), ((), ())), preferred_element_type=F32)


def _dot_nt(a, b):
    return lax.dot_general(a, b, (((1,), (1,)), ((), ())), preferred_element_type=F32)


def _dot_tn(a, b):
    return lax.dot_general(a, b, (((0,), (0,)), ((), ())), preferred_element_type=F32)


def _split3(x):
    x1 = x.astype(BF16)
    r1 = x - x1.astype(F32)
    x2 = r1.astype(BF16)
    x3 = (r1 - x2.astype(F32)).astype(BF16)
    return x1, x2, x3


def _dot_exact_lhs(x, t):
    x1, x2, x3 = _split3(x)
    return _dot(x1, t) + _dot(x2, t) + _dot(x3, t)


def _dot_exact_rhs(t, x):
    x1, x2, x3 = _split3(x)
    return _dot(t, x1) + _dot(t, x2) + _dot(t, x3)


def _iotas(shape=(BLK, LANES)):
    return lax.broadcasted_iota(jnp.int32, shape, 0), lax.broadcasted_iota(jnp.int32, shape, 1)


_DIMS = {"nn": (((1,), (0,)), ((), ())), "nt": (((1,), (1,)), ((), ())), "tn": (((0,), (0,)), ((), ()))}


def matmul(a, b, mode, out_dtype, name, tm, tn, tk, res=None, mnk=None, b_spec=None, o_spec=None, out_shape=None, norm_gain=None):
    if mnk is not None:
        m, n, k = mnk
    elif mode == "nn":
        (m, k), (k2, n) = a.shape, b.shape
    elif mode == "nt":
        (m, k), (n, k2) = a.shape, b.shape
    else:
        (k, m), (k2, n) = a.shape, b.shape
    assert m % tm == 0 and n % tn == 0 and k % tk == 0, (name, a.shape, b.shape)
    nk = k // tk
    a_spec = pl.BlockSpec((tk, tm), lambda i, j, kk: (kk, i)) if mode == "tn" else pl.BlockSpec((tm, tk), lambda i, j, kk: (i, kk))
    if b_spec is None:
        b_spec = pl.BlockSpec((tn, tk), lambda i, j, kk: (j, kk)) if mode == "nt" else pl.BlockSpec((tk, tn), lambda i, j, kk: (kk, j))
    r_spec = pl.BlockSpec((tm, tn), lambda i, j, kk: (i, j))
    if o_spec is None:
        o_spec = r_spec
    dims = _DIMS[mode]
    has_res, has_norm = res is not None, norm_gain is not None
    assert not has_norm or (tn == n and nk == 1)
    n_in = 2 + int(has_res) + int(has_norm)

    def body(*refs):
        a_ref, b_ref = refs[0], refs[1]
        r_ref = refs[2] if has_res else None
        o_ref = refs[n_in]

        def finish(v):
            if has_res:
                v = v + r_ref[...]
            o_ref[...] = v.astype(out_dtype)
            if has_norm:
                rstd = lax.rsqrt(jnp.mean(v * v, axis=-1, keepdims=True) + RMS_EPS)
                refs[n_in + 1][...] = (v * rstd * refs[n_in - 1][...]).astype(BF16)

        bv = b_ref[...]
        if bv.ndim == 3:
            bv = jnp.concatenate([bv[j] for j in range(bv.shape[0])], axis=1)
        p = lax.dot_general(a_ref[...].astype(BF16), bv.astype(BF16), dims, preferred_element_type=F32)
        if nk == 1:
            finish(p)
        else:
            acc = refs[-1]
            kk = pl.program_id(2)

            @pl.when(kk == 0)
            def _():
                acc[...] = p

            @pl.when(kk > 0)
            def _():
                acc[...] += p

            @pl.when(kk == nk - 1)
            def _():
                finish(acc[...])

    ops = [a, b] + ([res] if has_res else []) + ([norm_gain] if has_norm else [])
    specs = [a_spec, b_spec] + ([r_spec] if has_res else []) + ([pl.BlockSpec((1, tn), lambda i, j, kk: (0, j))] if has_norm else [])
    out_shape = jax.ShapeDtypeStruct((m, n) if out_shape is None else out_shape, out_dtype)
    return pl.pallas_call(
        body, name=name, out_shape=(out_shape, jax.ShapeDtypeStruct((m, n), BF16)) if has_norm else out_shape,
        grid=(m // tm, n // tn, nk), in_specs=specs, out_specs=(o_spec, r_spec) if has_norm else o_spec,
        scratch_shapes=[pltpu.VMEM((tm, tn), F32)] if nk > 1 else [],
        compiler_params=_params(("parallel", "parallel", "arbitrary")),
    )(*ops)


ROWS = 256


def _row_spec(cols, rows=ROWS):
    return pl.BlockSpec((rows, cols), lambda i: (i, 0))


def _fix_spec(r, cols):
    return pl.BlockSpec((r, cols), lambda i: (0, 0))


def rmsnorm_fwd(x, g, name):
    s, d = x.shape

    def body(x_ref, g_ref, h_ref):
        xv = x_ref[...]
        rstd = lax.rsqrt(jnp.mean(xv * xv, axis=-1, keepdims=True) + RMS_EPS)
        h_ref[...] = (xv * rstd * g_ref[...]).astype(BF16)

    return pl.pallas_call(
        body, name=name, out_shape=jax.ShapeDtypeStruct((s, d), BF16), grid=(s // ROWS,),
        in_specs=[_row_spec(d), _fix_spec(1, d)], out_specs=_row_spec(d), compiler_params=_params(("parallel",)),
    )(x, g)


def _rms_bwd_math(xv, gv, dh):
    rstd = lax.rsqrt(jnp.mean(xv * xv, axis=-1, keepdims=True) + RMS_EPS)
    xhat = xv * rstd
    u = dh * gv
    dx = rstd * (u - xhat * jnp.mean(u * xhat, axis=-1, keepdims=True))
    return dx, dh * xhat


def dh_norm_bwd(dy, w, x, g, dres, name, tm):
    s, k = dy.shape
    d = x.shape[1]

    def body(dy_ref, w_ref, x_ref, g_ref, dres_ref, dx_ref, dg_ref):
        wv = w_ref[...]
        if wv.ndim == 3:
            wv = jnp.concatenate([wv[j] for j in range(wv.shape[0])], axis=1)
        dx, dgt = _rms_bwd_math(x_ref[...], g_ref[...], _dot_nt(dy_ref[...], wv))
        dx_ref[...] = dres_ref[...] + dx
        part = jnp.sum(dgt, axis=0, keepdims=True)

        @pl.when(pl.program_id(0) == 0)
        def _():
            dg_ref[...] = part

        @pl.when(pl.program_id(0) > 0)
        def _():
            dg_ref[...] += part

    w_spec = pl.BlockSpec(w.shape, lambda i: (0,) * w.ndim)
    return pl.pallas_call(
        body, name=name, out_shape=(jax.ShapeDtypeStruct((s, d), F32), jax.ShapeDtypeStruct((1, d), F32)), grid=(s // tm,),
        in_specs=[pl.BlockSpec((tm, k), lambda i: (i, 0)), w_spec, _row_spec(d, tm), _fix_spec(1, d), _row_spec(d, tm)],
        out_specs=(_row_spec(d, tm), _fix_spec(1, d)), compiler_params=_params(("arbitrary",)),
    )(dy, w, x, g, dres)


def loss_head(x, g, target, name):
    s, d = x.shape

    def body(x_ref, g_ref, t_ref, dx_ref, dg_ref, loss_ref):
        xv, gv = x_ref[...], g_ref[...]
        rstd = lax.rsqrt(jnp.mean(xv * xv, axis=-1, keepdims=True) + RMS_EPS)
        err = xv * rstd * gv - t_ref[...]
        dx, dgt = _rms_bwd_math(xv, gv, err * (1.0 / d))
        dx_ref[...] = dx
        part = jnp.sum(dgt, axis=0, keepdims=True)
        lpart = jnp.full((1, LANES), 0.5 / d, F32) * jnp.sum(err * err)

        @pl.when(pl.program_id(0) == 0)
        def _():
            dg_ref[...] = part
            loss_ref[...] = lpart

        @pl.when(pl.program_id(0) > 0)
        def _():
            dg_ref[...] += part
            loss_ref[...] += lpart

    return pl.pallas_call(
        body, name=name,
        out_shape=(jax.ShapeDtypeStruct((s, d), F32), jax.ShapeDtypeStruct((1, d), F32), jax.ShapeDtypeStruct((1, LANES), F32)),
        grid=(s // ROWS,), in_specs=[_row_spec(d), _fix_spec(1, d), _row_spec(d)],
        out_specs=(_row_spec(d), _fix_spec(1, d), _fix_spec(1, LANES)), compiler_params=_params(("arbitrary",)),
    )(x, g, target)


def ffn_in_swiglu(h, w_in, name, tm=1024):
    s, d = h.shape
    cols = w_in.shape[2]

    def body(h_ref, wg_ref, wu_ref, g_ref, u_ref, a_ref):
        hv = h_ref[...]
        gv, uv = _dot(hv, wg_ref[...]), _dot(hv, wu_ref[...])
        g_ref[...] = gv.astype(BF16)
        u_ref[...] = uv.astype(BF16)
        a_ref[...] = (gv * (1.0 / (1.0 + jnp.exp(-gv))) * uv).astype(BF16)

    tile = pl.BlockSpec((tm, cols), lambda i, j: (i, j))
    out = jax.ShapeDtypeStruct((s, 2 * cols), BF16)
    return pl.pallas_call(
        body, name=name, out_shape=(out, out, out), grid=(s // tm, 2),
        in_specs=[pl.BlockSpec((tm, d), lambda i, j: (i, 0)), pl.BlockSpec((None, d, cols), lambda i, j: (j, 0, 0)),
                  pl.BlockSpec((None, d, cols), lambda i, j: (j + 2, 0, 0))],
        out_specs=(tile, tile, tile), compiler_params=_params(("parallel", "parallel")),
    )(h, w_in, w_in)


def swiglu_bwd(gate, up, dact, name):
    s, f = gate.shape

    def body(g_ref, u_ref, da_ref, o_ref):
        gv, uv, da = g_ref[...].astype(F32), u_ref[...].astype(F32), da_ref[...].astype(F32)
        sg = 1.0 / (1.0 + jnp.exp(-gv))
        o_ref[:, :f] = (da * uv * sg * (1.0 + gv * (1.0 - sg))).astype(BF16)
        o_ref[:, f:] = (da * gv * sg).astype(BF16)

    return pl.pallas_call(
        body, name=name, out_shape=jax.ShapeDtypeStruct((s, 2 * f), BF16), grid=(s // ROWS,),
        in_specs=[_row_spec(f)] * 3, out_specs=_row_spec(2 * f), compiler_params=_params(("parallel",)),
    )(gate, up, dact)


Q_OFF, K_OFF, V_OFF = 0, 8, 16


KB = 512
BQ = 512
SUB = KB // BLK


def _softplus_parts(z):
    sp = jnp.log(1.0 + jnp.exp(-jnp.abs(z)))
    ls = jnp.minimum(z, 0.0) - sp
    return ls, ls - z


def _wide(t):
    return jnp.concatenate([t] * SUB, axis=1)


def _chunk_dots(x, tri):
    terms = []
    for u in range(SUB):
        xu = x[:, u * BLK:(u + 1) * BLK]
        hi = xu.astype(BF16)
        terms += [hi, (xu - hi.astype(F32)).astype(BF16)]
    r = _dot(jnp.concatenate(terms, axis=0), tri)
    rows = x.shape[0]
    piece = lambda n: r[n * rows:(n + 1) * rows]
    return [piece(2 * u) + piece(2 * u + 1) for u in range(SUB)]


def _block_suffix_sums(x, suffix, c):
    loc = _chunk_dots(x, suffix)
    out = [None] * SUB
    for u in reversed(range(SUB)):
        out[u] = loc[u] + c
        c = c + jnp.sum(x[:, u * BLK:(u + 1) * BLK], axis=1, keepdims=True)
    return jnp.concatenate(out, axis=1), c


def _block_prefix_sums(x, tri, c):
    loc = _chunk_dots(x, tri)
    out = []
    for u in range(SUB):
        out.append(loc[u] + c)
        c = c + jnp.sum(x[:, u * BLK:(u + 1) * BLK], axis=1, keepdims=True)
    return jnp.concatenate(out, axis=1), c


def causal_fwd(qkv, npairs, mode, name, fq=None, fk=None, rider=None):
    s = qkv.shape[0]
    nq = s // BQ
    fox = mode == "fox"

    def body(*refs):
        if fox:
            q_ref, k_ref, v_ref, fq_ref, fk_ref, o_ref, st_ref = refs
        else:
            q_ref, k_ref, v_ref, o_ref, st_ref = refs
        i = pl.program_id(1)
        nkb = (i * BQ + BQ - 1) // KB + 1
        row, lane = _iotas((BQ, KB))
        row_s, lane_s = _iotas()
        _, lane_q = _iotas((BQ, LANES))
        nfull = (i * BQ) // KB
        qpos = i * BQ + row
        qf = q_ref[...].astype(F32) * 0.125
        hms = (lane_q < HEAD_DIM, lane_q >= HEAD_DIM)
        qas = [jnp.where(hm, qf, 0.0).astype(BF16) for hm in hms]
        suffix = jnp.where(row_s > lane_s, 1.0, 0.0).astype(BF16)
        zero = jnp.zeros((BQ, LANES), F32)
        col0 = jnp.zeros((BQ, 1), F32)

        def kv(j):
            r0 = pl.multiple_of(j * KB, KB)
            return r0, k_ref[pl.ds(r0, KB), :], v_ref[pl.ds(r0, KB), :]

        if fox:
            fqs = [_wide(fq_ref[a]) for a in range(2)]

            def step(j, carry, masked):
                r0, kb, vb = kv(j)
                new = []
                for a in range(2):
                    acc, mx, l = carry[3 * a:3 * a + 3]
                    z = _dot_nt(qas[a], kb) + fqs[a] - fk_ref[a:a + 1, pl.ds(r0, KB)]
                    if masked:
                        z = jnp.where(r0 + lane <= qpos, z, NEG)
                    mnew = jnp.maximum(mx, jnp.max(z, axis=1, keepdims=True))
                    p = jnp.exp(z - mnew)
                    alpha = jnp.exp(mx - mnew)
                    new += [alpha * acc + _dot(p.astype(BF16), vb), mnew, alpha * l + jnp.sum(p, axis=1, keepdims=True)]
                return tuple(new)

            neg = jnp.full((BQ, 1), NEG, F32)
            res = lax.fori_loop(0, nfull, functools.partial(step, masked=False), (zero, neg, col0, zero, neg, col0))
            res = lax.fori_loop(nfull, nkb, functools.partial(step, masked=True), res)
            outs = [res[3 * a] / res[3 * a + 2] for a in range(2)]
            stats = [res[3 * a + 1] + jnp.log(res[3 * a + 2]) for a in range(2)]
        else:
            def step(j, carry, masked):
                r0, kb, vb = kv(j)
                strict = r0 + lane < qpos
                new = []
                for a in range(2):
                    acc, c = carry[2 * a:2 * a + 2]
                    ls, lm = _softplus_parts(_dot_nt(qas[a], kb))
                    if masked:
                        lm = jnp.where(strict, lm, 0.0)
                    between, c = _block_suffix_sums(lm, suffix, c)
                    aw = jnp.exp(ls + between)
                    if masked:
                        aw = jnp.where(strict, aw, 0.0)
                    new += [acc + _dot(aw.astype(BF16), vb), c]
                return tuple(new)

            res = lax.fori_loop(0, nkb - nfull, lambda jj, c: step(nkb - 1 - jj, c, True), (zero, col0, zero, col0))
            res = lax.fori_loop(0, nfull, lambda jj, c: step(nfull - 1 - jj, c, False), res)
            outs, stats = [res[0], res[2]], [res[1], res[3]]
        o_ref[...] = jnp.where(hms[0], outs[0], outs[1])
        for a in range(2):
            st_ref[a] = jnp.broadcast_to(stats[a], (BQ, LANES))

    col = lambda off: (lambda p, i: (0, off + p))
    in_specs = [pl.BlockSpec((BQ, LANES), lambda p, i: (i, Q_OFF + p)),
                pl.BlockSpec((s, LANES), col(K_OFF)), pl.BlockSpec((s, LANES), col(V_OFF))]
    ops = [qkv, qkv, qkv]
    if fox:
        in_specs += [pl.BlockSpec((2, BQ, LANES), lambda p, i: (p, i, 0)), pl.BlockSpec((None, 2, s), lambda p, i: (p, 0, 0))]
        ops += [fq, fk]
    (o, stat), rode = call_with_rider(
        body, name, rider, ops, in_specs,
        [jax.ShapeDtypeStruct((s, npairs * LANES), F32), jax.ShapeDtypeStruct((2 * npairs, s, LANES), F32)],
        [pl.BlockSpec((BQ, LANES), lambda p, i: (i, p)), pl.BlockSpec((2, BQ, LANES), lambda p, i: (p, i, 0))], [], (npairs, nq))
    return o, stat, rode


def causal_bwd(qkv, do, stat, npairs, mode, name, fq=None, fk=None, rider=None):
    s = qkv.shape[0]
    nq = s // BQ
    fox = mode == "fox"

    def body(*refs):
        if fox:
            q_ref, k_ref, v_ref, do_ref, st_ref, fq_ref, fk_ref, dq_ref, dk_ref, dv_ref, df_ref, p_s, dp_s = refs
        else:
            q_ref, k_ref, v_ref, do_ref, st_ref, dq_ref, dk_ref, dv_ref = refs
        i = pl.program_id(1)

        @pl.when(i == 0)
        def _():
            dk_ref[...] = jnp.zeros_like(dk_ref)
            dv_ref[...] = jnp.zeros_like(dv_ref)
            if fox:
                df_ref[...] = jnp.zeros_like(df_ref)

        nkb = (i * BQ + BQ - 1) // KB + 1
        nfull = (i * BQ) // KB
        row, lane = _iotas((BQ, KB))
        row_s, lane_s = _iotas()
        _, lane_q = _iotas((BQ, LANES))
        qpos = i * BQ + row
        qf = q_ref[...].astype(F32) * 0.125
        dov = do_ref[...]
        hms = (lane_q < HEAD_DIM, lane_q >= HEAD_DIM)
        qas = [jnp.where(hm, qf, 0.0).astype(BF16) for hm in hms]
        doas = [jnp.where(hm, dov, 0.0).astype(BF16) for hm in hms]
        stas = [_wide(st_ref[a]) for a in range(2)]
        zero = jnp.zeros((BQ, LANES), F32)
        col0 = jnp.zeros((BQ, 1), F32)

        def kv(j):
            r0 = pl.multiple_of(j * KB, KB)
            return r0, k_ref[pl.ds(r0, KB), :], v_ref[pl.ds(r0, KB), :]

        if fox:
            fqs = [_wide(fq_ref[a]) for a in range(2)]

            def probs(j, deltas, masked):
                r0, kb, vb = kv(j)
                new = []
                for a in range(2):
                    z = _dot_nt(qas[a], kb) + fqs[a] - fk_ref[a:a + 1, pl.ds(r0, KB)]
                    p = jnp.exp(z - stas[a])
                    if masked:
                        p = jnp.where(r0 + lane <= qpos, p, 0.0)
                    dp = _dot_nt(doas[a], vb)
                    p_s[a, j] = p
                    dp_s[a, j] = dp
                    new.append(deltas[a] + jnp.sum(p * dp, axis=1, keepdims=True))
                return tuple(new)

            deltas = lax.fori_loop(0, nfull, functools.partial(probs, masked=False), (col0, col0))
            deltas = lax.fori_loop(nfull, nkb, functools.partial(probs, masked=True), deltas)

            def step(j, dqs):
                r0, kb, _ = kv(j)
                new = []
                dk = jnp.zeros((KB, LANES), F32)
                dv = jnp.zeros((KB, LANES), F32)
                for a in range(2):
                    p = p_s[a, j]
                    ds = p * (dp_s[a, j] - deltas[a])
                    dsb = ds.astype(BF16)
                    dk += _dot_tn(dsb, qas[a])
                    dv += _dot_tn(p.astype(BF16), doas[a])
                    df_ref[a:a + 1, pl.ds(r0, KB)] -= jnp.sum(ds, axis=0, keepdims=True)
                    new.append(dqs[a] + _dot(dsb, kb))
                dk_ref[pl.ds(r0, KB), :] += dk
                dv_ref[pl.ds(r0, KB), :] += dv
                return tuple(new)

            dqs = lax.fori_loop(0, nkb, step, (zero, zero))
        else:
            incl = jnp.where(row_s <= lane_s, 1.0, 0.0).astype(BF16)
            excl = jnp.where(row_s < lane_s, 1.0, 0.0).astype(BF16)

            def step(j, carry, masked):
                r0, kb, vb = kv(j)
                strict = r0 + lane < qpos
                new = []
                dk = jnp.zeros((KB, LANES), F32)
                dv = jnp.zeros((KB, LANES), F32)
                for a in range(2):
                    dq, cm, cg = carry[3 * a:3 * a + 3]
                    ls, lm = _softplus_parts(_dot_nt(qas[a], kb))
                    if masked:
                        lm = jnp.where(strict, lm, 0.0)
                    beta = jnp.exp(ls)
                    upto, cm = _block_prefix_sums(lm, incl, cm)
                    aw = jnp.exp(ls + stas[a] - upto)
                    if masked:
                        aw = jnp.where(strict, aw, 0.0)
                    g = aw * _dot_nt(doas[a], vb)
                    pre, cg = _block_prefix_sums(g, excl, cg)
                    dz = g * (1.0 - beta) - pre * beta
                    if masked:
                        dz = jnp.where(strict, dz, 0.0)
                    dzb = dz.astype(BF16)
                    dk += _dot_tn(dzb, qas[a])
                    dv += _dot_tn(aw.astype(BF16), doas[a])
                    new += [dq + _dot(dzb, kb), cm, cg]
                dk_ref[pl.ds(r0, KB), :] += dk
                dv_ref[pl.ds(r0, KB), :] += dv
                return tuple(new)

            res = lax.fori_loop(0, nfull, functools.partial(step, masked=False), (zero, col0, col0, zero, col0, col0))
            res = lax.fori_loop(nfull, nkb, functools.partial(step, masked=True), res)
            dqs = (res[0], res[3])
        dq_ref[...] = (jnp.where(hms[0], dqs[0], dqs[1]) * 0.125).astype(BF16)

    col = lambda off: (lambda p, i: (0, off + p))
    blk = pl.BlockSpec((BQ, LANES), lambda p, i: (i, p))
    acc = pl.BlockSpec((s, LANES), lambda p, i: (0, p))
    st_spec = pl.BlockSpec((2, BQ, LANES), lambda p, i: (p, i, 0))
    in_specs = [pl.BlockSpec((BQ, LANES), lambda p, i: (i, Q_OFF + p)), pl.BlockSpec((s, LANES), col(K_OFF)),
                pl.BlockSpec((s, LANES), col(V_OFF)), blk, st_spec]
    ops = [qkv, qkv, qkv, do, stat]
    w = npairs * LANES
    out_shape = [jax.ShapeDtypeStruct((s, w), BF16), jax.ShapeDtypeStruct((s, w), F32), jax.ShapeDtypeStruct((s, w), F32)]
    out_specs = [blk, acc, acc]
    scratch = []
    if fox:
        fk_spec = pl.BlockSpec((None, 2, s), lambda p, i: (p, 0, 0))
        in_specs += [st_spec, fk_spec]
        ops += [fq, fk]
        out_shape.append(jax.ShapeDtypeStruct((npairs, 2, s), F32))
        out_specs.append(fk_spec)
        scratch = [pltpu.VMEM((2, s // KB, BQ, KB), F32)] * 2
    outs, rode = call_with_rider(body, name, rider, ops, in_specs, out_shape, out_specs, scratch, (npairs, nq))
    return (*outs, rode)


def forget_fwd(fl, bias, name):
    s = fl.shape[0]

    def body(fl_ref, b_ref, f_ref):
        row, lane = _iotas()
        lower = jnp.where(lane <= row, 1.0, 0.0).astype(BF16)

        def step(n, carry):
            r0 = pl.multiple_of(n * BLK, BLK)
            ls, _ = _softplus_parts(fl_ref[pl.ds(r0, BLK), :] + b_ref[...])
            blk = _dot_exact_rhs(lower, ls) + carry
            f_ref[pl.ds(r0, BLK), :] = blk
            return blk[BLK - 1:BLK, :]

        lax.fori_loop(0, s // BLK, step, jnp.zeros((1, LANES), F32))

    return pl.pallas_call(
        body, name=name, out_shape=jax.ShapeDtypeStruct((s, LANES), F32),
        in_specs=[pl.BlockSpec(memory_space=pltpu.VMEM)] * 2, out_specs=pl.BlockSpec(memory_space=pltpu.VMEM),
        compiler_params=_params(),
    )(fl, bias)


def forget_bwd(fl, bias, df, name):
    s = fl.shape[0]
    nb = s // BLK

    def body(fl_ref, b_ref, df_ref, o_ref, db_ref):
        row, lane = _iotas()
        upper = jnp.where(lane >= row, 1.0, 0.0).astype(BF16)

        def step(nn, carry):
            tail, db = carry
            r0 = pl.multiple_of((nb - 1 - nn) * BLK, BLK)
            dls = _dot_exact_rhs(upper, df_ref[pl.ds(r0, BLK), :]) + tail
            xv = fl_ref[pl.ds(r0, BLK), :] + b_ref[...]
            dfl = dls * (1.0 / (1.0 + jnp.exp(xv)))
            o_ref[pl.ds(r0, BLK), :] = dfl
            return dls[0:1, :], db + jnp.sum(dfl, axis=0, keepdims=True)

        _, db = lax.fori_loop(0, nb, step, (jnp.zeros((1, LANES), F32), jnp.zeros((1, LANES), F32)))
        db_ref[...] = db

    return pl.pallas_call(
        body, name=name, out_shape=(jax.ShapeDtypeStruct((s, LANES), F32), jax.ShapeDtypeStruct((1, LANES), F32)),
        in_specs=[pl.BlockSpec(memory_space=pltpu.VMEM)] * 3,
        out_specs=(pl.BlockSpec(memory_space=pltpu.VMEM), pl.BlockSpec(memory_space=pltpu.VMEM)),
        compiler_params=_params(),
    )(fl, bias, df)


def _rot_tables(s):
    inv = ROPE_THETA ** (-jnp.arange(ROT_HALF, dtype=F32) * 2.0 / (2 * ROT_HALF))
    ang = jnp.arange(s, dtype=F32)[:, None] * inv[None, :]
    cos, sin = jnp.cos(ang), jnp.sin(ang)
    z8 = jnp.zeros((s, ROT_HALF), F32)
    rest = HEAD_DIM - 2 * ROT_HALF
    zr, onr = jnp.zeros((s, rest), F32), jnp.ones((s, rest), F32)
    tile = lambda t: jnp.tile(t, (1, 2))
    return tile(jnp.concatenate([cos, cos, onr], 1)), tile(jnp.concatenate([-sin, z8, zr], 1)), tile(jnp.concatenate([z8, sin, zr], 1))


def _deinterleave(dst, src_ref, stride, s, dtype):
    length = s // stride
    for r in range(stride):
        if stride == 1:
            dst[...] = src_ref[...].astype(dtype)
        else:
            dst[r * length:(r + 1) * length, :] = src_ref[pl.ds(r, length, stride=stride), :].astype(dtype)


def _band_masks(row, lane, first):
    return lane <= row, lane >= row + jnp.where(first, BLK, 0)


N_DIL_PAIRS = 4


def _rotate_into(q_ref, k_ref, v_ref, c_ref, s1_ref, s2_ref, qr, kr, vr):
    c, s1, s2 = c_ref[...], s1_ref[...], s2_ref[...]
    rot = lambda xv: xv * c + pltpu.roll(xv, LANES - ROT_HALF, 1) * s1 + pltpu.roll(xv, ROT_HALF, 1) * s2
    qr[...] = rot(q_ref[...].astype(F32)) * 0.125
    kr[...] = rot(k_ref[...].astype(F32))
    vr[...] = v_ref[...].astype(F32)


def _dilated_operands(qkv, tables):
    s = qkv.shape[0]
    col = lambda off: pl.BlockSpec((s, LANES), lambda p: (0, off + N_DIL_PAIRS + p))
    table = pl.BlockSpec((s, LANES), lambda p: (0, 0))
    return [qkv, qkv, qkv, *tables], [col(Q_OFF), col(K_OFF), col(V_OFF), table, table, table]


def dilated_fwd(qkv, tables, name, rider=None):
    s = qkv.shape[0]
    npairs, w = N_DIL_PAIRS, N_DIL_PAIRS * LANES
    nblk = s // BLK

    def body(q_in, k_in, v_in, c_ref, s1_ref, s2_ref, o_ref, lse_ref, q_ref, k_ref, v_ref, qs, ks, vs, od, ld, on, ln):
        row, lane = _iotas()
        _rotate_into(q_in, k_in, v_in, c_ref, s1_ref, s2_ref, q_ref, k_ref, v_ref)
        for pi, stride in enumerate(DIL_STRIDES):
            per = (s // stride) // BLK
            _deinterleave(qs, q_ref, stride, s, BF16)
            _deinterleave(ks, k_ref, stride, s, BF16)
            _deinterleave(vs, v_ref, stride, s, BF16)

            def block(b, carry):
                r0 = pl.multiple_of(b * BLK, BLK)
                rp = pl.multiple_of(jnp.maximum(b - 1, 0) * BLK, BLK)
                mc, mp = _band_masks(row, lane, b % per == 0)
                q = qs[pl.ds(r0, BLK), :]
                kc, kp, vc, vp = ks[pl.ds(r0, BLK), :], ks[pl.ds(rp, BLK), :], vs[pl.ds(r0, BLK), :], vs[pl.ds(rp, BLK), :]
                out = jnp.zeros((BLK, LANES), F32)
                lse = jnp.zeros((BLK, LANES), F32)
                for a in range(2):
                    hm = (lane < HEAD_DIM) if a == 0 else (lane >= HEAD_DIM)
                    qa = jnp.where(hm, q.astype(F32), 0.0).astype(BF16)
                    sc = jnp.where(mc, _dot_nt(qa, kc), NEG)
                    sp = jnp.where(mp, _dot_nt(qa, kp), NEG)
                    mx = jnp.maximum(jnp.max(sc, axis=1, keepdims=True), jnp.max(sp, axis=1, keepdims=True))
                    pc, pp = jnp.exp(sc - mx), jnp.exp(sp - mx)
                    l = jnp.sum(pc, axis=1, keepdims=True) + jnp.sum(pp, axis=1, keepdims=True)
                    oa = (_dot(pc.astype(BF16), vc) + _dot(pp.astype(BF16), vp)) / l
                    out = jnp.where(hm, oa, out)
                    lse = jnp.where(hm, mx + jnp.log(l), lse)
                od[pl.ds(r0, BLK), :] = out
                ld[pl.ds(r0, BLK), :] = lse
                return carry

            lax.fori_loop(0, nblk, block, 0, unroll=2)
            length = s // stride
            for r in range(stride):
                if stride == 1:
                    on[pi] = od[...]
                    ln[pi] = ld[...]
                else:
                    on[pi, pl.ds(r, length, stride=stride), :] = od[r * length:(r + 1) * length, :]
                    ln[pi, pl.ds(r, length, stride=stride), :] = ld[r * length:(r + 1) * length, :]

        def merge(n, carry):
            r0 = pl.multiple_of(n * BLK, BLK)
            ls = [ln[pi, pl.ds(r0, BLK), :] for pi in range(3)]
            mx = jnp.maximum(jnp.maximum(ls[0], ls[1]), ls[2])
            ws = [jnp.exp(lv - mx) for lv in ls]
            den = ws[0] + ws[1] + ws[2]
            num = ws[0] * on[0, pl.ds(r0, BLK), :] + ws[1] * on[1, pl.ds(r0, BLK), :] + ws[2] * on[2, pl.ds(r0, BLK), :]
            o_ref[pl.ds(r0, BLK), :] = num / den
            lse_ref[pl.ds(r0, BLK), :] = mx + jnp.log(den)
            return carry

        lax.fori_loop(0, nblk, merge, 0, unroll=2)

    colspec = pl.BlockSpec((s, LANES), lambda p: (0, p))
    out = jax.ShapeDtypeStruct((s, w), F32)
    scratch = ([pltpu.VMEM((s, LANES), F32)] * 3 + [pltpu.VMEM((s, LANES), BF16)] * 3 + [pltpu.VMEM((s, LANES), F32)] * 2
               + [pltpu.VMEM((3, s, LANES), F32)] * 2)
    ops, in_specs = _dilated_operands(qkv, tables)
    (o, lse), rode = call_with_rider(body, name, rider, ops, in_specs, [out, out], [colspec, colspec], scratch, (npairs,))
    return o, lse, rode


def dilated_bwd(qkv, tables, do, out, lse, do_off, name, rider=None):
    s = qkv.shape[0]
    npairs, w = N_DIL_PAIRS, N_DIL_PAIRS * LANES
    nblk = s // BLK

    def body(q_in, k_in, v_in, c_ref, s1_ref, s2_ref, do_ref, out_ref, lse_ref, dq_out, dk_out, dv_out,
             q_ref, k_ref, v_ref, dq_ref, dk_ref, dv_ref, qs, ks, vs, dos, dls, lss, dqd, dkd, dvd, dln):
        row, lane = _iotas()
        same_head = jnp.where((row < HEAD_DIM) == (lane < HEAD_DIM), 1.0, 0.0).astype(BF16)
        _rotate_into(q_in, k_in, v_in, c_ref, s1_ref, s2_ref, q_ref, k_ref, v_ref)

        def delta_blk(n, carry):
            r0 = pl.multiple_of(n * BLK, BLK)
            dln[pl.ds(r0, BLK), :] = _dot_exact_lhs(do_ref[pl.ds(r0, BLK), :] * out_ref[pl.ds(r0, BLK), :], same_head)
            return carry

        lax.fori_loop(0, nblk, delta_blk, 0, unroll=2)
        for pi, stride in enumerate(DIL_STRIDES):
            per = (s // stride) // BLK
            _deinterleave(qs, q_ref, stride, s, BF16)
            _deinterleave(ks, k_ref, stride, s, BF16)
            _deinterleave(vs, v_ref, stride, s, BF16)
            _deinterleave(dos, do_ref, stride, s, BF16)
            _deinterleave(dls, dln, stride, s, F32)
            _deinterleave(lss, lse_ref, stride, s, F32)

            def block(b, carry):
                r0 = pl.multiple_of(b * BLK, BLK)
                rp = pl.multiple_of(jnp.maximum(b - 1, 0) * BLK, BLK)
                first = b % per == 0
                mc, mp = _band_masks(row, lane, first)
                q, dov = qs[pl.ds(r0, BLK), :], dos[pl.ds(r0, BLK), :]
                kc, kp, vc, vp = ks[pl.ds(r0, BLK), :], ks[pl.ds(rp, BLK), :], vs[pl.ds(r0, BLK), :], vs[pl.ds(rp, BLK), :]
                lse_t, dl_t = lss[pl.ds(r0, BLK), :], dls[pl.ds(r0, BLK), :]
                dq = jnp.zeros((BLK, LANES), F32)
                dkc = jnp.zeros((BLK, LANES), F32)
                dkp = jnp.zeros((BLK, LANES), F32)
                dvc = jnp.zeros((BLK, LANES), F32)
                dvp = jnp.zeros((BLK, LANES), F32)
                for a in range(2):
                    hm = (lane < HEAD_DIM) if a == 0 else (lane >= HEAD_DIM)
                    pick = lane == a * HEAD_DIM
                    qa = jnp.where(hm, q.astype(F32), 0.0).astype(BF16)
                    doa = jnp.where(hm, dov.astype(F32), 0.0).astype(BF16)
                    lse_a = jnp.sum(jnp.where(pick, lse_t, 0.0), axis=1, keepdims=True)
                    dl_a = jnp.sum(jnp.where(pick, dl_t, 0.0), axis=1, keepdims=True)
                    pc = jnp.where(mc, jnp.exp(_dot_nt(qa, kc) - lse_a), 0.0)
                    pp = jnp.where(mp, jnp.exp(_dot_nt(qa, kp) - lse_a), 0.0)
                    dsc = (pc * (_dot_nt(doa, vc) - dl_a)).astype(BF16)
                    dsp = (pp * (_dot_nt(doa, vp) - dl_a)).astype(BF16)
                    dq = jnp.where(hm, _dot(dsc, kc) + _dot(dsp, kp), dq)
                    dkc += _dot_tn(dsc, qa)
                    dkp += _dot_tn(dsp, qa)
                    dvc += _dot_tn(pc.astype(BF16), doa)
                    dvp += _dot_tn(pp.astype(BF16), doa)
                dqd[pl.ds(r0, BLK), :] = dq
                dkd[pl.ds(r0, BLK), :] = dkc
                dvd[pl.ds(r0, BLK), :] = dvc

                @pl.when(jnp.logical_not(first))
                def _():
                    dkd[pl.ds(rp, BLK), :] += dkp
                    dvd[pl.ds(rp, BLK), :] += dvp

                return carry

            lax.fori_loop(0, nblk, block, 0, unroll=2)
            length = s // stride
            for dst, src in ((dq_ref, dqd), (dk_ref, dkd), (dv_ref, dvd)):
                for r in range(stride):
                    if stride == 1:
                        dst[...] = src[...]
                    else:
                        dst[pl.ds(r, length, stride=stride), :] += src[r * length:(r + 1) * length, :]

        c, s1, s2 = c_ref[...], s1_ref[...], s2_ref[...]
        rot_t = lambda dy: dy * c + pltpu.roll(dy * s1, ROT_HALF, 1) + pltpu.roll(dy * s2, LANES - ROT_HALF, 1)
        dq_out[...] = (rot_t(dq_ref[...]) * 0.125).astype(BF16)
        dk_out[...] = rot_t(dk_ref[...]).astype(BF16)
        dv_out[...] = dv_ref[...].astype(BF16)

    colspec = pl.BlockSpec((s, LANES), lambda p: (0, p))
    do_spec = pl.BlockSpec((s, LANES), lambda p: (0, do_off + p))
    o3 = jax.ShapeDtypeStruct((s, w), BF16)
    scratch = [pltpu.VMEM((s, LANES), F32)] * 6 + [pltpu.VMEM((s, LANES), BF16)] * 4 + [pltpu.VMEM((s, LANES), F32)] * 6
    ops, in_specs = _dilated_operands(qkv, tables)
    outs, rode = call_with_rider(body, name, rider, ops + [do, out, lse], in_specs + [do_spec, colspec, colspec],
                                 [o3, o3, o3], [colspec, colspec, colspec], scratch, (npairs,))
    return (*outs, rode)


def adamw(w, g, m, v, name):
    rows, cols = w.shape
    rb = min(rows, ROWS)
    c1 = 1.0 - ADAM_B1 ** ADAM_STEP
    c2 = 1.0 - ADAM_B2 ** ADAM_STEP

    def body(w_ref, g_ref, m_ref, v_ref, d_ref, mo_ref, vo_ref):
        gv = g_ref[...]
        mn = ADAM_B1 * m_ref[...] + (1.0 - ADAM_B1) * gv
        vn = ADAM_B2 * v_ref[...] + (1.0 - ADAM_B2) * (gv * gv)
        d_ref[...] = -ADAM_LR * ((mn / c1) / (jnp.sqrt(vn / c2) + ADAM_EPS) + ADAM_WD * w_ref[...])
        mo_ref[...] = mn
        vo_ref[...] = vn

    spec = _row_spec(cols, rb)
    out = jax.ShapeDtypeStruct((rows, cols), F32)
    return pl.pallas_call(
        body, name=name, out_shape=(out, out, out), grid=(rows // rb,), in_specs=[spec] * 4, out_specs=(spec,) * 3,
        compiler_params=_params(("parallel",)),
    )(w, g, m, v)


def _prefetch_call(body, name, scalar, ops, grid, in_specs, out_specs, out_shape, sem):
    spec = pltpu.PrefetchScalarGridSpec(num_scalar_prefetch=1, grid=grid, in_specs=in_specs, out_specs=out_specs)
    return pl.pallas_call(body, name=name, grid_spec=spec, out_shape=out_shape, compiler_params=_params(sem))(scalar, *ops)


def pair_sums(gs, gots, core, name):
    n = len(gs)

    def body(core_ref, *refs):
        for f in range(n):
            refs[2 * n + f][...] = (refs[f][...].astype(F32) + refs[n + f][...].astype(F32)).astype(BF16)

    blk = lambda g, rows_of: pl.BlockSpec((None, g.shape[1] // 2, g.shape[2]), rows_of)
    mine = [blk(g, lambda j, core_ref: (j, core_ref[0], 0)) for g in gs]
    half = [blk(g, lambda j, core_ref: (j, 0, 0)) for g in gs]
    outs = tuple(jax.ShapeDtypeStruct((g.shape[0], g.shape[1] // 2, g.shape[2]), BF16) for g in gs)
    return _prefetch_call(body, name, core, (*gs, *gots), (N_CHIPS,), mine + half, tuple(half), outs, ("parallel",))


def chip_sums(pairs, gots, chip, layers, intos, name):
    n = len(pairs)

    def body(chip_ref, *refs):
        for f in range(n):
            p_ref, a_ref, b_ref, c_ref = refs[4 * f:4 * f + 4]
            refs[5 * n + f][...] = ((p_ref[...].astype(F32) + a_ref[...].astype(F32)) + b_ref[...].astype(F32)) + c_ref[...].astype(F32)

    in_specs, ops = [], []
    for p, got in zip(pairs, gots):
        blk = lambda at, p=p: pl.BlockSpec((None,) + p.shape[1:], at)
        in_specs += [blk(lambda i, chip_ref: (chip_ref[0], 0, 0))] + [blk(lambda i, chip_ref, k=k: (k, 0, 0)) for k in range(3)]
        ops += [p, got, got, got]
    out_specs = tuple(pl.BlockSpec((None,) + p.shape[1:], lambda i, chip_ref, l=l: (l, 0, 0)) for p, l in zip(pairs, layers))
    spec = pltpu.PrefetchScalarGridSpec(num_scalar_prefetch=1, grid=(1,), in_specs=in_specs + [_ANY] * n, out_specs=out_specs)
    return pl.pallas_call(
        body, name=name, grid_spec=spec, out_shape=tuple(jax.ShapeDtypeStruct(t.shape, t.dtype) for t in intos),
        input_output_aliases={1 + 4 * n + f: f for f in range(n)}, compiler_params=_params(("arbitrary",)),
    )(chip, *ops, *intos)


def adamw_family(w, m, v, g_mine, g_other, core, name):
    nl, r, c = w.shape
    gc = g_mine.shape[2]
    rh = r // 2
    nb = 4 if rh % 512 == 0 else (2 if rh % 16 == 0 and rh > 256 else 1)
    rb = rh // nb
    c1 = 1.0 - ADAM_B1 ** ADAM_STEP
    c2 = 1.0 - ADAM_B2 ** ADAM_STEP

    def body(core_ref, w_ref, m_ref, v_ref, gm_ref, go_ref, g_ref, d_ref, mo_ref, vo_ref):
        gv = jnp.where(pl.program_id(1) == core_ref[0], gm_ref[...], go_ref[...])[:, :c]
        mn = ADAM_B1 * m_ref[...] + (1.0 - ADAM_B1) * gv
        vn = ADAM_B2 * v_ref[...] + (1.0 - ADAM_B2) * (gv * gv)
        g_ref[...] = gv
        d_ref[...] = -ADAM_LR * ((mn / c1) / (jnp.sqrt(vn / c2) + ADAM_EPS) + ADAM_WD * w_ref[...])
        mo_ref[...] = mn
        vo_ref[...] = vn

    full = pl.BlockSpec((None, rb, c), lambda l, h, i, core_ref: (l, h * nb + i, 0))
    half = pl.BlockSpec((None, rb, gc), lambda l, h, i, core_ref: (l, i, 0))
    out = jax.ShapeDtypeStruct((nl, r, c), F32)
    return _prefetch_call(body, name, core, (w, m, v, g_mine, g_other), (nl, 2, nb), [full, full, full, half, half],
                          (full, full, full, full), (out, out, out, out), ("parallel", "parallel", "parallel"))


def _coords():
    return lax.axis_index("x"), lax.axis_index("y"), lax.axis_index("c")


def _other_chips(x, y):
    return ((1 - x, y), (x, 1 - y), (1 - x, 1 - y))


_ANY = pl.BlockSpec(memory_space=pl.ANY)


def _exchange_call(body, name, arrays, out_shapes, n_copies, n_local=0):
    n = len(arrays)

    def wrapped(*refs):
        body(refs[:n], refs[n:n + len(out_shapes)], *refs[n + len(out_shapes):])

    scratch = [pltpu.SemaphoreType.DMA((n_copies,)), pltpu.SemaphoreType.DMA((n_copies,))]
    if n_local:
        scratch.append(pltpu.SemaphoreType.DMA((n_local,)))
    return pl.pallas_call(
        wrapped, name=name, out_shape=tuple(out_shapes), in_specs=[_ANY] * n, out_specs=tuple([_ANY] * len(out_shapes)),
        scratch_shapes=scratch, compiler_params=_params(),
    )(*arrays)


def _remote(send_sems, recv_sems, n, src, dst, to):
    return pltpu.make_async_remote_copy(src_ref=src, dst_ref=dst, send_sem=send_sems.at[n], recv_sem=recv_sems.at[n],
                                        device_id=to, device_id_type=MESH)


class Rider:
    def __init__(self, arrays, out_shapes, n_remote, n_local, copies, then=None):
        self.arrays, self.out_shapes, self.n_remote, self.n_local = list(arrays), list(out_shapes), n_remote, n_local
        self.copies, self.then = copies, then

    def sems(self):
        return [pltpu.SemaphoreType.DMA((self.n_remote,)), pltpu.SemaphoreType.DMA((self.n_remote,)),
                pltpu.SemaphoreType.DMA((max(self.n_local, 1),))]

    def run(self, name):
        n, no = len(self.arrays), len(self.out_shapes)

        def body(*refs):
            for stage in (self.copies, self.then):
                if stage is not None:
                    cps = stage(refs[:n], refs[n:n + no], *refs[n + no:])
                    for cp in cps:
                        cp.start()
                    for cp in cps:
                        cp.wait()

        return pl.pallas_call(
            body, name=name, out_shape=tuple(self.out_shapes), in_specs=[_ANY] * n, out_specs=tuple([_ANY] * no),
            scratch_shapes=self.sems(), compiler_params=_params(),
        )(*self.arrays)


def ride(rider, body, n_in, n_out, grid):
    if rider is None:
        return body
    ni, no = len(rider.arrays), len(rider.out_shapes)

    def wrapped(*refs):
        ins, r_in = refs[:n_in], refs[n_in:n_in + ni]
        outs = refs[n_in + ni:n_in + ni + n_out]
        r_out = refs[n_in + ni + n_out:n_in + ni + n_out + no]
        rest = refs[n_in + ni + n_out + no:]
        scratch, sems = rest[:len(rest) - 3], rest[len(rest) - 3:]
        step, total = 0, 1
        for a, g in enumerate(grid):
            step, total = step * g + pl.program_id(a), total * g
        assert total >= 3
        relay_at = (7 * total) // 8 if rider.then is not None else total - 1

        @pl.when(step == 0)
        def _():
            for cp in rider.copies(r_in, r_out, *sems):
                cp.start()

        body(*ins, *outs, *scratch)

        @pl.when(step == relay_at)
        def _():
            for cp in rider.copies(r_in, r_out, *sems):
                cp.wait()
            if rider.then is not None:
                for cp in rider.then(r_in, r_out, *sems):
                    cp.start()

        if rider.then is not None:
            @pl.when(step == total - 1)
            def _():
                for cp in rider.then(r_in, r_out, *sems):
                    cp.wait()

    return wrapped


def call_with_rider(body, name, rider, ops, in_specs, out_shape, out_specs, scratch, grid):
    n_in, n_out = len(ops), len(out_shape)
    ops, in_specs, out_shape, out_specs, scratch = list(ops), list(in_specs), list(out_shape), list(out_specs), list(scratch)
    if rider is not None:
        ops += rider.arrays
        in_specs += [_ANY] * len(rider.arrays)
        out_shape += rider.out_shapes
        out_specs += [_ANY] * len(rider.out_shapes)
        scratch += rider.sems()
    res = pl.pallas_call(
        ride(rider, body, n_in, n_out, grid), name=name, out_shape=tuple(out_shape), grid=grid, in_specs=in_specs,
        out_specs=tuple(out_specs), scratch_shapes=scratch, compiler_params=_params(("arbitrary",) * len(grid)),
    )(*ops)
    return tuple(res[:n_out]), list(res[n_out:])


def gather_rider(shards):
    nf = len(shards)
    half = lambda ref, which: pl.ds(which * (ref.shape[-2] // 2), ref.shape[-2] // 2)

    def copies(s_refs, o_refs, send_sems, recv_sems, local_sems):
        x, y, c = _coords()
        me = 2 * x + y
        cps = [pltpu.make_async_copy(s_refs[f], o_refs[f].at[me], local_sems.at[f]) for f in range(nf)]
        for k, (px, py) in enumerate(_other_chips(x, y)):
            for f in range(nf):
                rows = half(s_refs[f], c)
                cps.append(_remote(send_sems, recv_sems, k * nf + f, s_refs[f].at[rows], o_refs[f].at[me, rows], (px, py, c)))
        return cps

    def relay(s_refs, o_refs, send_sems, recv_sems, local_sems):
        x, y, c = _coords()
        cps = []
        for k, (px, py) in enumerate(_other_chips(x, y)):
            for f in range(nf):
                landed = o_refs[f].at[2 * px + py, half(s_refs[f], c)]
                cps.append(_remote(send_sems, recv_sems, (3 + k) * nf + f, landed, landed, (x, y, 1 - c)))
        return cps

    return Rider(shards, [jax.ShapeDtypeStruct((N_CHIPS,) + sh.shape, sh.dtype) for sh in shards], 6 * nf, nf, copies, relay)


def scatter_rider(pairs):
    nf = len(pairs)

    def copies(p_refs, o_refs, send_sems, recv_sems, local_sems):
        x, y, c = _coords()
        cps = []
        for k, (px, py) in enumerate(_other_chips(x, y)):
            for f in range(nf):
                cps.append(_remote(send_sems, recv_sems, k * nf + f, p_refs[f].at[2 * px + py], o_refs[f].at[k], (px, py, c)))
        return cps

    return Rider(pairs, [jax.ShapeDtypeStruct((3,) + p.shape[1:], p.dtype) for p in pairs], 3 * nf, 0, copies)


def pair_swap(grads, name):
    def body(g_refs, o_refs, send_sems, recv_sems):
        x, y, c = _coords()
        cps = []
        for f, g_ref in enumerate(g_refs):
            rh = g_ref.shape[1] // 2
            cps.append(_remote(send_sems, recv_sems, f, g_ref.at[:, pl.ds((1 - c) * rh, rh), :], o_refs[f], (x, y, 1 - c)))
        for cp in cps:
            cp.start()
        for cp in cps:
            cp.wait()

    outs = [jax.ShapeDtypeStruct((g.shape[0], g.shape[1] // 2, g.shape[2]), g.dtype) for g in grads]
    return _exchange_call(body, name, grads, outs, len(grads))


def half_swap(halves, name):
    def body(h_refs, o_refs, send_sems, recv_sems):
        x, y, c = _coords()
        cps = [_remote(send_sems, recv_sems, f, h_ref, o_refs[f], (x, y, 1 - c)) for f, h_ref in enumerate(h_refs)]
        for cp in cps:
            cp.start()
        for cp in cps:
            cp.wait()

    return _exchange_call(body, name, halves, [jax.ShapeDtypeStruct(h.shape, h.dtype) for h in halves], len(halves))


def allsum_small(part, name):
    def body(p_ref, tot_ref, all_ref, send_sems, recv_sems):
        x, y, c = _coords()
        me, sibling = (x, y, c), (x, y, 1 - c)
        chips = _other_chips(x, y)

        def slot(px, py, pc):
            return all_ref.at[4 * px + 2 * py + pc]

        def copy(k, block, to, src=None):
            return pltpu.make_async_remote_copy(src_ref=slot(*block) if src is None else src, dst_ref=slot(*block),
                                                send_sem=send_sems.at[k], recv_sem=recv_sems.at[k], device_id=to, device_id_type=MESH)

        slot(*me)[...] = p_ref[...]
        first = [copy(0, me, sibling, src=p_ref)] + [copy(1 + j, me, (*chip, c), src=p_ref) for j, chip in enumerate(chips)]
        for cp in first:
            cp.start()
        passed = [copy(4 + j, (*chip, c), sibling) for j, chip in enumerate(chips)]
        for j, chip in enumerate(chips):
            copy(1 + j, (*chip, c), me).wait_recv()
            passed[j].start()
        copy(0, sibling, me).wait_recv()
        for j, chip in enumerate(chips):
            copy(4 + j, (*chip, 1 - c), me).wait_recv()
        for cp in first + passed:
            cp.wait_send()
        tot = all_ref[0]
        for d in range(1, 8):
            tot = tot + all_ref[d]
        tot_ref[...] = tot

    vm = pl.BlockSpec(memory_space=pltpu.VMEM)
    return pl.pallas_call(
        body, name=name, out_shape=jax.ShapeDtypeStruct(part.shape, F32), in_specs=[vm], out_specs=vm,
        scratch_shapes=[pltpu.VMEM((8,) + part.shape, F32), pltpu.SemaphoreType.DMA((7,)), pltpu.SemaphoreType.DMA((7,))],
        compiler_params=_params(),
    )(part)


QKVF_COLS = 772
QKVF_PAD = 896
FORWARD_CARRY = {0: (("fi0", "fo0"), ("qkv1", "o1", "fo1")), 1: (("fi1", "qkv2", "o2"), ()),
                 2: (("fi2", "fo2"), ("qkv3", "o3", "fo3")), 3: (("fi3",), ())}


def _tables_for(s):
    return _rot_tables(s)


def layer_families(layer):
    return (0, 1, layer // 2) if layer % 2 == 0 else (2, 3, layer // 2)


class GradientExchange:
    def __init__(self, family_layers):
        self.core = lax.axis_index("c").astype(jnp.int32).reshape(1)
        self.chip = (2 * lax.axis_index("x") + lax.axis_index("y")).astype(jnp.int32).reshape(1)
        self.family_layers = family_layers
        self.pairs, self.mine, self.pending = {}, {}, []

    def add(self, items, tag):
        gs = [g for _, _, g in items]
        sums = pair_sums(gs, pair_swap(gs, f"grad_pair_swap_{tag}"), self.core, f"grad_pair_sum_{tag}")
        for (fam, li, _), pair in zip(items, sums):
            self.pairs[(fam, li)] = pair
            self.pending.append((fam, li))

    def rider(self, only=None):
        keys = [k for k in self.pending if only is None or k in only]
        self.pending = [k for k in self.pending if k not in keys]
        return (scatter_rider([self.pairs[k] for k in keys]) if keys else None), keys

    def landed(self, keys, outs):
        batch = []
        for k, o in list(zip(keys, outs)) + [(None, None)]:
            if batch and (k is None or k[0] in [b[0][0] for b in batch]):
                ks = [b[0] for b in batch]
                pairs = [self.pairs[b] for b in ks]
                intos = [self.mine[fam] if fam in self.mine else jnp.zeros((self.family_layers[fam],) + p.shape[1:], F32)
                         for (fam, _), p in zip(ks, pairs)]
                sums = chip_sums(pairs, [b[1] for b in batch], self.chip, [li for _, li in ks], intos,
                                 "grad_chip_sum_" + "_".join(f"{f}{li}" for f, li in ks))
                self.mine.update({fam: t for (fam, _), t in zip(ks, sums)})
                batch = []
            if k is not None:
                batch.append((k, o))

    def finish(self, weights, moments1, moments2):
        last, keys = self.rider()
        if last is not None:
            self.landed(keys, last.run("grad_chip_scatter_last"))
        mine = [self.mine[fam] for fam in range(len(weights))]
        other = half_swap(mine, "grad_half_swap")
        return [adamw_family(w, m, v, gm, go, self.core, f"adamw_{f}")
                for f, (w, m, v, gm, go) in enumerate(zip(weights, moments1, moments2, mine, other))]


class KeepGradients:
    def __init__(self):
        self.grads = {}

    def add(self, items, tag):
        for fam, li, g in items:
            self.grads[(fam, li)] = g

    def rider(self, only=None):
        return None, []

    def landed(self, keys, outs):
        pass


def kernel(x, norm_mix, w_qkv_even, w_o_even, w_qkvf_odd, b_forget, w_o_odd, norm_ffn, w_ffn_in, w_ffn_out, norm_final, loss_target, m_norm_mix, m_w_qkv_even, m_w_o_even, m_w_qkvf_odd, m_b_forget, m_w_o_odd, m_norm_ffn, m_w_ffn_in, m_w_ffn_out, m_norm_final, v_norm_mix, v_w_qkv_even, v_w_o_even, v_w_qkvf_odd, v_b_forget, v_w_o_odd, v_norm_ffn, v_w_ffn_in, v_w_ffn_out, v_norm_final):
    w_shards = [w_qkv_even, w_o_even, w_qkvf_odd, w_o_odd, w_ffn_in, w_ffn_out]
    shards = [w.astype(BF16) for w in w_shards]
    tables = _tables_for(x.shape[1])
    bias_pad = jnp.pad(b_forget, ((0, 0), (0, LANES - N_HEADS)))

    mine = {}
    for layer in range(DEPTH):
        fam_qkv, fam_o, li = layer_families(layer)
        mine.update({f"qkv{layer}": shards[fam_qkv][li], f"o{layer}": shards[fam_o][li],
                     f"fi{layer}": shards[4][layer], f"fo{layer}": shards[5][layer]})
    fetch = lambda names: gather_rider([mine[n] for n in names])
    have = dict(zip(("qkv0", "o0"), fetch(("qkv0", "o0")).run("gather_first")))
    saved, cur = [], x[0]
    h1 = rmsnorm_fwd(cur, norm_mix[0:1], "l0_norm_mix")
    for layer in range(DEPTH):
        carry, side_carry = FORWARD_CARRY[layer]
        next_gain = norm_mix[layer + 1:layer + 2] if layer + 1 < DEPTH else None
        cur, h1, keep = forward_layer(layer, cur, h1, have, norm_ffn[layer:layer + 1], next_gain, tables,
                                      bias_pad[layer // 2:layer // 2 + 1], fetch, carry, side_carry)
        saved.append(keep)

    dcur, g_final, loss_part = loss_head(cur, norm_final.reshape(1, D_MODEL), loss_target[0], "loss_head")

    exchange = GradientExchange([w.shape[0] for w in w_shards])
    g_mix, g_ffn, g_bias = [None] * DEPTH, [None] * DEPTH, [None] * (DEPTH // 2)
    for layer in reversed(range(DEPTH)):
        dcur, g_mix[layer], g_ffn[layer], g_b = backward_layer(layer, dcur, saved[layer], norm_mix[layer:layer + 1],
                                                               norm_ffn[layer:layer + 1], tables, bias_pad[layer // 2:layer // 2 + 1], exchange)
        if g_b is not None:
            g_bias[layer // 2] = g_b

    zero_row = jnp.zeros((1, D_MODEL), F32)
    pad16 = lambda v: jnp.pad(v, (0, D_MODEL - v.shape[0]))[None, :]
    small_rows = lambda mix, ffn, fin, bias, last: jnp.concatenate(
        [r.reshape(1, D_MODEL) for r in mix] + [r.reshape(1, D_MODEL) for r in ffn] + [fin.reshape(1, D_MODEL)]
        + [pad16(b) for b in bias] + [last] + [zero_row] * (SMALL_ROWS - 12), axis=0)
    loss_row = pad16(loss_part[0, :1])
    small_g = allsum_small(small_rows(g_mix, g_ffn, g_final, g_bias, loss_row), "allsum_small")
    loss = small_g[11, 0]
    small_g = small_g.at[11].set(0.0)
    sw = small_rows(list(norm_mix), list(norm_ffn), norm_final, list(b_forget), zero_row)
    sm = small_rows(list(m_norm_mix), list(m_norm_ffn), m_norm_final, list(m_b_forget), zero_row)
    sv = small_rows(list(v_norm_mix), list(v_norm_ffn), v_norm_final, list(v_b_forget), zero_row)
    sd, snm, snv = adamw(sw, small_g, sm, sv, "adamw_small")

    def small_out(a):
        return a[0:4], a[8, :], a[9:11, :N_HEADS], a[4:8]

    big = exchange.finish(w_shards, [m_w_qkv_even, m_w_o_even, m_w_qkvf_odd, m_w_o_odd, m_w_ffn_in, m_w_ffn_out],
                          [v_w_qkv_even, v_w_o_even, v_w_qkvf_odd, v_w_o_odd, v_w_ffn_in, v_w_ffn_out])

    def outputs(small, which):
        mix, fin, bias, ffn = small_out(small)
        qkv_e, o_e, qkvf, o_o, fi, fo = [big[f][which] for f in range(6)]
        return [mix, qkv_e, o_e, qkvf, bias, o_o, ffn, fi, fo, fin]

    return (loss, dcur[None], *outputs(small_g, 0), *outputs(sd, 1), *outputs(snm, 2), *outputs(snv, 3))


def _chip_tile(rows, cols, at):
    return pl.BlockSpec((None, rows, cols), at)


def forward_layer(layer, cur, h1, have, ffn_gain, next_gain, tables, bias_row, fetch=None, carry=(), side_carry=()):
    n = f"l{layer}"
    s = cur.shape[0]
    w_qkv, w_o = have[f"qkv{layer}"], have[f"o{layer}"]
    rider = fetch(carry) if carry else None
    side_rider = fetch(side_carry) if side_carry else None
    keep = {"x": cur, "h1": h1, "w_o": w_o.reshape(D_ATTN, D_MODEL)}
    side = []
    if layer % 2 == 0:
        qkv = matmul(h1, w_qkv, "nn", BF16, n + "_qkv", 1024, 768, 1024, mnk=(s, 3 * D_ATTN, D_MODEL),
                     b_spec=_chip_tile(D_MODEL, 768, lambda i, j, kk: (j, 0, 0)))
        o_sb, st, rode = causal_fwd(qkv, 4, "sb", n + "_sb_fwd", rider=rider)
        o_dil, lse_dil, side = dilated_fwd(qkv, tables, n + "_dil_fwd", rider=side_rider)
        attn = jnp.concatenate([o_sb, o_dil], axis=1).astype(BF16)
        keep.update(o_dil=o_dil, lse_dil=lse_dil, w_qkv=w_qkv)
    else:
        natural = jnp.transpose(w_qkv, (1, 0, 2)).reshape(D_MODEL, N_CHIPS * QKVF_COLS)
        w_gate = jnp.pad(natural[:, 3 * D_ATTN:], ((0, 0), (0, LANES - N_HEADS)))
        qkv = matmul(h1, natural[:, :3 * D_ATTN], "nn", BF16, n + "_qkv", 1024, 768, 1024)
        fl = matmul(h1, w_gate, "nn", F32, n + "_fgate", 512, LANES, 1024)
        cum = forget_fwd(fl, bias_row, n + "_forget_fwd")
        f_heads = cum[:, :N_HEADS].T
        fq = jnp.broadcast_to(f_heads[:, :, None], (N_HEADS, s, LANES))
        fk = f_heads.reshape(N_HEADS // 2, 2, s)
        attn, st, rode = causal_fwd(qkv, 8, "fox", n + "_fox_fwd", fq=fq, fk=fk, rider=rider)
        attn = attn.astype(BF16)
        keep.update(fl=fl, fq=fq, fk=fk, w_qkv=jnp.concatenate([natural[:, :3 * D_ATTN], w_gate], axis=1))
    have.update(zip(carry, rode))
    have.update(zip(side_carry, side))
    w_fi, w_fo = have[f"fi{layer}"], have[f"fo{layer}"].reshape(D_FF, D_MODEL)
    mid, h2 = matmul(attn, keep["w_o"], "nn", F32, n + "_attn_out", 512, 1024, 1024, res=cur, norm_gain=ffn_gain)
    gate, up, act = ffn_in_swiglu(h2, w_fi, n + "_ffn_in")
    if next_gain is None:
        out, h_next = matmul(act, w_fo, "nn", F32, n + "_ffn_out", 512, 1024, D_FF, res=mid), None
    else:
        out, h_next = matmul(act, w_fo, "nn", F32, n + "_ffn_out", 512, 1024, D_FF, res=mid, norm_gain=next_gain)
    keep.update(qkv=qkv, st=st, attn=attn, mid=mid, h2=h2, gate=gate, up=up, act=act, w_fi=w_fi, w_fo=w_fo)
    return out, h_next, keep


def backward_layer(layer, dcur, kp, mix_gain, ffn_gain, tables, bias_row, exchange):
    n = f"l{layer}"
    s = dcur.shape[0]
    fam_qkv, fam_o, li = layer_families(layer)
    g_fo = matmul(kp["act"], dcur, "tn", BF16, n + "_d_w_ffn_out", 1408, 1024, s)
    dact = matmul(dcur, kp["w_fo"], "nt", BF16, n + "_d_act", 1024, 1408, 1024)
    dgu = swiglu_bwd(kp["gate"], kp["up"], dact, n + "_d_swiglu")
    g_fi = matmul(kp["h2"], dgu, "tn", BF16, n + "_d_w_ffn_in", 1024, 1408, 2048, mnk=(D_MODEL, 2 * D_FF, s),
                  o_spec=_chip_tile(D_MODEL, 1408, lambda i, j, kk: (j, 0, 0)), out_shape=(N_CHIPS, D_MODEL, 1408))
    dmid, g_ffn = dh_norm_bwd(dgu, kp["w_fi"], kp["mid"], ffn_gain, dcur, n + "_d_h2", 256)
    g_o = matmul(kp["attn"], dmid, "tn", BF16, n + "_d_w_o", 1024, 1024, s)
    dattn = matmul(dmid, kp["w_o"], "nt", F32, n + "_d_attn", 1024, 1024, 1024)
    exchange.add([(5, layer, g_fo.reshape(N_CHIPS, D_FF // N_CHIPS, D_MODEL)), (4, layer, g_fi),
                  (fam_o, li, g_o.reshape(N_CHIPS, D_ATTN // N_CHIPS, D_MODEL))], f"l{layer}_ffn")
    g_bias = None
    if layer % 2 == 0:
        rider, keys = exchange.rider(only=[(4, layer), (fam_o, li)])
        dq_a, dk_a, dv_a, rode = causal_bwd(kp["qkv"], dattn, kp["st"], 4, "sb", n + "_sb_bwd", rider=rider)
        exchange.landed(keys, rode)
        rider, keys = exchange.rider()
        dq_b, dk_b, dv_b, rode = dilated_bwd(kp["qkv"], tables, dattn, kp["o_dil"], kp["lse_dil"], 4, n + "_dil_bwd", rider=rider)
        dproj = jnp.concatenate([dq_a, dq_b, dk_a.astype(BF16), dk_b, dv_a.astype(BF16), dv_b], axis=1)
        g_qkv = matmul(kp["h1"], dproj, "tn", BF16, n + "_d_w_qkv", 1024, 768, 2048, mnk=(D_MODEL, 3 * D_ATTN, s),
                       o_spec=_chip_tile(D_MODEL, 768, lambda i, j, kk: (j, 0, 0)), out_shape=(N_CHIPS, D_MODEL, 768))
    else:
        rider, keys = exchange.rider()
        dq_f, dk_f, dv_f, dfk, rode = causal_bwd(kp["qkv"], dattn, kp["st"], 8, "fox", n + "_fox_bwd", fq=kp["fq"], fk=kp["fk"],
                                                 rider=rider)
        dcum = jnp.pad(dfk.reshape(N_HEADS, s).T, ((0, 0), (0, LANES - N_HEADS)))
        dfl, dbias = forget_bwd(kp["fl"], bias_row, dcum, n + "_forget_bwd")
        g_bias = dbias[0, :N_HEADS]
        dproj = jnp.concatenate([dq_f, dk_f.astype(BF16), dv_f.astype(BF16), dfl.astype(BF16)], axis=1)
        g_nat = matmul(kp["h1"], dproj, "tn", BF16, n + "_d_w_qkv", 1024, 640, 2048)
        g_qkv = g_nat[:, :N_CHIPS * QKVF_COLS].reshape(D_MODEL, N_CHIPS, QKVF_COLS)
        g_qkv = jnp.transpose(jnp.pad(g_qkv, ((0, 0), (0, 0), (0, QKVF_PAD - QKVF_COLS))), (1, 0, 2))
    exchange.landed(keys, rode)
    exchange.add([(fam_qkv, li, g_qkv)], f"l{layer}_qkv")
    dx, g_mix = dh_norm_bwd(dproj, kp["w_qkv"], kp["x"], mix_gain, dmid, n + "_d_h1", 512)
    return dx, g_mix, g_ffn, g_bias


def local_step(xs, target, norm_mix, norm_ffn, norm_final, b_forget, layer_weights):
    tables = _tables_for(xs.shape[0])
    bias_pad = jnp.pad(b_forget, ((0, 0), (0, LANES - N_HEADS)))
    saved, cur, have = [], xs, {}
    h1 = rmsnorm_fwd(cur, norm_mix[0:1], "l0_norm_mix")
    for layer in range(DEPTH):
        have.update(zip((f"qkv{layer}", f"o{layer}", f"fi{layer}", f"fo{layer}"), layer_weights[layer]))
        next_gain = norm_mix[layer + 1:layer + 2] if layer + 1 < DEPTH else None
        cur, h1, keep = forward_layer(layer, cur, h1, have, norm_ffn[layer:layer + 1], next_gain, tables,
                                      bias_pad[layer // 2:layer // 2 + 1])
        saved.append(keep)
    dcur, g_final, loss_part = loss_head(cur, norm_final.reshape(1, D_MODEL), target, "loss_head")
    keeper = KeepGradients()
    g_mix, g_ffn, g_bias = [None] * DEPTH, [None] * DEPTH, [None] * (DEPTH // 2)
    for layer in reversed(range(DEPTH)):
        dcur, g_mix[layer], g_ffn[layer], g_b = backward_layer(layer, dcur, saved[layer], norm_mix[layer:layer + 1],
                                                               norm_ffn[layer:layer + 1], tables, bias_pad[layer // 2:layer // 2 + 1], keeper)
        if g_b is not None:
            g_bias[layer // 2] = g_b
    return dcur, keeper.grads, (g_mix, g_ffn, g_final, g_bias), loss_part
```

```python
import functools

import jax
import jax.numpy as jnp
from jax import lax
from jax.experimental import pallas as pl
from jax.experimental.pallas import tpu as pltpu

F32 = jnp.float32
BF16 = jnp.bfloat16
MESH = pl.DeviceIdType.MESH

D_MODEL = 1024
DEPTH = 4
HEAD_DIM = 64
N_HEADS = 16
D_ATTN = 1024
D_FF = 2816
ROPE_THETA = 500000.0
ROT_HALF = 8
RMS_EPS = 1e-5
DIL_STRIDES = (1, 4, 16)
ADAM_LR, ADAM_B1, ADAM_B2, ADAM_EPS, ADAM_WD, ADAM_STEP = 0.001, 0.9, 0.999, 1e-8, 0.01, 10

LANES = 128
BLK = 128
VMEM_LIMIT = 56 * 1024 * 1024
NEG = -1e30
N_CHIPS = 4
SMALL_ROWS = 16


def _params(sem=None):
    return pltpu.CompilerParams(dimension_semantics=sem, vmem_limit_bytes=VMEM_LIMIT)


def _dot(a, b):
    return lax.dot_general(a, b, (((1,), (0,)), ((), ())), preferred_element_type=F32)


def _dot_nt(a, b):
    return lax.dot_general(a, b, (((1,), (1,)), ((), ())), preferred_element_type=F32)


def _dot_tn(a, b):
    return lax.dot_general(a, b, (((0,), (0,)), ((), ())), preferred_element_type=F32)


def _split3(x):
    x1 = x.astype(BF16)
    r1 = x - x1.astype(F32)
    x2 = r1.astype(BF16)
    x3 = (r1 - x2.astype(F32)).astype(BF16)
    return x1, x2, x3


def _dot_exact_lhs(x, t):
    x1, x2, x3 = _split3(x)
    return _dot(x1, t) + _dot(x2, t) + _dot(x3, t)


def _dot_exact_rhs(t, x):
    x1, x2, x3 = _split3(x)
    return _dot(t, x1) + _dot(t, x2) + _dot(t, x3)


def _iotas(shape=(BLK, LANES)):
    return lax.broadcasted_iota(jnp.int32, shape, 0), lax.broadcasted_iota(jnp.int32, shape, 1)


_DIMS = {"nn": (((1,), (0,)), ((), ())), "nt": (((1,), (1,)), ((), ())), "tn": (((0,), (0,)), ((), ()))}


def matmul(a, b, mode, out_dtype, name, tm, tn, tk, res=None, mnk=None, b_spec=None, o_spec=None, out_shape=None, norm_gain=None):
    if mnk is not None:
        m, n, k = mnk
    elif mode == "nn":
        (m, k), (k2, n) = a.shape, b.shape
    elif mode == "nt":
        (m, k), (n, k2) = a.shape, b.shape
    else:
        (k, m), (k2, n) = a.shape, b.shape
    assert m % tm == 0 and n % tn == 0 and k % tk == 0, (name, a.shape, b.shape)
    nk = k // tk
    a_spec = pl.BlockSpec((tk, tm), lambda i, j, kk: (kk, i)) if mode == "tn" else pl.BlockSpec((tm, tk), lambda i, j, kk: (i, kk))
    if b_spec is None:
        b_spec = pl.BlockSpec((tn, tk), lambda i, j, kk: (j, kk)) if mode == "nt" else pl.BlockSpec((tk, tn), lambda i, j, kk: (kk, j))
    r_spec = pl.BlockSpec((tm, tn), lambda i, j, kk: (i, j))
    if o_spec is None:
        o_spec = r_spec
    dims = _DIMS[mode]
    has_res, has_norm = res is not None, norm_gain is not None
    assert not has_norm or (tn == n and nk == 1)
    n_in = 2 + int(has_res) + int(has_norm)

    def body(*refs):
        a_ref, b_ref = refs[0], refs[1]
        r_ref = refs[2] if has_res else None
        o_ref = refs[n_in]

        def finish(v):
            if has_res:
                v = v + r_ref[...]
            o_ref[...] = v.astype(out_dtype)
            if has_norm:
                rstd = lax.rsqrt(jnp.mean(v * v, axis=-1, keepdims=True) + RMS_EPS)
                refs[n_in + 1][...] = (v * rstd * refs[n_in - 1][...]).astype(BF16)

        bv = b_ref[...]
        if bv.ndim == 3:
            bv = jnp.concatenate([bv[j] for j in range(bv.shape[0])], axis=1)
        p = lax.dot_general(a_ref[...].astype(BF16), bv.astype(BF16), dims, preferred_element_type=F32)
        if nk == 1:
            finish(p)
        else:
            acc = refs[-1]
            kk = pl.program_id(2)

            @pl.when(kk == 0)
            def _():
                acc[...] = p

            @pl.when(kk > 0)
            def _():
                acc[...] += p

            @pl.when(kk == nk - 1)
            def _():
                finish(acc[...])

    ops = [a, b] + ([res] if has_res else []) + ([norm_gain] if has_norm else [])
    specs = [a_spec, b_spec] + ([r_spec] if has_res else []) + ([pl.BlockSpec((1, tn), lambda i, j, kk: (0, j))] if has_norm else [])
    out_shape = jax.ShapeDtypeStruct((m, n) if out_shape is None else out_shape, out_dtype)
    return pl.pallas_call(
        body, name=name, out_shape=(out_shape, jax.ShapeDtypeStruct((m, n), BF16)) if has_norm else out_shape,
        grid=(m // tm, n // tn, nk), in_specs=specs, out_specs=(o_spec, r_spec) if has_norm else o_spec,
        scratch_shapes=[pltpu.VMEM((tm, tn), F32)] if nk > 1 else [],
        compiler_params=_params(("parallel", "parallel", "arbitrary")),
    )(*ops)


ROWS = 256


def _row_spec(cols, rows=ROWS):
    return pl.BlockSpec((rows, cols), lambda i: (i, 0))


def _fix_spec(r, cols):
    return pl.BlockSpec((r, cols), lambda i: (0, 0))


def rmsnorm_fwd(x, g, name):
    s, d = x.shape

    def body(x_ref, g_ref, h_ref):
        xv = x_ref[...]
        rstd = lax.rsqrt(jnp.mean(xv * xv, axis=-1, keepdims=True) + RMS_EPS)
        h_ref[...] = (xv * rstd * g_ref[...]).astype(BF16)

    return pl.pallas_call(
        body, name=name, out_shape=jax.ShapeDtypeStruct((s, d), BF16), grid=(s // ROWS,),
        in_specs=[_row_spec(d), _fix_spec(1, d)], out_specs=_row_spec(d), compiler_params=_params(("parallel",)),
    )(x, g)


def _rms_bwd_math(xv, gv, dh):
    rstd = lax.rsqrt(jnp.mean(xv * xv, axis=-1, keepdims=True) + RMS_EPS)
    xhat = xv * rstd
    u = dh * gv
    dx = rstd * (u - xhat * jnp.mean(u * xhat, axis=-1, keepdims=True))
    return dx, dh * xhat


def dh_norm_bwd(dy, w, x, g, dres, name, tm):
    s, k = dy.shape
    d = x.shape[1]

    def body(dy_ref, w_ref, x_ref, g_ref, dres_ref, dx_ref, dg_ref):
        wv = w_ref[...]
        if wv.ndim == 3:
            wv = jnp.concatenate([wv[j] for j in range(wv.shape[0])], axis=1)
        dx, dgt = _rms_bwd_math(x_ref[...], g_ref[...], _dot_nt(dy_ref[...], wv))
        dx_ref[...] = dres_ref[...] + dx
        part = jnp.sum(dgt, axis=0, keepdims=True)

        @pl.when(pl.program_id(0) == 0)
        def _():
            dg_ref[...] = part

        @pl.when(pl.program_id(0) > 0)
        def _():
            dg_ref[...] += part

    w_spec = pl.BlockSpec(w.shape, lambda i: (0,) * w.ndim)
    return pl.pallas_call(
        body, name=name, out_shape=(jax.ShapeDtypeStruct((s, d), F32), jax.ShapeDtypeStruct((1, d), F32)), grid=(s // tm,),
        in_specs=[pl.BlockSpec((tm, k), lambda i: (i, 0)), w_spec, _row_spec(d, tm), _fix_spec(1, d), _row_spec(d, tm)],
        out_specs=(_row_spec(d, tm), _fix_spec(1, d)), compiler_params=_params(("arbitrary",)),
    )(dy, w, x, g, dres)


def loss_head(x, g, target, name):
    s, d = x.shape

    def body(x_ref, g_ref, t_ref, dx_ref, dg_ref, loss_ref):
        xv, gv = x_ref[...], g_ref[...]
        rstd = lax.rsqrt(jnp.mean(xv * xv, axis=-1, keepdims=True) + RMS_EPS)
        err = xv * rstd * gv - t_ref[...]
        dx, dgt = _rms_bwd_math(xv, gv, err * (1.0 / d))
        dx_ref[...] = dx
        part = jnp.sum(dgt, axis=0, keepdims=True)
        lpart = jnp.full((1, LANES), 0.5 / d, F32) * jnp.sum(err * err)

        @pl.when(pl.program_id(0) == 0)
        def _():
            dg_ref[...] = part
            loss_ref[...] = lpart

        @pl.when(pl.program_id(0) > 0)
        def _():
            dg_ref[...] += part
            loss_ref[...] += lpart

    return pl.pallas_call(
        body, name=name,
        out_shape=(jax.ShapeDtypeStruct((s, d), F32), jax.ShapeDtypeStruct((1, d), F32), jax.ShapeDtypeStruct((1, LANES), F32)),
        grid=(s // ROWS,), in_specs=[_row_spec(d), _fix_spec(1, d), _row_spec(d)],
        out_specs=(_row_spec(d), _fix_spec(1, d), _fix_spec(1, LANES)), compiler_params=_params(("arbitrary",)),
    )(x, g, target)


def ffn_in_swiglu(h, w_in, name, tm=1024):
    s, d = h.shape
    cols = w_in.shape[2]

    def body(h_ref, wg_ref, wu_ref, g_ref, u_ref, a_ref):
        hv = h_ref[...]
        gv, uv = _dot(hv, wg_ref[...]), _dot(hv, wu_ref[...])
        g_ref[...] = gv.astype(BF16)
        u_ref[...] = uv.astype(BF16)
        a_ref[...] = (gv * (1.0 / (1.0 + jnp.exp(-gv))) * uv).astype(BF16)

    tile = pl.BlockSpec((tm, cols), lambda i, j: (i, j))
    out = jax.ShapeDtypeStruct((s, 2 * cols), BF16)
    return pl.pallas_call(
        body, name=name, out_shape=(out, out, out), grid=(s // tm, 2),
        in_specs=[pl.BlockSpec((tm, d), lambda i, j: (i, 0)), pl.BlockSpec((None, d, cols), lambda i, j: (j, 0, 0)),
                  pl.BlockSpec((None, d, cols), lambda i, j: (j + 2, 0, 0))],
        out_specs=(tile, tile, tile), compiler_params=_params(("parallel", "parallel")),
    )(h, w_in, w_in)


def swiglu_bwd(gate, up, dact, name):
    s, f = gate.shape

    def body(g_ref, u_ref, da_ref, o_ref):
        gv, uv, da = g_ref[...].astype(F32), u_ref[...].astype(F32), da_ref[...].astype(F32)
        sg = 1.0 / (1.0 + jnp.exp(-gv))
        o_ref[:, :f] = (da * uv * sg * (1.0 + gv * (1.0 - sg))).astype(BF16)
        o_ref[:, f:] = (da * gv * sg).astype(BF16)

    return pl.pallas_call(
        body, name=name, out_shape=jax.ShapeDtypeStruct((s, 2 * f), BF16), grid=(s // ROWS,),
        in_specs=[_row_spec(f)] * 3, out_specs=_row_spec(2 * f), compiler_params=_params(("parallel",)),
    )(gate, up, dact)


Q_OFF, K_OFF, V_OFF = 0, 8, 16


KB = 512
BQ = 512
SUB = KB // BLK


def _softplus_parts(z):
    sp = jnp.log(1.0 + jnp.exp(-jnp.abs(z)))
    ls = jnp.minimum(z, 0.0) - sp
    return ls, ls - z


def _wide(t):
    return jnp.concatenate([t] * SUB, axis=1)


def _chunk_dots(x, tri):
    terms = []
    for u in range(SUB):
        xu = x[:, u * BLK:(u + 1) * BLK]
        hi = xu.astype(BF16)
        terms += [hi, (xu - hi.astype(F32)).astype(BF16)]
    r = _dot(jnp.concatenate(terms, axis=0), tri)
    rows = x.shape[0]
    piece = lambda n: r[n * rows:(n + 1) * rows]
    return [piece(2 * u) + piece(2 * u + 1) for u in range(SUB)]


def _block_suffix_sums(x, suffix, c):
    loc = _chunk_dots(x, suffix)
    out = [None] * SUB
    for u in reversed(range(SUB)):
        out[u] = loc[u] + c
        c = c + jnp.sum(x[:, u * BLK:(u + 1) * BLK], axis=1, keepdims=True)
    return jnp.concatenate(out, axis=1), c


def _block_prefix_sums(x, tri, c):
    loc = _chunk_dots(x, tri)
    out = []
    for u in range(SUB):
        out.append(loc[u] + c)
        c = c + jnp.sum(x[:, u * BLK:(u + 1) * BLK], axis=1, keepdims=True)
    return jnp.concatenate(out, axis=1), c


def causal_fwd(qkv, npairs, mode, name, fq=None, fk=None, rider=None):
    s = qkv.shape[0]
    nq = s // BQ
    fox = mode == "fox"

    def body(*refs):
        if fox:
            q_ref, k_ref, v_ref, fq_ref, fk_ref, o_ref, st_ref = refs
        else:
            q_ref, k_ref, v_ref, o_ref, st_ref = refs
        i = pl.program_id(1)
        nkb = (i * BQ + BQ - 1) // KB + 1
        row, lane = _iotas((BQ, KB))
        row_s, lane_s = _iotas()
        _, lane_q = _iotas((BQ, LANES))
        nfull = (i * BQ) // KB
        qpos = i * BQ + row
        qf = q_ref[...].astype(F32) * 0.125
        hms = (lane_q < HEAD_DIM, lane_q >= HEAD_DIM)
        qas = [jnp.where(hm, qf, 0.0).astype(BF16) for hm in hms]
        suffix = jnp.where(row_s > lane_s, 1.0, 0.0).astype(BF16)
        zero = jnp.zeros((BQ, LANES), F32)
        col0 = jnp.zeros((BQ, 1), F32)

        def kv(j):
            r0 = pl.multiple_of(j * KB, KB)
            return r0, k_ref[pl.ds(r0, KB), :], v_ref[pl.ds(r0, KB), :]

        if fox:
            fqs = [_wide(fq_ref[a]) for a in range(2)]

            def step(j, carry, masked):
                r0, kb, vb = kv(j)
                new = []
                for a in range(2):
                    acc, mx, l = carry[3 * a:3 * a + 3]
                    z = _dot_nt(qas[a], kb) + fqs[a] - fk_ref[a:a + 1, pl.ds(r0, KB)]
                    if masked:
                        z = jnp.where(r0 + lane <= qpos, z, NEG)
                    mnew = jnp.maximum(mx, jnp.max(z, axis=1, keepdims=True))
                    p = jnp.exp(z - mnew)
                    alpha = jnp.exp(mx - mnew)
                    new += [alpha * acc + _dot(p.astype(BF16), vb), mnew, alpha * l + jnp.sum(p, axis=1, keepdims=True)]
                return tuple(new)

            neg = jnp.full((BQ, 1), NEG, F32)
            res = lax.fori_loop(0, nfull, functools.partial(step, masked=False), (zero, neg, col0, zero, neg, col0))
            res = lax.fori_loop(nfull, nkb, functools.partial(step, masked=True), res)
            outs = [res[3 * a] / res[3 * a + 2] for a in range(2)]
            stats = [res[3 * a + 1] + jnp.log(res[3 * a + 2]) for a in range(2)]
        else:
            def step(j, carry, masked):
                r0, kb, vb = kv(j)
                strict = r0 + lane < qpos
                new = []
                for a in range(2):
                    acc, c = carry[2 * a:2 * a + 2]
                    ls, lm = _softplus_parts(_dot_nt(qas[a], kb))
                    if masked:
                        lm = jnp.where(strict, lm, 0.0)
                    between, c = _block_suffix_sums(lm, suffix, c)
                    aw = jnp.exp(ls + between)
                    if masked:
                        aw = jnp.where(strict, aw, 0.0)
                    new += [acc + _dot(aw.astype(BF16), vb), c]
                return tuple(new)

            res = lax.fori_loop(0, nkb - nfull, lambda jj, c: step(nkb - 1 - jj, c, True), (zero, col0, zero, col0))
            res = lax.fori_loop(0, nfull, lambda jj, c: step(nfull - 1 - jj, c, False), res)
            outs, stats = [res[0], res[2]], [res[1], res[3]]
        o_ref[...] = jnp.where(hms[0], outs[0], outs[1])
        for a in range(2):
            st_ref[a] = jnp.broadcast_to(stats[a], (BQ, LANES))

    col = lambda off: (lambda p, i: (0, off + p))
    in_specs = [pl.BlockSpec((BQ, LANES), lambda p, i: (i, Q_OFF + p)),
                pl.BlockSpec((s, LANES), col(K_OFF)), pl.BlockSpec((s, LANES), col(V_OFF))]
    ops = [qkv, qkv, qkv]
    if fox:
        in_specs += [pl.BlockSpec((2, BQ, LANES), lambda p, i: (p, i, 0)), pl.BlockSpec((None, 2, s), lambda p, i: (p, 0, 0))]
        ops += [fq, fk]
    (o, stat), rode = call_with_rider(
        body, name, rider, ops, in_specs,
        [jax.ShapeDtypeStruct((s, npairs * LANES), F32), jax.ShapeDtypeStruct((2 * npairs, s, LANES), F32)],
        [pl.BlockSpec((BQ, LANES), lambda p, i: (i, p)), pl.BlockSpec((2, BQ, LANES), lambda p, i: (p, i, 0))], [], (npairs, nq))
    return o, stat, rode


def causal_bwd(qkv, do, stat, npairs, mode, name, fq=None, fk=None, rider=None):
    s = qkv.shape[0]
    nq = s // BQ
    fox = mode == "fox"

    def body(*refs):
        if fox:
            q_ref, k_ref, v_ref, do_ref, st_ref, fq_ref, fk_ref, dq_ref, dk_ref, dv_ref, df_ref, p_s, dp_s = refs
        else:
            q_ref, k_ref, v_ref, do_ref, st_ref, dq_ref, dk_ref, dv_ref = refs
        i = pl.program_id(1)

        @pl.when(i == 0)
        def _():
            dk_ref[...] = jnp.zeros_like(dk_ref)
            dv_ref[...] = jnp.zeros_like(dv_ref)
            if fox:
                df_ref[...] = jnp.zeros_like(df_ref)

        nkb = (i * BQ + BQ - 1) // KB + 1
        nfull = (i * BQ) // KB
        row, lane = _iotas((BQ, KB))
        row_s, lane_s = _iotas()
        _, lane_q = _iotas((BQ, LANES))
        qpos = i * BQ + row
        qf = q_ref[...].astype(F32) * 0.125
        dov = do_ref[...]
        hms = (lane_q < HEAD_DIM, lane_q >= HEAD_DIM)
        qas = [jnp.where(hm, qf, 0.0).astype(BF16) for hm in hms]
        doas = [jnp.where(hm, dov, 0.0).astype(BF16) for hm in hms]
        stas = [_wide(st_ref[a]) for a in range(2)]
        zero = jnp.zeros((BQ, LANES), F32)
        col0 = jnp.zeros((BQ, 1), F32)

        def kv(j):
            r0 = pl.multiple_of(j * KB, KB)
            return r0, k_ref[pl.ds(r0, KB), :], v_ref[pl.ds(r0, KB), :]

        if fox:
            fqs = [_wide(fq_ref[a]) for a in range(2)]

            def probs(j, deltas, masked):
                r0, kb, vb = kv(j)
                new = []
                for a in range(2):
                    z = _dot_nt(qas[a], kb) + fqs[a] - fk_ref[a:a + 1, pl.ds(r0, KB)]
                    p = jnp.exp(z - stas[a])
                    if masked:
                        p = jnp.where(r0 + lane <= qpos, p, 0.0)
                    dp = _dot_nt(doas[a], vb)
                    p_s[a, j] = p
                    dp_s[a, j] = dp
                    new.append(deltas[a] + jnp.sum(p * dp, axis=1, keepdims=True))
                return tuple(new)

            deltas = lax.fori_loop(0, nfull, functools.partial(probs, masked=False), (col0, col0))
            deltas = lax.fori_loop(nfull, nkb, functools.partial(probs, masked=True), deltas)

            def step(j, dqs):
                r0, kb, _ = kv(j)
                new = []
                dk = jnp.zeros((KB, LANES), F32)
                dv = jnp.zeros((KB, LANES), F32)
                for a in range(2):
                    p = p_s[a, j]
                    ds = p * (dp_s[a, j] - deltas[a])
                    dsb = ds.astype(BF16)
                    dk += _dot_tn(dsb, qas[a])
                    dv += _dot_tn(p.astype(BF16), doas[a])
                    df_ref[a:a + 1, pl.ds(r0, KB)] -= jnp.sum(ds, axis=0, keepdims=True)
                    new.append(dqs[a] + _dot(dsb, kb))
                dk_ref[pl.ds(r0, KB), :] += dk
                dv_ref[pl.ds(r0, KB), :] += dv
                return tuple(new)

            dqs = lax.fori_loop(0, nkb, step, (zero, zero))
        else:
            incl = jnp.where(row_s <= lane_s, 1.0, 0.0).astype(BF16)
            excl = jnp.where(row_s < lane_s, 1.0, 0.0).astype(BF16)

            def step(j, carry, masked):
                r0, kb, vb = kv(j)
                strict = r0 + lane < qpos
                new = []
                dk = jnp.zeros((KB, LANES), F32)
                dv = jnp.zeros((KB, LANES), F32)
                for a in range(2):
                    dq, cm, cg = carry[3 * a:3 * a + 3]
                    ls, lm = _softplus_parts(_dot_nt(qas[a], kb))
                    if masked:
                        lm = jnp.where(strict, lm, 0.0)
                    beta = jnp.exp(ls)
                    upto, cm = _block_prefix_sums(lm, incl, cm)
                    aw = jnp.exp(ls + stas[a] - upto)
                    if masked:
                        aw = jnp.where(strict, aw, 0.0)
                    g = aw * _dot_nt(doas[a], vb)
                    pre, cg = _block_prefix_sums(g, excl, cg)
                    dz = g * (1.0 - beta) - pre * beta
                    if masked:
                        dz = jnp.where(strict, dz, 0.0)
                    dzb = dz.astype(BF16)
                    dk += _dot_tn(dzb, qas[a])
                    dv += _dot_tn(aw.astype(BF16), doas[a])
                    new += [dq + _dot(dzb, kb), cm, cg]
                dk_ref[pl.ds(r0, KB), :] += dk
                dv_ref[pl.ds(r0, KB), :] += dv
                return tuple(new)

            res = lax.fori_loop(0, nfull, functools.partial(step, masked=False), (zero, col0, col0, zero, col0, col0))
            res = lax.fori_loop(nfull, nkb, functools.partial(step, masked=True), res)
            dqs = (res[0], res[3])
        dq_ref[...] = (jnp.where(hms[0], dqs[0], dqs[1]) * 0.125).astype(BF16)

    col = lambda off: (lambda p, i: (0, off + p))
    blk = pl.BlockSpec((BQ, LANES), lambda p, i: (i, p))
    acc = pl.BlockSpec((s, LANES), lambda p, i: (0, p))
    st_spec = pl.BlockSpec((2, BQ, LANES), lambda p, i: (p, i, 0))
    in_specs = [pl.BlockSpec((BQ, LANES), lambda p, i: (i, Q_OFF + p)), pl.BlockSpec((s, LANES), col(K_OFF)),
                pl.BlockSpec((s, LANES), col(V_OFF)), blk, st_spec]
    ops = [qkv, qkv, qkv, do, stat]
    w = npairs * LANES
    out_shape = [jax.ShapeDtypeStruct((s, w), BF16), jax.ShapeDtypeStruct((s, w), F32), jax.ShapeDtypeStruct((s, w), F32)]
    out_specs = [blk, acc, acc]
    scratch = []
    if fox:
        fk_spec = pl.BlockSpec((None, 2, s), lambda p, i: (p, 0, 0))
        in_specs += [st_spec, fk_spec]
        ops += [fq, fk]
        out_shape.append(jax.ShapeDtypeStruct((npairs, 2, s), F32))
        out_specs.append(fk_spec)
        scratch = [pltpu.VMEM((2, s // KB, BQ, KB), F32)] * 2
    outs, rode = call_with_rider(body, name, rider, ops, in_specs, out_shape, out_specs, scratch, (npairs, nq))
    return (*outs, rode)


def forget_fwd(fl, bias, name):
    s = fl.shape[0]

    def body(fl_ref, b_ref, f_ref):
        row, lane = _iotas()
        lower = jnp.where(lane <= row, 1.0, 0.0).astype(BF16)

        def step(n, carry):
            r0 = pl.multiple_of(n * BLK, BLK)
            ls, _ = _softplus_parts(fl_ref[pl.ds(r0, BLK), :] + b_ref[...])
            blk = _dot_exact_rhs(lower, ls) + carry
            f_ref[pl.ds(r0, BLK), :] = blk
            return blk[BLK - 1:BLK, :]

        lax.fori_loop(0, s // BLK, step, jnp.zeros((1, LANES), F32))

    return pl.pallas_call(
        body, name=name, out_shape=jax.ShapeDtypeStruct((s, LANES), F32),
        in_specs=[pl.BlockSpec(memory_space=pltpu.VMEM)] * 2, out_specs=pl.BlockSpec(memory_space=pltpu.VMEM),
        compiler_params=_params(),
    )(fl, bias)


def forget_bwd(fl, bias, df, name):
    s = fl.shape[0]
    nb = s // BLK

    def body(fl_ref, b_ref, df_ref, o_ref, db_ref):
        row, lane = _iotas()
        upper = jnp.where(lane >= row, 1.0, 0.0).astype(BF16)

        def step(nn, carry):
            tail, db = carry
            r0 = pl.multiple_of((nb - 1 - nn) * BLK, BLK)
            dls = _dot_exact_rhs(upper, df_ref[pl.ds(r0, BLK), :]) + tail
            xv = fl_ref[pl.ds(r0, BLK), :] + b_ref[...]
            dfl = dls * (1.0 / (1.0 + jnp.exp(xv)))
            o_ref[pl.ds(r0, BLK), :] = dfl
            return dls[0:1, :], db + jnp.sum(dfl, axis=0, keepdims=True)

        _, db = lax.fori_loop(0, nb, step, (jnp.zeros((1, LANES), F32), jnp.zeros((1, LANES), F32)))
        db_ref[...] = db

    return pl.pallas_call(
        body, name=name, out_shape=(jax.ShapeDtypeStruct((s, LANES), F32), jax.ShapeDtypeStruct((1, LANES), F32)),
        in_specs=[pl.BlockSpec(memory_space=pltpu.VMEM)] * 3,
        out_specs=(pl.BlockSpec(memory_space=pltpu.VMEM), pl.BlockSpec(memory_space=pltpu.VMEM)),
        compiler_params=_params(),
    )(fl, bias, df)


def _rot_tables(s):
    inv = ROPE_THETA ** (-jnp.arange(ROT_HALF, dtype=F32) * 2.0 / (2 * ROT_HALF))
    ang = jnp.arange(s, dtype=F32)[:, None] * inv[None, :]
    cos, sin = jnp.cos(ang), jnp.sin(ang)
    z8 = jnp.zeros((s, ROT_HALF), F32)
    rest = HEAD_DIM - 2 * ROT_HALF
    zr, onr = jnp.zeros((s, rest), F32), jnp.ones((s, rest), F32)
    tile = lambda t: jnp.tile(t, (1, 2))
    return tile(jnp.concatenate([cos, cos, onr], 1)), tile(jnp.concatenate([-sin, z8, zr], 1)), tile(jnp.concatenate([z8, sin, zr], 1))


def _deinterleave(dst, src_ref, stride, s, dtype):
    length = s // stride
    for r in range(stride):
        if stride == 1:
            dst[...] = src_ref[...].astype(dtype)
        else:
            dst[r * length:(r + 1) * length, :] = src_ref[pl.ds(r, length, stride=stride), :].astype(dtype)


def _band_masks(row, lane, first):
    return lane <= row, lane >= row + jnp.where(first, BLK, 0)


N_DIL_PAIRS = 4


def _rotate_into(q_ref, k_ref, v_ref, c_ref, s1_ref, s2_ref, qr, kr, vr):
    c, s1, s2 = c_ref[...], s1_ref[...], s2_ref[...]
    rot = lambda xv: xv * c + pltpu.roll(xv, LANES - ROT_HALF, 1) * s1 + pltpu.roll(xv, ROT_HALF, 1) * s2
    qr[...] = rot(q_ref[...].astype(F32)) * 0.125
    kr[...] = rot(k_ref[...].astype(F32))
    vr[...] = v_ref[...].astype(F32)


def _dilated_operands(qkv, tables):
    s = qkv.shape[0]
    col = lambda off: pl.BlockSpec((s, LANES), lambda p: (0, off + N_DIL_PAIRS + p))
    table = pl.BlockSpec((s, LANES), lambda p: (0, 0))
    return [qkv, qkv, qkv, *tables], [col(Q_OFF), col(K_OFF), col(V_OFF), table, table, table]


def dilated_fwd(qkv, tables, name, rider=None):
    s = qkv.shape[0]
    npairs, w = N_DIL_PAIRS, N_DIL_PAIRS * LANES
    nblk = s // BLK

    def body(q_in, k_in, v_in, c_ref, s1_ref, s2_ref, o_ref, lse_ref, q_ref, k_ref, v_ref, qs, ks, vs, od, ld, on, ln):
        row, lane = _iotas()
        _rotate_into(q_in, k_in, v_in, c_ref, s1_ref, s2_ref, q_ref, k_ref, v_ref)
        for pi, stride in enumerate(DIL_STRIDES):
            per = (s // stride) // BLK
            _deinterleave(qs, q_ref, stride, s, BF16)
            _deinterleave(ks, k_ref, stride, s, BF16)
            _deinterleave(vs, v_ref, stride, s, BF16)

            def block(b, carry):
                r0 = pl.multiple_of(b * BLK, BLK)
                rp = pl.multiple_of(jnp.maximum(b - 1, 0) * BLK, BLK)
                mc, mp = _band_masks(row, lane, b % per == 0)
                q = qs[pl.ds(r0, BLK), :]
                kc, kp, vc, vp = ks[pl.ds(r0, BLK), :], ks[pl.ds(rp, BLK), :], vs[pl.ds(r0, BLK), :], vs[pl.ds(rp, BLK), :]
                out = jnp.zeros((BLK, LANES), F32)
                lse = jnp.zeros((BLK, LANES), F32)
                for a in range(2):
                    hm = (lane < HEAD_DIM) if a == 0 else (lane >= HEAD_DIM)
                    qa = jnp.where(hm, q.astype(F32), 0.0).astype(BF16)
                    sc = jnp.where(mc, _dot_nt(qa, kc), NEG)
                    sp = jnp.where(mp, _dot_nt(qa, kp), NEG)
                    mx = jnp.maximum(jnp.max(sc, axis=1, keepdims=True), jnp.max(sp, axis=1, keepdims=True))
                    pc, pp = jnp.exp(sc - mx), jnp.exp(sp - mx)
                    l = jnp.sum(pc, axis=1, keepdims=True) + jnp.sum(pp, axis=1, keepdims=True)
                    oa = (_dot(pc.astype(BF16), vc) + _dot(pp.astype(BF16), vp)) / l
                    out = jnp.where(hm, oa, out)
                    lse = jnp.where(hm, mx + jnp.log(l), lse)
                od[pl.ds(r0, BLK), :] = out
                ld[pl.ds(r0, BLK), :] = lse
                return carry

            lax.fori_loop(0, nblk, block, 0, unroll=2)
            length = s // stride
            for r in range(stride):
                if stride == 1:
                    on[pi] = od[...]
                    ln[pi] = ld[...]
                else:
                    on[pi, pl.ds(r, length, stride=stride), :] = od[r * length:(r + 1) * length, :]
                    ln[pi, pl.ds(r, length, stride=stride), :] = ld[r * length:(r + 1) * length, :]

        def merge(n, carry):
            r0 = pl.multiple_of(n * BLK, BLK)
            ls = [ln[pi, pl.ds(r0, BLK), :] for pi in range(3)]
            mx = jnp.maximum(jnp.maximum(ls[0], ls[1]), ls[2])
            ws = [jnp.exp(lv - mx) for lv in ls]
            den = ws[0] + ws[1] + ws[2]
            num = ws[0] * on[0, pl.ds(r0, BLK), :] + ws[1] * on[1, pl.ds(r0, BLK), :] + ws[2] * on[2, pl.ds(r0, BLK), :]
            o_ref[pl.ds(r0, BLK), :] = num / den
            lse_ref[pl.ds(r0, BLK), :] = mx + jnp.log(den)
            return carry

        lax.fori_loop(0, nblk, merge, 0, unroll=2)

    colspec = pl.BlockSpec((s, LANES), lambda p: (0, p))
    out = jax.ShapeDtypeStruct((s, w), F32)
    scratch = ([pltpu.VMEM((s, LANES), F32)] * 3 + [pltpu.VMEM((s, LANES), BF16)] * 3 + [pltpu.VMEM((s, LANES), F32)] * 2
               + [pltpu.VMEM((3, s, LANES), F32)] * 2)
    ops, in_specs = _dilated_operands(qkv, tables)
    (o, lse), rode = call_with_rider(body, name, rider, ops, in_specs, [out, out], [colspec, colspec], scratch, (npairs,))
    return o, lse, rode


def dilated_bwd(qkv, tables, do, out, lse, do_off, name, rider=None):
    s = qkv.shape[0]
    npairs, w = N_DIL_PAIRS, N_DIL_PAIRS * LANES
    nblk = s // BLK

    def body(q_in, k_in, v_in, c_ref, s1_ref, s2_ref, do_ref, out_ref, lse_ref, dq_out, dk_out, dv_out,
             q_ref, k_ref, v_ref, dq_ref, dk_ref, dv_ref, qs, ks, vs, dos, dls, lss, dqd, dkd, dvd, dln):
        row, lane = _iotas()
        same_head = jnp.where((row < HEAD_DIM) == (lane < HEAD_DIM), 1.0, 0.0).astype(BF16)
        _rotate_into(q_in, k_in, v_in, c_ref, s1_ref, s2_ref, q_ref, k_ref, v_ref)

        def delta_blk(n, carry):
            r0 = pl.multiple_of(n * BLK, BLK)
            dln[pl.ds(r0, BLK), :] = _dot_exact_lhs(do_ref[pl.ds(r0, BLK), :] * out_ref[pl.ds(r0, BLK), :], same_head)
            return carry

        lax.fori_loop(0, nblk, delta_blk, 0, unroll=2)
        for pi, stride in enumerate(DIL_STRIDES):
            per = (s // stride) // BLK
            _deinterleave(qs, q_ref, stride, s, BF16)
            _deinterleave(ks, k_ref, stride, s, BF16)
            _deinterleave(vs, v_ref, stride, s, BF16)
            _deinterleave(dos, do_ref, stride, s, BF16)
            _deinterleave(dls, dln, stride, s, F32)
            _deinterleave(lss, lse_ref, stride, s, F32)

            def block(b, carry):
                r0 = pl.multiple_of(b * BLK, BLK)
                rp = pl.multiple_of(jnp.maximum(b - 1, 0) * BLK, BLK)
                first = b % per == 0
                mc, mp = _band_masks(row, lane, first)
                q, dov = qs[pl.ds(r0, BLK), :], dos[pl.ds(r0, BLK), :]
                kc, kp, vc, vp = ks[pl.ds(r0, BLK), :], ks[pl.ds(rp, BLK), :], vs[pl.ds(r0, BLK), :], vs[pl.ds(rp, BLK), :]
                lse_t, dl_t = lss[pl.ds(r0, BLK), :], dls[pl.ds(r0, BLK), :]
                dq = jnp.zeros((BLK, LANES), F32)
                dkc = jnp.zeros((BLK, LANES), F32)
                dkp = jnp.zeros((BLK, LANES), F32)
                dvc = jnp.zeros((BLK, LANES), F32)
                dvp = jnp.zeros((BLK, LANES), F32)
                for a in range(2):
                    hm = (lane < HEAD_DIM) if a == 0 else (lane >= HEAD_DIM)
                    pick = lane == a * HEAD_DIM
                    qa = jnp.where(hm, q.astype(F32), 0.0).astype(BF16)
                    doa = jnp.where(hm, dov.astype(F32), 0.0).astype(BF16)
                    lse_a = jnp.sum(jnp.where(pick, lse_t, 0.0), axis=1, keepdims=True)
                    dl_a = jnp.sum(jnp.where(pick, dl_t, 0.0), axis=1, keepdims=True)
                    pc = jnp.where(mc, jnp.exp(_dot_nt(qa, kc) - lse_a), 0.0)
                    pp = jnp.where(mp, jnp.exp(_dot_nt(qa, kp) - lse_a), 0.0)
                    dsc = (pc * (_dot_nt(doa, vc) - dl_a)).astype(BF16)
                    dsp = (pp * (_dot_nt(doa, vp) - dl_a)).astype(BF16)
                    dq = jnp.where(hm, _dot(dsc, kc) + _dot(dsp, kp), dq)
                    dkc += _dot_tn(dsc, qa)
                    dkp += _dot_tn(dsp, qa)
                    dvc += _dot_tn(pc.astype(BF16), doa)
                    dvp += _dot_tn(pp.astype(BF16), doa)
                dqd[pl.ds(r0, BLK), :] = dq
                dkd[pl.ds(r0, BLK), :] = dkc
                dvd[pl.ds(r0, BLK), :] = dvc

                @pl.when(jnp.logical_not(first))
                def _():
                    dkd[pl.ds(rp, BLK), :] += dkp
                    dvd[pl.ds(rp, BLK), :] += dvp

                return carry

            lax.fori_loop(0, nblk, block, 0, unroll=2)
            length = s // stride
            for dst, src in ((dq_ref, dqd), (dk_ref, dkd), (dv_ref, dvd)):
                for r in range(stride):
                    if stride == 1:
                        dst[...] = src[...]
                    else:
                        dst[pl.ds(r, length, stride=stride), :] += src[r * length:(r + 1) * length, :]

        c, s1, s2 = c_ref[...], s1_ref[...], s2_ref[...]
        rot_t = lambda dy: dy * c + pltpu.roll(dy * s1, ROT_HALF, 1) + pltpu.roll(dy * s2, LANES - ROT_HALF, 1)
        dq_out[...] = (rot_t(dq_ref[...]) * 0.125).astype(BF16)
        dk_out[...] = rot_t(dk_ref[...]).astype(BF16)
        dv_out[...] = dv_ref[...].astype(BF16)

    colspec = pl.BlockSpec((s, LANES), lambda p: (0, p))
    do_spec = pl.BlockSpec((s, LANES), lambda p: (0, do_off + p))
    o3 = jax.ShapeDtypeStruct((s, w), BF16)
    scratch = [pltpu.VMEM((s, LANES), F32)] * 6 + [pltpu.VMEM((s, LANES), BF16)] * 4 + [pltpu.VMEM((s, LANES), F32)] * 6
    ops, in_specs = _dilated_operands(qkv, tables)
    outs, rode = call_with_rider(body, name, rider, ops + [do, out, lse], in_specs + [do_spec, colspec, colspec],
                                 [o3, o3, o3], [colspec, colspec, colspec], scratch, (npairs,))
    return (*outs, rode)


def adamw(w, g, m, v, name):
    rows, cols = w.shape
    rb = min(rows, ROWS)
    c1 = 1.0 - ADAM_B1 ** ADAM_STEP
    c2 = 1.0 - ADAM_B2 ** ADAM_STEP

    def body(w_ref, g_ref, m_ref, v_ref, d_ref, mo_ref, vo_ref):
        gv = g_ref[...]
        mn = ADAM_B1 * m_ref[...] + (1.0 - ADAM_B1) * gv
        vn = ADAM_B2 * v_ref[...] + (1.0 - ADAM_B2) * (gv * gv)
        d_ref[...] = -ADAM_LR * ((mn / c1) / (jnp.sqrt(vn / c2) + ADAM_EPS) + ADAM_WD * w_ref[...])
        mo_ref[...] = mn
        vo_ref[...] = vn

    spec = _row_spec(cols, rb)
    out = jax.ShapeDtypeStruct((rows, cols), F32)
    return pl.pallas_call(
        body, name=name, out_shape=(out, out, out), grid=(rows // rb,), in_specs=[spec] * 4, out_specs=(spec,) * 3,
        compiler_params=_params(("parallel",)),
    )(w, g, m, v)


def _prefetch_call(body, name, scalar, ops, grid, in_specs, out_specs, out_shape, sem):
    spec = pltpu.PrefetchScalarGridSpec(num_scalar_prefetch=1, grid=grid, in_specs=in_specs, out_specs=out_specs)
    return pl.pallas_call(body, name=name, grid_spec=spec, out_shape=out_shape, compiler_params=_params(sem))(scalar, *ops)


def pair_sums(gs, gots, core, name):
    n = len(gs)

    def body(core_ref, *refs):
        for f in range(n):
            refs[2 * n + f][...] = (refs[f][...].astype(F32) + refs[n + f][...].astype(F32)).astype(BF16)

    blk = lambda g, rows_of: pl.BlockSpec((None, g.shape[1] // 2, g.shape[2]), rows_of)
    mine = [blk(g, lambda j, core_ref: (j, core_ref[0], 0)) for g in gs]
    half = [blk(g, lambda j, core_ref: (j, 0, 0)) for g in gs]
    outs = tuple(jax.ShapeDtypeStruct((g.shape[0], g.shape[1] // 2, g.shape[2]), BF16) for g in gs)
    return _prefetch_call(body, name, core, (*gs, *gots), (N_CHIPS,), mine + half, tuple(half), outs, ("parallel",))


def chip_sums(pairs, gots, chip, layers, intos, name):
    n = len(pairs)

    def body(chip_ref, *refs):
        for f in range(n):
            p_ref, a_ref, b_ref, c_ref = refs[4 * f:4 * f + 4]
            refs[5 * n + f][...] = ((p_ref[...].astype(F32) + a_ref[...].astype(F32)) + b_ref[...].astype(F32)) + c_ref[...].astype(F32)

    in_specs, ops = [], []
    for p, got in zip(pairs, gots):
        blk = lambda at, p=p: pl.BlockSpec((None,) + p.shape[1:], at)
        in_specs += [blk(lambda i, chip_ref: (chip_ref[0], 0, 0))] + [blk(lambda i, chip_ref, k=k: (k, 0, 0)) for k in range(3)]
        ops += [p, got, got, got]
    out_specs = tuple(pl.BlockSpec((None,) + p.shape[1:], lambda i, chip_ref, l=l: (l, 0, 0)) for p, l in zip(pairs, layers))
    spec = pltpu.PrefetchScalarGridSpec(num_scalar_prefetch=1, grid=(1,), in_specs=in_specs + [_ANY] * n, out_specs=out_specs)
    return pl.pallas_call(
        body, name=name, grid_spec=spec, out_shape=tuple(jax.ShapeDtypeStruct(t.shape, t.dtype) for t in intos),
        input_output_aliases={1 + 4 * n + f: f for f in range(n)}, compiler_params=_params(("arbitrary",)),
    )(chip, *ops, *intos)


def adamw_family(w, m, v, g_mine, g_other, core, name):
    nl, r, c = w.shape
    gc = g_mine.shape[2]
    rh = r // 2
    nb = 4 if rh % 512 == 0 else (2 if rh % 16 == 0 and rh > 256 else 1)
    rb = rh // nb
    c1 = 1.0 - ADAM_B1 ** ADAM_STEP
    c2 = 1.0 - ADAM_B2 ** ADAM_STEP

    def body(core_ref, w_ref, m_ref, v_ref, gm_ref, go_ref, g_ref, d_ref, mo_ref, vo_ref):
        gv = jnp.where(pl.program_id(1) == core_ref[0], gm_ref[...], go_ref[...])[:, :c]
        mn = ADAM_B1 * m_ref[...] + (1.0 - ADAM_B1) * gv
        vn = ADAM_B2 * v_ref[...] + (1.0 - ADAM_B2) * (gv * gv)
        g_ref[...] = gv
        d_ref[...] = -ADAM_LR * ((mn / c1) / (jnp.sqrt(vn / c2) + ADAM_EPS) + ADAM_WD * w_ref[...])
        mo_ref[...] = mn
        vo_ref[...] = vn

    full = pl.BlockSpec((None, rb, c), lambda l, h, i, core_ref: (l, h * nb + i, 0))
    mine = pl.BlockSpec((None, rb, gc), lambda l, h, i, core_ref: (l, jnp.where(h == core_ref[0], i, 0), 0))
    other = pl.BlockSpec((None, rb, gc), lambda l, h, i, core_ref: (l, jnp.where(h == core_ref[0], 0, i), 0))
    out = jax.ShapeDtypeStruct((nl, r, c), F32)
    return _prefetch_call(body, name, core, (w, m, v, g_mine, g_other), (nl, 2, nb), [full, full, full, mine, other],
                          (full, full, full, full), (out, out, out, out), ("parallel", "parallel", "parallel"))


def _coords():
    return lax.axis_index("x"), lax.axis_index("y"), lax.axis_index("c")


def _other_chips(x, y):
    return ((1 - x, 1 - y), (1 - x, y), (x, 1 - y))


_ANY = pl.BlockSpec(memory_space=pl.ANY)


def _exchange_call(body, name, arrays, out_shapes, n_copies, n_local=0):
    n = len(arrays)

    def wrapped(*refs):
        body(refs[:n], refs[n:n + len(out_shapes)], *refs[n + len(out_shapes):])

    scratch = [pltpu.SemaphoreType.DMA((n_copies,)), pltpu.SemaphoreType.DMA((n_copies,))]
    if n_local:
        scratch.append(pltpu.SemaphoreType.DMA((n_local,)))
    return pl.pallas_call(
        wrapped, name=name, out_shape=tuple(out_shapes), in_specs=[_ANY] * n, out_specs=tuple([_ANY] * len(out_shapes)),
        scratch_shapes=scratch, compiler_params=_params(),
    )(*arrays)


def _remote(send_sems, recv_sems, n, src, dst, to):
    return pltpu.make_async_remote_copy(src_ref=src, dst_ref=dst, send_sem=send_sems.at[n], recv_sem=recv_sems.at[n],
                                        device_id=to, device_id_type=MESH)


class Rider:
    def __init__(self, arrays, out_shapes, n_remote, n_local, copies, then=None):
        self.arrays, self.out_shapes, self.n_remote, self.n_local = list(arrays), list(out_shapes), n_remote, n_local
        self.copies, self.then = copies, then

    def sems(self):
        return [pltpu.SemaphoreType.DMA((self.n_remote,)), pltpu.SemaphoreType.DMA((self.n_remote,)),
                pltpu.SemaphoreType.DMA((max(self.n_local, 1),))]

    def run(self, name):
        n, no = len(self.arrays), len(self.out_shapes)

        def body(*refs):
            for stage in (self.copies, self.then):
                if stage is not None:
                    cps = stage(refs[:n], refs[n:n + no], *refs[n + no:])
                    for cp in cps:
                        cp.start()
                    for cp in cps:
                        cp.wait()

        return pl.pallas_call(
            body, name=name, out_shape=tuple(self.out_shapes), in_specs=[_ANY] * n, out_specs=tuple([_ANY] * no),
            scratch_shapes=self.sems(), compiler_params=_params(),
        )(*self.arrays)


def ride(rider, body, n_in, n_out, grid):
    if rider is None:
        return body
    ni, no = len(rider.arrays), len(rider.out_shapes)

    def wrapped(*refs):
        ins, r_in = refs[:n_in], refs[n_in:n_in + ni]
        outs = refs[n_in + ni:n_in + ni + n_out]
        r_out = refs[n_in + ni + n_out:n_in + ni + n_out + no]
        rest = refs[n_in + ni + n_out + no:]
        scratch, sems = rest[:len(rest) - 3], rest[len(rest) - 3:]
        step, total = 0, 1
        for a, g in enumerate(grid):
            step, total = step * g + pl.program_id(a), total * g
        assert total >= 3
        relay_at = (7 * total) // 8 if rider.then is not None else total - 1

        @pl.when(step == 0)
        def _():
            for cp in rider.copies(r_in, r_out, *sems):
                cp.start()

        body(*ins, *outs, *scratch)

        @pl.when(step == relay_at)
        def _():
            for cp in rider.copies(r_in, r_out, *sems):
                cp.wait()
            if rider.then is not None:
                for cp in rider.then(r_in, r_out, *sems):
                    cp.start()

        if rider.then is not None:
            @pl.when(step == total - 1)
            def _():
                for cp in rider.then(r_in, r_out, *sems):
                    cp.wait()

    return wrapped


def call_with_rider(body, name, rider, ops, in_specs, out_shape, out_specs, scratch, grid):
    n_in, n_out = len(ops), len(out_shape)
    ops, in_specs, out_shape, out_specs, scratch = list(ops), list(in_specs), list(out_shape), list(out_specs), list(scratch)
    if rider is not None:
        ops += rider.arrays
        in_specs += [_ANY] * len(rider.arrays)
        out_shape += rider.out_shapes
        out_specs += [_ANY] * len(rider.out_shapes)
        scratch += rider.sems()
    res = pl.pallas_call(
        ride(rider, body, n_in, n_out, grid), name=name, out_shape=tuple(out_shape), grid=grid, in_specs=in_specs,
        out_specs=tuple(out_specs), scratch_shapes=scratch, compiler_params=_params(("arbitrary",) * len(grid)),
    )(*ops)
    return tuple(res[:n_out]), list(res[n_out:])


def gather_rider(shards):
    nf = len(shards)
    half = lambda ref, which: pl.ds(which * (ref.shape[-2] // 2), ref.shape[-2] // 2)

    def copies(s_refs, o_refs, send_sems, recv_sems, local_sems):
        x, y, c = _coords()
        me = 2 * x + y
        cps = [pltpu.make_async_copy(s_refs[f], o_refs[f].at[me], local_sems.at[f]) for f in range(nf)]
        for k, (px, py) in enumerate(_other_chips(x, y)):
            for f in range(nf):
                rows = half(s_refs[f], c)
                cps.append(_remote(send_sems, recv_sems, k * nf + f, s_refs[f].at[rows], o_refs[f].at[me, rows], (px, py, c)))
        return cps

    def relay(s_refs, o_refs, send_sems, recv_sems, local_sems):
        x, y, c = _coords()
        cps = []
        for k, (px, py) in enumerate(_other_chips(x, y)):
            for f in range(nf):
                landed = o_refs[f].at[2 * px + py, half(s_refs[f], c)]
                cps.append(_remote(send_sems, recv_sems, (3 + k) * nf + f, landed, landed, (x, y, 1 - c)))
        return cps

    return Rider(shards, [jax.ShapeDtypeStruct((N_CHIPS,) + sh.shape, sh.dtype) for sh in shards], 6 * nf, nf, copies, relay)


def scatter_rider(pairs):
    nf = len(pairs)

    def copies(p_refs, o_refs, send_sems, recv_sems, local_sems):
        x, y, c = _coords()
        cps = []
        for k, (px, py) in enumerate(_other_chips(x, y)):
            for f in range(nf):
                cps.append(_remote(send_sems, recv_sems, k * nf + f, p_refs[f].at[2 * px + py], o_refs[f].at[k], (px, py, c)))
        return cps

    return Rider(pairs, [jax.ShapeDtypeStruct((3,) + p.shape[1:], p.dtype) for p in pairs], 3 * nf, 0, copies)


def pair_swap(grads, name):
    def body(g_refs, o_refs, send_sems, recv_sems):
        x, y, c = _coords()
        cps = []
        for f, g_ref in enumerate(g_refs):
            rh = g_ref.shape[1] // 2
            cps.append(_remote(send_sems, recv_sems, f, g_ref.at[:, pl.ds((1 - c) * rh, rh), :], o_refs[f], (x, y, 1 - c)))
        for cp in cps:
            cp.start()
        for cp in cps:
            cp.wait()

    outs = [jax.ShapeDtypeStruct((g.shape[0], g.shape[1] // 2, g.shape[2]), g.dtype) for g in grads]
    return _exchange_call(body, name, grads, outs, len(grads))


def half_swap(halves, name):
    def body(h_refs, o_refs, send_sems, recv_sems):
        x, y, c = _coords()
        cps = [_remote(send_sems, recv_sems, f, h_ref, o_refs[f], (x, y, 1 - c)) for f, h_ref in enumerate(h_refs)]
        for cp in cps:
            cp.start()
        for cp in cps:
            cp.wait()

    return _exchange_call(body, name, halves, [jax.ShapeDtypeStruct(h.shape, h.dtype) for h in halves], len(halves))


def allsum_small(part, name):
    def body(p_ref, tot_ref, all_ref, send_sems, recv_sems):
        x, y, c = _coords()
        me, sibling = (x, y, c), (x, y, 1 - c)
        chips = _other_chips(x, y)

        def slot(px, py, pc):
            return all_ref.at[4 * px + 2 * py + pc]

        def copy(k, block, to, src=None):
            return pltpu.make_async_remote_copy(src_ref=slot(*block) if src is None else src, dst_ref=slot(*block),
                                                send_sem=send_sems.at[k], recv_sem=recv_sems.at[k], device_id=to, device_id_type=MESH)

        slot(*me)[...] = p_ref[...]
        first = [copy(0, me, sibling, src=p_ref)] + [copy(1 + j, me, (*chip, c), src=p_ref) for j, chip in enumerate(chips)]
        for cp in first:
            cp.start()
        passed = [copy(4 + j, (*chip, c), sibling) for j, chip in enumerate(chips)]
        for j, chip in enumerate(chips):
            copy(1 + j, (*chip, c), me).wait_recv()
            passed[j].start()
        copy(0, sibling, me).wait_recv()
        for j, chip in enumerate(chips):
            copy(4 + j, (*chip, 1 - c), me).wait_recv()
        for cp in first + passed:
            cp.wait_send()
        tot = all_ref[0]
        for d in range(1, 8):
            tot = tot + all_ref[d]
        tot_ref[...] = tot

    vm = pl.BlockSpec(memory_space=pltpu.VMEM)
    return pl.pallas_call(
        body, name=name, out_shape=jax.ShapeDtypeStruct(part.shape, F32), in_specs=[vm], out_specs=vm,
        scratch_shapes=[pltpu.VMEM((8,) + part.shape, F32), pltpu.SemaphoreType.DMA((7,)), pltpu.SemaphoreType.DMA((7,))],
        compiler_params=_params(),
    )(part)


QKVF_COLS = 772
QKVF_PAD = 896
FORWARD_CARRY = {0: (("fi0", "fo0"), ("qkv1", "o1", "fo1")), 1: (("fi1", "qkv2", "o2"), ()),
                 2: (("fi2", "fo2"), ("qkv3", "o3", "fo3")), 3: (("fi3",), ())}


def _tables_for(s):
    return _rot_tables(s)


def layer_families(layer):
    return (0, 1, layer // 2) if layer % 2 == 0 else (2, 3, layer // 2)


class GradientExchange:
    def __init__(self, family_layers):
        self.core = lax.axis_index("c").astype(jnp.int32).reshape(1)
        self.chip = (2 * lax.axis_index("x") + lax.axis_index("y")).astype(jnp.int32).reshape(1)
        self.family_layers = family_layers
        self.pairs, self.mine, self.pending = {}, {}, []

    def add(self, items, tag):
        gs = [g for _, _, g in items]
        sums = pair_sums(gs, pair_swap(gs, f"grad_pair_swap_{tag}"), self.core, f"grad_pair_sum_{tag}")
        for (fam, li, _), pair in zip(items, sums):
            self.pairs[(fam, li)] = pair
            self.pending.append((fam, li))

    def rider(self, only=None):
        keys = [k for k in self.pending if only is None or k in only]
        self.pending = [k for k in self.pending if k not in keys]
        return (scatter_rider([self.pairs[k] for k in keys]) if keys else None), keys

    def landed(self, keys, outs):
        batch = []
        for k, o in list(zip(keys, outs)) + [(None, None)]:
            if batch and (k is None or k[0] in [b[0][0] for b in batch]):
                ks = [b[0] for b in batch]
                pairs = [self.pairs[b] for b in ks]
                intos = [self.mine[fam] if fam in self.mine else jnp.zeros((self.family_layers[fam],) + p.shape[1:], F32)
                         for (fam, _), p in zip(ks, pairs)]
                sums = chip_sums(pairs, [b[1] for b in batch], self.chip, [li for _, li in ks], intos,
                                 "grad_chip_sum_" + "_".join(f"{f}{li}" for f, li in ks))
                self.mine.update({fam: t for (fam, _), t in zip(ks, sums)})
                batch = []
            if k is not None:
                batch.append((k, o))

    def finish(self, weights, moments1, moments2):
        last, keys = self.rider()
        if last is not None:
            self.landed(keys, last.run("grad_chip_scatter_last"))
        mine = [self.mine[fam] for fam in range(len(weights))]
        other = half_swap(mine, "grad_half_swap")
        return [adamw_family(w, m, v, gm, go, self.core, f"adamw_{f}")
                for f, (w, m, v, gm, go) in enumerate(zip(weights, moments1, moments2, mine, other))]


class KeepGradients:
    def __init__(self):
        self.grads = {}

    def add(self, items, tag):
        for fam, li, g in items:
            self.grads[(fam, li)] = g

    def rider(self, only=None):
        return None, []

    def landed(self, keys, outs):
        pass


def kernel(x, norm_mix, w_qkv_even, w_o_even, w_qkvf_odd, b_forget, w_o_odd, norm_ffn, w_ffn_in, w_ffn_out, norm_final, loss_target, m_norm_mix, m_w_qkv_even, m_w_o_even, m_w_qkvf_odd, m_b_forget, m_w_o_odd, m_norm_ffn, m_w_ffn_in, m_w_ffn_out, m_norm_final, v_norm_mix, v_w_qkv_even, v_w_o_even, v_w_qkvf_odd, v_b_forget, v_w_o_odd, v_norm_ffn, v_w_ffn_in, v_w_ffn_out, v_norm_final):
    w_shards = [w_qkv_even, w_o_even, w_qkvf_odd, w_o_odd, w_ffn_in, w_ffn_out]
    shards = [w.astype(BF16) for w in w_shards]
    tables = _tables_for(x.shape[1])
    bias_pad = jnp.pad(b_forget, ((0, 0), (0, LANES - N_HEADS)))

    mine = {}
    for layer in range(DEPTH):
        fam_qkv, fam_o, li = layer_families(layer)
        mine.update({f"qkv{layer}": shards[fam_qkv][li], f"o{layer}": shards[fam_o][li],
                     f"fi{layer}": shards[4][layer], f"fo{layer}": shards[5][layer]})
    fetch = lambda names: gather_rider([mine[n] for n in names])
    have = dict(zip(("qkv0", "o0"), fetch(("qkv0", "o0")).run("gather_first")))
    saved, cur = [], x[0]
    h1 = rmsnorm_fwd(cur, norm_mix[0:1], "l0_norm_mix")
    for layer in range(DEPTH):
        carry, side_carry = FORWARD_CARRY[layer]
        next_gain = norm_mix[layer + 1:layer + 2] if layer + 1 < DEPTH else None
        cur, h1, keep = forward_layer(layer, cur, h1, have, norm_ffn[layer:layer + 1], next_gain, tables,
                                      bias_pad[layer // 2:layer // 2 + 1], fetch, carry, side_carry)
        saved.append(keep)

    dcur, g_final, loss_part = loss_head(cur, norm_final.reshape(1, D_MODEL), loss_target[0], "loss_head")

    exchange = GradientExchange([w.shape[0] for w in w_shards])
    g_mix, g_ffn, g_bias = [None] * DEPTH, [None] * DEPTH, [None] * (DEPTH // 2)
    for layer in reversed(range(DEPTH)):
        dcur, g_mix[layer], g_ffn[layer], g_b = backward_layer(layer, dcur, saved[layer], norm_mix[layer:layer + 1],
                                                               norm_ffn[layer:layer + 1], tables, bias_pad[layer // 2:layer // 2 + 1], exchange)
        if g_b is not None:
            g_bias[layer // 2] = g_b

    zero_row = jnp.zeros((1, D_MODEL), F32)
    pad16 = lambda v: jnp.pad(v, (0, D_MODEL - v.shape[0]))[None, :]
    small_rows = lambda mix, ffn, fin, bias, last: jnp.concatenate(
        [r.reshape(1, D_MODEL) for r in mix] + [r.reshape(1, D_MODEL) for r in ffn] + [fin.reshape(1, D_MODEL)]
        + [pad16(b) for b in bias] + [last] + [zero_row] * (SMALL_ROWS - 12), axis=0)
    loss_row = pad16(loss_part[0, :1])
    small_g = allsum_small(small_rows(g_mix, g_ffn, g_final, g_bias, loss_row), "allsum_small")
    loss = small_g[11, 0]
    small_g = small_g.at[11].set(0.0)
    sw = small_rows(list(norm_mix), list(norm_ffn), norm_final, list(b_forget), zero_row)
    sm = small_rows(list(m_norm_mix), list(m_norm_ffn), m_norm_final, list(m_b_forget), zero_row)
    sv = small_rows(list(v_norm_mix), list(v_norm_ffn), v_norm_final, list(v_b_forget), zero_row)
    sd, snm, snv = adamw(sw, small_g, sm, sv, "adamw_small")

    def small_out(a):
        return a[0:4], a[8, :], a[9:11, :N_HEADS], a[4:8]

    big = exchange.finish(w_shards, [m_w_qkv_even, m_w_o_even, m_w_qkvf_odd, m_w_o_odd, m_w_ffn_in, m_w_ffn_out],
                          [v_w_qkv_even, v_w_o_even, v_w_qkvf_odd, v_w_o_odd, v_w_ffn_in, v_w_ffn_out])

    def outputs(small, which):
        mix, fin, bias, ffn = small_out(small)
        qkv_e, o_e, qkvf, o_o, fi, fo = [big[f][which] for f in range(6)]
        return [mix, qkv_e, o_e, qkvf, bias, o_o, ffn, fi, fo, fin]

    return (loss, dcur[None], *outputs(small_g, 0), *outputs(sd, 1), *outputs(snm, 2), *outputs(snv, 3))


def _chip_tile(rows, cols, at):
    return pl.BlockSpec((None, rows, cols), at)


def forward_layer(layer, cur, h1, have, ffn_gain, next_gain, tables, bias_row, fetch=None, carry=(), side_carry=()):
    n = f"l{layer}"
    s = cur.shape[0]
    w_qkv, w_o = have[f"qkv{layer}"], have[f"o{layer}"]
    rider = fetch(carry) if carry else None
    side_rider = fetch(side_carry) if side_carry else None
    keep = {"x": cur, "h1": h1, "w_o": w_o.reshape(D_ATTN, D_MODEL)}
    side = []
    if layer % 2 == 0:
        qkv = matmul(h1, w_qkv, "nn", BF16, n + "_qkv", 1024, 768, 1024, mnk=(s, 3 * D_ATTN, D_MODEL),
                     b_spec=_chip_tile(D_MODEL, 768, lambda i, j, kk: (j, 0, 0)))
        o_sb, st, rode = causal_fwd(qkv, 4, "sb", n + "_sb_fwd", rider=rider)
        o_dil, lse_dil, side = dilated_fwd(qkv, tables, n + "_dil_fwd", rider=side_rider)
        attn = jnp.concatenate([o_sb, o_dil], axis=1).astype(BF16)
        keep.update(o_dil=o_dil, lse_dil=lse_dil, w_qkv=w_qkv)
    else:
        natural = jnp.transpose(w_qkv, (1, 0, 2)).reshape(D_MODEL, N_CHIPS * QKVF_COLS)
        w_gate = jnp.pad(natural[:, 3 * D_ATTN:], ((0, 0), (0, LANES - N_HEADS)))
        qkv = matmul(h1, natural[:, :3 * D_ATTN], "nn", BF16, n + "_qkv", 1024, 768, 1024)
        fl = matmul(h1, w_gate, "nn", F32, n + "_fgate", 512, LANES, 1024)
        cum = forget_fwd(fl, bias_row, n + "_forget_fwd")
        f_heads = cum[:, :N_HEADS].T
        fq = jnp.broadcast_to(f_heads[:, :, None], (N_HEADS, s, LANES))
        fk = f_heads.reshape(N_HEADS // 2, 2, s)
        attn, st, rode = causal_fwd(qkv, 8, "fox", n + "_fox_fwd", fq=fq, fk=fk, rider=rider)
        attn = attn.astype(BF16)
        keep.update(fl=fl, fq=fq, fk=fk, w_qkv=jnp.concatenate([natural[:, :3 * D_ATTN], w_gate], axis=1))
    have.update(zip(carry, rode))
    have.update(zip(side_carry, side))
    w_fi, w_fo = have[f"fi{layer}"], have[f"fo{layer}"].reshape(D_FF, D_MODEL)
    mid, h2 = matmul(attn, keep["w_o"], "nn", F32, n + "_attn_out", 512, 1024, 1024, res=cur, norm_gain=ffn_gain)
    gate, up, act = ffn_in_swiglu(h2, w_fi, n + "_ffn_in")
    if next_gain is None:
        out, h_next = matmul(act, w_fo, "nn", F32, n + "_ffn_out", 512, 1024, D_FF, res=mid), None
    else:
        out, h_next = matmul(act, w_fo, "nn", F32, n + "_ffn_out", 512, 1024, D_FF, res=mid, norm_gain=next_gain)
    keep.update(qkv=qkv, st=st, attn=attn, mid=mid, h2=h2, gate=gate, up=up, act=act, w_fi=w_fi, w_fo=w_fo)
    return out, h_next, keep


def backward_layer(layer, dcur, kp, mix_gain, ffn_gain, tables, bias_row, exchange):
    n = f"l{layer}"
    s = dcur.shape[0]
    fam_qkv, fam_o, li = layer_families(layer)
    g_fo = matmul(kp["act"], dcur, "tn", BF16, n + "_d_w_ffn_out", 1408, 1024, s)
    dact = matmul(dcur, kp["w_fo"], "nt", BF16, n + "_d_act", 1024, 1408, 1024)
    dgu = swiglu_bwd(kp["gate"], kp["up"], dact, n + "_d_swiglu")
    g_fi = matmul(kp["h2"], dgu, "tn", BF16, n + "_d_w_ffn_in", 1024, 1408, 2048, mnk=(D_MODEL, 2 * D_FF, s),
                  o_spec=_chip_tile(D_MODEL, 1408, lambda i, j, kk: (j, 0, 0)), out_shape=(N_CHIPS, D_MODEL, 1408))
    dmid, g_ffn = dh_norm_bwd(dgu, kp["w_fi"], kp["mid"], ffn_gain, dcur, n + "_d_h2", 256)
    g_o = matmul(kp["attn"], dmid, "tn", BF16, n + "_d_w_o", 1024, 1024, s)
    dattn = matmul(dmid, kp["w_o"], "nt", F32, n + "_d_attn", 1024, 1024, 1024)
    exchange.add([(5, layer, g_fo.reshape(N_CHIPS, D_FF // N_CHIPS, D_MODEL)), (4, layer, g_fi),
                  (fam_o, li, g_o.reshape(N_CHIPS, D_ATTN // N_CHIPS, D_MODEL))], f"l{layer}_ffn")
    g_bias = None
    if layer % 2 == 0:
        rider, keys = exchange.rider(only=[(4, layer), (fam_o, li)])
        dq_a, dk_a, dv_a, rode = causal_bwd(kp["qkv"], dattn, kp["st"], 4, "sb", n + "_sb_bwd", rider=rider)
        exchange.landed(keys, rode)
        rider, keys = exchange.rider()
        dq_b, dk_b, dv_b, rode = dilated_bwd(kp["qkv"], tables, dattn, kp["o_dil"], kp["lse_dil"], 4, n + "_dil_bwd", rider=rider)
        dproj = jnp.concatenate([dq_a, dq_b, dk_a.astype(BF16), dk_b, dv_a.astype(BF16), dv_b], axis=1)
        g_qkv = matmul(kp["h1"], dproj, "tn", BF16, n + "_d_w_qkv", 1024, 768, 2048, mnk=(D_MODEL, 3 * D_ATTN, s),
                       o_spec=_chip_tile(D_MODEL, 768, lambda i, j, kk: (j, 0, 0)), out_shape=(N_CHIPS, D_MODEL, 768))
    else:
        rider, keys = exchange.rider()
        dq_f, dk_f, dv_f, dfk, rode = causal_bwd(kp["qkv"], dattn, kp["st"], 8, "fox", n + "_fox_bwd", fq=kp["fq"], fk=kp["fk"],
                                                 rider=rider)
        dcum = jnp.pad(dfk.reshape(N_HEADS, s).T, ((0, 0), (0, LANES - N_HEADS)))
        dfl, dbias = forget_bwd(kp["fl"], bias_row, dcum, n + "_forget_bwd")
        g_bias = dbias[0, :N_HEADS]
        dproj = jnp.concatenate([dq_f, dk_f.astype(BF16), dv_f.astype(BF16), dfl.astype(BF16)], axis=1)
        g_nat = matmul(kp["h1"], dproj, "tn", BF16, n + "_d_w_qkv", 1024, 640, 2048)
        g_qkv = g_nat[:, :N_CHIPS * QKVF_COLS].reshape(D_MODEL, N_CHIPS, QKVF_COLS)
        g_qkv = jnp.transpose(jnp.pad(g_qkv, ((0, 0), (0, 0), (0, QKVF_PAD - QKVF_COLS))), (1, 0, 2))
    exchange.landed(keys, rode)
    exchange.add([(fam_qkv, li, g_qkv)], f"l{layer}_qkv")
    dx, g_mix = dh_norm_bwd(dproj, kp["w_qkv"], kp["x"], mix_gain, dmid, n + "_d_h1", 512)
    return dx, g_mix, g_ffn, g_bias


def local_step(xs, target, norm_mix, norm_ffn, norm_final, b_forget, layer_weights):
    tables = _tables_for(xs.shape[0])
    bias_pad = jnp.pad(b_forget, ((0, 0), (0, LANES - N_HEADS)))
    saved, cur, have = [], xs, {}
    h1 = rmsnorm_fwd(cur, norm_mix[0:1], "l0_norm_mix")
    for layer in range(DEPTH):
        have.update(zip((f"qkv{layer}", f"o{layer}", f"fi{layer}", f"fo{layer}"), layer_weights[layer]))
        next_gain = norm_mix[layer + 1:layer + 2] if layer + 1 < DEPTH else None
        cur, h1, keep = forward_layer(layer, cur, h1, have, norm_ffn[layer:layer + 1], next_gain, tables,
                                      bias_pad[layer // 2:layer // 2 + 1])
        saved.append(keep)
    dcur, g_final, loss_part = loss_head(cur, norm_final.reshape(1, D_MODEL), target, "loss_head")
    keeper = KeepGradients()
    g_mix, g_ffn, g_bias = [None] * DEPTH, [None] * DEPTH, [None] * (DEPTH // 2)
    for layer in reversed(range(DEPTH)):
        dcur, g_mix[layer], g_ffn[layer], g_b = backward_layer(layer, dcur, saved[layer], norm_mix[layer:layer + 1],
                                                               norm_ffn[layer:layer + 1], tables, bias_pad[layer // 2:layer // 2 + 1], keeper)
        if g_b is not None:
            g_bias[layer // 2] = g_b
    return dcur, keeper.grads, (g_mix, g_ffn, g_final, g_bias), loss_part
```

```python
import functools

import jax
import jax.numpy as jnp
from jax import lax
from jax.experimental import pallas as pl
from jax.experimental.pallas import tpu as pltpu

F32 = jnp.float32
BF16 = jnp.bfloat16
MESH = pl.DeviceIdType.MESH

D_MODEL = 1024
DEPTH = 4
HEAD_DIM = 64
N_HEADS = 16
D_ATTN = 1024
D_FF = 2816
ROPE_THETA = 500000.0
ROT_HALF = 8
RMS_EPS = 1e-5
DIL_STRIDES = (1, 4, 16)
ADAM_LR, ADAM_B1, ADAM_B2, ADAM_EPS, ADAM_WD, ADAM_STEP = 0.001, 0.9, 0.999, 1e-8, 0.01, 10

LANES = 128
BLK = 128
VMEM_LIMIT = 56 * 1024 * 1024
NEG = -1e30
N_CHIPS = 4
SMALL_ROWS = 16


def _params(sem=None):
    return pltpu.CompilerParams(dimension_semantics=sem, vmem_limit_bytes=VMEM_LIMIT)


def _dot(a, b):
    return lax.dot_general(a, b, (((1,), (0,)), ((), ())), preferred_element_type=F32)


def _dot_nt(a, b):
    return lax.dot_general(a, b, (((1,), (1,)), ((), ())), preferred_element_type=F32)


def _dot_tn(a, b):
    return lax.dot_general(a, b, (((0,), (0,)), ((), ())), preferred_element_type=F32)


def _split3(x):
    x1 = x.astype(BF16)
    r1 = x - x1.astype(F32)
    x2 = r1.astype(BF16)
    x3 = (r1 - x2.astype(F32)).astype(BF16)
    return x1, x2, x3


def _dot_exact_lhs(x, t):
    x1, x2, x3 = _split3(x)
    return _dot(x1, t) + _dot(x2, t) + _dot(x3, t)


def _dot_exact_rhs(t, x):
    x1, x2, x3 = _split3(x)
    return _dot(t, x1) + _dot(t, x2) + _dot(t, x3)


def _iotas(shape=(BLK, LANES)):
    return lax.broadcasted_iota(jnp.int32, shape, 0), lax.broadcasted_iota(jnp.int32, shape, 1)


_DIMS = {"nn": (((1,), (0,)), ((), ())), "nt": (((1,), (1,)), ((), ())), "tn": (((0,), (0,)), ((), ()))}


def matmul(a, b, mode, out_dtype, name, tm, tn, tk, res=None, mnk=None, b_spec=None, o_spec=None, out_shape=None, norm_gain=None):
    if mnk is not None:
        m, n, k = mnk
    elif mode == "nn":
        (m, k), (k2, n) = a.shape, b.shape
    elif mode == "nt":
        (m, k), (n, k2) = a.shape, b.shape
    else:
        (k, m), (k2, n) = a.shape, b.shape
    assert m % tm == 0 and n % tn == 0 and k % tk == 0, (name, a.shape, b.shape)
    nk = k // tk
    a_spec = pl.BlockSpec((tk, tm), lambda i, j, kk: (kk, i)) if mode == "tn" else pl.BlockSpec((tm, tk), lambda i, j, kk: (i, kk))
    if b_spec is None:
        b_spec = pl.BlockSpec((tn, tk), lambda i, j, kk: (j, kk)) if mode == "nt" else pl.BlockSpec((tk, tn), lambda i, j, kk: (kk, j))
    r_spec = pl.BlockSpec((tm, tn), lambda i, j, kk: (i, j))
    if o_spec is None:
        o_spec = r_spec
    dims = _DIMS[mode]
    has_res, has_norm = res is not None, norm_gain is not None
    assert not has_norm or (tn == n and nk == 1)
    n_in = 2 + int(has_res) + int(has_norm)

    def body(*refs):
        a_ref, b_ref = refs[0], refs[1]
        r_ref = refs[2] if has_res else None
        o_ref = refs[n_in]

        def finish(v):
            if has_res:
                v = v + r_ref[...]
            o_ref[...] = v.astype(out_dtype)
            if has_norm:
                rstd = lax.rsqrt(jnp.mean(v * v, axis=-1, keepdims=True) + RMS_EPS)
                refs[n_in + 1][...] = (v * rstd * refs[n_in - 1][...]).astype(BF16)

        bv = b_ref[...]
        if bv.ndim == 3:
            bv = jnp.concatenate([bv[j] for j in range(bv.shape[0])], axis=1)
        p = lax.dot_general(a_ref[...].astype(BF16), bv.astype(BF16), dims, preferred_element_type=F32)
        if nk == 1:
            finish(p)
        else:
            acc = refs[-1]
            kk = pl.program_id(2)

            @pl.when(kk == 0)
            def _():
                acc[...] = p

            @pl.when(kk > 0)
            def _():
                acc[...] += p

            @pl.when(kk == nk - 1)
            def _():
                finish(acc[...])

    ops = [a, b] + ([res] if has_res else []) + ([norm_gain] if has_norm else [])
    specs = [a_spec, b_spec] + ([r_spec] if has_res else []) + ([pl.BlockSpec((1, tn), lambda i, j, kk: (0, j))] if has_norm else [])
    out_shape = jax.ShapeDtypeStruct((m, n) if out_shape is None else out_shape, out_dtype)
    return pl.pallas_call(
        body, name=name, out_shape=(out_shape, jax.ShapeDtypeStruct((m, n), BF16)) if has_norm else out_shape,
        grid=(m // tm, n // tn, nk), in_specs=specs, out_specs=(o_spec, r_spec) if has_norm else o_spec,
        scratch_shapes=[pltpu.VMEM((tm, tn), F32)] if nk > 1 else [],
        compiler_params=_params(("parallel", "parallel", "arbitrary")),
    )(*ops)


ROWS = 256


def _row_spec(cols, rows=ROWS):
    return pl.BlockSpec((rows, cols), lambda i: (i, 0))


def _fix_spec(r, cols):
    return pl.BlockSpec((r, cols), lambda i: (0, 0))


def rmsnorm_fwd(x, g, name):
    s, d = x.shape

    def body(x_ref, g_ref, h_ref):
        xv = x_ref[...]
        rstd = lax.rsqrt(jnp.mean(xv * xv, axis=-1, keepdims=True) + RMS_EPS)
        h_ref[...] = (xv * rstd * g_ref[...]).astype(BF16)

    return pl.pallas_call(
        body, name=name, out_shape=jax.ShapeDtypeStruct((s, d), BF16), grid=(s // ROWS,),
        in_specs=[_row_spec(d), _fix_spec(1, d)], out_specs=_row_spec(d), compiler_params=_params(("parallel",)),
    )(x, g)


def _rms_bwd_math(xv, gv, dh):
    rstd = lax.rsqrt(jnp.mean(xv * xv, axis=-1, keepdims=True) + RMS_EPS)
    xhat = xv * rstd
    u = dh * gv
    dx = rstd * (u - xhat * jnp.mean(u * xhat, axis=-1, keepdims=True))
    return dx, dh * xhat


def dh_norm_bwd(dy, w, x, g, dres, name, tm):
    s, k = dy.shape
    d = x.shape[1]

    def body(dy_ref, w_ref, x_ref, g_ref, dres_ref, dx_ref, dg_ref):
        wv = w_ref[...]
        if wv.ndim == 3:
            wv = jnp.concatenate([wv[j] for j in range(wv.shape[0])], axis=1)
        dx, dgt = _rms_bwd_math(x_ref[...], g_ref[...], _dot_nt(dy_ref[...], wv))
        dx_ref[...] = dres_ref[...] + dx
        part = jnp.sum(dgt, axis=0, keepdims=True)

        @pl.when(pl.program_id(0) == 0)
        def _():
            dg_ref[...] = part

        @pl.when(pl.program_id(0) > 0)
        def _():
            dg_ref[...] += part

    w_spec = pl.BlockSpec(w.shape, lambda i: (0,) * w.ndim)
    return pl.pallas_call(
        body, name=name, out_shape=(jax.ShapeDtypeStruct((s, d), F32), jax.ShapeDtypeStruct((1, d), F32)), grid=(s // tm,),
        in_specs=[pl.BlockSpec((tm, k), lambda i: (i, 0)), w_spec, _row_spec(d, tm), _fix_spec(1, d), _row_spec(d, tm)],
        out_specs=(_row_spec(d, tm), _fix_spec(1, d)), compiler_params=_params(("arbitrary",)),
    )(dy, w, x, g, dres)


def loss_head(x, g, target, name):
    s, d = x.shape

    def body(x_ref, g_ref, t_ref, dx_ref, dg_ref, loss_ref):
        xv, gv = x_ref[...], g_ref[...]
        rstd = lax.rsqrt(jnp.mean(xv * xv, axis=-1, keepdims=True) + RMS_EPS)
        err = xv * rstd * gv - t_ref[...]
        dx, dgt = _rms_bwd_math(xv, gv, err * (1.0 / d))
        dx_ref[...] = dx
        part = jnp.sum(dgt, axis=0, keepdims=True)
        lpart = jnp.full((1, LANES), 0.5 / d, F32) * jnp.sum(err * err)

        @pl.when(pl.program_id(0) == 0)
        def _():
            dg_ref[...] = part
            loss_ref[...] = lpart

        @pl.when(pl.program_id(0) > 0)
        def _():
            dg_ref[...] += part
            loss_ref[...] += lpart

    return pl.pallas_call(
        body, name=name,
        out_shape=(jax.ShapeDtypeStruct((s, d), F32), jax.ShapeDtypeStruct((1, d), F32), jax.ShapeDtypeStruct((1, LANES), F32)),
        grid=(s // ROWS,), in_specs=[_row_spec(d), _fix_spec(1, d), _row_spec(d)],
        out_specs=(_row_spec(d), _fix_spec(1, d), _fix_spec(1, LANES)), compiler_params=_params(("arbitrary",)),
    )(x, g, target)


def ffn_in_swiglu(h, w_in, name, tm=1024, rider=None):
    s, d = h.shape
    cols = w_in.shape[2]

    def body(h_ref, wg_ref, wu_ref, g_ref, u_ref, a_ref):
        hv = h_ref[...]
        gv, uv = _dot(hv, wg_ref[...]), _dot(hv, wu_ref[...])
        g_ref[...] = gv.astype(BF16)
        u_ref[...] = uv.astype(BF16)
        a_ref[...] = (gv * (1.0 / (1.0 + jnp.exp(-gv))) * uv).astype(BF16)

    tile = pl.BlockSpec((tm, cols), lambda i, j: (i, j))
    out = jax.ShapeDtypeStruct((s, 2 * cols), BF16)
    in_specs = [pl.BlockSpec((tm, d), lambda i, j: (i, 0)), pl.BlockSpec((None, d, cols), lambda i, j: (j, 0, 0)),
                pl.BlockSpec((None, d, cols), lambda i, j: (j + 2, 0, 0))]
    outs, rode = call_with_rider(body, name, rider, [h, w_in, w_in], in_specs, [out, out, out], [tile, tile, tile], [], (s // tm, 2))
    return (*outs, rode)


def swiglu_bwd(gate, up, dact, name):
    s, f = gate.shape

    def body(g_ref, u_ref, da_ref, o_ref):
        gv, uv, da = g_ref[...].astype(F32), u_ref[...].astype(F32), da_ref[...].astype(F32)
        sg = 1.0 / (1.0 + jnp.exp(-gv))
        o_ref[:, :f] = (da * uv * sg * (1.0 + gv * (1.0 - sg))).astype(BF16)
        o_ref[:, f:] = (da * gv * sg).astype(BF16)

    return pl.pallas_call(
        body, name=name, out_shape=jax.ShapeDtypeStruct((s, 2 * f), BF16), grid=(s // ROWS,),
        in_specs=[_row_spec(f)] * 3, out_specs=_row_spec(2 * f), compiler_params=_params(("parallel",)),
    )(gate, up, dact)


Q_OFF, K_OFF, V_OFF = 0, 8, 16


KB = 512
BQ = 512
SUB = KB // BLK


def _softplus_parts(z):
    sp = jnp.log(1.0 + jnp.exp(-jnp.abs(z)))
    ls = jnp.minimum(z, 0.0) - sp
    return ls, ls - z


def _wide(t):
    return jnp.concatenate([t] * SUB, axis=1)


def _chunk_dots(x, tri):
    terms = []
    for u in range(SUB):
        xu = x[:, u * BLK:(u + 1) * BLK]
        hi = xu.astype(BF16)
        terms += [hi, (xu - hi.astype(F32)).astype(BF16)]
    r = _dot(jnp.concatenate(terms, axis=0), tri)
    rows = x.shape[0]
    piece = lambda n: r[n * rows:(n + 1) * rows]
    return [piece(2 * u) + piece(2 * u + 1) for u in range(SUB)]


def _block_suffix_sums(x, suffix, c):
    loc = _chunk_dots(x, suffix)
    out = [None] * SUB
    for u in reversed(range(SUB)):
        out[u] = loc[u] + c
        c = c + jnp.sum(x[:, u * BLK:(u + 1) * BLK], axis=1, keepdims=True)
    return jnp.concatenate(out, axis=1), c


def _block_prefix_sums(x, tri, c):
    loc = _chunk_dots(x, tri)
    out = []
    for u in range(SUB):
        out.append(loc[u] + c)
        c = c + jnp.sum(x[:, u * BLK:(u + 1) * BLK], axis=1, keepdims=True)
    return jnp.concatenate(out, axis=1), c


def causal_fwd(qkv, npairs, mode, name, fq=None, fk=None, rider=None):
    s = qkv.shape[0]
    nq = s // BQ
    fox = mode == "fox"

    def body(*refs):
        if fox:
            q_ref, k_ref, v_ref, fq_ref, fk_ref, o_ref, st_ref = refs
        else:
            q_ref, k_ref, v_ref, o_ref, st_ref = refs
        i = pl.program_id(1)
        nkb = (i * BQ + BQ - 1) // KB + 1
        row, lane = _iotas((BQ, KB))
        row_s, lane_s = _iotas()
        _, lane_q = _iotas((BQ, LANES))
        nfull = (i * BQ) // KB
        qpos = i * BQ + row
        qf = q_ref[...].astype(F32) * 0.125
        hms = (lane_q < HEAD_DIM, lane_q >= HEAD_DIM)
        qas = [jnp.where(hm, qf, 0.0).astype(BF16) for hm in hms]
        suffix = jnp.where(row_s > lane_s, 1.0, 0.0).astype(BF16)
        zero = jnp.zeros((BQ, LANES), F32)
        col0 = jnp.zeros((BQ, 1), F32)

        def kv(j):
            r0 = pl.multiple_of(j * KB, KB)
            return r0, k_ref[pl.ds(r0, KB), :], v_ref[pl.ds(r0, KB), :]

        if fox:
            fqs = [_wide(fq_ref[a]) for a in range(2)]

            def step(j, carry, masked):
                r0, kb, vb = kv(j)
                new = []
                for a in range(2):
                    acc, mx, l = carry[3 * a:3 * a + 3]
                    z = _dot_nt(qas[a], kb) + fqs[a] - fk_ref[a:a + 1, pl.ds(r0, KB)]
                    if masked:
                        z = jnp.where(r0 + lane <= qpos, z, NEG)
                    mnew = jnp.maximum(mx, jnp.max(z, axis=1, keepdims=True))
                    p = jnp.exp(z - mnew)
                    alpha = jnp.exp(mx - mnew)
                    new += [alpha * acc + _dot(p.astype(BF16), vb), mnew, alpha * l + jnp.sum(p, axis=1, keepdims=True)]
                return tuple(new)

            neg = jnp.full((BQ, 1), NEG, F32)
            res = lax.fori_loop(0, nfull, functools.partial(step, masked=False), (zero, neg, col0, zero, neg, col0))
            res = lax.fori_loop(nfull, nkb, functools.partial(step, masked=True), res)
            outs = [res[3 * a] / res[3 * a + 2] for a in range(2)]
            stats = [res[3 * a + 1] + jnp.log(res[3 * a + 2]) for a in range(2)]
        else:
            def step(j, carry, masked):
                r0, kb, vb = kv(j)
                strict = r0 + lane < qpos
                new = []
                for a in range(2):
                    acc, c = carry[2 * a:2 * a + 2]
                    ls, lm = _softplus_parts(_dot_nt(qas[a], kb))
                    if masked:
                        lm = jnp.where(strict, lm, 0.0)
                    between, c = _block_suffix_sums(lm, suffix, c)
                    aw = jnp.exp(ls + between)
                    if masked:
                        aw = jnp.where(strict, aw, 0.0)
                    new += [acc + _dot(aw.astype(BF16), vb), c]
                return tuple(new)

            res = lax.fori_loop(0, nkb - nfull, lambda jj, c: step(nkb - 1 - jj, c, True), (zero, col0, zero, col0))
            res = lax.fori_loop(0, nfull, lambda jj, c: step(nfull - 1 - jj, c, False), res)
            outs, stats = [res[0], res[2]], [res[1], res[3]]
        o_ref[...] = jnp.where(hms[0], outs[0], outs[1])
        for a in range(2):
            st_ref[a] = jnp.broadcast_to(stats[a], (BQ, LANES))

    col = lambda off: (lambda p, i: (0, off + p))
    in_specs = [pl.BlockSpec((BQ, LANES), lambda p, i: (i, Q_OFF + p)),
                pl.BlockSpec((s, LANES), col(K_OFF)), pl.BlockSpec((s, LANES), col(V_OFF))]
    ops = [qkv, qkv, qkv]
    if fox:
        in_specs += [pl.BlockSpec((2, BQ, LANES), lambda p, i: (p, i, 0)), pl.BlockSpec((None, 2, s), lambda p, i: (p, 0, 0))]
        ops += [fq, fk]
    (o, stat), rode = call_with_rider(
        body, name, rider, ops, in_specs,
        [jax.ShapeDtypeStruct((s, npairs * LANES), F32), jax.ShapeDtypeStruct((2 * npairs, s, LANES), F32)],
        [pl.BlockSpec((BQ, LANES), lambda p, i: (i, p)), pl.BlockSpec((2, BQ, LANES), lambda p, i: (p, i, 0))], [], (npairs, nq))
    return o, stat, rode


def causal_bwd(qkv, do, stat, npairs, mode, name, fq=None, fk=None, rider=None):
    s = qkv.shape[0]
    nq = s // BQ
    fox = mode == "fox"

    def body(*refs):
        if fox:
            q_ref, k_ref, v_ref, do_ref, st_ref, fq_ref, fk_ref, dq_ref, dk_ref, dv_ref, df_ref, p_s, dp_s = refs
        else:
            q_ref, k_ref, v_ref, do_ref, st_ref, dq_ref, dk_ref, dv_ref = refs
        i = pl.program_id(1)

        @pl.when(i == 0)
        def _():
            dk_ref[...] = jnp.zeros_like(dk_ref)
            dv_ref[...] = jnp.zeros_like(dv_ref)
            if fox:
                df_ref[...] = jnp.zeros_like(df_ref)

        nkb = (i * BQ + BQ - 1) // KB + 1
        nfull = (i * BQ) // KB
        row, lane = _iotas((BQ, KB))
        row_s, lane_s = _iotas()
        _, lane_q = _iotas((BQ, LANES))
        qpos = i * BQ + row
        qf = q_ref[...].astype(F32) * 0.125
        dov = do_ref[...]
        hms = (lane_q < HEAD_DIM, lane_q >= HEAD_DIM)
        qas = [jnp.where(hm, qf, 0.0).astype(BF16) for hm in hms]
        doas = [jnp.where(hm, dov, 0.0).astype(BF16) for hm in hms]
        stas = [_wide(st_ref[a]) for a in range(2)]
        zero = jnp.zeros((BQ, LANES), F32)
        col0 = jnp.zeros((BQ, 1), F32)

        def kv(j):
            r0 = pl.multiple_of(j * KB, KB)
            return r0, k_ref[pl.ds(r0, KB), :], v_ref[pl.ds(r0, KB), :]

        if fox:
            fqs = [_wide(fq_ref[a]) for a in range(2)]

            def probs(j, deltas, masked):
                r0, kb, vb = kv(j)
                new = []
                for a in range(2):
                    z = _dot_nt(qas[a], kb) + fqs[a] - fk_ref[a:a + 1, pl.ds(r0, KB)]
                    p = jnp.exp(z - stas[a])
                    if masked:
                        p = jnp.where(r0 + lane <= qpos, p, 0.0)
                    dp = _dot_nt(doas[a], vb)
                    p_s[a, j] = p
                    dp_s[a, j] = dp
                    new.append(deltas[a] + jnp.sum(p * dp, axis=1, keepdims=True))
                return tuple(new)

            deltas = lax.fori_loop(0, nfull, functools.partial(probs, masked=False), (col0, col0))
            deltas = lax.fori_loop(nfull, nkb, functools.partial(probs, masked=True), deltas)

            def step(j, dqs):
                r0, kb, _ = kv(j)
                new = []
                dk = jnp.zeros((KB, LANES), F32)
                dv = jnp.zeros((KB, LANES), F32)
                for a in range(2):
                    p = p_s[a, j]
                    ds = p * (dp_s[a, j] - deltas[a])
                    dsb = ds.astype(BF16)
                    dk += _dot_tn(dsb, qas[a])
                    dv += _dot_tn(p.astype(BF16), doas[a])
                    df_ref[a:a + 1, pl.ds(r0, KB)] -= jnp.sum(ds, axis=0, keepdims=True)
                    new.append(dqs[a] + _dot(dsb, kb))
                dk_ref[pl.ds(r0, KB), :] += dk
                dv_ref[pl.ds(r0, KB), :] += dv
                return tuple(new)

            dqs = lax.fori_loop(0, nkb, step, (zero, zero))
        else:
            incl = jnp.where(row_s <= lane_s, 1.0, 0.0).astype(BF16)
            excl = jnp.where(row_s < lane_s, 1.0, 0.0).astype(BF16)

            def step(j, carry, masked):
                r0, kb, vb = kv(j)
                strict = r0 + lane < qpos
                new = []
                dk = jnp.zeros((KB, LANES), F32)
                dv = jnp.zeros((KB, LANES), F32)
                for a in range(2):
                    dq, cm, cg = carry[3 * a:3 * a + 3]
                    ls, lm = _softplus_parts(_dot_nt(qas[a], kb))
                    if masked:
                        lm = jnp.where(strict, lm, 0.0)
                    beta = jnp.exp(ls)
                    upto, cm = _block_prefix_sums(lm, incl, cm)
                    aw = jnp.exp(ls + stas[a] - upto)
                    if masked:
                        aw = jnp.where(strict, aw, 0.0)
                    g = aw * _dot_nt(doas[a], vb)
                    pre, cg = _block_prefix_sums(g, excl, cg)
                    dz = g * (1.0 - beta) - pre * beta
                    if masked:
                        dz = jnp.where(strict, dz, 0.0)
                    dzb = dz.astype(BF16)
                    dk += _dot_tn(dzb, qas[a])
                    dv += _dot_tn(aw.astype(BF16), doas[a])
                    new += [dq + _dot(dzb, kb), cm, cg]
                dk_ref[pl.ds(r0, KB), :] += dk
                dv_ref[pl.ds(r0, KB), :] += dv
                return tuple(new)

            res = lax.fori_loop(0, nfull, functools.partial(step, masked=False), (zero, col0, col0, zero, col0, col0))
            res = lax.fori_loop(nfull, nkb, functools.partial(step, masked=True), res)
            dqs = (res[0], res[3])
        dq_ref[...] = (jnp.where(hms[0], dqs[0], dqs[1]) * 0.125).astype(BF16)

    col = lambda off: (lambda p, i: (0, off + p))
    blk = pl.BlockSpec((BQ, LANES), lambda p, i: (i, p))
    acc = pl.BlockSpec((s, LANES), lambda p, i: (0, p))
    st_spec = pl.BlockSpec((2, BQ, LANES), lambda p, i: (p, i, 0))
    in_specs = [pl.BlockSpec((BQ, LANES), lambda p, i: (i, Q_OFF + p)), pl.BlockSpec((s, LANES), col(K_OFF)),
                pl.BlockSpec((s, LANES), col(V_OFF)), blk, st_spec]
    ops = [qkv, qkv, qkv, do, stat]
    w = npairs * LANES
    out_shape = [jax.ShapeDtypeStruct((s, w), BF16), jax.ShapeDtypeStruct((s, w), F32), jax.ShapeDtypeStruct((s, w), F32)]
    out_specs = [blk, acc, acc]
    scratch = []
    if fox:
        fk_spec = pl.BlockSpec((None, 2, s), lambda p, i: (p, 0, 0))
        in_specs += [st_spec, fk_spec]
        ops += [fq, fk]
        out_shape.append(jax.ShapeDtypeStruct((npairs, 2, s), F32))
        out_specs.append(fk_spec)
        scratch = [pltpu.VMEM((2, s // KB, BQ, KB), F32)] * 2
    outs, rode = call_with_rider(body, name, rider, ops, in_specs, out_shape, out_specs, scratch, (npairs, nq))
    return (*outs, rode)


def forget_fwd(fl, bias, name):
    s = fl.shape[0]

    def body(fl_ref, b_ref, f_ref):
        row, lane = _iotas()
        lower = jnp.where(lane <= row, 1.0, 0.0).astype(BF16)

        def step(n, carry):
            r0 = pl.multiple_of(n * BLK, BLK)
            ls, _ = _softplus_parts(fl_ref[pl.ds(r0, BLK), :] + b_ref[...])
            blk = _dot_exact_rhs(lower, ls) + carry
            f_ref[pl.ds(r0, BLK), :] = blk
            return blk[BLK - 1:BLK, :]

        lax.fori_loop(0, s // BLK, step, jnp.zeros((1, LANES), F32))

    return pl.pallas_call(
        body, name=name, out_shape=jax.ShapeDtypeStruct((s, LANES), F32),
        in_specs=[pl.BlockSpec(memory_space=pltpu.VMEM)] * 2, out_specs=pl.BlockSpec(memory_space=pltpu.VMEM),
        compiler_params=_params(),
    )(fl, bias)


def forget_bwd(fl, bias, df, name):
    s = fl.shape[0]
    nb = s // BLK

    def body(fl_ref, b_ref, df_ref, o_ref, db_ref):
        row, lane = _iotas()
        upper = jnp.where(lane >= row, 1.0, 0.0).astype(BF16)

        def step(nn, carry):
            tail, db = carry
            r0 = pl.multiple_of((nb - 1 - nn) * BLK, BLK)
            dls = _dot_exact_rhs(upper, df_ref[pl.ds(r0, BLK), :]) + tail
            xv = fl_ref[pl.ds(r0, BLK), :] + b_ref[...]
            dfl = dls * (1.0 / (1.0 + jnp.exp(xv)))
            o_ref[pl.ds(r0, BLK), :] = dfl
            return dls[0:1, :], db + jnp.sum(dfl, axis=0, keepdims=True)

        _, db = lax.fori_loop(0, nb, step, (jnp.zeros((1, LANES), F32), jnp.zeros((1, LANES), F32)))
        db_ref[...] = db

    return pl.pallas_call(
        body, name=name, out_shape=(jax.ShapeDtypeStruct((s, LANES), F32), jax.ShapeDtypeStruct((1, LANES), F32)),
        in_specs=[pl.BlockSpec(memory_space=pltpu.VMEM)] * 3,
        out_specs=(pl.BlockSpec(memory_space=pltpu.VMEM), pl.BlockSpec(memory_space=pltpu.VMEM)),
        compiler_params=_params(),
    )(fl, bias, df)


def _rot_tables(s):
    inv = ROPE_THETA ** (-jnp.arange(ROT_HALF, dtype=F32) * 2.0 / (2 * ROT_HALF))
    ang = jnp.arange(s, dtype=F32)[:, None] * inv[None, :]
    cos, sin = jnp.cos(ang), jnp.sin(ang)
    z8 = jnp.zeros((s, ROT_HALF), F32)
    rest = HEAD_DIM - 2 * ROT_HALF
    zr, onr = jnp.zeros((s, rest), F32), jnp.ones((s, rest), F32)
    tile = lambda t: jnp.tile(t, (1, 2))
    return tile(jnp.concatenate([cos, cos, onr], 1)), tile(jnp.concatenate([-sin, z8, zr], 1)), tile(jnp.concatenate([z8, sin, zr], 1))


def _deinterleave(dst, src_ref, stride, s, dtype):
    length = s // stride
    for r in range(stride):
        if stride == 1:
            dst[...] = src_ref[...].astype(dtype)
        else:
            dst[r * length:(r + 1) * length, :] = src_ref[pl.ds(r, length, stride=stride), :].astype(dtype)


def _band_masks(row, lane, first):
    return lane <= row, lane >= row + jnp.where(first, BLK, 0)


N_DIL_PAIRS = 4


def _rotate_into(q_ref, k_ref, v_ref, c_ref, s1_ref, s2_ref, qr, kr, vr):
    c, s1, s2 = c_ref[...], s1_ref[...], s2_ref[...]
    rot = lambda xv: xv * c + pltpu.roll(xv, LANES - ROT_HALF, 1) * s1 + pltpu.roll(xv, ROT_HALF, 1) * s2
    qr[...] = rot(q_ref[...].astype(F32)) * 0.125
    kr[...] = rot(k_ref[...].astype(F32))
    vr[...] = v_ref[...].astype(F32)


def _dilated_operands(qkv, tables):
    s = qkv.shape[0]
    col = lambda off: pl.BlockSpec((s, LANES), lambda p: (0, off + N_DIL_PAIRS + p))
    table = pl.BlockSpec((s, LANES), lambda p: (0, 0))
    return [qkv, qkv, qkv, *tables], [col(Q_OFF), col(K_OFF), col(V_OFF), table, table, table]


def dilated_fwd(qkv, tables, name, rider=None):
    s = qkv.shape[0]
    npairs, w = N_DIL_PAIRS, N_DIL_PAIRS * LANES
    nblk = s // BLK

    def body(q_in, k_in, v_in, c_ref, s1_ref, s2_ref, o_ref, lse_ref, q_ref, k_ref, v_ref, qs, ks, vs, od, ld, on, ln):
        row, lane = _iotas()
        _rotate_into(q_in, k_in, v_in, c_ref, s1_ref, s2_ref, q_ref, k_ref, v_ref)
        for pi, stride in enumerate(DIL_STRIDES):
            per = (s // stride) // BLK
            _deinterleave(qs, q_ref, stride, s, BF16)
            _deinterleave(ks, k_ref, stride, s, BF16)
            _deinterleave(vs, v_ref, stride, s, BF16)

            def block(b, carry):
                r0 = pl.multiple_of(b * BLK, BLK)
                rp = pl.multiple_of(jnp.maximum(b - 1, 0) * BLK, BLK)
                mc, mp = _band_masks(row, lane, b % per == 0)
                q = qs[pl.ds(r0, BLK), :]
                kc, kp, vc, vp = ks[pl.ds(r0, BLK), :], ks[pl.ds(rp, BLK), :], vs[pl.ds(r0, BLK), :], vs[pl.ds(rp, BLK), :]
                out = jnp.zeros((BLK, LANES), F32)
                lse = jnp.zeros((BLK, LANES), F32)
                for a in range(2):
                    hm = (lane < HEAD_DIM) if a == 0 else (lane >= HEAD_DIM)
                    qa = jnp.where(hm, q.astype(F32), 0.0).astype(BF16)
                    sc = jnp.where(mc, _dot_nt(qa, kc), NEG)
                    sp = jnp.where(mp, _dot_nt(qa, kp), NEG)
                    mx = jnp.maximum(jnp.max(sc, axis=1, keepdims=True), jnp.max(sp, axis=1, keepdims=True))
                    pc, pp = jnp.exp(sc - mx), jnp.exp(sp - mx)
                    l = jnp.sum(pc, axis=1, keepdims=True) + jnp.sum(pp, axis=1, keepdims=True)
                    oa = (_dot(pc.astype(BF16), vc) + _dot(pp.astype(BF16), vp)) / l
                    out = jnp.where(hm, oa, out)
                    lse = jnp.where(hm, mx + jnp.log(l), lse)
                od[pl.ds(r0, BLK), :] = out
                ld[pl.ds(r0, BLK), :] = lse
                return carry

            lax.fori_loop(0, nblk, block, 0, unroll=2)
            length = s // stride
            for r in range(stride):
                if stride == 1:
                    on[pi] = od[...]
                    ln[pi] = ld[...]
                else:
                    on[pi, pl.ds(r, length, stride=stride), :] = od[r * length:(r + 1) * length, :]
                    ln[pi, pl.ds(r, length, stride=stride), :] = ld[r * length:(r + 1) * length, :]

        def merge(n, carry):
            r0 = pl.multiple_of(n * BLK, BLK)
            ls = [ln[pi, pl.ds(r0, BLK), :] for pi in range(3)]
            mx = jnp.maximum(jnp.maximum(ls[0], ls[1]), ls[2])
            ws = [jnp.exp(lv - mx) for lv in ls]
            den = ws[0] + ws[1] + ws[2]
            num = ws[0] * on[0, pl.ds(r0, BLK), :] + ws[1] * on[1, pl.ds(r0, BLK), :] + ws[2] * on[2, pl.ds(r0, BLK), :]
            o_ref[pl.ds(r0, BLK), :] = num / den
            lse_ref[pl.ds(r0, BLK), :] = mx + jnp.log(den)
            return carry

        lax.fori_loop(0, nblk, merge, 0, unroll=2)

    colspec = pl.BlockSpec((s, LANES), lambda p: (0, p))
    out = jax.ShapeDtypeStruct((s, w), F32)
    scratch = ([pltpu.VMEM((s, LANES), F32)] * 3 + [pltpu.VMEM((s, LANES), BF16)] * 3 + [pltpu.VMEM((s, LANES), F32)] * 2
               + [pltpu.VMEM((3, s, LANES), F32)] * 2)
    ops, in_specs = _dilated_operands(qkv, tables)
    (o, lse), rode = call_with_rider(body, name, rider, ops, in_specs, [out, out], [colspec, colspec], scratch, (npairs,))
    return o, lse, rode


def dilated_bwd(qkv, tables, do, out, lse, do_off, name, rider=None):
    s = qkv.shape[0]
    npairs, w = N_DIL_PAIRS, N_DIL_PAIRS * LANES
    nblk = s // BLK

    def body(q_in, k_in, v_in, c_ref, s1_ref, s2_ref, do_ref, out_ref, lse_ref, dq_out, dk_out, dv_out,
             q_ref, k_ref, v_ref, dq_ref, dk_ref, dv_ref, qs, ks, vs, dos, dls, lss, dqd, dkd, dvd, dln):
        row, lane = _iotas()
        same_head = jnp.where((row < HEAD_DIM) == (lane < HEAD_DIM), 1.0, 0.0).astype(BF16)
        _rotate_into(q_in, k_in, v_in, c_ref, s1_ref, s2_ref, q_ref, k_ref, v_ref)

        def delta_blk(n, carry):
            r0 = pl.multiple_of(n * BLK, BLK)
            dln[pl.ds(r0, BLK), :] = _dot_exact_lhs(do_ref[pl.ds(r0, BLK), :] * out_ref[pl.ds(r0, BLK), :], same_head)
            return carry

        lax.fori_loop(0, nblk, delta_blk, 0, unroll=2)
        for pi, stride in enumerate(DIL_STRIDES):
            per = (s // stride) // BLK
            _deinterleave(qs, q_ref, stride, s, BF16)
            _deinterleave(ks, k_ref, stride, s, BF16)
            _deinterleave(vs, v_ref, stride, s, BF16)
            _deinterleave(dos, do_ref, stride, s, BF16)
            _deinterleave(dls, dln, stride, s, F32)
            _deinterleave(lss, lse_ref, stride, s, F32)

            def block(b, carry):
                r0 = pl.multiple_of(b * BLK, BLK)
                rp = pl.multiple_of(jnp.maximum(b - 1, 0) * BLK, BLK)
                first = b % per == 0
                mc, mp = _band_masks(row, lane, first)
                q, dov = qs[pl.ds(r0, BLK), :], dos[pl.ds(r0, BLK), :]
                kc, kp, vc, vp = ks[pl.ds(r0, BLK), :], ks[pl.ds(rp, BLK), :], vs[pl.ds(r0, BLK), :], vs[pl.ds(rp, BLK), :]
                lse_t, dl_t = lss[pl.ds(r0, BLK), :], dls[pl.ds(r0, BLK), :]
                dq = jnp.zeros((BLK, LANES), F32)
                dkc = jnp.zeros((BLK, LANES), F32)
                dkp = jnp.zeros((BLK, LANES), F32)
                dvc = jnp.zeros((BLK, LANES), F32)
                dvp = jnp.zeros((BLK, LANES), F32)
                for a in range(2):
                    hm = (lane < HEAD_DIM) if a == 0 else (lane >= HEAD_DIM)
                    pick = lane == a * HEAD_DIM
                    qa = jnp.where(hm, q.astype(F32), 0.0).astype(BF16)
                    doa = jnp.where(hm, dov.astype(F32), 0.0).astype(BF16)
                    lse_a = jnp.sum(jnp.where(pick, lse_t, 0.0), axis=1, keepdims=True)
                    dl_a = jnp.sum(jnp.where(pick, dl_t, 0.0), axis=1, keepdims=True)
                    pc = jnp.where(mc, jnp.exp(_dot_nt(qa, kc) - lse_a), 0.0)
                    pp = jnp.where(mp, jnp.exp(_dot_nt(qa, kp) - lse_a), 0.0)
                    dsc = (pc * (_dot_nt(doa, vc) - dl_a)).astype(BF16)
                    dsp = (pp * (_dot_nt(doa, vp) - dl_a)).astype(BF16)
                    dq = jnp.where(hm, _dot(dsc, kc) + _dot(dsp, kp), dq)
                    dkc += _dot_tn(dsc, qa)
                    dkp += _dot_tn(dsp, qa)
                    dvc += _dot_tn(pc.astype(BF16), doa)
                    dvp += _dot_tn(pp.astype(BF16), doa)
                dqd[pl.ds(r0, BLK), :] = dq
                dkd[pl.ds(r0, BLK), :] = dkc
                dvd[pl.ds(r0, BLK), :] = dvc

                @pl.when(jnp.logical_not(first))
                def _():
                    dkd[pl.ds(rp, BLK), :] += dkp
                    dvd[pl.ds(rp, BLK), :] += dvp

                return carry

            lax.fori_loop(0, nblk, block, 0, unroll=2)
            length = s // stride
            for dst, src in ((dq_ref, dqd), (dk_ref, dkd), (dv_ref, dvd)):
                for r in range(stride):
                    if stride == 1:
                        dst[...] = src[...]
                    else:
                        dst[pl.ds(r, length, stride=stride), :] += src[r * length:(r + 1) * length, :]

        c, s1, s2 = c_ref[...], s1_ref[...], s2_ref[...]
        rot_t = lambda dy: dy * c + pltpu.roll(dy * s1, ROT_HALF, 1) + pltpu.roll(dy * s2, LANES - ROT_HALF, 1)
        dq_out[...] = (rot_t(dq_ref[...]) * 0.125).astype(BF16)
        dk_out[...] = rot_t(dk_ref[...]).astype(BF16)
        dv_out[...] = dv_ref[...].astype(BF16)

    colspec = pl.BlockSpec((s, LANES), lambda p: (0, p))
    do_spec = pl.BlockSpec((s, LANES), lambda p: (0, do_off + p))
    o3 = jax.ShapeDtypeStruct((s, w), BF16)
    scratch = [pltpu.VMEM((s, LANES), F32)] * 6 + [pltpu.VMEM((s, LANES), BF16)] * 4 + [pltpu.VMEM((s, LANES), F32)] * 6
    ops, in_specs = _dilated_operands(qkv, tables)
    outs, rode = call_with_rider(body, name, rider, ops + [do, out, lse], in_specs + [do_spec, colspec, colspec],
                                 [o3, o3, o3], [colspec, colspec, colspec], scratch, (npairs,))
    return (*outs, rode)


def adamw(w, g, m, v, name):
    rows, cols = w.shape
    rb = min(rows, ROWS)
    c1 = 1.0 - ADAM_B1 ** ADAM_STEP
    c2 = 1.0 - ADAM_B2 ** ADAM_STEP

    def body(w_ref, g_ref, m_ref, v_ref, d_ref, mo_ref, vo_ref):
        gv = g_ref[...]
        mn = ADAM_B1 * m_ref[...] + (1.0 - ADAM_B1) * gv
        vn = ADAM_B2 * v_ref[...] + (1.0 - ADAM_B2) * (gv * gv)
        d_ref[...] = -ADAM_LR * ((mn / c1) / (jnp.sqrt(vn / c2) + ADAM_EPS) + ADAM_WD * w_ref[...])
        mo_ref[...] = mn
        vo_ref[...] = vn

    spec = _row_spec(cols, rb)
    out = jax.ShapeDtypeStruct((rows, cols), F32)
    return pl.pallas_call(
        body, name=name, out_shape=(out, out, out), grid=(rows // rb,), in_specs=[spec] * 4, out_specs=(spec,) * 3,
        compiler_params=_params(("parallel",)),
    )(w, g, m, v)


def _prefetch_call(body, name, scalar, ops, grid, in_specs, out_specs, out_shape, sem):
    spec = pltpu.PrefetchScalarGridSpec(num_scalar_prefetch=1, grid=grid, in_specs=in_specs, out_specs=out_specs)
    return pl.pallas_call(body, name=name, grid_spec=spec, out_shape=out_shape, compiler_params=_params(sem))(scalar, *ops)


def pair_sums(gs, gots, core, name):
    n = len(gs)

    def body(core_ref, *refs):
        for f in range(n):
            refs[2 * n + f][...] = (refs[f][...].astype(F32) + refs[n + f][...].astype(F32)).astype(BF16)

    blk = lambda g, rows_of: pl.BlockSpec((None, g.shape[1] // 2, g.shape[2]), rows_of)
    mine = [blk(g, lambda j, core_ref: (j, core_ref[0], 0)) for g in gs]
    half = [blk(g, lambda j, core_ref: (j, 0, 0)) for g in gs]
    outs = tuple(jax.ShapeDtypeStruct((g.shape[0], g.shape[1] // 2, g.shape[2]), BF16) for g in gs)
    return _prefetch_call(body, name, core, (*gs, *gots), (N_CHIPS,), mine + half, tuple(half), outs, ("parallel",))


def chip_sums(pairs, gots, chip, layers, intos, name):
    n = len(pairs)

    def body(chip_ref, *refs):
        for f in range(n):
            p_ref, a_ref, b_ref, c_ref = refs[4 * f:4 * f + 4]
            refs[5 * n + f][...] = ((p_ref[...].astype(F32) + a_ref[...].astype(F32)) + b_ref[...].astype(F32)) + c_ref[...].astype(F32)

    in_specs, ops = [], []
    for p, got in zip(pairs, gots):
        blk = lambda at, p=p: pl.BlockSpec((None,) + p.shape[1:], at)
        in_specs += [blk(lambda i, chip_ref: (chip_ref[0], 0, 0))] + [blk(lambda i, chip_ref, k=k: (k, 0, 0)) for k in range(3)]
        ops += [p, got, got, got]
    out_specs = tuple(pl.BlockSpec((None,) + p.shape[1:], lambda i, chip_ref, l=l: (l, 0, 0)) for p, l in zip(pairs, layers))
    spec = pltpu.PrefetchScalarGridSpec(num_scalar_prefetch=1, grid=(1,), in_specs=in_specs + [_ANY] * n, out_specs=out_specs)
    return pl.pallas_call(
        body, name=name, grid_spec=spec, out_shape=tuple(jax.ShapeDtypeStruct(t.shape, t.dtype) for t in intos),
        input_output_aliases={1 + 4 * n + f: f for f in range(n)}, compiler_params=_params(("arbitrary",)),
    )(chip, *ops, *intos)


def adamw_family(w, m, v, g_mine, g_other, core, name):
    nl, r, c = w.shape
    gc = g_mine.shape[2]
    rh = r // 2
    nb = 4 if rh % 512 == 0 else (2 if rh % 16 == 0 and rh > 256 else 1)
    rb = rh // nb
    c1 = 1.0 - ADAM_B1 ** ADAM_STEP
    c2 = 1.0 - ADAM_B2 ** ADAM_STEP

    def body(core_ref, w_ref, m_ref, v_ref, gm_ref, go_ref, g_ref, d_ref, mo_ref, vo_ref):
        gv = jnp.where(pl.program_id(1) == core_ref[0], gm_ref[...], go_ref[...])[:, :c]
        mn = ADAM_B1 * m_ref[...] + (1.0 - ADAM_B1) * gv
        vn = ADAM_B2 * v_ref[...] + (1.0 - ADAM_B2) * (gv * gv)
        g_ref[...] = gv
        d_ref[...] = -ADAM_LR * ((mn / c1) / (jnp.sqrt(vn / c2) + ADAM_EPS) + ADAM_WD * w_ref[...])
        mo_ref[...] = mn
        vo_ref[...] = vn

    full = pl.BlockSpec((None, rb, c), lambda l, h, i, core_ref: (l, h * nb + i, 0))
    mine = pl.BlockSpec((None, rb, gc), lambda l, h, i, core_ref: (l, jnp.where(h == core_ref[0], i, 0), 0))
    other = pl.BlockSpec((None, rb, gc), lambda l, h, i, core_ref: (l, jnp.where(h == core_ref[0], 0, i), 0))
    out = jax.ShapeDtypeStruct((nl, r, c), F32)
    return _prefetch_call(body, name, core, (w, m, v, g_mine, g_other), (nl, 2, nb), [full, full, full, mine, other],
                          (full, full, full, full), (out, out, out, out), ("parallel", "parallel", "parallel"))


def _coords():
    return lax.axis_index("x"), lax.axis_index("y"), lax.axis_index("c")


def _other_chips(x, y):
    return ((1 - x, 1 - y), (1 - x, y), (x, 1 - y))


_ANY = pl.BlockSpec(memory_space=pl.ANY)


def _exchange_call(body, name, arrays, out_shapes, n_copies, n_local=0):
    n = len(arrays)

    def wrapped(*refs):
        body(refs[:n], refs[n:n + len(out_shapes)], *refs[n + len(out_shapes):])

    scratch = [pltpu.SemaphoreType.DMA((n_copies,)), pltpu.SemaphoreType.DMA((n_copies,))]
    if n_local:
        scratch.append(pltpu.SemaphoreType.DMA((n_local,)))
    return pl.pallas_call(
        wrapped, name=name, out_shape=tuple(out_shapes), in_specs=[_ANY] * n, out_specs=tuple([_ANY] * len(out_shapes)),
        scratch_shapes=scratch, compiler_params=_params(),
    )(*arrays)


def _remote(send_sems, recv_sems, n, src, dst, to):
    return pltpu.make_async_remote_copy(src_ref=src, dst_ref=dst, send_sem=send_sems.at[n], recv_sem=recv_sems.at[n],
                                        device_id=to, device_id_type=MESH)


class Rider:
    def __init__(self, arrays, out_shapes, n_remote, n_local, copies, then=None):
        self.arrays, self.out_shapes, self.n_remote, self.n_local = list(arrays), list(out_shapes), n_remote, n_local
        self.copies, self.then = copies, then

    def sems(self):
        return [pltpu.SemaphoreType.DMA((self.n_remote,)), pltpu.SemaphoreType.DMA((self.n_remote,)),
                pltpu.SemaphoreType.DMA((max(self.n_local, 1),))]

    def run(self, name):
        n, no = len(self.arrays), len(self.out_shapes)

        def body(*refs):
            for stage in (self.copies, self.then):
                if stage is not None:
                    cps = stage(refs[:n], refs[n:n + no], *refs[n + no:])
                    for cp in cps:
                        cp.start()
                    for cp in cps:
                        cp.wait()

        return pl.pallas_call(
            body, name=name, out_shape=tuple(self.out_shapes), in_specs=[_ANY] * n, out_specs=tuple([_ANY] * no),
            scratch_shapes=self.sems(), compiler_params=_params(),
        )(*self.arrays)


def ride(rider, body, n_in, n_out, grid):
    if rider is None:
        return body
    ni, no = len(rider.arrays), len(rider.out_shapes)

    def wrapped(*refs):
        ins, r_in = refs[:n_in], refs[n_in:n_in + ni]
        outs = refs[n_in + ni:n_in + ni + n_out]
        r_out = refs[n_in + ni + n_out:n_in + ni + n_out + no]
        rest = refs[n_in + ni + n_out + no:]
        scratch, sems = rest[:len(rest) - 3], rest[len(rest) - 3:]
        step, total = 0, 1
        for a, g in enumerate(grid):
            step, total = step * g + pl.program_id(a), total * g
        assert total >= 3
        relay_at = (7 * total) // 8 if rider.then is not None else total - 1

        @pl.when(step == 0)
        def _():
            for cp in rider.copies(r_in, r_out, *sems):
                cp.start()

        body(*ins, *outs, *scratch)

        @pl.when(step == relay_at)
        def _():
            for cp in rider.copies(r_in, r_out, *sems):
                cp.wait()
            if rider.then is not None:
                for cp in rider.then(r_in, r_out, *sems):
                    cp.start()

        if rider.then is not None:
            @pl.when(step == total - 1)
            def _():
                for cp in rider.then(r_in, r_out, *sems):
                    cp.wait()

    return wrapped


def call_with_rider(body, name, rider, ops, in_specs, out_shape, out_specs, scratch, grid):
    n_in, n_out = len(ops), len(out_shape)
    ops, in_specs, out_shape, out_specs, scratch = list(ops), list(in_specs), list(out_shape), list(out_specs), list(scratch)
    if rider is not None:
        ops += rider.arrays
        in_specs += [_ANY] * len(rider.arrays)
        out_shape += rider.out_shapes
        out_specs += [_ANY] * len(rider.out_shapes)
        scratch += rider.sems()
    res = pl.pallas_call(
        ride(rider, body, n_in, n_out, grid), name=name, out_shape=tuple(out_shape), grid=grid, in_specs=in_specs,
        out_specs=tuple(out_specs), scratch_shapes=scratch, compiler_params=_params(("arbitrary",) * len(grid)),
    )(*ops)
    return tuple(res[:n_out]), list(res[n_out:])


def gather_rider(shards):
    nf = len(shards)
    half = lambda ref, which: pl.ds(which * (ref.shape[-2] // 2), ref.shape[-2] // 2)

    def copies(s_refs, o_refs, send_sems, recv_sems, local_sems):
        x, y, c = _coords()
        me = 2 * x + y
        cps = [pltpu.make_async_copy(s_refs[f], o_refs[f].at[me], local_sems.at[f]) for f in range(nf)]
        for k, (px, py) in enumerate(_other_chips(x, y)):
            for f in range(nf):
                rows = half(s_refs[f], c)
                cps.append(_remote(send_sems, recv_sems, k * nf + f, s_refs[f].at[rows], o_refs[f].at[me, rows], (px, py, c)))
        return cps

    def relay(s_refs, o_refs, send_sems, recv_sems, local_sems):
        x, y, c = _coords()
        cps = []
        for k, (px, py) in enumerate(_other_chips(x, y)):
            for f in range(nf):
                landed = o_refs[f].at[2 * px + py, half(s_refs[f], c)]
                cps.append(_remote(send_sems, recv_sems, (3 + k) * nf + f, landed, landed, (x, y, 1 - c)))
        return cps

    return Rider(shards, [jax.ShapeDtypeStruct((N_CHIPS,) + sh.shape, sh.dtype) for sh in shards], 6 * nf, nf, copies, relay)


def scatter_rider(pairs):
    nf = len(pairs)

    def copies(p_refs, o_refs, send_sems, recv_sems, local_sems):
        x, y, c = _coords()
        cps = []
        for k, (px, py) in enumerate(_other_chips(x, y)):
            for f in range(nf):
                cps.append(_remote(send_sems, recv_sems, k * nf + f, p_refs[f].at[2 * px + py], o_refs[f].at[k], (px, py, c)))
        return cps

    return Rider(pairs, [jax.ShapeDtypeStruct((3,) + p.shape[1:], p.dtype) for p in pairs], 3 * nf, 0, copies)


def pair_swap(grads, name):
    def body(g_refs, o_refs, send_sems, recv_sems):
        x, y, c = _coords()
        cps = []
        for f, g_ref in enumerate(g_refs):
            rh = g_ref.shape[1] // 2
            cps.append(_remote(send_sems, recv_sems, f, g_ref.at[:, pl.ds((1 - c) * rh, rh), :], o_refs[f], (x, y, 1 - c)))
        for cp in cps:
            cp.start()
        for cp in cps:
            cp.wait()

    outs = [jax.ShapeDtypeStruct((g.shape[0], g.shape[1] // 2, g.shape[2]), g.dtype) for g in grads]
    return _exchange_call(body, name, grads, outs, len(grads))


def half_swap(halves, name):
    def body(h_refs, o_refs, send_sems, recv_sems):
        x, y, c = _coords()
        cps = [_remote(send_sems, recv_sems, f, h_ref, o_refs[f], (x, y, 1 - c)) for f, h_ref in enumerate(h_refs)]
        for cp in cps:
            cp.start()
        for cp in cps:
            cp.wait()

    return _exchange_call(body, name, halves, [jax.ShapeDtypeStruct(h.shape, h.dtype) for h in halves], len(halves))


def allsum_small(part, name):
    def body(p_ref, tot_ref, all_ref, send_sems, recv_sems):
        x, y, c = _coords()
        me, sibling = (x, y, c), (x, y, 1 - c)
        chips = _other_chips(x, y)

        def slot(px, py, pc):
            return all_ref.at[4 * px + 2 * py + pc]

        def copy(k, block, to, src=None):
            return pltpu.make_async_remote_copy(src_ref=slot(*block) if src is None else src, dst_ref=slot(*block),
                                                send_sem=send_sems.at[k], recv_sem=recv_sems.at[k], device_id=to, device_id_type=MESH)

        slot(*me)[...] = p_ref[...]
        first = [copy(0, me, sibling, src=p_ref)] + [copy(1 + j, me, (*chip, c), src=p_ref) for j, chip in enumerate(chips)]
        for cp in first:
            cp.start()
        passed = [copy(4 + j, (*chip, c), sibling) for j, chip in enumerate(chips)]
        for j, chip in enumerate(chips):
            copy(1 + j, (*chip, c), me).wait_recv()
            passed[j].start()
        copy(0, sibling, me).wait_recv()
        for j, chip in enumerate(chips):
            copy(4 + j, (*chip, 1 - c), me).wait_recv()
        for cp in first + passed:
            cp.wait_send()
        tot = all_ref[0]
        for d in range(1, 8):
            tot = tot + all_ref[d]
        tot_ref[...] = tot

    vm = pl.BlockSpec(memory_space=pltpu.VMEM)
    return pl.pallas_call(
        body, name=name, out_shape=jax.ShapeDtypeStruct(part.shape, F32), in_specs=[vm], out_specs=vm,
        scratch_shapes=[pltpu.VMEM((8,) + part.shape, F32), pltpu.SemaphoreType.DMA((7,)), pltpu.SemaphoreType.DMA((7,))],
        compiler_params=_params(),
    )(part)


QKVF_COLS = 772
QKVF_PAD = 896
FORWARD_CARRY = {0: (("fi0",), ("qkv1", "o1", "fo1"), ("fo0",)), 1: (("fi1", "qkv2", "o2"), (), ()),
                 2: (("fi2",), ("qkv3", "o3", "fo3"), ("fo2",)), 3: (("fi3",), (), ())}


def _tables_for(s):
    return _rot_tables(s)


def layer_families(layer):
    return (0, 1, layer // 2) if layer % 2 == 0 else (2, 3, layer // 2)


class GradientExchange:
    def __init__(self, family_layers):
        self.core = lax.axis_index("c").astype(jnp.int32).reshape(1)
        self.chip = (2 * lax.axis_index("x") + lax.axis_index("y")).astype(jnp.int32).reshape(1)
        self.family_layers = family_layers
        self.pairs, self.mine, self.pending = {}, {}, []

    def add(self, items, tag):
        gs = [g for _, _, g in items]
        sums = pair_sums(gs, pair_swap(gs, f"grad_pair_swap_{tag}"), self.core, f"grad_pair_sum_{tag}")
        for (fam, li, _), pair in zip(items, sums):
            self.pairs[(fam, li)] = pair
            self.pending.append((fam, li))

    def rider(self, only=None):
        keys = [k for k in self.pending if only is None or k in only]
        self.pending = [k for k in self.pending if k not in keys]
        return (scatter_rider([self.pairs[k] for k in keys]) if keys else None), keys

    def landed(self, keys, outs):
        batch = []
        for k, o in list(zip(keys, outs)) + [(None, None)]:
            if batch and (k is None or k[0] in [b[0][0] for b in batch]):
                ks = [b[0] for b in batch]
                pairs = [self.pairs[b] for b in ks]
                intos = [self.mine[fam] if fam in self.mine else jnp.zeros((self.family_layers[fam],) + p.shape[1:], F32)
                         for (fam, _), p in zip(ks, pairs)]
                sums = chip_sums(pairs, [b[1] for b in batch], self.chip, [li for _, li in ks], intos,
                                 "grad_chip_sum_" + "_".join(f"{f}{li}" for f, li in ks))
                self.mine.update({fam: t for (fam, _), t in zip(ks, sums)})
                batch = []
            if k is not None:
                batch.append((k, o))

    def finish(self, weights, moments1, moments2):
        last, keys = self.rider()
        if last is not None:
            self.landed(keys, last.run("grad_chip_scatter_last"))
        mine = [self.mine[fam] for fam in range(len(weights))]
        other = half_swap(mine, "grad_half_swap")
        return [adamw_family(w, m, v, gm, go, self.core, f"adamw_{f}")
                for f, (w, m, v, gm, go) in enumerate(zip(weights, moments1, moments2, mine, other))]


class KeepGradients:
    def __init__(self):
        self.grads = {}

    def add(self, items, tag):
        for fam, li, g in items:
            self.grads[(fam, li)] = g

    def rider(self, only=None):
        return None, []

    def landed(self, keys, outs):
        pass


def kernel(x, norm_mix, w_qkv_even, w_o_even, w_qkvf_odd, b_forget, w_o_odd, norm_ffn, w_ffn_in, w_ffn_out, norm_final, loss_target, m_norm_mix, m_w_qkv_even, m_w_o_even, m_w_qkvf_odd, m_b_forget, m_w_o_odd, m_norm_ffn, m_w_ffn_in, m_w_ffn_out, m_norm_final, v_norm_mix, v_w_qkv_even, v_w_o_even, v_w_qkvf_odd, v_b_forget, v_w_o_odd, v_norm_ffn, v_w_ffn_in, v_w_ffn_out, v_norm_final):
    w_shards = [w_qkv_even, w_o_even, w_qkvf_odd, w_o_odd, w_ffn_in, w_ffn_out]
    shards = [w.astype(BF16) for w in w_shards]
    tables = _tables_for(x.shape[1])
    bias_pad = jnp.pad(b_forget, ((0, 0), (0, LANES - N_HEADS)))

    mine = {}
    for layer in range(DEPTH):
        fam_qkv, fam_o, li = layer_families(layer)
        mine.update({f"qkv{layer}": shards[fam_qkv][li], f"o{layer}": shards[fam_o][li],
                     f"fi{layer}": shards[4][layer], f"fo{layer}": shards[5][layer]})
    fetch = lambda names: gather_rider([mine[n] for n in names])
    have = dict(zip(("qkv0", "o0"), fetch(("qkv0", "o0")).run("gather_first")))
    saved, cur = [], x[0]
    h1 = rmsnorm_fwd(cur, norm_mix[0:1], "l0_norm_mix")
    for layer in range(DEPTH):
        next_gain = norm_mix[layer + 1:layer + 2] if layer + 1 < DEPTH else None
        cur, h1, keep = forward_layer(layer, cur, h1, have, norm_ffn[layer:layer + 1], next_gain, tables,
                                      bias_pad[layer // 2:layer // 2 + 1], fetch, *FORWARD_CARRY[layer])
        saved.append(keep)

    dcur, g_final, loss_part = loss_head(cur, norm_final.reshape(1, D_MODEL), loss_target[0], "loss_head")

    exchange = GradientExchange([w.shape[0] for w in w_shards])
    g_mix, g_ffn, g_bias = [None] * DEPTH, [None] * DEPTH, [None] * (DEPTH // 2)
    for layer in reversed(range(DEPTH)):
        dcur, g_mix[layer], g_ffn[layer], g_b = backward_layer(layer, dcur, saved[layer], norm_mix[layer:layer + 1],
                                                               norm_ffn[layer:layer + 1], tables, bias_pad[layer // 2:layer // 2 + 1], exchange)
        if g_b is not None:
            g_bias[layer // 2] = g_b

    zero_row = jnp.zeros((1, D_MODEL), F32)
    pad16 = lambda v: jnp.pad(v, (0, D_MODEL - v.shape[0]))[None, :]
    small_rows = lambda mix, ffn, fin, bias, last: jnp.concatenate(
        [r.reshape(1, D_MODEL) for r in mix] + [r.reshape(1, D_MODEL) for r in ffn] + [fin.reshape(1, D_MODEL)]
        + [pad16(b) for b in bias] + [last] + [zero_row] * (SMALL_ROWS - 12), axis=0)
    loss_row = pad16(loss_part[0, :1])
    small_g = allsum_small(small_rows(g_mix, g_ffn, g_final, g_bias, loss_row), "allsum_small")
    loss = small_g[11, 0]
    small_g = small_g.at[11].set(0.0)
    sw = small_rows(list(norm_mix), list(norm_ffn), norm_final, list(b_forget), zero_row)
    sm = small_rows(list(m_norm_mix), list(m_norm_ffn), m_norm_final, list(m_b_forget), zero_row)
    sv = small_rows(list(v_norm_mix), list(v_norm_ffn), v_norm_final, list(v_b_forget), zero_row)
    sd, snm, snv = adamw(sw, small_g, sm, sv, "adamw_small")

    def small_out(a):
        return a[0:4], a[8, :], a[9:11, :N_HEADS], a[4:8]

    big = exchange.finish(w_shards, [m_w_qkv_even, m_w_o_even, m_w_qkvf_odd, m_w_o_odd, m_w_ffn_in, m_w_ffn_out],
                          [v_w_qkv_even, v_w_o_even, v_w_qkvf_odd, v_w_o_odd, v_w_ffn_in, v_w_ffn_out])

    def outputs(small, which):
        mix, fin, bias, ffn = small_out(small)
        qkv_e, o_e, qkvf, o_o, fi, fo = [big[f][which] for f in range(6)]
        return [mix, qkv_e, o_e, qkvf, bias, o_o, ffn, fi, fo, fin]

    return (loss, dcur[None], *outputs(small_g, 0), *outputs(sd, 1), *outputs(snm, 2), *outputs(snv, 3))


def _chip_tile(rows, cols, at):
    return pl.BlockSpec((None, rows, cols), at)


def forward_layer(layer, cur, h1, have, ffn_gain, next_gain, tables, bias_row, fetch=None, carry=(), side_carry=(), ffn_carry=()):
    n = f"l{layer}"
    s = cur.shape[0]
    w_qkv, w_o = have[f"qkv{layer}"], have[f"o{layer}"]
    rider = fetch(carry) if carry else None
    side_rider = fetch(side_carry) if side_carry else None
    keep = {"x": cur, "h1": h1, "w_o": w_o.reshape(D_ATTN, D_MODEL)}
    side = []
    if layer % 2 == 0:
        qkv = matmul(h1, w_qkv, "nn", BF16, n + "_qkv", 1024, 768, 1024, mnk=(s, 3 * D_ATTN, D_MODEL),
                     b_spec=_chip_tile(D_MODEL, 768, lambda i, j, kk: (j, 0, 0)))
        o_sb, st, rode = causal_fwd(qkv, 4, "sb", n + "_sb_fwd", rider=rider)
        o_dil, lse_dil, side = dilated_fwd(qkv, tables, n + "_dil_fwd", rider=side_rider)
        attn = jnp.concatenate([o_sb, o_dil], axis=1).astype(BF16)
        keep.update(o_dil=o_dil, lse_dil=lse_dil, w_qkv=w_qkv)
    else:
        natural = jnp.transpose(w_qkv, (1, 0, 2)).reshape(D_MODEL, N_CHIPS * QKVF_COLS)
        w_gate = jnp.pad(natural[:, 3 * D_ATTN:], ((0, 0), (0, LANES - N_HEADS)))
        qkv = matmul(h1, natural[:, :3 * D_ATTN], "nn", BF16, n + "_qkv", 1024, 768, 1024)
        fl = matmul(h1, w_gate, "nn", F32, n + "_fgate", 512, LANES, 1024)
        cum = forget_fwd(fl, bias_row, n + "_forget_fwd")
        f_heads = cum[:, :N_HEADS].T
        fq = jnp.broadcast_to(f_heads[:, :, None], (N_HEADS, s, LANES))
        fk = f_heads.reshape(N_HEADS // 2, 2, s)
        attn, st, rode = causal_fwd(qkv, 8, "fox", n + "_fox_fwd", fq=fq, fk=fk, rider=rider)
        attn = attn.astype(BF16)
        keep.update(fl=fl, fq=fq, fk=fk, w_qkv=jnp.concatenate([natural[:, :3 * D_ATTN], w_gate], axis=1))
    have.update(zip(carry, rode))
    have.update(zip(side_carry, side))
    w_fi = have[f"fi{layer}"]
    mid, h2 = matmul(attn, keep["w_o"], "nn", F32, n + "_attn_out", 512, 1024, 1024, res=cur, norm_gain=ffn_gain)
    gate, up, act, rode = ffn_in_swiglu(h2, w_fi, n + "_ffn_in", rider=fetch(ffn_carry) if ffn_carry else None)
    have.update(zip(ffn_carry, rode))
    w_fo = have[f"fo{layer}"].reshape(D_FF, D_MODEL)
    if next_gain is None:
        out, h_next = matmul(act, w_fo, "nn", F32, n + "_ffn_out", 512, 1024, D_FF, res=mid), None
    else:
        out, h_next = matmul(act, w_fo, "nn", F32, n + "_ffn_out", 512, 1024, D_FF, res=mid, norm_gain=next_gain)
    keep.update(qkv=qkv, st=st, attn=attn, mid=mid, h2=h2, gate=gate, up=up, act=act, w_fi=w_fi, w_fo=w_fo)
    return out, h_next, keep


def backward_layer(layer, dcur, kp, mix_gain, ffn_gain, tables, bias_row, exchange):
    n = f"l{layer}"
    s = dcur.shape[0]
    fam_qkv, fam_o, li = layer_families(layer)
    g_fo = matmul(kp["act"], dcur, "tn", BF16, n + "_d_w_ffn_out", 1408, 1024, s)
    dact = matmul(dcur, kp["w_fo"], "nt", BF16, n + "_d_act", 1024, 1408, 1024)
    dgu = swiglu_bwd(kp["gate"], kp["up"], dact, n + "_d_swiglu")
    g_fi = matmul(kp["h2"], dgu, "tn", BF16, n + "_d_w_ffn_in", 1024, 1408, 2048, mnk=(D_MODEL, 2 * D_FF, s),
                  o_spec=_chip_tile(D_MODEL, 1408, lambda i, j, kk: (j, 0, 0)), out_shape=(N_CHIPS, D_MODEL, 1408))
    dmid, g_ffn = dh_norm_bwd(dgu, kp["w_fi"], kp["mid"], ffn_gain, dcur, n + "_d_h2", 256)
    g_o = matmul(kp["attn"], dmid, "tn", BF16, n + "_d_w_o", 1024, 1024, s)
    dattn = matmul(dmid, kp["w_o"], "nt", F32, n + "_d_attn", 1024, 1024, 1024)
    exchange.add([(5, layer, g_fo.reshape(N_CHIPS, D_FF // N_CHIPS, D_MODEL)), (4, layer, g_fi),
                  (fam_o, li, g_o.reshape(N_CHIPS, D_ATTN // N_CHIPS, D_MODEL))], f"l{layer}_ffn")
    g_bias = None
    if layer % 2 == 0:
        rider, keys = exchange.rider(only=[(4, layer), (fam_o, li)])
        dq_a, dk_a, dv_a, rode = causal_bwd(kp["qkv"], dattn, kp["st"], 4, "sb", n + "_sb_bwd", rider=rider)
        exchange.landed(keys, rode)
        rider, keys = exchange.rider()
        dq_b, dk_b, dv_b, rode = dilated_bwd(kp["qkv"], tables, dattn, kp["o_dil"], kp["lse_dil"], 4, n + "_dil_bwd", rider=rider)
        dproj = jnp.concatenate([dq_a, dq_b, dk_a.astype(BF16), dk_b, dv_a.astype(BF16), dv_b], axis=1)
        g_qkv = matmul(kp["h1"], dproj, "tn", BF16, n + "_d_w_qkv", 1024, 768, 2048, mnk=(D_MODEL, 3 * D_ATTN, s),
                       o_spec=_chip_tile(D_MODEL, 768, lambda i, j, kk: (j, 0, 0)), out_shape=(N_CHIPS, D_MODEL, 768))
    else:
        rider, keys = exchange.rider()
        dq_f, dk_f, dv_f, dfk, rode = causal_bwd(kp["qkv"], dattn, kp["st"], 8, "fox", n + "_fox_bwd", fq=kp["fq"], fk=kp["fk"],
                                                 rider=rider)
        dcum = jnp.pad(dfk.reshape(N_HEADS, s).T, ((0, 0), (0, LANES - N_HEADS)))
        dfl, dbias = forget_bwd(kp["fl"], bias_row, dcum, n + "_forget_bwd")
        g_bias = dbias[0, :N_HEADS]
        dproj = jnp.concatenate([dq_f, dk_f.astype(BF16), dv_f.astype(BF16), dfl.astype(BF16)], axis=1)
        g_nat = matmul(kp["h1"], dproj, "tn", BF16, n + "_d_w_qkv", 1024, 640, 2048)
        g_qkv = g_nat[:, :N_CHIPS * QKVF_COLS].reshape(D_MODEL, N_CHIPS, QKVF_COLS)
        g_qkv = jnp.transpose(jnp.pad(g_qkv, ((0, 0), (0, 0), (0, QKVF_PAD - QKVF_COLS))), (1, 0, 2))
    exchange.landed(keys, rode)
    exchange.add([(fam_qkv, li, g_qkv)], f"l{layer}_qkv")
    dx, g_mix = dh_norm_bwd(dproj, kp["w_qkv"], kp["x"], mix_gain, dmid, n + "_d_h1", 512)
    return dx, g_mix, g_ffn, g_bias


def local_step(xs, target, norm_mix, norm_ffn, norm_final, b_forget, layer_weights):
    tables = _tables_for(xs.shape[0])
    bias_pad = jnp.pad(b_forget, ((0, 0), (0, LANES - N_HEADS)))
    saved, cur, have = [], xs, {}
    h1 = rmsnorm_fwd(cur, norm_mix[0:1], "l0_norm_mix")
    for layer in range(DEPTH):
        have.update(zip((f"qkv{layer}", f"o{layer}", f"fi{layer}", f"fo{layer}"), layer_weights[layer]))
        next_gain = norm_mix[layer + 1:layer + 2] if layer + 1 < DEPTH else None
        cur, h1, keep = forward_layer(layer, cur, h1, have, norm_ffn[layer:layer + 1], next_gain, tables,
                                      bias_pad[layer // 2:layer // 2 + 1])
        saved.append(keep)
    dcur, g_final, loss_part = loss_head(cur, norm_final.reshape(1, D_MODEL), target, "loss_head")
    keeper = KeepGradients()
    g_mix, g_ffn, g_bias = [None] * DEPTH, [None] * DEPTH, [None] * (DEPTH // 2)
    for layer in reversed(range(DEPTH)):
        dcur, g_mix[layer], g_ffn[layer], g_b = backward_layer(layer, dcur, saved[layer], norm_mix[layer:layer + 1],
                                                               norm_ffn[layer:layer + 1], tables, bias_pad[layer // 2:layer // 2 + 1], keeper)
        if g_b is not None:
            g_bias[layer // 2] = g_b
    return dcur, keeper.grads, (g_mix, g_ffn, g_final, g_bias), loss_part
```

```python
import functools

import jax
import jax.numpy as jnp
from jax import lax
from jax.experimental import pallas as pl
from jax.experimental.pallas import tpu as pltpu

F32 = jnp.float32
BF16 = jnp.bfloat16
MESH = pl.DeviceIdType.MESH

D_MODEL = 1024
DEPTH = 4
HEAD_DIM = 64
N_HEADS = 16
D_ATTN = 1024
D_FF = 2816
ROPE_THETA = 500000.0
ROT_HALF = 8
RMS_EPS = 1e-5
DIL_STRIDES = (1, 4, 16)
ADAM_LR, ADAM_B1, ADAM_B2, ADAM_EPS, ADAM_WD, ADAM_STEP = 0.001, 0.9, 0.999, 1e-8, 0.01, 10

LANES = 128
BLK = 128
VMEM_LIMIT = 56 * 1024 * 1024
NEG = -1e30
N_CHIPS = 4
SMALL_ROWS = 16


def _params(sem=None):
    return pltpu.CompilerParams(dimension_semantics=sem, vmem_limit_bytes=VMEM_LIMIT)


def _dot(a, b):
    return lax.dot_general(a, b, (((1,), (0,)), ((), ())), preferred_element_type=F32)


def _dot_nt(a, b):
    return lax.dot_general(a, b, (((1,), (1,)), ((), ())), preferred_element_type=F32)


def _dot_tn(a, b):
    return lax.dot_general(a, b, (((0,), (0,)), ((), ())), preferred_element_type=F32)


def _split3(x):
    x1 = x.astype(BF16)
    r1 = x - x1.astype(F32)
    x2 = r1.astype(BF16)
    x3 = (r1 - x2.astype(F32)).astype(BF16)
    return x1, x2, x3


def _dot_exact_lhs(x, t):
    x1, x2, x3 = _split3(x)
    return _dot(x1, t) + _dot(x2, t) + _dot(x3, t)


def _dot_exact_rhs(t, x):
    x1, x2, x3 = _split3(x)
    return _dot(t, x1) + _dot(t, x2) + _dot(t, x3)


def _iotas(shape=(BLK, LANES)):
    return lax.broadcasted_iota(jnp.int32, shape, 0), lax.broadcasted_iota(jnp.int32, shape, 1)


_DIMS = {"nn": (((1,), (0,)), ((), ())), "nt": (((1,), (1,)), ((), ())), "tn": (((0,), (0,)), ((), ()))}


def matmul(a, b, mode, out_dtype, name, tm, tn, tk, res=None, mnk=None, b_spec=None, o_spec=None, out_shape=None, norm_gain=None):
    if mnk is not None:
        m, n, k = mnk
    elif mode == "nn":
        (m, k), (k2, n) = a.shape, b.shape
    elif mode == "nt":
        (m, k), (n, k2) = a.shape, b.shape
    else:
        (k, m), (k2, n) = a.shape, b.shape
    assert m % tm == 0 and n % tn == 0 and k % tk == 0, (name, a.shape, b.shape)
    nk = k // tk
    a_spec = pl.BlockSpec((tk, tm), lambda i, j, kk: (kk, i)) if mode == "tn" else pl.BlockSpec((tm, tk), lambda i, j, kk: (i, kk))
    if b_spec is None:
        b_spec = pl.BlockSpec((tn, tk), lambda i, j, kk: (j, kk)) if mode == "nt" else pl.BlockSpec((tk, tn), lambda i, j, kk: (kk, j))
    r_spec = pl.BlockSpec((tm, tn), lambda i, j, kk: (i, j))
    if o_spec is None:
        o_spec = r_spec
    dims = _DIMS[mode]
    has_res, has_norm = res is not None, norm_gain is not None
    assert not has_norm or (tn == n and nk == 1)
    n_in = 2 + int(has_res) + int(has_norm)

    def body(*refs):
        a_ref, b_ref = refs[0], refs[1]
        r_ref = refs[2] if has_res else None
        o_ref = refs[n_in]

        def finish(v):
            if has_res:
                v = v + r_ref[...]
            o_ref[...] = v.astype(out_dtype)
            if has_norm:
                rstd = lax.rsqrt(jnp.mean(v * v, axis=-1, keepdims=True) + RMS_EPS)
                refs[n_in + 1][...] = (v * rstd * refs[n_in - 1][...]).astype(BF16)

        bv = b_ref[...]
        if bv.ndim == 3:
            bv = jnp.concatenate([bv[j] for j in range(bv.shape[0])], axis=1)
        p = lax.dot_general(a_ref[...].astype(BF16), bv.astype(BF16), dims, preferred_element_type=F32)
        if nk == 1:
            finish(p)
        else:
            acc = refs[-1]
            kk = pl.program_id(2)

            @pl.when(kk == 0)
            def _():
                acc[...] = p

            @pl.when(kk > 0)
            def _():
                acc[...] += p

            @pl.when(kk == nk - 1)
            def _():
                finish(acc[...])

    ops = [a, b] + ([res] if has_res else []) + ([norm_gain] if has_norm else [])
    specs = [a_spec, b_spec] + ([r_spec] if has_res else []) + ([pl.BlockSpec((1, tn), lambda i, j, kk: (0, j))] if has_norm else [])
    out_shape = jax.ShapeDtypeStruct((m, n) if out_shape is None else out_shape, out_dtype)
    return pl.pallas_call(
        body, name=name, out_shape=(out_shape, jax.ShapeDtypeStruct((m, n), BF16)) if has_norm else out_shape,
        grid=(m // tm, n // tn, nk), in_specs=specs, out_specs=(o_spec, r_spec) if has_norm else o_spec,
        scratch_shapes=[pltpu.VMEM((tm, tn), F32)] if nk > 1 else [],
        compiler_params=_params(("parallel", "parallel", "arbitrary")),
    )(*ops)


ROWS = 256


def _row_spec(cols, rows=ROWS):
    return pl.BlockSpec((rows, cols), lambda i: (i, 0))


def _fix_spec(r, cols):
    return pl.BlockSpec((r, cols), lambda i: (0, 0))


def rmsnorm_fwd(x, g, name):
    s, d = x.shape

    def body(x_ref, g_ref, h_ref):
        xv = x_ref[...]
        rstd = lax.rsqrt(jnp.mean(xv * xv, axis=-1, keepdims=True) + RMS_EPS)
        h_ref[...] = (xv * rstd * g_ref[...]).astype(BF16)

    return pl.pallas_call(
        body, name=name, out_shape=jax.ShapeDtypeStruct((s, d), BF16), grid=(s // ROWS,),
        in_specs=[_row_spec(d), _fix_spec(1, d)], out_specs=_row_spec(d), compiler_params=_params(("parallel",)),
    )(x, g)


def _rms_bwd_math(xv, gv, dh):
    rstd = lax.rsqrt(jnp.mean(xv * xv, axis=-1, keepdims=True) + RMS_EPS)
    xhat = xv * rstd
    u = dh * gv
    dx = rstd * (u - xhat * jnp.mean(u * xhat, axis=-1, keepdims=True))
    return dx, dh * xhat


def dh_norm_bwd(dy, w, x, g, dres, name, tm):
    s, k = dy.shape
    d = x.shape[1]

    def body(dy_ref, w_ref, x_ref, g_ref, dres_ref, dx_ref, dg_ref):
        wv = w_ref[...]
        if wv.ndim == 3:
            wv = jnp.concatenate([wv[j] for j in range(wv.shape[0])], axis=1)
        dx, dgt = _rms_bwd_math(x_ref[...], g_ref[...], _dot_nt(dy_ref[...], wv))
        dx_ref[...] = dres_ref[...] + dx
        part = jnp.sum(dgt, axis=0, keepdims=True)

        @pl.when(pl.program_id(0) == 0)
        def _():
            dg_ref[...] = part

        @pl.when(pl.program_id(0) > 0)
        def _():
            dg_ref[...] += part

    w_spec = pl.BlockSpec(w.shape, lambda i: (0,) * w.ndim)
    return pl.pallas_call(
        body, name=name, out_shape=(jax.ShapeDtypeStruct((s, d), F32), jax.ShapeDtypeStruct((1, d), F32)), grid=(s // tm,),
        in_specs=[pl.BlockSpec((tm, k), lambda i: (i, 0)), w_spec, _row_spec(d, tm), _fix_spec(1, d), _row_spec(d, tm)],
        out_specs=(_row_spec(d, tm), _fix_spec(1, d)), compiler_params=_params(("arbitrary",)),
    )(dy, w, x, g, dres)


def loss_head(x, g, target, name):
    s, d = x.shape

    def body(x_ref, g_ref, t_ref, dx_ref, dg_ref, loss_ref):
        xv, gv = x_ref[...], g_ref[...]
        rstd = lax.rsqrt(jnp.mean(xv * xv, axis=-1, keepdims=True) + RMS_EPS)
        err = xv * rstd * gv - t_ref[...]
        dx, dgt = _rms_bwd_math(xv, gv, err * (1.0 / d))
        dx_ref[...] = dx
        part = jnp.sum(dgt, axis=0, keepdims=True)
        lpart = jnp.full((1, LANES), 0.5 / d, F32) * jnp.sum(err * err)

        @pl.when(pl.program_id(0) == 0)
        def _():
            dg_ref[...] = part
            loss_ref[...] = lpart

        @pl.when(pl.program_id(0) > 0)
        def _():
            dg_ref[...] += part
            loss_ref[...] += lpart

    return pl.pallas_call(
        body, name=name,
        out_shape=(jax.ShapeDtypeStruct((s, d), F32), jax.ShapeDtypeStruct((1, d), F32), jax.ShapeDtypeStruct((1, LANES), F32)),
        grid=(s // ROWS,), in_specs=[_row_spec(d), _fix_spec(1, d), _row_spec(d)],
        out_specs=(_row_spec(d), _fix_spec(1, d), _fix_spec(1, LANES)), compiler_params=_params(("arbitrary",)),
    )(x, g, target)


def ffn_in_swiglu(h, w_in, name, tm=1024, rider=None):
    s, d = h.shape
    cols = w_in.shape[2]

    def body(h_ref, wg_ref, wu_ref, g_ref, u_ref, a_ref):
        hv = h_ref[...]
        gv, uv = _dot(hv, wg_ref[...]), _dot(hv, wu_ref[...])
        g_ref[...] = gv.astype(BF16)
        u_ref[...] = uv.astype(BF16)
        a_ref[...] = (gv * (1.0 / (1.0 + jnp.exp(-gv))) * uv).astype(BF16)

    tile = pl.BlockSpec((tm, cols), lambda i, j: (i, j))
    out = jax.ShapeDtypeStruct((s, 2 * cols), BF16)
    in_specs = [pl.BlockSpec((tm, d), lambda i, j: (i, 0)), pl.BlockSpec((None, d, cols), lambda i, j: (j, 0, 0)),
                pl.BlockSpec((None, d, cols), lambda i, j: (j + 2, 0, 0))]
    outs, rode = call_with_rider(body, name, rider, [h, w_in, w_in], in_specs, [out, out, out], [tile, tile, tile], [], (s // tm, 2))
    return (*outs, rode)


def swiglu_bwd(gate, up, dact, name):
    s, f = gate.shape

    def body(g_ref, u_ref, da_ref, o_ref):
        gv, uv, da = g_ref[...].astype(F32), u_ref[...].astype(F32), da_ref[...].astype(F32)
        sg = 1.0 / (1.0 + jnp.exp(-gv))
        o_ref[:, :f] = (da * uv * sg * (1.0 + gv * (1.0 - sg))).astype(BF16)
        o_ref[:, f:] = (da * gv * sg).astype(BF16)

    return pl.pallas_call(
        body, name=name, out_shape=jax.ShapeDtypeStruct((s, 2 * f), BF16), grid=(s // ROWS,),
        in_specs=[_row_spec(f)] * 3, out_specs=_row_spec(2 * f), compiler_params=_params(("parallel",)),
    )(gate, up, dact)


Q_OFF, K_OFF, V_OFF = 0, 8, 16


KB = 512
BQ = 512
SUB = KB // BLK


def _softplus_parts(z):
    sp = jnp.log(1.0 + jnp.exp(-jnp.abs(z)))
    ls = jnp.minimum(z, 0.0) - sp
    return ls, ls - z


def _wide(t):
    return jnp.concatenate([t] * SUB, axis=1)


def _chunk_dots(x, tri):
    terms = []
    for u in range(SUB):
        xu = x[:, u * BLK:(u + 1) * BLK]
        hi = xu.astype(BF16)
        terms += [hi, (xu - hi.astype(F32)).astype(BF16)]
    r = _dot(jnp.concatenate(terms, axis=0), tri)
    rows = x.shape[0]
    piece = lambda n: r[n * rows:(n + 1) * rows]
    return [piece(2 * u) + piece(2 * u + 1) for u in range(SUB)]


def _block_suffix_sums(x, suffix, c):
    loc = _chunk_dots(x, suffix)
    out = [None] * SUB
    for u in reversed(range(SUB)):
        out[u] = loc[u] + c
        c = c + jnp.sum(x[:, u * BLK:(u + 1) * BLK], axis=1, keepdims=True)
    return jnp.concatenate(out, axis=1), c


def _block_prefix_sums(x, tri, c):
    loc = _chunk_dots(x, tri)
    out = []
    for u in range(SUB):
        out.append(loc[u] + c)
        c = c + jnp.sum(x[:, u * BLK:(u + 1) * BLK], axis=1, keepdims=True)
    return jnp.concatenate(out, axis=1), c


def causal_fwd(qkv, npairs, mode, name, fq=None, fk=None, rider=None):
    s = qkv.shape[0]
    nq = s // BQ
    fox = mode == "fox"

    def body(*refs):
        if fox:
            q_ref, k_ref, v_ref, fq_ref, fk_ref, o_ref, st_ref = refs
        else:
            q_ref, k_ref, v_ref, o_ref, st_ref = refs
        i = pl.program_id(1)
        nkb = (i * BQ + BQ - 1) // KB + 1
        row, lane = _iotas((BQ, KB))
        row_s, lane_s = _iotas()
        _, lane_q = _iotas((BQ, LANES))
        nfull = (i * BQ) // KB
        qpos = i * BQ + row
        qf = q_ref[...].astype(F32) * 0.125
        hms = (lane_q < HEAD_DIM, lane_q >= HEAD_DIM)
        qas = [jnp.where(hm, qf, 0.0).astype(BF16) for hm in hms]
        suffix = jnp.where(row_s > lane_s, 1.0, 0.0).astype(BF16)
        zero = jnp.zeros((BQ, LANES), F32)
        col0 = jnp.zeros((BQ, 1), F32)

        def kv(j):
            r0 = pl.multiple_of(j * KB, KB)
            return r0, k_ref[pl.ds(r0, KB), :], v_ref[pl.ds(r0, KB), :]

        if fox:
            fqs = [_wide(fq_ref[a]) for a in range(2)]

            def step(j, carry, masked):
                r0, kb, vb = kv(j)
                new = []
                for a in range(2):
                    acc, mx, l = carry[3 * a:3 * a + 3]
                    z = _dot_nt(qas[a], kb) + fqs[a] - fk_ref[a:a + 1, pl.ds(r0, KB)]
                    if masked:
                        z = jnp.where(r0 + lane <= qpos, z, NEG)
                    mnew = jnp.maximum(mx, jnp.max(z, axis=1, keepdims=True))
                    p = jnp.exp(z - mnew)
                    alpha = jnp.exp(mx - mnew)
                    new += [alpha * acc + _dot(p.astype(BF16), vb), mnew, alpha * l + jnp.sum(p, axis=1, keepdims=True)]
                return tuple(new)

            neg = jnp.full((BQ, 1), NEG, F32)
            res = lax.fori_loop(0, nfull, functools.partial(step, masked=False), (zero, neg, col0, zero, neg, col0))
            res = lax.fori_loop(nfull, nkb, functools.partial(step, masked=True), res)
            outs = [res[3 * a] / res[3 * a + 2] for a in range(2)]
            stats = [res[3 * a + 1] + jnp.log(res[3 * a + 2]) for a in range(2)]
        else:
            def step(j, carry, masked):
                r0, kb, vb = kv(j)
                strict = r0 + lane < qpos
                new = []
                for a in range(2):
                    acc, c = carry[2 * a:2 * a + 2]
                    ls, lm = _softplus_parts(_dot_nt(qas[a], kb))
                    if masked:
                        lm = jnp.where(strict, lm, 0.0)
                    between, c = _block_suffix_sums(lm, suffix, c)
                    aw = jnp.exp(ls + between)
                    if masked:
                        aw = jnp.where(strict, aw, 0.0)
                    new += [acc + _dot(aw.astype(BF16), vb), c]
                return tuple(new)

            res = lax.fori_loop(0, nkb - nfull, lambda jj, c: step(nkb - 1 - jj, c, True), (zero, col0, zero, col0))
            res = lax.fori_loop(0, nfull, lambda jj, c: step(nfull - 1 - jj, c, False), res)
            outs, stats = [res[0], res[2]], [res[1], res[3]]
        o_ref[...] = jnp.where(hms[0], outs[0], outs[1])
        for a in range(2):
            st_ref[a] = jnp.broadcast_to(stats[a], (BQ, LANES))

    col = lambda off: (lambda p, i: (0, off + p))
    in_specs = [pl.BlockSpec((BQ, LANES), lambda p, i: (i, Q_OFF + p)),
                pl.BlockSpec((s, LANES), col(K_OFF)), pl.BlockSpec((s, LANES), col(V_OFF))]
    ops = [qkv, qkv, qkv]
    if fox:
        in_specs += [pl.BlockSpec((2, BQ, LANES), lambda p, i: (p, i, 0)), pl.BlockSpec((None, 2, s), lambda p, i: (p, 0, 0))]
        ops += [fq, fk]
    (o, stat), rode = call_with_rider(
        body, name, rider, ops, in_specs,
        [jax.ShapeDtypeStruct((s, npairs * LANES), F32), jax.ShapeDtypeStruct((2 * npairs, s, LANES), F32)],
        [pl.BlockSpec((BQ, LANES), lambda p, i: (i, p)), pl.BlockSpec((2, BQ, LANES), lambda p, i: (p, i, 0))], [], (npairs, nq))
    return o, stat, rode


def causal_bwd(qkv, do, stat, npairs, mode, name, fq=None, fk=None, rider=None):
    s = qkv.shape[0]
    nq = s // BQ
    fox = mode == "fox"

    def body(*refs):
        if fox:
            q_ref, k_ref, v_ref, do_ref, st_ref, fq_ref, fk_ref, dq_ref, dk_ref, dv_ref, df_ref, p_s, dp_s = refs
        else:
            q_ref, k_ref, v_ref, do_ref, st_ref, dq_ref, dk_ref, dv_ref = refs
        i = pl.program_id(1)

        @pl.when(i == 0)
        def _():
            dk_ref[...] = jnp.zeros_like(dk_ref)
            dv_ref[...] = jnp.zeros_like(dv_ref)
            if fox:
                df_ref[...] = jnp.zeros_like(df_ref)

        nkb = (i * BQ + BQ - 1) // KB + 1
        nfull = (i * BQ) // KB
        row, lane = _iotas((BQ, KB))
        row_s, lane_s = _iotas()
        _, lane_q = _iotas((BQ, LANES))
        qpos = i * BQ + row
        qf = q_ref[...].astype(F32) * 0.125
        dov = do_ref[...]
        hms = (lane_q < HEAD_DIM, lane_q >= HEAD_DIM)
        qas = [jnp.where(hm, qf, 0.0).astype(BF16) for hm in hms]
        doas = [jnp.where(hm, dov, 0.0).astype(BF16) for hm in hms]
        stas = [_wide(st_ref[a]) for a in range(2)]
        zero = jnp.zeros((BQ, LANES), F32)
        col0 = jnp.zeros((BQ, 1), F32)

        def kv(j):
            r0 = pl.multiple_of(j * KB, KB)
            return r0, k_ref[pl.ds(r0, KB), :], v_ref[pl.ds(r0, KB), :]

        if fox:
            fqs = [_wide(fq_ref[a]) for a in range(2)]

            def probs(j, deltas, masked):
                r0, kb, vb = kv(j)
                new = []
                for a in range(2):
                    z = _dot_nt(qas[a], kb) + fqs[a] - fk_ref[a:a + 1, pl.ds(r0, KB)]
                    p = jnp.exp(z - stas[a])
                    if masked:
                        p = jnp.where(r0 + lane <= qpos, p, 0.0)
                    dp = _dot_nt(doas[a], vb)
                    p_s[a, j] = p
                    dp_s[a, j] = dp
                    new.append(deltas[a] + jnp.sum(p * dp, axis=1, keepdims=True))
                return tuple(new)

            deltas = lax.fori_loop(0, nfull, functools.partial(probs, masked=False), (col0, col0))
            deltas = lax.fori_loop(nfull, nkb, functools.partial(probs, masked=True), deltas)

            def step(j, dqs):
                r0, kb, _ = kv(j)
                new = []
                dk = jnp.zeros((KB, LANES), F32)
                dv = jnp.zeros((KB, LANES), F32)
                for a in range(2):
                    p = p_s[a, j]
                    ds = p * (dp_s[a, j] - deltas[a])
                    dsb = ds.astype(BF16)
                    dk += _dot_tn(dsb, qas[a])
                    dv += _dot_tn(p.astype(BF16), doas[a])
                    df_ref[a:a + 1, pl.ds(r0, KB)] -= jnp.sum(ds, axis=0, keepdims=True)
                    new.append(dqs[a] + _dot(dsb, kb))
                dk_ref[pl.ds(r0, KB), :] += dk
                dv_ref[pl.ds(r0, KB), :] += dv
                return tuple(new)

            dqs = lax.fori_loop(0, nkb, step, (zero, zero))
        else:
            incl = jnp.where(row_s <= lane_s, 1.0, 0.0).astype(BF16)
            excl = jnp.where(row_s < lane_s, 1.0, 0.0).astype(BF16)

            def step(j, carry, masked):
                r0, kb, vb = kv(j)
                strict = r0 + lane < qpos
                new = []
                dk = jnp.zeros((KB, LANES), F32)
                dv = jnp.zeros((KB, LANES), F32)
                for a in range(2):
                    dq, cm, cg = carry[3 * a:3 * a + 3]
                    ls, lm = _softplus_parts(_dot_nt(qas[a], kb))
                    if masked:
                        lm = jnp.where(strict, lm, 0.0)
                    beta = jnp.exp(ls)
                    upto, cm = _block_prefix_sums(lm, incl, cm)
                    aw = jnp.exp(ls + stas[a] - upto)
                    if masked:
                        aw = jnp.where(strict, aw, 0.0)
                    g = aw * _dot_nt(doas[a], vb)
                    pre, cg = _block_prefix_sums(g, excl, cg)
                    dz = g * (1.0 - beta) - pre * beta
                    if masked:
                        dz = jnp.where(strict, dz, 0.0)
                    dzb = dz.astype(BF16)
                    dk += _dot_tn(dzb, qas[a])
                    dv += _dot_tn(aw.astype(BF16), doas[a])
                    new += [dq + _dot(dzb, kb), cm, cg]
                dk_ref[pl.ds(r0, KB), :] += dk
                dv_ref[pl.ds(r0, KB), :] += dv
                return tuple(new)

            res = lax.fori_loop(0, nfull, functools.partial(step, masked=False), (zero, col0, col0, zero, col0, col0))
            res = lax.fori_loop(nfull, nkb, functools.partial(step, masked=True), res)
            dqs = (res[0], res[3])
        dq_ref[...] = (jnp.where(hms[0], dqs[0], dqs[1]) * 0.125).astype(BF16)

    col = lambda off: (lambda p, i: (0, off + p))
    blk = pl.BlockSpec((BQ, LANES), lambda p, i: (i, p))
    acc = pl.BlockSpec((s, LANES), lambda p, i: (0, p))
    st_spec = pl.BlockSpec((2, BQ, LANES), lambda p, i: (p, i, 0))
    in_specs = [pl.BlockSpec((BQ, LANES), lambda p, i: (i, Q_OFF + p)), pl.BlockSpec((s, LANES), col(K_OFF)),
                pl.BlockSpec((s, LANES), col(V_OFF)), blk, st_spec]
    ops = [qkv, qkv, qkv, do, stat]
    w = npairs * LANES
    out_shape = [jax.ShapeDtypeStruct((s, w), BF16), jax.ShapeDtypeStruct((s, w), F32), jax.ShapeDtypeStruct((s, w), F32)]
    out_specs = [blk, acc, acc]
    scratch = []
    if fox:
        fk_spec = pl.BlockSpec((None, 2, s), lambda p, i: (p, 0, 0))
        in_specs += [st_spec, fk_spec]
        ops += [fq, fk]
        out_shape.append(jax.ShapeDtypeStruct((npairs, 2, s), F32))
        out_specs.append(fk_spec)
        scratch = [pltpu.VMEM((2, s // KB, BQ, KB), F32)] * 2
    outs, rode = call_with_rider(body, name, rider, ops, in_specs, out_shape, out_specs, scratch, (npairs, nq))
    return (*outs, rode)


def forget_fwd(fl, bias, name):
    s = fl.shape[0]

    def body(fl_ref, b_ref, f_ref):
        row, lane = _iotas()
        lower = jnp.where(lane <= row, 1.0, 0.0).astype(BF16)

        def step(n, carry):
            r0 = pl.multiple_of(n * BLK, BLK)
            ls, _ = _softplus_parts(fl_ref[pl.ds(r0, BLK), :] + b_ref[...])
            blk = _dot_exact_rhs(lower, ls) + carry
            f_ref[pl.ds(r0, BLK), :] = blk
            return blk[BLK - 1:BLK, :]

        lax.fori_loop(0, s // BLK, step, jnp.zeros((1, LANES), F32))

    return pl.pallas_call(
        body, name=name, out_shape=jax.ShapeDtypeStruct((s, LANES), F32),
        in_specs=[pl.BlockSpec(memory_space=pltpu.VMEM)] * 2, out_specs=pl.BlockSpec(memory_space=pltpu.VMEM),
        compiler_params=_params(),
    )(fl, bias)


def forget_bwd(fl, bias, df, name):
    s = fl.shape[0]
    nb = s // BLK

    def body(fl_ref, b_ref, df_ref, o_ref, db_ref):
        row, lane = _iotas()
        upper = jnp.where(lane >= row, 1.0, 0.0).astype(BF16)

        def step(nn, carry):
            tail, db = carry
            r0 = pl.multiple_of((nb - 1 - nn) * BLK, BLK)
            dls = _dot_exact_rhs(upper, df_ref[pl.ds(r0, BLK), :]) + tail
            xv = fl_ref[pl.ds(r0, BLK), :] + b_ref[...]
            dfl = dls * (1.0 / (1.0 + jnp.exp(xv)))
            o_ref[pl.ds(r0, BLK), :] = dfl
            return dls[0:1, :], db + jnp.sum(dfl, axis=0, keepdims=True)

        _, db = lax.fori_loop(0, nb, step, (jnp.zeros((1, LANES), F32), jnp.zeros((1, LANES), F32)))
        db_ref[...] = db

    return pl.pallas_call(
        body, name=name, out_shape=(jax.ShapeDtypeStruct((s, LANES), F32), jax.ShapeDtypeStruct((1, LANES), F32)),
        in_specs=[pl.BlockSpec(memory_space=pltpu.VMEM)] * 3,
        out_specs=(pl.BlockSpec(memory_space=pltpu.VMEM), pl.BlockSpec(memory_space=pltpu.VMEM)),
        compiler_params=_params(),
    )(fl, bias, df)


def _rot_tables(s):
    inv = ROPE_THETA ** (-jnp.arange(ROT_HALF, dtype=F32) * 2.0 / (2 * ROT_HALF))
    ang = jnp.arange(s, dtype=F32)[:, None] * inv[None, :]
    cos, sin = jnp.cos(ang), jnp.sin(ang)
    z8 = jnp.zeros((s, ROT_HALF), F32)
    rest = HEAD_DIM - 2 * ROT_HALF
    zr, onr = jnp.zeros((s, rest), F32), jnp.ones((s, rest), F32)
    tile = lambda t: jnp.tile(t, (1, 2))
    return tile(jnp.concatenate([cos, cos, onr], 1)), tile(jnp.concatenate([-sin, z8, zr], 1)), tile(jnp.concatenate([z8, sin, zr], 1))


def _deinterleave(dst, src_ref, stride, s, dtype):
    length = s // stride
    for r in range(stride):
        if stride == 1:
            dst[...] = src_ref[...].astype(dtype)
        else:
            dst[r * length:(r + 1) * length, :] = src_ref[pl.ds(r, length, stride=stride), :].astype(dtype)


def _band_masks(row, lane, first):
    return lane <= row, lane >= row + jnp.where(first, BLK, 0)


N_DIL_PAIRS = 4


def _rotate_into(q_ref, k_ref, v_ref, c_ref, s1_ref, s2_ref, qr, kr, vr):
    c, s1, s2 = c_ref[...], s1_ref[...], s2_ref[...]
    rot = lambda xv: xv * c + pltpu.roll(xv, LANES - ROT_HALF, 1) * s1 + pltpu.roll(xv, ROT_HALF, 1) * s2
    qr[...] = rot(q_ref[...].astype(F32)) * 0.125
    kr[...] = rot(k_ref[...].astype(F32))
    vr[...] = v_ref[...].astype(F32)


def _dilated_operands(qkv, tables):
    s = qkv.shape[0]
    col = lambda off: pl.BlockSpec((s, LANES), lambda p: (0, off + N_DIL_PAIRS + p))
    table = pl.BlockSpec((s, LANES), lambda p: (0, 0))
    return [qkv, qkv, qkv, *tables], [col(Q_OFF), col(K_OFF), col(V_OFF), table, table, table]


def dilated_fwd(qkv, tables, name, rider=None):
    s = qkv.shape[0]
    npairs, w = N_DIL_PAIRS, N_DIL_PAIRS * LANES
    nblk = s // BLK

    def body(q_in, k_in, v_in, c_ref, s1_ref, s2_ref, o_ref, lse_ref, q_ref, k_ref, v_ref, qs, ks, vs, od, ld, on, ln):
        row, lane = _iotas()
        _rotate_into(q_in, k_in, v_in, c_ref, s1_ref, s2_ref, q_ref, k_ref, v_ref)
        for pi, stride in enumerate(DIL_STRIDES):
            per = (s // stride) // BLK
            _deinterleave(qs, q_ref, stride, s, BF16)
            _deinterleave(ks, k_ref, stride, s, BF16)
            _deinterleave(vs, v_ref, stride, s, BF16)

            def block(b, carry):
                r0 = pl.multiple_of(b * BLK, BLK)
                rp = pl.multiple_of(jnp.maximum(b - 1, 0) * BLK, BLK)
                mc, mp = _band_masks(row, lane, b % per == 0)
                q = qs[pl.ds(r0, BLK), :]
                kc, kp, vc, vp = ks[pl.ds(r0, BLK), :], ks[pl.ds(rp, BLK), :], vs[pl.ds(r0, BLK), :], vs[pl.ds(rp, BLK), :]
                out = jnp.zeros((BLK, LANES), F32)
                lse = jnp.zeros((BLK, LANES), F32)
                for a in range(2):
                    hm = (lane < HEAD_DIM) if a == 0 else (lane >= HEAD_DIM)
                    qa = jnp.where(hm, q.astype(F32), 0.0).astype(BF16)
                    sc = jnp.where(mc, _dot_nt(qa, kc), NEG)
                    sp = jnp.where(mp, _dot_nt(qa, kp), NEG)
                    mx = jnp.maximum(jnp.max(sc, axis=1, keepdims=True), jnp.max(sp, axis=1, keepdims=True))
                    pc, pp = jnp.exp(sc - mx), jnp.exp(sp - mx)
                    l = jnp.sum(pc, axis=1, keepdims=True) + jnp.sum(pp, axis=1, keepdims=True)
                    oa = (_dot(pc.astype(BF16), vc) + _dot(pp.astype(BF16), vp)) / l
                    out = jnp.where(hm, oa, out)
                    lse = jnp.where(hm, mx + jnp.log(l), lse)
                od[pl.ds(r0, BLK), :] = out
                ld[pl.ds(r0, BLK), :] = lse
                return carry

            lax.fori_loop(0, nblk, block, 0, unroll=2)
            length = s // stride
            for r in range(stride):
                if stride == 1:
                    on[pi] = od[...]
                    ln[pi] = ld[...]
                else:
                    on[pi, pl.ds(r, length, stride=stride), :] = od[r * length:(r + 1) * length, :]
                    ln[pi, pl.ds(r, length, stride=stride), :] = ld[r * length:(r + 1) * length, :]

        def merge(n, carry):
            r0 = pl.multiple_of(n * BLK, BLK)
            ls = [ln[pi, pl.ds(r0, BLK), :] for pi in range(3)]
            mx = jnp.maximum(jnp.maximum(ls[0], ls[1]), ls[2])
            ws = [jnp.exp(lv - mx) for lv in ls]
            den = ws[0] + ws[1] + ws[2]
            num = ws[0] * on[0, pl.ds(r0, BLK), :] + ws[1] * on[1, pl.ds(r0, BLK), :] + ws[2] * on[2, pl.ds(r0, BLK), :]
            o_ref[pl.ds(r0, BLK), :] = num / den
            lse_ref[pl.ds(r0, BLK), :] = mx + jnp.log(den)
            return carry

        lax.fori_loop(0, nblk, merge, 0, unroll=2)

    colspec = pl.BlockSpec((s, LANES), lambda p: (0, p))
    out = jax.ShapeDtypeStruct((s, w), F32)
    scratch = ([pltpu.VMEM((s, LANES), F32)] * 3 + [pltpu.VMEM((s, LANES), BF16)] * 3 + [pltpu.VMEM((s, LANES), F32)] * 2
               + [pltpu.VMEM((3, s, LANES), F32)] * 2)
    ops, in_specs = _dilated_operands(qkv, tables)
    (o, lse), rode = call_with_rider(body, name, rider, ops, in_specs, [out, out], [colspec, colspec], scratch, (npairs,))
    return o, lse, rode


def dilated_bwd(qkv, tables, do, out, lse, do_off, name, rider=None):
    s = qkv.shape[0]
    npairs, w = N_DIL_PAIRS, N_DIL_PAIRS * LANES
    nblk = s // BLK

    def body(q_in, k_in, v_in, c_ref, s1_ref, s2_ref, do_ref, out_ref, lse_ref, dq_out, dk_out, dv_out,
             q_ref, k_ref, v_ref, dq_ref, dk_ref, dv_ref, qs, ks, vs, dos, dls, lss, dqd, dkd, dvd, dln):
        row, lane = _iotas()
        same_head = jnp.where((row < HEAD_DIM) == (lane < HEAD_DIM), 1.0, 0.0).astype(BF16)
        _rotate_into(q_in, k_in, v_in, c_ref, s1_ref, s2_ref, q_ref, k_ref, v_ref)

        def delta_blk(n, carry):
            r0 = pl.multiple_of(n * BLK, BLK)
            dln[pl.ds(r0, BLK), :] = _dot_exact_lhs(do_ref[pl.ds(r0, BLK), :] * out_ref[pl.ds(r0, BLK), :], same_head)
            return carry

        lax.fori_loop(0, nblk, delta_blk, 0, unroll=2)
        for pi, stride in enumerate(DIL_STRIDES):
            per = (s // stride) // BLK
            _deinterleave(qs, q_ref, stride, s, BF16)
            _deinterleave(ks, k_ref, stride, s, BF16)
            _deinterleave(vs, v_ref, stride, s, BF16)
            _deinterleave(dos, do_ref, stride, s, BF16)
            _deinterleave(dls, dln, stride, s, F32)
            _deinterleave(lss, lse_ref, stride, s, F32)

            def block(b, carry):
                r0 = pl.multiple_of(b * BLK, BLK)
                rp = pl.multiple_of(jnp.maximum(b - 1, 0) * BLK, BLK)
                first = b % per == 0
                mc, mp = _band_masks(row, lane, first)
                q, dov = qs[pl.ds(r0, BLK), :], dos[pl.ds(r0, BLK), :]
                kc, kp, vc, vp = ks[pl.ds(r0, BLK), :], ks[pl.ds(rp, BLK), :], vs[pl.ds(r0, BLK), :], vs[pl.ds(rp, BLK), :]
                lse_t, dl_t = lss[pl.ds(r0, BLK), :], dls[pl.ds(r0, BLK), :]
                dq = jnp.zeros((BLK, LANES), F32)
                dkc = jnp.zeros((BLK, LANES), F32)
                dkp = jnp.zeros((BLK, LANES), F32)
                dvc = jnp.zeros((BLK, LANES), F32)
                dvp = jnp.zeros((BLK, LANES), F32)
                for a in range(2):
                    hm = (lane < HEAD_DIM) if a == 0 else (lane >= HEAD_DIM)
                    pick = lane == a * HEAD_DIM
                    qa = jnp.where(hm, q.astype(F32), 0.0).astype(BF16)
                    doa = jnp.where(hm, dov.astype(F32), 0.0).astype(BF16)
                    lse_a = jnp.sum(jnp.where(pick, lse_t, 0.0), axis=1, keepdims=True)
                    dl_a = jnp.sum(jnp.where(pick, dl_t, 0.0), axis=1, keepdims=True)
                    pc = jnp.where(mc, jnp.exp(_dot_nt(qa, kc) - lse_a), 0.0)
                    pp = jnp.where(mp, jnp.exp(_dot_nt(qa, kp) - lse_a), 0.0)
                    dsc = (pc * (_dot_nt(doa, vc) - dl_a)).astype(BF16)
                    dsp = (pp * (_dot_nt(doa, vp) - dl_a)).astype(BF16)
                    dq = jnp.where(hm, _dot(dsc, kc) + _dot(dsp, kp), dq)
                    dkc += _dot_tn(dsc, qa)
                    dkp += _dot_tn(dsp, qa)
                    dvc += _dot_tn(pc.astype(BF16), doa)
                    dvp += _dot_tn(pp.astype(BF16), doa)
                dqd[pl.ds(r0, BLK), :] = dq
                dkd[pl.ds(r0, BLK), :] = dkc
                dvd[pl.ds(r0, BLK), :] = dvc

                @pl.when(jnp.logical_not(first))
                def _():
                    dkd[pl.ds(rp, BLK), :] += dkp
                    dvd[pl.ds(rp, BLK), :] += dvp

                return carry

            lax.fori_loop(0, nblk, block, 0, unroll=2)
            length = s // stride
            for dst, src in ((dq_ref, dqd), (dk_ref, dkd), (dv_ref, dvd)):
                for r in range(stride):
                    if stride == 1:
                        dst[...] = src[...]
                    else:
                        dst[pl.ds(r, length, stride=stride), :] += src[r * length:(r + 1) * length, :]

        c, s1, s2 = c_ref[...], s1_ref[...], s2_ref[...]
        rot_t = lambda dy: dy * c + pltpu.roll(dy * s1, ROT_HALF, 1) + pltpu.roll(dy * s2, LANES - ROT_HALF, 1)
        dq_out[...] = (rot_t(dq_ref[...]) * 0.125).astype(BF16)
        dk_out[...] = rot_t(dk_ref[...]).astype(BF16)
        dv_out[...] = dv_ref[...].astype(BF16)

    colspec = pl.BlockSpec((s, LANES), lambda p: (0, p))
    do_spec = pl.BlockSpec((s, LANES), lambda p: (0, do_off + p))
    o3 = jax.ShapeDtypeStruct((s, w), BF16)
    scratch = [pltpu.VMEM((s, LANES), F32)] * 6 + [pltpu.VMEM((s, LANES), BF16)] * 4 + [pltpu.VMEM((s, LANES), F32)] * 6
    ops, in_specs = _dilated_operands(qkv, tables)
    outs, rode = call_with_rider(body, name, rider, ops + [do, out, lse], in_specs + [do_spec, colspec, colspec],
                                 [o3, o3, o3], [colspec, colspec, colspec], scratch, (npairs,))
    return (*outs, rode)


def adamw(w, g, m, v, name):
    rows, cols = w.shape
    rb = min(rows, ROWS)
    c1 = 1.0 - ADAM_B1 ** ADAM_STEP
    c2 = 1.0 - ADAM_B2 ** ADAM_STEP

    def body(w_ref, g_ref, m_ref, v_ref, d_ref, mo_ref, vo_ref):
        gv = g_ref[...]
        mn = ADAM_B1 * m_ref[...] + (1.0 - ADAM_B1) * gv
        vn = ADAM_B2 * v_ref[...] + (1.0 - ADAM_B2) * (gv * gv)
        d_ref[...] = -ADAM_LR * ((mn / c1) / (jnp.sqrt(vn / c2) + ADAM_EPS) + ADAM_WD * w_ref[...])
        mo_ref[...] = mn
        vo_ref[...] = vn

    spec = _row_spec(cols, rb)
    out = jax.ShapeDtypeStruct((rows, cols), F32)
    return pl.pallas_call(
        body, name=name, out_shape=(out, out, out), grid=(rows // rb,), in_specs=[spec] * 4, out_specs=(spec,) * 3,
        compiler_params=_params(("parallel",)),
    )(w, g, m, v)


def _prefetch_call(body, name, scalar, ops, grid, in_specs, out_specs, out_shape, sem):
    spec = pltpu.PrefetchScalarGridSpec(num_scalar_prefetch=1, grid=grid, in_specs=in_specs, out_specs=out_specs)
    return pl.pallas_call(body, name=name, grid_spec=spec, out_shape=out_shape, compiler_params=_params(sem))(scalar, *ops)


def pair_sums(gs, gots, core, name):
    n = len(gs)

    def body(core_ref, *refs):
        for f in range(n):
            refs[2 * n + f][...] = (refs[f][...].astype(F32) + refs[n + f][...].astype(F32)).astype(BF16)

    blk = lambda g, rows_of: pl.BlockSpec((None, g.shape[1] // 2, g.shape[2]), rows_of)
    mine = [blk(g, lambda j, core_ref: (j, core_ref[0], 0)) for g in gs]
    half = [blk(g, lambda j, core_ref: (j, 0, 0)) for g in gs]
    outs = tuple(jax.ShapeDtypeStruct((g.shape[0], g.shape[1] // 2, g.shape[2]), BF16) for g in gs)
    return _prefetch_call(body, name, core, (*gs, *gots), (N_CHIPS,), mine + half, tuple(half), outs, ("parallel",))


def chip_sums(pairs, gots, chip, layers, intos, name):
    n = len(pairs)

    def body(chip_ref, *refs):
        for f in range(n):
            p_ref, a_ref, b_ref, c_ref = refs[4 * f:4 * f + 4]
            refs[5 * n + f][...] = ((p_ref[...].astype(F32) + a_ref[...].astype(F32)) + b_ref[...].astype(F32)) + c_ref[...].astype(F32)

    in_specs, ops = [], []
    for p, got in zip(pairs, gots):
        blk = lambda at, p=p: pl.BlockSpec((None,) + p.shape[1:], at)
        in_specs += [blk(lambda i, chip_ref: (chip_ref[0], 0, 0))] + [blk(lambda i, chip_ref, k=k: (k, 0, 0)) for k in range(3)]
        ops += [p, got, got, got]
    out_specs = tuple(pl.BlockSpec((None,) + p.shape[1:], lambda i, chip_ref, l=l: (l, 0, 0)) for p, l in zip(pairs, layers))
    spec = pltpu.PrefetchScalarGridSpec(num_scalar_prefetch=1, grid=(1,), in_specs=in_specs + [_ANY] * n, out_specs=out_specs)
    return pl.pallas_call(
        body, name=name, grid_spec=spec, out_shape=tuple(jax.ShapeDtypeStruct(t.shape, t.dtype) for t in intos),
        input_output_aliases={1 + 4 * n + f: f for f in range(n)}, compiler_params=_params(("arbitrary",)),
    )(chip, *ops, *intos)


def adamw_family(w, m, v, g_mine, g_other, core, name):
    nl, r, c = w.shape
    gc = g_mine.shape[2]
    rh = r // 2
    nb = 4 if rh % 512 == 0 else (2 if rh % 16 == 0 and rh > 256 else 1)
    rb = rh // nb
    c1 = 1.0 - ADAM_B1 ** ADAM_STEP
    c2 = 1.0 - ADAM_B2 ** ADAM_STEP

    def body(core_ref, w_ref, m_ref, v_ref, gm_ref, go_ref, g_ref, d_ref, mo_ref, vo_ref):
        gv = jnp.where(pl.program_id(1) == core_ref[0], gm_ref[...], go_ref[...])[:, :c]
        mn = ADAM_B1 * m_ref[...] + (1.0 - ADAM_B1) * gv
        vn = ADAM_B2 * v_ref[...] + (1.0 - ADAM_B2) * (gv * gv)
        g_ref[...] = gv
        d_ref[...] = -ADAM_LR * ((mn / c1) / (jnp.sqrt(vn / c2) + ADAM_EPS) + ADAM_WD * w_ref[...])
        mo_ref[...] = mn
        vo_ref[...] = vn

    full = pl.BlockSpec((None, rb, c), lambda l, h, i, core_ref: (l, h * nb + i, 0))
    mine = pl.BlockSpec((None, rb, gc), lambda l, h, i, core_ref: (l, jnp.where(h == core_ref[0], i, 0), 0))
    other = pl.BlockSpec((None, rb, gc), lambda l, h, i, core_ref: (l, jnp.where(h == core_ref[0], 0, i), 0))
    out = jax.ShapeDtypeStruct((nl, r, c), F32)
    return _prefetch_call(body, name, core, (w, m, v, g_mine, g_other), (nl, 2, nb), [full, full, full, mine, other],
                          (full, full, full, full), (out, out, out, out), ("parallel", "parallel", "parallel"))


def _coords():
    return lax.axis_index("x"), lax.axis_index("y"), lax.axis_index("c")


def _other_chips(x, y):
    return ((1 - x, 1 - y), (1 - x, y), (x, 1 - y))


_ANY = pl.BlockSpec(memory_space=pl.ANY)


def _exchange_call(body, name, arrays, out_shapes, n_copies, n_local=0):
    n = len(arrays)

    def wrapped(*refs):
        body(refs[:n], refs[n:n + len(out_shapes)], *refs[n + len(out_shapes):])

    scratch = [pltpu.SemaphoreType.DMA((n_copies,)), pltpu.SemaphoreType.DMA((n_copies,))]
    if n_local:
        scratch.append(pltpu.SemaphoreType.DMA((n_local,)))
    return pl.pallas_call(
        wrapped, name=name, out_shape=tuple(out_shapes), in_specs=[_ANY] * n, out_specs=tuple([_ANY] * len(out_shapes)),
        scratch_shapes=scratch, compiler_params=_params(),
    )(*arrays)


def _remote(send_sems, recv_sems, n, src, dst, to):
    return pltpu.make_async_remote_copy(src_ref=src, dst_ref=dst, send_sem=send_sems.at[n], recv_sem=recv_sems.at[n],
                                        device_id=to, device_id_type=MESH)


class Rider:
    def __init__(self, arrays, out_shapes, n_remote, n_local, copies, then=None):
        self.arrays, self.out_shapes, self.n_remote, self.n_local = list(arrays), list(out_shapes), n_remote, n_local
        self.copies, self.then = copies, then

    def sems(self):
        return [pltpu.SemaphoreType.DMA((self.n_remote,)), pltpu.SemaphoreType.DMA((self.n_remote,)),
                pltpu.SemaphoreType.DMA((max(self.n_local, 1),))]

    def run(self, name):
        n, no = len(self.arrays), len(self.out_shapes)

        def body(*refs):
            for stage in (self.copies, self.then):
                if stage is not None:
                    cps = stage(refs[:n], refs[n:n + no], *refs[n + no:])
                    for cp in cps:
                        cp.start()
                    for cp in cps:
                        cp.wait()

        return pl.pallas_call(
            body, name=name, out_shape=tuple(self.out_shapes), in_specs=[_ANY] * n, out_specs=tuple([_ANY] * no),
            scratch_shapes=self.sems(), compiler_params=_params(),
        )(*self.arrays)


def ride(rider, body, n_in, n_out, grid):
    if rider is None:
        return body
    ni, no = len(rider.arrays), len(rider.out_shapes)

    def wrapped(*refs):
        ins, r_in = refs[:n_in], refs[n_in:n_in + ni]
        outs = refs[n_in + ni:n_in + ni + n_out]
        r_out = refs[n_in + ni + n_out:n_in + ni + n_out + no]
        rest = refs[n_in + ni + n_out + no:]
        scratch, sems = rest[:len(rest) - 3], rest[len(rest) - 3:]
        step, total = 0, 1
        for a, g in enumerate(grid):
            step, total = step * g + pl.program_id(a), total * g
        assert total >= 3
        relay_at = (7 * total) // 8 if rider.then is not None else total - 1

        @pl.when(step == 0)
        def _():
            for cp in rider.copies(r_in, r_out, *sems):
                cp.start()

        body(*ins, *outs, *scratch)

        @pl.when(step == relay_at)
        def _():
            for cp in rider.copies(r_in, r_out, *sems):
                cp.wait()
            if rider.then is not None:
                for cp in rider.then(r_in, r_out, *sems):
                    cp.start()

        if rider.then is not None:
            @pl.when(step == total - 1)
            def _():
                for cp in rider.then(r_in, r_out, *sems):
                    cp.wait()

    return wrapped


def call_with_rider(body, name, rider, ops, in_specs, out_shape, out_specs, scratch, grid):
    n_in, n_out = len(ops), len(out_shape)
    ops, in_specs, out_shape, out_specs, scratch = list(ops), list(in_specs), list(out_shape), list(out_specs), list(scratch)
    if rider is not None:
        ops += rider.arrays
        in_specs += [_ANY] * len(rider.arrays)
        out_shape += rider.out_shapes
        out_specs += [_ANY] * len(rider.out_shapes)
        scratch += rider.sems()
    res = pl.pallas_call(
        ride(rider, body, n_in, n_out, grid), name=name, out_shape=tuple(out_shape), grid=grid, in_specs=in_specs,
        out_specs=tuple(out_specs), scratch_shapes=scratch, compiler_params=_params(("arbitrary",) * len(grid)),
    )(*ops)
    return tuple(res[:n_out]), list(res[n_out:])


def gather_rider(shards):
    nf = len(shards)
    half = lambda ref, which: pl.ds(which * (ref.shape[-2] // 2), ref.shape[-2] // 2)

    def copies(s_refs, o_refs, send_sems, recv_sems, local_sems):
        x, y, c = _coords()
        me = 2 * x + y
        cps = [pltpu.make_async_copy(s_refs[f], o_refs[f].at[me], local_sems.at[f]) for f in range(nf)]
        for k, (px, py) in enumerate(_other_chips(x, y)):
            for f in range(nf):
                rows = half(s_refs[f], c)
                cps.append(_remote(send_sems, recv_sems, k * nf + f, s_refs[f].at[rows], o_refs[f].at[me, rows], (px, py, c)))
        return cps

    def relay(s_refs, o_refs, send_sems, recv_sems, local_sems):
        x, y, c = _coords()
        cps = []
        for k, (px, py) in enumerate(_other_chips(x, y)):
            for f in range(nf):
                landed = o_refs[f].at[2 * px + py, half(s_refs[f], c)]
                cps.append(_remote(send_sems, recv_sems, (3 + k) * nf + f, landed, landed, (x, y, 1 - c)))
        return cps

    return Rider(shards, [jax.ShapeDtypeStruct((N_CHIPS,) + sh.shape, sh.dtype) for sh in shards], 6 * nf, nf, copies, relay)


def scatter_rider(pairs):
    nf = len(pairs)

    def copies(p_refs, o_refs, send_sems, recv_sems, local_sems):
        x, y, c = _coords()
        cps = []
        for k, (px, py) in enumerate(_other_chips(x, y)):
            for f in range(nf):
                cps.append(_remote(send_sems, recv_sems, k * nf + f, p_refs[f].at[2 * px + py], o_refs[f].at[k], (px, py, c)))
        return cps

    return Rider(pairs, [jax.ShapeDtypeStruct((3,) + p.shape[1:], p.dtype) for p in pairs], 3 * nf, 0, copies)


def pair_swap(grads, name):
    def body(g_refs, o_refs, send_sems, recv_sems):
        x, y, c = _coords()
        cps = []
        for f, g_ref in enumerate(g_refs):
            rh = g_ref.shape[1] // 2
            cps.append(_remote(send_sems, recv_sems, f, g_ref.at[:, pl.ds((1 - c) * rh, rh), :], o_refs[f], (x, y, 1 - c)))
        for cp in cps:
            cp.start()
        for cp in cps:
            cp.wait()

    outs = [jax.ShapeDtypeStruct((g.shape[0], g.shape[1] // 2, g.shape[2]), g.dtype) for g in grads]
    return _exchange_call(body, name, grads, outs, len(grads))


def half_swap(halves, name):
    def body(h_refs, o_refs, send_sems, recv_sems):
        x, y, c = _coords()
        cps = [_remote(send_sems, recv_sems, f, h_ref, o_refs[f], (x, y, 1 - c)) for f, h_ref in enumerate(h_refs)]
        for cp in cps:
            cp.start()
        for cp in cps:
            cp.wait()

    return _exchange_call(body, name, halves, [jax.ShapeDtypeStruct(h.shape, h.dtype) for h in halves], len(halves))


def allsum_small(part, name):
    def body(p_ref, tot_ref, all_ref, send_sems, recv_sems):
        x, y, c = _coords()
        me, sibling = (x, y, c), (x, y, 1 - c)
        chips = _other_chips(x, y)

        def slot(px, py, pc):
            return all_ref.at[4 * px + 2 * py + pc]

        def copy(k, block, to, src=None):
            return pltpu.make_async_remote_copy(src_ref=slot(*block) if src is None else src, dst_ref=slot(*block),
                                                send_sem=send_sems.at[k], recv_sem=recv_sems.at[k], device_id=to, device_id_type=MESH)

        slot(*me)[...] = p_ref[...]
        first = [copy(0, me, sibling, src=p_ref)] + [copy(1 + j, me, (*chip, c), src=p_ref) for j, chip in enumerate(chips)]
        for cp in first:
            cp.start()
        passed = [copy(4 + j, (*chip, c), sibling) for j, chip in enumerate(chips)]
        for j, chip in enumerate(chips):
            copy(1 + j, (*chip, c), me).wait_recv()
            passed[j].start()
        copy(0, sibling, me).wait_recv()
        for j, chip in enumerate(chips):
            copy(4 + j, (*chip, 1 - c), me).wait_recv()
        for cp in first + passed:
            cp.wait_send()
        tot = all_ref[0]
        for d in range(1, 8):
            tot = tot + all_ref[d]
        tot_ref[...] = tot

    vm = pl.BlockSpec(memory_space=pltpu.VMEM)
    return pl.pallas_call(
        body, name=name, out_shape=jax.ShapeDtypeStruct(part.shape, F32), in_specs=[vm], out_specs=vm,
        scratch_shapes=[pltpu.VMEM((8,) + part.shape, F32), pltpu.SemaphoreType.DMA((7,)), pltpu.SemaphoreType.DMA((7,))],
        compiler_params=_params(),
    )(part)


QKVF_COLS = 772
QKVF_PAD = 896
FORWARD_CARRY = {0: (("fi0",), ("qkv1", "o1", "fo1"), ("fo0",)), 1: (("fi1", "qkv2"), (), ("o2",)),
                 2: (("fi2",), ("qkv3", "o3", "fo3"), ("fo2",)), 3: (("fi3",), (), ())}


def _tables_for(s):
    return _rot_tables(s)


def layer_families(layer):
    return (0, 1, layer // 2) if layer % 2 == 0 else (2, 3, layer // 2)


class GradientExchange:
    def __init__(self, family_layers):
        self.core = lax.axis_index("c").astype(jnp.int32).reshape(1)
        self.chip = (2 * lax.axis_index("x") + lax.axis_index("y")).astype(jnp.int32).reshape(1)
        self.family_layers = family_layers
        self.pairs, self.mine, self.pending = {}, {}, []

    def add(self, items, tag):
        gs = [g for _, _, g in items]
        sums = pair_sums(gs, pair_swap(gs, f"grad_pair_swap_{tag}"), self.core, f"grad_pair_sum_{tag}")
        for (fam, li, _), pair in zip(items, sums):
            self.pairs[(fam, li)] = pair
            self.pending.append((fam, li))

    def rider(self, only=None):
        keys = [k for k in self.pending if only is None or k in only]
        self.pending = [k for k in self.pending if k not in keys]
        return (scatter_rider([self.pairs[k] for k in keys]) if keys else None), keys

    def landed(self, keys, outs):
        batch = []
        for k, o in list(zip(keys, outs)) + [(None, None)]:
            if batch and (k is None or k[0] in [b[0][0] for b in batch]):
                ks = [b[0] for b in batch]
                pairs = [self.pairs[b] for b in ks]
                intos = [self.mine[fam] if fam in self.mine else jnp.zeros((self.family_layers[fam],) + p.shape[1:], F32)
                         for (fam, _), p in zip(ks, pairs)]
                sums = chip_sums(pairs, [b[1] for b in batch], self.chip, [li for _, li in ks], intos,
                                 "grad_chip_sum_" + "_".join(f"{f}{li}" for f, li in ks))
                self.mine.update({fam: t for (fam, _), t in zip(ks, sums)})
                batch = []
            if k is not None:
                batch.append((k, o))

    def finish(self, weights, moments1, moments2):
        last, keys = self.rider()
        if last is not None:
            self.landed(keys, last.run("grad_chip_scatter_last"))
        mine = [self.mine[fam] for fam in range(len(weights))]
        other = half_swap(mine, "grad_half_swap")
        return [adamw_family(w, m, v, gm, go, self.core, f"adamw_{f}")
                for f, (w, m, v, gm, go) in enumerate(zip(weights, moments1, moments2, mine, other))]


class KeepGradients:
    def __init__(self):
        self.grads = {}

    def add(self, items, tag):
        for fam, li, g in items:
            self.grads[(fam, li)] = g

    def rider(self, only=None):
        return None, []

    def landed(self, keys, outs):
        pass


def kernel(x, norm_mix, w_qkv_even, w_o_even, w_qkvf_odd, b_forget, w_o_odd, norm_ffn, w_ffn_in, w_ffn_out, norm_final, loss_target, m_norm_mix, m_w_qkv_even, m_w_o_even, m_w_qkvf_odd, m_b_forget, m_w_o_odd, m_norm_ffn, m_w_ffn_in, m_w_ffn_out, m_norm_final, v_norm_mix, v_w_qkv_even, v_w_o_even, v_w_qkvf_odd, v_b_forget, v_w_o_odd, v_norm_ffn, v_w_ffn_in, v_w_ffn_out, v_norm_final):
    w_shards = [w_qkv_even, w_o_even, w_qkvf_odd, w_o_odd, w_ffn_in, w_ffn_out]
    shards = [w.astype(BF16) for w in w_shards]
    tables = _tables_for(x.shape[1])
    bias_pad = jnp.pad(b_forget, ((0, 0), (0, LANES - N_HEADS)))

    mine = {}
    for layer in range(DEPTH):
        fam_qkv, fam_o, li = layer_families(layer)
        mine.update({f"qkv{layer}": shards[fam_qkv][li], f"o{layer}": shards[fam_o][li],
                     f"fi{layer}": shards[4][layer], f"fo{layer}": shards[5][layer]})
    fetch = lambda names: gather_rider([mine[n] for n in names])
    have = dict(zip(("qkv0", "o0"), fetch(("qkv0", "o0")).run("gather_first")))
    saved, cur = [], x[0]
    h1 = rmsnorm_fwd(cur, norm_mix[0:1], "l0_norm_mix")
    for layer in range(DEPTH):
        next_gain = norm_mix[layer + 1:layer + 2] if layer + 1 < DEPTH else None
        cur, h1, keep = forward_layer(layer, cur, h1, have, norm_ffn[layer:layer + 1], next_gain, tables,
                                      bias_pad[layer // 2:layer // 2 + 1], fetch, *FORWARD_CARRY[layer])
        saved.append(keep)

    dcur, g_final, loss_part = loss_head(cur, norm_final.reshape(1, D_MODEL), loss_target[0], "loss_head")

    exchange = GradientExchange([w.shape[0] for w in w_shards])
    g_mix, g_ffn, g_bias = [None] * DEPTH, [None] * DEPTH, [None] * (DEPTH // 2)
    for layer in reversed(range(DEPTH)):
        dcur, g_mix[layer], g_ffn[layer], g_b = backward_layer(layer, dcur, saved[layer], norm_mix[layer:layer + 1],
                                                               norm_ffn[layer:layer + 1], tables, bias_pad[layer // 2:layer // 2 + 1], exchange)
        if g_b is not None:
            g_bias[layer // 2] = g_b

    zero_row = jnp.zeros((1, D_MODEL), F32)
    pad16 = lambda v: jnp.pad(v, (0, D_MODEL - v.shape[0]))[None, :]
    small_rows = lambda mix, ffn, fin, bias, last: jnp.concatenate(
        [r.reshape(1, D_MODEL) for r in mix] + [r.reshape(1, D_MODEL) for r in ffn] + [fin.reshape(1, D_MODEL)]
        + [pad16(b) for b in bias] + [last] + [zero_row] * (SMALL_ROWS - 12), axis=0)
    loss_row = pad16(loss_part[0, :1])
    small_g = allsum_small(small_rows(g_mix, g_ffn, g_final, g_bias, loss_row), "allsum_small")
    loss = small_g[11, 0]
    small_g = small_g.at[11].set(0.0)
    sw = small_rows(list(norm_mix), list(norm_ffn), norm_final, list(b_forget), zero_row)
    sm = small_rows(list(m_norm_mix), list(m_norm_ffn), m_norm_final, list(m_b_forget), zero_row)
    sv = small_rows(list(v_norm_mix), list(v_norm_ffn), v_norm_final, list(v_b_forget), zero_row)
    sd, snm, snv = adamw(sw, small_g, sm, sv, "adamw_small")

    def small_out(a):
        return a[0:4], a[8, :], a[9:11, :N_HEADS], a[4:8]

    big = exchange.finish(w_shards, [m_w_qkv_even, m_w_o_even, m_w_qkvf_odd, m_w_o_odd, m_w_ffn_in, m_w_ffn_out],
                          [v_w_qkv_even, v_w_o_even, v_w_qkvf_odd, v_w_o_odd, v_w_ffn_in, v_w_ffn_out])

    def outputs(small, which):
        mix, fin, bias, ffn = small_out(small)
        qkv_e, o_e, qkvf, o_o, fi, fo = [big[f][which] for f in range(6)]
        return [mix, qkv_e, o_e, qkvf, bias, o_o, ffn, fi, fo, fin]

    return (loss, dcur[None], *outputs(small_g, 0), *outputs(sd, 1), *outputs(snm, 2), *outputs(snv, 3))


def _chip_tile(rows, cols, at):
    return pl.BlockSpec((None, rows, cols), at)


def forward_layer(layer, cur, h1, have, ffn_gain, next_gain, tables, bias_row, fetch=None, carry=(), side_carry=(), ffn_carry=()):
    n = f"l{layer}"
    s = cur.shape[0]
    w_qkv, w_o = have[f"qkv{layer}"], have[f"o{layer}"]
    rider = fetch(carry) if carry else None
    side_rider = fetch(side_carry) if side_carry else None
    keep = {"x": cur, "h1": h1, "w_o": w_o.reshape(D_ATTN, D_MODEL)}
    side = []
    if layer % 2 == 0:
        qkv = matmul(h1, w_qkv, "nn", BF16, n + "_qkv", 1024, 768, 1024, mnk=(s, 3 * D_ATTN, D_MODEL),
                     b_spec=_chip_tile(D_MODEL, 768, lambda i, j, kk: (j, 0, 0)))
        o_sb, st, rode = causal_fwd(qkv, 4, "sb", n + "_sb_fwd", rider=rider)
        o_dil, lse_dil, side = dilated_fwd(qkv, tables, n + "_dil_fwd", rider=side_rider)
        attn = jnp.concatenate([o_sb, o_dil], axis=1).astype(BF16)
        keep.update(o_dil=o_dil, lse_dil=lse_dil, w_qkv=w_qkv)
    else:
        natural = jnp.transpose(w_qkv, (1, 0, 2)).reshape(D_MODEL, N_CHIPS * QKVF_COLS)
        w_gate = jnp.pad(natural[:, 3 * D_ATTN:], ((0, 0), (0, LANES - N_HEADS)))
        qkv = matmul(h1, natural[:, :3 * D_ATTN], "nn", BF16, n + "_qkv", 1024, 768, 1024)
        fl = matmul(h1, w_gate, "nn", F32, n + "_fgate", 512, LANES, 1024)
        cum = forget_fwd(fl, bias_row, n + "_forget_fwd")
        f_heads = cum[:, :N_HEADS].T
        fq = jnp.broadcast_to(f_heads[:, :, None], (N_HEADS, s, LANES))
        fk = f_heads.reshape(N_HEADS // 2, 2, s)
        attn, st, rode = causal_fwd(qkv, 8, "fox", n + "_fox_fwd", fq=fq, fk=fk, rider=rider)
        attn = attn.astype(BF16)
        keep.update(fl=fl, fq=fq, fk=fk, w_qkv=jnp.concatenate([natural[:, :3 * D_ATTN], w_gate], axis=1))
    have.update(zip(carry, rode))
    have.update(zip(side_carry, side))
    w_fi = have[f"fi{layer}"]
    mid, h2 = matmul(attn, keep["w_o"], "nn", F32, n + "_attn_out", 512, 1024, 1024, res=cur, norm_gain=ffn_gain)
    gate, up, act, rode = ffn_in_swiglu(h2, w_fi, n + "_ffn_in", rider=fetch(ffn_carry) if ffn_carry else None)
    have.update(zip(ffn_carry, rode))
    w_fo = have[f"fo{layer}"].reshape(D_FF, D_MODEL)
    if next_gain is None:
        out, h_next = matmul(act, w_fo, "nn", F32, n + "_ffn_out", 512, 1024, D_FF, res=mid), None
    else:
        out, h_next = matmul(act, w_fo, "nn", F32, n + "_ffn_out", 512, 1024, D_FF, res=mid, norm_gain=next_gain)
    keep.update(qkv=qkv, st=st, attn=attn, mid=mid, h2=h2, gate=gate, up=up, act=act, w_fi=w_fi, w_fo=w_fo)
    return out, h_next, keep


def backward_layer(layer, dcur, kp, mix_gain, ffn_gain, tables, bias_row, exchange):
    n = f"l{layer}"
    s = dcur.shape[0]
    fam_qkv, fam_o, li = layer_families(layer)
    g_fo = matmul(kp["act"], dcur, "tn", BF16, n + "_d_w_ffn_out", 1408, 1024, s)
    dact = matmul(dcur, kp["w_fo"], "nt", BF16, n + "_d_act", 1024, 1408, 1024)
    dgu = swiglu_bwd(kp["gate"], kp["up"], dact, n + "_d_swiglu")
    g_fi = matmul(kp["h2"], dgu, "tn", BF16, n + "_d_w_ffn_in", 1024, 1408, 2048, mnk=(D_MODEL, 2 * D_FF, s),
                  o_spec=_chip_tile(D_MODEL, 1408, lambda i, j, kk: (j, 0, 0)), out_shape=(N_CHIPS, D_MODEL, 1408))
    dmid, g_ffn = dh_norm_bwd(dgu, kp["w_fi"], kp["mid"], ffn_gain, dcur, n + "_d_h2", 256)
    g_o = matmul(kp["attn"], dmid, "tn", BF16, n + "_d_w_o", 1024, 1024, s)
    dattn = matmul(dmid, kp["w_o"], "nt", F32, n + "_d_attn", 1024, 1024, 1024)
    exchange.add([(5, layer, g_fo.reshape(N_CHIPS, D_FF // N_CHIPS, D_MODEL)), (4, layer, g_fi),
                  (fam_o, li, g_o.reshape(N_CHIPS, D_ATTN // N_CHIPS, D_MODEL))], f"l{layer}_ffn")
    g_bias = None
    if layer % 2 == 0:
        rider, keys = exchange.rider(only=[(4, layer), (fam_o, li)])
        dq_a, dk_a, dv_a, rode = causal_bwd(kp["qkv"], dattn, kp["st"], 4, "sb", n + "_sb_bwd", rider=rider)
        exchange.landed(keys, rode)
        rider, keys = exchange.rider()
        dq_b, dk_b, dv_b, rode = dilated_bwd(kp["qkv"], tables, dattn, kp["o_dil"], kp["lse_dil"], 4, n + "_dil_bwd", rider=rider)
        dproj = jnp.concatenate([dq_a, dq_b, dk_a.astype(BF16), dk_b, dv_a.astype(BF16), dv_b], axis=1)
        g_qkv = matmul(kp["h1"], dproj, "tn", BF16, n + "_d_w_qkv", 1024, 768, 2048, mnk=(D_MODEL, 3 * D_ATTN, s),
                       o_spec=_chip_tile(D_MODEL, 768, lambda i, j, kk: (j, 0, 0)), out_shape=(N_CHIPS, D_MODEL, 768))
    else:
        rider, keys = exchange.rider()
        dq_f, dk_f, dv_f, dfk, rode = causal_bwd(kp["qkv"], dattn, kp["st"], 8, "fox", n + "_fox_bwd", fq=kp["fq"], fk=kp["fk"],
                                                 rider=rider)
        dcum = jnp.pad(dfk.reshape(N_HEADS, s).T, ((0, 0), (0, LANES - N_HEADS)))
        dfl, dbias = forget_bwd(kp["fl"], bias_row, dcum, n + "_forget_bwd")
        g_bias = dbias[0, :N_HEADS]
        dproj = jnp.concatenate([dq_f, dk_f.astype(BF16), dv_f.astype(BF16), dfl.astype(BF16)], axis=1)
        g_nat = matmul(kp["h1"], dproj, "tn", BF16, n + "_d_w_qkv", 1024, 640, 2048)
        g_qkv = g_nat[:, :N_CHIPS * QKVF_COLS].reshape(D_MODEL, N_CHIPS, QKVF_COLS)
        g_qkv = jnp.transpose(jnp.pad(g_qkv, ((0, 0), (0, 0), (0, QKVF_PAD - QKVF_COLS))), (1, 0, 2))
    exchange.landed(keys, rode)
    exchange.add([(fam_qkv, li, g_qkv)], f"l{layer}_qkv")
    dx, g_mix = dh_norm_bwd(dproj, kp["w_qkv"], kp["x"], mix_gain, dmid, n + "_d_h1", 512)
    return dx, g_mix, g_ffn, g_bias


def local_step(xs, target, norm_mix, norm_ffn, norm_final, b_forget, layer_weights):
    tables = _tables_for(xs.shape[0])
    bias_pad = jnp.pad(b_forget, ((0, 0), (0, LANES - N_HEADS)))
    saved, cur, have = [], xs, {}
    h1 = rmsnorm_fwd(cur, norm_mix[0:1], "l0_norm_mix")
    for layer in range(DEPTH):
        have.update(zip((f"qkv{layer}", f"o{layer}", f"fi{layer}", f"fo{layer}"), layer_weights[layer]))
        next_gain = norm_mix[layer + 1:layer + 2] if layer + 1 < DEPTH else None
        cur, h1, keep = forward_layer(layer, cur, h1, have, norm_ffn[layer:layer + 1], next_gain, tables,
                                      bias_pad[layer // 2:layer // 2 + 1])
        saved.append(keep)
    dcur, g_final, loss_part = loss_head(cur, norm_final.reshape(1, D_MODEL), target, "loss_head")
    keeper = KeepGradients()
    g_mix, g_ffn, g_bias = [None] * DEPTH, [None] * DEPTH, [None] * (DEPTH // 2)
    for layer in reversed(range(DEPTH)):
        dcur, g_mix[layer], g_ffn[layer], g_b = backward_layer(layer, dcur, saved[layer], norm_mix[layer:layer + 1],
                                                               norm_ffn[layer:layer + 1], tables, bias_pad[layer // 2:layer // 2 + 1], keeper)
        if g_b is not None:
            g_bias[layer // 2] = g_b
    return dcur, keeper.grads, (g_mix, g_ffn, g_final, g_bias), loss_part
```

```python
import functools

import jax
import jax.numpy as jnp
from jax import lax
from jax.experimental import pallas as pl
from jax.experimental.pallas import tpu as pltpu

F32 = jnp.float32
BF16 = jnp.bfloat16
MESH = pl.DeviceIdType.MESH

D_MODEL = 1024
DEPTH = 4
HEAD_DIM = 64
N_HEADS = 16
D_ATTN = 1024
D_FF = 2816
ROPE_THETA = 500000.0
ROT_HALF = 8
RMS_EPS = 1e-5
DIL_STRIDES = (1, 4, 16)
ADAM_LR, ADAM_B1, ADAM_B2, ADAM_EPS, ADAM_WD, ADAM_STEP = 0.001, 0.9, 0.999, 1e-8, 0.01, 10

LANES = 128
BLK = 128
VMEM_LIMIT = 56 * 1024 * 1024
NEG = -1e30
N_CHIPS = 4
SMALL_ROWS = 16


def _params(sem=None):
    return pltpu.CompilerParams(dimension_semantics=sem, vmem_limit_bytes=VMEM_LIMIT)


def _dot(a, b):
    return lax.dot_general(a, b, (((1,), (0,)), ((), ())), preferred_element_type=F32)


def _dot_nt(a, b):
    return lax.dot_general(a, b, (((1,), (1,)), ((), ())), preferred_element_type=F32)


def _dot_tn(a, b):
    return lax.dot_general(a, b, (((0,), (0,)), ((), ())), preferred_element_type=F32)


def _split3(x):
    x1 = x.astype(BF16)
    r1 = x - x1.astype(F32)
    x2 = r1.astype(BF16)
    x3 = (r1 - x2.astype(F32)).astype(BF16)
    return x1, x2, x3


def _dot_exact_lhs(x, t):
    x1, x2, x3 = _split3(x)
    return _dot(x1, t) + _dot(x2, t) + _dot(x3, t)


def _dot_exact_rhs(t, x):
    x1, x2, x3 = _split3(x)
    return _dot(t, x1) + _dot(t, x2) + _dot(t, x3)


def _iotas(shape=(BLK, LANES)):
    return lax.broadcasted_iota(jnp.int32, shape, 0), lax.broadcasted_iota(jnp.int32, shape, 1)


_DIMS = {"nn": (((1,), (0,)), ((), ())), "nt": (((1,), (1,)), ((), ())), "tn": (((0,), (0,)), ((), ()))}


def matmul(a, b, mode, out_dtype, name, tm, tn, tk, res=None, mnk=None, b_spec=None, o_spec=None, out_shape=None, norm_gain=None):
    if mnk is not None:
        m, n, k = mnk
    elif mode == "nn":
        (m, k), (k2, n) = a.shape, b.shape
    elif mode == "nt":
        (m, k), (n, k2) = a.shape, b.shape
    else:
        (k, m), (k2, n) = a.shape, b.shape
    assert m % tm == 0 and n % tn == 0 and k % tk == 0, (name, a.shape, b.shape)
    nk = k // tk
    a_spec = pl.BlockSpec((tk, tm), lambda i, j, kk: (kk, i)) if mode == "tn" else pl.BlockSpec((tm, tk), lambda i, j, kk: (i, kk))
    if b_spec is None:
        b_spec = pl.BlockSpec((tn, tk), lambda i, j, kk: (j, kk)) if mode == "nt" else pl.BlockSpec((tk, tn), lambda i, j, kk: (kk, j))
    r_spec = pl.BlockSpec((tm, tn), lambda i, j, kk: (i, j))
    if o_spec is None:
        o_spec = r_spec
    dims = _DIMS[mode]
    has_res, has_norm = res is not None, norm_gain is not None
    assert not has_norm or (tn == n and nk == 1)
    n_in = 2 + int(has_res) + int(has_norm)

    def body(*refs):
        a_ref, b_ref = refs[0], refs[1]
        r_ref = refs[2] if has_res else None
        o_ref = refs[n_in]

        def finish(v):
            if has_res:
                v = v + r_ref[...]
            o_ref[...] = v.astype(out_dtype)
            if has_norm:
                rstd = lax.rsqrt(jnp.mean(v * v, axis=-1, keepdims=True) + RMS_EPS)
                refs[n_in + 1][...] = (v * rstd * refs[n_in - 1][...]).astype(BF16)

        bv = b_ref[...]
        if bv.ndim == 3:
            bv = jnp.concatenate([bv[j] for j in range(bv.shape[0])], axis=1)
        p = lax.dot_general(a_ref[...].astype(BF16), bv.astype(BF16), dims, preferred_element_type=F32)
        if nk == 1:
            finish(p)
        else:
            acc = refs[-1]
            kk = pl.program_id(2)

            @pl.when(kk == 0)
            def _():
                acc[...] = p

            @pl.when(kk > 0)
            def _():
                acc[...] += p

            @pl.when(kk == nk - 1)
            def _():
                finish(acc[...])

    ops = [a, b] + ([res] if has_res else []) + ([norm_gain] if has_norm else [])
    specs = [a_spec, b_spec] + ([r_spec] if has_res else []) + ([pl.BlockSpec((1, tn), lambda i, j, kk: (0, j))] if has_norm else [])
    out_shape = jax.ShapeDtypeStruct((m, n) if out_shape is None else out_shape, out_dtype)
    return pl.pallas_call(
        body, name=name, out_shape=(out_shape, jax.ShapeDtypeStruct((m, n), BF16)) if has_norm else out_shape,
        grid=(m // tm, n // tn, nk), in_specs=specs, out_specs=(o_spec, r_spec) if has_norm else o_spec,
        scratch_shapes=[pltpu.VMEM((tm, tn), F32)] if nk > 1 else [],
        compiler_params=_params(("parallel", "parallel", "arbitrary")),
    )(*ops)


ROWS = 256


def _row_spec(cols, rows=ROWS):
    return pl.BlockSpec((rows, cols), lambda i: (i, 0))


def _fix_spec(r, cols):
    return pl.BlockSpec((r, cols), lambda i: (0, 0))


def rmsnorm_fwd(x, g, name):
    s, d = x.shape

    def body(x_ref, g_ref, h_ref):
        xv = x_ref[...]
        rstd = lax.rsqrt(jnp.mean(xv * xv, axis=-1, keepdims=True) + RMS_EPS)
        h_ref[...] = (xv * rstd * g_ref[...]).astype(BF16)

    return pl.pallas_call(
        body, name=name, out_shape=jax.ShapeDtypeStruct((s, d), BF16), grid=(s // ROWS,),
        in_specs=[_row_spec(d), _fix_spec(1, d)], out_specs=_row_spec(d), compiler_params=_params(("parallel",)),
    )(x, g)


def _rms_bwd_math(xv, gv, dh):
    rstd = lax.rsqrt(jnp.mean(xv * xv, axis=-1, keepdims=True) + RMS_EPS)
    xhat = xv * rstd
    u = dh * gv
    dx = rstd * (u - xhat * jnp.mean(u * xhat, axis=-1, keepdims=True))
    return dx, dh * xhat


def dh_norm_bwd(dy, w, x, g, dres, name, tm):
    s, k = dy.shape
    d = x.shape[1]

    def body(dy_ref, w_ref, x_ref, g_ref, dres_ref, dx_ref, dg_ref):
        wv = w_ref[...]
        if wv.ndim == 3:
            wv = jnp.concatenate([wv[j] for j in range(wv.shape[0])], axis=1)
        dx, dgt = _rms_bwd_math(x_ref[...], g_ref[...], _dot_nt(dy_ref[...], wv))
        dx_ref[...] = dres_ref[...] + dx
        part = jnp.sum(dgt, axis=0, keepdims=True)

        @pl.when(pl.program_id(0) == 0)
        def _():
            dg_ref[...] = part

        @pl.when(pl.program_id(0) > 0)
        def _():
            dg_ref[...] += part

    w_spec = pl.BlockSpec(w.shape, lambda i: (0,) * w.ndim)
    return pl.pallas_call(
        body, name=name, out_shape=(jax.ShapeDtypeStruct((s, d), F32), jax.ShapeDtypeStruct((1, d), F32)), grid=(s // tm,),
        in_specs=[pl.BlockSpec((tm, k), lambda i: (i, 0)), w_spec, _row_spec(d, tm), _fix_spec(1, d), _row_spec(d, tm)],
        out_specs=(_row_spec(d, tm), _fix_spec(1, d)), compiler_params=_params(("arbitrary",)),
    )(dy, w, x, g, dres)


def loss_head(x, g, target, name):
    s, d = x.shape

    def body(x_ref, g_ref, t_ref, dx_ref, dg_ref, loss_ref):
        xv, gv = x_ref[...], g_ref[...]
        rstd = lax.rsqrt(jnp.mean(xv * xv, axis=-1, keepdims=True) + RMS_EPS)
        err = xv * rstd * gv - t_ref[...]
        dx, dgt = _rms_bwd_math(xv, gv, err * (1.0 / d))
        dx_ref[...] = dx
        part = jnp.sum(dgt, axis=0, keepdims=True)
        lpart = jnp.full((1, LANES), 0.5 / d, F32) * jnp.sum(err * err)

        @pl.when(pl.program_id(0) == 0)
        def _():
            dg_ref[...] = part
            loss_ref[...] = lpart

        @pl.when(pl.program_id(0) > 0)
        def _():
            dg_ref[...] += part
            loss_ref[...] += lpart

    return pl.pallas_call(
        body, name=name,
        out_shape=(jax.ShapeDtypeStruct((s, d), F32), jax.ShapeDtypeStruct((1, d), F32), jax.ShapeDtypeStruct((1, LANES), F32)),
        grid=(s // ROWS,), in_specs=[_row_spec(d), _fix_spec(1, d), _row_spec(d)],
        out_specs=(_row_spec(d), _fix_spec(1, d), _fix_spec(1, LANES)), compiler_params=_params(("arbitrary",)),
    )(x, g, target)


def ffn_in_swiglu(h, w_in, name, tm=1024, rider=None):
    s, d = h.shape
    cols = w_in.shape[2]

    def body(h_ref, wg_ref, wu_ref, g_ref, u_ref, a_ref):
        hv = h_ref[...]
        gv, uv = _dot(hv, wg_ref[...]), _dot(hv, wu_ref[...])
        g_ref[...] = gv.astype(BF16)
        u_ref[...] = uv.astype(BF16)
        a_ref[...] = (gv * (1.0 / (1.0 + jnp.exp(-gv))) * uv).astype(BF16)

    tile = pl.BlockSpec((tm, cols), lambda i, j: (i, j))
    out = jax.ShapeDtypeStruct((s, 2 * cols), BF16)
    in_specs = [pl.BlockSpec((tm, d), lambda i, j: (i, 0)), pl.BlockSpec((None, d, cols), lambda i, j: (j, 0, 0)),
                pl.BlockSpec((None, d, cols), lambda i, j: (j + 2, 0, 0))]
    outs, rode = call_with_rider(body, name, rider, [h, w_in, w_in], in_specs, [out, out, out], [tile, tile, tile], [], (s // tm, 2))
    return (*outs, rode)


def swiglu_bwd(dy, w_out, gate, up, name):
    s, f = gate.shape
    d = dy.shape[1]

    def body(dy_ref, w_ref, g_ref, u_ref, o_ref):
        da = _dot_nt(dy_ref[...].astype(BF16), w_ref[...])
        gv, uv = g_ref[...].astype(F32), u_ref[...].astype(F32)
        sg = 1.0 / (1.0 + jnp.exp(-gv))
        o_ref[:, :f] = (da * uv * sg * (1.0 + gv * (1.0 - sg))).astype(BF16)
        o_ref[:, f:] = (da * gv * sg).astype(BF16)

    return pl.pallas_call(
        body, name=name, out_shape=jax.ShapeDtypeStruct((s, 2 * f), BF16), grid=(s // ROWS,),
        in_specs=[_row_spec(d), _fix_spec(f, d), _row_spec(f), _row_spec(f)], out_specs=_row_spec(2 * f),
        compiler_params=_params(("parallel",)),
    )(dy, w_out, gate, up)


Q_OFF, K_OFF, V_OFF = 0, 8, 16


KB = 512
BQ = 512
SUB = KB // BLK


def _softplus_parts(z):
    sp = jnp.log(1.0 + jnp.exp(-jnp.abs(z)))
    ls = jnp.minimum(z, 0.0) - sp
    return ls, ls - z


def _wide(t):
    return jnp.concatenate([t] * SUB, axis=1)


def _chunk_dots(x, tri):
    terms = []
    for u in range(SUB):
        xu = x[:, u * BLK:(u + 1) * BLK]
        hi = xu.astype(BF16)
        terms += [hi, (xu - hi.astype(F32)).astype(BF16)]
    r = _dot(jnp.concatenate(terms, axis=0), tri)
    rows = x.shape[0]
    piece = lambda n: r[n * rows:(n + 1) * rows]
    return [piece(2 * u) + piece(2 * u + 1) for u in range(SUB)]


def _block_suffix_sums(x, suffix, c):
    loc = _chunk_dots(x, suffix)
    out = [None] * SUB
    for u in reversed(range(SUB)):
        out[u] = loc[u] + c
        c = c + jnp.sum(x[:, u * BLK:(u + 1) * BLK], axis=1, keepdims=True)
    return jnp.concatenate(out, axis=1), c


def _block_prefix_sums(x, tri, c):
    loc = _chunk_dots(x, tri)
    out = []
    for u in range(SUB):
        out.append(loc[u] + c)
        c = c + jnp.sum(x[:, u * BLK:(u + 1) * BLK], axis=1, keepdims=True)
    return jnp.concatenate(out, axis=1), c


def causal_fwd(qkv, npairs, mode, name, fq=None, fk=None, rider=None):
    s = qkv.shape[0]
    nq = s // BQ
    fox = mode == "fox"

    def body(*refs):
        if fox:
            q_ref, k_ref, v_ref, fq_ref, fk_ref, o_ref, st_ref = refs
        else:
            q_ref, k_ref, v_ref, o_ref, st_ref = refs
        i = pl.program_id(1)
        nkb = (i * BQ + BQ - 1) // KB + 1
        row, lane = _iotas((BQ, KB))
        row_s, lane_s = _iotas()
        _, lane_q = _iotas((BQ, LANES))
        nfull = (i * BQ) // KB
        qpos = i * BQ + row
        qf = q_ref[...].astype(F32) * 0.125
        hms = (lane_q < HEAD_DIM, lane_q >= HEAD_DIM)
        qas = [jnp.where(hm, qf, 0.0).astype(BF16) for hm in hms]
        suffix = jnp.where(row_s > lane_s, 1.0, 0.0).astype(BF16)
        zero = jnp.zeros((BQ, LANES), F32)
        col0 = jnp.zeros((BQ, 1), F32)

        def kv(j):
            r0 = pl.multiple_of(j * KB, KB)
            return r0, k_ref[pl.ds(r0, KB), :], v_ref[pl.ds(r0, KB), :]

        if fox:
            fqs = [_wide(fq_ref[a]) for a in range(2)]

            def step(j, carry, masked):
                r0, kb, vb = kv(j)
                new = []
                for a in range(2):
                    acc, mx, l = carry[3 * a:3 * a + 3]
                    z = _dot_nt(qas[a], kb) + fqs[a] - fk_ref[a:a + 1, pl.ds(r0, KB)]
                    if masked:
                        z = jnp.where(r0 + lane <= qpos, z, NEG)
                    mnew = jnp.maximum(mx, jnp.max(z, axis=1, keepdims=True))
                    p = jnp.exp(z - mnew)
                    alpha = jnp.exp(mx - mnew)
                    new += [alpha * acc + _dot(p.astype(BF16), vb), mnew, alpha * l + jnp.sum(p, axis=1, keepdims=True)]
                return tuple(new)

            neg = jnp.full((BQ, 1), NEG, F32)
            res = lax.fori_loop(0, nfull, functools.partial(step, masked=False), (zero, neg, col0, zero, neg, col0))
            res = lax.fori_loop(nfull, nkb, functools.partial(step, masked=True), res)
            outs = [res[3 * a] / res[3 * a + 2] for a in range(2)]
            stats = [res[3 * a + 1] + jnp.log(res[3 * a + 2]) for a in range(2)]
        else:
            def step(j, carry, masked):
                r0, kb, vb = kv(j)
                strict = r0 + lane < qpos
                new = []
                for a in range(2):
                    acc, c = carry[2 * a:2 * a + 2]
                    ls, lm = _softplus_parts(_dot_nt(qas[a], kb))
                    if masked:
                        lm = jnp.where(strict, lm, 0.0)
                    between, c = _block_suffix_sums(lm, suffix, c)
                    aw = jnp.exp(ls + between)
                    if masked:
                        aw = jnp.where(strict, aw, 0.0)
                    new += [acc + _dot(aw.astype(BF16), vb), c]
                return tuple(new)

            res = lax.fori_loop(0, nkb - nfull, lambda jj, c: step(nkb - 1 - jj, c, True), (zero, col0, zero, col0))
            res = lax.fori_loop(0, nfull, lambda jj, c: step(nfull - 1 - jj, c, False), res)
            outs, stats = [res[0], res[2]], [res[1], res[3]]
        o_ref[...] = jnp.where(hms[0], outs[0], outs[1])
        for a in range(2):
            st_ref[a] = jnp.broadcast_to(stats[a], (BQ, LANES))

    col = lambda off: (lambda p, i: (0, off + p))
    in_specs = [pl.BlockSpec((BQ, LANES), lambda p, i: (i, Q_OFF + p)),
                pl.BlockSpec((s, LANES), col(K_OFF)), pl.BlockSpec((s, LANES), col(V_OFF))]
    ops = [qkv, qkv, qkv]
    if fox:
        in_specs += [pl.BlockSpec((2, BQ, LANES), lambda p, i: (p, i, 0)), pl.BlockSpec((None, 2, s), lambda p, i: (p, 0, 0))]
        ops += [fq, fk]
    (o, stat), rode = call_with_rider(
        body, name, rider, ops, in_specs,
        [jax.ShapeDtypeStruct((s, npairs * LANES), F32), jax.ShapeDtypeStruct((2 * npairs, s, LANES), F32)],
        [pl.BlockSpec((BQ, LANES), lambda p, i: (i, p)), pl.BlockSpec((2, BQ, LANES), lambda p, i: (p, i, 0))], [], (npairs, nq))
    return o, stat, rode


def causal_bwd(qkv, do, stat, npairs, mode, name, fq=None, fk=None, rider=None):
    s = qkv.shape[0]
    nq = s // BQ
    fox = mode == "fox"

    def body(*refs):
        if fox:
            q_ref, k_ref, v_ref, do_ref, st_ref, fq_ref, fk_ref, dq_ref, dk_ref, dv_ref, df_ref, p_s, dp_s = refs
        else:
            q_ref, k_ref, v_ref, do_ref, st_ref, dq_ref, dk_ref, dv_ref = refs
        i = pl.program_id(1)

        @pl.when(i == 0)
        def _():
            dk_ref[...] = jnp.zeros_like(dk_ref)
            dv_ref[...] = jnp.zeros_like(dv_ref)
            if fox:
                df_ref[...] = jnp.zeros_like(df_ref)

        nkb = (i * BQ + BQ - 1) // KB + 1
        nfull = (i * BQ) // KB
        row, lane = _iotas((BQ, KB))
        row_s, lane_s = _iotas()
        _, lane_q = _iotas((BQ, LANES))
        qpos = i * BQ + row
        qf = q_ref[...].astype(F32) * 0.125
        dov = do_ref[...]
        hms = (lane_q < HEAD_DIM, lane_q >= HEAD_DIM)
        qas = [jnp.where(hm, qf, 0.0).astype(BF16) for hm in hms]
        doas = [jnp.where(hm, dov, 0.0).astype(BF16) for hm in hms]
        stas = [_wide(st_ref[a]) for a in range(2)]
        zero = jnp.zeros((BQ, LANES), F32)
        col0 = jnp.zeros((BQ, 1), F32)

        def kv(j):
            r0 = pl.multiple_of(j * KB, KB)
            return r0, k_ref[pl.ds(r0, KB), :], v_ref[pl.ds(r0, KB), :]

        if fox:
            fqs = [_wide(fq_ref[a]) for a in range(2)]

            def probs(j, deltas, masked):
                r0, kb, vb = kv(j)
                new = []
                for a in range(2):
                    z = _dot_nt(qas[a], kb) + fqs[a] - fk_ref[a:a + 1, pl.ds(r0, KB)]
                    p = jnp.exp(z - stas[a])
                    if masked:
                        p = jnp.where(r0 + lane <= qpos, p, 0.0)
                    dp = _dot_nt(doas[a], vb)
                    p_s[a, j] = p
                    dp_s[a, j] = dp
                    new.append(deltas[a] + jnp.sum(p * dp, axis=1, keepdims=True))
                return tuple(new)

            deltas = lax.fori_loop(0, nfull, functools.partial(probs, masked=False), (col0, col0))
            deltas = lax.fori_loop(nfull, nkb, functools.partial(probs, masked=True), deltas)

            def step(j, dqs):
                r0, kb, _ = kv(j)
                new = []
                dk = jnp.zeros((KB, LANES), F32)
                dv = jnp.zeros((KB, LANES), F32)
                for a in range(2):
                    p = p_s[a, j]
                    ds = p * (dp_s[a, j] - deltas[a])
                    dsb = ds.astype(BF16)
                    dk += _dot_tn(dsb, qas[a])
                    dv += _dot_tn(p.astype(BF16), doas[a])
                    df_ref[a:a + 1, pl.ds(r0, KB)] -= jnp.sum(ds, axis=0, keepdims=True)
                    new.append(dqs[a] + _dot(dsb, kb))
                dk_ref[pl.ds(r0, KB), :] += dk
                dv_ref[pl.ds(r0, KB), :] += dv
                return tuple(new)

            dqs = lax.fori_loop(0, nkb, step, (zero, zero))
        else:
            incl = jnp.where(row_s <= lane_s, 1.0, 0.0).astype(BF16)
            excl = jnp.where(row_s < lane_s, 1.0, 0.0).astype(BF16)

            def step(j, carry, masked):
                r0, kb, vb = kv(j)
                strict = r0 + lane < qpos
                new = []
                dk = jnp.zeros((KB, LANES), F32)
                dv = jnp.zeros((KB, LANES), F32)
                for a in range(2):
                    dq, cm, cg = carry[3 * a:3 * a + 3]
                    ls, lm = _softplus_parts(_dot_nt(qas[a], kb))
                    if masked:
                        lm = jnp.where(strict, lm, 0.0)
                    beta = jnp.exp(ls)
                    upto, cm = _block_prefix_sums(lm, incl, cm)
                    aw = jnp.exp(ls + stas[a] - upto)
                    if masked:
                        aw = jnp.where(strict, aw, 0.0)
                    g = aw * _dot_nt(doas[a], vb)
                    pre, cg = _block_prefix_sums(g, excl, cg)
                    dz = g * (1.0 - beta) - pre * beta
                    if masked:
                        dz = jnp.where(strict, dz, 0.0)
                    dzb = dz.astype(BF16)
                    dk += _dot_tn(dzb, qas[a])
                    dv += _dot_tn(aw.astype(BF16), doas[a])
                    new += [dq + _dot(dzb, kb), cm, cg]
                dk_ref[pl.ds(r0, KB), :] += dk
                dv_ref[pl.ds(r0, KB), :] += dv
                return tuple(new)

            res = lax.fori_loop(0, nfull, functools.partial(step, masked=False), (zero, col0, col0, zero, col0, col0))
            res = lax.fori_loop(nfull, nkb, functools.partial(step, masked=True), res)
            dqs = (res[0], res[3])
        dq_ref[...] = (jnp.where(hms[0], dqs[0], dqs[1]) * 0.125).astype(BF16)

    col = lambda off: (lambda p, i: (0, off + p))
    blk = pl.BlockSpec((BQ, LANES), lambda p, i: (i, p))
    acc = pl.BlockSpec((s, LANES), lambda p, i: (0, p))
    st_spec = pl.BlockSpec((2, BQ, LANES), lambda p, i: (p, i, 0))
    in_specs = [pl.BlockSpec((BQ, LANES), lambda p, i: (i, Q_OFF + p)), pl.BlockSpec((s, LANES), col(K_OFF)),
                pl.BlockSpec((s, LANES), col(V_OFF)), blk, st_spec]
    ops = [qkv, qkv, qkv, do, stat]
    w = npairs * LANES
    out_shape = [jax.ShapeDtypeStruct((s, w), BF16), jax.ShapeDtypeStruct((s, w), F32), jax.ShapeDtypeStruct((s, w), F32)]
    out_specs = [blk, acc, acc]
    scratch = []
    if fox:
        fk_spec = pl.BlockSpec((None, 2, s), lambda p, i: (p, 0, 0))
        in_specs += [st_spec, fk_spec]
        ops += [fq, fk]
        out_shape.append(jax.ShapeDtypeStruct((npairs, 2, s), F32))
        out_specs.append(fk_spec)
        scratch = [pltpu.VMEM((2, s // KB, BQ, KB), F32)] * 2
    outs, rode = call_with_rider(body, name, rider, ops, in_specs, out_shape, out_specs, scratch, (npairs, nq))
    return (*outs, rode)


def forget_fwd(fl, bias, name):
    s = fl.shape[0]

    def body(fl_ref, b_ref, f_ref):
        row, lane = _iotas()
        lower = jnp.where(lane <= row, 1.0, 0.0).astype(BF16)

        def step(n, carry):
            r0 = pl.multiple_of(n * BLK, BLK)
            ls, _ = _softplus_parts(fl_ref[pl.ds(r0, BLK), :] + b_ref[...])
            blk = _dot_exact_rhs(lower, ls) + carry
            f_ref[pl.ds(r0, BLK), :] = blk
            return blk[BLK - 1:BLK, :]

        lax.fori_loop(0, s // BLK, step, jnp.zeros((1, LANES), F32))

    return pl.pallas_call(
        body, name=name, out_shape=jax.ShapeDtypeStruct((s, LANES), F32),
        in_specs=[pl.BlockSpec(memory_space=pltpu.VMEM)] * 2, out_specs=pl.BlockSpec(memory_space=pltpu.VMEM),
        compiler_params=_params(),
    )(fl, bias)


def forget_bwd(fl, bias, df, name):
    s = fl.shape[0]
    nb = s // BLK

    def body(fl_ref, b_ref, df_ref, o_ref, db_ref):
        row, lane = _iotas()
        upper = jnp.where(lane >= row, 1.0, 0.0).astype(BF16)

        def step(nn, carry):
            tail, db = carry
            r0 = pl.multiple_of((nb - 1 - nn) * BLK, BLK)
            dls = _dot_exact_rhs(upper, df_ref[pl.ds(r0, BLK), :]) + tail
            xv = fl_ref[pl.ds(r0, BLK), :] + b_ref[...]
            dfl = dls * (1.0 / (1.0 + jnp.exp(xv)))
            o_ref[pl.ds(r0, BLK), :] = dfl
            return dls[0:1, :], db + jnp.sum(dfl, axis=0, keepdims=True)

        _, db = lax.fori_loop(0, nb, step, (jnp.zeros((1, LANES), F32), jnp.zeros((1, LANES), F32)))
        db_ref[...] = db

    return pl.pallas_call(
        body, name=name, out_shape=(jax.ShapeDtypeStruct((s, LANES), F32), jax.ShapeDtypeStruct((1, LANES), F32)),
        in_specs=[pl.BlockSpec(memory_space=pltpu.VMEM)] * 3,
        out_specs=(pl.BlockSpec(memory_space=pltpu.VMEM), pl.BlockSpec(memory_space=pltpu.VMEM)),
        compiler_params=_params(),
    )(fl, bias, df)


def _rot_tables(s):
    inv = ROPE_THETA ** (-jnp.arange(ROT_HALF, dtype=F32) * 2.0 / (2 * ROT_HALF))
    ang = jnp.arange(s, dtype=F32)[:, None] * inv[None, :]
    cos, sin = jnp.cos(ang), jnp.sin(ang)
    z8 = jnp.zeros((s, ROT_HALF), F32)
    rest = HEAD_DIM - 2 * ROT_HALF
    zr, onr = jnp.zeros((s, rest), F32), jnp.ones((s, rest), F32)
    tile = lambda t: jnp.tile(t, (1, 2))
    return tile(jnp.concatenate([cos, cos, onr], 1)), tile(jnp.concatenate([-sin, z8, zr], 1)), tile(jnp.concatenate([z8, sin, zr], 1))


def _deinterleave(dst, src_ref, stride, s, dtype):
    length = s // stride
    for r in range(stride):
        if stride == 1:
            dst[...] = src_ref[...].astype(dtype)
        else:
            dst[r * length:(r + 1) * length, :] = src_ref[pl.ds(r, length, stride=stride), :].astype(dtype)


def _band_masks(row, lane, first):
    return lane <= row, lane >= row + jnp.where(first, BLK, 0)


N_DIL_PAIRS = 4


def _rotate_into(q_ref, k_ref, v_ref, c_ref, s1_ref, s2_ref, qr, kr, vr):
    c, s1, s2 = c_ref[...], s1_ref[...], s2_ref[...]
    rot = lambda xv: xv * c + pltpu.roll(xv, LANES - ROT_HALF, 1) * s1 + pltpu.roll(xv, ROT_HALF, 1) * s2
    qr[...] = rot(q_ref[...].astype(F32)) * 0.125
    kr[...] = rot(k_ref[...].astype(F32))
    vr[...] = v_ref[...].astype(F32)


def _dilated_operands(qkv, tables):
    s = qkv.shape[0]
    col = lambda off: pl.BlockSpec((s, LANES), lambda p: (0, off + N_DIL_PAIRS + p))
    table = pl.BlockSpec((s, LANES), lambda p: (0, 0))
    return [qkv, qkv, qkv, *tables], [col(Q_OFF), col(K_OFF), col(V_OFF), table, table, table]


def dilated_fwd(qkv, tables, name, rider=None):
    s = qkv.shape[0]
    npairs, w = N_DIL_PAIRS, N_DIL_PAIRS * LANES
    nblk = s // BLK

    def body(q_in, k_in, v_in, c_ref, s1_ref, s2_ref, o_ref, lse_ref, q_ref, k_ref, v_ref, qs, ks, vs, od, ld, on, ln):
        row, lane = _iotas()
        _rotate_into(q_in, k_in, v_in, c_ref, s1_ref, s2_ref, q_ref, k_ref, v_ref)
        for pi, stride in enumerate(DIL_STRIDES):
            per = (s // stride) // BLK
            _deinterleave(qs, q_ref, stride, s, BF16)
            _deinterleave(ks, k_ref, stride, s, BF16)
            _deinterleave(vs, v_ref, stride, s, BF16)

            def block(b, carry):
                r0 = pl.multiple_of(b * BLK, BLK)
                rp = pl.multiple_of(jnp.maximum(b - 1, 0) * BLK, BLK)
                mc, mp = _band_masks(row, lane, b % per == 0)
                q = qs[pl.ds(r0, BLK), :]
                kc, kp, vc, vp = ks[pl.ds(r0, BLK), :], ks[pl.ds(rp, BLK), :], vs[pl.ds(r0, BLK), :], vs[pl.ds(rp, BLK), :]
                out = jnp.zeros((BLK, LANES), F32)
                lse = jnp.zeros((BLK, LANES), F32)
                for a in range(2):
                    hm = (lane < HEAD_DIM) if a == 0 else (lane >= HEAD_DIM)
                    qa = jnp.where(hm, q.astype(F32), 0.0).astype(BF16)
                    sc = jnp.where(mc, _dot_nt(qa, kc), NEG)
                    sp = jnp.where(mp, _dot_nt(qa, kp), NEG)
                    mx = jnp.maximum(jnp.max(sc, axis=1, keepdims=True), jnp.max(sp, axis=1, keepdims=True))
                    pc, pp = jnp.exp(sc - mx), jnp.exp(sp - mx)
                    l = jnp.sum(pc, axis=1, keepdims=True) + jnp.sum(pp, axis=1, keepdims=True)
                    oa = (_dot(pc.astype(BF16), vc) + _dot(pp.astype(BF16), vp)) / l
                    out = jnp.where(hm, oa, out)
                    lse = jnp.where(hm, mx + jnp.log(l), lse)
                od[pl.ds(r0, BLK), :] = out
                ld[pl.ds(r0, BLK), :] = lse
                return carry

            lax.fori_loop(0, nblk, block, 0, unroll=2)
            length = s // stride
            for r in range(stride):
                if stride == 1:
                    on[pi] = od[...]
                    ln[pi] = ld[...]
                else:
                    on[pi, pl.ds(r, length, stride=stride), :] = od[r * length:(r + 1) * length, :]
                    ln[pi, pl.ds(r, length, stride=stride), :] = ld[r * length:(r + 1) * length, :]

        def merge(n, carry):
            r0 = pl.multiple_of(n * BLK, BLK)
            ls = [ln[pi, pl.ds(r0, BLK), :] for pi in range(3)]
            mx = jnp.maximum(jnp.maximum(ls[0], ls[1]), ls[2])
            ws = [jnp.exp(lv - mx) for lv in ls]
            den = ws[0] + ws[1] + ws[2]
            num = ws[0] * on[0, pl.ds(r0, BLK), :] + ws[1] * on[1, pl.ds(r0, BLK), :] + ws[2] * on[2, pl.ds(r0, BLK), :]
            o_ref[pl.ds(r0, BLK), :] = num / den
            lse_ref[pl.ds(r0, BLK), :] = mx + jnp.log(den)
            return carry

        lax.fori_loop(0, nblk, merge, 0, unroll=2)

    colspec = pl.BlockSpec((s, LANES), lambda p: (0, p))
    out = jax.ShapeDtypeStruct((s, w), F32)
    scratch = ([pltpu.VMEM((s, LANES), F32)] * 3 + [pltpu.VMEM((s, LANES), BF16)] * 3 + [pltpu.VMEM((s, LANES), F32)] * 2
               + [pltpu.VMEM((3, s, LANES), F32)] * 2)
    ops, in_specs = _dilated_operands(qkv, tables)
    (o, lse), rode = call_with_rider(body, name, rider, ops, in_specs, [out, out], [colspec, colspec], scratch, (npairs,))
    return o, lse, rode


def dilated_bwd(qkv, tables, do, out, lse, do_off, name, rider=None):
    s = qkv.shape[0]
    npairs, w = N_DIL_PAIRS, N_DIL_PAIRS * LANES
    nblk = s // BLK

    def body(q_in, k_in, v_in, c_ref, s1_ref, s2_ref, do_ref, out_ref, lse_ref, dq_out, dk_out, dv_out,
             q_ref, k_ref, v_ref, dq_ref, dk_ref, dv_ref, qs, ks, vs, dos, dls, lss, dqd, dkd, dvd, dln):
        row, lane = _iotas()
        same_head = jnp.where((row < HEAD_DIM) == (lane < HEAD_DIM), 1.0, 0.0).astype(BF16)
        _rotate_into(q_in, k_in, v_in, c_ref, s1_ref, s2_ref, q_ref, k_ref, v_ref)

        def delta_blk(n, carry):
            r0 = pl.multiple_of(n * BLK, BLK)
            dln[pl.ds(r0, BLK), :] = _dot_exact_lhs(do_ref[pl.ds(r0, BLK), :] * out_ref[pl.ds(r0, BLK), :], same_head)
            return carry

        lax.fori_loop(0, nblk, delta_blk, 0, unroll=2)
        for pi, stride in enumerate(DIL_STRIDES):
            per = (s // stride) // BLK
            _deinterleave(qs, q_ref, stride, s, BF16)
            _deinterleave(ks, k_ref, stride, s, BF16)
            _deinterleave(vs, v_ref, stride, s, BF16)
            _deinterleave(dos, do_ref, stride, s, BF16)
            _deinterleave(dls, dln, stride, s, F32)
            _deinterleave(lss, lse_ref, stride, s, F32)

            def block(b, carry):
                r0 = pl.multiple_of(b * BLK, BLK)
                rp = pl.multiple_of(jnp.maximum(b - 1, 0) * BLK, BLK)
                first = b % per == 0
                mc, mp = _band_masks(row, lane, first)
                q, dov = qs[pl.ds(r0, BLK), :], dos[pl.ds(r0, BLK), :]
                kc, kp, vc, vp = ks[pl.ds(r0, BLK), :], ks[pl.ds(rp, BLK), :], vs[pl.ds(r0, BLK), :], vs[pl.ds(rp, BLK), :]
                lse_t, dl_t = lss[pl.ds(r0, BLK), :], dls[pl.ds(r0, BLK), :]
                dq = jnp.zeros((BLK, LANES), F32)
                dkc = jnp.zeros((BLK, LANES), F32)
                dkp = jnp.zeros((BLK, LANES), F32)
                dvc = jnp.zeros((BLK, LANES), F32)
                dvp = jnp.zeros((BLK, LANES), F32)
                for a in range(2):
                    hm = (lane < HEAD_DIM) if a == 0 else (lane >= HEAD_DIM)
                    pick = lane == a * HEAD_DIM
                    qa = jnp.where(hm, q.astype(F32), 0.0).astype(BF16)
                    doa = jnp.where(hm, dov.astype(F32), 0.0).astype(BF16)
                    lse_a = jnp.sum(jnp.where(pick, lse_t, 0.0), axis=1, keepdims=True)
                    dl_a = jnp.sum(jnp.where(pick, dl_t, 0.0), axis=1, keepdims=True)
                    pc = jnp.where(mc, jnp.exp(_dot_nt(qa, kc) - lse_a), 0.0)
                    pp = jnp.where(mp, jnp.exp(_dot_nt(qa, kp) - lse_a), 0.0)
                    dsc = (pc * (_dot_nt(doa, vc) - dl_a)).astype(BF16)
                    dsp = (pp * (_dot_nt(doa, vp) - dl_a)).astype(BF16)
                    dq = jnp.where(hm, _dot(dsc, kc) + _dot(dsp, kp), dq)
                    dkc += _dot_tn(dsc, qa)
                    dkp += _dot_tn(dsp, qa)
                    dvc += _dot_tn(pc.astype(BF16), doa)
                    dvp += _dot_tn(pp.astype(BF16), doa)
                dqd[pl.ds(r0, BLK), :] = dq
                dkd[pl.ds(r0, BLK), :] = dkc
                dvd[pl.ds(r0, BLK), :] = dvc

                @pl.when(jnp.logical_not(first))
                def _():
                    dkd[pl.ds(rp, BLK), :] += dkp
                    dvd[pl.ds(rp, BLK), :] += dvp

                return carry

            lax.fori_loop(0, nblk, block, 0, unroll=2)
            length = s // stride
            for dst, src in ((dq_ref, dqd), (dk_ref, dkd), (dv_ref, dvd)):
                for r in range(stride):
                    if stride == 1:
                        dst[...] = src[...]
                    else:
                        dst[pl.ds(r, length, stride=stride), :] += src[r * length:(r + 1) * length, :]

        c, s1, s2 = c_ref[...], s1_ref[...], s2_ref[...]
        rot_t = lambda dy: dy * c + pltpu.roll(dy * s1, ROT_HALF, 1) + pltpu.roll(dy * s2, LANES - ROT_HALF, 1)
        dq_out[...] = (rot_t(dq_ref[...]) * 0.125).astype(BF16)
        dk_out[...] = rot_t(dk_ref[...]).astype(BF16)
        dv_out[...] = dv_ref[...].astype(BF16)

    colspec = pl.BlockSpec((s, LANES), lambda p: (0, p))
    do_spec = pl.BlockSpec((s, LANES), lambda p: (0, do_off + p))
    o3 = jax.ShapeDtypeStruct((s, w), BF16)
    scratch = [pltpu.VMEM((s, LANES), F32)] * 6 + [pltpu.VMEM((s, LANES), BF16)] * 4 + [pltpu.VMEM((s, LANES), F32)] * 6
    ops, in_specs = _dilated_operands(qkv, tables)
    outs, rode = call_with_rider(body, name, rider, ops + [do, out, lse], in_specs + [do_spec, colspec, colspec],
                                 [o3, o3, o3], [colspec, colspec, colspec], scratch, (npairs,))
    return (*outs, rode)


def adamw(w, g, m, v, name):
    rows, cols = w.shape
    rb = min(rows, ROWS)
    c1 = 1.0 - ADAM_B1 ** ADAM_STEP
    c2 = 1.0 - ADAM_B2 ** ADAM_STEP

    def body(w_ref, g_ref, m_ref, v_ref, d_ref, mo_ref, vo_ref):
        gv = g_ref[...]
        mn = ADAM_B1 * m_ref[...] + (1.0 - ADAM_B1) * gv
        vn = ADAM_B2 * v_ref[...] + (1.0 - ADAM_B2) * (gv * gv)
        d_ref[...] = -ADAM_LR * ((mn / c1) / (jnp.sqrt(vn / c2) + ADAM_EPS) + ADAM_WD * w_ref[...])
        mo_ref[...] = mn
        vo_ref[...] = vn

    spec = _row_spec(cols, rb)
    out = jax.ShapeDtypeStruct((rows, cols), F32)
    return pl.pallas_call(
        body, name=name, out_shape=(out, out, out), grid=(rows // rb,), in_specs=[spec] * 4, out_specs=(spec,) * 3,
        compiler_params=_params(("parallel",)),
    )(w, g, m, v)


def _prefetch_call(body, name, scalar, ops, grid, in_specs, out_specs, out_shape, sem):
    spec = pltpu.PrefetchScalarGridSpec(num_scalar_prefetch=1, grid=grid, in_specs=in_specs, out_specs=out_specs)
    return pl.pallas_call(body, name=name, grid_spec=spec, out_shape=out_shape, compiler_params=_params(sem))(scalar, *ops)


def pair_sums(gs, gots, core, name):
    n = len(gs)

    def body(core_ref, *refs):
        for f in range(n):
            refs[2 * n + f][...] = (refs[f][...].astype(F32) + refs[n + f][...].astype(F32)).astype(BF16)

    blk = lambda g, rows_of: pl.BlockSpec((None, g.shape[1] // 2, g.shape[2]), rows_of)
    mine = [blk(g, lambda j, core_ref: (j, core_ref[0], 0)) for g in gs]
    half = [blk(g, lambda j, core_ref: (j, 0, 0)) for g in gs]
    outs = tuple(jax.ShapeDtypeStruct((g.shape[0], g.shape[1] // 2, g.shape[2]), BF16) for g in gs)
    return _prefetch_call(body, name, core, (*gs, *gots), (N_CHIPS,), mine + half, tuple(half), outs, ("parallel",))


def chip_sums(pairs, gots, chip, layers, intos, name):
    n = len(pairs)

    def body(chip_ref, *refs):
        for f in range(n):
            p_ref, a_ref, b_ref, c_ref = refs[4 * f:4 * f + 4]
            refs[5 * n + f][...] = ((p_ref[...].astype(F32) + a_ref[...].astype(F32)) + b_ref[...].astype(F32)) + c_ref[...].astype(F32)

    in_specs, ops = [], []
    for p, got in zip(pairs, gots):
        blk = lambda at, p=p: pl.BlockSpec((None,) + p.shape[1:], at)
        in_specs += [blk(lambda i, chip_ref: (chip_ref[0], 0, 0))] + [blk(lambda i, chip_ref, k=k: (k, 0, 0)) for k in range(3)]
        ops += [p, got, got, got]
    out_specs = tuple(pl.BlockSpec((None,) + p.shape[1:], lambda i, chip_ref, l=l: (l, 0, 0)) for p, l in zip(pairs, layers))
    spec = pltpu.PrefetchScalarGridSpec(num_scalar_prefetch=1, grid=(1,), in_specs=in_specs + [_ANY] * n, out_specs=out_specs)
    return pl.pallas_call(
        body, name=name, grid_spec=spec, out_shape=tuple(jax.ShapeDtypeStruct(t.shape, t.dtype) for t in intos),
        input_output_aliases={1 + 4 * n + f: f for f in range(n)}, compiler_params=_params(("arbitrary",)),
    )(chip, *ops, *intos)


def adamw_family(w, m, v, g_mine, g_other, core, name):
    nl, r, c = w.shape
    gc = g_mine.shape[2]
    rh = r // 2
    nb = 4 if rh % 512 == 0 else (2 if rh % 16 == 0 and rh > 256 else 1)
    rb = rh // nb
    c1 = 1.0 - ADAM_B1 ** ADAM_STEP
    c2 = 1.0 - ADAM_B2 ** ADAM_STEP

    def body(core_ref, w_ref, m_ref, v_ref, gm_ref, go_ref, g_ref, d_ref, mo_ref, vo_ref):
        gv = jnp.where(pl.program_id(1) == core_ref[0], gm_ref[...], go_ref[...])[:, :c]
        mn = ADAM_B1 * m_ref[...] + (1.0 - ADAM_B1) * gv
        vn = ADAM_B2 * v_ref[...] + (1.0 - ADAM_B2) * (gv * gv)
        g_ref[...] = gv
        d_ref[...] = -ADAM_LR * ((mn / c1) / (jnp.sqrt(vn / c2) + ADAM_EPS) + ADAM_WD * w_ref[...])
        mo_ref[...] = mn
        vo_ref[...] = vn

    full = pl.BlockSpec((None, rb, c), lambda l, h, i, core_ref: (l, h * nb + i, 0))
    mine = pl.BlockSpec((None, rb, gc), lambda l, h, i, core_ref: (l, jnp.where(h == core_ref[0], i, 0), 0))
    other = pl.BlockSpec((None, rb, gc), lambda l, h, i, core_ref: (l, jnp.where(h == core_ref[0], 0, i), 0))
    out = jax.ShapeDtypeStruct((nl, r, c), F32)
    return _prefetch_call(body, name, core, (w, m, v, g_mine, g_other), (nl, 2, nb), [full, full, full, mine, other],
                          (full, full, full, full), (out, out, out, out), ("parallel", "parallel", "parallel"))


def _coords():
    return lax.axis_index("x"), lax.axis_index("y"), lax.axis_index("c")


def _other_chips(x, y):
    return ((1 - x, 1 - y), (1 - x, y), (x, 1 - y))


_ANY = pl.BlockSpec(memory_space=pl.ANY)


def _exchange_call(body, name, arrays, out_shapes, n_copies, n_local=0):
    n = len(arrays)

    def wrapped(*refs):
        body(refs[:n], refs[n:n + len(out_shapes)], *refs[n + len(out_shapes):])

    scratch = [pltpu.SemaphoreType.DMA((n_copies,)), pltpu.SemaphoreType.DMA((n_copies,))]
    if n_local:
        scratch.append(pltpu.SemaphoreType.DMA((n_local,)))
    return pl.pallas_call(
        wrapped, name=name, out_shape=tuple(out_shapes), in_specs=[_ANY] * n, out_specs=tuple([_ANY] * len(out_shapes)),
        scratch_shapes=scratch, compiler_params=_params(),
    )(*arrays)


def _remote(send_sems, recv_sems, n, src, dst, to):
    return pltpu.make_async_remote_copy(src_ref=src, dst_ref=dst, send_sem=send_sems.at[n], recv_sem=recv_sems.at[n],
                                        device_id=to, device_id_type=MESH)


class Rider:
    def __init__(self, arrays, out_shapes, n_remote, n_local, copies, then=None):
        self.arrays, self.out_shapes, self.n_remote, self.n_local = list(arrays), list(out_shapes), n_remote, n_local
        self.copies, self.then = copies, then

    def sems(self):
        return [pltpu.SemaphoreType.DMA((self.n_remote,)), pltpu.SemaphoreType.DMA((self.n_remote,)),
                pltpu.SemaphoreType.DMA((max(self.n_local, 1),))]

    def run(self, name):
        n, no = len(self.arrays), len(self.out_shapes)

        def body(*refs):
            for stage in (self.copies, self.then):
                if stage is not None:
                    cps = stage(refs[:n], refs[n:n + no], *refs[n + no:])
                    for cp in cps:
                        cp.start()
                    for cp in cps:
                        cp.wait()

        return pl.pallas_call(
            body, name=name, out_shape=tuple(self.out_shapes), in_specs=[_ANY] * n, out_specs=tuple([_ANY] * no),
            scratch_shapes=self.sems(), compiler_params=_params(),
        )(*self.arrays)


def ride(rider, body, n_in, n_out, grid):
    if rider is None:
        return body
    ni, no = len(rider.arrays), len(rider.out_shapes)

    def wrapped(*refs):
        ins, r_in = refs[:n_in], refs[n_in:n_in + ni]
        outs = refs[n_in + ni:n_in + ni + n_out]
        r_out = refs[n_in + ni + n_out:n_in + ni + n_out + no]
        rest = refs[n_in + ni + n_out + no:]
        scratch, sems = rest[:len(rest) - 3], rest[len(rest) - 3:]
        step, total = 0, 1
        for a, g in enumerate(grid):
            step, total = step * g + pl.program_id(a), total * g
        assert total >= 3
        relay_at = (7 * total) // 8 if rider.then is not None else total - 1

        @pl.when(step == 0)
        def _():
            for cp in rider.copies(r_in, r_out, *sems):
                cp.start()

        body(*ins, *outs, *scratch)

        @pl.when(step == relay_at)
        def _():
            for cp in rider.copies(r_in, r_out, *sems):
                cp.wait()
            if rider.then is not None:
                for cp in rider.then(r_in, r_out, *sems):
                    cp.start()

        if rider.then is not None:
            @pl.when(step == total - 1)
            def _():
                for cp in rider.then(r_in, r_out, *sems):
                    cp.wait()

    return wrapped


def call_with_rider(body, name, rider, ops, in_specs, out_shape, out_specs, scratch, grid):
    n_in, n_out = len(ops), len(out_shape)
    ops, in_specs, out_shape, out_specs, scratch = list(ops), list(in_specs), list(out_shape), list(out_specs), list(scratch)
    if rider is not None:
        ops += rider.arrays
        in_specs += [_ANY] * len(rider.arrays)
        out_shape += rider.out_shapes
        out_specs += [_ANY] * len(rider.out_shapes)
        scratch += rider.sems()
    res = pl.pallas_call(
        ride(rider, body, n_in, n_out, grid), name=name, out_shape=tuple(out_shape), grid=grid, in_specs=in_specs,
        out_specs=tuple(out_specs), scratch_shapes=scratch, compiler_params=_params(("arbitrary",) * len(grid)),
    )(*ops)
    return tuple(res[:n_out]), list(res[n_out:])


def gather_rider(shards):
    nf = len(shards)
    half = lambda ref, which: pl.ds(which * (ref.shape[-2] // 2), ref.shape[-2] // 2)

    def copies(s_refs, o_refs, send_sems, recv_sems, local_sems):
        x, y, c = _coords()
        me = 2 * x + y
        cps = [pltpu.make_async_copy(s_refs[f], o_refs[f].at[me], local_sems.at[f]) for f in range(nf)]
        for k, (px, py) in enumerate(_other_chips(x, y)):
            for f in range(nf):
                rows = half(s_refs[f], c)
                cps.append(_remote(send_sems, recv_sems, k * nf + f, s_refs[f].at[rows], o_refs[f].at[me, rows], (px, py, c)))
        return cps

    def relay(s_refs, o_refs, send_sems, recv_sems, local_sems):
        x, y, c = _coords()
        cps = []
        for k, (px, py) in enumerate(_other_chips(x, y)):
            for f in range(nf):
                landed = o_refs[f].at[2 * px + py, half(s_refs[f], c)]
                cps.append(_remote(send_sems, recv_sems, (3 + k) * nf + f, landed, landed, (x, y, 1 - c)))
        return cps

    return Rider(shards, [jax.ShapeDtypeStruct((N_CHIPS,) + sh.shape, sh.dtype) for sh in shards], 6 * nf, nf, copies, relay)


def scatter_rider(pairs):
    nf = len(pairs)

    def copies(p_refs, o_refs, send_sems, recv_sems, local_sems):
        x, y, c = _coords()
        cps = []
        for k, (px, py) in enumerate(_other_chips(x, y)):
            for f in range(nf):
                cps.append(_remote(send_sems, recv_sems, k * nf + f, p_refs[f].at[2 * px + py], o_refs[f].at[k], (px, py, c)))
        return cps

    return Rider(pairs, [jax.ShapeDtypeStruct((3,) + p.shape[1:], p.dtype) for p in pairs], 3 * nf, 0, copies)


def pair_swap(grads, name):
    def body(g_refs, o_refs, send_sems, recv_sems):
        x, y, c = _coords()
        cps = []
        for f, g_ref in enumerate(g_refs):
            rh = g_ref.shape[1] // 2
            cps.append(_remote(send_sems, recv_sems, f, g_ref.at[:, pl.ds((1 - c) * rh, rh), :], o_refs[f], (x, y, 1 - c)))
        for cp in cps:
            cp.start()
        for cp in cps:
            cp.wait()

    outs = [jax.ShapeDtypeStruct((g.shape[0], g.shape[1] // 2, g.shape[2]), g.dtype) for g in grads]
    return _exchange_call(body, name, grads, outs, len(grads))


def half_swap(halves, name):
    def body(h_refs, o_refs, send_sems, recv_sems):
        x, y, c = _coords()
        cps = [_remote(send_sems, recv_sems, f, h_ref, o_refs[f], (x, y, 1 - c)) for f, h_ref in enumerate(h_refs)]
        for cp in cps:
            cp.start()
        for cp in cps:
            cp.wait()

    return _exchange_call(body, name, halves, [jax.ShapeDtypeStruct(h.shape, h.dtype) for h in halves], len(halves))


def allsum_small(part, name):
    def body(p_ref, tot_ref, all_ref, send_sems, recv_sems):
        x, y, c = _coords()
        me, sibling = (x, y, c), (x, y, 1 - c)
        chips = _other_chips(x, y)

        def slot(px, py, pc):
            return all_ref.at[4 * px + 2 * py + pc]

        def copy(k, block, to, src=None):
            return pltpu.make_async_remote_copy(src_ref=slot(*block) if src is None else src, dst_ref=slot(*block),
                                                send_sem=send_sems.at[k], recv_sem=recv_sems.at[k], device_id=to, device_id_type=MESH)

        slot(*me)[...] = p_ref[...]
        first = [copy(0, me, sibling, src=p_ref)] + [copy(1 + j, me, (*chip, c), src=p_ref) for j, chip in enumerate(chips)]
        for cp in first:
            cp.start()
        passed = [copy(4 + j, (*chip, c), sibling) for j, chip in enumerate(chips)]
        for j, chip in enumerate(chips):
            copy(1 + j, (*chip, c), me).wait_recv()
            passed[j].start()
        copy(0, sibling, me).wait_recv()
        for j, chip in enumerate(chips):
            copy(4 + j, (*chip, 1 - c), me).wait_recv()
        for cp in first + passed:
            cp.wait_send()
        tot = all_ref[0]
        for d in range(1, 8):
            tot = tot + all_ref[d]
        tot_ref[...] = tot

    vm = pl.BlockSpec(memory_space=pltpu.VMEM)
    return pl.pallas_call(
        body, name=name, out_shape=jax.ShapeDtypeStruct(part.shape, F32), in_specs=[vm], out_specs=vm,
        scratch_shapes=[pltpu.VMEM((8,) + part.shape, F32), pltpu.SemaphoreType.DMA((7,)), pltpu.SemaphoreType.DMA((7,))],
        compiler_params=_params(),
    )(part)


QKVF_COLS = 772
QKVF_PAD = 896
FORWARD_CARRY = {0: (("fi0",), ("qkv1", "o1", "fo1"), ("fo0",)), 1: (("fi1", "qkv2", "o2"), (), ()),
                 2: (("fi2",), ("qkv3", "o3", "fo3"), ("fo2",)), 3: (("fi3",), (), ())}


def _tables_for(s):
    return _rot_tables(s)


def layer_families(layer):
    return (0, 1, layer // 2) if layer % 2 == 0 else (2, 3, layer // 2)


class GradientExchange:
    def __init__(self, family_layers):
        self.core = lax.axis_index("c").astype(jnp.int32).reshape(1)
        self.chip = (2 * lax.axis_index("x") + lax.axis_index("y")).astype(jnp.int32).reshape(1)
        self.family_layers = family_layers
        self.pairs, self.mine, self.pending = {}, {}, []

    def add(self, items, tag):
        gs = [g for _, _, g in items]
        sums = pair_sums(gs, pair_swap(gs, f"grad_pair_swap_{tag}"), self.core, f"grad_pair_sum_{tag}")
        for (fam, li, _), pair in zip(items, sums):
            self.pairs[(fam, li)] = pair
            self.pending.append((fam, li))

    def rider(self, only=None):
        keys = [k for k in self.pending if only is None or k in only]
        self.pending = [k for k in self.pending if k not in keys]
        return (scatter_rider([self.pairs[k] for k in keys]) if keys else None), keys

    def landed(self, keys, outs):
        batch = []
        for k, o in list(zip(keys, outs)) + [(None, None)]:
            if batch and (k is None or k[0] in [b[0][0] for b in batch]):
                ks = [b[0] for b in batch]
                pairs = [self.pairs[b] for b in ks]
                intos = [self.mine[fam] if fam in self.mine else jnp.zeros((self.family_layers[fam],) + p.shape[1:], F32)
                         for (fam, _), p in zip(ks, pairs)]
                sums = chip_sums(pairs, [b[1] for b in batch], self.chip, [li for _, li in ks], intos,
                                 "grad_chip_sum_" + "_".join(f"{f}{li}" for f, li in ks))
                self.mine.update({fam: t for (fam, _), t in zip(ks, sums)})
                batch = []
            if k is not None:
                batch.append((k, o))

    def finish(self, weights, moments1, moments2):
        last, keys = self.rider()
        if last is not None:
            self.landed(keys, last.run("grad_chip_scatter_last"))
        mine = [self.mine[fam] for fam in range(len(weights))]
        other = half_swap(mine, "grad_half_swap")
        return [adamw_family(w, m, v, gm, go, self.core, f"adamw_{f}")
                for f, (w, m, v, gm, go) in enumerate(zip(weights, moments1, moments2, mine, other))]


class KeepGradients:
    def __init__(self):
        self.grads = {}

    def add(self, items, tag):
        for fam, li, g in items:
            self.grads[(fam, li)] = g

    def rider(self, only=None):
        return None, []

    def landed(self, keys, outs):
        pass


def kernel(x, norm_mix, w_qkv_even, w_o_even, w_qkvf_odd, b_forget, w_o_odd, norm_ffn, w_ffn_in, w_ffn_out, norm_final, loss_target, m_norm_mix, m_w_qkv_even, m_w_o_even, m_w_qkvf_odd, m_b_forget, m_w_o_odd, m_norm_ffn, m_w_ffn_in, m_w_ffn_out, m_norm_final, v_norm_mix, v_w_qkv_even, v_w_o_even, v_w_qkvf_odd, v_b_forget, v_w_o_odd, v_norm_ffn, v_w_ffn_in, v_w_ffn_out, v_norm_final):
    w_shards = [w_qkv_even, w_o_even, w_qkvf_odd, w_o_odd, w_ffn_in, w_ffn_out]
    shards = [w.astype(BF16) for w in w_shards]
    tables = _tables_for(x.shape[1])
    bias_pad = jnp.pad(b_forget, ((0, 0), (0, LANES - N_HEADS)))

    mine = {}
    for layer in range(DEPTH):
        fam_qkv, fam_o, li = layer_families(layer)
        mine.update({f"qkv{layer}": shards[fam_qkv][li], f"o{layer}": shards[fam_o][li],
                     f"fi{layer}": shards[4][layer], f"fo{layer}": shards[5][layer]})
    fetch = lambda names: gather_rider([mine[n] for n in names])
    have = dict(zip(("qkv0", "o0"), fetch(("qkv0", "o0")).run("gather_first")))
    saved, cur = [], x[0]
    h1 = rmsnorm_fwd(cur, norm_mix[0:1], "l0_norm_mix")
    for layer in range(DEPTH):
        next_gain = norm_mix[layer + 1:layer + 2] if layer + 1 < DEPTH else None
        cur, h1, keep = forward_layer(layer, cur, h1, have, norm_ffn[layer:layer + 1], next_gain, tables,
                                      bias_pad[layer // 2:layer // 2 + 1], fetch, *FORWARD_CARRY[layer])
        saved.append(keep)

    dcur, g_final, loss_part = loss_head(cur, norm_final.reshape(1, D_MODEL), loss_target[0], "loss_head")

    exchange = GradientExchange([w.shape[0] for w in w_shards])
    g_mix, g_ffn, g_bias = [None] * DEPTH, [None] * DEPTH, [None] * (DEPTH // 2)
    for layer in reversed(range(DEPTH)):
        dcur, g_mix[layer], g_ffn[layer], g_b = backward_layer(layer, dcur, saved[layer], norm_mix[layer:layer + 1],
                                                               norm_ffn[layer:layer + 1], tables, bias_pad[layer // 2:layer // 2 + 1], exchange)
        if g_b is not None:
            g_bias[layer // 2] = g_b

    zero_row = jnp.zeros((1, D_MODEL), F32)
    pad16 = lambda v: jnp.pad(v, (0, D_MODEL - v.shape[0]))[None, :]
    small_rows = lambda mix, ffn, fin, bias, last: jnp.concatenate(
        [r.reshape(1, D_MODEL) for r in mix] + [r.reshape(1, D_MODEL) for r in ffn] + [fin.reshape(1, D_MODEL)]
        + [pad16(b) for b in bias] + [last] + [zero_row] * (SMALL_ROWS - 12), axis=0)
    loss_row = pad16(loss_part[0, :1])
    small_g = allsum_small(small_rows(g_mix, g_ffn, g_final, g_bias, loss_row), "allsum_small")
    loss = small_g[11, 0]
    small_g = small_g.at[11].set(0.0)
    sw = small_rows(list(norm_mix), list(norm_ffn), norm_final, list(b_forget), zero_row)
    sm = small_rows(list(m_norm_mix), list(m_norm_ffn), m_norm_final, list(m_b_forget), zero_row)
    sv = small_rows(list(v_norm_mix), list(v_norm_ffn), v_norm_final, list(v_b_forget), zero_row)
    sd, snm, snv = adamw(sw, small_g, sm, sv, "adamw_small")

    def small_out(a):
        return a[0:4], a[8, :], a[9:11, :N_HEADS], a[4:8]

    big = exchange.finish(w_shards, [m_w_qkv_even, m_w_o_even, m_w_qkvf_odd, m_w_o_odd, m_w_ffn_in, m_w_ffn_out],
                          [v_w_qkv_even, v_w_o_even, v_w_qkvf_odd, v_w_o_odd, v_w_ffn_in, v_w_ffn_out])

    def outputs(small, which):
        mix, fin, bias, ffn = small_out(small)
        qkv_e, o_e, qkvf, o_o, fi, fo = [big[f][which] for f in range(6)]
        return [mix, qkv_e, o_e, qkvf, bias, o_o, ffn, fi, fo, fin]

    return (loss, dcur[None], *outputs(small_g, 0), *outputs(sd, 1), *outputs(snm, 2), *outputs(snv, 3))


def _chip_tile(rows, cols, at):
    return pl.BlockSpec((None, rows, cols), at)


def forward_layer(layer, cur, h1, have, ffn_gain, next_gain, tables, bias_row, fetch=None, carry=(), side_carry=(), ffn_carry=()):
    n = f"l{layer}"
    s = cur.shape[0]
    w_qkv, w_o = have[f"qkv{layer}"], have[f"o{layer}"]
    rider = fetch(carry) if carry else None
    side_rider = fetch(side_carry) if side_carry else None
    keep = {"x": cur, "h1": h1, "w_o": w_o.reshape(D_ATTN, D_MODEL)}
    side = []
    if layer % 2 == 0:
        qkv = matmul(h1, w_qkv, "nn", BF16, n + "_qkv", 1024, 768, 1024, mnk=(s, 3 * D_ATTN, D_MODEL),
                     b_spec=_chip_tile(D_MODEL, 768, lambda i, j, kk: (j, 0, 0)))
        o_sb, st, rode = causal_fwd(qkv, 4, "sb", n + "_sb_fwd", rider=rider)
        o_dil, lse_dil, side = dilated_fwd(qkv, tables, n + "_dil_fwd", rider=side_rider)
        attn = jnp.concatenate([o_sb, o_dil], axis=1).astype(BF16)
        keep.update(o_dil=o_dil, lse_dil=lse_dil, w_qkv=w_qkv)
    else:
        natural = jnp.transpose(w_qkv, (1, 0, 2)).reshape(D_MODEL, N_CHIPS * QKVF_COLS)
        w_gate = jnp.pad(natural[:, 3 * D_ATTN:], ((0, 0), (0, LANES - N_HEADS)))
        qkv = matmul(h1, natural[:, :3 * D_ATTN], "nn", BF16, n + "_qkv", 1024, 768, 1024)
        fl = matmul(h1, w_gate, "nn", F32, n + "_fgate", 512, LANES, 1024)
        cum = forget_fwd(fl, bias_row, n + "_forget_fwd")
        f_heads = cum[:, :N_HEADS].T
        fq = jnp.broadcast_to(f_heads[:, :, None], (N_HEADS, s, LANES))
        fk = f_heads.reshape(N_HEADS // 2, 2, s)
        attn, st, rode = causal_fwd(qkv, 8, "fox", n + "_fox_fwd", fq=fq, fk=fk, rider=rider)
        attn = attn.astype(BF16)
        keep.update(fl=fl, fq=fq, fk=fk, w_qkv=jnp.concatenate([natural[:, :3 * D_ATTN], w_gate], axis=1))
    have.update(zip(carry, rode))
    have.update(zip(side_carry, side))
    w_fi = have[f"fi{layer}"]
    mid, h2 = matmul(attn, keep["w_o"], "nn", F32, n + "_attn_out", 512, 1024, 1024, res=cur, norm_gain=ffn_gain)
    gate, up, act, rode = ffn_in_swiglu(h2, w_fi, n + "_ffn_in", rider=fetch(ffn_carry) if ffn_carry else None)
    have.update(zip(ffn_carry, rode))
    w_fo = have[f"fo{layer}"].reshape(D_FF, D_MODEL)
    if next_gain is None:
        out, h_next = matmul(act, w_fo, "nn", F32, n + "_ffn_out", 512, 1024, D_FF, res=mid), None
    else:
        out, h_next = matmul(act, w_fo, "nn", F32, n + "_ffn_out", 512, 1024, D_FF, res=mid, norm_gain=next_gain)
    keep.update(qkv=qkv, st=st, attn=attn, mid=mid, h2=h2, gate=gate, up=up, act=act, w_fi=w_fi, w_fo=w_fo)
    return out, h_next, keep


def backward_layer(layer, dcur, kp, mix_gain, ffn_gain, tables, bias_row, exchange):
    n = f"l{layer}"
    s = dcur.shape[0]
    fam_qkv, fam_o, li = layer_families(layer)
    g_fo = matmul(kp["act"], dcur, "tn", BF16, n + "_d_w_ffn_out", 1408, 1024, s)
    dgu = swiglu_bwd(dcur, kp["w_fo"], kp["gate"], kp["up"], n + "_d_swiglu")
    g_fi = matmul(kp["h2"], dgu, "tn", BF16, n + "_d_w_ffn_in", 1024, 1408, 2048, mnk=(D_MODEL, 2 * D_FF, s),
                  o_spec=_chip_tile(D_MODEL, 1408, lambda i, j, kk: (j, 0, 0)), out_shape=(N_CHIPS, D_MODEL, 1408))
    dmid, g_ffn = dh_norm_bwd(dgu, kp["w_fi"], kp["mid"], ffn_gain, dcur, n + "_d_h2", 256)
    g_o = matmul(kp["attn"], dmid, "tn", BF16, n + "_d_w_o", 1024, 1024, s)
    dattn = matmul(dmid, kp["w_o"], "nt", F32, n + "_d_attn", 1024, 1024, 1024)
    exchange.add([(5, layer, g_fo.reshape(N_CHIPS, D_FF // N_CHIPS, D_MODEL)), (4, layer, g_fi),
                  (fam_o, li, g_o.reshape(N_CHIPS, D_ATTN // N_CHIPS, D_MODEL))], f"l{layer}_ffn")
    g_bias = None
    if layer % 2 == 0:
        rider, keys = exchange.rider(only=[(4, layer), (fam_o, li)])
        dq_a, dk_a, dv_a, rode = causal_bwd(kp["qkv"], dattn, kp["st"], 4, "sb", n + "_sb_bwd", rider=rider)
        exchange.landed(keys, rode)
        rider, keys = exchange.rider()
        dq_b, dk_b, dv_b, rode = dilated_bwd(kp["qkv"], tables, dattn, kp["o_dil"], kp["lse_dil"], 4, n + "_dil_bwd", rider=rider)
        dproj = jnp.concatenate([dq_a, dq_b, dk_a.astype(BF16), dk_b, dv_a.astype(BF16), dv_b], axis=1)
        g_qkv = matmul(kp["h1"], dproj, "tn", BF16, n + "_d_w_qkv", 1024, 768, 2048, mnk=(D_MODEL, 3 * D_ATTN, s),
                       o_spec=_chip_tile(D_MODEL, 768, lambda i, j, kk: (j, 0, 0)), out_shape=(N_CHIPS, D_MODEL, 768))
    else:
        rider, keys = exchange.rider()
        dq_f, dk_f, dv_f, dfk, rode = causal_bwd(kp["qkv"], dattn, kp["st"], 8, "fox", n + "_fox_bwd", fq=kp["fq"], fk=kp["fk"],
                                                 rider=rider)
        dcum = jnp.pad(dfk.reshape(N_HEADS, s).T, ((0, 0), (0, LANES - N_HEADS)))
        dfl, dbias = forget_bwd(kp["fl"], bias_row, dcum, n + "_forget_bwd")
        g_bias = dbias[0, :N_HEADS]
        dproj = jnp.concatenate([dq_f, dk_f.astype(BF16), dv_f.astype(BF16), dfl.astype(BF16)], axis=1)
        g_nat = matmul(kp["h1"], dproj, "tn", BF16, n + "_d_w_qkv", 1024, 640, 2048)
        g_qkv = g_nat[:, :N_CHIPS * QKVF_COLS].reshape(D_MODEL, N_CHIPS, QKVF_COLS)
        g_qkv = jnp.transpose(jnp.pad(g_qkv, ((0, 0), (0, 0), (0, QKVF_PAD - QKVF_COLS))), (1, 0, 2))
    exchange.landed(keys, rode)
    exchange.add([(fam_qkv, li, g_qkv)], f"l{layer}_qkv")
    dx, g_mix = dh_norm_bwd(dproj, kp["w_qkv"], kp["x"], mix_gain, dmid, n + "_d_h1", 512)
    return dx, g_mix, g_ffn, g_bias


def local_step(xs, target, norm_mix, norm_ffn, norm_final, b_forget, layer_weights):
    tables = _tables_for(xs.shape[0])
    bias_pad = jnp.pad(b_forget, ((0, 0), (0, LANES - N_HEADS)))
    saved, cur, have = [], xs, {}
    h1 = rmsnorm_fwd(cur, norm_mix[0:1], "l0_norm_mix")
    for layer in range(DEPTH):
        have.update(zip((f"qkv{layer}", f"o{layer}", f"fi{layer}", f"fo{layer}"), layer_weights[layer]))
        next_gain = norm_mix[layer + 1:layer + 2] if layer + 1 < DEPTH else None
        cur, h1, keep = forward_layer(layer, cur, h1, have, norm_ffn[layer:layer + 1], next_gain, tables,
                                      bias_pad[layer // 2:layer // 2 + 1])
        saved.append(keep)
    dcur, g_final, loss_part = loss_head(cur, norm_final.reshape(1, D_MODEL), target, "loss_head")
    keeper = KeepGradients()
    g_mix, g_ffn, g_bias = [None] * DEPTH, [None] * DEPTH, [None] * (DEPTH // 2)
    for layer in reversed(range(DEPTH)):
        dcur, g_mix[layer], g_ffn[layer], g_b = backward_layer(layer, dcur, saved[layer], norm_mix[layer:layer + 1],
                                                               norm_ffn[layer:layer + 1], tables, bias_pad[layer // 2:layer // 2 + 1], keeper)
        if g_b is not None:
            g_bias[layer // 2] = g_b
    return dcur, keeper.grads, (g_mix, g_ffn, g_final, g_bias), loss_part
```

```python
import functools

import jax
import jax.numpy as jnp
from jax import lax
from jax.experimental import pallas as pl
from jax.experimental.pallas import tpu as pltpu

F32 = jnp.float32
BF16 = jnp.bfloat16
MESH = pl.DeviceIdType.MESH

D_MODEL = 1024
DEPTH = 4
HEAD_DIM = 64
N_HEADS = 16
D_ATTN = 1024
D_FF = 2816
ROPE_THETA = 500000.0
ROT_HALF = 8
RMS_EPS = 1e-5
DIL_STRIDES = (1, 4, 16)
ADAM_LR, ADAM_B1, ADAM_B2, ADAM_EPS, ADAM_WD, ADAM_STEP = 0.001, 0.9, 0.999, 1e-8, 0.01, 10

LANES = 128
BLK = 128
VMEM_LIMIT = 56 * 1024 * 1024
NEG = -1e30
N_CHIPS = 4
SMALL_ROWS = 16


def _params(sem=None):
    return pltpu.CompilerParams(dimension_semantics=sem, vmem_limit_bytes=VMEM_LIMIT)


def _dot(a, b):
    return lax.dot_general(a, b, (((1,), (0,)), ((), ())), preferred_element_type=F32)


def _dot_nt(a, b):
    return lax.dot_general(a, b, (((1,), (1,)), ((), ())), preferred_element_type=F32)


def _dot_tn(a, b):
    return lax.dot_general(a, b, (((0,), (0,)), ((), ())), preferred_element_type=F32)


def _split3(x):
    x1 = x.astype(BF16)
    r1 = x - x1.astype(F32)
    x2 = r1.astype(BF16)
    x3 = (r1 - x2.astype(F32)).astype(BF16)
    return x1, x2, x3


def _dot_exact_lhs(x, t):
    x1, x2, x3 = _split3(x)
    return _dot(x1, t) + _dot(x2, t) + _dot(x3, t)


def _dot_exact_rhs(t, x):
    x1, x2, x3 = _split3(x)
    return _dot(t, x1) + _dot(t, x2) + _dot(t, x3)


def _iotas(shape=(BLK, LANES)):
    return lax.broadcasted_iota(jnp.int32, shape, 0), lax.broadcasted_iota(jnp.int32, shape, 1)


_DIMS = {"nn": (((1,), (0,)), ((), ())), "nt": (((1,), (1,)), ((), ())), "tn": (((0,), (0,)), ((), ()))}


def matmul(a, b, mode, out_dtype, name, tm, tn, tk, res=None, mnk=None, b_spec=None, o_spec=None, out_shape=None, norm_gain=None):
    if mnk is not None:
        m, n, k = mnk
    elif mode == "nn":
        (m, k), (k2, n) = a.shape, b.shape
    elif mode == "nt":
        (m, k), (n, k2) = a.shape, b.shape
    else:
        (k, m), (k2, n) = a.shape, b.shape
    assert m % tm == 0 and n % tn == 0 and k % tk == 0, (name, a.shape, b.shape)
    nk = k // tk
    a_spec = pl.BlockSpec((tk, tm), lambda i, j, kk: (kk, i)) if mode == "tn" else pl.BlockSpec((tm, tk), lambda i, j, kk: (i, kk))
    if b_spec is None:
        b_spec = pl.BlockSpec((tn, tk), lambda i, j, kk: (j, kk)) if mode == "nt" else pl.BlockSpec((tk, tn), lambda i, j, kk: (kk, j))
    r_spec = pl.BlockSpec((tm, tn), lambda i, j, kk: (i, j))
    if o_spec is None:
        o_spec = r_spec
    dims = _DIMS[mode]
    has_res, has_norm = res is not None, norm_gain is not None
    assert not has_norm or (tn == n and nk == 1)
    n_in = 2 + int(has_res) + int(has_norm)

    def body(*refs):
        a_ref, b_ref = refs[0], refs[1]
        r_ref = refs[2] if has_res else None
        o_ref = refs[n_in]

        def finish(v):
            if has_res:
                v = v + r_ref[...]
            o_ref[...] = v.astype(out_dtype)
            if has_norm:
                rstd = lax.rsqrt(jnp.mean(v * v, axis=-1, keepdims=True) + RMS_EPS)
                refs[n_in + 1][...] = (v * rstd * refs[n_in - 1][...]).astype(BF16)

        bv = b_ref[...]
        if bv.ndim == 3:
            bv = jnp.concatenate([bv[j] for j in range(bv.shape[0])], axis=1)
        p = lax.dot_general(a_ref[...].astype(BF16), bv.astype(BF16), dims, preferred_element_type=F32)
        if nk == 1:
            finish(p)
        else:
            acc = refs[-1]
            kk = pl.program_id(2)

            @pl.when(kk == 0)
            def _():
                acc[...] = p

            @pl.when(kk > 0)
            def _():
                acc[...] += p

            @pl.when(kk == nk - 1)
            def _():
                finish(acc[...])

    ops = [a, b] + ([res] if has_res else []) + ([norm_gain] if has_norm else [])
    specs = [a_spec, b_spec] + ([r_spec] if has_res else []) + ([pl.BlockSpec((1, tn), lambda i, j, kk: (0, j))] if has_norm else [])
    out_shape = jax.ShapeDtypeStruct((m, n) if out_shape is None else out_shape, out_dtype)
    return pl.pallas_call(
        body, name=name, out_shape=(out_shape, jax.ShapeDtypeStruct((m, n), BF16)) if has_norm else out_shape,
        grid=(m // tm, n // tn, nk), in_specs=specs, out_specs=(o_spec, r_spec) if has_norm else o_spec,
        scratch_shapes=[pltpu.VMEM((tm, tn), F32)] if nk > 1 else [],
        compiler_params=_params(("parallel", "parallel", "arbitrary")),
    )(*ops)


ROWS = 256


def _row_spec(cols, rows=ROWS):
    return pl.BlockSpec((rows, cols), lambda i: (i, 0))


def _fix_spec(r, cols):
    return pl.BlockSpec((r, cols), lambda i: (0, 0))


def rmsnorm_fwd(x, g, name):
    s, d = x.shape

    def body(x_ref, g_ref, h_ref):
        xv = x_ref[...]
        rstd = lax.rsqrt(jnp.mean(xv * xv, axis=-1, keepdims=True) + RMS_EPS)
        h_ref[...] = (xv * rstd * g_ref[...]).astype(BF16)

    return pl.pallas_call(
        body, name=name, out_shape=jax.ShapeDtypeStruct((s, d), BF16), grid=(s // ROWS,),
        in_specs=[_row_spec(d), _fix_spec(1, d)], out_specs=_row_spec(d), compiler_params=_params(("parallel",)),
    )(x, g)


def _rms_bwd_math(xv, gv, dh):
    rstd = lax.rsqrt(jnp.mean(xv * xv, axis=-1, keepdims=True) + RMS_EPS)
    xhat = xv * rstd
    u = dh * gv
    dx = rstd * (u - xhat * jnp.mean(u * xhat, axis=-1, keepdims=True))
    return dx, dh * xhat


def dh_norm_bwd(dy, w, x, g, dres, name, tm):
    s, k = dy.shape
    d = x.shape[1]

    def body(dy_ref, w_ref, x_ref, g_ref, dres_ref, dx_ref, dg_ref):
        wv = w_ref[...]
        if wv.ndim == 3:
            wv = jnp.concatenate([wv[j] for j in range(wv.shape[0])], axis=1)
        dx, dgt = _rms_bwd_math(x_ref[...], g_ref[...], _dot_nt(dy_ref[...], wv))
        dx_ref[...] = dres_ref[...] + dx
        part = jnp.sum(dgt, axis=0, keepdims=True)

        @pl.when(pl.program_id(0) == 0)
        def _():
            dg_ref[...] = part

        @pl.when(pl.program_id(0) > 0)
        def _():
            dg_ref[...] += part

    w_spec = pl.BlockSpec(w.shape, lambda i: (0,) * w.ndim)
    return pl.pallas_call(
        body, name=name, out_shape=(jax.ShapeDtypeStruct((s, d), F32), jax.ShapeDtypeStruct((1, d), F32)), grid=(s // tm,),
        in_specs=[pl.BlockSpec((tm, k), lambda i: (i, 0)), w_spec, _row_spec(d, tm), _fix_spec(1, d), _row_spec(d, tm)],
        out_specs=(_row_spec(d, tm), _fix_spec(1, d)), compiler_params=_params(("arbitrary",)),
    )(dy, w, x, g, dres)


def loss_head(x, g, target, name):
    s, d = x.shape

    def body(x_ref, g_ref, t_ref, dx_ref, dg_ref, loss_ref):
        xv, gv = x_ref[...], g_ref[...]
        rstd = lax.rsqrt(jnp.mean(xv * xv, axis=-1, keepdims=True) + RMS_EPS)
        err = xv * rstd * gv - t_ref[...]
        dx, dgt = _rms_bwd_math(xv, gv, err * (1.0 / d))
        dx_ref[...] = dx
        part = jnp.sum(dgt, axis=0, keepdims=True)
        lpart = jnp.full((1, LANES), 0.5 / d, F32) * jnp.sum(err * err)

        @pl.when(pl.program_id(0) == 0)
        def _():
            dg_ref[...] = part
            loss_ref[...] = lpart

        @pl.when(pl.program_id(0) > 0)
        def _():
            dg_ref[...] += part
            loss_ref[...] += lpart

    return pl.pallas_call(
        body, name=name,
        out_shape=(jax.ShapeDtypeStruct((s, d), F32), jax.ShapeDtypeStruct((1, d), F32), jax.ShapeDtypeStruct((1, LANES), F32)),
        grid=(s // ROWS,), in_specs=[_row_spec(d), _fix_spec(1, d), _row_spec(d)],
        out_specs=(_row_spec(d), _fix_spec(1, d), _fix_spec(1, LANES)), compiler_params=_params(("arbitrary",)),
    )(x, g, target)


def ffn_in_swiglu(h, w_in, name, tm=1024, rider=None):
    s, d = h.shape
    cols = w_in.shape[2]

    def body(h_ref, wg_ref, wu_ref, g_ref, u_ref, a_ref):
        hv = h_ref[...]
        gv, uv = _dot(hv, wg_ref[...]), _dot(hv, wu_ref[...])
        g_ref[...] = gv.astype(BF16)
        u_ref[...] = uv.astype(BF16)
        a_ref[...] = (gv * (1.0 / (1.0 + jnp.exp(-gv))) * uv).astype(BF16)

    tile = pl.BlockSpec((tm, cols), lambda i, j: (i, j))
    out = jax.ShapeDtypeStruct((s, 2 * cols), BF16)
    in_specs = [pl.BlockSpec((tm, d), lambda i, j: (i, 0)), pl.BlockSpec((None, d, cols), lambda i, j: (j, 0, 0)),
                pl.BlockSpec((None, d, cols), lambda i, j: (j + 2, 0, 0))]
    outs, rode = call_with_rider(body, name, rider, [h, w_in, w_in], in_specs, [out, out, out], [tile, tile, tile], [], (s // tm, 2))
    return (*outs, rode)


def swiglu_bwd(dy, w_out, gate, up, name):
    s, f = gate.shape
    d = dy.shape[1]

    def body(dy_ref, w_ref, g_ref, u_ref, o_ref):
        da = _dot_nt(dy_ref[...].astype(BF16), w_ref[...])
        gv, uv = g_ref[...].astype(F32), u_ref[...].astype(F32)
        sg = 1.0 / (1.0 + jnp.exp(-gv))
        o_ref[:, :f] = (da * uv * sg * (1.0 + gv * (1.0 - sg))).astype(BF16)
        o_ref[:, f:] = (da * gv * sg).astype(BF16)

    return pl.pallas_call(
        body, name=name, out_shape=jax.ShapeDtypeStruct((s, 2 * f), BF16), grid=(s // ROWS,),
        in_specs=[_row_spec(d), _fix_spec(f, d), _row_spec(f), _row_spec(f)], out_specs=_row_spec(2 * f),
        compiler_params=_params(("parallel",)),
    )(dy, w_out, gate, up)


Q_OFF, K_OFF, V_OFF = 0, 8, 16


KB = 512
BQ = 512
SUB = KB // BLK


def _softplus_parts(z):
    sp = jnp.log(1.0 + jnp.exp(-jnp.abs(z)))
    ls = jnp.minimum(z, 0.0) - sp
    return ls, ls - z


def _wide(t):
    return jnp.concatenate([t] * SUB, axis=1)


def _chunk_dots(x, tri):
    terms = []
    for u in range(SUB):
        xu = x[:, u * BLK:(u + 1) * BLK]
        hi = xu.astype(BF16)
        terms += [hi, (xu - hi.astype(F32)).astype(BF16)]
    r = _dot(jnp.concatenate(terms, axis=0), tri)
    rows = x.shape[0]
    piece = lambda n: r[n * rows:(n + 1) * rows]
    return [piece(2 * u) + piece(2 * u + 1) for u in range(SUB)]


def _block_suffix_sums(x, suffix, c):
    loc = _chunk_dots(x, suffix)
    out = [None] * SUB
    for u in reversed(range(SUB)):
        out[u] = loc[u] + c
        c = c + jnp.sum(x[:, u * BLK:(u + 1) * BLK], axis=1, keepdims=True)
    return jnp.concatenate(out, axis=1), c


def _block_prefix_sums(x, tri, c):
    loc = _chunk_dots(x, tri)
    out = []
    for u in range(SUB):
        out.append(loc[u] + c)
        c = c + jnp.sum(x[:, u * BLK:(u + 1) * BLK], axis=1, keepdims=True)
    return jnp.concatenate(out, axis=1), c


def causal_fwd(qkv, npairs, mode, name, fq=None, fk=None, rider=None):
    s = qkv.shape[0]
    nq = s // BQ
    fox = mode == "fox"

    def body(*refs):
        if fox:
            q_ref, k_ref, v_ref, fq_ref, fk_ref, o_ref, st_ref = refs
        else:
            q_ref, k_ref, v_ref, o_ref, st_ref = refs
        i = pl.program_id(1)
        nkb = (i * BQ + BQ - 1) // KB + 1
        row, lane = _iotas((BQ, KB))
        row_s, lane_s = _iotas()
        _, lane_q = _iotas((BQ, LANES))
        nfull = (i * BQ) // KB
        qpos = i * BQ + row
        qf = q_ref[...].astype(F32) * 0.125
        hms = (lane_q < HEAD_DIM, lane_q >= HEAD_DIM)
        qas = [jnp.where(hm, qf, 0.0).astype(BF16) for hm in hms]
        suffix = jnp.where(row_s > lane_s, 1.0, 0.0).astype(BF16)
        zero = jnp.zeros((BQ, LANES), F32)
        col0 = jnp.zeros((BQ, 1), F32)

        def kv(j):
            r0 = pl.multiple_of(j * KB, KB)
            return r0, k_ref[pl.ds(r0, KB), :], v_ref[pl.ds(r0, KB), :]

        if fox:
            fqs = [_wide(fq_ref[a]) for a in range(2)]

            def step(j, carry, masked):
                r0, kb, vb = kv(j)
                new = []
                for a in range(2):
                    acc, mx, l = carry[3 * a:3 * a + 3]
                    z = _dot_nt(qas[a], kb) + fqs[a] - fk_ref[a:a + 1, pl.ds(r0, KB)]
                    if masked:
                        z = jnp.where(r0 + lane <= qpos, z, NEG)
                    mnew = jnp.maximum(mx, jnp.max(z, axis=1, keepdims=True))
                    p = jnp.exp(z - mnew)
                    alpha = jnp.exp(mx - mnew)
                    new += [alpha * acc + _dot(p.astype(BF16), vb), mnew, alpha * l + jnp.sum(p, axis=1, keepdims=True)]
                return tuple(new)

            neg = jnp.full((BQ, 1), NEG, F32)
            res = lax.fori_loop(0, nfull, functools.partial(step, masked=False), (zero, neg, col0, zero, neg, col0))
            res = lax.fori_loop(nfull, nkb, functools.partial(step, masked=True), res)
            outs = [res[3 * a] / res[3 * a + 2] for a in range(2)]
            stats = [res[3 * a + 1] + jnp.log(res[3 * a + 2]) for a in range(2)]
        else:
            def step(j, carry, masked):
                r0, kb, vb = kv(j)
                strict = r0 + lane < qpos
                new = []
                for a in range(2):
                    acc, c = carry[2 * a:2 * a + 2]
                    ls, lm = _softplus_parts(_dot_nt(qas[a], kb))
                    if masked:
                        lm = jnp.where(strict, lm, 0.0)
                    between, c = _block_suffix_sums(lm, suffix, c)
                    aw = jnp.exp(ls + between)
                    if masked:
                        aw = jnp.where(strict, aw, 0.0)
                    new += [acc + _dot(aw.astype(BF16), vb), c]
                return tuple(new)

            res = lax.fori_loop(0, nkb - nfull, lambda jj, c: step(nkb - 1 - jj, c, True), (zero, col0, zero, col0))
            res = lax.fori_loop(0, nfull, lambda jj, c: step(nfull - 1 - jj, c, False), res)
            outs, stats = [res[0], res[2]], [res[1], res[3]]
        o_ref[...] = jnp.where(hms[0], outs[0], outs[1])
        for a in range(2):
            st_ref[a] = jnp.broadcast_to(stats[a], (BQ, LANES))

    col = lambda off: (lambda p, i: (0, off + p))
    in_specs = [pl.BlockSpec((BQ, LANES), lambda p, i: (i, Q_OFF + p)),
                pl.BlockSpec((s, LANES), col(K_OFF)), pl.BlockSpec((s, LANES), col(V_OFF))]
    ops = [qkv, qkv, qkv]
    if fox:
        in_specs += [pl.BlockSpec((2, BQ, LANES), lambda p, i: (p, i, 0)), pl.BlockSpec((None, 2, s), lambda p, i: (p, 0, 0))]
        ops += [fq, fk]
    (o, stat), rode = call_with_rider(
        body, name, rider, ops, in_specs,
        [jax.ShapeDtypeStruct((s, npairs * LANES), F32), jax.ShapeDtypeStruct((2 * npairs, s, LANES), F32)],
        [pl.BlockSpec((BQ, LANES), lambda p, i: (i, p)), pl.BlockSpec((2, BQ, LANES), lambda p, i: (p, i, 0))], [], (npairs, nq))
    return o, stat, rode


def causal_bwd(qkv, do, stat, npairs, mode, name, fq=None, fk=None, rider=None):
    s = qkv.shape[0]
    nq = s // BQ
    fox = mode == "fox"

    def body(*refs):
        if fox:
            q_ref, k_ref, v_ref, do_ref, st_ref, fq_ref, fk_ref, dq_ref, dk_out, dv_out, df_ref, p_s, dp_s, dk_ref, dv_ref = refs
        else:
            q_ref, k_ref, v_ref, do_ref, st_ref, dq_ref, dk_out, dv_out, dk_ref, dv_ref = refs
        i = pl.program_id(1)

        @pl.when(i == 0)
        def _():
            dk_ref[...] = jnp.zeros_like(dk_ref)
            dv_ref[...] = jnp.zeros_like(dv_ref)
            if fox:
                df_ref[...] = jnp.zeros_like(df_ref)

        nkb = (i * BQ + BQ - 1) // KB + 1
        nfull = (i * BQ) // KB
        row, lane = _iotas((BQ, KB))
        row_s, lane_s = _iotas()
        _, lane_q = _iotas((BQ, LANES))
        qpos = i * BQ + row
        qf = q_ref[...].astype(F32) * 0.125
        dov = do_ref[...]
        hms = (lane_q < HEAD_DIM, lane_q >= HEAD_DIM)
        qas = [jnp.where(hm, qf, 0.0).astype(BF16) for hm in hms]
        doas = [jnp.where(hm, dov, 0.0).astype(BF16) for hm in hms]
        stas = [_wide(st_ref[a]) for a in range(2)]
        zero = jnp.zeros((BQ, LANES), F32)
        col0 = jnp.zeros((BQ, 1), F32)

        def kv(j):
            r0 = pl.multiple_of(j * KB, KB)
            return r0, k_ref[pl.ds(r0, KB), :], v_ref[pl.ds(r0, KB), :]

        if fox:
            fqs = [_wide(fq_ref[a]) for a in range(2)]

            def probs(j, deltas, masked):
                r0, kb, vb = kv(j)
                new = []
                for a in range(2):
                    z = _dot_nt(qas[a], kb) + fqs[a] - fk_ref[a:a + 1, pl.ds(r0, KB)]
                    p = jnp.exp(z - stas[a])
                    if masked:
                        p = jnp.where(r0 + lane <= qpos, p, 0.0)
                    dp = _dot_nt(doas[a], vb)
                    p_s[a, j] = p
                    dp_s[a, j] = dp
                    new.append(deltas[a] + jnp.sum(p * dp, axis=1, keepdims=True))
                return tuple(new)

            deltas = lax.fori_loop(0, nfull, functools.partial(probs, masked=False), (col0, col0))
            deltas = lax.fori_loop(nfull, nkb, functools.partial(probs, masked=True), deltas)

            def step(j, dqs):
                r0, kb, _ = kv(j)
                new = []
                dk = jnp.zeros((KB, LANES), F32)
                dv = jnp.zeros((KB, LANES), F32)
                for a in range(2):
                    p = p_s[a, j]
                    ds = p * (dp_s[a, j] - deltas[a])
                    dsb = ds.astype(BF16)
                    dk += _dot_tn(dsb, qas[a])
                    dv += _dot_tn(p.astype(BF16), doas[a])
                    df_ref[a:a + 1, pl.ds(r0, KB)] -= jnp.sum(ds, axis=0, keepdims=True)
                    new.append(dqs[a] + _dot(dsb, kb))
                dk_ref[pl.ds(r0, KB), :] += dk
                dv_ref[pl.ds(r0, KB), :] += dv
                return tuple(new)

            dqs = lax.fori_loop(0, nkb, step, (zero, zero))
        else:
            incl = jnp.where(row_s <= lane_s, 1.0, 0.0).astype(BF16)
            excl = jnp.where(row_s < lane_s, 1.0, 0.0).astype(BF16)

            def step(j, carry, masked):
                r0, kb, vb = kv(j)
                strict = r0 + lane < qpos
                new = []
                dk = jnp.zeros((KB, LANES), F32)
                dv = jnp.zeros((KB, LANES), F32)
                for a in range(2):
                    dq, cm, cg = carry[3 * a:3 * a + 3]
                    ls, lm = _softplus_parts(_dot_nt(qas[a], kb))
                    if masked:
                        lm = jnp.where(strict, lm, 0.0)
                    beta = jnp.exp(ls)
                    upto, cm = _block_prefix_sums(lm, incl, cm)
                    aw = jnp.exp(ls + stas[a] - upto)
                    if masked:
                        aw = jnp.where(strict, aw, 0.0)
                    g = aw * _dot_nt(doas[a], vb)
                    pre, cg = _block_prefix_sums(g, excl, cg)
                    dz = g * (1.0 - beta) - pre * beta
                    if masked:
                        dz = jnp.where(strict, dz, 0.0)
                    dzb = dz.astype(BF16)
                    dk += _dot_tn(dzb, qas[a])
                    dv += _dot_tn(aw.astype(BF16), doas[a])
                    new += [dq + _dot(dzb, kb), cm, cg]
                dk_ref[pl.ds(r0, KB), :] += dk
                dv_ref[pl.ds(r0, KB), :] += dv
                return tuple(new)

            res = lax.fori_loop(0, nfull, functools.partial(step, masked=False), (zero, col0, col0, zero, col0, col0))
            res = lax.fori_loop(nfull, nkb, functools.partial(step, masked=True), res)
            dqs = (res[0], res[3])
        dq_ref[...] = (jnp.where(hms[0], dqs[0], dqs[1]) * 0.125).astype(BF16)

        @pl.when(i == nq - 1)
        def _():
            dk_out[...] = dk_ref[...].astype(BF16)
            dv_out[...] = dv_ref[...].astype(BF16)

    col = lambda off: (lambda p, i: (0, off + p))
    blk = pl.BlockSpec((BQ, LANES), lambda p, i: (i, p))
    acc = pl.BlockSpec((s, LANES), lambda p, i: (0, p))
    st_spec = pl.BlockSpec((2, BQ, LANES), lambda p, i: (p, i, 0))
    in_specs = [pl.BlockSpec((BQ, LANES), lambda p, i: (i, Q_OFF + p)), pl.BlockSpec((s, LANES), col(K_OFF)),
                pl.BlockSpec((s, LANES), col(V_OFF)), blk, st_spec]
    ops = [qkv, qkv, qkv, do, stat]
    w = npairs * LANES
    out_shape = [jax.ShapeDtypeStruct((s, w), BF16)] * 3
    out_specs = [blk, acc, acc]
    scratch = []
    if fox:
        fk_spec = pl.BlockSpec((None, 2, s), lambda p, i: (p, 0, 0))
        in_specs += [st_spec, fk_spec]
        ops += [fq, fk]
        out_shape.append(jax.ShapeDtypeStruct((npairs, 2, s), F32))
        out_specs.append(fk_spec)
        scratch = [pltpu.VMEM((2, s // KB, BQ, KB), F32)] * 2
    scratch = scratch + [pltpu.VMEM((s, LANES), F32)] * 2
    outs, rode = call_with_rider(body, name, rider, ops, in_specs, out_shape, out_specs, scratch, (npairs, nq))
    return (*outs, rode)


def forget_fwd(fl, bias, name):
    s = fl.shape[0]

    def body(fl_ref, b_ref, f_ref):
        row, lane = _iotas()
        lower = jnp.where(lane <= row, 1.0, 0.0).astype(BF16)

        def step(n, carry):
            r0 = pl.multiple_of(n * BLK, BLK)
            ls, _ = _softplus_parts(fl_ref[pl.ds(r0, BLK), :] + b_ref[...])
            blk = _dot_exact_rhs(lower, ls) + carry
            f_ref[pl.ds(r0, BLK), :] = blk
            return blk[BLK - 1:BLK, :]

        lax.fori_loop(0, s // BLK, step, jnp.zeros((1, LANES), F32))

    return pl.pallas_call(
        body, name=name, out_shape=jax.ShapeDtypeStruct((s, LANES), F32),
        in_specs=[pl.BlockSpec(memory_space=pltpu.VMEM)] * 2, out_specs=pl.BlockSpec(memory_space=pltpu.VMEM),
        compiler_params=_params(),
    )(fl, bias)


def forget_bwd(fl, bias, df, name):
    s = fl.shape[0]
    nb = s // BLK

    def body(fl_ref, b_ref, df_ref, o_ref, db_ref):
        row, lane = _iotas()
        upper = jnp.where(lane >= row, 1.0, 0.0).astype(BF16)

        def step(nn, carry):
            tail, db = carry
            r0 = pl.multiple_of((nb - 1 - nn) * BLK, BLK)
            dls = _dot_exact_rhs(upper, df_ref[pl.ds(r0, BLK), :]) + tail
            xv = fl_ref[pl.ds(r0, BLK), :] + b_ref[...]
            dfl = dls * (1.0 / (1.0 + jnp.exp(xv)))
            o_ref[pl.ds(r0, BLK), :] = dfl
            return dls[0:1, :], db + jnp.sum(dfl, axis=0, keepdims=True)

        _, db = lax.fori_loop(0, nb, step, (jnp.zeros((1, LANES), F32), jnp.zeros((1, LANES), F32)))
        db_ref[...] = db

    return pl.pallas_call(
        body, name=name, out_shape=(jax.ShapeDtypeStruct((s, LANES), F32), jax.ShapeDtypeStruct((1, LANES), F32)),
        in_specs=[pl.BlockSpec(memory_space=pltpu.VMEM)] * 3,
        out_specs=(pl.BlockSpec(memory_space=pltpu.VMEM), pl.BlockSpec(memory_space=pltpu.VMEM)),
        compiler_params=_params(),
    )(fl, bias, df)


def _rot_tables(s):
    inv = ROPE_THETA ** (-jnp.arange(ROT_HALF, dtype=F32) * 2.0 / (2 * ROT_HALF))
    ang = jnp.arange(s, dtype=F32)[:, None] * inv[None, :]
    cos, sin = jnp.cos(ang), jnp.sin(ang)
    z8 = jnp.zeros((s, ROT_HALF), F32)
    rest = HEAD_DIM - 2 * ROT_HALF
    zr, onr = jnp.zeros((s, rest), F32), jnp.ones((s, rest), F32)
    tile = lambda t: jnp.tile(t, (1, 2))
    return tile(jnp.concatenate([cos, cos, onr], 1)), tile(jnp.concatenate([-sin, z8, zr], 1)), tile(jnp.concatenate([z8, sin, zr], 1))


def _deinterleave(dst, src_ref, stride, s, dtype):
    length = s // stride
    for r in range(stride):
        if stride == 1:
            dst[...] = src_ref[...].astype(dtype)
        else:
            dst[r * length:(r + 1) * length, :] = src_ref[pl.ds(r, length, stride=stride), :].astype(dtype)


def _band_masks(row, lane, first):
    return lane <= row, lane >= row + jnp.where(first, BLK, 0)


N_DIL_PAIRS = 4


def _rotate_into(q_ref, k_ref, v_ref, c_ref, s1_ref, s2_ref, qr, kr, vr):
    c, s1, s2 = c_ref[...], s1_ref[...], s2_ref[...]
    rot = lambda xv: xv * c + pltpu.roll(xv, LANES - ROT_HALF, 1) * s1 + pltpu.roll(xv, ROT_HALF, 1) * s2
    qr[...] = rot(q_ref[...].astype(F32)) * 0.125
    kr[...] = rot(k_ref[...].astype(F32))
    vr[...] = v_ref[...].astype(F32)


def _dilated_operands(qkv, tables):
    s = qkv.shape[0]
    col = lambda off: pl.BlockSpec((s, LANES), lambda p: (0, off + N_DIL_PAIRS + p))
    table = pl.BlockSpec((s, LANES), lambda p: (0, 0))
    return [qkv, qkv, qkv, *tables], [col(Q_OFF), col(K_OFF), col(V_OFF), table, table, table]


def dilated_fwd(qkv, tables, name, rider=None):
    s = qkv.shape[0]
    npairs, w = N_DIL_PAIRS, N_DIL_PAIRS * LANES
    nblk = s // BLK

    def body(q_in, k_in, v_in, c_ref, s1_ref, s2_ref, o_ref, lse_ref, q_ref, k_ref, v_ref, qs, ks, vs, od, ld, on, ln):
        row, lane = _iotas()
        _rotate_into(q_in, k_in, v_in, c_ref, s1_ref, s2_ref, q_ref, k_ref, v_ref)
        for pi, stride in enumerate(DIL_STRIDES):
            per = (s // stride) // BLK
            _deinterleave(qs, q_ref, stride, s, BF16)
            _deinterleave(ks, k_ref, stride, s, BF16)
            _deinterleave(vs, v_ref, stride, s, BF16)

            def block(b, carry):
                r0 = pl.multiple_of(b * BLK, BLK)
                rp = pl.multiple_of(jnp.maximum(b - 1, 0) * BLK, BLK)
                mc, mp = _band_masks(row, lane, b % per == 0)
                q = qs[pl.ds(r0, BLK), :]
                kc, kp, vc, vp = ks[pl.ds(r0, BLK), :], ks[pl.ds(rp, BLK), :], vs[pl.ds(r0, BLK), :], vs[pl.ds(rp, BLK), :]
                out = jnp.zeros((BLK, LANES), F32)
                lse = jnp.zeros((BLK, LANES), F32)
                for a in range(2):
                    hm = (lane < HEAD_DIM) if a == 0 else (lane >= HEAD_DIM)
                    qa = jnp.where(hm, q.astype(F32), 0.0).astype(BF16)
                    sc = jnp.where(mc, _dot_nt(qa, kc), NEG)
                    sp = jnp.where(mp, _dot_nt(qa, kp), NEG)
                    mx = jnp.maximum(jnp.max(sc, axis=1, keepdims=True), jnp.max(sp, axis=1, keepdims=True))
                    pc, pp = jnp.exp(sc - mx), jnp.exp(sp - mx)
                    l = jnp.sum(pc, axis=1, keepdims=True) + jnp.sum(pp, axis=1, keepdims=True)
                    oa = (_dot(pc.astype(BF16), vc) + _dot(pp.astype(BF16), vp)) / l
                    out = jnp.where(hm, oa, out)
                    lse = jnp.where(hm, mx + jnp.log(l), lse)
                od[pl.ds(r0, BLK), :] = out
                ld[pl.ds(r0, BLK), :] = lse
                return carry

            lax.fori_loop(0, nblk, block, 0, unroll=2)
            length = s // stride
            for r in range(stride):
                if stride == 1:
                    on[pi] = od[...]
                    ln[pi] = ld[...]
                else:
                    on[pi, pl.ds(r, length, stride=stride), :] = od[r * length:(r + 1) * length, :]
                    ln[pi, pl.ds(r, length, stride=stride), :] = ld[r * length:(r + 1) * length, :]

        def merge(n, carry):
            r0 = pl.multiple_of(n * BLK, BLK)
            ls = [ln[pi, pl.ds(r0, BLK), :] for pi in range(3)]
            mx = jnp.maximum(jnp.maximum(ls[0], ls[1]), ls[2])
            ws = [jnp.exp(lv - mx) for lv in ls]
            den = ws[0] + ws[1] + ws[2]
            num = ws[0] * on[0, pl.ds(r0, BLK), :] + ws[1] * on[1, pl.ds(r0, BLK), :] + ws[2] * on[2, pl.ds(r0, BLK), :]
            o_ref[pl.ds(r0, BLK), :] = num / den
            lse_ref[pl.ds(r0, BLK), :] = mx + jnp.log(den)
            return carry

        lax.fori_loop(0, nblk, merge, 0, unroll=2)

    colspec = pl.BlockSpec((s, LANES), lambda p: (0, p))
    out = jax.ShapeDtypeStruct((s, w), F32)
    scratch = ([pltpu.VMEM((s, LANES), F32)] * 3 + [pltpu.VMEM((s, LANES), BF16)] * 3 + [pltpu.VMEM((s, LANES), F32)] * 2
               + [pltpu.VMEM((3, s, LANES), F32)] * 2)
    ops, in_specs = _dilated_operands(qkv, tables)
    (o, lse), rode = call_with_rider(body, name, rider, ops, in_specs, [out, out], [colspec, colspec], scratch, (npairs,))
    return o, lse, rode


def dilated_bwd(qkv, tables, do, out, lse, do_off, name, rider=None):
    s = qkv.shape[0]
    npairs, w = N_DIL_PAIRS, N_DIL_PAIRS * LANES
    nblk = s // BLK

    def body(q_in, k_in, v_in, c_ref, s1_ref, s2_ref, do_ref, out_ref, lse_ref, dq_out, dk_out, dv_out,
             q_ref, k_ref, v_ref, dq_ref, dk_ref, dv_ref, qs, ks, vs, dos, dls, lss, dqd, dkd, dvd, dln):
        row, lane = _iotas()
        same_head = jnp.where((row < HEAD_DIM) == (lane < HEAD_DIM), 1.0, 0.0).astype(BF16)
        _rotate_into(q_in, k_in, v_in, c_ref, s1_ref, s2_ref, q_ref, k_ref, v_ref)

        def delta_blk(n, carry):
            r0 = pl.multiple_of(n * BLK, BLK)
            dln[pl.ds(r0, BLK), :] = _dot_exact_lhs(do_ref[pl.ds(r0, BLK), :] * out_ref[pl.ds(r0, BLK), :], same_head)
            return carry

        lax.fori_loop(0, nblk, delta_blk, 0, unroll=2)
        for pi, stride in enumerate(DIL_STRIDES):
            per = (s // stride) // BLK
            _deinterleave(qs, q_ref, stride, s, BF16)
            _deinterleave(ks, k_ref, stride, s, BF16)
            _deinterleave(vs, v_ref, stride, s, BF16)
            _deinterleave(dos, do_ref, stride, s, BF16)
            _deinterleave(dls, dln, stride, s, F32)
            _deinterleave(lss, lse_ref, stride, s, F32)

            def block(b, carry):
                r0 = pl.multiple_of(b * BLK, BLK)
                rp = pl.multiple_of(jnp.maximum(b - 1, 0) * BLK, BLK)
                first = b % per == 0
                mc, mp = _band_masks(row, lane, first)
                q, dov = qs[pl.ds(r0, BLK), :], dos[pl.ds(r0, BLK), :]
                kc, kp, vc, vp = ks[pl.ds(r0, BLK), :], ks[pl.ds(rp, BLK), :], vs[pl.ds(r0, BLK), :], vs[pl.ds(rp, BLK), :]
                lse_t, dl_t = lss[pl.ds(r0, BLK), :], dls[pl.ds(r0, BLK), :]
                dq = jnp.zeros((BLK, LANES), F32)
                dkc = jnp.zeros((BLK, LANES), F32)
                dkp = jnp.zeros((BLK, LANES), F32)
                dvc = jnp.zeros((BLK, LANES), F32)
                dvp = jnp.zeros((BLK, LANES), F32)
                for a in range(2):
                    hm = (lane < HEAD_DIM) if a == 0 else (lane >= HEAD_DIM)
                    pick = lane == a * HEAD_DIM
                    qa = jnp.where(hm, q.astype(F32), 0.0).astype(BF16)
                    doa = jnp.where(hm, dov.astype(F32), 0.0).astype(BF16)
                    lse_a = jnp.sum(jnp.where(pick, lse_t, 0.0), axis=1, keepdims=True)
                    dl_a = jnp.sum(jnp.where(pick, dl_t, 0.0), axis=1, keepdims=True)
                    pc = jnp.where(mc, jnp.exp(_dot_nt(qa, kc) - lse_a), 0.0)
                    pp = jnp.where(mp, jnp.exp(_dot_nt(qa, kp) - lse_a), 0.0)
                    dsc = (pc * (_dot_nt(doa, vc) - dl_a)).astype(BF16)
                    dsp = (pp * (_dot_nt(doa, vp) - dl_a)).astype(BF16)
                    dq = jnp.where(hm, _dot(dsc, kc) + _dot(dsp, kp), dq)
                    dkc += _dot_tn(dsc, qa)
                    dkp += _dot_tn(dsp, qa)
                    dvc += _dot_tn(pc.astype(BF16), doa)
                    dvp += _dot_tn(pp.astype(BF16), doa)
                dqd[pl.ds(r0, BLK), :] = dq
                dkd[pl.ds(r0, BLK), :] = dkc
                dvd[pl.ds(r0, BLK), :] = dvc

                @pl.when(jnp.logical_not(first))
                def _():
                    dkd[pl.ds(rp, BLK), :] += dkp
                    dvd[pl.ds(rp, BLK), :] += dvp

                return carry

            lax.fori_loop(0, nblk, block, 0, unroll=2)
            length = s // stride
            for dst, src in ((dq_ref, dqd), (dk_ref, dkd), (dv_ref, dvd)):
                for r in range(stride):
                    if stride == 1:
                        dst[...] = src[...]
                    else:
                        dst[pl.ds(r, length, stride=stride), :] += src[r * length:(r + 1) * length, :]

        c, s1, s2 = c_ref[...], s1_ref[...], s2_ref[...]
        rot_t = lambda dy: dy * c + pltpu.roll(dy * s1, ROT_HALF, 1) + pltpu.roll(dy * s2, LANES - ROT_HALF, 1)
        dq_out[...] = (rot_t(dq_ref[...]) * 0.125).astype(BF16)
        dk_out[...] = rot_t(dk_ref[...]).astype(BF16)
        dv_out[...] = dv_ref[...].astype(BF16)

    colspec = pl.BlockSpec((s, LANES), lambda p: (0, p))
    do_spec = pl.BlockSpec((s, LANES), lambda p: (0, do_off + p))
    o3 = jax.ShapeDtypeStruct((s, w), BF16)
    scratch = [pltpu.VMEM((s, LANES), F32)] * 6 + [pltpu.VMEM((s, LANES), BF16)] * 4 + [pltpu.VMEM((s, LANES), F32)] * 6
    ops, in_specs = _dilated_operands(qkv, tables)
    outs, rode = call_with_rider(body, name, rider, ops + [do, out, lse], in_specs + [do_spec, colspec, colspec],
                                 [o3, o3, o3], [colspec, colspec, colspec], scratch, (npairs,))
    return (*outs, rode)


def adamw(w, g, m, v, name):
    rows, cols = w.shape
    rb = min(rows, ROWS)
    c1 = 1.0 - ADAM_B1 ** ADAM_STEP
    c2 = 1.0 - ADAM_B2 ** ADAM_STEP

    def body(w_ref, g_ref, m_ref, v_ref, d_ref, mo_ref, vo_ref):
        gv = g_ref[...]
        mn = ADAM_B1 * m_ref[...] + (1.0 - ADAM_B1) * gv
        vn = ADAM_B2 * v_ref[...] + (1.0 - ADAM_B2) * (gv * gv)
        d_ref[...] = -ADAM_LR * ((mn / c1) / (jnp.sqrt(vn / c2) + ADAM_EPS) + ADAM_WD * w_ref[...])
        mo_ref[...] = mn
        vo_ref[...] = vn

    spec = _row_spec(cols, rb)
    out = jax.ShapeDtypeStruct((rows, cols), F32)
    return pl.pallas_call(
        body, name=name, out_shape=(out, out, out), grid=(rows // rb,), in_specs=[spec] * 4, out_specs=(spec,) * 3,
        compiler_params=_params(("parallel",)),
    )(w, g, m, v)


def _prefetch_call(body, name, scalar, ops, grid, in_specs, out_specs, out_shape, sem):
    spec = pltpu.PrefetchScalarGridSpec(num_scalar_prefetch=1, grid=grid, in_specs=in_specs, out_specs=out_specs)
    return pl.pallas_call(body, name=name, grid_spec=spec, out_shape=out_shape, compiler_params=_params(sem))(scalar, *ops)


def pair_sums(gs, gots, core, name):
    n = len(gs)

    def body(core_ref, *refs):
        for f in range(n):
            refs[2 * n + f][...] = (refs[f][...].astype(F32) + refs[n + f][...].astype(F32)).astype(BF16)

    blk = lambda g, rows_of: pl.BlockSpec((None, g.shape[1] // 2, g.shape[2]), rows_of)
    mine = [blk(g, lambda j, core_ref: (j, core_ref[0], 0)) for g in gs]
    half = [blk(g, lambda j, core_ref: (j, 0, 0)) for g in gs]
    outs = tuple(jax.ShapeDtypeStruct((g.shape[0], g.shape[1] // 2, g.shape[2]), BF16) for g in gs)
    return _prefetch_call(body, name, core, (*gs, *gots), (N_CHIPS,), mine + half, tuple(half), outs, ("parallel",))


def chip_sums(pairs, gots, chip, layers, intos, name):
    n = len(pairs)

    def body(chip_ref, *refs):
        for f in range(n):
            p_ref, a_ref, b_ref, c_ref = refs[4 * f:4 * f + 4]
            refs[5 * n + f][...] = ((p_ref[...].astype(F32) + a_ref[...].astype(F32)) + b_ref[...].astype(F32)) + c_ref[...].astype(F32)

    in_specs, ops = [], []
    for p, got in zip(pairs, gots):
        blk = lambda at, p=p: pl.BlockSpec((None,) + p.shape[1:], at)
        in_specs += [blk(lambda i, chip_ref: (chip_ref[0], 0, 0))] + [blk(lambda i, chip_ref, k=k: (k, 0, 0)) for k in range(3)]
        ops += [p, got, got, got]
    out_specs = tuple(pl.BlockSpec((None,) + p.shape[1:], lambda i, chip_ref, l=l: (l, 0, 0)) for p, l in zip(pairs, layers))
    spec = pltpu.PrefetchScalarGridSpec(num_scalar_prefetch=1, grid=(1,), in_specs=in_specs + [_ANY] * n, out_specs=out_specs)
    return pl.pallas_call(
        body, name=name, grid_spec=spec, out_shape=tuple(jax.ShapeDtypeStruct(t.shape, t.dtype) for t in intos),
        input_output_aliases={1 + 4 * n + f: f for f in range(n)}, compiler_params=_params(("arbitrary",)),
    )(chip, *ops, *intos)


def adamw_family(w, m, v, g_mine, g_other, core, name):
    nl, r, c = w.shape
    gc = g_mine.shape[2]
    rh = r // 2
    nb = 4 if rh % 512 == 0 else (2 if rh % 16 == 0 and rh > 256 else 1)
    rb = rh // nb
    c1 = 1.0 - ADAM_B1 ** ADAM_STEP
    c2 = 1.0 - ADAM_B2 ** ADAM_STEP

    def body(core_ref, w_ref, m_ref, v_ref, gm_ref, go_ref, g_ref, d_ref, mo_ref, vo_ref):
        gv = jnp.where(pl.program_id(1) == core_ref[0], gm_ref[...], go_ref[...])[:, :c]
        mn = ADAM_B1 * m_ref[...] + (1.0 - ADAM_B1) * gv
        vn = ADAM_B2 * v_ref[...] + (1.0 - ADAM_B2) * (gv * gv)
        g_ref[...] = gv
        d_ref[...] = -ADAM_LR * ((mn / c1) / (jnp.sqrt(vn / c2) + ADAM_EPS) + ADAM_WD * w_ref[...])
        mo_ref[...] = mn
        vo_ref[...] = vn

    full = pl.BlockSpec((None, rb, c), lambda l, h, i, core_ref: (l, h * nb + i, 0))
    mine = pl.BlockSpec((None, rb, gc), lambda l, h, i, core_ref: (l, jnp.where(h == core_ref[0], i, 0), 0))
    other = pl.BlockSpec((None, rb, gc), lambda l, h, i, core_ref: (l, jnp.where(h == core_ref[0], 0, i), 0))
    out = jax.ShapeDtypeStruct((nl, r, c), F32)
    return _prefetch_call(body, name, core, (w, m, v, g_mine, g_other), (nl, 2, nb), [full, full, full, mine, other],
                          (full, full, full, full), (out, out, out, out), ("parallel", "parallel", "parallel"))


def _coords():
    return lax.axis_index("x"), lax.axis_index("y"), lax.axis_index("c")


def _other_chips(x, y):
    return ((1 - x, 1 - y), (1 - x, y), (x, 1 - y))


_ANY = pl.BlockSpec(memory_space=pl.ANY)


def _exchange_call(body, name, arrays, out_shapes, n_copies, n_local=0):
    n = len(arrays)

    def wrapped(*refs):
        body(refs[:n], refs[n:n + len(out_shapes)], *refs[n + len(out_shapes):])

    scratch = [pltpu.SemaphoreType.DMA((n_copies,)), pltpu.SemaphoreType.DMA((n_copies,))]
    if n_local:
        scratch.append(pltpu.SemaphoreType.DMA((n_local,)))
    return pl.pallas_call(
        wrapped, name=name, out_shape=tuple(out_shapes), in_specs=[_ANY] * n, out_specs=tuple([_ANY] * len(out_shapes)),
        scratch_shapes=scratch, compiler_params=_params(),
    )(*arrays)


def _remote(send_sems, recv_sems, n, src, dst, to):
    return pltpu.make_async_remote_copy(src_ref=src, dst_ref=dst, send_sem=send_sems.at[n], recv_sem=recv_sems.at[n],
                                        device_id=to, device_id_type=MESH)


class Rider:
    def __init__(self, arrays, out_shapes, n_remote, n_local, copies, then=None):
        self.arrays, self.out_shapes, self.n_remote, self.n_local = list(arrays), list(out_shapes), n_remote, n_local
        self.copies, self.then = copies, then

    def sems(self):
        return [pltpu.SemaphoreType.DMA((self.n_remote,)), pltpu.SemaphoreType.DMA((self.n_remote,)),
                pltpu.SemaphoreType.DMA((max(self.n_local, 1),))]

    def run(self, name):
        n, no = len(self.arrays), len(self.out_shapes)

        def body(*refs):
            for stage in (self.copies, self.then):
                if stage is not None:
                    cps = stage(refs[:n], refs[n:n + no], *refs[n + no:])
                    for cp in cps:
                        cp.start()
                    for cp in cps:
                        cp.wait()

        return pl.pallas_call(
            body, name=name, out_shape=tuple(self.out_shapes), in_specs=[_ANY] * n, out_specs=tuple([_ANY] * no),
            scratch_shapes=self.sems(), compiler_params=_params(),
        )(*self.arrays)


def ride(rider, body, n_in, n_out, grid):
    if rider is None:
        return body
    ni, no = len(rider.arrays), len(rider.out_shapes)

    def wrapped(*refs):
        ins, r_in = refs[:n_in], refs[n_in:n_in + ni]
        outs = refs[n_in + ni:n_in + ni + n_out]
        r_out = refs[n_in + ni + n_out:n_in + ni + n_out + no]
        rest = refs[n_in + ni + n_out + no:]
        scratch, sems = rest[:len(rest) - 3], rest[len(rest) - 3:]
        step, total = 0, 1
        for a, g in enumerate(grid):
            step, total = step * g + pl.program_id(a), total * g
        assert total >= 3
        relay_at = (7 * total) // 8 if rider.then is not None else total - 1

        @pl.when(step == 0)
        def _():
            for cp in rider.copies(r_in, r_out, *sems):
                cp.start()

        body(*ins, *outs, *scratch)

        @pl.when(step == relay_at)
        def _():
            for cp in rider.copies(r_in, r_out, *sems):
                cp.wait()
            if rider.then is not None:
                for cp in rider.then(r_in, r_out, *sems):
                    cp.start()

        if rider.then is not None:
            @pl.when(step == total - 1)
            def _():
                for cp in rider.then(r_in, r_out, *sems):
                    cp.wait()

    return wrapped


def call_with_rider(body, name, rider, ops, in_specs, out_shape, out_specs, scratch, grid):
    n_in, n_out = len(ops), len(out_shape)
    ops, in_specs, out_shape, out_specs, scratch = list(ops), list(in_specs), list(out_shape), list(out_specs), list(scratch)
    if rider is not None:
        ops += rider.arrays
        in_specs += [_ANY] * len(rider.arrays)
        out_shape += rider.out_shapes
        out_specs += [_ANY] * len(rider.out_shapes)
        scratch += rider.sems()
    res = pl.pallas_call(
        ride(rider, body, n_in, n_out, grid), name=name, out_shape=tuple(out_shape), grid=grid, in_specs=in_specs,
        out_specs=tuple(out_specs), scratch_shapes=scratch, compiler_params=_params(("arbitrary",) * len(grid)),
    )(*ops)
    return tuple(res[:n_out]), list(res[n_out:])


def gather_rider(shards):
    nf = len(shards)
    half = lambda ref, which: pl.ds(which * (ref.shape[-2] // 2), ref.shape[-2] // 2)

    def copies(s_refs, o_refs, send_sems, recv_sems, local_sems):
        x, y, c = _coords()
        me = 2 * x + y
        cps = [pltpu.make_async_copy(s_refs[f], o_refs[f].at[me], local_sems.at[f]) for f in range(nf)]
        for k, (px, py) in enumerate(_other_chips(x, y)):
            for f in range(nf):
                rows = half(s_refs[f], c)
                cps.append(_remote(send_sems, recv_sems, k * nf + f, s_refs[f].at[rows], o_refs[f].at[me, rows], (px, py, c)))
        return cps

    def relay(s_refs, o_refs, send_sems, recv_sems, local_sems):
        x, y, c = _coords()
        cps = []
        for k, (px, py) in enumerate(_other_chips(x, y)):
            for f in range(nf):
                landed = o_refs[f].at[2 * px + py, half(s_refs[f], c)]
                cps.append(_remote(send_sems, recv_sems, (3 + k) * nf + f, landed, landed, (x, y, 1 - c)))
        return cps

    return Rider(shards, [jax.ShapeDtypeStruct((N_CHIPS,) + sh.shape, sh.dtype) for sh in shards], 6 * nf, nf, copies, relay)


def scatter_rider(pairs):
    nf = len(pairs)

    def copies(p_refs, o_refs, send_sems, recv_sems, local_sems):
        x, y, c = _coords()
        cps = []
        for k, (px, py) in enumerate(_other_chips(x, y)):
            for f in range(nf):
                cps.append(_remote(send_sems, recv_sems, k * nf + f, p_refs[f].at[2 * px + py], o_refs[f].at[k], (px, py, c)))
        return cps

    return Rider(pairs, [jax.ShapeDtypeStruct((3,) + p.shape[1:], p.dtype) for p in pairs], 3 * nf, 0, copies)


def pair_swap(grads, name):
    def body(g_refs, o_refs, send_sems, recv_sems):
        x, y, c = _coords()
        cps = []
        for f, g_ref in enumerate(g_refs):
            rh = g_ref.shape[1] // 2
            cps.append(_remote(send_sems, recv_sems, f, g_ref.at[:, pl.ds((1 - c) * rh, rh), :], o_refs[f], (x, y, 1 - c)))
        for cp in cps:
            cp.start()
        for cp in cps:
            cp.wait()

    outs = [jax.ShapeDtypeStruct((g.shape[0], g.shape[1] // 2, g.shape[2]), g.dtype) for g in grads]
    return _exchange_call(body, name, grads, outs, len(grads))


def half_swap(halves, name):
    def body(h_refs, o_refs, send_sems, recv_sems):
        x, y, c = _coords()
        cps = [_remote(send_sems, recv_sems, f, h_ref, o_refs[f], (x, y, 1 - c)) for f, h_ref in enumerate(h_refs)]
        for cp in cps:
            cp.start()
        for cp in cps:
            cp.wait()

    return _exchange_call(body, name, halves, [jax.ShapeDtypeStruct(h.shape, h.dtype) for h in halves], len(halves))


def allsum_small(part, name):
    def body(p_ref, tot_ref, all_ref, send_sems, recv_sems):
        x, y, c = _coords()
        me, sibling = (x, y, c), (x, y, 1 - c)
        chips = _other_chips(x, y)

        def slot(px, py, pc):
            return all_ref.at[4 * px + 2 * py + pc]

        def copy(k, block, to, src=None):
            return pltpu.make_async_remote_copy(src_ref=slot(*block) if src is None else src, dst_ref=slot(*block),
                                                send_sem=send_sems.at[k], recv_sem=recv_sems.at[k], device_id=to, device_id_type=MESH)

        slot(*me)[...] = p_ref[...]
        first = [copy(0, me, sibling, src=p_ref)] + [copy(1 + j, me, (*chip, c), src=p_ref) for j, chip in enumerate(chips)]
        for cp in first:
            cp.start()
        passed = [copy(4 + j, (*chip, c), sibling) for j, chip in enumerate(chips)]
        for j, chip in enumerate(chips):
            copy(1 + j, (*chip, c), me).wait_recv()
            passed[j].start()
        copy(0, sibling, me).wait_recv()
        for j, chip in enumerate(chips):
            copy(4 + j, (*chip, 1 - c), me).wait_recv()
        for cp in first + passed:
            cp.wait_send()
        tot = all_ref[0]
        for d in range(1, 8):
            tot = tot + all_ref[d]
        tot_ref[...] = tot

    vm = pl.BlockSpec(memory_space=pltpu.VMEM)
    return pl.pallas_call(
        body, name=name, out_shape=jax.ShapeDtypeStruct(part.shape, F32), in_specs=[vm], out_specs=vm,
        scratch_shapes=[pltpu.VMEM((8,) + part.shape, F32), pltpu.SemaphoreType.DMA((7,)), pltpu.SemaphoreType.DMA((7,))],
        compiler_params=_params(),
    )(part)


QKVF_COLS = 772
QKVF_PAD = 896
FORWARD_CARRY = {0: (("fi0",), ("qkv1", "o1", "fo1"), ("fo0",)), 1: (("fi1", "qkv2", "o2"), (), ()),
                 2: (("fi2",), ("qkv3", "o3", "fo3"), ("fo2",)), 3: (("fi3",), (), ())}


def _tables_for(s):
    return _rot_tables(s)


def layer_families(layer):
    return (0, 1, layer // 2) if layer % 2 == 0 else (2, 3, layer // 2)


class GradientExchange:
    def __init__(self, family_layers):
        self.core = lax.axis_index("c").astype(jnp.int32).reshape(1)
        self.chip = (2 * lax.axis_index("x") + lax.axis_index("y")).astype(jnp.int32).reshape(1)
        self.family_layers = family_layers
        self.pairs, self.mine, self.pending = {}, {}, []

    def add(self, items, tag):
        gs = [g for _, _, g in items]
        sums = pair_sums(gs, pair_swap(gs, f"grad_pair_swap_{tag}"), self.core, f"grad_pair_sum_{tag}")
        for (fam, li, _), pair in zip(items, sums):
            self.pairs[(fam, li)] = pair
            self.pending.append((fam, li))

    def rider(self, only=None):
        keys = [k for k in self.pending if only is None or k in only]
        self.pending = [k for k in self.pending if k not in keys]
        return (scatter_rider([self.pairs[k] for k in keys]) if keys else None), keys

    def landed(self, keys, outs):
        batch = []
        for k, o in list(zip(keys, outs)) + [(None, None)]:
            if batch and (k is None or k[0] in [b[0][0] for b in batch]):
                ks = [b[0] for b in batch]
                pairs = [self.pairs[b] for b in ks]
                intos = [self.mine[fam] if fam in self.mine else jnp.zeros((self.family_layers[fam],) + p.shape[1:], F32)
                         for (fam, _), p in zip(ks, pairs)]
                sums = chip_sums(pairs, [b[1] for b in batch], self.chip, [li for _, li in ks], intos,
                                 "grad_chip_sum_" + "_".join(f"{f}{li}" for f, li in ks))
                self.mine.update({fam: t for (fam, _), t in zip(ks, sums)})
                batch = []
            if k is not None:
                batch.append((k, o))

    def finish(self, weights, moments1, moments2):
        last, keys = self.rider()
        if last is not None:
            self.landed(keys, last.run("grad_chip_scatter_last"))
        mine = [self.mine[fam] for fam in range(len(weights))]
        other = half_swap(mine, "grad_half_swap")
        return [adamw_family(w, m, v, gm, go, self.core, f"adamw_{f}")
                for f, (w, m, v, gm, go) in enumerate(zip(weights, moments1, moments2, mine, other))]


class KeepGradients:
    def __init__(self):
        self.grads = {}

    def add(self, items, tag):
        for fam, li, g in items:
            self.grads[(fam, li)] = g

    def rider(self, only=None):
        return None, []

    def landed(self, keys, outs):
        pass


def kernel(x, norm_mix, w_qkv_even, w_o_even, w_qkvf_odd, b_forget, w_o_odd, norm_ffn, w_ffn_in, w_ffn_out, norm_final, loss_target, m_norm_mix, m_w_qkv_even, m_w_o_even, m_w_qkvf_odd, m_b_forget, m_w_o_odd, m_norm_ffn, m_w_ffn_in, m_w_ffn_out, m_norm_final, v_norm_mix, v_w_qkv_even, v_w_o_even, v_w_qkvf_odd, v_b_forget, v_w_o_odd, v_norm_ffn, v_w_ffn_in, v_w_ffn_out, v_norm_final):
    w_shards = [w_qkv_even, w_o_even, w_qkvf_odd, w_o_odd, w_ffn_in, w_ffn_out]
    shards = [w.astype(BF16) for w in w_shards]
    tables = _tables_for(x.shape[1])
    bias_pad = jnp.pad(b_forget, ((0, 0), (0, LANES - N_HEADS)))

    mine = {}
    for layer in range(DEPTH):
        fam_qkv, fam_o, li = layer_families(layer)
        mine.update({f"qkv{layer}": shards[fam_qkv][li], f"o{layer}": shards[fam_o][li],
                     f"fi{layer}": shards[4][layer], f"fo{layer}": shards[5][layer]})
    fetch = lambda names: gather_rider([mine[n] for n in names])
    have = dict(zip(("qkv0", "o0"), fetch(("qkv0", "o0")).run("gather_first")))
    saved, cur = [], x[0]
    h1 = rmsnorm_fwd(cur, norm_mix[0:1], "l0_norm_mix")
    for layer in range(DEPTH):
        next_gain = norm_mix[layer + 1:layer + 2] if layer + 1 < DEPTH else None
        cur, h1, keep = forward_layer(layer, cur, h1, have, norm_ffn[layer:layer + 1], next_gain, tables,
                                      bias_pad[layer // 2:layer // 2 + 1], fetch, *FORWARD_CARRY[layer])
        saved.append(keep)

    dcur, g_final, loss_part = loss_head(cur, norm_final.reshape(1, D_MODEL), loss_target[0], "loss_head")

    exchange = GradientExchange([w.shape[0] for w in w_shards])
    g_mix, g_ffn, g_bias = [None] * DEPTH, [None] * DEPTH, [None] * (DEPTH // 2)
    for layer in reversed(range(DEPTH)):
        dcur, g_mix[layer], g_ffn[layer], g_b = backward_layer(layer, dcur, saved[layer], norm_mix[layer:layer + 1],
                                                               norm_ffn[layer:layer + 1], tables, bias_pad[layer // 2:layer // 2 + 1], exchange)
        if g_b is not None:
            g_bias[layer // 2] = g_b

    zero_row = jnp.zeros((1, D_MODEL), F32)
    pad16 = lambda v: jnp.pad(v, (0, D_MODEL - v.shape[0]))[None, :]
    small_rows = lambda mix, ffn, fin, bias, last: jnp.concatenate(
        [r.reshape(1, D_MODEL) for r in mix] + [r.reshape(1, D_MODEL) for r in ffn] + [fin.reshape(1, D_MODEL)]
        + [pad16(b) for b in bias] + [last] + [zero_row] * (SMALL_ROWS - 12), axis=0)
    loss_row = pad16(loss_part[0, :1])
    small_g = allsum_small(small_rows(g_mix, g_ffn, g_final, g_bias, loss_row), "allsum_small")
    loss = small_g[11, 0]
    small_g = small_g.at[11].set(0.0)
    sw = small_rows(list(norm_mix), list(norm_ffn), norm_final, list(b_forget), zero_row)
    sm = small_rows(list(m_norm_mix), list(m_norm_ffn), m_norm_final, list(m_b_forget), zero_row)
    sv = small_rows(list(v_norm_mix), list(v_norm_ffn), v_norm_final, list(v_b_forget), zero_row)
    sd, snm, snv = adamw(sw, small_g, sm, sv, "adamw_small")

    def small_out(a):
        return a[0:4], a[8, :], a[9:11, :N_HEADS], a[4:8]

    big = exchange.finish(w_shards, [m_w_qkv_even, m_w_o_even, m_w_qkvf_odd, m_w_o_odd, m_w_ffn_in, m_w_ffn_out],
                          [v_w_qkv_even, v_w_o_even, v_w_qkvf_odd, v_w_o_odd, v_w_ffn_in, v_w_ffn_out])

    def outputs(small, which):
        mix, fin, bias, ffn = small_out(small)
        qkv_e, o_e, qkvf, o_o, fi, fo = [big[f][which] for f in range(6)]
        return [mix, qkv_e, o_e, qkvf, bias, o_o, ffn, fi, fo, fin]

    return (loss, dcur[None], *outputs(small_g, 0), *outputs(sd, 1), *outputs(snm, 2), *outputs(snv, 3))


def _chip_tile(rows, cols, at):
    return pl.BlockSpec((None, rows, cols), at)


def forward_layer(layer, cur, h1, have, ffn_gain, next_gain, tables, bias_row, fetch=None, carry=(), side_carry=(), ffn_carry=()):
    n = f"l{layer}"
    s = cur.shape[0]
    w_qkv, w_o = have[f"qkv{layer}"], have[f"o{layer}"]
    rider = fetch(carry) if carry else None
    side_rider = fetch(side_carry) if side_carry else None
    keep = {"x": cur, "h1": h1, "w_o": w_o.reshape(D_ATTN, D_MODEL)}
    side = []
    if layer % 2 == 0:
        qkv = matmul(h1, w_qkv, "nn", BF16, n + "_qkv", 1024, 768, 1024, mnk=(s, 3 * D_ATTN, D_MODEL),
                     b_spec=_chip_tile(D_MODEL, 768, lambda i, j, kk: (j, 0, 0)))
        o_sb, st, rode = causal_fwd(qkv, 4, "sb", n + "_sb_fwd", rider=rider)
        o_dil, lse_dil, side = dilated_fwd(qkv, tables, n + "_dil_fwd", rider=side_rider)
        attn = jnp.concatenate([o_sb, o_dil], axis=1).astype(BF16)
        keep.update(o_dil=o_dil, lse_dil=lse_dil, w_qkv=w_qkv)
    else:
        natural = jnp.transpose(w_qkv, (1, 0, 2)).reshape(D_MODEL, N_CHIPS * QKVF_COLS)
        w_gate = jnp.pad(natural[:, 3 * D_ATTN:], ((0, 0), (0, LANES - N_HEADS)))
        qkv = matmul(h1, natural[:, :3 * D_ATTN], "nn", BF16, n + "_qkv", 1024, 768, 1024)
        fl = matmul(h1, w_gate, "nn", F32, n + "_fgate", 512, LANES, 1024)
        cum = forget_fwd(fl, bias_row, n + "_forget_fwd")
        f_heads = cum[:, :N_HEADS].T
        fq = jnp.broadcast_to(f_heads[:, :, None], (N_HEADS, s, LANES))
        fk = f_heads.reshape(N_HEADS // 2, 2, s)
        attn, st, rode = causal_fwd(qkv, 8, "fox", n + "_fox_fwd", fq=fq, fk=fk, rider=rider)
        attn = attn.astype(BF16)
        keep.update(fl=fl, fq=fq, fk=fk, w_qkv=jnp.concatenate([natural[:, :3 * D_ATTN], w_gate], axis=1))
    have.update(zip(carry, rode))
    have.update(zip(side_carry, side))
    w_fi = have[f"fi{layer}"]
    mid, h2 = matmul(attn, keep["w_o"], "nn", F32, n + "_attn_out", 512, 1024, 1024, res=cur, norm_gain=ffn_gain)
    gate, up, act, rode = ffn_in_swiglu(h2, w_fi, n + "_ffn_in", rider=fetch(ffn_carry) if ffn_carry else None)
    have.update(zip(ffn_carry, rode))
    w_fo = have[f"fo{layer}"].reshape(D_FF, D_MODEL)
    if next_gain is None:
        out, h_next = matmul(act, w_fo, "nn", F32, n + "_ffn_out", 512, 1024, D_FF, res=mid), None
    else:
        out, h_next = matmul(act, w_fo, "nn", F32, n + "_ffn_out", 512, 1024, D_FF, res=mid, norm_gain=next_gain)
    keep.update(qkv=qkv, st=st, attn=attn, mid=mid, h2=h2, gate=gate, up=up, act=act, w_fi=w_fi, w_fo=w_fo)
    return out, h_next, keep


def backward_layer(layer, dcur, kp, mix_gain, ffn_gain, tables, bias_row, exchange):
    n = f"l{layer}"
    s = dcur.shape[0]
    fam_qkv, fam_o, li = layer_families(layer)
    g_fo = matmul(kp["act"], dcur, "tn", BF16, n + "_d_w_ffn_out", 1408, 1024, s)
    dgu = swiglu_bwd(dcur, kp["w_fo"], kp["gate"], kp["up"], n + "_d_swiglu")
    g_fi = matmul(kp["h2"], dgu, "tn", BF16, n + "_d_w_ffn_in", 1024, 1408, 2048, mnk=(D_MODEL, 2 * D_FF, s),
                  o_spec=_chip_tile(D_MODEL, 1408, lambda i, j, kk: (j, 0, 0)), out_shape=(N_CHIPS, D_MODEL, 1408))
    dmid, g_ffn = dh_norm_bwd(dgu, kp["w_fi"], kp["mid"], ffn_gain, dcur, n + "_d_h2", 256)
    g_o = matmul(kp["attn"], dmid, "tn", BF16, n + "_d_w_o", 1024, 1024, s)
    dattn = matmul(dmid, kp["w_o"], "nt", F32, n + "_d_attn", 1024, 1024, 1024)
    exchange.add([(5, layer, g_fo.reshape(N_CHIPS, D_FF // N_CHIPS, D_MODEL)), (4, layer, g_fi),
                  (fam_o, li, g_o.reshape(N_CHIPS, D_ATTN // N_CHIPS, D_MODEL))], f"l{layer}_ffn")
    g_bias = None
    if layer % 2 == 0:
        rider, keys = exchange.rider(only=[(4, layer), (fam_o, li)])
        dq_a, dk_a, dv_a, rode = causal_bwd(kp["qkv"], dattn, kp["st"], 4, "sb", n + "_sb_bwd", rider=rider)
        exchange.landed(keys, rode)
        rider, keys = exchange.rider()
        dq_b, dk_b, dv_b, rode = dilated_bwd(kp["qkv"], tables, dattn, kp["o_dil"], kp["lse_dil"], 4, n + "_dil_bwd", rider=rider)
        dproj = jnp.concatenate([dq_a, dq_b, dk_a.astype(BF16), dk_b, dv_a.astype(BF16), dv_b], axis=1)
        g_qkv = matmul(kp["h1"], dproj, "tn", BF16, n + "_d_w_qkv", 1024, 768, 2048, mnk=(D_MODEL, 3 * D_ATTN, s),
                       o_spec=_chip_tile(D_MODEL, 768, lambda i, j, kk: (j, 0, 0)), out_shape=(N_CHIPS, D_MODEL, 768))
    else:
        rider, keys = exchange.rider()
        dq_f, dk_f, dv_f, dfk, rode = causal_bwd(kp["qkv"], dattn, kp["st"], 8, "fox", n + "_fox_bwd", fq=kp["fq"], fk=kp["fk"],
                                                 rider=rider)
        dcum = jnp.pad(dfk.reshape(N_HEADS, s).T, ((0, 0), (0, LANES - N_HEADS)))
        dfl, dbias = forget_bwd(kp["fl"], bias_row, dcum, n + "_forget_bwd")
        g_bias = dbias[0, :N_HEADS]
        dproj = jnp.concatenate([dq_f, dk_f.astype(BF16), dv_f.astype(BF16), dfl.astype(BF16)], axis=1)
        g_nat = matmul(kp["h1"], dproj, "tn", BF16, n + "_d_w_qkv", 1024, 640, 2048)
        g_qkv = g_nat[:, :N_CHIPS * QKVF_COLS].reshape(D_MODEL, N_CHIPS, QKVF_COLS)
        g_qkv = jnp.transpose(jnp.pad(g_qkv, ((0, 0), (0, 0), (0, QKVF_PAD - QKVF_COLS))), (1, 0, 2))
    exchange.landed(keys, rode)
    exchange.add([(fam_qkv, li, g_qkv)], f"l{layer}_qkv")
    dx, g_mix = dh_norm_bwd(dproj, kp["w_qkv"], kp["x"], mix_gain, dmid, n + "_d_h1", 512)
    return dx, g_mix, g_ffn, g_bias


def local_step(xs, target, norm_mix, norm_ffn, norm_final, b_forget, layer_weights):
    tables = _tables_for(xs.shape[0])
    bias_pad = jnp.pad(b_forget, ((0, 0), (0, LANES - N_HEADS)))
    saved, cur, have = [], xs, {}
    h1 = rmsnorm_fwd(cur, norm_mix[0:1], "l0_norm_mix")
    for layer in range(DEPTH):
        have.update(zip((f"qkv{layer}", f"o{layer}", f"fi{layer}", f"fo{layer}"), layer_weights[layer]))
        next_gain = norm_mix[layer + 1:layer + 2] if layer + 1 < DEPTH else None
        cur, h1, keep = forward_layer(layer, cur, h1, have, norm_ffn[layer:layer + 1], next_gain, tables,
                                      bias_pad[layer // 2:layer // 2 + 1])
        saved.append(keep)
    dcur, g_final, loss_part = loss_head(cur, norm_final.reshape(1, D_MODEL), target, "loss_head")
    keeper = KeepGradients()
    g_mix, g_ffn, g_bias = [None] * DEPTH, [None] * DEPTH, [None] * (DEPTH // 2)
    for layer in reversed(range(DEPTH)):
        dcur, g_mix[layer], g_ffn[layer], g_b = backward_layer(layer, dcur, saved[layer], norm_mix[layer:layer + 1],
                                                               norm_ffn[layer:layer + 1], tables, bias_pad[layer // 2:layer // 2 + 1], keeper)
        if g_b is not None:
            g_bias[layer // 2] = g_b
    return dcur, keeper.grads, (g_mix, g_ffn, g_final, g_bias), loss_part
```

```python
import functools

import jax
import jax.numpy as jnp
from jax import lax
from jax.experimental import pallas as pl
from jax.experimental.pallas import tpu as pltpu

F32 = jnp.float32
BF16 = jnp.bfloat16
MESH = pl.DeviceIdType.MESH

D_MODEL = 1024
DEPTH = 4
HEAD_DIM = 64
N_HEADS = 16
D_ATTN = 1024
D_FF = 2816
ROPE_THETA = 500000.0
ROT_HALF = 8
RMS_EPS = 1e-5
DIL_STRIDES = (1, 4, 16)
ADAM_LR, ADAM_B1, ADAM_B2, ADAM_EPS, ADAM_WD, ADAM_STEP = 0.001, 0.9, 0.999, 1e-8, 0.01, 10

LANES = 128
BLK = 128
VMEM_LIMIT = 56 * 1024 * 1024
NEG = -1e30
N_CHIPS = 4
SMALL_ROWS = 16


def _params(sem=None):
    return pltpu.CompilerParams(dimension_semantics=sem, vmem_limit_bytes=VMEM_LIMIT)


def _dot(a, b):
    return lax.dot_general(a, b, (((1,), (0,)), ((), ())), preferred_element_type=F32)


def _dot_nt(a, b):
    return lax.dot_general(a, b, (((1,), (1,)), ((), ())), preferred_element_type=F32)


def _dot_tn(a, b):
    return lax.dot_general(a, b, (((0,), (0,)), ((), ())), preferred_element_type=F32)


def _split3(x):
    x1 = x.astype(BF16)
    r1 = x - x1.astype(F32)
    x2 = r1.astype(BF16)
    x3 = (r1 - x2.astype(F32)).astype(BF16)
    return x1, x2, x3


def _dot_exact_lhs(x, t):
    x1, x2, x3 = _split3(x)
    return _dot(x1, t) + _dot(x2, t) + _dot(x3, t)


def _dot_exact_rhs(t, x):
    x1, x2, x3 = _split3(x)
    return _dot(t, x1) + _dot(t, x2) + _dot(t, x3)


def _iotas(shape=(BLK, LANES)):
    return lax.broadcasted_iota(jnp.int32, shape, 0), lax.broadcasted_iota(jnp.int32, shape, 1)


_DIMS = {"nn": (((1,), (0,)), ((), ())), "nt": (((1,), (1,)), ((), ())), "tn": (((0,), (0,)), ((), ()))}


def matmul(a, b, mode, out_dtype, name, tm, tn, tk, res=None, mnk=None, b_spec=None, o_spec=None, out_shape=None, norm_gain=None):
    if mnk is not None:
        m, n, k = mnk
    elif mode == "nn":
        (m, k), (k2, n) = a.shape, b.shape
    elif mode == "nt":
        (m, k), (n, k2) = a.shape, b.shape
    else:
        (k, m), (k2, n) = a.shape, b.shape
    assert m % tm == 0 and n % tn == 0 and k % tk == 0, (name, a.shape, b.shape)
    nk = k // tk
    a_spec = pl.BlockSpec((tk, tm), lambda i, j, kk: (kk, i)) if mode == "tn" else pl.BlockSpec((tm, tk), lambda i, j, kk: (i, kk))
    if b_spec is None:
        b_spec = pl.BlockSpec((tn, tk), lambda i, j, kk: (j, kk)) if mode == "nt" else pl.BlockSpec((tk, tn), lambda i, j, kk: (kk, j))
    r_spec = pl.BlockSpec((tm, tn), lambda i, j, kk: (i, j))
    if o_spec is None:
        o_spec = r_spec
    dims = _DIMS[mode]
    has_res, has_norm = res is not None, norm_gain is not None
    assert not has_norm or (tn == n and nk == 1)
    n_in = 2 + int(has_res) + int(has_norm)

    def body(*refs):
        a_ref, b_ref = refs[0], refs[1]
        r_ref = refs[2] if has_res else None
        o_ref = refs[n_in]

        def finish(v):
            if has_res:
                v = v + r_ref[...]
            o_ref[...] = v.astype(out_dtype)
            if has_norm:
                rstd = lax.rsqrt(jnp.mean(v * v, axis=-1, keepdims=True) + RMS_EPS)
                refs[n_in + 1][...] = (v * rstd * refs[n_in - 1][...]).astype(BF16)

        bv = b_ref[...]
        if bv.ndim == 3:
            bv = jnp.concatenate([bv[j] for j in range(bv.shape[0])], axis=1)
        p = lax.dot_general(a_ref[...].astype(BF16), bv.astype(BF16), dims, preferred_element_type=F32)
        if nk == 1:
            finish(p)
        else:
            acc = refs[-1]
            kk = pl.program_id(2)

            @pl.when(kk == 0)
            def _():
                acc[...] = p

            @pl.when(kk > 0)
            def _():
                acc[...] += p

            @pl.when(kk == nk - 1)
            def _():
                finish(acc[...])

    ops = [a, b] + ([res] if has_res else []) + ([norm_gain] if has_norm else [])
    specs = [a_spec, b_spec] + ([r_spec] if has_res else []) + ([pl.BlockSpec((1, tn), lambda i, j, kk: (0, j))] if has_norm else [])
    out_shape = jax.ShapeDtypeStruct((m, n) if out_shape is None else out_shape, out_dtype)
    return pl.pallas_call(
        body, name=name, out_shape=(out_shape, jax.ShapeDtypeStruct((m, n), BF16)) if has_norm else out_shape,
        grid=(m // tm, n // tn, nk), in_specs=specs, out_specs=(o_spec, r_spec) if has_norm else o_spec,
        scratch_shapes=[pltpu.VMEM((tm, tn), F32)] if nk > 1 else [],
        compiler_params=_params(("parallel", "parallel", "arbitrary")),
    )(*ops)


ROWS = 256


def _row_spec(cols, rows=ROWS):
    return pl.BlockSpec((rows, cols), lambda i: (i, 0))


def _fix_spec(r, cols):
    return pl.BlockSpec((r, cols), lambda i: (0, 0))


def rmsnorm_fwd(x, g, name):
    s, d = x.shape

    def body(x_ref, g_ref, h_ref):
        xv = x_ref[...]
        rstd = lax.rsqrt(jnp.mean(xv * xv, axis=-1, keepdims=True) + RMS_EPS)
        h_ref[...] = (xv * rstd * g_ref[...]).astype(BF16)

    return pl.pallas_call(
        body, name=name, out_shape=jax.ShapeDtypeStruct((s, d), BF16), grid=(s // ROWS,),
        in_specs=[_row_spec(d), _fix_spec(1, d)], out_specs=_row_spec(d), compiler_params=_params(("parallel",)),
    )(x, g)


def _rms_bwd_math(xv, gv, dh):
    rstd = lax.rsqrt(jnp.mean(xv * xv, axis=-1, keepdims=True) + RMS_EPS)
    xhat = xv * rstd
    u = dh * gv
    dx = rstd * (u - xhat * jnp.mean(u * xhat, axis=-1, keepdims=True))
    return dx, dh * xhat


def dh_norm_bwd(dy, w, x, g, dres, name, tm):
    s, k = dy.shape
    d = x.shape[1]

    def body(dy_ref, w_ref, x_ref, g_ref, dres_ref, dx_ref, dg_ref):
        wv = w_ref[...]
        if wv.ndim == 3:
            wv = jnp.concatenate([wv[j] for j in range(wv.shape[0])], axis=1)
        dx, dgt = _rms_bwd_math(x_ref[...], g_ref[...], _dot_nt(dy_ref[...], wv))
        dx_ref[...] = dres_ref[...] + dx
        part = jnp.sum(dgt, axis=0, keepdims=True)

        @pl.when(pl.program_id(0) == 0)
        def _():
            dg_ref[...] = part

        @pl.when(pl.program_id(0) > 0)
        def _():
            dg_ref[...] += part

    w_spec = pl.BlockSpec(w.shape, lambda i: (0,) * w.ndim)
    return pl.pallas_call(
        body, name=name, out_shape=(jax.ShapeDtypeStruct((s, d), F32), jax.ShapeDtypeStruct((1, d), F32)), grid=(s // tm,),
        in_specs=[pl.BlockSpec((tm, k), lambda i: (i, 0)), w_spec, _row_spec(d, tm), _fix_spec(1, d), _row_spec(d, tm)],
        out_specs=(_row_spec(d, tm), _fix_spec(1, d)), compiler_params=_params(("arbitrary",)),
    )(dy, w, x, g, dres)


def loss_head(x, g, target, name):
    s, d = x.shape

    def body(x_ref, g_ref, t_ref, dx_ref, dg_ref, loss_ref):
        xv, gv = x_ref[...], g_ref[...]
        rstd = lax.rsqrt(jnp.mean(xv * xv, axis=-1, keepdims=True) + RMS_EPS)
        err = xv * rstd * gv - t_ref[...]
        dx, dgt = _rms_bwd_math(xv, gv, err * (1.0 / d))
        dx_ref[...] = dx
        part = jnp.sum(dgt, axis=0, keepdims=True)
        lpart = jnp.full((1, LANES), 0.5 / d, F32) * jnp.sum(err * err)

        @pl.when(pl.program_id(0) == 0)
        def _():
            dg_ref[...] = part
            loss_ref[...] = lpart

        @pl.when(pl.program_id(0) > 0)
        def _():
            dg_ref[...] += part
            loss_ref[...] += lpart

    return pl.pallas_call(
        body, name=name,
        out_shape=(jax.ShapeDtypeStruct((s, d), F32), jax.ShapeDtypeStruct((1, d), F32), jax.ShapeDtypeStruct((1, LANES), F32)),
        grid=(s // ROWS,), in_specs=[_row_spec(d), _fix_spec(1, d), _row_spec(d)],
        out_specs=(_row_spec(d), _fix_spec(1, d), _fix_spec(1, LANES)), compiler_params=_params(("arbitrary",)),
    )(x, g, target)


def ffn_in_swiglu(h, w_in, name, tm=1024, rider=None):
    s, d = h.shape
    cols = w_in.shape[2]

    def body(h_ref, wg_ref, wu_ref, g_ref, u_ref, a_ref):
        hv = h_ref[...]
        gv, uv = _dot(hv, wg_ref[...]), _dot(hv, wu_ref[...])
        g_ref[...] = gv.astype(BF16)
        u_ref[...] = uv.astype(BF16)
        a_ref[...] = (gv * (1.0 / (1.0 + jnp.exp(-gv))) * uv).astype(BF16)

    tile = pl.BlockSpec((tm, cols), lambda i, j: (i, j))
    out = jax.ShapeDtypeStruct((s, 2 * cols), BF16)
    in_specs = [pl.BlockSpec((tm, d), lambda i, j: (i, 0)), pl.BlockSpec((None, d, cols), lambda i, j: (j, 0, 0)),
                pl.BlockSpec((None, d, cols), lambda i, j: (j + 2, 0, 0))]
    outs, rode = call_with_rider(body, name, rider, [h, w_in, w_in], in_specs, [out, out, out], [tile, tile, tile], [], (s // tm, 2))
    return (*outs, rode)


def swiglu_bwd(dy, w_out, gate, up, name):
    s, f = gate.shape
    d = dy.shape[1]

    def body(dy_ref, w_ref, g_ref, u_ref, o_ref):
        da = _dot_nt(dy_ref[...].astype(BF16), w_ref[...])
        gv, uv = g_ref[...].astype(F32), u_ref[...].astype(F32)
        sg = 1.0 / (1.0 + jnp.exp(-gv))
        o_ref[:, :f] = (da * uv * sg * (1.0 + gv * (1.0 - sg))).astype(BF16)
        o_ref[:, f:] = (da * gv * sg).astype(BF16)

    return pl.pallas_call(
        body, name=name, out_shape=jax.ShapeDtypeStruct((s, 2 * f), BF16), grid=(s // ROWS,),
        in_specs=[_row_spec(d), _fix_spec(f, d), _row_spec(f), _row_spec(f)], out_specs=_row_spec(2 * f),
        compiler_params=_params(("parallel",)),
    )(dy, w_out, gate, up)


Q_OFF, K_OFF, V_OFF = 0, 8, 16


KB = 512
BQ = 512
SUB = KB // BLK


def _softplus_parts(z):
    sp = jnp.log(1.0 + jnp.exp(-jnp.abs(z)))
    ls = jnp.minimum(z, 0.0) - sp
    return ls, ls - z


def _wide(t):
    return jnp.concatenate([t] * SUB, axis=1)


def _chunk_dots(x, tri):
    terms = []
    for u in range(SUB):
        xu = x[:, u * BLK:(u + 1) * BLK]
        hi = xu.astype(BF16)
        terms += [hi, (xu - hi.astype(F32)).astype(BF16)]
    r = _dot(jnp.concatenate(terms, axis=0), tri)
    rows = x.shape[0]
    piece = lambda n: r[n * rows:(n + 1) * rows]
    return [piece(2 * u) + piece(2 * u + 1) for u in range(SUB)]


def _block_suffix_sums(x, suffix, c):
    loc = _chunk_dots(x, suffix)
    out = [None] * SUB
    for u in reversed(range(SUB)):
        out[u] = loc[u] + c
        c = c + jnp.sum(x[:, u * BLK:(u + 1) * BLK], axis=1, keepdims=True)
    return jnp.concatenate(out, axis=1), c


def _block_prefix_sums(x, tri, c):
    loc = _chunk_dots(x, tri)
    out = []
    for u in range(SUB):
        out.append(loc[u] + c)
        c = c + jnp.sum(x[:, u * BLK:(u + 1) * BLK], axis=1, keepdims=True)
    return jnp.concatenate(out, axis=1), c


def causal_fwd(qkv, npairs, mode, name, fq=None, fk=None, rider=None):
    s = qkv.shape[0]
    nq = s // BQ
    fox = mode == "fox"

    def body(*refs):
        if fox:
            q_ref, k_ref, v_ref, fq_ref, fk_ref, o_ref, st_ref = refs
        else:
            q_ref, k_ref, v_ref, o_ref, st_ref = refs
        i = pl.program_id(1)
        nkb = (i * BQ + BQ - 1) // KB + 1
        row, lane = _iotas((BQ, KB))
        row_s, lane_s = _iotas()
        _, lane_q = _iotas((BQ, LANES))
        nfull = (i * BQ) // KB
        qpos = i * BQ + row
        qf = q_ref[...].astype(F32) * 0.125
        hms = (lane_q < HEAD_DIM, lane_q >= HEAD_DIM)
        qas = [jnp.where(hm, qf, 0.0).astype(BF16) for hm in hms]
        suffix = jnp.where(row_s > lane_s, 1.0, 0.0).astype(BF16)
        zero = jnp.zeros((BQ, LANES), F32)
        col0 = jnp.zeros((BQ, 1), F32)

        def kv(j):
            r0 = pl.multiple_of(j * KB, KB)
            return r0, k_ref[pl.ds(r0, KB), :], v_ref[pl.ds(r0, KB), :]

        if fox:
            fqs = [_wide(fq_ref[a]) for a in range(2)]

            def step(j, carry, masked):
                r0, kb, vb = kv(j)
                new = []
                for a in range(2):
                    acc, mx, l = carry[3 * a:3 * a + 3]
                    z = _dot_nt(qas[a], kb) + fqs[a] - fk_ref[a:a + 1, pl.ds(r0, KB)]
                    if masked:
                        z = jnp.where(r0 + lane <= qpos, z, NEG)
                    mnew = jnp.maximum(mx, jnp.max(z, axis=1, keepdims=True))
                    p = jnp.exp(z - mnew)
                    alpha = jnp.exp(mx - mnew)
                    new += [alpha * acc + _dot(p.astype(BF16), vb), mnew, alpha * l + jnp.sum(p, axis=1, keepdims=True)]
                return tuple(new)

            neg = jnp.full((BQ, 1), NEG, F32)
            res = lax.fori_loop(0, nfull, functools.partial(step, masked=False), (zero, neg, col0, zero, neg, col0))
            res = lax.fori_loop(nfull, nkb, functools.partial(step, masked=True), res)
            outs = [res[3 * a] / res[3 * a + 2] for a in range(2)]
            stats = [res[3 * a + 1] + jnp.log(res[3 * a + 2]) for a in range(2)]
        else:
            def step(j, carry, masked):
                r0, kb, vb = kv(j)
                strict = r0 + lane < qpos
                new = []
                for a in range(2):
                    acc, c = carry[2 * a:2 * a + 2]
                    ls, lm = _softplus_parts(_dot_nt(qas[a], kb))
                    if masked:
                        lm = jnp.where(strict, lm, 0.0)
                    between, c = _block_suffix_sums(lm, suffix, c)
                    aw = jnp.exp(ls + between)
                    if masked:
                        aw = jnp.where(strict, aw, 0.0)
                    new += [acc + _dot(aw.astype(BF16), vb), c]
                return tuple(new)

            res = lax.fori_loop(0, nkb - nfull, lambda jj, c: step(nkb - 1 - jj, c, True), (zero, col0, zero, col0))
            res = lax.fori_loop(0, nfull, lambda jj, c: step(nfull - 1 - jj, c, False), res)
            outs, stats = [res[0], res[2]], [res[1], res[3]]
        o_ref[...] = jnp.where(hms[0], outs[0], outs[1]).astype(o_ref.dtype)
        for a in range(2):
            st_ref[a] = jnp.broadcast_to(stats[a], (BQ, LANES))

    col = lambda off: (lambda p, i: (0, off + p))
    in_specs = [pl.BlockSpec((BQ, LANES), lambda p, i: (i, Q_OFF + p)),
                pl.BlockSpec((s, LANES), col(K_OFF)), pl.BlockSpec((s, LANES), col(V_OFF))]
    ops = [qkv, qkv, qkv]
    if fox:
        in_specs += [pl.BlockSpec((2, BQ, LANES), lambda p, i: (p, i, 0)), pl.BlockSpec((None, 2, s), lambda p, i: (p, 0, 0))]
        ops += [fq, fk]
    (o, stat), rode = call_with_rider(
        body, name, rider, ops, in_specs,
        [jax.ShapeDtypeStruct((s, npairs * LANES), BF16 if fox else F32), jax.ShapeDtypeStruct((2 * npairs, s, LANES), F32)],
        [pl.BlockSpec((BQ, LANES), lambda p, i: (i, p)), pl.BlockSpec((2, BQ, LANES), lambda p, i: (p, i, 0))], [], (npairs, nq))
    return o, stat, rode


def causal_bwd(qkv, do, stat, npairs, mode, name, fq=None, fk=None, rider=None):
    s = qkv.shape[0]
    nq = s // BQ
    fox = mode == "fox"

    def body(*refs):
        if fox:
            q_ref, k_ref, v_ref, do_ref, st_ref, fq_ref, fk_ref, dq_ref, dk_out, dv_out, df_ref, p_s, dp_s, dk_ref, dv_ref = refs
        else:
            q_ref, k_ref, v_ref, do_ref, st_ref, dq_ref, dk_out, dv_out, dk_ref, dv_ref = refs
        i = pl.program_id(1)

        @pl.when(i == 0)
        def _():
            dk_ref[...] = jnp.zeros_like(dk_ref)
            dv_ref[...] = jnp.zeros_like(dv_ref)
            if fox:
                df_ref[...] = jnp.zeros_like(df_ref)

        nkb = (i * BQ + BQ - 1) // KB + 1
        nfull = (i * BQ) // KB
        row, lane = _iotas((BQ, KB))
        row_s, lane_s = _iotas()
        _, lane_q = _iotas((BQ, LANES))
        qpos = i * BQ + row
        qf = q_ref[...].astype(F32) * 0.125
        dov = do_ref[...]
        hms = (lane_q < HEAD_DIM, lane_q >= HEAD_DIM)
        qas = [jnp.where(hm, qf, 0.0).astype(BF16) for hm in hms]
        doas = [jnp.where(hm, dov, 0.0).astype(BF16) for hm in hms]
        stas = [_wide(st_ref[a]) for a in range(2)]
        zero = jnp.zeros((BQ, LANES), F32)
        col0 = jnp.zeros((BQ, 1), F32)

        def kv(j):
            r0 = pl.multiple_of(j * KB, KB)
            return r0, k_ref[pl.ds(r0, KB), :], v_ref[pl.ds(r0, KB), :]

        if fox:
            fqs = [_wide(fq_ref[a]) for a in range(2)]

            def probs(j, deltas, masked):
                r0, kb, vb = kv(j)
                new = []
                for a in range(2):
                    z = _dot_nt(qas[a], kb) + fqs[a] - fk_ref[a:a + 1, pl.ds(r0, KB)]
                    p = jnp.exp(z - stas[a])
                    if masked:
                        p = jnp.where(r0 + lane <= qpos, p, 0.0)
                    dp = _dot_nt(doas[a], vb)
                    p_s[a, j] = p
                    dp_s[a, j] = dp
                    new.append(deltas[a] + jnp.sum(p * dp, axis=1, keepdims=True))
                return tuple(new)

            deltas = lax.fori_loop(0, nfull, functools.partial(probs, masked=False), (col0, col0))
            deltas = lax.fori_loop(nfull, nkb, functools.partial(probs, masked=True), deltas)

            def step(j, dqs):
                r0, kb, _ = kv(j)
                new = []
                dk = jnp.zeros((KB, LANES), F32)
                dv = jnp.zeros((KB, LANES), F32)
                for a in range(2):
                    p = p_s[a, j]
                    ds = p * (dp_s[a, j] - deltas[a])
                    dsb = ds.astype(BF16)
                    dk += _dot_tn(dsb, qas[a])
                    dv += _dot_tn(p.astype(BF16), doas[a])
                    df_ref[a:a + 1, pl.ds(r0, KB)] -= jnp.sum(ds, axis=0, keepdims=True)
                    new.append(dqs[a] + _dot(dsb, kb))
                dk_ref[pl.ds(r0, KB), :] += dk
                dv_ref[pl.ds(r0, KB), :] += dv
                return tuple(new)

            dqs = lax.fori_loop(0, nkb, step, (zero, zero))
        else:
            incl = jnp.where(row_s <= lane_s, 1.0, 0.0).astype(BF16)
            excl = jnp.where(row_s < lane_s, 1.0, 0.0).astype(BF16)

            def step(j, carry, masked):
                r0, kb, vb = kv(j)
                strict = r0 + lane < qpos
                new = []
                dk = jnp.zeros((KB, LANES), F32)
                dv = jnp.zeros((KB, LANES), F32)
                for a in range(2):
                    dq, cm, cg = carry[3 * a:3 * a + 3]
                    ls, lm = _softplus_parts(_dot_nt(qas[a], kb))
                    if masked:
                        lm = jnp.where(strict, lm, 0.0)
                    beta = jnp.exp(ls)
                    upto, cm = _block_prefix_sums(lm, incl, cm)
                    aw = jnp.exp(ls + stas[a] - upto)
                    if masked:
                        aw = jnp.where(strict, aw, 0.0)
                    g = aw * _dot_nt(doas[a], vb)
                    pre, cg = _block_prefix_sums(g, excl, cg)
                    dz = g * (1.0 - beta) - pre * beta
                    if masked:
                        dz = jnp.where(strict, dz, 0.0)
                    dzb = dz.astype(BF16)
                    dk += _dot_tn(dzb, qas[a])
                    dv += _dot_tn(aw.astype(BF16), doas[a])
                    new += [dq + _dot(dzb, kb), cm, cg]
                dk_ref[pl.ds(r0, KB), :] += dk
                dv_ref[pl.ds(r0, KB), :] += dv
                return tuple(new)

            res = lax.fori_loop(0, nfull, functools.partial(step, masked=False), (zero, col0, col0, zero, col0, col0))
            res = lax.fori_loop(nfull, nkb, functools.partial(step, masked=True), res)
            dqs = (res[0], res[3])
        dq_ref[...] = (jnp.where(hms[0], dqs[0], dqs[1]) * 0.125).astype(BF16)

        @pl.when(i == nq - 1)
        def _():
            dk_out[...] = dk_ref[...].astype(BF16)
            dv_out[...] = dv_ref[...].astype(BF16)

    col = lambda off: (lambda p, i: (0, off + p))
    blk = pl.BlockSpec((BQ, LANES), lambda p, i: (i, p))
    acc = pl.BlockSpec((s, LANES), lambda p, i: (0, p))
    st_spec = pl.BlockSpec((2, BQ, LANES), lambda p, i: (p, i, 0))
    in_specs = [pl.BlockSpec((BQ, LANES), lambda p, i: (i, Q_OFF + p)), pl.BlockSpec((s, LANES), col(K_OFF)),
                pl.BlockSpec((s, LANES), col(V_OFF)), blk, st_spec]
    ops = [qkv, qkv, qkv, do, stat]
    w = npairs * LANES
    out_shape = [jax.ShapeDtypeStruct((s, w), BF16)] * 3
    out_specs = [blk, acc, acc]
    scratch = []
    if fox:
        fk_spec = pl.BlockSpec((None, 2, s), lambda p, i: (p, 0, 0))
        in_specs += [st_spec, fk_spec]
        ops += [fq, fk]
        out_shape.append(jax.ShapeDtypeStruct((npairs, 2, s), F32))
        out_specs.append(fk_spec)
        scratch = [pltpu.VMEM((2, s // KB, BQ, KB), F32)] * 2
    scratch = scratch + [pltpu.VMEM((s, LANES), F32)] * 2
    outs, rode = call_with_rider(body, name, rider, ops, in_specs, out_shape, out_specs, scratch, (npairs, nq))
    return (*outs, rode)


def forget_fwd(fl, bias, name):
    s = fl.shape[0]

    def body(fl_ref, b_ref, f_ref):
        row, lane = _iotas()
        lower = jnp.where(lane <= row, 1.0, 0.0).astype(BF16)

        def step(n, carry):
            r0 = pl.multiple_of(n * BLK, BLK)
            ls, _ = _softplus_parts(fl_ref[pl.ds(r0, BLK), :] + b_ref[...])
            blk = _dot_exact_rhs(lower, ls) + carry
            f_ref[pl.ds(r0, BLK), :] = blk
            return blk[BLK - 1:BLK, :]

        lax.fori_loop(0, s // BLK, step, jnp.zeros((1, LANES), F32))

    return pl.pallas_call(
        body, name=name, out_shape=jax.ShapeDtypeStruct((s, LANES), F32),
        in_specs=[pl.BlockSpec(memory_space=pltpu.VMEM)] * 2, out_specs=pl.BlockSpec(memory_space=pltpu.VMEM),
        compiler_params=_params(),
    )(fl, bias)


def forget_bwd(fl, bias, df, name):
    s = fl.shape[0]
    nb = s // BLK

    def body(fl_ref, b_ref, df_ref, o_ref, db_ref):
        row, lane = _iotas()
        upper = jnp.where(lane >= row, 1.0, 0.0).astype(BF16)

        def step(nn, carry):
            tail, db = carry
            r0 = pl.multiple_of((nb - 1 - nn) * BLK, BLK)
            dls = _dot_exact_rhs(upper, df_ref[pl.ds(r0, BLK), :]) + tail
            xv = fl_ref[pl.ds(r0, BLK), :] + b_ref[...]
            dfl = dls * (1.0 / (1.0 + jnp.exp(xv)))
            o_ref[pl.ds(r0, BLK), :] = dfl
            return dls[0:1, :], db + jnp.sum(dfl, axis=0, keepdims=True)

        _, db = lax.fori_loop(0, nb, step, (jnp.zeros((1, LANES), F32), jnp.zeros((1, LANES), F32)))
        db_ref[...] = db

    return pl.pallas_call(
        body, name=name, out_shape=(jax.ShapeDtypeStruct((s, LANES), F32), jax.ShapeDtypeStruct((1, LANES), F32)),
        in_specs=[pl.BlockSpec(memory_space=pltpu.VMEM)] * 3,
        out_specs=(pl.BlockSpec(memory_space=pltpu.VMEM), pl.BlockSpec(memory_space=pltpu.VMEM)),
        compiler_params=_params(),
    )(fl, bias, df)


def _rot_tables(s):
    inv = ROPE_THETA ** (-jnp.arange(ROT_HALF, dtype=F32) * 2.0 / (2 * ROT_HALF))
    ang = jnp.arange(s, dtype=F32)[:, None] * inv[None, :]
    cos, sin = jnp.cos(ang), jnp.sin(ang)
    z8 = jnp.zeros((s, ROT_HALF), F32)
    rest = HEAD_DIM - 2 * ROT_HALF
    zr, onr = jnp.zeros((s, rest), F32), jnp.ones((s, rest), F32)
    tile = lambda t: jnp.tile(t, (1, 2))
    return tile(jnp.concatenate([cos, cos, onr], 1)), tile(jnp.concatenate([-sin, z8, zr], 1)), tile(jnp.concatenate([z8, sin, zr], 1))


def _deinterleave(dst, src_ref, stride, s, dtype):
    length = s // stride
    for r in range(stride):
        if stride == 1:
            dst[...] = src_ref[...].astype(dtype)
        else:
            dst[r * length:(r + 1) * length, :] = src_ref[pl.ds(r, length, stride=stride), :].astype(dtype)


def _band_masks(row, lane, first):
    return lane <= row, lane >= row + jnp.where(first, BLK, 0)


N_DIL_PAIRS = 4


def _rotate_into(q_ref, k_ref, v_ref, c_ref, s1_ref, s2_ref, qr, kr, vr):
    c, s1, s2 = c_ref[...], s1_ref[...], s2_ref[...]
    rot = lambda xv: xv * c + pltpu.roll(xv, LANES - ROT_HALF, 1) * s1 + pltpu.roll(xv, ROT_HALF, 1) * s2
    qr[...] = rot(q_ref[...].astype(F32)) * 0.125
    kr[...] = rot(k_ref[...].astype(F32))
    vr[...] = v_ref[...].astype(F32)


def _dilated_operands(qkv, tables):
    s = qkv.shape[0]
    col = lambda off: pl.BlockSpec((s, LANES), lambda p: (0, off + N_DIL_PAIRS + p))
    table = pl.BlockSpec((s, LANES), lambda p: (0, 0))
    return [qkv, qkv, qkv, *tables], [col(Q_OFF), col(K_OFF), col(V_OFF), table, table, table]


def dilated_fwd(qkv, tables, name, rider=None):
    s = qkv.shape[0]
    npairs, w = N_DIL_PAIRS, N_DIL_PAIRS * LANES
    nblk = s // BLK

    def body(q_in, k_in, v_in, c_ref, s1_ref, s2_ref, o_ref, lse_ref, q_ref, k_ref, v_ref, qs, ks, vs, od, ld, on, ln):
        row, lane = _iotas()
        _rotate_into(q_in, k_in, v_in, c_ref, s1_ref, s2_ref, q_ref, k_ref, v_ref)
        for pi, stride in enumerate(DIL_STRIDES):
            per = (s // stride) // BLK
            _deinterleave(qs, q_ref, stride, s, BF16)
            _deinterleave(ks, k_ref, stride, s, BF16)
            _deinterleave(vs, v_ref, stride, s, BF16)

            def block(b, carry):
                r0 = pl.multiple_of(b * BLK, BLK)
                rp = pl.multiple_of(jnp.maximum(b - 1, 0) * BLK, BLK)
                mc, mp = _band_masks(row, lane, b % per == 0)
                q = qs[pl.ds(r0, BLK), :]
                kc, kp, vc, vp = ks[pl.ds(r0, BLK), :], ks[pl.ds(rp, BLK), :], vs[pl.ds(r0, BLK), :], vs[pl.ds(rp, BLK), :]
                out = jnp.zeros((BLK, LANES), F32)
                lse = jnp.zeros((BLK, LANES), F32)
                for a in range(2):
                    hm = (lane < HEAD_DIM) if a == 0 else (lane >= HEAD_DIM)
                    qa = jnp.where(hm, q.astype(F32), 0.0).astype(BF16)
                    sc = jnp.where(mc, _dot_nt(qa, kc), NEG)
                    sp = jnp.where(mp, _dot_nt(qa, kp), NEG)
                    mx = jnp.maximum(jnp.max(sc, axis=1, keepdims=True), jnp.max(sp, axis=1, keepdims=True))
                    pc, pp = jnp.exp(sc - mx), jnp.exp(sp - mx)
                    l = jnp.sum(pc, axis=1, keepdims=True) + jnp.sum(pp, axis=1, keepdims=True)
                    oa = (_dot(pc.astype(BF16), vc) + _dot(pp.astype(BF16), vp)) / l
                    out = jnp.where(hm, oa, out)
                    lse = jnp.where(hm, mx + jnp.log(l), lse)
                od[pl.ds(r0, BLK), :] = out
                ld[pl.ds(r0, BLK), :] = lse
                return carry

            lax.fori_loop(0, nblk, block, 0, unroll=2)
            length = s // stride
            for r in range(stride):
                if stride == 1:
                    on[pi] = od[...]
                    ln[pi] = ld[...]
                else:
                    on[pi, pl.ds(r, length, stride=stride), :] = od[r * length:(r + 1) * length, :]
                    ln[pi, pl.ds(r, length, stride=stride), :] = ld[r * length:(r + 1) * length, :]

        def merge(n, carry):
            r0 = pl.multiple_of(n * BLK, BLK)
            ls = [ln[pi, pl.ds(r0, BLK), :] for pi in range(3)]
            mx = jnp.maximum(jnp.maximum(ls[0], ls[1]), ls[2])
            ws = [jnp.exp(lv - mx) for lv in ls]
            den = ws[0] + ws[1] + ws[2]
            num = ws[0] * on[0, pl.ds(r0, BLK), :] + ws[1] * on[1, pl.ds(r0, BLK), :] + ws[2] * on[2, pl.ds(r0, BLK), :]
            o_ref[pl.ds(r0, BLK), :] = num / den
            lse_ref[pl.ds(r0, BLK), :] = mx + jnp.log(den)
            return carry

        lax.fori_loop(0, nblk, merge, 0, unroll=2)

    colspec = pl.BlockSpec((s, LANES), lambda p: (0, p))
    out = jax.ShapeDtypeStruct((s, w), F32)
    scratch = ([pltpu.VMEM((s, LANES), F32)] * 3 + [pltpu.VMEM((s, LANES), BF16)] * 3 + [pltpu.VMEM((s, LANES), F32)] * 2
               + [pltpu.VMEM((3, s, LANES), F32)] * 2)
    ops, in_specs = _dilated_operands(qkv, tables)
    (o, lse), rode = call_with_rider(body, name, rider, ops, in_specs, [out, out], [colspec, colspec], scratch, (npairs,))
    return o, lse, rode


def dilated_bwd(qkv, tables, do, out, lse, do_off, name, rider=None):
    s = qkv.shape[0]
    npairs, w = N_DIL_PAIRS, N_DIL_PAIRS * LANES
    nblk = s // BLK

    def body(q_in, k_in, v_in, c_ref, s1_ref, s2_ref, do_ref, out_ref, lse_ref, dq_out, dk_out, dv_out,
             q_ref, k_ref, v_ref, dq_ref, dk_ref, dv_ref, qs, ks, vs, dos, dls, lss, dqd, dkd, dvd, dln):
        row, lane = _iotas()
        same_head = jnp.where((row < HEAD_DIM) == (lane < HEAD_DIM), 1.0, 0.0).astype(BF16)
        _rotate_into(q_in, k_in, v_in, c_ref, s1_ref, s2_ref, q_ref, k_ref, v_ref)

        def delta_blk(n, carry):
            r0 = pl.multiple_of(n * BLK, BLK)
            dln[pl.ds(r0, BLK), :] = _dot_exact_lhs(do_ref[pl.ds(r0, BLK), :] * out_ref[pl.ds(r0, BLK), :], same_head)
            return carry

        lax.fori_loop(0, nblk, delta_blk, 0, unroll=2)
        for pi, stride in enumerate(DIL_STRIDES):
            per = (s // stride) // BLK
            _deinterleave(qs, q_ref, stride, s, BF16)
            _deinterleave(ks, k_ref, stride, s, BF16)
            _deinterleave(vs, v_ref, stride, s, BF16)
            _deinterleave(dos, do_ref, stride, s, BF16)
            _deinterleave(dls, dln, stride, s, F32)
            _deinterleave(lss, lse_ref, stride, s, F32)

            def block(b, carry):
                r0 = pl.multiple_of(b * BLK, BLK)
                rp = pl.multiple_of(jnp.maximum(b - 1, 0) * BLK, BLK)
                first = b % per == 0
                mc, mp = _band_masks(row, lane, first)
                q, dov = qs[pl.ds(r0, BLK), :], dos[pl.ds(r0, BLK), :]
                kc, kp, vc, vp = ks[pl.ds(r0, BLK), :], ks[pl.ds(rp, BLK), :], vs[pl.ds(r0, BLK), :], vs[pl.ds(rp, BLK), :]
                lse_t, dl_t = lss[pl.ds(r0, BLK), :], dls[pl.ds(r0, BLK), :]
                dq = jnp.zeros((BLK, LANES), F32)
                dkc = jnp.zeros((BLK, LANES), F32)
                dkp = jnp.zeros((BLK, LANES), F32)
                dvc = jnp.zeros((BLK, LANES), F32)
                dvp = jnp.zeros((BLK, LANES), F32)
                for a in range(2):
                    hm = (lane < HEAD_DIM) if a == 0 else (lane >= HEAD_DIM)
                    pick = lane == a * HEAD_DIM
                    qa = jnp.where(hm, q.astype(F32), 0.0).astype(BF16)
                    doa = jnp.where(hm, dov.astype(F32), 0.0).astype(BF16)
                    lse_a = jnp.sum(jnp.where(pick, lse_t, 0.0), axis=1, keepdims=True)
                    dl_a = jnp.sum(jnp.where(pick, dl_t, 0.0), axis=1, keepdims=True)
                    pc = jnp.where(mc, jnp.exp(_dot_nt(qa, kc) - lse_a), 0.0)
                    pp = jnp.where(mp, jnp.exp(_dot_nt(qa, kp) - lse_a), 0.0)
                    dsc = (pc * (_dot_nt(doa, vc) - dl_a)).astype(BF16)
                    dsp = (pp * (_dot_nt(doa, vp) - dl_a)).astype(BF16)
                    dq = jnp.where(hm, _dot(dsc, kc) + _dot(dsp, kp), dq)
                    dkc += _dot_tn(dsc, qa)
                    dkp += _dot_tn(dsp, qa)
                    dvc += _dot_tn(pc.astype(BF16), doa)
                    dvp += _dot_tn(pp.astype(BF16), doa)
                dqd[pl.ds(r0, BLK), :] = dq
                dkd[pl.ds(r0, BLK), :] = dkc
                dvd[pl.ds(r0, BLK), :] = dvc

                @pl.when(jnp.logical_not(first))
                def _():
                    dkd[pl.ds(rp, BLK), :] += dkp
                    dvd[pl.ds(rp, BLK), :] += dvp

                return carry

            lax.fori_loop(0, nblk, block, 0, unroll=2)
            length = s // stride
            for dst, src in ((dq_ref, dqd), (dk_ref, dkd), (dv_ref, dvd)):
                for r in range(stride):
                    if stride == 1:
                        dst[...] = src[...]
                    else:
                        dst[pl.ds(r, length, stride=stride), :] += src[r * length:(r + 1) * length, :]

        c, s1, s2 = c_ref[...], s1_ref[...], s2_ref[...]
        rot_t = lambda dy: dy * c + pltpu.roll(dy * s1, ROT_HALF, 1) + pltpu.roll(dy * s2, LANES - ROT_HALF, 1)
        dq_out[...] = (rot_t(dq_ref[...]) * 0.125).astype(BF16)
        dk_out[...] = rot_t(dk_ref[...]).astype(BF16)
        dv_out[...] = dv_ref[...].astype(BF16)

    colspec = pl.BlockSpec((s, LANES), lambda p: (0, p))
    do_spec = pl.BlockSpec((s, LANES), lambda p: (0, do_off + p))
    o3 = jax.ShapeDtypeStruct((s, w), BF16)
    scratch = [pltpu.VMEM((s, LANES), F32)] * 6 + [pltpu.VMEM((s, LANES), BF16)] * 4 + [pltpu.VMEM((s, LANES), F32)] * 6
    ops, in_specs = _dilated_operands(qkv, tables)
    outs, rode = call_with_rider(body, name, rider, ops + [do, out, lse], in_specs + [do_spec, colspec, colspec],
                                 [o3, o3, o3], [colspec, colspec, colspec], scratch, (npairs,))
    return (*outs, rode)


def adamw(w, g, m, v, name):
    rows, cols = w.shape
    rb = min(rows, ROWS)
    c1 = 1.0 - ADAM_B1 ** ADAM_STEP
    c2 = 1.0 - ADAM_B2 ** ADAM_STEP

    def body(w_ref, g_ref, m_ref, v_ref, d_ref, mo_ref, vo_ref):
        gv = g_ref[...]
        mn = ADAM_B1 * m_ref[...] + (1.0 - ADAM_B1) * gv
        vn = ADAM_B2 * v_ref[...] + (1.0 - ADAM_B2) * (gv * gv)
        d_ref[...] = -ADAM_LR * ((mn / c1) / (jnp.sqrt(vn / c2) + ADAM_EPS) + ADAM_WD * w_ref[...])
        mo_ref[...] = mn
        vo_ref[...] = vn

    spec = _row_spec(cols, rb)
    out = jax.ShapeDtypeStruct((rows, cols), F32)
    return pl.pallas_call(
        body, name=name, out_shape=(out, out, out), grid=(rows // rb,), in_specs=[spec] * 4, out_specs=(spec,) * 3,
        compiler_params=_params(("parallel",)),
    )(w, g, m, v)


def _prefetch_call(body, name, scalar, ops, grid, in_specs, out_specs, out_shape, sem):
    spec = pltpu.PrefetchScalarGridSpec(num_scalar_prefetch=1, grid=grid, in_specs=in_specs, out_specs=out_specs)
    return pl.pallas_call(body, name=name, grid_spec=spec, out_shape=out_shape, compiler_params=_params(sem))(scalar, *ops)


def pair_sums(gs, gots, core, name):
    n = len(gs)

    def body(core_ref, *refs):
        for f in range(n):
            refs[2 * n + f][...] = (refs[f][...].astype(F32) + refs[n + f][...].astype(F32)).astype(BF16)

    blk = lambda g, rows_of: pl.BlockSpec((None, g.shape[1] // 2, g.shape[2]), rows_of)
    mine = [blk(g, lambda j, core_ref: (j, core_ref[0], 0)) for g in gs]
    half = [blk(g, lambda j, core_ref: (j, 0, 0)) for g in gs]
    outs = tuple(jax.ShapeDtypeStruct((g.shape[0], g.shape[1] // 2, g.shape[2]), BF16) for g in gs)
    return _prefetch_call(body, name, core, (*gs, *gots), (N_CHIPS,), mine + half, tuple(half), outs, ("parallel",))


def chip_sums(pairs, gots, chip, layers, intos, name):
    n = len(pairs)

    def body(chip_ref, *refs):
        for f in range(n):
            p_ref, a_ref, b_ref, c_ref = refs[4 * f:4 * f + 4]
            refs[5 * n + f][...] = ((p_ref[...].astype(F32) + a_ref[...].astype(F32)) + b_ref[...].astype(F32)) + c_ref[...].astype(F32)

    in_specs, ops = [], []
    for p, got in zip(pairs, gots):
        blk = lambda at, p=p: pl.BlockSpec((None,) + p.shape[1:], at)
        in_specs += [blk(lambda i, chip_ref: (chip_ref[0], 0, 0))] + [blk(lambda i, chip_ref, k=k: (k, 0, 0)) for k in range(3)]
        ops += [p, got, got, got]
    out_specs = tuple(pl.BlockSpec((None,) + p.shape[1:], lambda i, chip_ref, l=l: (l, 0, 0)) for p, l in zip(pairs, layers))
    spec = pltpu.PrefetchScalarGridSpec(num_scalar_prefetch=1, grid=(1,), in_specs=in_specs + [_ANY] * n, out_specs=out_specs)
    return pl.pallas_call(
        body, name=name, grid_spec=spec, out_shape=tuple(jax.ShapeDtypeStruct(t.shape, t.dtype) for t in intos),
        input_output_aliases={1 + 4 * n + f: f for f in range(n)}, compiler_params=_params(("arbitrary",)),
    )(chip, *ops, *intos)


def adamw_family(w, m, v, g_mine, g_other, core, name):
    nl, r, c = w.shape
    gc = g_mine.shape[2]
    rh = r // 2
    nb = 4 if rh % 512 == 0 else (2 if rh % 16 == 0 and rh > 256 else 1)
    rb = rh // nb
    c1 = 1.0 - ADAM_B1 ** ADAM_STEP
    c2 = 1.0 - ADAM_B2 ** ADAM_STEP

    def body(core_ref, w_ref, m_ref, v_ref, gm_ref, go_ref, g_ref, d_ref, mo_ref, vo_ref):
        gv = jnp.where(pl.program_id(1) == core_ref[0], gm_ref[...], go_ref[...])[:, :c]
        mn = ADAM_B1 * m_ref[...] + (1.0 - ADAM_B1) * gv
        vn = ADAM_B2 * v_ref[...] + (1.0 - ADAM_B2) * (gv * gv)
        g_ref[...] = gv
        d_ref[...] = -ADAM_LR * ((mn / c1) / (jnp.sqrt(vn / c2) + ADAM_EPS) + ADAM_WD * w_ref[...])
        mo_ref[...] = mn
        vo_ref[...] = vn

    full = pl.BlockSpec((None, rb, c), lambda l, h, i, core_ref: (l, h * nb + i, 0))
    mine = pl.BlockSpec((None, rb, gc), lambda l, h, i, core_ref: (l, jnp.where(h == core_ref[0], i, 0), 0))
    other = pl.BlockSpec((None, rb, gc), lambda l, h, i, core_ref: (l, jnp.where(h == core_ref[0], 0, i), 0))
    out = jax.ShapeDtypeStruct((nl, r, c), F32)
    return _prefetch_call(body, name, core, (w, m, v, g_mine, g_other), (nl, 2, nb), [full, full, full, mine, other],
                          (full, full, full, full), (out, out, out, out), ("parallel", "parallel", "parallel"))


def _coords():
    return lax.axis_index("x"), lax.axis_index("y"), lax.axis_index("c")


def _other_chips(x, y):
    return ((1 - x, 1 - y), (1 - x, y), (x, 1 - y))


_ANY = pl.BlockSpec(memory_space=pl.ANY)


def _exchange_call(body, name, arrays, out_shapes, n_copies, n_local=0):
    n = len(arrays)

    def wrapped(*refs):
        body(refs[:n], refs[n:n + len(out_shapes)], *refs[n + len(out_shapes):])

    scratch = [pltpu.SemaphoreType.DMA((n_copies,)), pltpu.SemaphoreType.DMA((n_copies,))]
    if n_local:
        scratch.append(pltpu.SemaphoreType.DMA((n_local,)))
    return pl.pallas_call(
        wrapped, name=name, out_shape=tuple(out_shapes), in_specs=[_ANY] * n, out_specs=tuple([_ANY] * len(out_shapes)),
        scratch_shapes=scratch, compiler_params=_params(),
    )(*arrays)


def _remote(send_sems, recv_sems, n, src, dst, to):
    return pltpu.make_async_remote_copy(src_ref=src, dst_ref=dst, send_sem=send_sems.at[n], recv_sem=recv_sems.at[n],
                                        device_id=to, device_id_type=MESH)


class Rider:
    def __init__(self, arrays, out_shapes, n_remote, n_local, copies, then=None):
        self.arrays, self.out_shapes, self.n_remote, self.n_local = list(arrays), list(out_shapes), n_remote, n_local
        self.copies, self.then = copies, then

    def sems(self):
        return [pltpu.SemaphoreType.DMA((self.n_remote,)), pltpu.SemaphoreType.DMA((self.n_remote,)),
                pltpu.SemaphoreType.DMA((max(self.n_local, 1),))]

    def run(self, name):
        n, no = len(self.arrays), len(self.out_shapes)

        def body(*refs):
            for stage in (self.copies, self.then):
                if stage is not None:
                    cps = stage(refs[:n], refs[n:n + no], *refs[n + no:])
                    for cp in cps:
                        cp.start()
                    for cp in cps:
                        cp.wait()

        return pl.pallas_call(
            body, name=name, out_shape=tuple(self.out_shapes), in_specs=[_ANY] * n, out_specs=tuple([_ANY] * no),
            scratch_shapes=self.sems(), compiler_params=_params(),
        )(*self.arrays)


def ride(rider, body, n_in, n_out, grid):
    if rider is None:
        return body
    ni, no = len(rider.arrays), len(rider.out_shapes)

    def wrapped(*refs):
        ins, r_in = refs[:n_in], refs[n_in:n_in + ni]
        outs = refs[n_in + ni:n_in + ni + n_out]
        r_out = refs[n_in + ni + n_out:n_in + ni + n_out + no]
        rest = refs[n_in + ni + n_out + no:]
        scratch, sems = rest[:len(rest) - 3], rest[len(rest) - 3:]
        step, total = 0, 1
        for a, g in enumerate(grid):
            step, total = step * g + pl.program_id(a), total * g
        assert total >= 3
        relay_at = (7 * total) // 8 if rider.then is not None else total - 1

        @pl.when(step == 0)
        def _():
            for cp in rider.copies(r_in, r_out, *sems):
                cp.start()

        body(*ins, *outs, *scratch)

        @pl.when(step == relay_at)
        def _():
            for cp in rider.copies(r_in, r_out, *sems):
                cp.wait()
            if rider.then is not None:
                for cp in rider.then(r_in, r_out, *sems):
                    cp.start()

        if rider.then is not None:
            @pl.when(step == total - 1)
            def _():
                for cp in rider.then(r_in, r_out, *sems):
                    cp.wait()

    return wrapped


def call_with_rider(body, name, rider, ops, in_specs, out_shape, out_specs, scratch, grid):
    n_in, n_out = len(ops), len(out_shape)
    ops, in_specs, out_shape, out_specs, scratch = list(ops), list(in_specs), list(out_shape), list(out_specs), list(scratch)
    if rider is not None:
        ops += rider.arrays
        in_specs += [_ANY] * len(rider.arrays)
        out_shape += rider.out_shapes
        out_specs += [_ANY] * len(rider.out_shapes)
        scratch += rider.sems()
    res = pl.pallas_call(
        ride(rider, body, n_in, n_out, grid), name=name, out_shape=tuple(out_shape), grid=grid, in_specs=in_specs,
        out_specs=tuple(out_specs), scratch_shapes=scratch, compiler_params=_params(("arbitrary",) * len(grid)),
    )(*ops)
    return tuple(res[:n_out]), list(res[n_out:])


def gather_rider(shards):
    nf = len(shards)
    half = lambda ref, which: pl.ds(which * (ref.shape[-2] // 2), ref.shape[-2] // 2)

    def copies(s_refs, o_refs, send_sems, recv_sems, local_sems):
        x, y, c = _coords()
        me = 2 * x + y
        cps = [pltpu.make_async_copy(s_refs[f], o_refs[f].at[me], local_sems.at[f]) for f in range(nf)]
        for k, (px, py) in enumerate(_other_chips(x, y)):
            for f in range(nf):
                rows = half(s_refs[f], c)
                cps.append(_remote(send_sems, recv_sems, k * nf + f, s_refs[f].at[rows], o_refs[f].at[me, rows], (px, py, c)))
        return cps

    def relay(s_refs, o_refs, send_sems, recv_sems, local_sems):
        x, y, c = _coords()
        cps = []
        for k, (px, py) in enumerate(_other_chips(x, y)):
            for f in range(nf):
                landed = o_refs[f].at[2 * px + py, half(s_refs[f], c)]
                cps.append(_remote(send_sems, recv_sems, (3 + k) * nf + f, landed, landed, (x, y, 1 - c)))
        return cps

    return Rider(shards, [jax.ShapeDtypeStruct((N_CHIPS,) + sh.shape, sh.dtype) for sh in shards], 6 * nf, nf, copies, relay)


def scatter_rider(pairs):
    nf = len(pairs)

    def copies(p_refs, o_refs, send_sems, recv_sems, local_sems):
        x, y, c = _coords()
        cps = []
        for k, (px, py) in enumerate(_other_chips(x, y)):
            for f in range(nf):
                cps.append(_remote(send_sems, recv_sems, k * nf + f, p_refs[f].at[2 * px + py], o_refs[f].at[k], (px, py, c)))
        return cps

    return Rider(pairs, [jax.ShapeDtypeStruct((3,) + p.shape[1:], p.dtype) for p in pairs], 3 * nf, 0, copies)


def pair_swap(grads, name):
    def body(g_refs, o_refs, send_sems, recv_sems):
        x, y, c = _coords()
        cps = []
        for f, g_ref in enumerate(g_refs):
            rh = g_ref.shape[1] // 2
            cps.append(_remote(send_sems, recv_sems, f, g_ref.at[:, pl.ds((1 - c) * rh, rh), :], o_refs[f], (x, y, 1 - c)))
        for cp in cps:
            cp.start()
        for cp in cps:
            cp.wait()

    outs = [jax.ShapeDtypeStruct((g.shape[0], g.shape[1] // 2, g.shape[2]), g.dtype) for g in grads]
    return _exchange_call(body, name, grads, outs, len(grads))


def half_swap(halves, name):
    def body(h_refs, o_refs, send_sems, recv_sems):
        x, y, c = _coords()
        cps = [_remote(send_sems, recv_sems, f, h_ref, o_refs[f], (x, y, 1 - c)) for f, h_ref in enumerate(h_refs)]
        for cp in cps:
            cp.start()
        for cp in cps:
            cp.wait()

    return _exchange_call(body, name, halves, [jax.ShapeDtypeStruct(h.shape, h.dtype) for h in halves], len(halves))


def allsum_small(part, name):
    def body(p_ref, tot_ref, all_ref, send_sems, recv_sems):
        x, y, c = _coords()
        me, sibling = (x, y, c), (x, y, 1 - c)
        chips = _other_chips(x, y)

        def slot(px, py, pc):
            return all_ref.at[4 * px + 2 * py + pc]

        def copy(k, block, to, src=None):
            return pltpu.make_async_remote_copy(src_ref=slot(*block) if src is None else src, dst_ref=slot(*block),
                                                send_sem=send_sems.at[k], recv_sem=recv_sems.at[k], device_id=to, device_id_type=MESH)

        slot(*me)[...] = p_ref[...]
        first = [copy(0, me, sibling, src=p_ref)] + [copy(1 + j, me, (*chip, c), src=p_ref) for j, chip in enumerate(chips)]
        for cp in first:
            cp.start()
        passed = [copy(4 + j, (*chip, c), sibling) for j, chip in enumerate(chips)]
        for j, chip in enumerate(chips):
            copy(1 + j, (*chip, c), me).wait_recv()
            passed[j].start()
        copy(0, sibling, me).wait_recv()
        for j, chip in enumerate(chips):
            copy(4 + j, (*chip, 1 - c), me).wait_recv()
        for cp in first + passed:
            cp.wait_send()
        tot = all_ref[0]
        for d in range(1, 8):
            tot = tot + all_ref[d]
        tot_ref[...] = tot

    vm = pl.BlockSpec(memory_space=pltpu.VMEM)
    return pl.pallas_call(
        body, name=name, out_shape=jax.ShapeDtypeStruct(part.shape, F32), in_specs=[vm], out_specs=vm,
        scratch_shapes=[pltpu.VMEM((8,) + part.shape, F32), pltpu.SemaphoreType.DMA((7,)), pltpu.SemaphoreType.DMA((7,))],
        compiler_params=_params(),
    )(part)


QKVF_COLS = 772
QKVF_PAD = 896
FORWARD_CARRY = {0: (("fi0",), ("qkv1", "o1", "fo1"), ("fo0",)), 1: (("fi1", "qkv2", "o2"), (), ()),
                 2: (("fi2",), ("qkv3", "o3", "fo3"), ("fo2",)), 3: (("fi3",), (), ())}


def _tables_for(s):
    return _rot_tables(s)


def layer_families(layer):
    return (0, 1, layer // 2) if layer % 2 == 0 else (2, 3, layer // 2)


class GradientExchange:
    def __init__(self, family_layers):
        self.core = lax.axis_index("c").astype(jnp.int32).reshape(1)
        self.chip = (2 * lax.axis_index("x") + lax.axis_index("y")).astype(jnp.int32).reshape(1)
        self.family_layers = family_layers
        self.pairs, self.mine, self.pending = {}, {}, []

    def add(self, items, tag):
        gs = [g for _, _, g in items]
        sums = pair_sums(gs, pair_swap(gs, f"grad_pair_swap_{tag}"), self.core, f"grad_pair_sum_{tag}")
        for (fam, li, _), pair in zip(items, sums):
            self.pairs[(fam, li)] = pair
            self.pending.append((fam, li))

    def rider(self, only=None):
        keys = [k for k in self.pending if only is None or k in only]
        self.pending = [k for k in self.pending if k not in keys]
        return (scatter_rider([self.pairs[k] for k in keys]) if keys else None), keys

    def landed(self, keys, outs):
        batch = []
        for k, o in list(zip(keys, outs)) + [(None, None)]:
            if batch and (k is None or k[0] in [b[0][0] for b in batch]):
                ks = [b[0] for b in batch]
                pairs = [self.pairs[b] for b in ks]
                intos = [self.mine[fam] if fam in self.mine else jnp.zeros((self.family_layers[fam],) + p.shape[1:], F32)
                         for (fam, _), p in zip(ks, pairs)]
                sums = chip_sums(pairs, [b[1] for b in batch], self.chip, [li for _, li in ks], intos,
                                 "grad_chip_sum_" + "_".join(f"{f}{li}" for f, li in ks))
                self.mine.update({fam: t for (fam, _), t in zip(ks, sums)})
                batch = []
            if k is not None:
                batch.append((k, o))

    def finish(self, weights, moments1, moments2):
        last, keys = self.rider()
        if last is not None:
            self.landed(keys, last.run("grad_chip_scatter_last"))
        mine = [self.mine[fam] for fam in range(len(weights))]
        other = half_swap(mine, "grad_half_swap")
        return [adamw_family(w, m, v, gm, go, self.core, f"adamw_{f}")
                for f, (w, m, v, gm, go) in enumerate(zip(weights, moments1, moments2, mine, other))]


class KeepGradients:
    def __init__(self):
        self.grads = {}

    def add(self, items, tag):
        for fam, li, g in items:
            self.grads[(fam, li)] = g

    def rider(self, only=None):
        return None, []

    def landed(self, keys, outs):
        pass


def kernel(x, norm_mix, w_qkv_even, w_o_even, w_qkvf_odd, b_forget, w_o_odd, norm_ffn, w_ffn_in, w_ffn_out, norm_final, loss_target, m_norm_mix, m_w_qkv_even, m_w_o_even, m_w_qkvf_odd, m_b_forget, m_w_o_odd, m_norm_ffn, m_w_ffn_in, m_w_ffn_out, m_norm_final, v_norm_mix, v_w_qkv_even, v_w_o_even, v_w_qkvf_odd, v_b_forget, v_w_o_odd, v_norm_ffn, v_w_ffn_in, v_w_ffn_out, v_norm_final):
    w_shards = [w_qkv_even, w_o_even, w_qkvf_odd, w_o_odd, w_ffn_in, w_ffn_out]
    shards = [w.astype(BF16) for w in w_shards]
    tables = _tables_for(x.shape[1])
    bias_pad = jnp.pad(b_forget, ((0, 0), (0, LANES - N_HEADS)))

    mine = {}
    for layer in range(DEPTH):
        fam_qkv, fam_o, li = layer_families(layer)
        mine.update({f"qkv{layer}": shards[fam_qkv][li], f"o{layer}": shards[fam_o][li],
                     f"fi{layer}": shards[4][layer], f"fo{layer}": shards[5][layer]})
    fetch = lambda names: gather_rider([mine[n] for n in names])
    have = dict(zip(("qkv0", "o0"), fetch(("qkv0", "o0")).run("gather_first")))
    saved, cur = [], x[0]
    h1 = rmsnorm_fwd(cur, norm_mix[0:1], "l0_norm_mix")
    for layer in range(DEPTH):
        next_gain = norm_mix[layer + 1:layer + 2] if layer + 1 < DEPTH else None
        cur, h1, keep = forward_layer(layer, cur, h1, have, norm_ffn[layer:layer + 1], next_gain, tables,
                                      bias_pad[layer // 2:layer // 2 + 1], fetch, *FORWARD_CARRY[layer])
        saved.append(keep)

    dcur, g_final, loss_part = loss_head(cur, norm_final.reshape(1, D_MODEL), loss_target[0], "loss_head")

    exchange = GradientExchange([w.shape[0] for w in w_shards])
    g_mix, g_ffn, g_bias = [None] * DEPTH, [None] * DEPTH, [None] * (DEPTH // 2)
    for layer in reversed(range(DEPTH)):
        dcur, g_mix[layer], g_ffn[layer], g_b = backward_layer(layer, dcur, saved[layer], norm_mix[layer:layer + 1],
                                                               norm_ffn[layer:layer + 1], tables, bias_pad[layer // 2:layer // 2 + 1], exchange)
        if g_b is not None:
            g_bias[layer // 2] = g_b

    zero_row = jnp.zeros((1, D_MODEL), F32)
    pad16 = lambda v: jnp.pad(v, (0, D_MODEL - v.shape[0]))[None, :]
    small_rows = lambda mix, ffn, fin, bias, last: jnp.concatenate(
        [r.reshape(1, D_MODEL) for r in mix] + [r.reshape(1, D_MODEL) for r in ffn] + [fin.reshape(1, D_MODEL)]
        + [pad16(b) for b in bias] + [last] + [zero_row] * (SMALL_ROWS - 12), axis=0)
    loss_row = pad16(loss_part[0, :1])
    small_g = allsum_small(small_rows(g_mix, g_ffn, g_final, g_bias, loss_row), "allsum_small")
    loss = small_g[11, 0]
    small_g = small_g.at[11].set(0.0)
    sw = small_rows(list(norm_mix), list(norm_ffn), norm_final, list(b_forget), zero_row)
    sm = small_rows(list(m_norm_mix), list(m_norm_ffn), m_norm_final, list(m_b_forget), zero_row)
    sv = small_rows(list(v_norm_mix), list(v_norm_ffn), v_norm_final, list(v_b_forget), zero_row)
    sd, snm, snv = adamw(sw, small_g, sm, sv, "adamw_small")

    def small_out(a):
        return a[0:4], a[8, :], a[9:11, :N_HEADS], a[4:8]

    big = exchange.finish(w_shards, [m_w_qkv_even, m_w_o_even, m_w_qkvf_odd, m_w_o_odd, m_w_ffn_in, m_w_ffn_out],
                          [v_w_qkv_even, v_w_o_even, v_w_qkvf_odd, v_w_o_odd, v_w_ffn_in, v_w_ffn_out])

    def outputs(small, which):
        mix, fin, bias, ffn = small_out(small)
        qkv_e, o_e, qkvf, o_o, fi, fo = [big[f][which] for f in range(6)]
        return [mix, qkv_e, o_e, qkvf, bias, o_o, ffn, fi, fo, fin]

    return (loss, dcur[None], *outputs(small_g, 0), *outputs(sd, 1), *outputs(snm, 2), *outputs(snv, 3))


def _chip_tile(rows, cols, at):
    return pl.BlockSpec((None, rows, cols), at)


def forward_layer(layer, cur, h1, have, ffn_gain, next_gain, tables, bias_row, fetch=None, carry=(), side_carry=(), ffn_carry=()):
    n = f"l{layer}"
    s = cur.shape[0]
    w_qkv, w_o = have[f"qkv{layer}"], have[f"o{layer}"]
    rider = fetch(carry) if carry else None
    side_rider = fetch(side_carry) if side_carry else None
    keep = {"x": cur, "h1": h1, "w_o": w_o.reshape(D_ATTN, D_MODEL)}
    side = []
    if layer % 2 == 0:
        qkv = matmul(h1, w_qkv, "nn", BF16, n + "_qkv", 1024, 768, 1024, mnk=(s, 3 * D_ATTN, D_MODEL),
                     b_spec=_chip_tile(D_MODEL, 768, lambda i, j, kk: (j, 0, 0)))
        o_sb, st, rode = causal_fwd(qkv, 4, "sb", n + "_sb_fwd", rider=rider)
        o_dil, lse_dil, side = dilated_fwd(qkv, tables, n + "_dil_fwd", rider=side_rider)
        attn = jnp.concatenate([o_sb, o_dil], axis=1).astype(BF16)
        keep.update(o_dil=o_dil, lse_dil=lse_dil, w_qkv=w_qkv)
    else:
        natural = jnp.transpose(w_qkv, (1, 0, 2)).reshape(D_MODEL, N_CHIPS * QKVF_COLS)
        w_gate = jnp.pad(natural[:, 3 * D_ATTN:], ((0, 0), (0, LANES - N_HEADS)))
        qkv = matmul(h1, natural[:, :3 * D_ATTN], "nn", BF16, n + "_qkv", 1024, 768, 1024)
        fl = matmul(h1, w_gate, "nn", F32, n + "_fgate", 512, LANES, 1024)
        cum = forget_fwd(fl, bias_row, n + "_forget_fwd")
        f_heads = cum[:, :N_HEADS].T
        fq = jnp.broadcast_to(f_heads[:, :, None], (N_HEADS, s, LANES))
        fk = f_heads.reshape(N_HEADS // 2, 2, s)
        attn, st, rode = causal_fwd(qkv, 8, "fox", n + "_fox_fwd", fq=fq, fk=fk, rider=rider)
        attn = attn.astype(BF16)
        keep.update(fl=fl, fq=fq, fk=fk, w_qkv=jnp.concatenate([natural[:, :3 * D_ATTN], w_gate], axis=1))
    have.update(zip(carry, rode))
    have.update(zip(side_carry, side))
    w_fi = have[f"fi{layer}"]
    mid, h2 = matmul(attn, keep["w_o"], "nn", F32, n + "_attn_out", 512, 1024, 1024, res=cur, norm_gain=ffn_gain)
    gate, up, act, rode = ffn_in_swiglu(h2, w_fi, n + "_ffn_in", rider=fetch(ffn_carry) if ffn_carry else None)
    have.update(zip(ffn_carry, rode))
    w_fo = have[f"fo{layer}"].reshape(D_FF, D_MODEL)
    if next_gain is None:
        out, h_next = matmul(act, w_fo, "nn", F32, n + "_ffn_out", 512, 1024, D_FF, res=mid), None
    else:
        out, h_next = matmul(act, w_fo, "nn", F32, n + "_ffn_out", 512, 1024, D_FF, res=mid, norm_gain=next_gain)
    keep.update(qkv=qkv, st=st, attn=attn, mid=mid, h2=h2, gate=gate, up=up, act=act, w_fi=w_fi, w_fo=w_fo)
    return out, h_next, keep


def backward_layer(layer, dcur, kp, mix_gain, ffn_gain, tables, bias_row, exchange):
    n = f"l{layer}"
    s = dcur.shape[0]
    fam_qkv, fam_o, li = layer_families(layer)
    g_fo = matmul(kp["act"], dcur, "tn", BF16, n + "_d_w_ffn_out", 1408, 1024, s)
    dgu = swiglu_bwd(dcur, kp["w_fo"], kp["gate"], kp["up"], n + "_d_swiglu")
    g_fi = matmul(kp["h2"], dgu, "tn", BF16, n + "_d_w_ffn_in", 1024, 1408, 2048, mnk=(D_MODEL, 2 * D_FF, s),
                  o_spec=_chip_tile(D_MODEL, 1408, lambda i, j, kk: (j, 0, 0)), out_shape=(N_CHIPS, D_MODEL, 1408))
    dmid, g_ffn = dh_norm_bwd(dgu, kp["w_fi"], kp["mid"], ffn_gain, dcur, n + "_d_h2", 256)
    g_o = matmul(kp["attn"], dmid, "tn", BF16, n + "_d_w_o", 1024, 1024, s)
    dattn = matmul(dmid, kp["w_o"], "nt", F32, n + "_d_attn", 1024, 1024, 1024)
    exchange.add([(5, layer, g_fo.reshape(N_CHIPS, D_FF // N_CHIPS, D_MODEL)), (4, layer, g_fi),
                  (fam_o, li, g_o.reshape(N_CHIPS, D_ATTN // N_CHIPS, D_MODEL))], f"l{layer}_ffn")
    g_bias = None
    if layer % 2 == 0:
        rider, keys = exchange.rider(only=[(4, layer), (fam_o, li)])
        dq_a, dk_a, dv_a, rode = causal_bwd(kp["qkv"], dattn, kp["st"], 4, "sb", n + "_sb_bwd", rider=rider)
        exchange.landed(keys, rode)
        rider, keys = exchange.rider()
        dq_b, dk_b, dv_b, rode = dilated_bwd(kp["qkv"], tables, dattn, kp["o_dil"], kp["lse_dil"], 4, n + "_dil_bwd", rider=rider)
        dproj = jnp.concatenate([dq_a, dq_b, dk_a.astype(BF16), dk_b, dv_a.astype(BF16), dv_b], axis=1)
        g_qkv = matmul(kp["h1"], dproj, "tn", BF16, n + "_d_w_qkv", 1024, 768, 2048, mnk=(D_MODEL, 3 * D_ATTN, s),
                       o_spec=_chip_tile(D_MODEL, 768, lambda i, j, kk: (j, 0, 0)), out_shape=(N_CHIPS, D_MODEL, 768))
    else:
        rider, keys = exchange.rider()
        dq_f, dk_f, dv_f, dfk, rode = causal_bwd(kp["qkv"], dattn, kp["st"], 8, "fox", n + "_fox_bwd", fq=kp["fq"], fk=kp["fk"],
                                                 rider=rider)
        dcum = jnp.pad(dfk.reshape(N_HEADS, s).T, ((0, 0), (0, LANES - N_HEADS)))
        dfl, dbias = forget_bwd(kp["fl"], bias_row, dcum, n + "_forget_bwd")
        g_bias = dbias[0, :N_HEADS]
        dproj = jnp.concatenate([dq_f, dk_f.astype(BF16), dv_f.astype(BF16), dfl.astype(BF16)], axis=1)
        g_nat = matmul(kp["h1"], dproj, "tn", BF16, n + "_d_w_qkv", 1024, 640, 2048)
        g_qkv = g_nat[:, :N_CHIPS * QKVF_COLS].reshape(D_MODEL, N_CHIPS, QKVF_COLS)
        g_qkv = jnp.transpose(jnp.pad(g_qkv, ((0, 0), (0, 0), (0, QKVF_PAD - QKVF_COLS))), (1, 0, 2))
    exchange.landed(keys, rode)
    exchange.add([(fam_qkv, li, g_qkv)], f"l{layer}_qkv")
    dx, g_mix = dh_norm_bwd(dproj, kp["w_qkv"], kp["x"], mix_gain, dmid, n + "_d_h1", 512)
    return dx, g_mix, g_ffn, g_bias


def local_step(xs, target, norm_mix, norm_ffn, norm_final, b_forget, layer_weights):
    tables = _tables_for(xs.shape[0])
    bias_pad = jnp.pad(b_forget, ((0, 0), (0, LANES - N_HEADS)))
    saved, cur, have = [], xs, {}
    h1 = rmsnorm_fwd(cur, norm_mix[0:1], "l0_norm_mix")
    for layer in range(DEPTH):
        have.update(zip((f"qkv{layer}", f"o{layer}", f"fi{layer}", f"fo{layer}"), layer_weights[layer]))
        next_gain = norm_mix[layer + 1:layer + 2] if layer + 1 < DEPTH else None
        cur, h1, keep = forward_layer(layer, cur, h1, have, norm_ffn[layer:layer + 1], next_gain, tables,
                                      bias_pad[layer // 2:layer // 2 + 1])
        saved.append(keep)
    dcur, g_final, loss_part = loss_head(cur, norm_final.reshape(1, D_MODEL), target, "loss_head")
    keeper = KeepGradients()
    g_mix, g_ffn, g_bias = [None] * DEPTH, [None] * DEPTH, [None] * (DEPTH // 2)
    for layer in reversed(range(DEPTH)):
        dcur, g_mix[layer], g_ffn[layer], g_b = backward_layer(layer, dcur, saved[layer], norm_mix[layer:layer + 1],
                                                               norm_ffn[layer:layer + 1], tables, bias_pad[layer // 2:layer // 2 + 1], keeper)
        if g_b is not None:
            g_bias[layer // 2] = g_b
    return dcur, keeper.grads, (g_mix, g_ffn, g_final, g_bias), loss_part
```
